```python
import jax, jax.numpy as jnp
from jax import lax
import numpy as np

D_MODEL = 1024
BATCH = 8
SEQ = 4096
DEPTH = 1

POOL_WINDOWS = (2, 4, 8, 16)
N_POOL_GROUPS = len(POOL_WINDOWS)
POOL_WIDTH = D_MODEL
POOL_GROUP = POOL_WIDTH // N_POOL_GROUPS
HEAD_DIM = 64
N_Q_HEADS = D_MODEL // HEAD_DIM
N_KV_HEADS = 2
GQA_GROUP = N_Q_HEADS // N_KV_HEADS
WINDOW = 128
BLOCK = 128
ROPE_DIM = HEAD_DIM // 4
ROPE_THETA = 500000.0
Q_WIDTH = N_Q_HEADS * HEAD_DIM
KV_WIDTH = N_KV_HEADS * HEAD_DIM
D_FF = 2816
CONV_WIDTH = 3
EPS = 1e-6
IN_WIDTH = POOL_WIDTH + Q_WIDTH + 2 * KV_WIDTH + 2 * D_MODEL

kernel_name = "hybrid_pool_swa_sink_convglu_block"


def rmsnorm(x, g):
    xf = x.astype(jnp.float32)
    r = lax.rsqrt(jnp.mean(xf * xf, axis=-1, keepdims=True) + EPS)
    return (xf * r * g.astype(jnp.float32)).astype(x.dtype)


def partial_rope(x, positions):
    half = ROPE_DIM // 2
    inv_freq = ROPE_THETA ** (-jnp.arange(0, ROPE_DIM, 2, dtype=jnp.float32) / ROPE_DIM)
    ang = positions.astype(jnp.float32)[..., None] * inv_freq
    cos = jnp.cos(ang)[:, :, None, :]
    sin = jnp.sin(ang)[:, :, None, :]
    xf = x.astype(jnp.float32)
    x1, x2, xp = xf[..., :half], xf[..., half:ROPE_DIM], xf[..., ROPE_DIM:]
    out = jnp.concatenate([x1 * cos - x2 * sin, x2 * cos + x1 * sin, xp], axis=-1)
    return out.astype(x.dtype)


def pool_mixer(u, w_pool, pool_scale):
    B, S, _ = u.shape
    ug = u.reshape(B, S, N_POOL_GROUPS, POOL_GROUP).astype(jnp.float32)
    cs = jnp.cumsum(ug, axis=1)
    t = jnp.arange(S, dtype=jnp.float32)
    pooled = []
    for g, w in enumerate(POOL_WINDOWS):
        csg = cs[:, :, g]
        shifted = jnp.pad(csg, ((0, 0), (w, 0), (0, 0)))[:, :S]
        count = jnp.minimum(t + 1.0, float(w))[None, :, None]
        pooled.append((csg - shifted) / count)
    pooled = jnp.stack(pooled, axis=2) - ug
    mixed = jnp.einsum('bsgc,gcd->bsgd', pooled.astype(u.dtype), w_pool)
    return mixed.reshape(B, S, POOL_WIDTH) * pool_scale


def swa_sink_attention(q, k, v, sinks):
    B, S = q.shape[0], q.shape[1]
    nb = S // BLOCK
    qb = q.reshape(B, nb, BLOCK, N_KV_HEADS, GQA_GROUP, HEAD_DIM)

    def band(t):
        tb = t.reshape(B, nb, BLOCK, N_KV_HEADS, HEAD_DIM)
        prev = jnp.pad(tb, ((0, 0), (1, 0), (0, 0), (0, 0), (0, 0)))[:, :-1]
        return jnp.concatenate([prev, tb], axis=2)

    kb, vb = band(k), band(v)
    s = jnp.einsum('bnqhgd,bnkhd->bhgnqk', qb, kb,
                   preferred_element_type=jnp.float32)
    q_pos = jnp.arange(BLOCK)[:, None] + BLOCK
    k_pos = jnp.arange(2 * BLOCK)[None, :]
    rel_ok = (k_pos <= q_pos) & (q_pos - k_pos < WINDOW)
    blk_ok = (jnp.arange(nb)[:, None, None] > 0) | (k_pos[None] >= BLOCK)
    mask = rel_ok[None] & blk_ok
    s = jnp.where(mask, s, -jnp.inf)
    sink = sinks.astype(jnp.float32).reshape(1, N_KV_HEADS, GQA_GROUP, 1, 1, 1)
    m = jnp.maximum(jnp.max(s, axis=-1, keepdims=True), sink)
    p = jnp.exp(s - m)
    denom = jnp.sum(p, axis=-1, keepdims=True) + jnp.exp(sink - m)
    probs = (p / denom).astype(v.dtype)
    out = jnp.einsum('bhgnqk,bnkhd->bnqhgd', probs, vb)
    return out.reshape(B, S, Q_WIDTH)


def causal_depthwise_conv(u, w, b):
    S = u.shape[1]
    up = jnp.pad(u, ((0, 0), (CONV_WIDTH - 1, 0), (0, 0)))
    y = b
    for j in range(CONV_WIDTH):
        y = y + w[j] * up[:, j:j + S]
    return y


def _fwd_setup_inputs(seed: int = 0) -> dict:
    key = jax.random.key(seed)
    ks = jax.random.split(key, 18)
    f32 = jnp.float32
    nrm = lambda k, shape, s: jax.random.normal(k, shape, f32) * s
    x = jax.random.normal(ks[0], (BATCH, SEQ, D_MODEL), f32)
    offsets = jax.random.randint(ks[1], (BATCH, 1), 0, 1024, dtype=jnp.int32)
    positions = offsets + jnp.arange(SEQ, dtype=jnp.int32)[None, :]
    return {
        "x": x,
        "positions": positions,
        "attn_norm": 1.0 + nrm(ks[2], (DEPTH, D_MODEL), 0.05),
        "w_in": nrm(ks[3], (DEPTH, D_MODEL, IN_WIDTH), D_MODEL ** -0.5),
        "b_gate": nrm(ks[4], (DEPTH, 2 * D_MODEL), 0.1),
        "w_pool": nrm(ks[5], (DEPTH, N_POOL_GROUPS, POOL_GROUP, POOL_GROUP), POOL_GROUP ** -0.5),
        "pool_scale": 1.0 + nrm(ks[6], (DEPTH, POOL_WIDTH), 0.1),
        "q_norm": 1.0 + nrm(ks[7], (DEPTH, HEAD_DIM), 0.05),
        "k_norm": 1.0 + nrm(ks[8], (DEPTH, HEAD_DIM), 0.05),
        "sinks": nrm(ks[9], (DEPTH, N_Q_HEADS), 0.5),
        "w_out": nrm(ks[10], (DEPTH, D_MODEL, D_MODEL), D_MODEL ** -0.5),
        "ffn_norm": 1.0 + nrm(ks[11], (DEPTH, D_MODEL), 0.05),
        "w_up": nrm(ks[12], (DEPTH, D_MODEL, 2 * D_FF), D_MODEL ** -0.5),
        "conv_w": nrm(ks[13], (DEPTH, CONV_WIDTH, 2 * D_FF), CONV_WIDTH ** -0.5),
        "conv_b": nrm(ks[14], (DEPTH, 2 * D_FF), 0.02),
        "w_down": nrm(ks[15], (DEPTH, D_FF, D_MODEL), D_FF ** -0.5),
    }


def _fwd_reference(x, positions, attn_norm, w_in, b_gate, w_pool, pool_scale, q_norm, k_norm,
              sinks, w_out, ffn_norm, w_up, conv_w, conv_b, w_down):
    B, S, _ = x.shape
    scale = HEAD_DIM ** -0.5
    for l in range(DEPTH):
        h = rmsnorm(x, attn_norm[l])
        z = h @ w_in[l]
        o1 = POOL_WIDTH
        o2 = o1 + Q_WIDTH
        o3 = o2 + KV_WIDTH
        o4 = o3 + KV_WIDTH
        u_pool = z[..., :o1]
        q = z[..., o1:o2].reshape(B, S, N_Q_HEADS, HEAD_DIM)
        k = z[..., o2:o3].reshape(B, S, N_KV_HEADS, HEAD_DIM)
        v = z[..., o3:o4].reshape(B, S, N_KV_HEADS, HEAD_DIM)
        gates = jax.nn.sigmoid((z[..., o4:] + b_gate[l]).astype(jnp.float32)).astype(x.dtype)
        g_pool, g_attn = gates[..., :D_MODEL], gates[..., D_MODEL:]

        a = pool_mixer(u_pool, w_pool[l], pool_scale[l])

        q = partial_rope(rmsnorm(q, q_norm[l]), positions) * scale
        k = partial_rope(rmsnorm(k, k_norm[l]), positions)
        b = swa_sink_attention(q, k, v, sinks[l])

        x = x + (g_pool * a + g_attn * b) @ w_out[l]

        h = rmsnorm(x, ffn_norm[l])
        up = causal_depthwise_conv(h @ w_up[l], conv_w[l], conv_b[l])
        gate, val = up[..., :D_FF], up[..., D_FF:]
        x = x + (jax.nn.silu(gate) * val) @ w_down[l]
    return x


import jax as _jax
import jax.numpy as _jnp

TWIN_FORMAT = 'train_step'
FWD_PARAMS = ['x', 'positions', 'attn_norm', 'w_in', 'b_gate', 'w_pool', 'pool_scale', 'q_norm', 'k_norm', 'sinks', 'w_out', 'ffn_norm', 'w_up', 'conv_w', 'conv_b', 'w_down']
TWIN_WEIGHTS = ['attn_norm', 'w_in', 'b_gate', 'w_pool', 'pool_scale', 'q_norm', 'k_norm', 'sinks', 'w_out', 'ffn_norm', 'w_up', 'conv_w', 'conv_b', 'w_down']
TWIN_DIFF_INPUT = 'x'
TWIN_INPUTS = ['x', 'positions', 'attn_norm', 'w_in', 'b_gate', 'w_pool', 'pool_scale', 'q_norm', 'k_norm', 'sinks', 'w_out', 'ffn_norm', 'w_up', 'conv_w', 'conv_b', 'w_down', 'loss_target', 'm_attn_norm', 'm_w_in', 'm_b_gate', 'm_w_pool', 'm_pool_scale', 'm_q_norm', 'm_k_norm', 'm_sinks', 'm_w_out', 'm_ffn_norm', 'm_w_up', 'm_conv_w', 'm_conv_b', 'm_w_down', 'v_attn_norm', 'v_w_in', 'v_b_gate', 'v_w_pool', 'v_pool_scale', 'v_q_norm', 'v_k_norm', 'v_sinks', 'v_w_out', 'v_ffn_norm', 'v_w_up', 'v_conv_w', 'v_conv_b', 'v_w_down']
TWIN_OUTPUTS = ['loss', 'grad_x', 'grad_attn_norm', 'grad_w_in', 'grad_b_gate', 'grad_w_pool', 'grad_pool_scale', 'grad_q_norm', 'grad_k_norm', 'grad_sinks', 'grad_w_out', 'grad_ffn_norm', 'grad_w_up', 'grad_conv_w', 'grad_conv_b', 'grad_w_down', 'delta_attn_norm', 'delta_w_in', 'delta_b_gate', 'delta_w_pool', 'delta_pool_scale', 'delta_q_norm', 'delta_k_norm', 'delta_sinks', 'delta_w_out', 'delta_ffn_norm', 'delta_w_up', 'delta_conv_w', 'delta_conv_b', 'delta_w_down', 'new_m_attn_norm', 'new_m_w_in', 'new_m_b_gate', 'new_m_w_pool', 'new_m_pool_scale', 'new_m_q_norm', 'new_m_k_norm', 'new_m_sinks', 'new_m_w_out', 'new_m_ffn_norm', 'new_m_w_up', 'new_m_conv_w', 'new_m_conv_b', 'new_m_w_down', 'new_v_attn_norm', 'new_v_w_in', 'new_v_b_gate', 'new_v_w_pool', 'new_v_pool_scale', 'new_v_q_norm', 'new_v_k_norm', 'new_v_sinks', 'new_v_w_out', 'new_v_ffn_norm', 'new_v_w_up', 'new_v_conv_w', 'new_v_conv_b', 'new_v_w_down']
TWIN_LEAF_KINDS = {'loss': 'loss', 'grad_x': 'grad_x', 'grad_attn_norm': 'grad_w', 'grad_w_in': 'grad_w', 'grad_b_gate': 'grad_w', 'grad_w_pool': 'grad_w', 'grad_pool_scale': 'grad_w', 'grad_q_norm': 'grad_w', 'grad_k_norm': 'grad_w', 'grad_sinks': 'grad_w', 'grad_w_out': 'grad_w', 'grad_ffn_norm': 'grad_w', 'grad_w_up': 'grad_w', 'grad_conv_w': 'grad_w', 'grad_conv_b': 'grad_w', 'grad_w_down': 'grad_w', 'delta_attn_norm': 'delta_w', 'delta_w_in': 'delta_w', 'delta_b_gate': 'delta_w', 'delta_w_pool': 'delta_w', 'delta_pool_scale': 'delta_w', 'delta_q_norm': 'delta_w', 'delta_k_norm': 'delta_w', 'delta_sinks': 'delta_w', 'delta_w_out': 'delta_w', 'delta_ffn_norm': 'delta_w', 'delta_w_up': 'delta_w', 'delta_conv_w': 'delta_w', 'delta_conv_b': 'delta_w', 'delta_w_down': 'delta_w', 'new_m_attn_norm': 'new_m', 'new_m_w_in': 'new_m', 'new_m_b_gate': 'new_m', 'new_m_w_pool': 'new_m', 'new_m_pool_scale': 'new_m', 'new_m_q_norm': 'new_m', 'new_m_k_norm': 'new_m', 'new_m_sinks': 'new_m', 'new_m_w_out': 'new_m', 'new_m_ffn_norm': 'new_m', 'new_m_w_up': 'new_m', 'new_m_conv_w': 'new_m', 'new_m_conv_b': 'new_m', 'new_m_w_down': 'new_m', 'new_v_attn_norm': 'new_v', 'new_v_w_in': 'new_v', 'new_v_b_gate': 'new_v', 'new_v_w_pool': 'new_v', 'new_v_pool_scale': 'new_v', 'new_v_q_norm': 'new_v', 'new_v_k_norm': 'new_v', 'new_v_sinks': 'new_v', 'new_v_w_out': 'new_v', 'new_v_ffn_norm': 'new_v', 'new_v_w_up': 'new_v', 'new_v_conv_w': 'new_v', 'new_v_conv_b': 'new_v', 'new_v_w_down': 'new_v'}


def _forward(args):
    return _fwd_reference(*[args[k] for k in FWD_PARAMS])


def _output_shape():
    out = _jax.eval_shape(lambda: _forward(_fwd_setup_inputs(0)))
    return out.shape, out.dtype

N_MICROBATCH = 1
ADAM_LR = 0.001
ADAM_B1 = 0.9
ADAM_B2 = 0.999
ADAM_EPS = 1e-08
ADAM_WD = 0.01
ADAM_STEP = 10
PER_EXAMPLE_BATCH_AXIS = {'x': 0, 'positions': 0, 'loss_target': 0}
SHARED_INPUTS = []
_WEIGHT_DTYPES = {'attn_norm': _jnp.float32, 'w_in': _jnp.float32, 'b_gate': _jnp.float32, 'w_pool': _jnp.float32, 'pool_scale': _jnp.float32, 'q_norm': _jnp.float32, 'k_norm': _jnp.float32, 'sinks': _jnp.float32, 'w_out': _jnp.float32, 'ffn_norm': _jnp.float32, 'w_up': _jnp.float32, 'conv_w': _jnp.float32, 'conv_b': _jnp.float32, 'w_down': _jnp.float32}
MOMENT_SCALE = {'attn_norm': 7.896903e+00, 'w_in': 2.063885e-01, 'b_gate': 1.859991e+00, 'w_pool': 5.490134e-01, 'pool_scale': 7.340738e+00, 'q_norm': 1.585774e+00, 'k_norm': 1.567460e+00, 'sinks': 2.333430e-01, 'w_out': 4.316210e-01, 'ffn_norm': 2.712430e+01, 'w_up': 2.527921e-01, 'conv_w': 3.803112e+00, 'conv_b': 3.303387e+00, 'w_down': 3.336528e-01}


def _to_microbatches(a, axis):
    t = _jnp.moveaxis(a, axis, 0)
    t = t.reshape((N_MICROBATCH, t.shape[0] // N_MICROBATCH) + t.shape[1:])
    return _jnp.moveaxis(t, 1, axis + 1)


def setup_inputs(seed: int = 0) -> dict:
    inp = _fwd_setup_inputs(seed)
    key = _jax.random.fold_in(_jax.random.key(seed), 7919)
    shape, _ = _output_shape()
    out = dict(inp)
    out["loss_target"] = _jax.random.normal(_jax.random.fold_in(key, 0), shape, _jnp.float32)
    for i, name in enumerate(TWIN_WEIGHTS):
        w = inp[name].astype(_jnp.float32)
        if MOMENT_SCALE is None:
            s = _jnp.sqrt(_jnp.mean(_jnp.square(w)) + 1e-30)
        else:
            s = MOMENT_SCALE[name]
        km, kv = _jax.random.split(_jax.random.fold_in(key, i + 1))
        out[name] = w
        out["m_" + name] = s * _jax.random.normal(km, w.shape, _jnp.float32)
        out["v_" + name] = (s * s) * _jax.random.uniform(kv, w.shape, _jnp.float32, 0.5, 1.5)
    if N_MICROBATCH > 1:
        for name, axis in PER_EXAMPLE_BATCH_AXIS.items():
            out[name] = _to_microbatches(out[name], axis)
    return {'x': out['x'], 'positions': out['positions'], 'attn_norm': out['attn_norm'], 'w_in': out['w_in'], 'b_gate': out['b_gate'], 'w_pool': out['w_pool'], 'pool_scale': out['pool_scale'], 'q_norm': out['q_norm'], 'k_norm': out['k_norm'], 'sinks': out['sinks'], 'w_out': out['w_out'], 'ffn_norm': out['ffn_norm'], 'w_up': out['w_up'], 'conv_w': out['conv_w'], 'conv_b': out['conv_b'], 'w_down': out['w_down'], 'loss_target': out['loss_target'], 'm_attn_norm': out['m_attn_norm'], 'm_w_in': out['m_w_in'], 'm_b_gate': out['m_b_gate'], 'm_w_pool': out['m_w_pool'], 'm_pool_scale': out['m_pool_scale'], 'm_q_norm': out['m_q_norm'], 'm_k_norm': out['m_k_norm'], 'm_sinks': out['m_sinks'], 'm_w_out': out['m_w_out'], 'm_ffn_norm': out['m_ffn_norm'], 'm_w_up': out['m_w_up'], 'm_conv_w': out['m_conv_w'], 'm_conv_b': out['m_conv_b'], 'm_w_down': out['m_w_down'], 'v_attn_norm': out['v_attn_norm'], 'v_w_in': out['v_w_in'], 'v_b_gate': out['v_b_gate'], 'v_w_pool': out['v_w_pool'], 'v_pool_scale': out['v_pool_scale'], 'v_q_norm': out['v_q_norm'], 'v_k_norm': out['v_k_norm'], 'v_sinks': out['v_sinks'], 'v_w_out': out['v_w_out'], 'v_ffn_norm': out['v_ffn_norm'], 'v_w_up': out['v_w_up'], 'v_conv_w': out['v_conv_w'], 'v_conv_b': out['v_conv_b'], 'v_w_down': out['v_w_down']}


def _loss(weights, diff, rest, loss_target):
    with _jax.named_scope("forward"):
        args = {**rest, TWIN_DIFF_INPUT: diff, **{k: w.astype(_WEIGHT_DTYPES[k]) for k, w in weights.items()}}
        y = _forward(args)
    with _jax.named_scope("loss_head"):
        err = _jnp.square(y.astype(_jnp.float32) - loss_target)
        return 0.5 * _jnp.sum(_jnp.mean(err, axis=-1)) if err.ndim else 0.5 * err


def _adamw(w, g, m, v):
    m = ADAM_B1 * m + (1.0 - ADAM_B1) * g
    v = ADAM_B2 * v + (1.0 - ADAM_B2) * _jnp.square(g)
    m_hat = m / (1.0 - ADAM_B1 ** ADAM_STEP)
    v_hat = v / (1.0 - ADAM_B2 ** ADAM_STEP)
    delta = -ADAM_LR * (m_hat / (_jnp.sqrt(v_hat) + ADAM_EPS) + ADAM_WD * w)
    return delta, m, v


def reference(x, positions, attn_norm, w_in, b_gate, w_pool, pool_scale, q_norm, k_norm, sinks, w_out, ffn_norm, w_up, conv_w, conv_b, w_down, loss_target, m_attn_norm, m_w_in, m_b_gate, m_w_pool, m_pool_scale, m_q_norm, m_k_norm, m_sinks, m_w_out, m_ffn_norm, m_w_up, m_conv_w, m_conv_b, m_w_down, v_attn_norm, v_w_in, v_b_gate, v_w_pool, v_pool_scale, v_q_norm, v_k_norm, v_sinks, v_w_out, v_ffn_norm, v_w_up, v_conv_w, v_conv_b, v_w_down):
    given = dict(x=x, positions=positions, attn_norm=attn_norm, w_in=w_in, b_gate=b_gate, w_pool=w_pool, pool_scale=pool_scale, q_norm=q_norm, k_norm=k_norm, sinks=sinks, w_out=w_out, ffn_norm=ffn_norm, w_up=w_up, conv_w=conv_w, conv_b=conv_b, w_down=w_down, loss_target=loss_target, m_attn_norm=m_attn_norm, m_w_in=m_w_in, m_b_gate=m_b_gate, m_w_pool=m_w_pool, m_pool_scale=m_pool_scale, m_q_norm=m_q_norm, m_k_norm=m_k_norm, m_sinks=m_sinks, m_w_out=m_w_out, m_ffn_norm=m_ffn_norm, m_w_up=m_w_up, m_conv_w=m_conv_w, m_conv_b=m_conv_b, m_w_down=m_w_down, v_attn_norm=v_attn_norm, v_w_in=v_w_in, v_b_gate=v_b_gate, v_w_pool=v_w_pool, v_pool_scale=v_pool_scale, v_q_norm=v_q_norm, v_k_norm=v_k_norm, v_sinks=v_sinks, v_w_out=v_w_out, v_ffn_norm=v_ffn_norm, v_w_up=v_w_up, v_conv_w=v_conv_w, v_conv_b=v_conv_b, v_w_down=v_w_down)
    weights = {n: given[n] for n in TWIN_WEIGHTS}
    shared = {n: given[n] for n in SHARED_INPUTS}
    per_example = {n: given[n] for n in ['x', 'positions']}
    grad_fn = _jax.value_and_grad(_loss, argnums=(0, 1))

    def one_microbatch(ex, loss_target):
        ex = dict(ex)
        diff = ex.pop(TWIN_DIFF_INPUT)
        return grad_fn(weights, diff, {**shared, **ex}, loss_target)

    if N_MICROBATCH == 1:
        loss, (grad_w, grad_x) = one_microbatch(per_example, given["loss_target"])
    else:
        def body(carry, xs):
            loss_sum, grad_sum = carry
            l_k, (gw_k, gx_k) = one_microbatch(xs[0], xs[1])
            with _jax.named_scope("update"):
                return (loss_sum + l_k, _jax.tree.map(_jnp.add, grad_sum, gw_k)), gx_k

        init = (_jnp.zeros((), _jnp.float32), _jax.tree.map(_jnp.zeros_like, weights))
        (loss, grad_w), grad_x = _jax.lax.scan(body, init, (per_example, given["loss_target"]))
    with _jax.named_scope("update"):
        delta_w, new_m, new_v = {}, {}, {}
        for n in TWIN_WEIGHTS:
            delta_w[n], new_m[n], new_v[n] = _adamw(weights[n], grad_w[n], given["m_" + n], given["v_" + n])
    return (loss, grad_x, *[grad_w[n] for n in TWIN_WEIGHTS], *[delta_w[n] for n in TWIN_WEIGHTS],
            *[new_m[n] for n in TWIN_WEIGHTS], *[new_v[n] for n in TWIN_WEIGHTS])
```

```python
import functools

import jax
import jax.numpy as jnp
from jax import lax
from jax.experimental import pallas as pl
from jax.experimental.pallas import tpu as pltpu

F32 = jnp.float32
MM = jnp.bfloat16
WIRE = jnp.bfloat16

D_MODEL = 1024
D_FF = 2816
HEAD_DIM = 64
N_Q_HEADS = 16
N_KV_HEADS = 2
GQA_GROUP = 8
BLOCK = 128
ROPE_DIM = 16
ROPE_THETA = 500000.0
POOL_WINDOWS = (2, 4, 8, 16)
POOL_GROUP = 256
POOL_HALO = 16
CONV_HALO = 8
EPS = 1e-6
NEG = -1e30
O_U, O_Q, O_KV, O_G, IN_WIDTH = 0, 1024, 2048, 2304, 4352
FF_CHUNK = 1408

ADAM_LR, ADAM_B1, ADAM_B2, ADAM_EPS, ADAM_WD, ADAM_STEP = 0.001, 0.9, 0.999, 1e-08, 0.01, 10

N_DEV = 8
LANES = 128
VMEM_LIMIT_BYTES = 56 * 1024 * 1024
MESH = pl.DeviceIdType.MESH


def _params(*sem):
    return pltpu.CompilerParams(dimension_semantics=sem, vmem_limit_bytes=VMEM_LIMIT_BYTES)


def _resident(shape):
    nd = len(shape)
    return pl.BlockSpec(shape, lambda *_: (0,) * nd, pipeline_mode=pl.Buffered(1))


def _rows(tm, width):
    return pl.BlockSpec((tm, width), lambda i: (i, 0))


def _mm(a, b):
    return jnp.dot(a.astype(MM), b.astype(MM), preferred_element_type=F32)


def _mm_nt(a, b):
    return lax.dot_general(a.astype(MM), b.astype(MM), (((1,), (1,)), ((), ())), preferred_element_type=F32)


def _mm_tn(a, b):
    return lax.dot_general(a.astype(MM), b.astype(MM), (((0,), (0,)), ((), ())), preferred_element_type=F32)


def _rmsnorm_fwd(x, g):
    r = lax.rsqrt(jnp.mean(x * x, axis=-1, keepdims=True) + EPS)
    return x * r * g, r


def _rmsnorm_bwd(x, r, g, dy):
    xn = x * r
    dxn = dy * g
    dx = r * (dxn - xn * jnp.mean(dxn * xn, axis=-1, keepdims=True))
    return dx, dy * xn


def _group_sum64(v, bd):
    hi = v.astype(MM)
    lo = (v - hi.astype(F32)).astype(MM)
    outs = []
    for t in range(v.shape[1] // LANES):
        sl = slice(LANES * t, LANES * (t + 1))
        outs.append(jnp.dot(hi[:, sl], bd, preferred_element_type=F32)
                    + jnp.dot(lo[:, sl], bd, preferred_element_type=F32))
    return outs[0] if len(outs) == 1 else jnp.concatenate(outs, axis=1)


def _head_norm_fwd(x, g, bd):
    r = lax.rsqrt(_group_sum64(x * x, bd) * (1.0 / HEAD_DIM) + EPS)
    return x * r * g, r


def _head_norm_bwd(x, r, g, dy, bd):
    xn = x * r
    dxn = dy * g
    dx = r * (dxn - xn * (_group_sum64(dxn * xn, bd) * (1.0 / HEAD_DIM)))
    return dx, dy * xn


def _rope(x, c, s1, s2):
    w = x.shape[1]
    return x * c + pltpu.roll(x, w - ROPE_DIM // 2, 1) * s1 + pltpu.roll(x, ROPE_DIM // 2, 1) * s2


def _rope_bwd(dy, c, s1, s2):
    w = dy.shape[1]
    return dy * c + pltpu.roll(dy * s1, ROPE_DIM // 2, 1) + pltpu.roll(dy * s2, w - ROPE_DIM // 2, 1)


def _tile_lanes(t, reps):
    return t if reps == 1 else jnp.concatenate([t] * reps, axis=1)


def _rope_tables(pos_col, invf):
    s = pos_col.shape[0]
    tm = min(s, 1024)

    def body(pos_ref, invf_ref, c_ref, s1_ref, s2_ref):
        ang = pos_ref[...].astype(F32) * invf_ref[...]
        lane = lax.broadcasted_iota(jnp.int32, ang.shape, 1) % HEAD_DIM
        sn = jnp.sin(ang)
        c_ref[...] = jnp.cos(ang)
        s1_ref[...] = jnp.where(lane < ROPE_DIM // 2, -sn, 0.0)
        s2_ref[...] = jnp.where((lane >= ROPE_DIM // 2) & (lane < ROPE_DIM), sn, 0.0)

    out = jax.ShapeDtypeStruct((s, LANES), F32)
    return pl.pallas_call(
        body, name="rope_tables", grid=(s // tm,), out_shape=(out, out, out),
        in_specs=[_rows(tm, 1), _resident((1, LANES))],
        out_specs=(_rows(tm, LANES),) * 3, compiler_params=_params("parallel"),
    )(pos_col, invf)


def _inproj_fwd(x, attn_norm, win_t, b_gate, tm):
    s = x.shape[0]

    def body(x_ref, gn_ref, w_ref, bg_ref, h_ref, u_ref, q_ref, kv_ref, g_ref):
        h, _ = _rmsnorm_fwd(x_ref[...], gn_ref[...])
        h = h.astype(MM)
        h_ref[...] = h
        u_ref[...] = _mm_nt(h, w_ref[O_U:O_Q, :])
        q_ref[...] = _mm_nt(h, w_ref[O_Q:O_KV, :])
        kv_ref[...] = _mm_nt(h, w_ref[O_KV:O_G, :])
        g_ref[...] = jax.nn.sigmoid(_mm_nt(h, w_ref[O_G:IN_WIDTH, :]) + bg_ref[...])

    sd = jax.ShapeDtypeStruct
    return pl.pallas_call(
        body, name="inproj_fwd", grid=(s // tm,),
        out_shape=(sd((s, D_MODEL), MM), sd((s, 1024), F32), sd((s, 1024), F32), sd((s, 256), F32),
                   sd((s, 2048), F32)),
        in_specs=[_rows(tm, D_MODEL), _resident((1, D_MODEL)), _resident((IN_WIDTH, D_MODEL)), _resident((1, 2048))],
        out_specs=(_rows(tm, D_MODEL), _rows(tm, 1024), _rows(tm, 1024), _rows(tm, 256), _rows(tm, 2048)),
        compiler_params=_params("parallel"),
    )(x, attn_norm, win_t, b_gate)


def _pooled(ext_ref, tm, row0):
    t = (row0 + lax.broadcasted_iota(jnp.int32, (tm, 1), 0)).astype(F32)
    out = []
    for gi, w in enumerate(POOL_WINDOWS):
        cols = slice(gi * POOL_GROUP, (gi + 1) * POOL_GROUP)
        acc = ext_ref[pl.ds(POOL_HALO, tm), cols]
        for k in range(1, w):
            acc = acc + ext_ref[pl.ds(POOL_HALO - k, tm), cols]
        cnt = jnp.minimum(t + 1.0, float(w))
        out.append(acc / cnt - ext_ref[pl.ds(POOL_HALO, tm), cols])
    return out


def _pool_fwd(u, wpool, pool_scale, tm):
    s = u.shape[0]
    hb = tm // POOL_HALO

    def body(u_ref, halo_ref, wp_ref, ps_ref, a_ref, ext_ref):
        i = pl.program_id(0)
        ext_ref[pl.ds(0, POOL_HALO), :] = jnp.where(i > 0, halo_ref[...], 0.0)
        ext_ref[pl.ds(POOL_HALO, tm), :] = u_ref[...]
        pooled = _pooled(ext_ref, tm, i * tm)
        for gi in range(4):
            cols = slice(gi * POOL_GROUP, (gi + 1) * POOL_GROUP)
            a_ref[:, cols] = _mm(pooled[gi], wp_ref[gi]) * ps_ref[:, cols]

    return pl.pallas_call(
        body, name="pool_fwd", grid=(s // tm,), out_shape=jax.ShapeDtypeStruct((s, 1024), F32),
        in_specs=[_rows(tm, 1024), pl.BlockSpec((POOL_HALO, 1024), lambda i: (jnp.maximum(i * hb - 1, 0), 0)),
                  _resident((4, POOL_GROUP, POOL_GROUP)), _resident((1, 1024))],
        out_specs=_rows(tm, 1024), scratch_shapes=[pltpu.VMEM((POOL_HALO + tm, 1024), F32)],
        compiler_params=_params("parallel"),
    )(u, u, wpool, pool_scale)


def _attn_mask(n):
    shape = (GQA_GROUP * BLOCK, 2 * BLOCK)
    qi = lax.broadcasted_iota(jnp.int32, shape, 0) % BLOCK
    kj = lax.broadcasted_iota(jnp.int32, shape, 1)
    return (kj > qi) & (kj <= qi + BLOCK) & ((n > 0) | (kj >= BLOCK))


def _stack_heads(x, hk):
    base = hk * GQA_GROUP * HEAD_DIM
    return jnp.concatenate([x[:, base + j * HEAD_DIM: base + (j + 1) * HEAD_DIM] for j in range(GQA_GROUP)], axis=0)


def _band(prev, cur, hk):
    sl = slice(hk * HEAD_DIM, (hk + 1) * HEAD_DIM)
    return jnp.concatenate([prev[:, sl], cur[:, sl]], axis=0)


def _softmax_sink(s, sink):
    m = jnp.maximum(jnp.max(s, axis=-1, keepdims=True), sink)
    p = jnp.exp(s - m)
    es = jnp.exp(sink - m)
    den = jnp.sum(p, axis=-1, keepdims=True) + es
    return p / den, es / den


def _attn_fwd(q, kv, tabs, q_norm_t, k_norm_t, sink_col, bd):
    s = q.shape[0]
    nb = s // BLOCK
    scale = HEAD_DIM ** -0.5
    cur = lambda n: (n, 0)
    prv = lambda n: (jnp.maximum(n - 1, 0), 0)

    def body(q_ref, kvc_ref, kvp_ref, c_ref, s1_ref, s2_ref, cp_ref, s1p_ref, s2p_ref, qn_ref, kn_ref, sink_ref,
             bd_ref, o_ref):
        n = pl.program_id(0)
        bdm = bd_ref[...]
        c, s1, s2 = c_ref[...], s1_ref[...], s2_ref[...]
        qh, _ = _head_norm_fwd(q_ref[...], qn_ref[...], bdm)
        qr = _rope(qh, _tile_lanes(c, 8), _tile_lanes(s1, 8), _tile_lanes(s2, 8)) * scale
        kc, _ = _head_norm_fwd(kvc_ref[:, 0:128], kn_ref[...], bdm)
        kc = _rope(kc, c, s1, s2)
        kp, _ = _head_norm_fwd(kvp_ref[:, 0:128], kn_ref[...], bdm)
        kp = _rope(kp, cp_ref[...], s1p_ref[...], s2p_ref[...])
        vc, vp = kvc_ref[:, 128:256], kvp_ref[:, 128:256]
        mask = _attn_mask(n)
        for hk in range(N_KV_HEADS):
            sc = jnp.where(mask, _mm_nt(_stack_heads(qr, hk), _band(kp, kc, hk)), NEG)
            pr, _ = _softmax_sink(sc, sink_ref[hk])
            o = _mm(pr, _band(vp, vc, hk))
            for j in range(GQA_GROUP):
                col = (hk * GQA_GROUP + j) * HEAD_DIM
                o_ref[:, col:col + HEAD_DIM] = o[j * BLOCK:(j + 1) * BLOCK, :]

    tab = lambda im: pl.BlockSpec((BLOCK, LANES), im)
    return pl.pallas_call(
        body, name="attn_fwd", grid=(nb,), out_shape=jax.ShapeDtypeStruct((s, 1024), F32),
        in_specs=[pl.BlockSpec((BLOCK, 1024), cur), pl.BlockSpec((BLOCK, 256), cur), pl.BlockSpec((BLOCK, 256), prv),
                  tab(cur), tab(cur), tab(cur), tab(prv), tab(prv), tab(prv),
                  _resident((1, 1024)), _resident((1, 128)), _resident((N_KV_HEADS, GQA_GROUP * BLOCK, 1)),
                  _resident((LANES, LANES))],
        out_specs=pl.BlockSpec((BLOCK, 1024), cur), compiler_params=_params("parallel"),
    )(q, kv, kv, *tabs, *tabs, q_norm_t, k_norm_t, sink_col, bd)


def _mix_out_fwd(x, g, a, b, wout, tm):
    s = x.shape[0]

    def body(x_ref, g_ref, a_ref, b_ref, w_ref, x1_ref, mix_ref):
        mix = (g_ref[:, 0:1024] * a_ref[...] + g_ref[:, 1024:2048] * b_ref[...]).astype(MM)
        mix_ref[...] = mix
        x1_ref[...] = x_ref[...] + _mm(mix, w_ref[...])

    return pl.pallas_call(
        body, name="mix_out_fwd", grid=(s // tm,),
        out_shape=(jax.ShapeDtypeStruct((s, D_MODEL), F32), jax.ShapeDtypeStruct((s, D_MODEL), MM)),
        in_specs=[_rows(tm, 1024), _rows(tm, 2048), _rows(tm, 1024), _rows(tm, 1024), _resident((1024, 1024))],
        out_specs=(_rows(tm, 1024), _rows(tm, 1024)), compiler_params=_params("parallel"),
    )(x, g, a, b, wout)


def _conv_glu(ext_ref, cw_ref, cb_ref, tm, c):
    out = []
    for base in (c * FF_CHUNK, D_FF + c * FF_CHUNK):
        cols = slice(base, base + FF_CHUNK)
        y = cb_ref[:, cols] + cw_ref[0:1, cols] * ext_ref[pl.ds(CONV_HALO - 2, tm), cols]
        y = y + cw_ref[1:2, cols] * ext_ref[pl.ds(CONV_HALO - 1, tm), cols]
        y = y + cw_ref[2:3, cols] * ext_ref[pl.ds(CONV_HALO, tm), cols]
        out.append(y)
    return out


def _ffn_fwd(x1, ffn_norm, wup_t, conv_w, conv_b, wdown, target, tm):
    s = x1.shape[0]
    inv_d = 1.0 / D_MODEL

    def body(x1_ref, gn_ref, wu_ref, cw_ref, cb_ref, wd_ref, tgt_ref, h2_ref, u_ref, dy_ref, dyb_ref, loss_ref,
             ext_ref):
        i = pl.program_id(0)

        @pl.when(i == 0)
        def _():
            ext_ref[pl.ds(0, CONV_HALO), :] = jnp.zeros((CONV_HALO, 2 * D_FF), F32)
            loss_ref[...] = jnp.zeros_like(loss_ref)

        x1 = x1_ref[...]
        h2, _ = _rmsnorm_fwd(x1, gn_ref[...])
        h2 = h2.astype(MM)
        h2_ref[...] = h2
        for c in range(4):
            cols = slice(c * FF_CHUNK, (c + 1) * FF_CHUNK)
            uc = _mm_nt(h2, wu_ref[cols, :])
            u_ref[:, cols] = uc
            ext_ref[pl.ds(CONV_HALO, tm), cols] = uc
        y = x1
        for c in range(2):
            gate, val = _conv_glu(ext_ref, cw_ref, cb_ref, tm, c)
            act = gate * jax.nn.sigmoid(gate) * val
            y = y + _mm(act, wd_ref[c * FF_CHUNK:(c + 1) * FF_CHUNK, :])
        ext_ref[pl.ds(0, CONV_HALO), :] = ext_ref[pl.ds(tm, CONV_HALO), :]
        err = y - tgt_ref[...]
        loss_ref[...] += jnp.full(loss_ref.shape, 0.5 * inv_d * jnp.sum(err * err), F32)
        dy = err * inv_d
        dy_ref[...] = dy
        dyb_ref[...] = dy.astype(MM)

    sd = jax.ShapeDtypeStruct
    return pl.pallas_call(
        body, name="ffn_fwd", grid=(s // tm,),
        out_shape=(sd((s, D_MODEL), MM), sd((s, 2 * D_FF), F32), sd((s, D_MODEL), F32), sd((s, D_MODEL), MM),
                   sd((8, LANES), F32)),
        in_specs=[_rows(tm, 1024), _resident((1, 1024)), _resident((2 * D_FF, D_MODEL)), _resident((3, 2 * D_FF)),
                  _resident((1, 2 * D_FF)), _resident((D_FF, D_MODEL)), _rows(tm, 1024)],
        out_specs=(_rows(tm, 1024), _rows(tm, 2 * D_FF), _rows(tm, 1024), _rows(tm, 1024),
                   pl.BlockSpec((8, LANES), lambda i: (0, 0))),
        scratch_shapes=[pltpu.VMEM((CONV_HALO + tm, 2 * D_FF), F32)], compiler_params=_params("arbitrary"),
    )(x1, ffn_norm, wup_t, conv_w, conv_b, wdown, target)


def _ffn_bwd_a(dyb, u, conv_w, conv_b, wdown, tm):
    s = dyb.shape[0]
    nt = s // tm
    hb = tm // CONV_HALO
    rev = lambda i: (nt - 1 - i, 0)

    def body(dy_ref, u_ref, halo_ref, cw_ref, cb_ref, wd_ref, du_ref, act_ref, dcw_ref, dcb_ref, ext_ref, extd_ref):
        i = pl.program_id(0)
        first_tile = i == nt - 1

        @pl.when(i == 0)
        def _():
            extd_ref[pl.ds(tm, CONV_HALO), :] = jnp.zeros((CONV_HALO, 2 * D_FF), F32)
            dcw_ref[...] = jnp.zeros_like(dcw_ref)
            dcb_ref[...] = jnp.zeros_like(dcb_ref)

        ext_ref[pl.ds(0, CONV_HALO), :] = jnp.where(first_tile, 0.0, halo_ref[...])
        ext_ref[pl.ds(CONV_HALO, tm), :] = u_ref[...]
        dy = dy_ref[...]
        for c in range(2):
            gate, val = _conv_glu(ext_ref, cw_ref, cb_ref, tm, c)
            sg = jax.nn.sigmoid(gate)
            sl = gate * sg
            act_ref[:, c * FF_CHUNK:(c + 1) * FF_CHUNK] = (sl * val).astype(MM)
            d_act = _mm_nt(dy, wd_ref[c * FF_CHUNK:(c + 1) * FF_CHUNK, :])
            extd_ref[pl.ds(0, tm), c * FF_CHUNK:(c + 1) * FF_CHUNK] = d_act * val * (sg * (1.0 + gate * (1.0 - sg)))
            extd_ref[pl.ds(0, tm), D_FF + c * FF_CHUNK:D_FF + (c + 1) * FF_CHUNK] = d_act * sl
        for c in range(4):
            cols = slice(c * FF_CHUNK, (c + 1) * FF_CHUNK)
            d0 = extd_ref[pl.ds(0, tm), cols]
            dcb_ref[:, cols] += jnp.sum(d0, axis=0, keepdims=True)
            for j in range(3):
                dcw_ref[j:j + 1, cols] += jnp.sum(d0 * ext_ref[pl.ds(CONV_HALO - 2 + j, tm), cols], axis=0,
                                                  keepdims=True)
            du = cw_ref[2:3, cols] * d0 + cw_ref[1:2, cols] * extd_ref[pl.ds(1, tm), cols]
            du = du + cw_ref[0:1, cols] * extd_ref[pl.ds(2, tm), cols]
            du_ref[:, cols] = du.astype(MM)
        extd_ref[pl.ds(tm, CONV_HALO), :] = extd_ref[pl.ds(0, CONV_HALO), :]

    sd = jax.ShapeDtypeStruct
    return pl.pallas_call(
        body, name="ffn_bwd_a", grid=(nt,),
        out_shape=(sd((s, 2 * D_FF), MM), sd((s, D_FF), MM), sd((3, 2 * D_FF), F32), sd((1, 2 * D_FF), F32)),
        in_specs=[pl.BlockSpec((tm, D_MODEL), rev), pl.BlockSpec((tm, 2 * D_FF), rev),
                  pl.BlockSpec((CONV_HALO, 2 * D_FF), lambda i: (jnp.maximum((nt - 1 - i) * hb - 1, 0), 0)),
                  _resident((3, 2 * D_FF)), _resident((1, 2 * D_FF)), _resident((D_FF, D_MODEL))],
        out_specs=(pl.BlockSpec((tm, 2 * D_FF), rev), pl.BlockSpec((tm, D_FF), rev),
                   pl.BlockSpec((3, 2 * D_FF), lambda i: (0, 0)), pl.BlockSpec((1, 2 * D_FF), lambda i: (0, 0))),
        scratch_shapes=[pltpu.VMEM((CONV_HALO + tm, 2 * D_FF), F32), pltpu.VMEM((tm + CONV_HALO, 2 * D_FF), F32)],
        compiler_params=_params("arbitrary"),
    )(dyb, u, u, conv_w, conv_b, wdown)


def _matmul_tn(a, b, tmo, tk, name):
    s, m = a.shape
    n = b.shape[1]
    nk = s // tk

    def body(a_ref, b_ref, o_ref):
        k = pl.program_id(1)

        @pl.when(k == 0)
        def _():
            o_ref[...] = jnp.zeros_like(o_ref)

        o_ref[...] += _mm_tn(a_ref[...], b_ref[pl.ds(pl.multiple_of(k * tk, tk), tk), :])

    return pl.pallas_call(
        body, name=name, grid=(m // tmo, nk), out_shape=jax.ShapeDtypeStruct((m, n), F32),
        in_specs=[pl.BlockSpec((tk, tmo), lambda i, k: (k, i)), _resident((s, n))],
        out_specs=pl.BlockSpec((tmo, n), lambda i, k: (i, 0)), compiler_params=_params("parallel", "arbitrary"),
    )(a, b)


def _ffn_bwd_b(du, wup_t, x1, ffn_norm, dy, tm):
    s = du.shape[0]

    def body(du_ref, wu_ref, x1_ref, gn_ref, dy_ref, dx1_ref, dx1b_ref, dg_ref):
        @pl.when(pl.program_id(0) == 0)
        def _():
            dg_ref[...] = jnp.zeros_like(dg_ref)

        dh2 = _mm(du_ref[...], wu_ref[...])
        x1 = x1_ref[...]
        _, r = _rmsnorm_fwd(x1, gn_ref[...])
        dx, dgr = _rmsnorm_bwd(x1, r, gn_ref[...], dh2)
        dg_ref[...] += jnp.sum(dgr, axis=0, keepdims=True)
        dx1 = dy_ref[...] + dx
        dx1_ref[...] = dx1
        dx1b_ref[...] = dx1.astype(MM)

    return pl.pallas_call(
        body, name="ffn_bwd_b", grid=(s // tm,),
        out_shape=(jax.ShapeDtypeStruct((s, D_MODEL), F32), jax.ShapeDtypeStruct((s, D_MODEL), MM),
                   jax.ShapeDtypeStruct((1, D_MODEL), F32)),
        in_specs=[_rows(tm, 2 * D_FF), _resident((2 * D_FF, D_MODEL)), _rows(tm, 1024), _resident((1, 1024)),
                  _rows(tm, 1024)],
        out_specs=(_rows(tm, 1024), _rows(tm, 1024), pl.BlockSpec((1, D_MODEL), lambda i: (0, 0))),
        compiler_params=_params("arbitrary"),
    )(du, wup_t, x1, ffn_norm, dy)


def _mix_bwd(dx1b, wout, g, a, b, tm):
    s = dx1b.shape[0]

    def body(dx_ref, w_ref, g_ref, a_ref, b_ref, da_ref, db_ref, dzg_ref, dbg_ref):
        @pl.when(pl.program_id(0) == 0)
        def _():
            dbg_ref[...] = jnp.zeros_like(dbg_ref)

        dmix = _mm_nt(dx_ref[...], w_ref[...])
        for half, src, dst in ((0, a_ref, da_ref), (1, b_ref, db_ref)):
            cols = slice(half * 1024, (half + 1) * 1024)
            gt = g_ref[:, cols]
            dst[...] = dmix * gt
            dz = dmix * src[...] * gt * (1.0 - gt)
            dzg_ref[:, cols] = dz.astype(MM)
            dbg_ref[:, cols] += jnp.sum(dz, axis=0, keepdims=True)

    sd = jax.ShapeDtypeStruct
    return pl.pallas_call(
        body, name="mix_bwd", grid=(s // tm,),
        out_shape=(sd((s, 1024), F32), sd((s, 1024), F32), sd((s, 2048), MM), sd((1, 2048), F32)),
        in_specs=[_rows(tm, 1024), _resident((1024, 1024)), _rows(tm, 2048), _rows(tm, 1024), _rows(tm, 1024)],
        out_specs=(_rows(tm, 1024), _rows(tm, 1024), _rows(tm, 2048), pl.BlockSpec((1, 2048), lambda i: (0, 0))),
        compiler_params=_params("arbitrary"),
    )(dx1b, wout, g, a, b)


def _pool_bwd(u, da, wpool, pool_scale, tm):
    s = u.shape[0]
    nt = s // tm
    hb = tm // POOL_HALO

    def body(u_ref, uh_ref, da_ref, dah_ref, wp_ref, ps_ref, dzu_ref, dwp_ref, dps_ref, ext_ref, exte_ref):
        i = pl.program_id(0)

        @pl.when(i == 0)
        def _():
            dwp_ref[...] = jnp.zeros_like(dwp_ref)
            dps_ref[...] = jnp.zeros_like(dps_ref)

        ext_ref[pl.ds(0, POOL_HALO), :] = jnp.where(i > 0, uh_ref[...], 0.0)
        ext_ref[pl.ds(POOL_HALO, tm), :] = u_ref[...]
        pooled = _pooled(ext_ref, tm, i * tm)
        da = da_ref[...]
        dah = jnp.where(i < nt - 1, dah_ref[...], 0.0)
        t = (i * tm + lax.broadcasted_iota(jnp.int32, (tm + POOL_HALO, 1), 0)).astype(F32)
        for gi, w in enumerate(POOL_WINDOWS):
            cols = slice(gi * POOL_GROUP, (gi + 1) * POOL_GROUP)
            pg = pooled[gi].astype(MM)
            wg = wp_ref[gi]
            mixed = _mm(pg, wg)
            dps_ref[:, cols] += jnp.sum(da[:, cols] * mixed, axis=0, keepdims=True)
            dmx = (da[:, cols] * ps_ref[:, cols]).astype(MM)
            dwp_ref[gi] += _mm_tn(pg, dmx)
            dpl = _mm_nt(dmx, wg)
            dplh = _mm_nt(dah[:, cols] * ps_ref[:, cols], wg)
            cnt = jnp.minimum(t + 1.0, float(w))
            exte_ref[pl.ds(0, tm), cols] = dpl / cnt[0:tm]
            exte_ref[pl.ds(tm, POOL_HALO), cols] = dplh / cnt[tm:tm + POOL_HALO]
            acc = exte_ref[pl.ds(0, tm), cols]
            for k in range(1, w):
                acc = acc + exte_ref[pl.ds(k, tm), cols]
            dzu_ref[:, cols] = (acc - dpl).astype(MM)

    sd = jax.ShapeDtypeStruct
    last_halo = s // POOL_HALO - 1
    return pl.pallas_call(
        body, name="pool_bwd", grid=(nt,),
        out_shape=(sd((s, 1024), MM), sd((4, POOL_GROUP, POOL_GROUP), F32), sd((1, 1024), F32)),
        in_specs=[_rows(tm, 1024), pl.BlockSpec((POOL_HALO, 1024), lambda i: (jnp.maximum(i * hb - 1, 0), 0)),
                  _rows(tm, 1024),
                  pl.BlockSpec((POOL_HALO, 1024), lambda i: (jnp.minimum((i + 1) * hb, last_halo), 0)),
                  _resident((4, POOL_GROUP, POOL_GROUP)), _resident((1, 1024))],
        out_specs=(_rows(tm, 1024), pl.BlockSpec((4, POOL_GROUP, POOL_GROUP), lambda i: (0, 0, 0)),
                   pl.BlockSpec((1, 1024), lambda i: (0, 0))),
        scratch_shapes=[pltpu.VMEM((POOL_HALO + tm, 1024), F32), pltpu.VMEM((tm + POOL_HALO, 1024), F32)],
        compiler_params=_params("arbitrary"),
    )(u, u, da, da, wpool, pool_scale)


def _attn_bwd(q, kv, db, tabs, q_norm_t, k_norm_t, sink_col, bd):
    s = q.shape[0]
    nb = s // BLOCK
    scale = HEAD_DIM ** -0.5
    cur = lambda n: (jnp.minimum(n, nb - 1), 0)
    prv = lambda n: (jnp.maximum(n - 1, 0), 0)

    def body(q_ref, kvc_ref, kvp_ref, db_ref, c_ref, s1_ref, s2_ref, cp_ref, s1p_ref, s2p_ref, qn_ref, kn_ref,
             sink_ref, bd_ref, dzq_ref, dzkv_ref, dqn_ref, dkn_ref, dsk_ref,
             carry_ref, tot_ref, dqr_ref, qacc_ref, kacc_ref, sacc_ref):
        n = pl.program_id(0)
        bdm = bd_ref[...]
        kn = kn_ref[...]

        @pl.when(n == 0)
        def _():
            carry_ref[...] = jnp.zeros_like(carry_ref)
            qacc_ref[...] = jnp.zeros_like(qacc_ref)
            kacc_ref[...] = jnp.zeros_like(kacc_ref)
            sacc_ref[...] = jnp.zeros_like(sacc_ref)

        kp_raw = kvp_ref[:, 0:128]
        kph, rp = _head_norm_fwd(kp_raw, kn, bdm)
        cp, s1p, s2p = cp_ref[...], s1p_ref[...], s2p_ref[...]

        @pl.when(n < nb)
        def _():
            c, s1, s2 = c_ref[...], s1_ref[...], s2_ref[...]
            c8, s18, s28 = _tile_lanes(c, 8), _tile_lanes(s1, 8), _tile_lanes(s2, 8)
            q_raw = q_ref[...]
            qh, rq = _head_norm_fwd(q_raw, qn_ref[...], bdm)
            qr = _rope(qh, c8, s18, s28) * scale
            kc, _ = _head_norm_fwd(kvc_ref[:, 0:128], kn, bdm)
            kc = _rope(kc, c, s1, s2)
            kp = _rope(kph, cp, s1p, s2p)
            vc, vp = kvc_ref[:, 128:256], kvp_ref[:, 128:256]
            mask = _attn_mask(n)
            lane = lax.broadcasted_iota(jnp.int32, (1, LANES), 1)
            dsk = jnp.zeros((1, LANES), F32)
            for hk in range(N_KV_HEADS):
                qs = _stack_heads(qr, hk)
                kb = _band(kp, kc, hk)
                vb = _band(vp, vc, hk)
                sc = jnp.where(mask, _mm_nt(qs, kb), NEG)
                pr, psink = _softmax_sink(sc, sink_ref[hk])
                do = _stack_heads(db_ref[...], hk)
                dv = _mm_tn(pr, do)
                dp = _mm_nt(do, vb)
                rowdot = jnp.sum(pr * dp, axis=-1, keepdims=True)
                ds = pr * (dp - rowdot)
                dsr = -psink * rowdot
                dqs = _mm(ds, kb)
                dk = _mm_tn(ds, qs)
                for j in range(GQA_GROUP):
                    h = hk * GQA_GROUP + j
                    dqr_ref[:, h * HEAD_DIM:(h + 1) * HEAD_DIM] = dqs[j * BLOCK:(j + 1) * BLOCK, :]
                    dsk = dsk + jnp.where(lane == h, jnp.sum(dsr[j * BLOCK:(j + 1) * BLOCK, :]), 0.0)
                kcol = slice(hk * HEAD_DIM, (hk + 1) * HEAD_DIM)
                vcol = slice(128 + hk * HEAD_DIM, 128 + (hk + 1) * HEAD_DIM)
                tot_ref[:, kcol] = carry_ref[:, kcol] + dk[0:BLOCK, :]
                tot_ref[:, vcol] = carry_ref[:, vcol] + dv[0:BLOCK, :]
                carry_ref[:, kcol] = dk[BLOCK:2 * BLOCK, :]
                carry_ref[:, vcol] = dv[BLOCK:2 * BLOCK, :]
            sacc_ref[...] += dsk
            dqh = _rope_bwd(dqr_ref[...] * scale, c8, s18, s28)
            dq, dgq = _head_norm_bwd(q_raw, rq, qn_ref[...], dqh, bdm)
            dzq_ref[...] = dq.astype(MM)
            qacc_ref[...] += jnp.sum(dgq, axis=0, keepdims=True)

        @pl.when(n == nb)
        def _():
            tot_ref[...] = carry_ref[...]

        dkh = _rope_bwd(tot_ref[:, 0:128], cp, s1p, s2p)
        dkr, dgk = _head_norm_bwd(kp_raw, rp, kn, dkh, bdm)
        dzkv_ref[:, 0:128] = dkr.astype(MM)
        dzkv_ref[:, 128:256] = tot_ref[:, 128:256].astype(MM)
        kacc_ref[...] += jnp.where(n > 0, jnp.sum(dgk, axis=0, keepdims=True), 0.0)

        @pl.when(n == nb)
        def _():
            fold = qacc_ref[:, 0:HEAD_DIM]
            for h in range(1, N_Q_HEADS):
                fold = fold + qacc_ref[:, h * HEAD_DIM:(h + 1) * HEAD_DIM]
            dqn_ref[...] = fold
            dkn_ref[...] = kacc_ref[:, 0:HEAD_DIM] + kacc_ref[:, HEAD_DIM:2 * HEAD_DIM]
            dsk_ref[...] = sacc_ref[...]

    tab = lambda im: pl.BlockSpec((BLOCK, LANES), im)
    sd = jax.ShapeDtypeStruct
    const = lambda n: (0, 0)
    return pl.pallas_call(
        body, name="attn_bwd", grid=(nb + 1,),
        out_shape=(sd((s, 1024), MM), sd((s, 256), MM), sd((1, HEAD_DIM), F32), sd((1, HEAD_DIM), F32),
                   sd((1, LANES), F32)),
        in_specs=[pl.BlockSpec((BLOCK, 1024), cur), pl.BlockSpec((BLOCK, 256), cur), pl.BlockSpec((BLOCK, 256), prv),
                  pl.BlockSpec((BLOCK, 1024), cur), tab(cur), tab(cur), tab(cur), tab(prv), tab(prv), tab(prv),
                  _resident((1, 1024)), _resident((1, 128)), _resident((N_KV_HEADS, GQA_GROUP * BLOCK, 1)),
                  _resident((LANES, LANES))],
        out_specs=(pl.BlockSpec((BLOCK, 1024), cur), pl.BlockSpec((BLOCK, 256), prv),
                   pl.BlockSpec((1, HEAD_DIM), const), pl.BlockSpec((1, HEAD_DIM), const),
                   pl.BlockSpec((1, LANES), const)),
        scratch_shapes=[pltpu.VMEM((BLOCK, 256), F32), pltpu.VMEM((BLOCK, 256), F32), pltpu.VMEM((BLOCK, 1024), F32),
                        pltpu.VMEM((1, 1024), F32), pltpu.VMEM((1, 128), F32), pltpu.VMEM((1, LANES), F32)],
        compiler_params=_params("arbitrary"),
    )(q, kv, kv, db, *tabs, *tabs, q_norm_t, k_norm_t, sink_col, bd)


def _inproj_bwd(dzu, dzq, dzkv, dzg, win_t, x, attn_norm, dx1, tm):
    s = x.shape[0]

    def body(du_ref, dq_ref, dkv_ref, dg_ref, w_ref, x_ref, gn_ref, dx1_ref, gx_ref, dgn_ref):
        @pl.when(pl.program_id(0) == 0)
        def _():
            dgn_ref[...] = jnp.zeros_like(dgn_ref)

        dh = _mm(du_ref[...], w_ref[O_U:O_Q, :]) + _mm(dq_ref[...], w_ref[O_Q:O_KV, :])
        dh = dh + _mm(dkv_ref[...], w_ref[O_KV:O_G, :]) + _mm(dg_ref[...], w_ref[O_G:IN_WIDTH, :])
        x = x_ref[...]
        _, r = _rmsnorm_fwd(x, gn_ref[...])
        dx, dgr = _rmsnorm_bwd(x, r, gn_ref[...], dh)
        dgn_ref[...] += jnp.sum(dgr, axis=0, keepdims=True)
        gx_ref[...] = dx1_ref[...] + dx

    return pl.pallas_call(
        body, name="inproj_bwd", grid=(s // tm,),
        out_shape=(jax.ShapeDtypeStruct((s, D_MODEL), F32), jax.ShapeDtypeStruct((1, D_MODEL), F32)),
        in_specs=[_rows(tm, 1024), _rows(tm, 1024), _rows(tm, 256), _rows(tm, 2048),
                  _resident((IN_WIDTH, D_MODEL)), _rows(tm, 1024), _resident((1, 1024)), _rows(tm, 1024)],
        out_specs=(_rows(tm, 1024), pl.BlockSpec((1, D_MODEL), lambda i: (0, 0))),
        compiler_params=_params("arbitrary"),
    )(dzu, dzq, dzkv, dzg, win_t, x, attn_norm, dx1)


def _local_step(x, pos_col, target, attn_norm, win_t, b_gate, wpool, pool_scale, q_norm, k_norm, sinks, wout,
                ffn_norm, wup_t, conv_w, conv_b, wdown, tm):
    inv_freq = ROPE_THETA ** (-jnp.arange(0, ROPE_DIM, 2, dtype=F32) / ROPE_DIM)
    lane = jnp.arange(LANES) % HEAD_DIM
    invf = jnp.where(lane < ROPE_DIM, inv_freq[lane % (ROPE_DIM // 2)], 0.0).reshape(1, LANES).astype(F32)
    bd = (jnp.arange(LANES)[:, None] // HEAD_DIM == jnp.arange(LANES)[None, :] // HEAD_DIM).astype(MM)
    q_norm_t = jnp.tile(q_norm, (1, N_Q_HEADS))
    k_norm_t = jnp.tile(k_norm, (1, N_KV_HEADS))
    sink_col = jnp.repeat(sinks.reshape(N_KV_HEADS, GQA_GROUP), BLOCK, axis=1).reshape(N_KV_HEADS, GQA_GROUP * BLOCK, 1)

    tabs = _rope_tables(pos_col, invf)
    h, u, q, kv, g = _inproj_fwd(x, attn_norm, win_t, b_gate, tm)
    a = _pool_fwd(u, wpool, pool_scale, tm)
    b = _attn_fwd(q, kv, tabs, q_norm_t, k_norm_t, sink_col, bd)
    x1, mix = _mix_out_fwd(x, g, a, b, wout, tm)
    h2, uff, dy, dyb, lossp = _ffn_fwd(x1, ffn_norm, wup_t, conv_w, conv_b, wdown, target, tm)

    du, act, d_conv_w, d_conv_b = _ffn_bwd_a(dyb, uff, conv_w, conv_b, wdown, tm)
    tk = min(x.shape[0], 1024)
    d_wdown = _matmul_tn(act, dyb, FF_CHUNK, tk, "dw_down")
    dx1, dx1b, d_ffn_norm = _ffn_bwd_b(du, wup_t, x1, ffn_norm, dy, tm)
    d_wup_t = _matmul_tn(du, h2, 512, tk, "dw_up")
    d_wout = _matmul_tn(mix, dx1b, 512, tk, "dw_out")
    da, db, dzg, d_b_gate = _mix_bwd(dx1b, wout, g, a, b, tm)
    dzu, d_wpool, d_pool_scale = _pool_bwd(u, da, wpool, pool_scale, tm)
    dzq, dzkv, d_q_norm, d_k_norm, d_sinks = _attn_bwd(q, kv, db, tabs, q_norm_t, k_norm_t, sink_col, bd)
    grad_x, d_attn_norm = _inproj_bwd(dzu, dzq, dzkv, dzg, win_t, x, attn_norm, dx1, tm)
    d_win_t = jnp.concatenate([_matmul_tn(dzu, h, 512, tk, "dw_in_u"), _matmul_tn(dzq, h, 512, tk, "dw_in_q"),
                               _matmul_tn(dzkv, h, 256, tk, "dw_in_kv"), _matmul_tn(dzg, h, 512, tk, "dw_in_g")],
                              axis=0)
    grads = dict(attn_norm=d_attn_norm, w_in_t=d_win_t, b_gate=d_b_gate, w_pool=d_wpool, pool_scale=d_pool_scale,
                 q_norm=d_q_norm, k_norm=d_k_norm, sinks=d_sinks[:, 0:N_Q_HEADS], w_out=d_wout, ffn_norm=d_ffn_norm,
                 w_up_t=d_wup_t, conv_w=d_conv_w, conv_b=d_conv_b, w_down=d_wdown)
    return lossp, grad_x, grads


ANY = pl.BlockSpec(memory_space=pl.ANY)


def _position():
    return lax.axis_index("x"), lax.axis_index("y"), lax.axis_index("c")


def _all_gather(shards):
    k = len(shards)

    def body(*refs):
        ins, outs = refs[:k], refs[k:2 * k]
        send_sems, recv_sems, local_sems = refs[2 * k:]
        x, y, c = _position()
        me, sibling = (x, y, c), (x, y, 1 - c)
        chips = [(1 - x, y), (x, 1 - y), (1 - x, 1 - y)]

        def copy(a, kk, block, to, src=None):
            dst = outs[a].at[4 * block[0] + 2 * block[1] + block[2]]
            return pltpu.make_async_remote_copy(
                src_ref=dst if src is None else src, dst_ref=dst, send_sem=send_sems.at[a * 7 + kk],
                recv_sem=recv_sems.at[a * 7 + kk], device_id=to, device_id_type=MESH)

        mine = [pltpu.make_async_copy(ins[a], outs[a].at[4 * x + 2 * y + c], local_sems.at[a]) for a in range(k)]
        for cp in mine:
            cp.start()
        first = []
        for a in range(k):
            first.append(copy(a, 0, me, sibling, src=ins[a]))
            first += [copy(a, 1 + j, me, (*chip, c), src=ins[a]) for j, chip in enumerate(chips)]
        for cp in first:
            cp.start()
        passed = []
        for j, chip in enumerate(chips):
            for a in range(k):
                copy(a, 1 + j, (*chip, c), me).wait_recv()
                cp = copy(a, 4 + j, (*chip, c), sibling)
                cp.start()
                passed.append(cp)
        for a in range(k):
            copy(a, 0, sibling, me).wait_recv()
            for j, chip in enumerate(chips):
                copy(a, 4 + j, (*chip, 1 - c), me).wait_recv()
        for cp in first + passed:
            cp.wait_send()
        for cp in mine:
            cp.wait()

    return pl.pallas_call(
        body, name="all_gather_weights",
        out_shape=tuple(jax.ShapeDtypeStruct((N_DEV,) + s.shape, s.dtype) for s in shards),
        in_specs=[ANY] * k, out_specs=(ANY,) * k,
        scratch_shapes=[pltpu.SemaphoreType.DMA((7 * k,)), pltpu.SemaphoreType.DMA((7 * k,)),
                        pltpu.SemaphoreType.DMA((k,))],
    )(*shards)


def _pair_exchange(fulls):
    k = len(fulls)

    def body(*refs):
        ins, outs = refs[:k], refs[k:2 * k]
        send_sems, recv_sems = refs[2 * k:]
        x, y, c = _position()
        cps = []
        for a in range(k):
            for ch in range(4):
                cps.append(pltpu.make_async_remote_copy(
                    src_ref=ins[a].at[2 * ch + 1 - c], dst_ref=outs[a].at[ch], send_sem=send_sems.at[a * 4 + ch],
                    recv_sem=recv_sems.at[a * 4 + ch], device_id=(x, y, 1 - c), device_id_type=MESH))
        for cp in cps:
            cp.start()
        for cp in cps:
            cp.wait()

    return pl.pallas_call(
        body, name="grad_pair_exchange",
        out_shape=tuple(jax.ShapeDtypeStruct((4,) + f.shape[1:], f.dtype) for f in fulls),
        in_specs=[ANY] * k, out_specs=(ANY,) * k,
        scratch_shapes=[pltpu.SemaphoreType.DMA((4 * k,)), pltpu.SemaphoreType.DMA((4 * k,))],
    )(*fulls)


def _pair_add(full, recv, wire):
    _, r, c_ = full.shape

    def body(f_ref, r_ref, pw_ref, own_ref):
        ch = pl.program_id(0)
        x, y, c = _position()
        tot = f_ref[0, c] + r_ref[0]
        pw_ref[0] = tot.astype(pw_ref.dtype)

        @pl.when(ch == 2 * x + y)
        def _():
            own_ref[...] = tot

    return pl.pallas_call(
        body, name="grad_pair_add", grid=(4,),
        out_shape=(jax.ShapeDtypeStruct((4, r, c_), wire), jax.ShapeDtypeStruct((r, c_), F32)),
        in_specs=[pl.BlockSpec((1, 2, r, c_), lambda i: (i, 0, 0, 0)), pl.BlockSpec((1, r, c_), lambda i: (i, 0, 0))],
        out_specs=(pl.BlockSpec((1, r, c_), lambda i: (i, 0, 0)), pl.BlockSpec((r, c_), lambda i: (0, 0))),
        compiler_params=_params("arbitrary"),
    )(full.reshape(4, 2, r, c_), recv)


def _chip_exchange(pws):
    k = len(pws)

    def body(*refs):
        ins, outs = refs[:k], refs[k:2 * k]
        send_sems, recv_sems = refs[2 * k:]
        x, y, c = _position()
        cps = []
        for a in range(k):
            for rel, (px, py) in enumerate([(1 - x, y), (x, 1 - y), (1 - x, 1 - y)]):
                cps.append(pltpu.make_async_remote_copy(
                    src_ref=ins[a].at[2 * px + py], dst_ref=outs[a].at[rel], send_sem=send_sems.at[a * 3 + rel],
                    recv_sem=recv_sems.at[a * 3 + rel], device_id=(px, py, c), device_id_type=MESH))
        for cp in cps:
            cp.start()
        for cp in cps:
            cp.wait()

    return pl.pallas_call(
        body, name="grad_chip_exchange",
        out_shape=tuple(jax.ShapeDtypeStruct((3,) + p.shape[1:], p.dtype) for p in pws),
        in_specs=[ANY] * k, out_specs=(ANY,) * k,
        scratch_shapes=[pltpu.SemaphoreType.DMA((3 * k,)), pltpu.SemaphoreType.DMA((3 * k,))],
    )(*pws)


def _adamw_math(w, g, m, v):
    m = ADAM_B1 * m + (1.0 - ADAM_B1) * g
    v = ADAM_B2 * v + (1.0 - ADAM_B2) * (g * g)
    m_hat = m / (1.0 - ADAM_B1 ** ADAM_STEP)
    v_hat = v / (1.0 - ADAM_B2 ** ADAM_STEP)
    delta = -ADAM_LR * (m_hat / (jnp.sqrt(v_hat) + ADAM_EPS) + ADAM_WD * w)
    return delta, m, v


def _row_tile(r):
    for t in (256, 176, 128):
        if r % t == 0 and r > t:
            return t
    return r


def _sum_parts(own, recv):
    r, c_ = own.shape
    t = _row_tile(r)

    def body(o_ref, r_ref, g_ref):
        g = o_ref[...]
        for i in range(3):
            g = g + r_ref[i].astype(F32)
        g_ref[...] = g

    return pl.pallas_call(
        body, name="grad_sum", grid=(r // t,), out_shape=jax.ShapeDtypeStruct((r, c_), F32),
        in_specs=[pl.BlockSpec((t, c_), lambda i: (i, 0)), pl.BlockSpec((3, t, c_), lambda i: (0, i, 0))],
        out_specs=pl.BlockSpec((t, c_), lambda i: (i, 0)), compiler_params=_params("parallel"),
    )(own, recv)


def _adamw(g_own, recv, w, m, v):
    r, c_ = w.shape
    t = _row_tile(r)
    blk = pl.BlockSpec((t, c_), lambda i: (i, 0))

    def body(*refs):
        if recv is None:
            g_ref, w_ref, m_ref, v_ref, go_ref, d_ref, mo_ref, vo_ref = refs
            g = g_ref[...]
        else:
            g_ref, r_ref, w_ref, m_ref, v_ref, go_ref, d_ref, mo_ref, vo_ref = refs
            g = g_ref[...]
            for i in range(3):
                g = g + r_ref[i].astype(F32)
        go_ref[...] = g
        d_ref[...], mo_ref[...], vo_ref[...] = _adamw_math(w_ref[...], g, m_ref[...], v_ref[...])

    ins = [g_own] + ([] if recv is None else [recv]) + [w, m, v]
    specs = [blk] + ([] if recv is None else [pl.BlockSpec((3, t, c_), lambda i: (0, i, 0))]) + [blk] * 3
    return pl.pallas_call(
        body, name="adamw", grid=(r // t,), out_shape=(jax.ShapeDtypeStruct((r, c_), F32),) * 4,
        in_specs=specs, out_specs=(blk,) * 4, compiler_params=_params("parallel"),
    )(*ins)


SMALL_ROWS = 88


def _small_allreduce_adamw(gp, wp, mp, vp):
    def body(g_ref, w_ref, m_ref, v_ref, go_ref, d_ref, mo_ref, vo_ref, slots_ref, send_sems, recv_sems):
        x, y, c = _position()
        me = 4 * x + 2 * y + c
        slots_ref[me] = g_ref[...]
        cps = []
        for rel in range(1, N_DEV):
            fx, fy, fc = (rel >> 2) & 1, (rel >> 1) & 1, rel & 1
            to = (1 - x if fx else x, 1 - y if fy else y, 1 - c if fc else c)
            cps.append(pltpu.make_async_remote_copy(
                src_ref=g_ref, dst_ref=slots_ref.at[me], send_sem=send_sems.at[rel - 1],
                recv_sem=recv_sems.at[rel - 1], device_id=to, device_id_type=MESH))
        for cp in cps:
            cp.start()
        for cp in cps:
            cp.wait()
        g = slots_ref[0]
        for i in range(1, N_DEV):
            g = g + slots_ref[i]
        go_ref[...] = g
        d_ref[...], mo_ref[...], vo_ref[...] = _adamw_math(w_ref[...], g, m_ref[...], v_ref[...])

    vm = pl.BlockSpec(memory_space=pltpu.VMEM)
    return pl.pallas_call(
        body, name="small_allreduce_adamw", out_shape=(jax.ShapeDtypeStruct((SMALL_ROWS, LANES), F32),) * 4,
        in_specs=[vm] * 4, out_specs=(vm,) * 4,
        scratch_shapes=[pltpu.VMEM((N_DEV, SMALL_ROWS, LANES), F32), pltpu.SemaphoreType.DMA((N_DEV - 1,)),
                        pltpu.SemaphoreType.DMA((N_DEV - 1,))],
    )(gp, wp, mp, vp)


SMALL = ("attn_norm", "b_gate", "pool_scale", "q_norm", "k_norm", "sinks", "ffn_norm", "conv_b")
SHARDED = ("w_in", "w_pool", "w_out", "w_up", "conv_w", "w_down")
WEIGHTS = ("attn_norm", "w_in", "b_gate", "w_pool", "pool_scale", "q_norm", "k_norm", "sinks", "w_out", "ffn_norm",
           "w_up", "conv_w", "conv_b", "w_down")


def _pack_small(d):
    flat = jnp.concatenate([d[n].reshape(-1) for n in SMALL])
    return jnp.pad(flat, (0, SMALL_ROWS * LANES - flat.shape[0])).reshape(SMALL_ROWS, LANES)


def _unpack_small(p, like):
    flat, out, o = p.reshape(-1), {}, 0
    for n in SMALL:
        out[n] = flat[o:o + like[n].size].reshape(like[n].shape)
        o += like[n].size
    return out


def kernel(x, positions, attn_norm, w_in, b_gate, w_pool, pool_scale, q_norm, k_norm, sinks, w_out, ffn_norm, w_up, conv_w, conv_b, w_down, loss_target, m_attn_norm, m_w_in, m_b_gate, m_w_pool, m_pool_scale, m_q_norm, m_k_norm, m_sinks, m_w_out, m_ffn_norm, m_w_up, m_conv_w, m_conv_b, m_w_down, v_attn_norm, v_w_in, v_b_gate, v_w_pool, v_pool_scale, v_q_norm, v_k_norm, v_sinks, v_w_out, v_ffn_norm, v_w_up, v_conv_w, v_conv_b, v_w_down):
    w = dict(attn_norm=attn_norm, w_in=w_in, b_gate=b_gate, w_pool=w_pool, pool_scale=pool_scale, q_norm=q_norm,
             k_norm=k_norm, sinks=sinks, w_out=w_out, ffn_norm=ffn_norm, w_up=w_up, conv_w=conv_w, conv_b=conv_b,
             w_down=w_down)
    m = dict(attn_norm=m_attn_norm, w_in=m_w_in, b_gate=m_b_gate, w_pool=m_w_pool, pool_scale=m_pool_scale,
             q_norm=m_q_norm, k_norm=m_k_norm, sinks=m_sinks, w_out=m_w_out, ffn_norm=m_ffn_norm, w_up=m_w_up,
             conv_w=m_conv_w, conv_b=m_conv_b, w_down=m_w_down)
    v = dict(attn_norm=v_attn_norm, w_in=v_w_in, b_gate=v_b_gate, w_pool=v_w_pool, pool_scale=v_pool_scale,
             q_norm=v_q_norm, k_norm=v_k_norm, sinks=v_sinks, w_out=v_w_out, ffn_norm=v_ffn_norm, w_up=v_w_up,
             conv_w=v_conv_w, conv_b=v_conv_b, w_down=v_w_down)
    seq = x.shape[1]

    shards = [w_in[0].T.astype(MM), w_pool[0].astype(MM).reshape(128, POOL_GROUP), w_out[0].astype(MM),
              w_up[0].T.astype(MM), conv_w[0], w_down[0].astype(MM)]
    g_win, g_wpool, g_wout, g_wup, g_convw, g_wdown = _all_gather(shards)
    win_t = g_win.reshape(IN_WIDTH, D_MODEL)
    wpool = g_wpool.reshape(N_DEV, 4, 32, POOL_GROUP).transpose(1, 0, 2, 3).reshape(4, POOL_GROUP, POOL_GROUP)
    wout = g_wout.reshape(D_MODEL, D_MODEL)
    wup_t = g_wup.reshape(2 * D_FF, D_MODEL)
    convw = g_convw.transpose(1, 0, 2).reshape(3, 2 * D_FF)
    wdown = g_wdown.reshape(D_FF, D_MODEL)

    lossp, grad_x, gr = _local_step(
        x[0], positions.reshape(seq, 1), loss_target[0], attn_norm, win_t, b_gate, wpool, pool_scale, q_norm, k_norm,
        sinks, wout, ffn_norm, wup_t, convw, conv_b, wdown, tm=256)
    loss = lax.psum(lossp[0, 0], ("x", "y", "c"))

    fulls = [gr["w_in_t"].reshape(N_DEV, 544, D_MODEL),
             gr["w_pool"].reshape(4, N_DEV, 32, POOL_GROUP).transpose(1, 0, 2, 3).reshape(N_DEV, 128, POOL_GROUP),
             gr["w_out"].reshape(N_DEV, 128, D_MODEL), gr["w_up_t"].reshape(N_DEV, 704, D_MODEL),
             gr["conv_w"].reshape(3, N_DEV, 704).transpose(1, 0, 2), gr["w_down"].reshape(N_DEV, 352, D_MODEL)]
    from_sibling = _pair_exchange(fulls)
    wires = (WIRE, WIRE, WIRE, WIRE, F32, WIRE)
    pws, owns = zip(*[_pair_add(f, r, wd) for f, r, wd in zip(fulls, from_sibling, wires)])
    recvs = _chip_exchange(list(pws))

    out = {}
    nat = {"w_in": (D_MODEL, 544), "w_pool": (128, POOL_GROUP), "w_out": (128, D_MODEL), "w_up": (D_MODEL, 704),
           "conv_w": (3, 704), "w_down": (352, D_MODEL)}
    for name, own, recv in zip(SHARDED, owns, recvs):
        w2, m2, v2 = (t[name].reshape(nat[name]) for t in (w, m, v))
        if name in ("w_in", "w_up"):
            res = _adamw(_sum_parts(own, recv).T, None, w2, m2, v2)
        else:
            res = _adamw(own, recv, w2, m2, v2)
        out[name] = [t.reshape(w[name].shape) for t in res]

    small = _small_allreduce_adamw(_pack_small(gr), _pack_small(w), _pack_small(m), _pack_small(v))
    unpacked = [_unpack_small(p, w) for p in small]
    for name in SMALL:
        out[name] = [u[name] for u in unpacked]

    return (loss, grad_x[None], *[out[n][0] for n in WEIGHTS], *[out[n][1] for n in WEIGHTS],
            *[out[n][2] for n in WEIGHTS], *[out[n][3] for n in WEIGHTS])
```

```python
import functools

import jax
import jax.numpy as jnp
from jax import lax
from jax.experimental import pallas as pl
from jax.experimental.pallas import tpu as pltpu

F32 = jnp.float32
MM = jnp.bfloat16
WIRE = jnp.bfloat16

D_MODEL = 1024
D_FF = 2816
HEAD_DIM = 64
N_Q_HEADS = 16
N_KV_HEADS = 2
GQA_GROUP = 8
BLOCK = 128
ROPE_DIM = 16
ROPE_THETA = 500000.0
POOL_WINDOWS = (2, 4, 8, 16)
POOL_GROUP = 256
POOL_HALO = 16
CONV_HALO = 8
EPS = 1e-6
NEG = -1e30
O_U, O_Q, O_KV, O_G, IN_WIDTH = 0, 1024, 2048, 2304, 4352
FF_CHUNK = 1408

ADAM_LR, ADAM_B1, ADAM_B2, ADAM_EPS, ADAM_WD, ADAM_STEP = 0.001, 0.9, 0.999, 1e-08, 0.01, 10

N_DEV = 8
LANES = 128
VMEM_LIMIT_BYTES = 56 * 1024 * 1024
MESH = pl.DeviceIdType.MESH


def _params(*sem):
    return pltpu.CompilerParams(dimension_semantics=sem, vmem_limit_bytes=VMEM_LIMIT_BYTES)


def _resident(shape):
    nd = len(shape)
    return pl.BlockSpec(shape, lambda *_: (0,) * nd, pipeline_mode=pl.Buffered(1))


def _rows(tm, width):
    return pl.BlockSpec((tm, width), lambda i: (i, 0))


def _mm(a, b):
    return jnp.dot(a.astype(MM), b.astype(MM), preferred_element_type=F32)


def _mm_nt(a, b):
    return lax.dot_general(a.astype(MM), b.astype(MM), (((1,), (1,)), ((), ())), preferred_element_type=F32)


def _mm_tn(a, b):
    return lax.dot_general(a.astype(MM), b.astype(MM), (((0,), (0,)), ((), ())), preferred_element_type=F32)


def _rmsnorm_fwd(x, g):
    r = lax.rsqrt(jnp.mean(x * x, axis=-1, keepdims=True) + EPS)
    return x * r * g, r


def _rmsnorm_bwd(x, r, g, dy):
    xn = x * r
    dxn = dy * g
    dx = r * (dxn - xn * jnp.mean(dxn * xn, axis=-1, keepdims=True))
    return dx, dy * xn


def _group_sum64(v, bd):
    hi = v.astype(MM)
    lo = (v - hi.astype(F32)).astype(MM)
    outs = []
    for t in range(v.shape[1] // LANES):
        sl = slice(LANES * t, LANES * (t + 1))
        outs.append(jnp.dot(hi[:, sl], bd, preferred_element_type=F32)
                    + jnp.dot(lo[:, sl], bd, preferred_element_type=F32))
    return outs[0] if len(outs) == 1 else jnp.concatenate(outs, axis=1)


def _head_norm_fwd(x, g, bd):
    r = lax.rsqrt(_group_sum64(x * x, bd) * (1.0 / HEAD_DIM) + EPS)
    return x * r * g, r


def _head_norm_bwd(x, r, g, dy, bd):
    xn = x * r
    dxn = dy * g
    dx = r * (dxn - xn * (_group_sum64(dxn * xn, bd) * (1.0 / HEAD_DIM)))
    return dx, dy * xn


def _rope(x, c, s1, s2):
    w = x.shape[1]
    return x * c + pltpu.roll(x, w - ROPE_DIM // 2, 1) * s1 + pltpu.roll(x, ROPE_DIM // 2, 1) * s2


def _rope_bwd(dy, c, s1, s2):
    w = dy.shape[1]
    return dy * c + pltpu.roll(dy * s1, ROPE_DIM // 2, 1) + pltpu.roll(dy * s2, w - ROPE_DIM // 2, 1)


def _tile_lanes(t, reps):
    return t if reps == 1 else jnp.concatenate([t] * reps, axis=1)


def _rope_tables(pos_col, invf):
    s = pos_col.shape[0]
    tm = min(s, 1024)

    def body(pos_ref, invf_ref, c_ref, s1_ref, s2_ref):
        ang = pos_ref[...].astype(F32) * invf_ref[...]
        lane = lax.broadcasted_iota(jnp.int32, ang.shape, 1) % HEAD_DIM
        sn = jnp.sin(ang)
        c_ref[...] = jnp.cos(ang)
        s1_ref[...] = jnp.where(lane < ROPE_DIM // 2, -sn, 0.0)
        s2_ref[...] = jnp.where((lane >= ROPE_DIM // 2) & (lane < ROPE_DIM), sn, 0.0)

    out = jax.ShapeDtypeStruct((s, LANES), F32)
    return pl.pallas_call(
        body, name="rope_tables", grid=(s // tm,), out_shape=(out, out, out),
        in_specs=[_rows(tm, 1), _resident((1, LANES))],
        out_specs=(_rows(tm, LANES),) * 3, compiler_params=_params("parallel"),
    )(pos_col, invf)


def _inproj_fwd(x, attn_norm, win_t, b_gate, tm):
    s = x.shape[0]

    def body(x_ref, gn_ref, w_ref, bg_ref, h_ref, u_ref, q_ref, kv_ref, g_ref):
        h, _ = _rmsnorm_fwd(x_ref[...], gn_ref[...])
        h = h.astype(MM)
        h_ref[...] = h
        u_ref[...] = _mm_nt(h, w_ref[O_U:O_Q, :])
        q_ref[...] = _mm_nt(h, w_ref[O_Q:O_KV, :])
        kv_ref[...] = _mm_nt(h, w_ref[O_KV:O_G, :])
        g_ref[...] = jax.nn.sigmoid(_mm_nt(h, w_ref[O_G:IN_WIDTH, :]) + bg_ref[...])

    sd = jax.ShapeDtypeStruct
    return pl.pallas_call(
        body, name="inproj_fwd", grid=(s // tm,),
        out_shape=(sd((s, D_MODEL), MM), sd((s, 1024), F32), sd((s, 1024), F32), sd((s, 256), F32),
                   sd((s, 2048), F32)),
        in_specs=[_rows(tm, D_MODEL), _resident((1, D_MODEL)), _resident((IN_WIDTH, D_MODEL)), _resident((1, 2048))],
        out_specs=(_rows(tm, D_MODEL), _rows(tm, 1024), _rows(tm, 1024), _rows(tm, 256), _rows(tm, 2048)),
        compiler_params=_params("parallel"),
    )(x, attn_norm, win_t, b_gate)


def _pooled(ext_ref, tm, row0):
    t = (row0 + lax.broadcasted_iota(jnp.int32, (tm, 1), 0)).astype(F32)
    out = []
    for gi, w in enumerate(POOL_WINDOWS):
        cols = slice(gi * POOL_GROUP, (gi + 1) * POOL_GROUP)
        acc = ext_ref[pl.ds(POOL_HALO, tm), cols]
        for k in range(1, w):
            acc = acc + ext_ref[pl.ds(POOL_HALO - k, tm), cols]
        cnt = jnp.minimum(t + 1.0, float(w))
        out.append(acc / cnt - ext_ref[pl.ds(POOL_HALO, tm), cols])
    return out


def _pool_fwd(u, wpool, pool_scale, tm):
    s = u.shape[0]
    hb = tm // POOL_HALO

    def body(u_ref, halo_ref, wp_ref, ps_ref, a_ref, ext_ref):
        i = pl.program_id(0)
        ext_ref[pl.ds(0, POOL_HALO), :] = jnp.where(i > 0, halo_ref[...], 0.0)
        ext_ref[pl.ds(POOL_HALO, tm), :] = u_ref[...]
        pooled = _pooled(ext_ref, tm, i * tm)
        for gi in range(4):
            cols = slice(gi * POOL_GROUP, (gi + 1) * POOL_GROUP)
            a_ref[:, cols] = _mm(pooled[gi], wp_ref[gi]) * ps_ref[:, cols]

    return pl.pallas_call(
        body, name="pool_fwd", grid=(s // tm,), out_shape=jax.ShapeDtypeStruct((s, 1024), F32),
        in_specs=[_rows(tm, 1024), pl.BlockSpec((POOL_HALO, 1024), lambda i: (jnp.maximum(i * hb - 1, 0), 0)),
                  _resident((4, POOL_GROUP, POOL_GROUP)), _resident((1, 1024))],
        out_specs=_rows(tm, 1024), scratch_shapes=[pltpu.VMEM((POOL_HALO + tm, 1024), F32)],
        compiler_params=_params("parallel"),
    )(u, u, wpool, pool_scale)


PAIRS = GQA_GROUP // 2
PAIR_COLS = PAIRS * BLOCK


def _attn_mask_t(n):
    shape = (2 * BLOCK, PAIR_COLS)
    kj = lax.broadcasted_iota(jnp.int32, shape, 0)
    qi = lax.broadcasted_iota(jnp.int32, shape, 1) % BLOCK
    return (kj > qi) & (kj <= qi + BLOCK) & ((n > 0) | (kj >= BLOCK))


def _stack_pairs(x, hk):
    return jnp.concatenate([x[:, (PAIRS * hk + j) * LANES:(PAIRS * hk + j + 1) * LANES] for j in range(PAIRS)], axis=0)


def _parity_bands(t, hk):
    low = lax.broadcasted_iota(jnp.int32, t.shape, 1) < HEAD_DIM
    own = jnp.where(low if hk == 0 else ~low, t, 0.0)
    other = pltpu.roll(own, HEAD_DIM, 1)
    return (own, other) if hk == 0 else (other, own)


def _fold_parity(even, odd, hk):
    low = lax.broadcasted_iota(jnp.int32, even.shape, 1) < HEAD_DIM
    comb = jnp.where(low, even, odd)
    comb = comb + pltpu.roll(comb, HEAD_DIM, 1)
    return jnp.where(low if hk == 0 else ~low, comb, 0.0)


def _softmax_sink_t(s, sink):
    m = jnp.maximum(jnp.max(s, axis=0, keepdims=True), sink)
    p = jnp.exp(s - m)
    es = jnp.exp(sink - m)
    inv = 1.0 / (jnp.sum(p, axis=0, keepdims=True) + es)
    return p * inv, es * inv


def _attn_fwd(q, kv, tabs, q_norm_t, k_norm_t, sink_rows, bd):
    s = q.shape[0]
    nb = s // BLOCK
    scale = HEAD_DIM ** -0.5
    cur = lambda n: (n, 0)
    prv = lambda n: (jnp.maximum(n - 1, 0), 0)

    def body(q_ref, kvc_ref, kvp_ref, c_ref, s1_ref, s2_ref, cp_ref, s1p_ref, s2p_ref, qn_ref, kn_ref, sink_ref,
             bd_ref, o_ref):
        n = pl.program_id(0)
        bdm = bd_ref[...]
        c, s1, s2 = c_ref[...], s1_ref[...], s2_ref[...]
        qh, _ = _head_norm_fwd(q_ref[...], qn_ref[...], bdm)
        qr = (_rope(qh, _tile_lanes(c, 8), _tile_lanes(s1, 8), _tile_lanes(s2, 8)) * scale).astype(MM)
        kc, _ = _head_norm_fwd(kvc_ref[:, 0:128], kn_ref[...], bdm)
        kp, _ = _head_norm_fwd(kvp_ref[:, 0:128], kn_ref[...], bdm)
        k2 = jnp.concatenate([_rope(kp, cp_ref[...], s1p_ref[...], s2p_ref[...]), _rope(kc, c, s1, s2)], axis=0)
        v2 = jnp.concatenate([kvp_ref[:, 128:256], kvc_ref[:, 128:256]], axis=0)
        mask = _attn_mask_t(n)
        for hk in range(N_KV_HEADS):
            qs = _stack_pairs(qr, hk)
            ot = jnp.zeros((LANES, PAIR_COLS), F32)
            for par, (kb, vb) in enumerate(zip(_parity_bands(k2, hk), _parity_bands(v2, hk))):
                sc = jnp.where(mask, _mm_nt(kb, qs), NEG)
                pr, _ = _softmax_sink_t(sc, sink_ref[hk, par])
                ot = ot + _mm(vb.T, pr)
            for j in range(PAIRS):
                col = (PAIRS * hk + j) * LANES
                o_ref[:, col:col + LANES] = ot[:, j * BLOCK:(j + 1) * BLOCK].T

    tab = lambda im: pl.BlockSpec((BLOCK, LANES), im)
    return pl.pallas_call(
        body, name="attn_fwd", grid=(nb,), out_shape=jax.ShapeDtypeStruct((s, 1024), F32),
        in_specs=[pl.BlockSpec((BLOCK, 1024), cur), pl.BlockSpec((BLOCK, 256), cur), pl.BlockSpec((BLOCK, 256), prv),
                  tab(cur), tab(cur), tab(cur), tab(prv), tab(prv), tab(prv),
                  _resident((1, 1024)), _resident((1, 128)), _resident((N_KV_HEADS, 2, 1, PAIR_COLS)),
                  _resident((LANES, LANES))],
        out_specs=pl.BlockSpec((BLOCK, 1024), cur), compiler_params=_params("parallel"),
    )(q, kv, kv, *tabs, *tabs, q_norm_t, k_norm_t, sink_rows, bd)


def _mix_out_fwd(x, g, a, b, wout, tm):
    s = x.shape[0]

    def body(x_ref, g_ref, a_ref, b_ref, w_ref, x1_ref, mix_ref):
        mix = (g_ref[:, 0:1024] * a_ref[...] + g_ref[:, 1024:2048] * b_ref[...]).astype(MM)
        mix_ref[...] = mix
        x1_ref[...] = x_ref[...] + _mm(mix, w_ref[...])

    return pl.pallas_call(
        body, name="mix_out_fwd", grid=(s // tm,),
        out_shape=(jax.ShapeDtypeStruct((s, D_MODEL), F32), jax.ShapeDtypeStruct((s, D_MODEL), MM)),
        in_specs=[_rows(tm, 1024), _rows(tm, 2048), _rows(tm, 1024), _rows(tm, 1024), _resident((1024, 1024))],
        out_specs=(_rows(tm, 1024), _rows(tm, 1024)), compiler_params=_params("parallel"),
    )(x, g, a, b, wout)


def _conv_glu(ext_ref, cw_ref, cb_ref, tm, c):
    out = []
    for base in (c * FF_CHUNK, D_FF + c * FF_CHUNK):
        cols = slice(base, base + FF_CHUNK)
        y = cb_ref[:, cols] + cw_ref[0:1, cols] * ext_ref[pl.ds(CONV_HALO - 2, tm), cols]
        y = y + cw_ref[1:2, cols] * ext_ref[pl.ds(CONV_HALO - 1, tm), cols]
        y = y + cw_ref[2:3, cols] * ext_ref[pl.ds(CONV_HALO, tm), cols]
        out.append(y)
    return out


def _ffn_fwd(x1, ffn_norm, wup_t, conv_w, conv_b, wdown, target, tm):
    s = x1.shape[0]
    inv_d = 1.0 / D_MODEL

    def body(x1_ref, gn_ref, wu_ref, cw_ref, cb_ref, wd_ref, tgt_ref, h2_ref, u_ref, dy_ref, dyb_ref, loss_ref,
             ext_ref):
        i = pl.program_id(0)

        @pl.when(i == 0)
        def _():
            ext_ref[pl.ds(0, CONV_HALO), :] = jnp.zeros((CONV_HALO, 2 * D_FF), F32)
            loss_ref[...] = jnp.zeros_like(loss_ref)

        x1 = x1_ref[...]
        h2, _ = _rmsnorm_fwd(x1, gn_ref[...])
        h2 = h2.astype(MM)
        h2_ref[...] = h2
        for c in range(4):
            cols = slice(c * FF_CHUNK, (c + 1) * FF_CHUNK)
            uc = _mm_nt(h2, wu_ref[cols, :])
            u_ref[:, cols] = uc
            ext_ref[pl.ds(CONV_HALO, tm), cols] = uc
        y = x1
        for c in range(2):
            gate, val = _conv_glu(ext_ref, cw_ref, cb_ref, tm, c)
            act = gate * jax.nn.sigmoid(gate) * val
            y = y + _mm(act, wd_ref[c * FF_CHUNK:(c + 1) * FF_CHUNK, :])
        ext_ref[pl.ds(0, CONV_HALO), :] = ext_ref[pl.ds(tm, CONV_HALO), :]
        err = y - tgt_ref[...]
        loss_ref[...] += jnp.full(loss_ref.shape, 0.5 * inv_d * jnp.sum(err * err), F32)
        dy = err * inv_d
        dy_ref[...] = dy
        dyb_ref[...] = dy.astype(MM)

    sd = jax.ShapeDtypeStruct
    return pl.pallas_call(
        body, name="ffn_fwd", grid=(s // tm,),
        out_shape=(sd((s, D_MODEL), MM), sd((s, 2 * D_FF), F32), sd((s, D_MODEL), F32), sd((s, D_MODEL), MM),
                   sd((8, LANES), F32)),
        in_specs=[_rows(tm, 1024), _resident((1, 1024)), _resident((2 * D_FF, D_MODEL)), _resident((3, 2 * D_FF)),
                  _resident((1, 2 * D_FF)), _resident((D_FF, D_MODEL)), _rows(tm, 1024)],
        out_specs=(_rows(tm, 1024), _rows(tm, 2 * D_FF), _rows(tm, 1024), _rows(tm, 1024),
                   pl.BlockSpec((8, LANES), lambda i: (0, 0))),
        scratch_shapes=[pltpu.VMEM((CONV_HALO + tm, 2 * D_FF), F32)], compiler_params=_params("arbitrary"),
    )(x1, ffn_norm, wup_t, conv_w, conv_b, wdown, target)


def _ffn_bwd_a(dyb, u, conv_w, conv_b, wdown, tm):
    s = dyb.shape[0]
    nt = s // tm
    hb = tm // CONV_HALO
    rev = lambda i: (nt - 1 - i, 0)

    def body(dy_ref, u_ref, halo_ref, cw_ref, cb_ref, wd_ref, du_ref, act_ref, dcw_ref, dcb_ref, ext_ref, extd_ref):
        i = pl.program_id(0)
        first_tile = i == nt - 1

        @pl.when(i == 0)
        def _():
            extd_ref[pl.ds(tm, CONV_HALO), :] = jnp.zeros((CONV_HALO, 2 * D_FF), F32)
            dcw_ref[...] = jnp.zeros_like(dcw_ref)
            dcb_ref[...] = jnp.zeros_like(dcb_ref)

        ext_ref[pl.ds(0, CONV_HALO), :] = jnp.where(first_tile, 0.0, halo_ref[...])
        ext_ref[pl.ds(CONV_HALO, tm), :] = u_ref[...]
        dy = dy_ref[...]
        for c in range(2):
            gate, val = _conv_glu(ext_ref, cw_ref, cb_ref, tm, c)
            sg = jax.nn.sigmoid(gate)
            sl = gate * sg
            act_ref[:, c * FF_CHUNK:(c + 1) * FF_CHUNK] = (sl * val).astype(MM)
            d_act = _mm_nt(dy, wd_ref[c * FF_CHUNK:(c + 1) * FF_CHUNK, :])
            extd_ref[pl.ds(0, tm), c * FF_CHUNK:(c + 1) * FF_CHUNK] = d_act * val * (sg * (1.0 + gate * (1.0 - sg)))
            extd_ref[pl.ds(0, tm), D_FF + c * FF_CHUNK:D_FF + (c + 1) * FF_CHUNK] = d_act * sl
        for c in range(4):
            cols = slice(c * FF_CHUNK, (c + 1) * FF_CHUNK)
            d0 = extd_ref[pl.ds(0, tm), cols]
            dcb_ref[:, cols] += jnp.sum(d0, axis=0, keepdims=True)
            for j in range(3):
                dcw_ref[j:j + 1, cols] += jnp.sum(d0 * ext_ref[pl.ds(CONV_HALO - 2 + j, tm), cols], axis=0,
                                                  keepdims=True)
            du = cw_ref[2:3, cols] * d0 + cw_ref[1:2, cols] * extd_ref[pl.ds(1, tm), cols]
            du = du + cw_ref[0:1, cols] * extd_ref[pl.ds(2, tm), cols]
            du_ref[:, cols] = du.astype(MM)
        extd_ref[pl.ds(tm, CONV_HALO), :] = extd_ref[pl.ds(0, CONV_HALO), :]

    sd = jax.ShapeDtypeStruct
    return pl.pallas_call(
        body, name="ffn_bwd_a", grid=(nt,),
        out_shape=(sd((s, 2 * D_FF), MM), sd((s, D_FF), MM), sd((3, 2 * D_FF), F32), sd((1, 2 * D_FF), F32)),
        in_specs=[pl.BlockSpec((tm, D_MODEL), rev), pl.BlockSpec((tm, 2 * D_FF), rev),
                  pl.BlockSpec((CONV_HALO, 2 * D_FF), lambda i: (jnp.maximum((nt - 1 - i) * hb - 1, 0), 0)),
                  _resident((3, 2 * D_FF)), _resident((1, 2 * D_FF)), _resident((D_FF, D_MODEL))],
        out_specs=(pl.BlockSpec((tm, 2 * D_FF), rev), pl.BlockSpec((tm, D_FF), rev),
                   pl.BlockSpec((3, 2 * D_FF), lambda i: (0, 0)), pl.BlockSpec((1, 2 * D_FF), lambda i: (0, 0))),
        scratch_shapes=[pltpu.VMEM((CONV_HALO + tm, 2 * D_FF), F32), pltpu.VMEM((tm + CONV_HALO, 2 * D_FF), F32)],
        compiler_params=_params("arbitrary"),
    )(dyb, u, u, conv_w, conv_b, wdown)


def _after(after):
    return ([], []) if after is None else ([after], [pl.BlockSpec(memory_space=pl.ANY)])


def _matmul_tn(a, b, tmo, tk, name, after=None):
    s, m = a.shape
    n = b.shape[1]
    nk = s // tk
    tie, tie_spec = _after(after)

    def body(a_ref, b_ref, *rest):
        o_ref = rest[-1]
        k = pl.program_id(1)

        @pl.when(k == 0)
        def _():
            o_ref[...] = jnp.zeros_like(o_ref)

        o_ref[...] += _mm_tn(a_ref[...], b_ref[pl.ds(pl.multiple_of(k * tk, tk), tk), :])

    return pl.pallas_call(
        body, name=name, grid=(m // tmo, nk), out_shape=jax.ShapeDtypeStruct((m, n), F32),
        in_specs=[pl.BlockSpec((tk, tmo), lambda i, k: (k, i)), _resident((s, n))] + tie_spec,
        out_specs=pl.BlockSpec((tmo, n), lambda i, k: (i, 0)), compiler_params=_params("parallel", "arbitrary"),
    )(a, b, *tie)


def _ffn_bwd_b(du, wup_t, x1, ffn_norm, dy, tm):
    s = du.shape[0]

    def body(du_ref, wu_ref, x1_ref, gn_ref, dy_ref, dx1_ref, dx1b_ref, dg_ref):
        @pl.when(pl.program_id(0) == 0)
        def _():
            dg_ref[...] = jnp.zeros_like(dg_ref)

        dh2 = _mm(du_ref[...], wu_ref[...])
        x1 = x1_ref[...]
        _, r = _rmsnorm_fwd(x1, gn_ref[...])
        dx, dgr = _rmsnorm_bwd(x1, r, gn_ref[...], dh2)
        dg_ref[...] += jnp.sum(dgr, axis=0, keepdims=True)
        dx1 = dy_ref[...] + dx
        dx1_ref[...] = dx1
        dx1b_ref[...] = dx1.astype(MM)

    return pl.pallas_call(
        body, name="ffn_bwd_b", grid=(s // tm,),
        out_shape=(jax.ShapeDtypeStruct((s, D_MODEL), F32), jax.ShapeDtypeStruct((s, D_MODEL), MM),
                   jax.ShapeDtypeStruct((1, D_MODEL), F32)),
        in_specs=[_rows(tm, 2 * D_FF), _resident((2 * D_FF, D_MODEL)), _rows(tm, 1024), _resident((1, 1024)),
                  _rows(tm, 1024)],
        out_specs=(_rows(tm, 1024), _rows(tm, 1024), pl.BlockSpec((1, D_MODEL), lambda i: (0, 0))),
        compiler_params=_params("arbitrary"),
    )(du, wup_t, x1, ffn_norm, dy)


def _mix_bwd(dx1b, wout, g, a, b, tm):
    s = dx1b.shape[0]

    def body(dx_ref, w_ref, g_ref, a_ref, b_ref, da_ref, db_ref, dzg_ref, dbg_ref):
        @pl.when(pl.program_id(0) == 0)
        def _():
            dbg_ref[...] = jnp.zeros_like(dbg_ref)

        dmix = _mm_nt(dx_ref[...], w_ref[...])
        for half, src, dst in ((0, a_ref, da_ref), (1, b_ref, db_ref)):
            cols = slice(half * 1024, (half + 1) * 1024)
            gt = g_ref[:, cols]
            dst[...] = dmix * gt
            dz = dmix * src[...] * gt * (1.0 - gt)
            dzg_ref[:, cols] = dz.astype(MM)
            dbg_ref[:, cols] += jnp.sum(dz, axis=0, keepdims=True)

    sd = jax.ShapeDtypeStruct
    return pl.pallas_call(
        body, name="mix_bwd", grid=(s // tm,),
        out_shape=(sd((s, 1024), F32), sd((s, 1024), F32), sd((s, 2048), MM), sd((1, 2048), F32)),
        in_specs=[_rows(tm, 1024), _resident((1024, 1024)), _rows(tm, 2048), _rows(tm, 1024), _rows(tm, 1024)],
        out_specs=(_rows(tm, 1024), _rows(tm, 1024), _rows(tm, 2048), pl.BlockSpec((1, 2048), lambda i: (0, 0))),
        compiler_params=_params("arbitrary"),
    )(dx1b, wout, g, a, b)


def _pool_bwd(u, da, wpool, pool_scale, tm, after=None):
    s = u.shape[0]
    nt = s // tm
    hb = tm // POOL_HALO

    tie, tie_spec = _after(after)

    def body(u_ref, uh_ref, da_ref, dah_ref, wp_ref, ps_ref, *rest):
        dzu_ref, dwp_ref, dps_ref, ext_ref, exte_ref = rest[-5:]
        i = pl.program_id(0)

        @pl.when(i == 0)
        def _():
            dwp_ref[...] = jnp.zeros_like(dwp_ref)
            dps_ref[...] = jnp.zeros_like(dps_ref)

        ext_ref[pl.ds(0, POOL_HALO), :] = jnp.where(i > 0, uh_ref[...], 0.0)
        ext_ref[pl.ds(POOL_HALO, tm), :] = u_ref[...]
        pooled = _pooled(ext_ref, tm, i * tm)
        da = da_ref[...]
        dah = jnp.where(i < nt - 1, dah_ref[...], 0.0)
        t = (i * tm + lax.broadcasted_iota(jnp.int32, (tm + POOL_HALO, 1), 0)).astype(F32)
        for gi, w in enumerate(POOL_WINDOWS):
            cols = slice(gi * POOL_GROUP, (gi + 1) * POOL_GROUP)
            pg = pooled[gi].astype(MM)
            wg = wp_ref[gi]
            mixed = _mm(pg, wg)
            dps_ref[:, cols] += jnp.sum(da[:, cols] * mixed, axis=0, keepdims=True)
            dmx = (da[:, cols] * ps_ref[:, cols]).astype(MM)
            dwp_ref[gi] += _mm_tn(pg, dmx)
            dpl = _mm_nt(dmx, wg)
            dplh = _mm_nt(dah[:, cols] * ps_ref[:, cols], wg)
            cnt = jnp.minimum(t + 1.0, float(w))
            exte_ref[pl.ds(0, tm), cols] = dpl / cnt[0:tm]
            exte_ref[pl.ds(tm, POOL_HALO), cols] = dplh / cnt[tm:tm + POOL_HALO]
            acc = exte_ref[pl.ds(0, tm), cols]
            for k in range(1, w):
                acc = acc + exte_ref[pl.ds(k, tm), cols]
            dzu_ref[:, cols] = (acc - dpl).astype(MM)

    sd = jax.ShapeDtypeStruct
    last_halo = s // POOL_HALO - 1
    return pl.pallas_call(
        body, name="pool_bwd", grid=(nt,),
        out_shape=(sd((s, 1024), MM), sd((4, POOL_GROUP, POOL_GROUP), F32), sd((1, 1024), F32)),
        in_specs=[_rows(tm, 1024), pl.BlockSpec((POOL_HALO, 1024), lambda i: (jnp.maximum(i * hb - 1, 0), 0)),
                  _rows(tm, 1024),
                  pl.BlockSpec((POOL_HALO, 1024), lambda i: (jnp.minimum((i + 1) * hb, last_halo), 0)),
                  _resident((4, POOL_GROUP, POOL_GROUP)), _resident((1, 1024))] + tie_spec,
        out_specs=(_rows(tm, 1024), pl.BlockSpec((4, POOL_GROUP, POOL_GROUP), lambda i: (0, 0, 0)),
                   pl.BlockSpec((1, 1024), lambda i: (0, 0))),
        scratch_shapes=[pltpu.VMEM((POOL_HALO + tm, 1024), F32), pltpu.VMEM((tm + POOL_HALO, 1024), F32)],
        compiler_params=_params("arbitrary"),
    )(u, u, da, da, wpool, pool_scale, *tie)


def _attn_bwd(q, kv, db, tabs, q_norm_t, k_norm_t, sink_rows, bd):
    s = q.shape[0]
    nb = s // BLOCK
    scale = HEAD_DIM ** -0.5
    cur = lambda n: (jnp.minimum(n, nb - 1), 0)
    prv = lambda n: (jnp.maximum(n - 1, 0), 0)

    def body(q_ref, kvc_ref, kvp_ref, db_ref, c_ref, s1_ref, s2_ref, cp_ref, s1p_ref, s2p_ref, qn_ref, kn_ref,
             sink_ref, bd_ref, dzq_ref, dzkv_ref, dqn_ref, dkn_ref, dsk_ref,
             carry_ref, tot_ref, dqr_ref, qacc_ref, kacc_ref, sacc_ref):
        n = pl.program_id(0)
        bdm = bd_ref[...]
        kn = kn_ref[...]

        @pl.when(n == 0)
        def _():
            carry_ref[...] = jnp.zeros_like(carry_ref)
            qacc_ref[...] = jnp.zeros_like(qacc_ref)
            kacc_ref[...] = jnp.zeros_like(kacc_ref)
            sacc_ref[...] = jnp.zeros_like(sacc_ref)

        kp_raw = kvp_ref[:, 0:128]
        kph, rp = _head_norm_fwd(kp_raw, kn, bdm)
        cp, s1p, s2p = cp_ref[...], s1p_ref[...], s2p_ref[...]

        @pl.when(n < nb)
        def _():
            c, s1, s2 = c_ref[...], s1_ref[...], s2_ref[...]
            c8, s18, s28 = _tile_lanes(c, 8), _tile_lanes(s1, 8), _tile_lanes(s2, 8)
            q_raw = q_ref[...]
            qh, rq = _head_norm_fwd(q_raw, qn_ref[...], bdm)
            qr = (_rope(qh, c8, s18, s28) * scale).astype(MM)
            kc, _ = _head_norm_fwd(kvc_ref[:, 0:128], kn, bdm)
            k2 = jnp.concatenate([_rope(kph, cp, s1p, s2p), _rope(kc, c, s1, s2)], axis=0)
            v2 = jnp.concatenate([kvp_ref[:, 128:256], kvc_ref[:, 128:256]], axis=0)
            dob = db_ref[...].astype(MM)
            mask = _attn_mask_t(n)
            lane = lax.broadcasted_iota(jnp.int32, (1, LANES), 1)
            dsk = jnp.zeros((1, LANES), F32)
            dk2 = jnp.zeros((2 * BLOCK, LANES), F32)
            dv2 = jnp.zeros((2 * BLOCK, LANES), F32)
            for hk in range(N_KV_HEADS):
                qs = _stack_pairs(qr, hk)
                do = _stack_pairs(dob, hk)
                dqt = jnp.zeros((LANES, PAIR_COLS), F32)
                dkb, dvb = [], []
                for par, (kb, vb) in enumerate(zip(_parity_bands(k2, hk), _parity_bands(v2, hk))):
                    sc = jnp.where(mask, _mm_nt(kb, qs), NEG)
                    pr, psink = _softmax_sink_t(sc, sink_ref[hk, par])
                    dp = _mm_nt(vb, do)
                    coldot = jnp.sum(pr * dp, axis=0, keepdims=True)
                    ds = (pr * (dp - coldot)).astype(MM)
                    dsr = -psink * coldot
                    for j in range(PAIRS):
                        h = hk * GQA_GROUP + 2 * j + par
                        dsk = dsk + jnp.where(lane == h, jnp.sum(dsr[:, j * BLOCK:(j + 1) * BLOCK]), 0.0)
                    dqt = dqt + _mm(kb.T, ds)
                    dkb.append(_mm(ds, qs))
                    dvb.append(_mm(pr, do))
                for j in range(PAIRS):
                    col = (PAIRS * hk + j) * LANES
                    dqr_ref[:, col:col + LANES] = dqt[:, j * BLOCK:(j + 1) * BLOCK].T
                dk2 = dk2 + _fold_parity(dkb[0], dkb[1], hk)
                dv2 = dv2 + _fold_parity(dvb[0], dvb[1], hk)
            tot_ref[:, 0:128] = carry_ref[:, 0:128] + dk2[0:BLOCK, :]
            tot_ref[:, 128:256] = carry_ref[:, 128:256] + dv2[0:BLOCK, :]
            carry_ref[:, 0:128] = dk2[BLOCK:2 * BLOCK, :]
            carry_ref[:, 128:256] = dv2[BLOCK:2 * BLOCK, :]
            sacc_ref[...] += dsk
            dqh = _rope_bwd(dqr_ref[...] * scale, c8, s18, s28)
            dq, dgq = _head_norm_bwd(q_raw, rq, qn_ref[...], dqh, bdm)
            dzq_ref[...] = dq.astype(MM)
            qacc_ref[...] += jnp.sum(dgq, axis=0, keepdims=True)

        @pl.when(n == nb)
        def _():
            tot_ref[...] = carry_ref[...]

        dkh = _rope_bwd(tot_ref[:, 0:128], cp, s1p, s2p)
        dkr, dgk = _head_norm_bwd(kp_raw, rp, kn, dkh, bdm)
        dzkv_ref[:, 0:128] = dkr.astype(MM)
        dzkv_ref[:, 128:256] = tot_ref[:, 128:256].astype(MM)
        kacc_ref[...] += jnp.where(n > 0, jnp.sum(dgk, axis=0, keepdims=True), 0.0)

        @pl.when(n == nb)
        def _():
            fold = qacc_ref[:, 0:HEAD_DIM]
            for h in range(1, N_Q_HEADS):
                fold = fold + qacc_ref[:, h * HEAD_DIM:(h + 1) * HEAD_DIM]
            dqn_ref[...] = fold
            dkn_ref[...] = kacc_ref[:, 0:HEAD_DIM] + kacc_ref[:, HEAD_DIM:2 * HEAD_DIM]
            dsk_ref[...] = sacc_ref[...]

    tab = lambda im: pl.BlockSpec((BLOCK, LANES), im)
    sd = jax.ShapeDtypeStruct
    const = lambda n: (0, 0)
    return pl.pallas_call(
        body, name="attn_bwd", grid=(nb + 1,),
        out_shape=(sd((s, 1024), MM), sd((s, 256), MM), sd((1, HEAD_DIM), F32), sd((1, HEAD_DIM), F32),
                   sd((1, LANES), F32)),
        in_specs=[pl.BlockSpec((BLOCK, 1024), cur), pl.BlockSpec((BLOCK, 256), cur), pl.BlockSpec((BLOCK, 256), prv),
                  pl.BlockSpec((BLOCK, 1024), cur), tab(cur), tab(cur), tab(cur), tab(prv), tab(prv), tab(prv),
                  _resident((1, 1024)), _resident((1, 128)), _resident((N_KV_HEADS, 2, 1, PAIR_COLS)),
                  _resident((LANES, LANES))],
        out_specs=(pl.BlockSpec((BLOCK, 1024), cur), pl.BlockSpec((BLOCK, 256), prv),
                   pl.BlockSpec((1, HEAD_DIM), const), pl.BlockSpec((1, HEAD_DIM), const),
                   pl.BlockSpec((1, LANES), const)),
        scratch_shapes=[pltpu.VMEM((BLOCK, 256), F32), pltpu.VMEM((BLOCK, 256), F32), pltpu.VMEM((BLOCK, 1024), F32),
                        pltpu.VMEM((1, 1024), F32), pltpu.VMEM((1, 128), F32), pltpu.VMEM((1, LANES), F32)],
        compiler_params=_params("arbitrary"),
    )(q, kv, kv, db, *tabs, *tabs, q_norm_t, k_norm_t, sink_rows, bd)


def _inproj_bwd(dzu, dzq, dzkv, dzg, win_t, x, attn_norm, dx1, tm, after=None):
    s = x.shape[0]
    tie, tie_spec = _after(after)

    def body(du_ref, dq_ref, dkv_ref, dg_ref, w_ref, x_ref, gn_ref, dx1_ref, *rest):
        gx_ref, dgn_ref = rest[-2:]

        @pl.when(pl.program_id(0) == 0)
        def _():
            dgn_ref[...] = jnp.zeros_like(dgn_ref)

        dh = _mm(du_ref[...], w_ref[O_U:O_Q, :]) + _mm(dq_ref[...], w_ref[O_Q:O_KV, :])
        dh = dh + _mm(dkv_ref[...], w_ref[O_KV:O_G, :]) + _mm(dg_ref[...], w_ref[O_G:IN_WIDTH, :])
        x = x_ref[...]
        _, r = _rmsnorm_fwd(x, gn_ref[...])
        dx, dgr = _rmsnorm_bwd(x, r, gn_ref[...], dh)
        dgn_ref[...] += jnp.sum(dgr, axis=0, keepdims=True)
        gx_ref[...] = dx1_ref[...] + dx

    return pl.pallas_call(
        body, name="inproj_bwd", grid=(s // tm,),
        out_shape=(jax.ShapeDtypeStruct((s, D_MODEL), F32), jax.ShapeDtypeStruct((1, D_MODEL), F32)),
        in_specs=[_rows(tm, 1024), _rows(tm, 1024), _rows(tm, 256), _rows(tm, 2048),
                  _resident((IN_WIDTH, D_MODEL)), _rows(tm, 1024), _resident((1, 1024)), _rows(tm, 1024)] + tie_spec,
        out_specs=(_rows(tm, 1024), pl.BlockSpec((1, D_MODEL), lambda i: (0, 0))),
        compiler_params=_params("arbitrary"),
    )(dzu, dzq, dzkv, dzg, win_t, x, attn_norm, dx1, *tie)


def _attention_constants(q_norm, k_norm, sinks):
    inv_freq = ROPE_THETA ** (-jnp.arange(0, ROPE_DIM, 2, dtype=F32) / ROPE_DIM)
    lane = jnp.arange(LANES) % HEAD_DIM
    invf = jnp.where(lane < ROPE_DIM, inv_freq[lane % (ROPE_DIM // 2)], 0.0).reshape(1, LANES).astype(F32)
    bd = (jnp.arange(LANES)[:, None] // HEAD_DIM == jnp.arange(LANES)[None, :] // HEAD_DIM).astype(MM)
    q_norm_t = jnp.tile(q_norm, (1, N_Q_HEADS))
    k_norm_t = jnp.tile(k_norm, (1, N_KV_HEADS))
    sink_rows = jnp.repeat(sinks.reshape(N_KV_HEADS, PAIRS, 2).transpose(0, 2, 1), BLOCK, axis=2)
    sink_rows = sink_rows.reshape(N_KV_HEADS, 2, 1, PAIR_COLS)
    return invf, bd, q_norm_t, k_norm_t, sink_rows


ANY = pl.BlockSpec(memory_space=pl.ANY)


def _position():
    return lax.axis_index("x"), lax.axis_index("y"), lax.axis_index("c")


def _all_gather(shards):
    k = len(shards)

    def body(*refs):
        ins, outs = refs[:k], refs[k:2 * k]
        send_sems, recv_sems, local_sems = refs[2 * k:]
        x, y, c = _position()
        me, sibling = (x, y, c), (x, y, 1 - c)
        chips = [(1 - x, y), (x, 1 - y), (1 - x, 1 - y)]

        def copy(a, kk, block, to, src=None):
            dst = outs[a].at[4 * block[0] + 2 * block[1] + block[2]]
            return pltpu.make_async_remote_copy(
                src_ref=dst if src is None else src, dst_ref=dst, send_sem=send_sems.at[a * 7 + kk],
                recv_sem=recv_sems.at[a * 7 + kk], device_id=to, device_id_type=MESH)

        mine = [pltpu.make_async_copy(ins[a], outs[a].at[4 * x + 2 * y + c], local_sems.at[a]) for a in range(k)]
        for cp in mine:
            cp.start()
        first = []
        for a in range(k):
            first.append(copy(a, 0, me, sibling, src=ins[a]))
            first += [copy(a, 1 + j, me, (*chip, c), src=ins[a]) for j, chip in enumerate(chips)]
        for cp in first:
            cp.start()
        passed = []
        for j, chip in enumerate(chips):
            for a in range(k):
                copy(a, 1 + j, (*chip, c), me).wait_recv()
                cp = copy(a, 4 + j, (*chip, c), sibling)
                cp.start()
                passed.append(cp)
        for a in range(k):
            copy(a, 0, sibling, me).wait_recv()
            for j, chip in enumerate(chips):
                copy(a, 4 + j, (*chip, 1 - c), me).wait_recv()
        for cp in first + passed:
            cp.wait_send()
        for cp in mine:
            cp.wait()

    return pl.pallas_call(
        body, name="all_gather_weights",
        out_shape=tuple(jax.ShapeDtypeStruct((N_DEV,) + s.shape, s.dtype) for s in shards),
        in_specs=[ANY] * k, out_specs=(ANY,) * k,
        scratch_shapes=[pltpu.SemaphoreType.DMA((7 * k,)), pltpu.SemaphoreType.DMA((7 * k,)),
                        pltpu.SemaphoreType.DMA((k,))],
    )(*shards)


HBM = pl.BlockSpec(memory_space=pltpu.HBM)
SEM = pl.BlockSpec(memory_space=pltpu.SEMAPHORE)
EFFECT = pltpu.SideEffectType.DATAFLOW_SIDE_EFFECTING


def _exchange_start(name, bufs, n_sems, copies):
    k = len(bufs)

    def body(*refs):
        for cp in copies(refs[:k], refs[k], refs[k + 1]):
            cp.start()
        refs[-1][...] = jnp.zeros_like(refs[-1])

    dma = pltpu.SemaphoreType.DMA((n_sems,))
    out = pl.pallas_call(
        body, name=name,
        out_shape=(dma, dma, *[pltpu.HBM(b.shape, b.dtype) for b in bufs], jax.ShapeDtypeStruct((8, LANES), F32)),
        in_specs=[HBM] * k, out_specs=(SEM, SEM, *[HBM] * k, pl.BlockSpec(memory_space=pltpu.VMEM)),
        input_output_aliases={i: 2 + i for i in range(k)},
        compiler_params=pltpu.CompilerParams(has_side_effects=EFFECT),
    )(*[pltpu.with_memory_space_constraint(b, pltpu.HBM) for b in bufs])
    return out[0], out[1], list(out[2:2 + k]), out[-1]


def _exchange_mid(name, bufs, sems_in, n_sems, waits, copies, after):
    k, ns = len(bufs), len(sems_in)

    def body(*refs):
        ins = refs[:k]
        waits(ins, *refs[k:k + ns])
        for cp in copies(ins, refs[k + ns + 1], refs[k + ns + 2]):
            cp.start()

    dma = pltpu.SemaphoreType.DMA((n_sems,))
    out = pl.pallas_call(
        body, name=name, out_shape=(dma, dma, *[pltpu.HBM(b.shape, b.dtype) for b in bufs]),
        in_specs=[HBM] * k + [SEM] * ns + [ANY], out_specs=(SEM, SEM, *[HBM] * k),
        input_output_aliases={i: 2 + i for i in range(k)},
        compiler_params=pltpu.CompilerParams(has_side_effects=EFFECT),
    )(*bufs, *sems_in, after)
    return out[0], out[1], list(out[2:])


def _exchange_wait(name, bufs, sems, waits, after=None):
    k, ns = len(bufs), len(sems)
    tie, tie_spec = _after(after)

    def body(*refs):
        waits(refs[:k], *refs[k:k + ns])

    out = pl.pallas_call(
        body, name=name, out_shape=tuple(pltpu.HBM(b.shape, b.dtype) for b in bufs),
        in_specs=[HBM] * k + [SEM] * ns + tie_spec, out_specs=(HBM,) * k,
        input_output_aliases={i: i for i in range(k)},
        compiler_params=pltpu.CompilerParams(has_side_effects=EFFECT),
    )(*bufs, *sems, *tie)
    return list(out)


def _gather_copies(k, direct):
    def copies(refs, send_sems, recv_sems):
        x, y, c = _position()
        chips = [(1 - x, y), (x, 1 - y), (1 - x, 1 - y)]
        out = []
        for a in range(k):
            land = refs[k + a]
            if direct:
                mine = land.at[4 * x + 2 * y + c]
                for kk, to in enumerate([(x, y, 1 - c)] + [(*chip, c) for chip in chips]):
                    out.append(pltpu.make_async_remote_copy(
                        src_ref=refs[a], dst_ref=mine, send_sem=send_sems.at[4 * a + kk],
                        recv_sem=recv_sems.at[4 * a + kk], device_id=to, device_id_type=MESH))
            else:
                for j, (px, py) in enumerate(chips):
                    slot = land.at[4 * px + 2 * py + c]
                    out.append(pltpu.make_async_remote_copy(
                        src_ref=slot, dst_ref=slot, send_sem=send_sems.at[3 * a + j], recv_sem=recv_sems.at[3 * a + j],
                        device_id=(x, y, 1 - c), device_id_type=MESH))
        return out
    return copies


def _all_gather_behind(shards, mid_after):
    k = len(shards)
    lands = [lax.empty((N_DEV,) + s.shape, s.dtype) for s in shards]
    direct, passed = _gather_copies(k, True), _gather_copies(k, False)

    def start_copies(refs, send_sems, recv_sems):
        return direct(refs, send_sems, recv_sems)

    send_a, recv_a, bufs, token = _exchange_start("gather_start", list(shards) + lands, 4 * k, start_copies)

    def finish():
        def wait_ici(refs, send_sems, recv_sems):
            for i, cp in enumerate(direct(refs, send_sems, recv_sems)):
                if i % 4:
                    cp.wait_recv()

        send_b, recv_b, bufs2 = _exchange_mid("gather_pass", bufs, [send_a, recv_a], 3 * k, wait_ici, passed,
                                              mid_after())

        def wait_all(refs, sa, ra, sb, rb):
            x, y, c = _position()
            for a in range(k):
                pltpu.sync_copy(refs[a], refs[k + a].at[4 * x + 2 * y + c])
            for i, cp in enumerate(direct(refs, sa, ra)):
                cp.wait_send()
                if i % 4 == 0:
                    cp.wait_recv()
            for cp in passed(refs, sb, rb):
                cp.wait()

        return _exchange_wait("gather_wait", bufs2, [send_a, recv_a, send_b, recv_b], wait_all)[k:]

    return token, finish


def _pair_copies(k):
    def copies(refs, send_sems, recv_sems):
        x, y, c = _position()
        return [pltpu.make_async_remote_copy(
            src_ref=refs[a].at[2 * ch + 1 - c], dst_ref=refs[k + a].at[ch], send_sem=send_sems.at[4 * a + ch],
            recv_sem=recv_sems.at[4 * a + ch], device_id=(x, y, 1 - c), device_id_type=MESH)
            for a in range(k) for ch in range(4)]
    return copies


def _chip_copies(k):
    def copies(refs, send_sems, recv_sems):
        x, y, c = _position()
        return [pltpu.make_async_remote_copy(
            src_ref=refs[a].at[2 * px + py], dst_ref=refs[k + a].at[rel], send_sem=send_sems.at[3 * a + rel],
            recv_sem=recv_sems.at[3 * a + rel], device_id=(px, py, c), device_id_type=MESH)
            for a in range(k) for rel, (px, py) in enumerate([(1 - x, y), (x, 1 - y), (1 - x, 1 - y)])]
    return copies


def _symmetric_exchange(name, srcs, n_land, copies_of):
    k = len(srcs)
    lands = [lax.empty((n_land,) + s.shape[1:], s.dtype) for s in srcs]
    copies = copies_of(k)
    send_sems, recv_sems, bufs, token = _exchange_start(name + "_start", list(srcs) + lands, n_land * k, copies)

    def finish(after):
        def wait_all(refs, ss, rs):
            for cp in copies(refs, ss, rs):
                cp.wait()

        done = _exchange_wait(name + "_wait", bufs, [send_sems, recv_sems], wait_all, after)
        return done[:k], done[k:]

    return token, finish


def _pair_add(full, recv, wire):
    _, r, c_ = full.shape

    def body(f_ref, r_ref, pw_ref, own_ref):
        ch = pl.program_id(0)
        x, y, c = _position()
        tot = f_ref[0, c] + r_ref[0]
        pw_ref[0] = tot.astype(pw_ref.dtype)

        @pl.when(ch == 2 * x + y)
        def _():
            own_ref[...] = tot

    return pl.pallas_call(
        body, name="grad_pair_add", grid=(4,),
        out_shape=(jax.ShapeDtypeStruct((4, r, c_), wire), jax.ShapeDtypeStruct((r, c_), F32)),
        in_specs=[pl.BlockSpec((1, 2, r, c_), lambda i: (i, 0, 0, 0)), pl.BlockSpec((1, r, c_), lambda i: (i, 0, 0))],
        out_specs=(pl.BlockSpec((1, r, c_), lambda i: (i, 0, 0)), pl.BlockSpec((r, c_), lambda i: (0, 0))),
        compiler_params=_params("arbitrary"),
    )(full.reshape(4, 2, r, c_), recv)


def _adamw_math(w, g, m, v):
    m = ADAM_B1 * m + (1.0 - ADAM_B1) * g
    v = ADAM_B2 * v + (1.0 - ADAM_B2) * (g * g)
    m_hat = m / (1.0 - ADAM_B1 ** ADAM_STEP)
    v_hat = v / (1.0 - ADAM_B2 ** ADAM_STEP)
    delta = -ADAM_LR * (m_hat / (jnp.sqrt(v_hat) + ADAM_EPS) + ADAM_WD * w)
    return delta, m, v


def _row_tile(r):
    for t in (256, 176, 128):
        if r % t == 0 and r > t:
            return t
    return r


def _sum_parts(own, recv):
    r, c_ = own.shape
    t = _row_tile(r)

    def body(o_ref, r_ref, g_ref):
        g = o_ref[...]
        for i in range(3):
            g = g + r_ref[i].astype(F32)
        g_ref[...] = g

    return pl.pallas_call(
        body, name="grad_sum", grid=(r // t,), out_shape=jax.ShapeDtypeStruct((r, c_), F32),
        in_specs=[pl.BlockSpec((t, c_), lambda i: (i, 0)), pl.BlockSpec((3, t, c_), lambda i: (0, i, 0))],
        out_specs=pl.BlockSpec((t, c_), lambda i: (i, 0)), compiler_params=_params("parallel"),
    )(own, recv)


def _adamw(g_own, recv, w, m, v):
    r, c_ = w.shape
    t = _row_tile(r)
    blk = pl.BlockSpec((t, c_), lambda i: (i, 0))

    def body(*refs):
        if recv is None:
            g_ref, w_ref, m_ref, v_ref, go_ref, d_ref, mo_ref, vo_ref = refs
            g = g_ref[...]
        else:
            g_ref, r_ref, w_ref, m_ref, v_ref, go_ref, d_ref, mo_ref, vo_ref = refs
            g = g_ref[...]
            for i in range(3):
                g = g + r_ref[i].astype(F32)
        go_ref[...] = g
        d_ref[...], mo_ref[...], vo_ref[...] = _adamw_math(w_ref[...], g, m_ref[...], v_ref[...])

    ins = [g_own] + ([] if recv is None else [recv]) + [w, m, v]
    specs = [blk] + ([] if recv is None else [pl.BlockSpec((3, t, c_), lambda i: (0, i, 0))]) + [blk] * 3
    return pl.pallas_call(
        body, name="adamw", grid=(r // t,), out_shape=(jax.ShapeDtypeStruct((r, c_), F32),) * 4,
        in_specs=specs, out_specs=(blk,) * 4, compiler_params=_params("parallel"),
    )(*ins)


SMALL_ROWS = 88


def _small_allreduce_adamw(gp, wp, mp, vp):
    def body(g_ref, w_ref, m_ref, v_ref, go_ref, d_ref, mo_ref, vo_ref, slots_ref, send_sems, recv_sems):
        x, y, c = _position()
        me = 4 * x + 2 * y + c
        slots_ref[me] = g_ref[...]
        cps = []
        for rel in range(1, N_DEV):
            fx, fy, fc = (rel >> 2) & 1, (rel >> 1) & 1, rel & 1
            to = (1 - x if fx else x, 1 - y if fy else y, 1 - c if fc else c)
            cps.append(pltpu.make_async_remote_copy(
                src_ref=g_ref, dst_ref=slots_ref.at[me], send_sem=send_sems.at[rel - 1],
                recv_sem=recv_sems.at[rel - 1], device_id=to, device_id_type=MESH))
        for cp in cps:
            cp.start()
        for cp in cps:
            cp.wait()
        g = slots_ref[0]
        for i in range(1, N_DEV):
            g = g + slots_ref[i]
        go_ref[...] = g
        d_ref[...], mo_ref[...], vo_ref[...] = _adamw_math(w_ref[...], g, m_ref[...], v_ref[...])

    vm = pl.BlockSpec(memory_space=pltpu.VMEM)
    return pl.pallas_call(
        body, name="small_allreduce_adamw", out_shape=(jax.ShapeDtypeStruct((SMALL_ROWS, LANES), F32),) * 4,
        in_specs=[vm] * 4, out_specs=(vm,) * 4,
        scratch_shapes=[pltpu.VMEM((N_DEV, SMALL_ROWS, LANES), F32), pltpu.SemaphoreType.DMA((N_DEV - 1,)),
                        pltpu.SemaphoreType.DMA((N_DEV - 1,))],
    )(gp, wp, mp, vp)


SMALL = ("attn_norm", "b_gate", "pool_scale", "q_norm", "k_norm", "sinks", "ffn_norm", "conv_b")
SHARDED = ("w_in", "w_pool", "w_out", "w_up", "conv_w", "w_down")
WEIGHTS = ("attn_norm", "w_in", "b_gate", "w_pool", "pool_scale", "q_norm", "k_norm", "sinks", "w_out", "ffn_norm",
           "w_up", "conv_w", "conv_b", "w_down")


N_SMALL = 10896


def _pack_small(d, loss=None):
    parts = [d[n].reshape(-1) for n in SMALL] + ([] if loss is None else [loss.reshape(1)])
    flat = jnp.concatenate(parts)
    return jnp.pad(flat, (0, SMALL_ROWS * LANES - flat.shape[0])).reshape(SMALL_ROWS, LANES)


def _unpack_small(p, like):
    flat, out, o = p.reshape(-1), {}, 0
    for n in SMALL:
        out[n] = flat[o:o + like[n].size].reshape(like[n].shape)
        o += like[n].size
    return out


def kernel(x, positions, attn_norm, w_in, b_gate, w_pool, pool_scale, q_norm, k_norm, sinks, w_out, ffn_norm, w_up, conv_w, conv_b, w_down, loss_target, m_attn_norm, m_w_in, m_b_gate, m_w_pool, m_pool_scale, m_q_norm, m_k_norm, m_sinks, m_w_out, m_ffn_norm, m_w_up, m_conv_w, m_conv_b, m_w_down, v_attn_norm, v_w_in, v_b_gate, v_w_pool, v_pool_scale, v_q_norm, v_k_norm, v_sinks, v_w_out, v_ffn_norm, v_w_up, v_conv_w, v_conv_b, v_w_down):
    w = dict(attn_norm=attn_norm, w_in=w_in, b_gate=b_gate, w_pool=w_pool, pool_scale=pool_scale, q_norm=q_norm,
             k_norm=k_norm, sinks=sinks, w_out=w_out, ffn_norm=ffn_norm, w_up=w_up, conv_w=conv_w, conv_b=conv_b,
             w_down=w_down)
    m = dict(attn_norm=m_attn_norm, w_in=m_w_in, b_gate=m_b_gate, w_pool=m_w_pool, pool_scale=m_pool_scale,
             q_norm=m_q_norm, k_norm=m_k_norm, sinks=m_sinks, w_out=m_w_out, ffn_norm=m_ffn_norm, w_up=m_w_up,
             conv_w=m_conv_w, conv_b=m_conv_b, w_down=m_w_down)
    v = dict(attn_norm=v_attn_norm, w_in=v_w_in, b_gate=v_b_gate, w_pool=v_w_pool, pool_scale=v_pool_scale,
             q_norm=v_q_norm, k_norm=v_k_norm, sinks=v_sinks, w_out=v_w_out, ffn_norm=v_ffn_norm, w_up=v_w_up,
             conv_w=v_conv_w, conv_b=v_conv_b, w_down=v_w_down)
    seq = x.shape[1]
    tm = 256
    tk = min(seq, 1024)
    xs, target, pos_col = x[0], loss_target[0], positions.reshape(seq, 1)
    invf, bd, q_norm_t, k_norm_t, sink_rows = _attention_constants(q_norm, k_norm, sinks)
    out = {}
    nat = {"w_in": (D_MODEL, 544), "w_pool": (128, POOL_GROUP), "w_out": (128, D_MODEL), "w_up": (D_MODEL, 704),
           "conv_w": (3, 704), "w_down": (352, D_MODEL)}

    def update(names, owns, recvs):
        for name, own, recv in zip(names, owns, recvs):
            w2, m2, v2 = (t[name].reshape(nat[name]) for t in (w, m, v))
            if name in ("w_in", "w_up"):
                res = _adamw(_sum_parts(own, recv).T, None, w2, m2, v2)
            else:
                res = _adamw(own, recv, w2, m2, v2)
            out[name] = [t.reshape(w[name].shape) for t in res]

    (g_win,) = _all_gather([w_in[0].T.astype(MM)])
    win_t = g_win.reshape(IN_WIDTH, D_MODEL)
    fwd = {}
    token, gather_rest = _all_gather_behind(
        [w_pool[0].astype(MM).reshape(128, POOL_GROUP), w_out[0].astype(MM), w_up[0].T.astype(MM), conv_w[0],
         w_down[0].astype(MM)], lambda: fwd["b"])

    tabs = _rope_tables(pos_col, invf)
    h, u, q, kv, g = _inproj_fwd(xs, attn_norm + token[0:1, 0:1], win_t, b_gate, tm)
    fwd["b"] = b = _attn_fwd(q, kv, tabs, q_norm_t, k_norm_t, sink_rows, bd)
    g_wpool, g_wout, g_wup, g_convw, g_wdown = gather_rest()
    wpool = g_wpool.reshape(N_DEV, 4, 32, POOL_GROUP).transpose(1, 0, 2, 3).reshape(4, POOL_GROUP, POOL_GROUP)
    wout = g_wout.reshape(D_MODEL, D_MODEL)
    wup_t = g_wup.reshape(2 * D_FF, D_MODEL)
    convw = g_convw.transpose(1, 0, 2).reshape(3, 2 * D_FF)
    wdown = g_wdown.reshape(D_FF, D_MODEL)
    a = _pool_fwd(u, wpool, pool_scale, tm)
    x1, mix = _mix_out_fwd(xs, g, a, b, wout, tm)
    h2, uff, dy, dyb, lossp = _ffn_fwd(x1, ffn_norm, wup_t, convw, conv_b, wdown, target, tm)

    du, act, d_conv_w, d_conv_b = _ffn_bwd_a(dyb, uff, convw, conv_b, wdown, tm)
    d_wdown = _matmul_tn(act, dyb, FF_CHUNK, tk, "dw_down")
    dx1, dx1b, d_ffn_norm = _ffn_bwd_b(du, wup_t, x1, ffn_norm, dy, tm)
    d_wup_t = _matmul_tn(du, h2, 512, tk, "dw_up")
    late = ("w_down", "w_up", "conv_w")
    late_wire = (WIRE, WIRE, F32)
    late_full = [d_wdown.reshape(N_DEV, 352, D_MODEL), d_wup_t.reshape(N_DEV, 704, D_MODEL),
                 d_conv_w.reshape(3, N_DEV, 704).transpose(1, 0, 2)]
    token, late_pair = _symmetric_exchange("late_pair", late_full, 4, _pair_copies)
    d_wout = _matmul_tn(mix, dx1b, 512, tk, "dw_out", after=token)
    da, db, dzg, d_b_gate = _mix_bwd(dx1b, wout, g, a, b, tm)
    late_pw, late_own = zip(*[_pair_add(f, r, wd) for f, r, wd in zip(*late_pair(dzg), late_wire)])
    token, late_chip = _symmetric_exchange("late_chip", list(late_pw), 3, _chip_copies)
    dzu, d_wpool, d_pool_scale = _pool_bwd(u, da, wpool, pool_scale, tm, after=token)
    dzq, dzkv, d_q_norm, d_k_norm, d_sinks = _attn_bwd(q, kv, db, tabs, q_norm_t, k_norm_t, sink_rows, bd)
    d_win_t = jnp.concatenate([_matmul_tn(dzu, h, 512, tk, "dw_in_u"), _matmul_tn(dzq, h, 512, tk, "dw_in_q"),
                               _matmul_tn(dzkv, h, 256, tk, "dw_in_kv"), _matmul_tn(dzg, h, 512, tk, "dw_in_g")],
                              axis=0)
    early = ("w_in", "w_pool", "w_out")
    early_full = [d_win_t.reshape(N_DEV, 544, D_MODEL),
                  d_wpool.reshape(4, N_DEV, 32, POOL_GROUP).transpose(1, 0, 2, 3).reshape(N_DEV, 128, POOL_GROUP),
                  d_wout.reshape(N_DEV, 128, D_MODEL)]
    token, early_pair = _symmetric_exchange("early_pair", early_full, 4, _pair_copies)
    grad_x, d_attn_norm = _inproj_bwd(dzu, dzq, dzkv, dzg, win_t, xs, attn_norm, dx1, tm, after=token)
    early_pw, early_own = zip(*[_pair_add(f, r, WIRE) for f, r in zip(*early_pair(grad_x))])
    token, early_chip = _symmetric_exchange("early_chip", list(early_pw), 3, _chip_copies)

    update(late, late_own, late_chip(token)[1])
    gr = dict(attn_norm=d_attn_norm, b_gate=d_b_gate, pool_scale=d_pool_scale, q_norm=d_q_norm, k_norm=d_k_norm,
              sinks=d_sinks[:, 0:N_Q_HEADS], ffn_norm=d_ffn_norm, conv_b=d_conv_b)
    small = _small_allreduce_adamw(_pack_small(gr, lossp[0, 0]), _pack_small(w), _pack_small(m), _pack_small(v))
    loss = small[0].reshape(-1)[N_SMALL]
    unpacked = [_unpack_small(p, w) for p in small]
    for name in SMALL:
        out[name] = [t[name] for t in unpacked]
    update(early, early_own, early_chip(small[1])[1])

    return (loss, grad_x[None], *[out[n][0] for n in WEIGHTS], *[out[n][1] for n in WEIGHTS],
            *[out[n][2] for n in WEIGHTS], *[out[n][3] for n in WEIGHTS])
```

```python
import functools

import jax
import jax.numpy as jnp
from jax import lax
from jax.experimental import pallas as pl
from jax.experimental.pallas import tpu as pltpu

F32 = jnp.float32
MM = jnp.bfloat16
WIRE = jnp.bfloat16

D_MODEL = 1024
D_FF = 2816
HEAD_DIM = 64
N_Q_HEADS = 16
N_KV_HEADS = 2
GQA_GROUP = 8
BLOCK = 128
ROPE_DIM = 16
ROPE_THETA = 500000.0
POOL_WINDOWS = (2, 4, 8, 16)
POOL_GROUP = 256
POOL_HALO = 16
CONV_HALO = 8
EPS = 1e-6
NEG = -1e30
O_U, O_Q, O_KV, O_G, IN_WIDTH = 0, 1024, 2048, 2304, 4352
FF_CHUNK = 1408

ADAM_LR, ADAM_B1, ADAM_B2, ADAM_EPS, ADAM_WD, ADAM_STEP = 0.001, 0.9, 0.999, 1e-08, 0.01, 10

N_DEV = 8
LANES = 128
VMEM_LIMIT_BYTES = 56 * 1024 * 1024
MESH = pl.DeviceIdType.MESH


def _params(*sem):
    return pltpu.CompilerParams(dimension_semantics=sem, vmem_limit_bytes=VMEM_LIMIT_BYTES)


def _resident(shape):
    nd = len(shape)
    return pl.BlockSpec(shape, lambda *_: (0,) * nd, pipeline_mode=pl.Buffered(1))


def _rows(tm, width):
    return pl.BlockSpec((tm, width), lambda i: (i, 0))


def _mm(a, b):
    return jnp.dot(a.astype(MM), b.astype(MM), preferred_element_type=F32)


def _mm_nt(a, b):
    return lax.dot_general(a.astype(MM), b.astype(MM), (((1,), (1,)), ((), ())), preferred_element_type=F32)


def _mm_tn(a, b):
    return lax.dot_general(a.astype(MM), b.astype(MM), (((0,), (0,)), ((), ())), preferred_element_type=F32)


def _rmsnorm_fwd(x, g):
    r = lax.rsqrt(jnp.mean(x * x, axis=-1, keepdims=True) + EPS)
    return x * r * g, r


def _rmsnorm_bwd(x, r, g, dy):
    xn = x * r
    dxn = dy * g
    dx = r * (dxn - xn * jnp.mean(dxn * xn, axis=-1, keepdims=True))
    return dx, dy * xn


def _group_sum64(v, bd):
    hi = v.astype(MM)
    lo = (v - hi.astype(F32)).astype(MM)
    outs = []
    for t in range(v.shape[1] // LANES):
        sl = slice(LANES * t, LANES * (t + 1))
        outs.append(jnp.dot(hi[:, sl], bd, preferred_element_type=F32)
                    + jnp.dot(lo[:, sl], bd, preferred_element_type=F32))
    return outs[0] if len(outs) == 1 else jnp.concatenate(outs, axis=1)


def _head_norm_fwd(x, g, bd):
    r = lax.rsqrt(_group_sum64(x * x, bd) * (1.0 / HEAD_DIM) + EPS)
    return x * r * g, r


def _head_norm_bwd(x, r, g, dy, bd):
    xn = x * r
    dxn = dy * g
    dx = r * (dxn - xn * (_group_sum64(dxn * xn, bd) * (1.0 / HEAD_DIM)))
    return dx, dy * xn


def _rope(x, c, s1, s2):
    w = x.shape[1]
    return x * c + pltpu.roll(x, w - ROPE_DIM // 2, 1) * s1 + pltpu.roll(x, ROPE_DIM // 2, 1) * s2


def _rope_bwd(dy, c, s1, s2):
    w = dy.shape[1]
    return dy * c + pltpu.roll(dy * s1, ROPE_DIM // 2, 1) + pltpu.roll(dy * s2, w - ROPE_DIM // 2, 1)


def _tile_lanes(t, reps):
    return t if reps == 1 else jnp.concatenate([t] * reps, axis=1)


def _rope_tables(pos_col, invf):
    s = pos_col.shape[0]
    tm = min(s, 1024)

    def body(pos_ref, invf_ref, c_ref, s1_ref, s2_ref):
        ang = pos_ref[...].astype(F32) * invf_ref[...]
        lane = lax.broadcasted_iota(jnp.int32, ang.shape, 1) % HEAD_DIM
        sn = jnp.sin(ang)
        c_ref[...] = jnp.cos(ang)
        s1_ref[...] = jnp.where(lane < ROPE_DIM // 2, -sn, 0.0)
        s2_ref[...] = jnp.where((lane >= ROPE_DIM // 2) & (lane < ROPE_DIM), sn, 0.0)

    out = jax.ShapeDtypeStruct((s, LANES), F32)
    return pl.pallas_call(
        body, name="rope_tables", grid=(s // tm,), out_shape=(out, out, out),
        in_specs=[_rows(tm, 1), _resident((1, LANES))],
        out_specs=(_rows(tm, LANES),) * 3, compiler_params=_params("parallel"),
    )(pos_col, invf)


def _inproj_fwd(x, attn_norm, win_t, b_gate, tm):
    s = x.shape[0]

    def body(x_ref, gn_ref, w_ref, bg_ref, h_ref, u_ref, q_ref, kv_ref, g_ref):
        h, _ = _rmsnorm_fwd(x_ref[...], gn_ref[...])
        h = h.astype(MM)
        h_ref[...] = h
        u_ref[...] = _mm_nt(h, w_ref[O_U:O_Q, :])
        q_ref[...] = _mm_nt(h, w_ref[O_Q:O_KV, :])
        kv_ref[...] = _mm_nt(h, w_ref[O_KV:O_G, :])
        g_ref[...] = jax.nn.sigmoid(_mm_nt(h, w_ref[O_G:IN_WIDTH, :]) + bg_ref[...])

    sd = jax.ShapeDtypeStruct
    return pl.pallas_call(
        body, name="inproj_fwd", grid=(s // tm,),
        out_shape=(sd((s, D_MODEL), MM), sd((s, 1024), F32), sd((s, 1024), F32), sd((s, 256), F32),
                   sd((s, 2048), F32)),
        in_specs=[_rows(tm, D_MODEL), _resident((1, D_MODEL)), _resident((IN_WIDTH, D_MODEL)), _resident((1, 2048))],
        out_specs=(_rows(tm, D_MODEL), _rows(tm, 1024), _rows(tm, 1024), _rows(tm, 256), _rows(tm, 2048)),
        compiler_params=_params("parallel"),
    )(x, attn_norm, win_t, b_gate)


def _pooled(ext_ref, tm, row0):
    t = (row0 + lax.broadcasted_iota(jnp.int32, (tm, 1), 0)).astype(F32)
    out = []
    for gi, w in enumerate(POOL_WINDOWS):
        cols = slice(gi * POOL_GROUP, (gi + 1) * POOL_GROUP)
        acc = ext_ref[pl.ds(POOL_HALO, tm), cols]
        for k in range(1, w):
            acc = acc + ext_ref[pl.ds(POOL_HALO - k, tm), cols]
        cnt = jnp.minimum(t + 1.0, float(w))
        out.append(acc / cnt - ext_ref[pl.ds(POOL_HALO, tm), cols])
    return out


def _pool_fwd(u, wpool, pool_scale, tm):
    s = u.shape[0]
    hb = tm // POOL_HALO

    def body(u_ref, halo_ref, wp_ref, ps_ref, a_ref, ext_ref):
        i = pl.program_id(0)
        ext_ref[pl.ds(0, POOL_HALO), :] = jnp.where(i > 0, halo_ref[...], 0.0)
        ext_ref[pl.ds(POOL_HALO, tm), :] = u_ref[...]
        pooled = _pooled(ext_ref, tm, i * tm)
        for gi in range(4):
            cols = slice(gi * POOL_GROUP, (gi + 1) * POOL_GROUP)
            a_ref[:, cols] = _mm(pooled[gi], wp_ref[gi]) * ps_ref[:, cols]

    return pl.pallas_call(
        body, name="pool_fwd", grid=(s // tm,), out_shape=jax.ShapeDtypeStruct((s, 1024), F32),
        in_specs=[_rows(tm, 1024), pl.BlockSpec((POOL_HALO, 1024), lambda i: (jnp.maximum(i * hb - 1, 0), 0)),
                  _resident((4, POOL_GROUP, POOL_GROUP)), _resident((1, 1024))],
        out_specs=_rows(tm, 1024), scratch_shapes=[pltpu.VMEM((POOL_HALO + tm, 1024), F32)],
        compiler_params=_params("parallel"),
    )(u, u, wpool, pool_scale)


PAIRS = GQA_GROUP // 2
PAIR_COLS = PAIRS * BLOCK


def _attn_mask_t(n):
    shape = (2 * BLOCK, PAIR_COLS)
    kj = lax.broadcasted_iota(jnp.int32, shape, 0)
    qi = lax.broadcasted_iota(jnp.int32, shape, 1) % BLOCK
    return (kj > qi) & (kj <= qi + BLOCK) & ((n > 0) | (kj >= BLOCK))


def _stack_pairs(x, hk):
    return jnp.concatenate([x[:, (PAIRS * hk + j) * LANES:(PAIRS * hk + j + 1) * LANES] for j in range(PAIRS)], axis=0)


def _parity_bands(t, hk):
    low = lax.broadcasted_iota(jnp.int32, t.shape, 1) < HEAD_DIM
    own = jnp.where(low if hk == 0 else ~low, t, 0.0)
    other = pltpu.roll(own, HEAD_DIM, 1)
    return (own, other) if hk == 0 else (other, own)


def _fold_parity(even, odd, hk):
    low = lax.broadcasted_iota(jnp.int32, even.shape, 1) < HEAD_DIM
    comb = jnp.where(low, even, odd)
    comb = comb + pltpu.roll(comb, HEAD_DIM, 1)
    return jnp.where(low if hk == 0 else ~low, comb, 0.0)


def _softmax_sink_t(s, sink):
    m = jnp.maximum(jnp.max(s, axis=0, keepdims=True), sink)
    p = jnp.exp(s - m)
    es = jnp.exp(sink - m)
    inv = 1.0 / (jnp.sum(p, axis=0, keepdims=True) + es)
    return p * inv, es * inv


def _attn_fwd(q, kv, tabs, q_norm_t, k_norm_t, sink_rows, bd):
    s = q.shape[0]
    nb = s // BLOCK
    scale = HEAD_DIM ** -0.5
    cur = lambda n: (n, 0)
    prv = lambda n: (jnp.maximum(n - 1, 0), 0)

    def body(q_ref, kvc_ref, kvp_ref, c_ref, s1_ref, s2_ref, cp_ref, s1p_ref, s2p_ref, qn_ref, kn_ref, sink_ref,
             bd_ref, o_ref):
        n = pl.program_id(0)
        bdm = bd_ref[...]
        c, s1, s2 = c_ref[...], s1_ref[...], s2_ref[...]
        qh, _ = _head_norm_fwd(q_ref[...], qn_ref[...], bdm)
        qr = (_rope(qh, _tile_lanes(c, 8), _tile_lanes(s1, 8), _tile_lanes(s2, 8)) * scale).astype(MM)
        kc, _ = _head_norm_fwd(kvc_ref[:, 0:128], kn_ref[...], bdm)
        kp, _ = _head_norm_fwd(kvp_ref[:, 0:128], kn_ref[...], bdm)
        k2 = jnp.concatenate([_rope(kp, cp_ref[...], s1p_ref[...], s2p_ref[...]), _rope(kc, c, s1, s2)], axis=0)
        v2 = jnp.concatenate([kvp_ref[:, 128:256], kvc_ref[:, 128:256]], axis=0)
        mask = _attn_mask_t(n)
        for hk in range(N_KV_HEADS):
            qs = _stack_pairs(qr, hk)
            ot = jnp.zeros((LANES, PAIR_COLS), F32)
            for par, (kb, vb) in enumerate(zip(_parity_bands(k2, hk), _parity_bands(v2, hk))):
                sc = jnp.where(mask, _mm_nt(kb, qs), NEG)
                pr, _ = _softmax_sink_t(sc, sink_ref[hk, par])
                ot = ot + _mm(vb.T, pr)
            for j in range(PAIRS):
                col = (PAIRS * hk + j) * LANES
                o_ref[:, col:col + LANES] = ot[:, j * BLOCK:(j + 1) * BLOCK].T

    tab = lambda im: pl.BlockSpec((BLOCK, LANES), im)
    return pl.pallas_call(
        body, name="attn_fwd", grid=(nb,), out_shape=jax.ShapeDtypeStruct((s, 1024), F32),
        in_specs=[pl.BlockSpec((BLOCK, 1024), cur), pl.BlockSpec((BLOCK, 256), cur), pl.BlockSpec((BLOCK, 256), prv),
                  tab(cur), tab(cur), tab(cur), tab(prv), tab(prv), tab(prv),
                  _resident((1, 1024)), _resident((1, 128)), _resident((N_KV_HEADS, 2, 1, PAIR_COLS)),
                  _resident((LANES, LANES))],
        out_specs=pl.BlockSpec((BLOCK, 1024), cur), compiler_params=_params("parallel"),
    )(q, kv, kv, *tabs, *tabs, q_norm_t, k_norm_t, sink_rows, bd)


def _mix_out_fwd(x, g, a, b, wout, tm):
    s = x.shape[0]

    def body(x_ref, g_ref, a_ref, b_ref, w_ref, x1_ref, mix_ref):
        mix = (g_ref[:, 0:1024] * a_ref[...] + g_ref[:, 1024:2048] * b_ref[...]).astype(MM)
        mix_ref[...] = mix
        x1_ref[...] = x_ref[...] + _mm(mix, w_ref[...])

    return pl.pallas_call(
        body, name="mix_out_fwd", grid=(s // tm,),
        out_shape=(jax.ShapeDtypeStruct((s, D_MODEL), F32), jax.ShapeDtypeStruct((s, D_MODEL), MM)),
        in_specs=[_rows(tm, 1024), _rows(tm, 2048), _rows(tm, 1024), _rows(tm, 1024), _resident((1024, 1024))],
        out_specs=(_rows(tm, 1024), _rows(tm, 1024)), compiler_params=_params("parallel"),
    )(x, g, a, b, wout)


def _conv_glu(ext_ref, cw_ref, cb_ref, tm, c):
    out = []
    for base in (c * FF_CHUNK, D_FF + c * FF_CHUNK):
        cols = slice(base, base + FF_CHUNK)
        y = cb_ref[:, cols] + cw_ref[0:1, cols] * ext_ref[pl.ds(CONV_HALO - 2, tm), cols]
        y = y + cw_ref[1:2, cols] * ext_ref[pl.ds(CONV_HALO - 1, tm), cols]
        y = y + cw_ref[2:3, cols] * ext_ref[pl.ds(CONV_HALO, tm), cols]
        out.append(y)
    return out


def _ffn_fwd(x1, ffn_norm, wup_t, conv_w, conv_b, wdown, target, tm):
    s = x1.shape[0]
    inv_d = 1.0 / D_MODEL

    def body(x1_ref, gn_ref, wu_ref, cw_ref, cb_ref, wd_ref, tgt_ref, h2_ref, u_ref, dy_ref, dyb_ref, loss_ref,
             ext_ref):
        i = pl.program_id(0)

        @pl.when(i == 0)
        def _():
            ext_ref[pl.ds(0, CONV_HALO), :] = jnp.zeros((CONV_HALO, 2 * D_FF), F32)
            loss_ref[...] = jnp.zeros_like(loss_ref)

        x1 = x1_ref[...]
        h2, _ = _rmsnorm_fwd(x1, gn_ref[...])
        h2 = h2.astype(MM)
        h2_ref[...] = h2
        for c in range(4):
            cols = slice(c * FF_CHUNK, (c + 1) * FF_CHUNK)
            uc = _mm_nt(h2, wu_ref[cols, :])
            u_ref[:, cols] = uc
            ext_ref[pl.ds(CONV_HALO, tm), cols] = uc
        y = x1
        for c in range(2):
            gate, val = _conv_glu(ext_ref, cw_ref, cb_ref, tm, c)
            act = gate * jax.nn.sigmoid(gate) * val
            y = y + _mm(act, wd_ref[c * FF_CHUNK:(c + 1) * FF_CHUNK, :])
        ext_ref[pl.ds(0, CONV_HALO), :] = ext_ref[pl.ds(tm, CONV_HALO), :]
        err = y - tgt_ref[...]
        loss_ref[...] += jnp.full(loss_ref.shape, 0.5 * inv_d * jnp.sum(err * err), F32)
        dy = err * inv_d
        dy_ref[...] = dy
        dyb_ref[...] = dy.astype(MM)

    sd = jax.ShapeDtypeStruct
    return pl.pallas_call(
        body, name="ffn_fwd", grid=(s // tm,),
        out_shape=(sd((s, D_MODEL), MM), sd((s, 2 * D_FF), F32), sd((s, D_MODEL), F32), sd((s, D_MODEL), MM),
                   sd((8, LANES), F32)),
        in_specs=[_rows(tm, 1024), _resident((1, 1024)), _resident((2 * D_FF, D_MODEL)), _resident((3, 2 * D_FF)),
                  _resident((1, 2 * D_FF)), _resident((D_FF, D_MODEL)), _rows(tm, 1024)],
        out_specs=(_rows(tm, 1024), _rows(tm, 2 * D_FF), _rows(tm, 1024), _rows(tm, 1024),
                   pl.BlockSpec((8, LANES), lambda i: (0, 0))),
        scratch_shapes=[pltpu.VMEM((CONV_HALO + tm, 2 * D_FF), F32)], compiler_params=_params("arbitrary"),
    )(x1, ffn_norm, wup_t, conv_w, conv_b, wdown, target)


def _ffn_bwd_a(dyb, u, conv_w, conv_b, wdown, tm):
    s = dyb.shape[0]
    nt = s // tm
    hb = tm // CONV_HALO
    rev = lambda i: (nt - 1 - i, 0)

    def body(dy_ref, u_ref, halo_ref, cw_ref, cb_ref, wd_ref, du_ref, act_ref, dcw_ref, dcb_ref, ext_ref, extd_ref):
        i = pl.program_id(0)
        first_tile = i == nt - 1

        @pl.when(i == 0)
        def _():
            extd_ref[pl.ds(tm, CONV_HALO), :] = jnp.zeros((CONV_HALO, 2 * D_FF), F32)
            dcw_ref[...] = jnp.zeros_like(dcw_ref)
            dcb_ref[...] = jnp.zeros_like(dcb_ref)

        ext_ref[pl.ds(0, CONV_HALO), :] = jnp.where(first_tile, 0.0, halo_ref[...])
        ext_ref[pl.ds(CONV_HALO, tm), :] = u_ref[...]
        dy = dy_ref[...]
        for c in range(2):
            gate, val = _conv_glu(ext_ref, cw_ref, cb_ref, tm, c)
            sg = jax.nn.sigmoid(gate)
            sl = gate * sg
            act_ref[:, c * FF_CHUNK:(c + 1) * FF_CHUNK] = (sl * val).astype(MM)
            d_act = _mm_nt(dy, wd_ref[c * FF_CHUNK:(c + 1) * FF_CHUNK, :])
            extd_ref[pl.ds(0, tm), c * FF_CHUNK:(c + 1) * FF_CHUNK] = d_act * val * (sg * (1.0 + gate * (1.0 - sg)))
            extd_ref[pl.ds(0, tm), D_FF + c * FF_CHUNK:D_FF + (c + 1) * FF_CHUNK] = d_act * sl
        for c in range(4):
            cols = slice(c * FF_CHUNK, (c + 1) * FF_CHUNK)
            d0 = extd_ref[pl.ds(0, tm), cols]
            dcb_ref[:, cols] += jnp.sum(d0, axis=0, keepdims=True)
            for j in range(3):
                dcw_ref[j:j + 1, cols] += jnp.sum(d0 * ext_ref[pl.ds(CONV_HALO - 2 + j, tm), cols], axis=0,
                                                  keepdims=True)
            du = cw_ref[2:3, cols] * d0 + cw_ref[1:2, cols] * extd_ref[pl.ds(1, tm), cols]
            du = du + cw_ref[0:1, cols] * extd_ref[pl.ds(2, tm), cols]
            du_ref[:, cols] = du.astype(MM)
        extd_ref[pl.ds(tm, CONV_HALO), :] = extd_ref[pl.ds(0, CONV_HALO), :]

    sd = jax.ShapeDtypeStruct
    return pl.pallas_call(
        body, name="ffn_bwd_a", grid=(nt,),
        out_shape=(sd((s, 2 * D_FF), MM), sd((s, D_FF), MM), sd((3, 2 * D_FF), F32), sd((1, 2 * D_FF), F32)),
        in_specs=[pl.BlockSpec((tm, D_MODEL), rev), pl.BlockSpec((tm, 2 * D_FF), rev),
                  pl.BlockSpec((CONV_HALO, 2 * D_FF), lambda i: (jnp.maximum((nt - 1 - i) * hb - 1, 0), 0)),
                  _resident((3, 2 * D_FF)), _resident((1, 2 * D_FF)), _resident((D_FF, D_MODEL))],
        out_specs=(pl.BlockSpec((tm, 2 * D_FF), rev), pl.BlockSpec((tm, D_FF), rev),
                   pl.BlockSpec((3, 2 * D_FF), lambda i: (0, 0)), pl.BlockSpec((1, 2 * D_FF), lambda i: (0, 0))),
        scratch_shapes=[pltpu.VMEM((CONV_HALO + tm, 2 * D_FF), F32), pltpu.VMEM((tm + CONV_HALO, 2 * D_FF), F32)],
        compiler_params=_params("arbitrary"),
    )(dyb, u, u, conv_w, conv_b, wdown)


def _after(after):
    return ([], []) if after is None else ([after], [pl.BlockSpec(memory_space=pl.ANY)])


def _matmul_tn(a, b, tmo, tk, name, after=None, into=None, row0=0):
    s, m = a.shape
    n = b.shape[1]
    nk = s // tk
    tie, tie_spec = _after(after)
    rows = m if into is None else into if isinstance(into, int) else into.shape[0]
    ob = row0 // tmo
    assert ob * tmo == row0
    grown, grown_spec = ([], []) if into is None or isinstance(into, int) else ([into], [ANY])

    def body(a_ref, b_ref, *rest):
        o_ref = rest[-1]
        k = pl.program_id(1)

        @pl.when(k == 0)
        def _():
            o_ref[...] = jnp.zeros_like(o_ref)

        o_ref[...] += _mm_tn(a_ref[...], b_ref[pl.ds(pl.multiple_of(k * tk, tk), tk), :])

    return pl.pallas_call(
        body, name=name, grid=(m // tmo, nk), out_shape=jax.ShapeDtypeStruct((rows, n), F32),
        in_specs=[pl.BlockSpec((tk, tmo), lambda i, k: (k, i)), _resident((s, n))] + tie_spec + grown_spec,
        out_specs=pl.BlockSpec((tmo, n), lambda i, k: (i + ob, 0)),
        input_output_aliases={2 + len(tie): 0} if grown else {},
        compiler_params=_params("parallel", "arbitrary"),
    )(a, b, *tie, *grown)


def _ffn_bwd_b(du, wup_t, x1, ffn_norm, dy, tm):
    s = du.shape[0]

    def body(du_ref, wu_ref, x1_ref, gn_ref, dy_ref, dx1_ref, dx1b_ref, dg_ref):
        @pl.when(pl.program_id(0) == 0)
        def _():
            dg_ref[...] = jnp.zeros_like(dg_ref)

        dh2 = _mm(du_ref[...], wu_ref[...])
        x1 = x1_ref[...]
        _, r = _rmsnorm_fwd(x1, gn_ref[...])
        dx, dgr = _rmsnorm_bwd(x1, r, gn_ref[...], dh2)
        dg_ref[...] += jnp.sum(dgr, axis=0, keepdims=True)
        dx1 = dy_ref[...] + dx
        dx1_ref[...] = dx1
        dx1b_ref[...] = dx1.astype(MM)

    return pl.pallas_call(
        body, name="ffn_bwd_b", grid=(s // tm,),
        out_shape=(jax.ShapeDtypeStruct((s, D_MODEL), F32), jax.ShapeDtypeStruct((s, D_MODEL), MM),
                   jax.ShapeDtypeStruct((1, D_MODEL), F32)),
        in_specs=[_rows(tm, 2 * D_FF), _resident((2 * D_FF, D_MODEL)), _rows(tm, 1024), _resident((1, 1024)),
                  _rows(tm, 1024)],
        out_specs=(_rows(tm, 1024), _rows(tm, 1024), pl.BlockSpec((1, D_MODEL), lambda i: (0, 0))),
        compiler_params=_params("arbitrary"),
    )(du, wup_t, x1, ffn_norm, dy)


def _mix_bwd(dx1b, wout, g, a, b, tm):
    s = dx1b.shape[0]

    def body(dx_ref, w_ref, g_ref, a_ref, b_ref, da_ref, db_ref, dzg_ref, dbg_ref):
        @pl.when(pl.program_id(0) == 0)
        def _():
            dbg_ref[...] = jnp.zeros_like(dbg_ref)

        dmix = _mm_nt(dx_ref[...], w_ref[...])
        for half, src, dst in ((0, a_ref, da_ref), (1, b_ref, db_ref)):
            cols = slice(half * 1024, (half + 1) * 1024)
            gt = g_ref[:, cols]
            dst[...] = dmix * gt
            dz = dmix * src[...] * gt * (1.0 - gt)
            dzg_ref[:, cols] = dz.astype(MM)
            dbg_ref[:, cols] += jnp.sum(dz, axis=0, keepdims=True)

    sd = jax.ShapeDtypeStruct
    return pl.pallas_call(
        body, name="mix_bwd", grid=(s // tm,),
        out_shape=(sd((s, 1024), F32), sd((s, 1024), F32), sd((s, 2048), MM), sd((1, 2048), F32)),
        in_specs=[_rows(tm, 1024), _resident((1024, 1024)), _rows(tm, 2048), _rows(tm, 1024), _rows(tm, 1024)],
        out_specs=(_rows(tm, 1024), _rows(tm, 1024), _rows(tm, 2048), pl.BlockSpec((1, 2048), lambda i: (0, 0))),
        compiler_params=_params("arbitrary"),
    )(dx1b, wout, g, a, b)


def _pool_bwd(u, da, wpool, pool_scale, tm, after=None):
    s = u.shape[0]
    nt = s // tm
    hb = tm // POOL_HALO

    tie, tie_spec = _after(after)

    def body(u_ref, uh_ref, da_ref, dah_ref, wp_ref, ps_ref, *rest):
        dzu_ref, dwp_ref, dps_ref, ext_ref, exte_ref = rest[-5:]
        i = pl.program_id(0)

        @pl.when(i == 0)
        def _():
            dwp_ref[...] = jnp.zeros_like(dwp_ref)
            dps_ref[...] = jnp.zeros_like(dps_ref)

        ext_ref[pl.ds(0, POOL_HALO), :] = jnp.where(i > 0, uh_ref[...], 0.0)
        ext_ref[pl.ds(POOL_HALO, tm), :] = u_ref[...]
        pooled = _pooled(ext_ref, tm, i * tm)
        da = da_ref[...]
        dah = jnp.where(i < nt - 1, dah_ref[...], 0.0)
        t = (i * tm + lax.broadcasted_iota(jnp.int32, (tm + POOL_HALO, 1), 0)).astype(F32)
        for gi, w in enumerate(POOL_WINDOWS):
            cols = slice(gi * POOL_GROUP, (gi + 1) * POOL_GROUP)
            pg = pooled[gi].astype(MM)
            wg = wp_ref[gi]
            mixed = _mm(pg, wg)
            dps_ref[:, cols] += jnp.sum(da[:, cols] * mixed, axis=0, keepdims=True)
            dmx = (da[:, cols] * ps_ref[:, cols]).astype(MM)
            dwp_ref[gi] += _mm_tn(pg, dmx)
            dpl = _mm_nt(dmx, wg)
            dplh = _mm_nt(dah[:, cols] * ps_ref[:, cols], wg)
            cnt = jnp.minimum(t + 1.0, float(w))
            exte_ref[pl.ds(0, tm), cols] = dpl / cnt[0:tm]
            exte_ref[pl.ds(tm, POOL_HALO), cols] = dplh / cnt[tm:tm + POOL_HALO]
            acc = exte_ref[pl.ds(0, tm), cols]
            for k in range(1, w):
                acc = acc + exte_ref[pl.ds(k, tm), cols]
            dzu_ref[:, cols] = (acc - dpl).astype(MM)

    sd = jax.ShapeDtypeStruct
    last_halo = s // POOL_HALO - 1
    return pl.pallas_call(
        body, name="pool_bwd", grid=(nt,),
        out_shape=(sd((s, 1024), MM), sd((4, POOL_GROUP, POOL_GROUP), F32), sd((1, 1024), F32)),
        in_specs=[_rows(tm, 1024), pl.BlockSpec((POOL_HALO, 1024), lambda i: (jnp.maximum(i * hb - 1, 0), 0)),
                  _rows(tm, 1024),
                  pl.BlockSpec((POOL_HALO, 1024), lambda i: (jnp.minimum((i + 1) * hb, last_halo), 0)),
                  _resident((4, POOL_GROUP, POOL_GROUP)), _resident((1, 1024))] + tie_spec,
        out_specs=(_rows(tm, 1024), pl.BlockSpec((4, POOL_GROUP, POOL_GROUP), lambda i: (0, 0, 0)),
                   pl.BlockSpec((1, 1024), lambda i: (0, 0))),
        scratch_shapes=[pltpu.VMEM((POOL_HALO + tm, 1024), F32), pltpu.VMEM((tm + POOL_HALO, 1024), F32)],
        compiler_params=_params("arbitrary"),
    )(u, u, da, da, wpool, pool_scale, *tie)


def _attn_bwd(q, kv, db, tabs, q_norm_t, k_norm_t, sink_rows, bd):
    s = q.shape[0]
    nb = s // BLOCK
    scale = HEAD_DIM ** -0.5
    cur = lambda n: (jnp.minimum(n, nb - 1), 0)
    prv = lambda n: (jnp.maximum(n - 1, 0), 0)

    def body(q_ref, kvc_ref, kvp_ref, db_ref, c_ref, s1_ref, s2_ref, cp_ref, s1p_ref, s2p_ref, qn_ref, kn_ref,
             sink_ref, bd_ref, dzq_ref, dzkv_ref, dqn_ref, dkn_ref, dsk_ref,
             carry_ref, tot_ref, dqr_ref, qacc_ref, kacc_ref, sacc_ref):
        n = pl.program_id(0)
        bdm = bd_ref[...]
        kn = kn_ref[...]

        @pl.when(n == 0)
        def _():
            carry_ref[...] = jnp.zeros_like(carry_ref)
            qacc_ref[...] = jnp.zeros_like(qacc_ref)
            kacc_ref[...] = jnp.zeros_like(kacc_ref)
            sacc_ref[...] = jnp.zeros_like(sacc_ref)

        kp_raw = kvp_ref[:, 0:128]
        kph, rp = _head_norm_fwd(kp_raw, kn, bdm)
        cp, s1p, s2p = cp_ref[...], s1p_ref[...], s2p_ref[...]

        @pl.when(n < nb)
        def _():
            c, s1, s2 = c_ref[...], s1_ref[...], s2_ref[...]
            c8, s18, s28 = _tile_lanes(c, 8), _tile_lanes(s1, 8), _tile_lanes(s2, 8)
            q_raw = q_ref[...]
            qh, rq = _head_norm_fwd(q_raw, qn_ref[...], bdm)
            qr = (_rope(qh, c8, s18, s28) * scale).astype(MM)
            kc, _ = _head_norm_fwd(kvc_ref[:, 0:128], kn, bdm)
            k2 = jnp.concatenate([_rope(kph, cp, s1p, s2p), _rope(kc, c, s1, s2)], axis=0)
            v2 = jnp.concatenate([kvp_ref[:, 128:256], kvc_ref[:, 128:256]], axis=0)
            dob = db_ref[...].astype(MM)
            mask = _attn_mask_t(n)
            lane = lax.broadcasted_iota(jnp.int32, (1, LANES), 1)
            dsk = jnp.zeros((1, LANES), F32)
            dk2 = jnp.zeros((2 * BLOCK, LANES), F32)
            dv2 = jnp.zeros((2 * BLOCK, LANES), F32)
            for hk in range(N_KV_HEADS):
                qs = _stack_pairs(qr, hk)
                do = _stack_pairs(dob, hk)
                dqt = jnp.zeros((LANES, PAIR_COLS), F32)
                dkb, dvb = [], []
                for par, (kb, vb) in enumerate(zip(_parity_bands(k2, hk), _parity_bands(v2, hk))):
                    sc = jnp.where(mask, _mm_nt(kb, qs), NEG)
                    pr, psink = _softmax_sink_t(sc, sink_ref[hk, par])
                    dp = _mm_nt(vb, do)
                    coldot = jnp.sum(pr * dp, axis=0, keepdims=True)
                    ds = (pr * (dp - coldot)).astype(MM)
                    dsr = -psink * coldot
                    for j in range(PAIRS):
                        h = hk * GQA_GROUP + 2 * j + par
                        dsk = dsk + jnp.where(lane == h, jnp.sum(dsr[:, j * BLOCK:(j + 1) * BLOCK]), 0.0)
                    dqt = dqt + _mm(kb.T, ds)
                    dkb.append(_mm(ds, qs))
                    dvb.append(_mm(pr, do))
                for j in range(PAIRS):
                    col = (PAIRS * hk + j) * LANES
                    dqr_ref[:, col:col + LANES] = dqt[:, j * BLOCK:(j + 1) * BLOCK].T
                dk2 = dk2 + _fold_parity(dkb[0], dkb[1], hk)
                dv2 = dv2 + _fold_parity(dvb[0], dvb[1], hk)
            tot_ref[:, 0:128] = carry_ref[:, 0:128] + dk2[0:BLOCK, :]
            tot_ref[:, 128:256] = carry_ref[:, 128:256] + dv2[0:BLOCK, :]
            carry_ref[:, 0:128] = dk2[BLOCK:2 * BLOCK, :]
            carry_ref[:, 128:256] = dv2[BLOCK:2 * BLOCK, :]
            sacc_ref[...] += dsk
            dqh = _rope_bwd(dqr_ref[...] * scale, c8, s18, s28)
            dq, dgq = _head_norm_bwd(q_raw, rq, qn_ref[...], dqh, bdm)
            dzq_ref[...] = dq.astype(MM)
            qacc_ref[...] += jnp.sum(dgq, axis=0, keepdims=True)

        @pl.when(n == nb)
        def _():
            tot_ref[...] = carry_ref[...]

        dkh = _rope_bwd(tot_ref[:, 0:128], cp, s1p, s2p)
        dkr, dgk = _head_norm_bwd(kp_raw, rp, kn, dkh, bdm)
        dzkv_ref[:, 0:128] = dkr.astype(MM)
        dzkv_ref[:, 128:256] = tot_ref[:, 128:256].astype(MM)
        kacc_ref[...] += jnp.where(n > 0, jnp.sum(dgk, axis=0, keepdims=True), 0.0)

        @pl.when(n == nb)
        def _():
            fold = qacc_ref[:, 0:HEAD_DIM]
            for h in range(1, N_Q_HEADS):
                fold = fold + qacc_ref[:, h * HEAD_DIM:(h + 1) * HEAD_DIM]
            dqn_ref[...] = fold
            dkn_ref[...] = kacc_ref[:, 0:HEAD_DIM] + kacc_ref[:, HEAD_DIM:2 * HEAD_DIM]
            dsk_ref[...] = sacc_ref[...]

    tab = lambda im: pl.BlockSpec((BLOCK, LANES), im)
    sd = jax.ShapeDtypeStruct
    const = lambda n: (0, 0)
    return pl.pallas_call(
        body, name="attn_bwd", grid=(nb + 1,),
        out_shape=(sd((s, 1024), MM), sd((s, 256), MM), sd((1, HEAD_DIM), F32), sd((1, HEAD_DIM), F32),
                   sd((1, LANES), F32)),
        in_specs=[pl.BlockSpec((BLOCK, 1024), cur), pl.BlockSpec((BLOCK, 256), cur), pl.BlockSpec((BLOCK, 256), prv),
                  pl.BlockSpec((BLOCK, 1024), cur), tab(cur), tab(cur), tab(cur), tab(prv), tab(prv), tab(prv),
                  _resident((1, 1024)), _resident((1, 128)), _resident((N_KV_HEADS, 2, 1, PAIR_COLS)),
                  _resident((LANES, LANES))],
        out_specs=(pl.BlockSpec((BLOCK, 1024), cur), pl.BlockSpec((BLOCK, 256), prv),
                   pl.BlockSpec((1, HEAD_DIM), const), pl.BlockSpec((1, HEAD_DIM), const),
                   pl.BlockSpec((1, LANES), const)),
        scratch_shapes=[pltpu.VMEM((BLOCK, 256), F32), pltpu.VMEM((BLOCK, 256), F32), pltpu.VMEM((BLOCK, 1024), F32),
                        pltpu.VMEM((1, 1024), F32), pltpu.VMEM((1, 128), F32), pltpu.VMEM((1, LANES), F32)],
        compiler_params=_params("arbitrary"),
    )(q, kv, kv, db, *tabs, *tabs, q_norm_t, k_norm_t, sink_rows, bd)


def _inproj_bwd(dzu, dzq, dzkv, dzg, win_t, x, attn_norm, dx1, tm, after=None):
    s = x.shape[0]
    tie, tie_spec = _after(after)

    def body(du_ref, dq_ref, dkv_ref, dg_ref, w_ref, x_ref, gn_ref, dx1_ref, *rest):
        gx_ref, dgn_ref = rest[-2:]

        @pl.when(pl.program_id(0) == 0)
        def _():
            dgn_ref[...] = jnp.zeros_like(dgn_ref)

        dh = _mm(du_ref[...], w_ref[O_U:O_Q, :]) + _mm(dq_ref[...], w_ref[O_Q:O_KV, :])
        dh = dh + _mm(dkv_ref[...], w_ref[O_KV:O_G, :]) + _mm(dg_ref[...], w_ref[O_G:IN_WIDTH, :])
        x = x_ref[...]
        _, r = _rmsnorm_fwd(x, gn_ref[...])
        dx, dgr = _rmsnorm_bwd(x, r, gn_ref[...], dh)
        dgn_ref[...] += jnp.sum(dgr, axis=0, keepdims=True)
        gx_ref[...] = dx1_ref[...] + dx

    return pl.pallas_call(
        body, name="inproj_bwd", grid=(s // tm,),
        out_shape=(jax.ShapeDtypeStruct((s, D_MODEL), F32), jax.ShapeDtypeStruct((1, D_MODEL), F32)),
        in_specs=[_rows(tm, 1024), _rows(tm, 1024), _rows(tm, 256), _rows(tm, 2048),
                  _resident((IN_WIDTH, D_MODEL)), _rows(tm, 1024), _resident((1, 1024)), _rows(tm, 1024)] + tie_spec,
        out_specs=(_rows(tm, 1024), pl.BlockSpec((1, D_MODEL), lambda i: (0, 0))),
        compiler_params=_params("arbitrary"),
    )(dzu, dzq, dzkv, dzg, win_t, x, attn_norm, dx1, *tie)


def _attention_constants(q_norm, k_norm, sinks):
    inv_freq = ROPE_THETA ** (-jnp.arange(0, ROPE_DIM, 2, dtype=F32) / ROPE_DIM)
    lane = jnp.arange(LANES) % HEAD_DIM
    invf = jnp.where(lane < ROPE_DIM, inv_freq[lane % (ROPE_DIM // 2)], 0.0).reshape(1, LANES).astype(F32)
    bd = (jnp.arange(LANES)[:, None] // HEAD_DIM == jnp.arange(LANES)[None, :] // HEAD_DIM).astype(MM)
    q_norm_t = jnp.tile(q_norm, (1, N_Q_HEADS))
    k_norm_t = jnp.tile(k_norm, (1, N_KV_HEADS))
    sink_rows = jnp.repeat(sinks.reshape(N_KV_HEADS, PAIRS, 2).transpose(0, 2, 1), BLOCK, axis=2)
    sink_rows = sink_rows.reshape(N_KV_HEADS, 2, 1, PAIR_COLS)
    return invf, bd, q_norm_t, k_norm_t, sink_rows


ANY = pl.BlockSpec(memory_space=pl.ANY)


def _position():
    return lax.axis_index("x"), lax.axis_index("y"), lax.axis_index("c")


def _all_gather(shards):
    k = len(shards)

    def body(*refs):
        ins, outs = refs[:k], refs[k:2 * k]
        send_sems, recv_sems, local_sems = refs[2 * k:]
        x, y, c = _position()
        me, sibling = (x, y, c), (x, y, 1 - c)
        chips = [(1 - x, y), (x, 1 - y), (1 - x, 1 - y)]

        def copy(a, kk, block, to, src=None):
            dst = outs[a].at[4 * block[0] + 2 * block[1] + block[2]]
            return pltpu.make_async_remote_copy(
                src_ref=dst if src is None else src, dst_ref=dst, send_sem=send_sems.at[a * 7 + kk],
                recv_sem=recv_sems.at[a * 7 + kk], device_id=to, device_id_type=MESH)

        mine = [pltpu.make_async_copy(ins[a], outs[a].at[4 * x + 2 * y + c], local_sems.at[a]) for a in range(k)]
        for cp in mine:
            cp.start()
        first = []
        for a in range(k):
            first.append(copy(a, 0, me, sibling, src=ins[a]))
            first += [copy(a, 1 + j, me, (*chip, c), src=ins[a]) for j, chip in enumerate(chips)]
        for cp in first:
            cp.start()
        passed = []
        for j, chip in enumerate(chips):
            for a in range(k):
                copy(a, 1 + j, (*chip, c), me).wait_recv()
                cp = copy(a, 4 + j, (*chip, c), sibling)
                cp.start()
                passed.append(cp)
        for a in range(k):
            copy(a, 0, sibling, me).wait_recv()
            for j, chip in enumerate(chips):
                copy(a, 4 + j, (*chip, 1 - c), me).wait_recv()
        for cp in first + passed:
            cp.wait_send()
        for cp in mine:
            cp.wait()

    return pl.pallas_call(
        body, name="all_gather_weights",
        out_shape=tuple(jax.ShapeDtypeStruct((N_DEV,) + s.shape, s.dtype) for s in shards),
        in_specs=[ANY] * k, out_specs=(ANY,) * k,
        scratch_shapes=[pltpu.SemaphoreType.DMA((7 * k,)), pltpu.SemaphoreType.DMA((7 * k,)),
                        pltpu.SemaphoreType.DMA((k,))],
    )(*shards)


HBM = pl.BlockSpec(memory_space=pltpu.HBM)
SEM = pl.BlockSpec(memory_space=pltpu.SEMAPHORE)
EFFECT = pltpu.SideEffectType.DATAFLOW_SIDE_EFFECTING


def _exchange_start(name, bufs, n_sems, copies, after=None):
    k = len(bufs)
    tie, tie_spec = _after(after)
    n_in = k + len(tie)

    def body(*refs):
        for cp in copies(refs[:k], refs[n_in], refs[n_in + 1]):
            cp.start()
        refs[-1][...] = jnp.zeros_like(refs[-1])

    dma = pltpu.SemaphoreType.DMA((n_sems,))
    out = pl.pallas_call(
        body, name=name,
        out_shape=(dma, dma, *[pltpu.HBM(b.shape, b.dtype) for b in bufs], jax.ShapeDtypeStruct((8, LANES), F32)),
        in_specs=[HBM] * k + tie_spec, out_specs=(SEM, SEM, *[HBM] * k, pl.BlockSpec(memory_space=pltpu.VMEM)),
        input_output_aliases={i: 2 + i for i in range(k)},
        compiler_params=pltpu.CompilerParams(has_side_effects=EFFECT),
    )(*[pltpu.with_memory_space_constraint(b, pltpu.HBM) for b in bufs], *tie)
    return out[0], out[1], list(out[2:2 + k]), out[-1]


def _exchange_mid(name, bufs, sems_in, n_sems, waits, copies, after):
    k, ns = len(bufs), len(sems_in)

    def body(*refs):
        ins = refs[:k]
        waits(ins, *refs[k:k + ns])
        for cp in copies(ins, refs[k + ns + 1], refs[k + ns + 2]):
            cp.start()

    dma = pltpu.SemaphoreType.DMA((n_sems,))
    out = pl.pallas_call(
        body, name=name, out_shape=(dma, dma, *[pltpu.HBM(b.shape, b.dtype) for b in bufs]),
        in_specs=[HBM] * k + [SEM] * ns + [ANY], out_specs=(SEM, SEM, *[HBM] * k),
        input_output_aliases={i: 2 + i for i in range(k)},
        compiler_params=pltpu.CompilerParams(has_side_effects=EFFECT),
    )(*bufs, *sems_in, after)
    return out[0], out[1], list(out[2:])


def _exchange_wait(name, bufs, sems, waits, after=None):
    k, ns = len(bufs), len(sems)
    tie, tie_spec = _after(after)

    def body(*refs):
        waits(refs[:k], *refs[k:k + ns])

    out = pl.pallas_call(
        body, name=name, out_shape=tuple(pltpu.HBM(b.shape, b.dtype) for b in bufs),
        in_specs=[HBM] * k + [SEM] * ns + tie_spec, out_specs=(HBM,) * k,
        input_output_aliases={i: i for i in range(k)},
        compiler_params=pltpu.CompilerParams(has_side_effects=EFFECT),
    )(*bufs, *sems, *tie)
    return list(out)


def _gather_copies(k, direct):
    def copies(refs, send_sems, recv_sems):
        x, y, c = _position()
        chips = [(1 - x, y), (x, 1 - y), (1 - x, 1 - y)]
        out = []
        for a in range(k):
            land = refs[k + a]
            if direct:
                mine = land.at[4 * x + 2 * y + c]
                for kk, to in enumerate([(x, y, 1 - c)] + [(*chip, c) for chip in chips]):
                    out.append(pltpu.make_async_remote_copy(
                        src_ref=refs[a], dst_ref=mine, send_sem=send_sems.at[4 * a + kk],
                        recv_sem=recv_sems.at[4 * a + kk], device_id=to, device_id_type=MESH))
            else:
                for j, (px, py) in enumerate(chips):
                    slot = land.at[4 * px + 2 * py + c]
                    out.append(pltpu.make_async_remote_copy(
                        src_ref=slot, dst_ref=slot, send_sem=send_sems.at[3 * a + j], recv_sem=recv_sems.at[3 * a + j],
                        device_id=(x, y, 1 - c), device_id_type=MESH))
        return out
    return copies


def _all_gather_behind(shards, start_after, mid_after):
    k = len(shards)
    me = 4 * lax.axis_index("x") + 2 * lax.axis_index("y") + lax.axis_index("c")
    lands = [lax.dynamic_update_slice(lax.empty((N_DEV,) + s.shape, s.dtype), s[None], (me, 0, 0)) for s in shards]
    direct, passed = _gather_copies(k, True), _gather_copies(k, False)

    send_a, recv_a, bufs, token = _exchange_start("gather_start", list(shards) + lands, 4 * k, direct, start_after)

    def finish():
        def wait_ici(refs, send_sems, recv_sems):
            for i, cp in enumerate(direct(refs, send_sems, recv_sems)):
                if i % 4:
                    cp.wait_recv()

        send_b, recv_b, bufs2 = _exchange_mid("gather_pass", bufs, [send_a, recv_a], 3 * k, wait_ici, passed,
                                              mid_after())

        def wait_all(refs, sa, ra, sb, rb):
            for i, cp in enumerate(direct(refs, sa, ra)):
                cp.wait_send()
                if i % 4 == 0:
                    cp.wait_recv()
            for cp in passed(refs, sb, rb):
                cp.wait()

        return _exchange_wait("gather_wait", bufs2, [send_a, recv_a, send_b, recv_b], wait_all)[k:]

    return token, finish


def _pair_copies(k):
    def copies(refs, send_sems, recv_sems):
        x, y, c = _position()
        return [pltpu.make_async_remote_copy(
            src_ref=refs[a].at[2 * ch + 1 - c], dst_ref=refs[k + a].at[ch], send_sem=send_sems.at[4 * a + ch],
            recv_sem=recv_sems.at[4 * a + ch], device_id=(x, y, 1 - c), device_id_type=MESH)
            for a in range(k) for ch in range(4)]
    return copies


def _chip_copies(k):
    def copies(refs, send_sems, recv_sems):
        x, y, c = _position()
        return [pltpu.make_async_remote_copy(
            src_ref=refs[a].at[2 * px + py], dst_ref=refs[k + a].at[rel], send_sem=send_sems.at[3 * a + rel],
            recv_sem=recv_sems.at[3 * a + rel], device_id=(px, py, c), device_id_type=MESH)
            for a in range(k) for rel, (px, py) in enumerate([(1 - x, y), (x, 1 - y), (1 - x, 1 - y)])]
    return copies


def _symmetric_exchange(name, srcs, n_land, copies_of):
    k = len(srcs)
    lands = [lax.empty((n_land,) + s.shape[1:], s.dtype) for s in srcs]
    copies = copies_of(k)
    send_sems, recv_sems, bufs, token = _exchange_start(name + "_start", list(srcs) + lands, n_land * k, copies)

    def finish(after):
        def wait_all(refs, ss, rs):
            for cp in copies(refs, ss, rs):
                cp.wait()

        done = _exchange_wait(name + "_wait", bufs, [send_sems, recv_sems], wait_all, after)
        return done[:k], done[k:]

    return token, finish


def _pair_add(full, recv, wire):
    _, r, c_ = full.shape

    def body(f_ref, r_ref, pw_ref, own_ref):
        ch = pl.program_id(0)
        x, y, c = _position()
        tot = f_ref[0, c] + r_ref[0]
        pw_ref[0] = tot.astype(pw_ref.dtype)

        @pl.when(ch == 2 * x + y)
        def _():
            own_ref[...] = tot

    return pl.pallas_call(
        body, name="grad_pair_add", grid=(4,),
        out_shape=(jax.ShapeDtypeStruct((4, r, c_), wire), jax.ShapeDtypeStruct((r, c_), F32)),
        in_specs=[pl.BlockSpec((1, 2, r, c_), lambda i: (i, 0, 0, 0)), pl.BlockSpec((1, r, c_), lambda i: (i, 0, 0))],
        out_specs=(pl.BlockSpec((1, r, c_), lambda i: (i, 0, 0)), pl.BlockSpec((r, c_), lambda i: (0, 0))),
        compiler_params=_params("arbitrary"),
    )(full.reshape(4, 2, r, c_), recv)


def _adamw_math(w, g, m, v):
    m = ADAM_B1 * m + (1.0 - ADAM_B1) * g
    v = ADAM_B2 * v + (1.0 - ADAM_B2) * (g * g)
    m_hat = m / (1.0 - ADAM_B1 ** ADAM_STEP)
    v_hat = v / (1.0 - ADAM_B2 ** ADAM_STEP)
    delta = -ADAM_LR * (m_hat / (jnp.sqrt(v_hat) + ADAM_EPS) + ADAM_WD * w)
    return delta, m, v


def _row_tile(r):
    for t in (256, 176, 128):
        if r % t == 0 and r > t:
            return t
    return r


def _adamw(g_own, recv, w, m, v):
    r, c_ = w.shape
    t = _row_tile(r)
    blk = pl.BlockSpec((t, c_), lambda i: (i, 0))

    def body(g_ref, r_ref, w_ref, m_ref, v_ref, go_ref, d_ref, mo_ref, vo_ref):
        g = g_ref[...]
        for i in range(3):
            g = g + r_ref[i].astype(F32)
        go_ref[...] = g
        d_ref[...], mo_ref[...], vo_ref[...] = _adamw_math(w_ref[...], g, m_ref[...], v_ref[...])

    return pl.pallas_call(
        body, name="adamw", grid=(r // t,), out_shape=(jax.ShapeDtypeStruct((r, c_), F32),) * 4,
        in_specs=[blk, pl.BlockSpec((3, t, c_), lambda i: (0, i, 0)), blk, blk, blk], out_specs=(blk,) * 4,
        compiler_params=_params("parallel"),
    )(g_own, recv, w, m, v)


def _adamw_transposed(g_own, recv, w, m, v):
    r = g_own.shape[0]
    full, rem = divmod(r, LANES)
    nat = pl.BlockSpec((LANES, r), lambda i: (i, 0))

    def body(g_ref, r_ref, w_ref, m_ref, v_ref, go_ref, d_ref, mo_ref, vo_ref, gt_ref):
        gsum = g_ref[...]
        for i in range(3):
            gsum = gsum + r_ref[i].astype(F32)
        for b in range(full):
            gt_ref[:, b * LANES:(b + 1) * LANES] = gsum[b * LANES:(b + 1) * LANES, :].T
        if rem:
            tail = jnp.concatenate([gsum[full * LANES:r, :], jnp.zeros((LANES - rem, LANES), F32)], axis=0).T
            gt_ref[:, full * LANES:r] = tail[:, 0:rem]
        g = gt_ref[...]
        go_ref[...] = g
        d_ref[...], mo_ref[...], vo_ref[...] = _adamw_math(w_ref[...], g, m_ref[...], v_ref[...])

    return pl.pallas_call(
        body, name="adamw_transposed", grid=(D_MODEL // LANES,),
        out_shape=(jax.ShapeDtypeStruct((D_MODEL, r), F32),) * 4,
        in_specs=[pl.BlockSpec((r, LANES), lambda i: (0, i)), pl.BlockSpec((3, r, LANES), lambda i: (0, 0, i)),
                  nat, nat, nat],
        out_specs=(nat,) * 4, scratch_shapes=[pltpu.VMEM((LANES, r), F32)], compiler_params=_params("parallel"),
    )(g_own, recv, w, m, v)


SMALL_ROWS = 88


def _small_allreduce_adamw(gp, wp, mp, vp):
    def body(g_ref, w_ref, m_ref, v_ref, go_ref, d_ref, mo_ref, vo_ref, slots_ref, send_sems, recv_sems):
        x, y, c = _position()
        me = 4 * x + 2 * y + c
        slots_ref[me] = g_ref[...]
        cps = []
        for rel in range(1, N_DEV):
            fx, fy, fc = (rel >> 2) & 1, (rel >> 1) & 1, rel & 1
            to = (1 - x if fx else x, 1 - y if fy else y, 1 - c if fc else c)
            cps.append(pltpu.make_async_remote_copy(
                src_ref=g_ref, dst_ref=slots_ref.at[me], send_sem=send_sems.at[rel - 1],
                recv_sem=recv_sems.at[rel - 1], device_id=to, device_id_type=MESH))
        for cp in cps:
            cp.start()
        for cp in cps:
            cp.wait()
        g = slots_ref[0]
        for i in range(1, N_DEV):
            g = g + slots_ref[i]
        go_ref[...] = g
        d_ref[...], mo_ref[...], vo_ref[...] = _adamw_math(w_ref[...], g, m_ref[...], v_ref[...])

    vm = pl.BlockSpec(memory_space=pltpu.VMEM)
    return pl.pallas_call(
        body, name="small_allreduce_adamw", out_shape=(jax.ShapeDtypeStruct((SMALL_ROWS, LANES), F32),) * 4,
        in_specs=[vm] * 4, out_specs=(vm,) * 4,
        scratch_shapes=[pltpu.VMEM((N_DEV, SMALL_ROWS, LANES), F32), pltpu.SemaphoreType.DMA((N_DEV - 1,)),
                        pltpu.SemaphoreType.DMA((N_DEV - 1,))],
    )(gp, wp, mp, vp)


SMALL = ("attn_norm", "b_gate", "pool_scale", "q_norm", "k_norm", "sinks", "ffn_norm", "conv_b")
SHARDED = ("w_in", "w_pool", "w_out", "w_up", "conv_w", "w_down")
WEIGHTS = ("attn_norm", "w_in", "b_gate", "w_pool", "pool_scale", "q_norm", "k_norm", "sinks", "w_out", "ffn_norm",
           "w_up", "conv_w", "conv_b", "w_down")


N_SMALL = 10896


def _pack_small(d, loss=None):
    parts = [d[n].reshape(-1) for n in SMALL] + ([] if loss is None else [loss.reshape(1)])
    flat = jnp.concatenate(parts)
    return jnp.pad(flat, (0, SMALL_ROWS * LANES - flat.shape[0])).reshape(SMALL_ROWS, LANES)


def _unpack_small(p, like):
    flat, out, o = p.reshape(-1), {}, 0
    for n in SMALL:
        out[n] = flat[o:o + like[n].size].reshape(like[n].shape)
        o += like[n].size
    return out


def kernel(x, positions, attn_norm, w_in, b_gate, w_pool, pool_scale, q_norm, k_norm, sinks, w_out, ffn_norm, w_up, conv_w, conv_b, w_down, loss_target, m_attn_norm, m_w_in, m_b_gate, m_w_pool, m_pool_scale, m_q_norm, m_k_norm, m_sinks, m_w_out, m_ffn_norm, m_w_up, m_conv_w, m_conv_b, m_w_down, v_attn_norm, v_w_in, v_b_gate, v_w_pool, v_pool_scale, v_q_norm, v_k_norm, v_sinks, v_w_out, v_ffn_norm, v_w_up, v_conv_w, v_conv_b, v_w_down):
    w = dict(attn_norm=attn_norm, w_in=w_in, b_gate=b_gate, w_pool=w_pool, pool_scale=pool_scale, q_norm=q_norm,
             k_norm=k_norm, sinks=sinks, w_out=w_out, ffn_norm=ffn_norm, w_up=w_up, conv_w=conv_w, conv_b=conv_b,
             w_down=w_down)
    m = dict(attn_norm=m_attn_norm, w_in=m_w_in, b_gate=m_b_gate, w_pool=m_w_pool, pool_scale=m_pool_scale,
             q_norm=m_q_norm, k_norm=m_k_norm, sinks=m_sinks, w_out=m_w_out, ffn_norm=m_ffn_norm, w_up=m_w_up,
             conv_w=m_conv_w, conv_b=m_conv_b, w_down=m_w_down)
    v = dict(attn_norm=v_attn_norm, w_in=v_w_in, b_gate=v_b_gate, w_pool=v_w_pool, pool_scale=v_pool_scale,
             q_norm=v_q_norm, k_norm=v_k_norm, sinks=v_sinks, w_out=v_w_out, ffn_norm=v_ffn_norm, w_up=v_w_up,
             conv_w=v_conv_w, conv_b=v_conv_b, w_down=v_w_down)
    seq = x.shape[1]
    tm = 256
    tk = min(seq, 1024)
    xs, target, pos_col = x[0], loss_target[0], positions.reshape(seq, 1)
    invf, bd, q_norm_t, k_norm_t, sink_rows = _attention_constants(q_norm, k_norm, sinks)
    out = {}
    nat = {"w_in": (D_MODEL, 544), "w_pool": (128, POOL_GROUP), "w_out": (128, D_MODEL), "w_up": (D_MODEL, 704),
           "conv_w": (3, 704), "w_down": (352, D_MODEL)}

    def update(names, owns, recvs):
        for name, own, recv in zip(names, owns, recvs):
            w2, m2, v2 = (t[name].reshape(nat[name]) for t in (w, m, v))
            if name in ("w_in", "w_up"):
                res = _adamw_transposed(own, recv, w2, m2, v2)
            else:
                res = _adamw(own, recv, w2, m2, v2)
            out[name] = [t.reshape(w[name].shape) for t in res]

    (g_win,) = _all_gather([w_in[0].T.astype(MM)])
    win_t = g_win.reshape(IN_WIDTH, D_MODEL)
    fwd = {}
    token, gather_rest = _all_gather_behind(
        [w_pool[0].astype(MM).reshape(128, POOL_GROUP), w_out[0].astype(MM), w_up[0].T.astype(MM), conv_w[0],
         w_down[0].astype(MM)], g_win, lambda: fwd["b"])

    tabs = _rope_tables(pos_col, invf)
    h, u, q, kv, g = _inproj_fwd(xs, attn_norm + token[0:1, 0:1], win_t, b_gate, tm)
    fwd["b"] = b = _attn_fwd(q, kv, tabs, q_norm_t, k_norm_t, sink_rows, bd)
    g_wpool, g_wout, g_wup, g_convw, g_wdown = gather_rest()
    wpool = g_wpool.reshape(N_DEV, 4, 32, POOL_GROUP).transpose(1, 0, 2, 3).reshape(4, POOL_GROUP, POOL_GROUP)
    wout = g_wout.reshape(D_MODEL, D_MODEL)
    wup_t = g_wup.reshape(2 * D_FF, D_MODEL)
    convw = g_convw.transpose(1, 0, 2).reshape(3, 2 * D_FF)
    wdown = g_wdown.reshape(D_FF, D_MODEL)
    a = _pool_fwd(u, wpool, pool_scale, tm)
    x1, mix = _mix_out_fwd(xs, g, a, b, wout, tm)
    h2, uff, dy, dyb, lossp = _ffn_fwd(x1, ffn_norm, wup_t, convw, conv_b, wdown, target, tm)

    du, act, d_conv_w, d_conv_b = _ffn_bwd_a(dyb, uff, convw, conv_b, wdown, tm)
    d_wdown = _matmul_tn(act, dyb, FF_CHUNK, tk, "dw_down")
    dx1, dx1b, d_ffn_norm = _ffn_bwd_b(du, wup_t, x1, ffn_norm, dy, tm)
    d_wup_t = _matmul_tn(du, h2, 512, tk, "dw_up")
    late = ("w_down", "w_up", "conv_w")
    late_wire = (WIRE, WIRE, F32)
    late_full = [d_wdown.reshape(N_DEV, 352, D_MODEL), d_wup_t.reshape(N_DEV, 704, D_MODEL),
                 d_conv_w.reshape(3, N_DEV, 704).transpose(1, 0, 2)]
    token, late_pair = _symmetric_exchange("late_pair", late_full, 4, _pair_copies)
    d_wout = _matmul_tn(mix, dx1b, 512, tk, "dw_out", after=token)
    da, db, dzg, d_b_gate = _mix_bwd(dx1b, wout, g, a, b, tm)
    late_pw, late_own = zip(*[_pair_add(f, r, wd) for f, r, wd in zip(*late_pair(dzg), late_wire)])
    token, late_chip = _symmetric_exchange("late_chip", list(late_pw), 3, _chip_copies)
    d_win_t = _matmul_tn(dzg, h, 256, tk, "dw_in_g", after=token, into=IN_WIDTH, row0=O_G)
    dzu, d_wpool, d_pool_scale = _pool_bwd(u, da, wpool, pool_scale, tm)
    d_win_t = _matmul_tn(dzu, h, 512, tk, "dw_in_u", into=d_win_t, row0=O_U)
    dzq, dzkv, d_q_norm, d_k_norm, d_sinks = _attn_bwd(q, kv, db, tabs, q_norm_t, k_norm_t, sink_rows, bd)
    d_win_t = _matmul_tn(dzq, h, 512, tk, "dw_in_q", into=d_win_t, row0=O_Q)
    d_win_t = _matmul_tn(dzkv, h, 256, tk, "dw_in_kv", into=d_win_t, row0=O_KV)
    early = ("w_in", "w_pool", "w_out")
    early_full = [d_win_t.reshape(N_DEV, 544, D_MODEL),
                  d_wpool.reshape(4, N_DEV, 32, POOL_GROUP).transpose(1, 0, 2, 3).reshape(N_DEV, 128, POOL_GROUP),
                  d_wout.reshape(N_DEV, 128, D_MODEL)]
    token, early_pair = _symmetric_exchange("early_pair", early_full, 4, _pair_copies)
    grad_x, d_attn_norm = _inproj_bwd(dzu, dzq, dzkv, dzg, win_t, xs, attn_norm, dx1, tm, after=token)
    early_pw, early_own = zip(*[_pair_add(f, r, WIRE) for f, r in zip(*early_pair(grad_x))])
    token, early_chip = _symmetric_exchange("early_chip", list(early_pw), 3, _chip_copies)

    update(late, late_own, late_chip(token)[1])
    gr = dict(attn_norm=d_attn_norm, b_gate=d_b_gate, pool_scale=d_pool_scale, q_norm=d_q_norm, k_norm=d_k_norm,
              sinks=d_sinks[:, 0:N_Q_HEADS], ffn_norm=d_ffn_norm, conv_b=d_conv_b)
    small = _small_allreduce_adamw(_pack_small(gr, lossp[0, 0]), _pack_small(w), _pack_small(m), _pack_small(v))
    loss = small[0].reshape(-1)[N_SMALL]
    unpacked = [_unpack_small(p, w) for p in small]
    for name in SMALL:
        out[name] = [t[name] for t in unpacked]
    update(early, early_own, early_chip(small[1])[1])

    return (loss, grad_x[None], *[out[n][0] for n in WEIGHTS], *[out[n][1] for n in WEIGHTS],
            *[out[n][2] for n in WEIGHTS], *[out[n][3] for n in WEIGHTS])
```

```python
import functools

import jax
import jax.numpy as jnp
from jax import lax
from jax.experimental import pallas as pl
from jax.experimental.pallas import tpu as pltpu

F32 = jnp.float32
MM = jnp.bfloat16
WIRE = jnp.bfloat16

D_MODEL = 1024
D_FF = 2816
HEAD_DIM = 64
N_Q_HEADS = 16
N_KV_HEADS = 2
GQA_GROUP = 8
BLOCK = 128
ROPE_DIM = 16
ROPE_THETA = 500000.0
POOL_WINDOWS = (2, 4, 8, 16)
POOL_GROUP = 256
POOL_HALO = 16
CONV_HALO = 8
EPS = 1e-6
NEG = -1e30
O_U, O_Q, O_KV, O_G, IN_WIDTH = 0, 1024, 2048, 2304, 4352
FF_CHUNK = 1408

ADAM_LR, ADAM_B1, ADAM_B2, ADAM_EPS, ADAM_WD, ADAM_STEP = 0.001, 0.9, 0.999, 1e-08, 0.01, 10

N_DEV = 8
LANES = 128
VMEM_LIMIT_BYTES = 56 * 1024 * 1024
MESH = pl.DeviceIdType.MESH


def _params(*sem):
    return pltpu.CompilerParams(dimension_semantics=sem, vmem_limit_bytes=VMEM_LIMIT_BYTES)


def _resident(shape):
    nd = len(shape)
    return pl.BlockSpec(shape, lambda *_: (0,) * nd, pipeline_mode=pl.Buffered(1))


def _rows(tm, width):
    return pl.BlockSpec((tm, width), lambda i: (i, 0))


def _mm(a, b):
    return jnp.dot(a.astype(MM), b.astype(MM), preferred_element_type=F32)


def _mm_nt(a, b):
    return lax.dot_general(a.astype(MM), b.astype(MM), (((1,), (1,)), ((), ())), preferred_element_type=F32)


def _mm_tn(a, b):
    return lax.dot_general(a.astype(MM), b.astype(MM), (((0,), (0,)), ((), ())), preferred_element_type=F32)


def _rmsnorm_fwd(x, g):
    r = lax.rsqrt(jnp.mean(x * x, axis=-1, keepdims=True) + EPS)
    return x * r * g, r


def _rmsnorm_bwd(x, r, g, dy):
    xn = x * r
    dxn = dy * g
    dx = r * (dxn - xn * jnp.mean(dxn * xn, axis=-1, keepdims=True))
    return dx, dy * xn


def _group_sum64(v, bd):
    hi = v.astype(MM)
    lo = (v - hi.astype(F32)).astype(MM)
    outs = []
    for t in range(v.shape[1] // LANES):
        sl = slice(LANES * t, LANES * (t + 1))
        outs.append(jnp.dot(hi[:, sl], bd, preferred_element_type=F32)
                    + jnp.dot(lo[:, sl], bd, preferred_element_type=F32))
    return outs[0] if len(outs) == 1 else jnp.concatenate(outs, axis=1)


def _head_norm_fwd(x, g, bd):
    r = lax.rsqrt(_group_sum64(x * x, bd) * (1.0 / HEAD_DIM) + EPS)
    return x * r * g, r


def _head_norm_bwd(x, r, g, dy, bd):
    xn = x * r
    dxn = dy * g
    dx = r * (dxn - xn * (_group_sum64(dxn * xn, bd) * (1.0 / HEAD_DIM)))
    return dx, dy * xn


def _rope(x, c, s1, s2):
    w = x.shape[1]
    return x * c + pltpu.roll(x, w - ROPE_DIM // 2, 1) * s1 + pltpu.roll(x, ROPE_DIM // 2, 1) * s2


def _rope_bwd(dy, c, s1, s2):
    w = dy.shape[1]
    return dy * c + pltpu.roll(dy * s1, ROPE_DIM // 2, 1) + pltpu.roll(dy * s2, w - ROPE_DIM // 2, 1)


def _tile_lanes(t, reps):
    return t if reps == 1 else jnp.concatenate([t] * reps, axis=1)


def _rope_tables(pos_col, invf):
    s = pos_col.shape[0]
    tm = min(s, 1024)

    def body(pos_ref, invf_ref, c_ref, s1_ref, s2_ref):
        ang = pos_ref[...].astype(F32) * invf_ref[...]
        lane = lax.broadcasted_iota(jnp.int32, ang.shape, 1) % HEAD_DIM
        sn = jnp.sin(ang)
        c_ref[...] = jnp.cos(ang)
        s1_ref[...] = jnp.where(lane < ROPE_DIM // 2, -sn, 0.0)
        s2_ref[...] = jnp.where((lane >= ROPE_DIM // 2) & (lane < ROPE_DIM), sn, 0.0)

    out = jax.ShapeDtypeStruct((s, LANES), F32)
    return pl.pallas_call(
        body, name="rope_tables", grid=(s // tm,), out_shape=(out, out, out),
        in_specs=[_rows(tm, 1), _resident((1, LANES))],
        out_specs=(_rows(tm, LANES),) * 3, compiler_params=_params("parallel"),
    )(pos_col, invf)


def _inproj_fwd(x, attn_norm, win_t, b_gate, tm):
    s = x.shape[0]

    def body(x_ref, gn_ref, w_ref, bg_ref, h_ref, u_ref, q_ref, kv_ref, g_ref):
        h, _ = _rmsnorm_fwd(x_ref[...], gn_ref[...])
        h = h.astype(MM)
        h_ref[...] = h
        u_ref[...] = _mm_nt(h, w_ref[O_U:O_Q, :])
        q_ref[...] = _mm_nt(h, w_ref[O_Q:O_KV, :])
        kv_ref[...] = _mm_nt(h, w_ref[O_KV:O_G, :])
        g_ref[...] = jax.nn.sigmoid(_mm_nt(h, w_ref[O_G:IN_WIDTH, :]) + bg_ref[...])

    sd = jax.ShapeDtypeStruct
    return pl.pallas_call(
        body, name="inproj_fwd", grid=(s // tm,),
        out_shape=(sd((s, D_MODEL), MM), sd((s, 1024), F32), sd((s, 1024), F32), sd((s, 256), F32),
                   sd((s, 2048), F32)),
        in_specs=[_rows(tm, D_MODEL), _resident((1, D_MODEL)), _resident((IN_WIDTH, D_MODEL)), _resident((1, 2048))],
        out_specs=(_rows(tm, D_MODEL), _rows(tm, 1024), _rows(tm, 1024), _rows(tm, 256), _rows(tm, 2048)),
        compiler_params=_params("parallel"),
    )(x, attn_norm, win_t, b_gate)


def _pooled(ext_ref, tm, row0):
    t = (row0 + lax.broadcasted_iota(jnp.int32, (tm, 1), 0)).astype(F32)
    out = []
    for gi, w in enumerate(POOL_WINDOWS):
        cols = slice(gi * POOL_GROUP, (gi + 1) * POOL_GROUP)
        acc = ext_ref[pl.ds(POOL_HALO, tm), cols]
        for k in range(1, w):
            acc = acc + ext_ref[pl.ds(POOL_HALO - k, tm), cols]
        cnt = jnp.minimum(t + 1.0, float(w))
        out.append(acc / cnt - ext_ref[pl.ds(POOL_HALO, tm), cols])
    return out


def _pool_fwd(u, wpool, pool_scale, tm):
    s = u.shape[0]
    hb = tm // POOL_HALO

    def body(u_ref, halo_ref, wp_ref, ps_ref, a_ref, ext_ref):
        i = pl.program_id(0)
        ext_ref[pl.ds(0, POOL_HALO), :] = jnp.where(i > 0, halo_ref[...], 0.0)
        ext_ref[pl.ds(POOL_HALO, tm), :] = u_ref[...]
        pooled = _pooled(ext_ref, tm, i * tm)
        for gi in range(4):
            cols = slice(gi * POOL_GROUP, (gi + 1) * POOL_GROUP)
            a_ref[:, cols] = _mm(pooled[gi], wp_ref[gi]) * ps_ref[:, cols]

    return pl.pallas_call(
        body, name="pool_fwd", grid=(s // tm,), out_shape=jax.ShapeDtypeStruct((s, 1024), F32),
        in_specs=[_rows(tm, 1024), pl.BlockSpec((POOL_HALO, 1024), lambda i: (jnp.maximum(i * hb - 1, 0), 0)),
                  _resident((4, POOL_GROUP, POOL_GROUP)), _resident((1, 1024))],
        out_specs=_rows(tm, 1024), scratch_shapes=[pltpu.VMEM((POOL_HALO + tm, 1024), F32)],
        compiler_params=_params("parallel"),
    )(u, u, wpool, pool_scale)


PAIRS = GQA_GROUP // 2
PAIR_COLS = PAIRS * BLOCK


def _attn_mask_t(n):
    shape = (2 * BLOCK, PAIR_COLS)
    kj = lax.broadcasted_iota(jnp.int32, shape, 0)
    qi = lax.broadcasted_iota(jnp.int32, shape, 1) % BLOCK
    return (kj > qi) & (kj <= qi + BLOCK) & ((n > 0) | (kj >= BLOCK))


def _stack_pairs(x, hk):
    return jnp.concatenate([x[:, (PAIRS * hk + j) * LANES:(PAIRS * hk + j + 1) * LANES] for j in range(PAIRS)], axis=0)


def _parity_bands(t, hk):
    low = lax.broadcasted_iota(jnp.int32, t.shape, 1) < HEAD_DIM
    own = jnp.where(low if hk == 0 else ~low, t, 0.0)
    other = pltpu.roll(own, HEAD_DIM, 1)
    return (own, other) if hk == 0 else (other, own)


def _fold_parity(even, odd, hk):
    low = lax.broadcasted_iota(jnp.int32, even.shape, 1) < HEAD_DIM
    comb = jnp.where(low, even, odd)
    comb = comb + pltpu.roll(comb, HEAD_DIM, 1)
    return jnp.where(low if hk == 0 else ~low, comb, 0.0)


def _softmax_sink_t(s, sink):
    m = jnp.maximum(jnp.max(s, axis=0, keepdims=True), sink)
    p = jnp.exp(s - m)
    es = jnp.exp(sink - m)
    inv = 1.0 / (jnp.sum(p, axis=0, keepdims=True) + es)
    return p * inv, es * inv


def _attn_fwd(q, kv, tabs, q_norm_t, k_norm_t, sink_rows, bd):
    s = q.shape[0]
    nb = s // BLOCK
    scale = HEAD_DIM ** -0.5
    cur = lambda n: (n, 0)
    prv = lambda n: (jnp.maximum(n - 1, 0), 0)

    def body(q_ref, kvc_ref, kvp_ref, c_ref, s1_ref, s2_ref, cp_ref, s1p_ref, s2p_ref, qn_ref, kn_ref, sink_ref,
             bd_ref, o_ref):
        n = pl.program_id(0)
        bdm = bd_ref[...]
        c, s1, s2 = c_ref[...], s1_ref[...], s2_ref[...]
        qh, _ = _head_norm_fwd(q_ref[...], qn_ref[...], bdm)
        qr = (_rope(qh, _tile_lanes(c, 8), _tile_lanes(s1, 8), _tile_lanes(s2, 8)) * scale).astype(MM)
        kc, _ = _head_norm_fwd(kvc_ref[:, 0:128], kn_ref[...], bdm)
        kp, _ = _head_norm_fwd(kvp_ref[:, 0:128], kn_ref[...], bdm)
        k2 = jnp.concatenate([_rope(kp, cp_ref[...], s1p_ref[...], s2p_ref[...]), _rope(kc, c, s1, s2)], axis=0)
        v2 = jnp.concatenate([kvp_ref[:, 128:256], kvc_ref[:, 128:256]], axis=0)
        mask = _attn_mask_t(n)
        for hk in range(N_KV_HEADS):
            qs = _stack_pairs(qr, hk)
            ot = jnp.zeros((LANES, PAIR_COLS), F32)
            for par, (kb, vb) in enumerate(zip(_parity_bands(k2, hk), _parity_bands(v2, hk))):
                sc = jnp.where(mask, _mm_nt(kb, qs), NEG)
                pr, _ = _softmax_sink_t(sc, sink_ref[hk, par])
                ot = ot + _mm(vb.T, pr)
            for j in range(PAIRS):
                col = (PAIRS * hk + j) * LANES
                o_ref[:, col:col + LANES] = ot[:, j * BLOCK:(j + 1) * BLOCK].T

    tab = lambda im: pl.BlockSpec((BLOCK, LANES), im)
    return pl.pallas_call(
        body, name="attn_fwd", grid=(nb,), out_shape=jax.ShapeDtypeStruct((s, 1024), F32),
        in_specs=[pl.BlockSpec((BLOCK, 1024), cur), pl.BlockSpec((BLOCK, 256), cur), pl.BlockSpec((BLOCK, 256), prv),
                  tab(cur), tab(cur), tab(cur), tab(prv), tab(prv), tab(prv),
                  _resident((1, 1024)), _resident((1, 128)), _resident((N_KV_HEADS, 2, 1, PAIR_COLS)),
                  _resident((LANES, LANES))],
        out_specs=pl.BlockSpec((BLOCK, 1024), cur), compiler_params=_params("parallel"),
    )(q, kv, kv, *tabs, *tabs, q_norm_t, k_norm_t, sink_rows, bd)


def _mix_out_fwd(x, g, a, b, wout, tm):
    s = x.shape[0]

    def body(x_ref, g_ref, a_ref, b_ref, w_ref, x1_ref, mix_ref):
        mix = (g_ref[:, 0:1024] * a_ref[...] + g_ref[:, 1024:2048] * b_ref[...]).astype(MM)
        mix_ref[...] = mix
        x1_ref[...] = x_ref[...] + _mm(mix, w_ref[...])

    return pl.pallas_call(
        body, name="mix_out_fwd", grid=(s // tm,),
        out_shape=(jax.ShapeDtypeStruct((s, D_MODEL), F32), jax.ShapeDtypeStruct((s, D_MODEL), MM)),
        in_specs=[_rows(tm, 1024), _rows(tm, 2048), _rows(tm, 1024), _rows(tm, 1024), _resident((1024, 1024))],
        out_specs=(_rows(tm, 1024), _rows(tm, 1024)), compiler_params=_params("parallel"),
    )(x, g, a, b, wout)


SHIFT_ROWS = 16


def _sublane_major_matrices(tm):
    r = jnp.arange(tm)
    pm = (r[None, :] == ((tm // 8) * (r % 8) + r // 8)[:, None]).astype(MM)
    return pm, pm.T


def _to_sublane_major(pm, v):
    return jnp.dot(pm, v, preferred_element_type=F32).astype(MM)


def _to_time_order(pmt, v):
    hi = v.astype(MM)
    r1 = v - hi.astype(F32)
    mid = r1.astype(MM)
    lo = (r1 - mid.astype(F32)).astype(MM)
    dot = functools.partial(jnp.dot, preferred_element_type=F32)
    return dot(pmt, hi) + dot(pmt, mid) + dot(pmt, lo)


def _step_back(vreg_rows, before):
    sub = lax.broadcasted_iota(jnp.int32, vreg_rows.shape, 0)
    return jnp.where(sub == 0, before[7:8, :], pltpu.roll(vreg_rows, 1, 0))


def _step_ahead(vreg_rows, after):
    sub = lax.broadcasted_iota(jnp.int32, vreg_rows.shape, 0)
    return jnp.where(sub == 7, after[0:1, :], pltpu.roll(vreg_rows, 7, 0))


def _fill_back_rows(ext_ref, before, tm, cols):
    last = ext_ref[pl.ds(SHIFT_ROWS + tm - 8, 8), cols]
    pen = ext_ref[pl.ds(SHIFT_ROWS + tm - 16, 8), cols]
    ext_ref[pl.ds(8, 8), cols] = _step_back(last, before[8:16, :])
    ext_ref[pl.ds(0, 8), cols] = _step_back(pen, before[0:8, :])


def _conv_glu(ext_ref, cw_ref, cb_ref, tm, c):
    out = []
    for base in (c * FF_CHUNK, D_FF + c * FF_CHUNK):
        cols = slice(base, base + FF_CHUNK)
        y = cb_ref[:, cols] + cw_ref[0:1, cols] * ext_ref[pl.ds(0, tm), cols]
        y = y + cw_ref[1:2, cols] * ext_ref[pl.ds(8, tm), cols]
        y = y + cw_ref[2:3, cols] * ext_ref[pl.ds(SHIFT_ROWS, tm), cols]
        out.append(y)
    return out


def _ffn_fwd(x1, ffn_norm, wup_t, conv_w, conv_b, wdown, target, tm):
    s = x1.shape[0]
    inv_d = 1.0 / D_MODEL
    pm, pmt = _sublane_major_matrices(tm)

    def body(x1_ref, gn_ref, wu_ref, cw_ref, cb_ref, wd_ref, tgt_ref, pm_ref, pmt_ref, h2_ref, u_ref, dy_ref, dyb_ref,
             loss_ref, ext_ref, carry_ref):
        i = pl.program_id(0)

        @pl.when(i == 0)
        def _():
            carry_ref[...] = jnp.zeros_like(carry_ref)
            loss_ref[...] = jnp.zeros_like(loss_ref)

        x1 = x1_ref[...]
        h2, _ = _rmsnorm_fwd(x1, gn_ref[...])
        h2 = _to_sublane_major(pm_ref[...], h2.astype(MM))
        h2_ref[...] = h2
        for c in range(4):
            cols = slice(c * FF_CHUNK, (c + 1) * FF_CHUNK)
            uc = _mm_nt(h2, wu_ref[cols, :])
            u_ref[:, cols] = uc
            ext_ref[pl.ds(SHIFT_ROWS, tm), cols] = uc
            _fill_back_rows(ext_ref, carry_ref[:, cols], tm, cols)
            carry_ref[:, cols] = uc[tm - SHIFT_ROWS:tm, :]
        down = jnp.zeros((tm, D_MODEL), F32)
        for c in range(2):
            gate, val = _conv_glu(ext_ref, cw_ref, cb_ref, tm, c)
            act = gate * jax.nn.sigmoid(gate) * val
            down = down + _mm(act, wd_ref[c * FF_CHUNK:(c + 1) * FF_CHUNK, :])
        err = x1 + _to_time_order(pmt_ref[...], down) - tgt_ref[...]
        loss_ref[...] += jnp.full(loss_ref.shape, 0.5 * inv_d * jnp.sum(err * err), F32)
        dy = err * inv_d
        dy_ref[...] = dy
        dyb_ref[...] = _to_sublane_major(pm_ref[...], dy.astype(MM))

    sd = jax.ShapeDtypeStruct
    return pl.pallas_call(
        body, name="ffn_fwd", grid=(s // tm,),
        out_shape=(sd((s, D_MODEL), MM), sd((s, 2 * D_FF), F32), sd((s, D_MODEL), F32), sd((s, D_MODEL), MM),
                   sd((8, LANES), F32)),
        in_specs=[_rows(tm, 1024), _resident((1, 1024)), _resident((2 * D_FF, D_MODEL)), _resident((3, 2 * D_FF)),
                  _resident((1, 2 * D_FF)), _resident((D_FF, D_MODEL)), _rows(tm, 1024), _resident((tm, tm)),
                  _resident((tm, tm))],
        out_specs=(_rows(tm, 1024), _rows(tm, 2 * D_FF), _rows(tm, 1024), _rows(tm, 1024),
                   pl.BlockSpec((8, LANES), lambda i: (0, 0))),
        scratch_shapes=[pltpu.VMEM((SHIFT_ROWS + tm, 2 * D_FF), F32), pltpu.VMEM((SHIFT_ROWS, 2 * D_FF), F32)],
        compiler_params=_params("arbitrary"),
    )(x1, ffn_norm, wup_t, conv_w, conv_b, wdown, target, pm, pmt)


def _ffn_bwd_a(dyb, u, conv_w, conv_b, wdown, tm):
    s = dyb.shape[0]
    nt = s // tm
    hb = tm // SHIFT_ROWS
    rev = lambda i: (nt - 1 - i, 0)

    def body(dy_ref, u_ref, before_ref, cw_ref, cb_ref, wd_ref, du_ref, act_ref, dcw_ref, dcb_ref, ext_ref, extd_ref,
             ahead_ref):
        i = pl.program_id(0)
        first_tile = i == nt - 1

        @pl.when(i == 0)
        def _():
            ahead_ref[...] = jnp.zeros_like(ahead_ref)
            dcw_ref[...] = jnp.zeros_like(dcw_ref)
            dcb_ref[...] = jnp.zeros_like(dcb_ref)

        ext_ref[pl.ds(SHIFT_ROWS, tm), :] = u_ref[...]
        for c in range(4):
            cols = slice(c * FF_CHUNK, (c + 1) * FF_CHUNK)
            _fill_back_rows(ext_ref, jnp.where(first_tile, 0.0, before_ref[:, cols]), tm, cols)
        dy = dy_ref[...]
        for c in range(2):
            gate, val = _conv_glu(ext_ref, cw_ref, cb_ref, tm, c)
            sg = jax.nn.sigmoid(gate)
            sl = gate * sg
            act_ref[:, c * FF_CHUNK:(c + 1) * FF_CHUNK] = (sl * val).astype(MM)
            d_act = _mm_nt(dy, wd_ref[c * FF_CHUNK:(c + 1) * FF_CHUNK, :])
            extd_ref[pl.ds(0, tm), c * FF_CHUNK:(c + 1) * FF_CHUNK] = d_act * val * (sg * (1.0 + gate * (1.0 - sg)))
            extd_ref[pl.ds(0, tm), D_FF + c * FF_CHUNK:D_FF + (c + 1) * FF_CHUNK] = d_act * sl
        for c in range(4):
            cols = slice(c * FF_CHUNK, (c + 1) * FF_CHUNK)
            ahead = ahead_ref[:, cols]
            first2 = extd_ref[pl.ds(0, SHIFT_ROWS), cols]
            extd_ref[pl.ds(tm, 8), cols] = _step_ahead(first2[0:8, :], ahead[0:8, :])
            extd_ref[pl.ds(tm + 8, 8), cols] = _step_ahead(first2[8:16, :], ahead[8:16, :])
            ahead_ref[:, cols] = first2
            d0 = extd_ref[pl.ds(0, tm), cols]
            dcb_ref[:, cols] += jnp.sum(d0, axis=0, keepdims=True)
            for j in range(3):
                dcw_ref[j:j + 1, cols] += jnp.sum(d0 * ext_ref[pl.ds(8 * j, tm), cols], axis=0, keepdims=True)
            du = cw_ref[2:3, cols] * d0 + cw_ref[1:2, cols] * extd_ref[pl.ds(8, tm), cols]
            du = du + cw_ref[0:1, cols] * extd_ref[pl.ds(SHIFT_ROWS, tm), cols]
            du_ref[:, cols] = du.astype(MM)

    sd = jax.ShapeDtypeStruct
    return pl.pallas_call(
        body, name="ffn_bwd_a", grid=(nt,),
        out_shape=(sd((s, 2 * D_FF), MM), sd((s, D_FF), MM), sd((3, 2 * D_FF), F32), sd((1, 2 * D_FF), F32)),
        in_specs=[pl.BlockSpec((tm, D_MODEL), rev), pl.BlockSpec((tm, 2 * D_FF), rev),
                  pl.BlockSpec((SHIFT_ROWS, 2 * D_FF), lambda i: (jnp.maximum((nt - 1 - i) * hb - 1, 0), 0)),
                  _resident((3, 2 * D_FF)), _resident((1, 2 * D_FF)), _resident((D_FF, D_MODEL))],
        out_specs=(pl.BlockSpec((tm, 2 * D_FF), rev), pl.BlockSpec((tm, D_FF), rev),
                   pl.BlockSpec((3, 2 * D_FF), lambda i: (0, 0)), pl.BlockSpec((1, 2 * D_FF), lambda i: (0, 0))),
        scratch_shapes=[pltpu.VMEM((SHIFT_ROWS + tm, 2 * D_FF), F32), pltpu.VMEM((tm + SHIFT_ROWS, 2 * D_FF), F32),
                        pltpu.VMEM((SHIFT_ROWS, 2 * D_FF), F32)],
        compiler_params=_params("arbitrary"),
    )(dyb, u, u, conv_w, conv_b, wdown)


def _after(after):
    return ([], []) if after is None else ([after], [pl.BlockSpec(memory_space=pl.ANY)])


def _matmul_tn(a, b, tmo, tk, name, after=None, into=None, row0=0):
    s, m = a.shape
    n = b.shape[1]
    nk = s // tk
    tie, tie_spec = _after(after)
    rows = m if into is None else into if isinstance(into, int) else into.shape[0]
    ob = row0 // tmo
    assert ob * tmo == row0
    grown, grown_spec = ([], []) if into is None or isinstance(into, int) else ([into], [ANY])

    def body(a_ref, b_ref, *rest):
        o_ref = rest[-1]
        k = pl.program_id(1)

        @pl.when(k == 0)
        def _():
            o_ref[...] = jnp.zeros_like(o_ref)

        o_ref[...] += _mm_tn(a_ref[...], b_ref[pl.ds(pl.multiple_of(k * tk, tk), tk), :])

    return pl.pallas_call(
        body, name=name, grid=(m // tmo, nk), out_shape=jax.ShapeDtypeStruct((rows, n), F32),
        in_specs=[pl.BlockSpec((tk, tmo), lambda i, k: (k, i)), _resident((s, n))] + tie_spec + grown_spec,
        out_specs=pl.BlockSpec((tmo, n), lambda i, k: (i + ob, 0)),
        input_output_aliases={2 + len(tie): 0} if grown else {},
        compiler_params=_params("parallel", "arbitrary"),
    )(a, b, *tie, *grown)


def _ffn_bwd_b(du, wup_t, x1, ffn_norm, dy, tm):
    s = du.shape[0]

    def body(du_ref, wu_ref, x1_ref, gn_ref, dy_ref, pmt_ref, dx1_ref, dx1b_ref, dg_ref):
        @pl.when(pl.program_id(0) == 0)
        def _():
            dg_ref[...] = jnp.zeros_like(dg_ref)

        dh2 = _to_time_order(pmt_ref[...], _mm(du_ref[...], wu_ref[...]))
        x1 = x1_ref[...]
        _, r = _rmsnorm_fwd(x1, gn_ref[...])
        dx, dgr = _rmsnorm_bwd(x1, r, gn_ref[...], dh2)
        dg_ref[...] += jnp.sum(dgr, axis=0, keepdims=True)
        dx1 = dy_ref[...] + dx
        dx1_ref[...] = dx1
        dx1b_ref[...] = dx1.astype(MM)

    return pl.pallas_call(
        body, name="ffn_bwd_b", grid=(s // tm,),
        out_shape=(jax.ShapeDtypeStruct((s, D_MODEL), F32), jax.ShapeDtypeStruct((s, D_MODEL), MM),
                   jax.ShapeDtypeStruct((1, D_MODEL), F32)),
        in_specs=[_rows(tm, 2 * D_FF), _resident((2 * D_FF, D_MODEL)), _rows(tm, 1024), _resident((1, 1024)),
                  _rows(tm, 1024), _resident((tm, tm))],
        out_specs=(_rows(tm, 1024), _rows(tm, 1024), pl.BlockSpec((1, D_MODEL), lambda i: (0, 0))),
        compiler_params=_params("arbitrary"),
    )(du, wup_t, x1, ffn_norm, dy, _sublane_major_matrices(tm)[1])


def _mix_bwd(dx1b, wout, g, a, b, tm):
    s = dx1b.shape[0]

    def body(dx_ref, w_ref, g_ref, a_ref, b_ref, da_ref, db_ref, dzg_ref, dbg_ref):
        @pl.when(pl.program_id(0) == 0)
        def _():
            dbg_ref[...] = jnp.zeros_like(dbg_ref)

        dmix = _mm_nt(dx_ref[...], w_ref[...])
        for half, src, dst in ((0, a_ref, da_ref), (1, b_ref, db_ref)):
            cols = slice(half * 1024, (half + 1) * 1024)
            gt = g_ref[:, cols]
            dst[...] = dmix * gt
            dz = dmix * src[...] * gt * (1.0 - gt)
            dzg_ref[:, cols] = dz.astype(MM)
            dbg_ref[:, cols] += jnp.sum(dz, axis=0, keepdims=True)

    sd = jax.ShapeDtypeStruct
    return pl.pallas_call(
        body, name="mix_bwd", grid=(s // tm,),
        out_shape=(sd((s, 1024), F32), sd((s, 1024), F32), sd((s, 2048), MM), sd((1, 2048), F32)),
        in_specs=[_rows(tm, 1024), _resident((1024, 1024)), _rows(tm, 2048), _rows(tm, 1024), _rows(tm, 1024)],
        out_specs=(_rows(tm, 1024), _rows(tm, 1024), _rows(tm, 2048), pl.BlockSpec((1, 2048), lambda i: (0, 0))),
        compiler_params=_params("arbitrary"),
    )(dx1b, wout, g, a, b)


def _pool_bwd(u, da, wpool, pool_scale, tm, after=None):
    s = u.shape[0]
    nt = s // tm
    hb = tm // POOL_HALO

    tie, tie_spec = _after(after)

    def body(u_ref, uh_ref, da_ref, dah_ref, wp_ref, ps_ref, *rest):
        dzu_ref, dwp_ref, dps_ref, ext_ref, exte_ref = rest[-5:]
        i = pl.program_id(0)

        @pl.when(i == 0)
        def _():
            dwp_ref[...] = jnp.zeros_like(dwp_ref)
            dps_ref[...] = jnp.zeros_like(dps_ref)

        ext_ref[pl.ds(0, POOL_HALO), :] = jnp.where(i > 0, uh_ref[...], 0.0)
        ext_ref[pl.ds(POOL_HALO, tm), :] = u_ref[...]
        pooled = _pooled(ext_ref, tm, i * tm)
        da = da_ref[...]
        dah = jnp.where(i < nt - 1, dah_ref[...], 0.0)
        t = (i * tm + lax.broadcasted_iota(jnp.int32, (tm + POOL_HALO, 1), 0)).astype(F32)
        for gi, w in enumerate(POOL_WINDOWS):
            cols = slice(gi * POOL_GROUP, (gi + 1) * POOL_GROUP)
            pg = pooled[gi].astype(MM)
            wg = wp_ref[gi]
            mixed = _mm(pg, wg)
            dps_ref[:, cols] += jnp.sum(da[:, cols] * mixed, axis=0, keepdims=True)
            dmx = (da[:, cols] * ps_ref[:, cols]).astype(MM)
            dwp_ref[gi] += _mm_tn(pg, dmx)
            dpl = _mm_nt(dmx, wg)
            dplh = _mm_nt(dah[:, cols] * ps_ref[:, cols], wg)
            cnt = jnp.minimum(t + 1.0, float(w))
            exte_ref[pl.ds(0, tm), cols] = dpl / cnt[0:tm]
            exte_ref[pl.ds(tm, POOL_HALO), cols] = dplh / cnt[tm:tm + POOL_HALO]
            acc = exte_ref[pl.ds(0, tm), cols]
            for k in range(1, w):
                acc = acc + exte_ref[pl.ds(k, tm), cols]
            dzu_ref[:, cols] = (acc - dpl).astype(MM)

    sd = jax.ShapeDtypeStruct
    last_halo = s // POOL_HALO - 1
    return pl.pallas_call(
        body, name="pool_bwd", grid=(nt,),
        out_shape=(sd((s, 1024), MM), sd((4, POOL_GROUP, POOL_GROUP), F32), sd((1, 1024), F32)),
        in_specs=[_rows(tm, 1024), pl.BlockSpec((POOL_HALO, 1024), lambda i: (jnp.maximum(i * hb - 1, 0), 0)),
                  _rows(tm, 1024),
                  pl.BlockSpec((POOL_HALO, 1024), lambda i: (jnp.minimum((i + 1) * hb, last_halo), 0)),
                  _resident((4, POOL_GROUP, POOL_GROUP)), _resident((1, 1024))] + tie_spec,
        out_specs=(_rows(tm, 1024), pl.BlockSpec((4, POOL_GROUP, POOL_GROUP), lambda i: (0, 0, 0)),
                   pl.BlockSpec((1, 1024), lambda i: (0, 0))),
        scratch_shapes=[pltpu.VMEM((POOL_HALO + tm, 1024), F32), pltpu.VMEM((tm + POOL_HALO, 1024), F32)],
        compiler_params=_params("arbitrary"),
    )(u, u, da, da, wpool, pool_scale, *tie)


def _attn_bwd(q, kv, db, tabs, q_norm_t, k_norm_t, sink_rows, bd):
    s = q.shape[0]
    nb = s // BLOCK
    scale = HEAD_DIM ** -0.5
    cur = lambda n: (jnp.minimum(n, nb - 1), 0)
    prv = lambda n: (jnp.maximum(n - 1, 0), 0)

    def body(q_ref, kvc_ref, kvp_ref, db_ref, c_ref, s1_ref, s2_ref, cp_ref, s1p_ref, s2p_ref, qn_ref, kn_ref,
             sink_ref, bd_ref, dzq_ref, dzkv_ref, dqn_ref, dkn_ref, dsk_ref,
             carry_ref, tot_ref, dqr_ref, qacc_ref, kacc_ref, sacc_ref):
        n = pl.program_id(0)
        bdm = bd_ref[...]
        kn = kn_ref[...]

        @pl.when(n == 0)
        def _():
            carry_ref[...] = jnp.zeros_like(carry_ref)
            qacc_ref[...] = jnp.zeros_like(qacc_ref)
            kacc_ref[...] = jnp.zeros_like(kacc_ref)
            sacc_ref[...] = jnp.zeros_like(sacc_ref)

        kp_raw = kvp_ref[:, 0:128]
        kph, rp = _head_norm_fwd(kp_raw, kn, bdm)
        cp, s1p, s2p = cp_ref[...], s1p_ref[...], s2p_ref[...]

        @pl.when(n < nb)
        def _():
            c, s1, s2 = c_ref[...], s1_ref[...], s2_ref[...]
            c8, s18, s28 = _tile_lanes(c, 8), _tile_lanes(s1, 8), _tile_lanes(s2, 8)
            q_raw = q_ref[...]
            qh, rq = _head_norm_fwd(q_raw, qn_ref[...], bdm)
            qr = (_rope(qh, c8, s18, s28) * scale).astype(MM)
            kc, _ = _head_norm_fwd(kvc_ref[:, 0:128], kn, bdm)
            k2 = jnp.concatenate([_rope(kph, cp, s1p, s2p), _rope(kc, c, s1, s2)], axis=0)
            v2 = jnp.concatenate([kvp_ref[:, 128:256], kvc_ref[:, 128:256]], axis=0)
            dob = db_ref[...].astype(MM)
            mask = _attn_mask_t(n)
            lane = lax.broadcasted_iota(jnp.int32, (1, LANES), 1)
            dsk = jnp.zeros((1, LANES), F32)
            dk2 = jnp.zeros((2 * BLOCK, LANES), F32)
            dv2 = jnp.zeros((2 * BLOCK, LANES), F32)
            for hk in range(N_KV_HEADS):
                qs = _stack_pairs(qr, hk)
                do = _stack_pairs(dob, hk)
                dqt = jnp.zeros((LANES, PAIR_COLS), F32)
                dkb, dvb = [], []
                for par, (kb, vb) in enumerate(zip(_parity_bands(k2, hk), _parity_bands(v2, hk))):
                    sc = jnp.where(mask, _mm_nt(kb, qs), NEG)
                    pr, psink = _softmax_sink_t(sc, sink_ref[hk, par])
                    dp = _mm_nt(vb, do)
                    coldot = jnp.sum(pr * dp, axis=0, keepdims=True)
                    ds = (pr * (dp - coldot)).astype(MM)
                    dsr = -psink * coldot
                    for j in range(PAIRS):
                        h = hk * GQA_GROUP + 2 * j + par
                        dsk = dsk + jnp.where(lane == h, jnp.sum(dsr[:, j * BLOCK:(j + 1) * BLOCK]), 0.0)
                    dqt = dqt + _mm(kb.T, ds)
                    dkb.append(_mm(ds, qs))
                    dvb.append(_mm(pr, do))
                for j in range(PAIRS):
                    col = (PAIRS * hk + j) * LANES
                    dqr_ref[:, col:col + LANES] = dqt[:, j * BLOCK:(j + 1) * BLOCK].T
                dk2 = dk2 + _fold_parity(dkb[0], dkb[1], hk)
                dv2 = dv2 + _fold_parity(dvb[0], dvb[1], hk)
            tot_ref[:, 0:128] = carry_ref[:, 0:128] + dk2[0:BLOCK, :]
            tot_ref[:, 128:256] = carry_ref[:, 128:256] + dv2[0:BLOCK, :]
            carry_ref[:, 0:128] = dk2[BLOCK:2 * BLOCK, :]
            carry_ref[:, 128:256] = dv2[BLOCK:2 * BLOCK, :]
            sacc_ref[...] += dsk
            dqh = _rope_bwd(dqr_ref[...] * scale, c8, s18, s28)
            dq, dgq = _head_norm_bwd(q_raw, rq, qn_ref[...], dqh, bdm)
            dzq_ref[...] = dq.astype(MM)
            qacc_ref[...] += jnp.sum(dgq, axis=0, keepdims=True)

        @pl.when(n == nb)
        def _():
            tot_ref[...] = carry_ref[...]

        dkh = _rope_bwd(tot_ref[:, 0:128], cp, s1p, s2p)
        dkr, dgk = _head_norm_bwd(kp_raw, rp, kn, dkh, bdm)
        dzkv_ref[:, 0:128] = dkr.astype(MM)
        dzkv_ref[:, 128:256] = tot_ref[:, 128:256].astype(MM)
        kacc_ref[...] += jnp.where(n > 0, jnp.sum(dgk, axis=0, keepdims=True), 0.0)

        @pl.when(n == nb)
        def _():
            fold = qacc_ref[:, 0:HEAD_DIM]
            for h in range(1, N_Q_HEADS):
                fold = fold + qacc_ref[:, h * HEAD_DIM:(h + 1) * HEAD_DIM]
            dqn_ref[...] = fold
            dkn_ref[...] = kacc_ref[:, 0:HEAD_DIM] + kacc_ref[:, HEAD_DIM:2 * HEAD_DIM]
            dsk_ref[...] = sacc_ref[...]

    tab = lambda im: pl.BlockSpec((BLOCK, LANES), im)
    sd = jax.ShapeDtypeStruct
    const = lambda n: (0, 0)
    return pl.pallas_call(
        body, name="attn_bwd", grid=(nb + 1,),
        out_shape=(sd((s, 1024), MM), sd((s, 256), MM), sd((1, HEAD_DIM), F32), sd((1, HEAD_DIM), F32),
                   sd((1, LANES), F32)),
        in_specs=[pl.BlockSpec((BLOCK, 1024), cur), pl.BlockSpec((BLOCK, 256), cur), pl.BlockSpec((BLOCK, 256), prv),
                  pl.BlockSpec((BLOCK, 1024), cur), tab(cur), tab(cur), tab(cur), tab(prv), tab(prv), tab(prv),
                  _resident((1, 1024)), _resident((1, 128)), _resident((N_KV_HEADS, 2, 1, PAIR_COLS)),
                  _resident((LANES, LANES))],
        out_specs=(pl.BlockSpec((BLOCK, 1024), cur), pl.BlockSpec((BLOCK, 256), prv),
                   pl.BlockSpec((1, HEAD_DIM), const), pl.BlockSpec((1, HEAD_DIM), const),
                   pl.BlockSpec((1, LANES), const)),
        scratch_shapes=[pltpu.VMEM((BLOCK, 256), F32), pltpu.VMEM((BLOCK, 256), F32), pltpu.VMEM((BLOCK, 1024), F32),
                        pltpu.VMEM((1, 1024), F32), pltpu.VMEM((1, 128), F32), pltpu.VMEM((1, LANES), F32)],
        compiler_params=_params("arbitrary"),
    )(q, kv, kv, db, *tabs, *tabs, q_norm_t, k_norm_t, sink_rows, bd)


def _inproj_bwd(dzu, dzq, dzkv, dzg, win_t, x, attn_norm, dx1, tm, after=None):
    s = x.shape[0]
    tie, tie_spec = _after(after)

    def body(du_ref, dq_ref, dkv_ref, dg_ref, w_ref, x_ref, gn_ref, dx1_ref, *rest):
        gx_ref, dgn_ref = rest[-2:]

        @pl.when(pl.program_id(0) == 0)
        def _():
            dgn_ref[...] = jnp.zeros_like(dgn_ref)

        dh = _mm(du_ref[...], w_ref[O_U:O_Q, :]) + _mm(dq_ref[...], w_ref[O_Q:O_KV, :])
        dh = dh + _mm(dkv_ref[...], w_ref[O_KV:O_G, :]) + _mm(dg_ref[...], w_ref[O_G:IN_WIDTH, :])
        x = x_ref[...]
        _, r = _rmsnorm_fwd(x, gn_ref[...])
        dx, dgr = _rmsnorm_bwd(x, r, gn_ref[...], dh)
        dgn_ref[...] += jnp.sum(dgr, axis=0, keepdims=True)
        gx_ref[...] = dx1_ref[...] + dx

    return pl.pallas_call(
        body, name="inproj_bwd", grid=(s // tm,),
        out_shape=(jax.ShapeDtypeStruct((s, D_MODEL), F32), jax.ShapeDtypeStruct((1, D_MODEL), F32)),
        in_specs=[_rows(tm, 1024), _rows(tm, 1024), _rows(tm, 256), _rows(tm, 2048),
                  _resident((IN_WIDTH, D_MODEL)), _rows(tm, 1024), _resident((1, 1024)), _rows(tm, 1024)] + tie_spec,
        out_specs=(_rows(tm, 1024), pl.BlockSpec((1, D_MODEL), lambda i: (0, 0))),
        compiler_params=_params("arbitrary"),
    )(dzu, dzq, dzkv, dzg, win_t, x, attn_norm, dx1, *tie)


def _attention_constants(q_norm, k_norm, sinks):
    inv_freq = ROPE_THETA ** (-jnp.arange(0, ROPE_DIM, 2, dtype=F32) / ROPE_DIM)
    lane = jnp.arange(LANES) % HEAD_DIM
    invf = jnp.where(lane < ROPE_DIM, inv_freq[lane % (ROPE_DIM // 2)], 0.0).reshape(1, LANES).astype(F32)
    bd = (jnp.arange(LANES)[:, None] // HEAD_DIM == jnp.arange(LANES)[None, :] // HEAD_DIM).astype(MM)
    q_norm_t = jnp.tile(q_norm, (1, N_Q_HEADS))
    k_norm_t = jnp.tile(k_norm, (1, N_KV_HEADS))
    sink_rows = jnp.repeat(sinks.reshape(N_KV_HEADS, PAIRS, 2).transpose(0, 2, 1), BLOCK, axis=2)
    sink_rows = sink_rows.reshape(N_KV_HEADS, 2, 1, PAIR_COLS)
    return invf, bd, q_norm_t, k_norm_t, sink_rows


ANY = pl.BlockSpec(memory_space=pl.ANY)


def _position():
    return lax.axis_index("x"), lax.axis_index("y"), lax.axis_index("c")


def _all_gather(shards):
    k = len(shards)

    def body(*refs):
        ins, outs = refs[:k], refs[k:2 * k]
        send_sems, recv_sems, local_sems = refs[2 * k:]
        x, y, c = _position()
        me, sibling = (x, y, c), (x, y, 1 - c)
        chips = [(1 - x, y), (x, 1 - y), (1 - x, 1 - y)]

        def copy(a, kk, block, to, src=None):
            dst = outs[a].at[4 * block[0] + 2 * block[1] + block[2]]
            return pltpu.make_async_remote_copy(
                src_ref=dst if src is None else src, dst_ref=dst, send_sem=send_sems.at[a * 7 + kk],
                recv_sem=recv_sems.at[a * 7 + kk], device_id=to, device_id_type=MESH)

        mine = [pltpu.make_async_copy(ins[a], outs[a].at[4 * x + 2 * y + c], local_sems.at[a]) for a in range(k)]
        for cp in mine:
            cp.start()
        first = []
        for a in range(k):
            first.append(copy(a, 0, me, sibling, src=ins[a]))
            first += [copy(a, 1 + j, me, (*chip, c), src=ins[a]) for j, chip in enumerate(chips)]
        for cp in first:
            cp.start()
        passed = []
        for j, chip in enumerate(chips):
            for a in range(k):
                copy(a, 1 + j, (*chip, c), me).wait_recv()
                cp = copy(a, 4 + j, (*chip, c), sibling)
                cp.start()
                passed.append(cp)
        for a in range(k):
            copy(a, 0, sibling, me).wait_recv()
            for j, chip in enumerate(chips):
                copy(a, 4 + j, (*chip, 1 - c), me).wait_recv()
        for cp in first + passed:
            cp.wait_send()
        for cp in mine:
            cp.wait()

    return pl.pallas_call(
        body, name="all_gather_weights",
        out_shape=tuple(jax.ShapeDtypeStruct((N_DEV,) + s.shape, s.dtype) for s in shards),
        in_specs=[ANY] * k, out_specs=(ANY,) * k,
        scratch_shapes=[pltpu.SemaphoreType.DMA((7 * k,)), pltpu.SemaphoreType.DMA((7 * k,)),
                        pltpu.SemaphoreType.DMA((k,))],
    )(*shards)


HBM = pl.BlockSpec(memory_space=pltpu.HBM)
SEM = pl.BlockSpec(memory_space=pltpu.SEMAPHORE)
EFFECT = pltpu.SideEffectType.DATAFLOW_SIDE_EFFECTING


def _exchange_start(name, bufs, n_sems, copies, after=None):
    k = len(bufs)
    tie, tie_spec = _after(after)
    n_in = k + len(tie)

    def body(*refs):
        for cp in copies(refs[:k], refs[n_in], refs[n_in + 1]):
            cp.start()
        refs[-1][...] = jnp.zeros_like(refs[-1])

    dma = pltpu.SemaphoreType.DMA((n_sems,))
    out = pl.pallas_call(
        body, name=name,
        out_shape=(dma, dma, *[pltpu.HBM(b.shape, b.dtype) for b in bufs], jax.ShapeDtypeStruct((8, LANES), F32)),
        in_specs=[HBM] * k + tie_spec, out_specs=(SEM, SEM, *[HBM] * k, pl.BlockSpec(memory_space=pltpu.VMEM)),
        input_output_aliases={i: 2 + i for i in range(k)},
        compiler_params=pltpu.CompilerParams(has_side_effects=EFFECT),
    )(*[pltpu.with_memory_space_constraint(b, pltpu.HBM) for b in bufs], *tie)
    return out[0], out[1], list(out[2:2 + k]), out[-1]


def _exchange_mid(name, bufs, sems_in, n_sems, waits, copies, after):
    k, ns = len(bufs), len(sems_in)

    def body(*refs):
        ins = refs[:k]
        waits(ins, *refs[k:k + ns])
        for cp in copies(ins, refs[k + ns + 1], refs[k + ns + 2]):
            cp.start()

    dma = pltpu.SemaphoreType.DMA((n_sems,))
    out = pl.pallas_call(
        body, name=name, out_shape=(dma, dma, *[pltpu.HBM(b.shape, b.dtype) for b in bufs]),
        in_specs=[HBM] * k + [SEM] * ns + [ANY], out_specs=(SEM, SEM, *[HBM] * k),
        input_output_aliases={i: 2 + i for i in range(k)},
        compiler_params=pltpu.CompilerParams(has_side_effects=EFFECT),
    )(*bufs, *sems_in, after)
    return out[0], out[1], list(out[2:])


def _exchange_wait(name, bufs, sems, waits, after=None):
    k, ns = len(bufs), len(sems)
    tie, tie_spec = _after(after)

    def body(*refs):
        waits(refs[:k], *refs[k:k + ns])

    out = pl.pallas_call(
        body, name=name, out_shape=tuple(pltpu.HBM(b.shape, b.dtype) for b in bufs),
        in_specs=[HBM] * k + [SEM] * ns + tie_spec, out_specs=(HBM,) * k,
        input_output_aliases={i: i for i in range(k)},
        compiler_params=pltpu.CompilerParams(has_side_effects=EFFECT),
    )(*bufs, *sems, *tie)
    return list(out)


def _gather_copies(k, direct):
    def copies(refs, send_sems, recv_sems):
        x, y, c = _position()
        chips = [(1 - x, y), (x, 1 - y), (1 - x, 1 - y)]
        out = []
        for a in range(k):
            land = refs[k + a]
            if direct:
                mine = land.at[4 * x + 2 * y + c]
                for kk, to in enumerate([(x, y, 1 - c)] + [(*chip, c) for chip in chips]):
                    out.append(pltpu.make_async_remote_copy(
                        src_ref=refs[a], dst_ref=mine, send_sem=send_sems.at[4 * a + kk],
                        recv_sem=recv_sems.at[4 * a + kk], device_id=to, device_id_type=MESH))
            else:
                for j, (px, py) in enumerate(chips):
                    slot = land.at[4 * px + 2 * py + c]
                    out.append(pltpu.make_async_remote_copy(
                        src_ref=slot, dst_ref=slot, send_sem=send_sems.at[3 * a + j], recv_sem=recv_sems.at[3 * a + j],
                        device_id=(x, y, 1 - c), device_id_type=MESH))
        return out
    return copies


def _all_gather_behind(shards, start_after, mid_after):
    k = len(shards)
    me = 4 * lax.axis_index("x") + 2 * lax.axis_index("y") + lax.axis_index("c")
    lands = [lax.dynamic_update_slice(lax.empty((N_DEV,) + s.shape, s.dtype), s[None], (me, 0, 0)) for s in shards]
    direct, passed = _gather_copies(k, True), _gather_copies(k, False)

    send_a, recv_a, bufs, token = _exchange_start("gather_start", list(shards) + lands, 4 * k, direct, start_after)

    def finish():
        def wait_ici(refs, send_sems, recv_sems):
            for i, cp in enumerate(direct(refs, send_sems, recv_sems)):
                if i % 4:
                    cp.wait_recv()

        send_b, recv_b, bufs2 = _exchange_mid("gather_pass", bufs, [send_a, recv_a], 3 * k, wait_ici, passed,
                                              mid_after())

        def wait_all(refs, sa, ra, sb, rb):
            for i, cp in enumerate(direct(refs, sa, ra)):
                cp.wait_send()
                if i % 4 == 0:
                    cp.wait_recv()
            for cp in passed(refs, sb, rb):
                cp.wait()

        return _exchange_wait("gather_wait", bufs2, [send_a, recv_a, send_b, recv_b], wait_all)[k:]

    return token, finish


def _pair_copies(k):
    def copies(refs, send_sems, recv_sems):
        x, y, c = _position()
        return [pltpu.make_async_remote_copy(
            src_ref=refs[a].at[2 * ch + 1 - c], dst_ref=refs[k + a].at[ch], send_sem=send_sems.at[4 * a + ch],
            recv_sem=recv_sems.at[4 * a + ch], device_id=(x, y, 1 - c), device_id_type=MESH)
            for a in range(k) for ch in range(4)]
    return copies


def _chip_copies(k):
    def copies(refs, send_sems, recv_sems):
        x, y, c = _position()
        return [pltpu.make_async_remote_copy(
            src_ref=refs[a].at[2 * px + py], dst_ref=refs[k + a].at[rel], send_sem=send_sems.at[3 * a + rel],
            recv_sem=recv_sems.at[3 * a + rel], device_id=(px, py, c), device_id_type=MESH)
            for a in range(k) for rel, (px, py) in enumerate([(1 - x, y), (x, 1 - y), (1 - x, 1 - y)])]
    return copies


def _symmetric_exchange(name, srcs, n_land, copies_of):
    k = len(srcs)
    lands = [lax.empty((n_land,) + s.shape[1:], s.dtype) for s in srcs]
    copies = copies_of(k)
    send_sems, recv_sems, bufs, token = _exchange_start(name + "_start", list(srcs) + lands, n_land * k, copies)

    def finish(after):
        def wait_all(refs, ss, rs):
            for cp in copies(refs, ss, rs):
                cp.wait()

        done = _exchange_wait(name + "_wait", bufs, [send_sems, recv_sems], wait_all, after)
        return done[:k], done[k:]

    return token, finish


def _pair_add(full, recv, wire):
    _, r, c_ = full.shape

    def body(f_ref, r_ref, pw_ref, own_ref):
        ch = pl.program_id(0)
        x, y, c = _position()
        tot = f_ref[0, c] + r_ref[0]
        pw_ref[0] = tot.astype(pw_ref.dtype)

        @pl.when(ch == 2 * x + y)
        def _():
            own_ref[...] = tot

    return pl.pallas_call(
        body, name="grad_pair_add", grid=(4,),
        out_shape=(jax.ShapeDtypeStruct((4, r, c_), wire), jax.ShapeDtypeStruct((r, c_), F32)),
        in_specs=[pl.BlockSpec((1, 2, r, c_), lambda i: (i, 0, 0, 0)), pl.BlockSpec((1, r, c_), lambda i: (i, 0, 0))],
        out_specs=(pl.BlockSpec((1, r, c_), lambda i: (i, 0, 0)), pl.BlockSpec((r, c_), lambda i: (0, 0))),
        compiler_params=_params("arbitrary"),
    )(full.reshape(4, 2, r, c_), recv)


def _adamw_math(w, g, m, v):
    m = ADAM_B1 * m + (1.0 - ADAM_B1) * g
    v = ADAM_B2 * v + (1.0 - ADAM_B2) * (g * g)
    m_hat = m / (1.0 - ADAM_B1 ** ADAM_STEP)
    v_hat = v / (1.0 - ADAM_B2 ** ADAM_STEP)
    delta = -ADAM_LR * (m_hat / (jnp.sqrt(v_hat) + ADAM_EPS) + ADAM_WD * w)
    return delta, m, v


def _row_tile(r):
    for t in (256, 176, 128):
        if r % t == 0 and r > t:
            return t
    return r


def _adamw(g_own, recv, w, m, v):
    r, c_ = w.shape
    t = _row_tile(r)
    blk = pl.BlockSpec((t, c_), lambda i: (i, 0))

    def body(g_ref, r_ref, w_ref, m_ref, v_ref, go_ref, d_ref, mo_ref, vo_ref):
        g = g_ref[...]
        for i in range(3):
            g = g + r_ref[i].astype(F32)
        go_ref[...] = g
        d_ref[...], mo_ref[...], vo_ref[...] = _adamw_math(w_ref[...], g, m_ref[...], v_ref[...])

    return pl.pallas_call(
        body, name="adamw", grid=(r // t,), out_shape=(jax.ShapeDtypeStruct((r, c_), F32),) * 4,
        in_specs=[blk, pl.BlockSpec((3, t, c_), lambda i: (0, i, 0)), blk, blk, blk], out_specs=(blk,) * 4,
        compiler_params=_params("parallel"),
    )(g_own, recv, w, m, v)


def _adamw_transposed(g_own, recv, w, m, v):
    r = g_own.shape[0]
    full, rem = divmod(r, LANES)
    nat = pl.BlockSpec((LANES, r), lambda i: (i, 0))

    def body(g_ref, r_ref, w_ref, m_ref, v_ref, go_ref, d_ref, mo_ref, vo_ref, gt_ref):
        gsum = g_ref[...]
        for i in range(3):
            gsum = gsum + r_ref[i].astype(F32)
        for b in range(full):
            gt_ref[:, b * LANES:(b + 1) * LANES] = gsum[b * LANES:(b + 1) * LANES, :].T
        if rem:
            tail = jnp.concatenate([gsum[full * LANES:r, :], jnp.zeros((LANES - rem, LANES), F32)], axis=0).T
            gt_ref[:, full * LANES:r] = tail[:, 0:rem]
        g = gt_ref[...]
        go_ref[...] = g
        d_ref[...], mo_ref[...], vo_ref[...] = _adamw_math(w_ref[...], g, m_ref[...], v_ref[...])

    return pl.pallas_call(
        body, name="adamw_transposed", grid=(D_MODEL // LANES,),
        out_shape=(jax.ShapeDtypeStruct((D_MODEL, r), F32),) * 4,
        in_specs=[pl.BlockSpec((r, LANES), lambda i: (0, i)), pl.BlockSpec((3, r, LANES), lambda i: (0, 0, i)),
                  nat, nat, nat],
        out_specs=(nat,) * 4, scratch_shapes=[pltpu.VMEM((LANES, r), F32)], compiler_params=_params("parallel"),
    )(g_own, recv, w, m, v)


SMALL_ROWS = 88


def _small_allreduce_adamw(gp, wp, mp, vp):
    def body(g_ref, w_ref, m_ref, v_ref, go_ref, d_ref, mo_ref, vo_ref, slots_ref, send_sems, recv_sems):
        x, y, c = _position()
        me = 4 * x + 2 * y + c
        slots_ref[me] = g_ref[...]
        cps = []
        for rel in range(1, N_DEV):
            fx, fy, fc = (rel >> 2) & 1, (rel >> 1) & 1, rel & 1
            to = (1 - x if fx else x, 1 - y if fy else y, 1 - c if fc else c)
            cps.append(pltpu.make_async_remote_copy(
                src_ref=g_ref, dst_ref=slots_ref.at[me], send_sem=send_sems.at[rel - 1],
                recv_sem=recv_sems.at[rel - 1], device_id=to, device_id_type=MESH))
        for cp in cps:
            cp.start()
        for cp in cps:
            cp.wait()
        g = slots_ref[0]
        for i in range(1, N_DEV):
            g = g + slots_ref[i]
        go_ref[...] = g
        d_ref[...], mo_ref[...], vo_ref[...] = _adamw_math(w_ref[...], g, m_ref[...], v_ref[...])

    vm = pl.BlockSpec(memory_space=pltpu.VMEM)
    return pl.pallas_call(
        body, name="small_allreduce_adamw", out_shape=(jax.ShapeDtypeStruct((SMALL_ROWS, LANES), F32),) * 4,
        in_specs=[vm] * 4, out_specs=(vm,) * 4,
        scratch_shapes=[pltpu.VMEM((N_DEV, SMALL_ROWS, LANES), F32), pltpu.SemaphoreType.DMA((N_DEV - 1,)),
                        pltpu.SemaphoreType.DMA((N_DEV - 1,))],
    )(gp, wp, mp, vp)


SMALL = ("attn_norm", "b_gate", "pool_scale", "q_norm", "k_norm", "sinks", "ffn_norm", "conv_b")
SHARDED = ("w_in", "w_pool", "w_out", "w_up", "conv_w", "w_down")
WEIGHTS = ("attn_norm", "w_in", "b_gate", "w_pool", "pool_scale", "q_norm", "k_norm", "sinks", "w_out", "ffn_norm",
           "w_up", "conv_w", "conv_b", "w_down")


N_SMALL = 10896


def _pack_small(d, loss=None):
    parts = [d[n].reshape(-1) for n in SMALL] + ([] if loss is None else [loss.reshape(1)])
    flat = jnp.concatenate(parts)
    return jnp.pad(flat, (0, SMALL_ROWS * LANES - flat.shape[0])).reshape(SMALL_ROWS, LANES)


def _unpack_small(p, like):
    flat, out, o = p.reshape(-1), {}, 0
    for n in SMALL:
        out[n] = flat[o:o + like[n].size].reshape(like[n].shape)
        o += like[n].size
    return out


def kernel(x, positions, attn_norm, w_in, b_gate, w_pool, pool_scale, q_norm, k_norm, sinks, w_out, ffn_norm, w_up, conv_w, conv_b, w_down, loss_target, m_attn_norm, m_w_in, m_b_gate, m_w_pool, m_pool_scale, m_q_norm, m_k_norm, m_sinks, m_w_out, m_ffn_norm, m_w_up, m_conv_w, m_conv_b, m_w_down, v_attn_norm, v_w_in, v_b_gate, v_w_pool, v_pool_scale, v_q_norm, v_k_norm, v_sinks, v_w_out, v_ffn_norm, v_w_up, v_conv_w, v_conv_b, v_w_down):
    w = dict(attn_norm=attn_norm, w_in=w_in, b_gate=b_gate, w_pool=w_pool, pool_scale=pool_scale, q_norm=q_norm,
             k_norm=k_norm, sinks=sinks, w_out=w_out, ffn_norm=ffn_norm, w_up=w_up, conv_w=conv_w, conv_b=conv_b,
             w_down=w_down)
    m = dict(attn_norm=m_attn_norm, w_in=m_w_in, b_gate=m_b_gate, w_pool=m_w_pool, pool_scale=m_pool_scale,
             q_norm=m_q_norm, k_norm=m_k_norm, sinks=m_sinks, w_out=m_w_out, ffn_norm=m_ffn_norm, w_up=m_w_up,
             conv_w=m_conv_w, conv_b=m_conv_b, w_down=m_w_down)
    v = dict(attn_norm=v_attn_norm, w_in=v_w_in, b_gate=v_b_gate, w_pool=v_w_pool, pool_scale=v_pool_scale,
             q_norm=v_q_norm, k_norm=v_k_norm, sinks=v_sinks, w_out=v_w_out, ffn_norm=v_ffn_norm, w_up=v_w_up,
             conv_w=v_conv_w, conv_b=v_conv_b, w_down=v_w_down)
    seq = x.shape[1]
    tm = 256
    tk = min(seq, 1024)
    xs, target, pos_col = x[0], loss_target[0], positions.reshape(seq, 1)
    invf, bd, q_norm_t, k_norm_t, sink_rows = _attention_constants(q_norm, k_norm, sinks)
    out = {}
    nat = {"w_in": (D_MODEL, 544), "w_pool": (128, POOL_GROUP), "w_out": (128, D_MODEL), "w_up": (D_MODEL, 704),
           "conv_w": (3, 704), "w_down": (352, D_MODEL)}

    def update(names, owns, recvs):
        for name, own, recv in zip(names, owns, recvs):
            w2, m2, v2 = (t[name].reshape(nat[name]) for t in (w, m, v))
            if name in ("w_in", "w_up"):
                res = _adamw_transposed(own, recv, w2, m2, v2)
            else:
                res = _adamw(own, recv, w2, m2, v2)
            out[name] = [t.reshape(w[name].shape) for t in res]

    (g_win,) = _all_gather([w_in[0].T.astype(MM)])
    win_t = g_win.reshape(IN_WIDTH, D_MODEL)
    fwd = {}
    token, gather_rest = _all_gather_behind(
        [w_pool[0].astype(MM).reshape(128, POOL_GROUP), w_out[0].astype(MM), w_up[0].T.astype(MM), conv_w[0],
         w_down[0].astype(MM)], g_win, lambda: fwd["b"])

    tabs = _rope_tables(pos_col, invf)
    h, u, q, kv, g = _inproj_fwd(xs, attn_norm + token[0:1, 0:1], win_t, b_gate, tm)
    fwd["b"] = b = _attn_fwd(q, kv, tabs, q_norm_t, k_norm_t, sink_rows, bd)
    g_wpool, g_wout, g_wup, g_convw, g_wdown = gather_rest()
    wpool = g_wpool.reshape(N_DEV, 4, 32, POOL_GROUP).transpose(1, 0, 2, 3).reshape(4, POOL_GROUP, POOL_GROUP)
    wout = g_wout.reshape(D_MODEL, D_MODEL)
    wup_t = g_wup.reshape(2 * D_FF, D_MODEL)
    convw = g_convw.transpose(1, 0, 2).reshape(3, 2 * D_FF)
    wdown = g_wdown.reshape(D_FF, D_MODEL)
    a = _pool_fwd(u, wpool, pool_scale, tm)
    x1, mix = _mix_out_fwd(xs, g, a, b, wout, tm)
    h2, uff, dy, dyb, lossp = _ffn_fwd(x1, ffn_norm, wup_t, convw, conv_b, wdown, target, tm)

    du, act, d_conv_w, d_conv_b = _ffn_bwd_a(dyb, uff, convw, conv_b, wdown, tm)
    d_wdown = _matmul_tn(act, dyb, FF_CHUNK, tk, "dw_down")
    dx1, dx1b, d_ffn_norm = _ffn_bwd_b(du, wup_t, x1, ffn_norm, dy, tm)
    d_wup_t = _matmul_tn(du, h2, 512, tk, "dw_up")
    late = ("w_down", "w_up", "conv_w")
    late_wire = (WIRE, WIRE, F32)
    late_full = [d_wdown.reshape(N_DEV, 352, D_MODEL), d_wup_t.reshape(N_DEV, 704, D_MODEL),
                 d_conv_w.reshape(3, N_DEV, 704).transpose(1, 0, 2)]
    token, late_pair = _symmetric_exchange("late_pair", late_full, 4, _pair_copies)
    d_wout = _matmul_tn(mix, dx1b, 512, tk, "dw_out", after=token)
    da, db, dzg, d_b_gate = _mix_bwd(dx1b, wout, g, a, b, tm)
    late_pw, late_own = zip(*[_pair_add(f, r, wd) for f, r, wd in zip(*late_pair(dzg), late_wire)])
    token, late_chip = _symmetric_exchange("late_chip", list(late_pw), 3, _chip_copies)
    d_win_t = _matmul_tn(dzg, h, 256, tk, "dw_in_g", after=token, into=IN_WIDTH, row0=O_G)
    dzu, d_wpool, d_pool_scale = _pool_bwd(u, da, wpool, pool_scale, tm)
    d_win_t = _matmul_tn(dzu, h, 512, tk, "dw_in_u", into=d_win_t, row0=O_U)
    dzq, dzkv, d_q_norm, d_k_norm, d_sinks = _attn_bwd(q, kv, db, tabs, q_norm_t, k_norm_t, sink_rows, bd)
    d_win_t = _matmul_tn(dzq, h, 512, tk, "dw_in_q", into=d_win_t, row0=O_Q)
    d_win_t = _matmul_tn(dzkv, h, 256, tk, "dw_in_kv", into=d_win_t, row0=O_KV)
    early = ("w_in", "w_pool", "w_out")
    early_full = [d_win_t.reshape(N_DEV, 544, D_MODEL),
                  d_wpool.reshape(4, N_DEV, 32, POOL_GROUP).transpose(1, 0, 2, 3).reshape(N_DEV, 128, POOL_GROUP),
                  d_wout.reshape(N_DEV, 128, D_MODEL)]
    token, early_pair = _symmetric_exchange("early_pair", early_full, 4, _pair_copies)
    update(late, late_own, late_chip(token)[1])
    early_pw, early_own = zip(*[_pair_add(f, r, WIRE) for f, r in zip(*early_pair(out["w_up"][1]))])
    token, early_chip = _symmetric_exchange("early_chip", list(early_pw), 3, _chip_copies)
    grad_x, d_attn_norm = _inproj_bwd(dzu, dzq, dzkv, dzg, win_t, xs, attn_norm, dx1, tm, after=token)
    gr = dict(attn_norm=d_attn_norm, b_gate=d_b_gate, pool_scale=d_pool_scale, q_norm=d_q_norm, k_norm=d_k_norm,
              sinks=d_sinks[:, 0:N_Q_HEADS], ffn_norm=d_ffn_norm, conv_b=d_conv_b)
    small = _small_allreduce_adamw(_pack_small(gr, lossp[0, 0]), _pack_small(w), _pack_small(m), _pack_small(v))
    loss = small[0].reshape(-1)[N_SMALL]
    unpacked = [_unpack_small(p, w) for p in small]
    for name in SMALL:
        out[name] = [t[name] for t in unpacked]
    update(early, early_own, early_chip(small[1])[1])

    return (loss, grad_x[None], *[out[n][0] for n in WEIGHTS], *[out[n][1] for n in WEIGHTS],
            *[out[n][2] for n in WEIGHTS], *[out[n][3] for n in WEIGHTS])
```

```python
import functools

import jax
import jax.numpy as jnp
from jax import lax
from jax.experimental import pallas as pl
from jax.experimental.pallas import tpu as pltpu

F32 = jnp.float32
MM = jnp.bfloat16
WIRE = jnp.bfloat16

D_MODEL = 1024
D_FF = 2816
HEAD_DIM = 64
N_Q_HEADS = 16
N_KV_HEADS = 2
GQA_GROUP = 8
BLOCK = 128
ROPE_DIM = 16
ROPE_THETA = 500000.0
POOL_WINDOWS = (2, 4, 8, 16)
POOL_GROUP = 256
POOL_HALO = 16
CONV_HALO = 8
EPS = 1e-6
NEG = -1e30
O_U, O_Q, O_KV, O_G, IN_WIDTH = 0, 1024, 2048, 2304, 4352
FF_CHUNK = 1408

ADAM_LR, ADAM_B1, ADAM_B2, ADAM_EPS, ADAM_WD, ADAM_STEP = 0.001, 0.9, 0.999, 1e-08, 0.01, 10

N_DEV = 8
LANES = 128
VMEM_LIMIT_BYTES = 56 * 1024 * 1024
MESH = pl.DeviceIdType.MESH


def _params(*sem):
    return pltpu.CompilerParams(dimension_semantics=sem, vmem_limit_bytes=VMEM_LIMIT_BYTES)


def _resident(shape):
    nd = len(shape)
    return pl.BlockSpec(shape, lambda *_: (0,) * nd, pipeline_mode=pl.Buffered(1))


def _rows(tm, width):
    return pl.BlockSpec((tm, width), lambda i: (i, 0))


def _mm(a, b):
    return jnp.dot(a.astype(MM), b.astype(MM), preferred_element_type=F32)


def _mm_nt(a, b):
    return lax.dot_general(a.astype(MM), b.astype(MM), (((1,), (1,)), ((), ())), preferred_element_type=F32)


def _mm_tn(a, b):
    return lax.dot_general(a.astype(MM), b.astype(MM), (((0,), (0,)), ((), ())), preferred_element_type=F32)


def _rmsnorm_fwd(x, g):
    r = lax.rsqrt(jnp.mean(x * x, axis=-1, keepdims=True) + EPS)
    return x * r * g, r


def _rmsnorm_bwd(x, r, g, dy):
    xn = x * r
    dxn = dy * g
    dx = r * (dxn - xn * jnp.mean(dxn * xn, axis=-1, keepdims=True))
    return dx, dy * xn


def _group_sum64(v, bd):
    hi = v.astype(MM)
    lo = (v - hi.astype(F32)).astype(MM)
    outs = []
    for t in range(v.shape[1] // LANES):
        sl = slice(LANES * t, LANES * (t + 1))
        outs.append(jnp.dot(hi[:, sl], bd, preferred_element_type=F32)
                    + jnp.dot(lo[:, sl], bd, preferred_element_type=F32))
    return outs[0] if len(outs) == 1 else jnp.concatenate(outs, axis=1)


def _head_norm_fwd(x, g, bd):
    r = lax.rsqrt(_group_sum64(x * x, bd) * (1.0 / HEAD_DIM) + EPS)
    return x * r * g, r


def _head_norm_bwd(x, r, g, dy, bd):
    xn = x * r
    dxn = dy * g
    dx = r * (dxn - xn * (_group_sum64(dxn * xn, bd) * (1.0 / HEAD_DIM)))
    return dx, dy * xn


def _rope(x, c, s1, s2):
    w = x.shape[1]
    return x * c + pltpu.roll(x, w - ROPE_DIM // 2, 1) * s1 + pltpu.roll(x, ROPE_DIM // 2, 1) * s2


def _rope_bwd(dy, c, s1, s2):
    w = dy.shape[1]
    return dy * c + pltpu.roll(dy * s1, ROPE_DIM // 2, 1) + pltpu.roll(dy * s2, w - ROPE_DIM // 2, 1)


def _tile_lanes(t, reps):
    return t if reps == 1 else jnp.concatenate([t] * reps, axis=1)


def _rope_tables(pos_col, invf):
    s = pos_col.shape[0]
    tm = min(s, 1024)

    def body(pos_ref, invf_ref, c_ref, s1_ref, s2_ref):
        ang = pos_ref[...].astype(F32) * invf_ref[...]
        lane = lax.broadcasted_iota(jnp.int32, ang.shape, 1) % HEAD_DIM
        sn = jnp.sin(ang)
        c_ref[...] = jnp.cos(ang)
        s1_ref[...] = jnp.where(lane < ROPE_DIM // 2, -sn, 0.0)
        s2_ref[...] = jnp.where((lane >= ROPE_DIM // 2) & (lane < ROPE_DIM), sn, 0.0)

    out = jax.ShapeDtypeStruct((s, LANES), F32)
    return pl.pallas_call(
        body, name="rope_tables", grid=(s // tm,), out_shape=(out, out, out),
        in_specs=[_rows(tm, 1), _resident((1, LANES))],
        out_specs=(_rows(tm, LANES),) * 3, compiler_params=_params("parallel"),
    )(pos_col, invf)


def _inproj_fwd(x, attn_norm, win_t, b_gate, tm):
    s = x.shape[0]

    def body(x_ref, gn_ref, w_ref, bg_ref, h_ref, u_ref, q_ref, kv_ref, g_ref):
        h, _ = _rmsnorm_fwd(x_ref[...], gn_ref[...])
        h = h.astype(MM)
        h_ref[...] = h
        u_ref[...] = _mm_nt(h, w_ref[O_U:O_Q, :])
        q_ref[...] = _mm_nt(h, w_ref[O_Q:O_KV, :])
        kv_ref[...] = _mm_nt(h, w_ref[O_KV:O_G, :])
        g_ref[...] = jax.nn.sigmoid(_mm_nt(h, w_ref[O_G:IN_WIDTH, :]) + bg_ref[...])

    sd = jax.ShapeDtypeStruct
    return pl.pallas_call(
        body, name="inproj_fwd", grid=(s // tm,),
        out_shape=(sd((s, D_MODEL), MM), sd((s, 1024), F32), sd((s, 1024), F32), sd((s, 256), F32),
                   sd((s, 2048), F32)),
        in_specs=[_rows(tm, D_MODEL), _resident((1, D_MODEL)), _resident((IN_WIDTH, D_MODEL)), _resident((1, 2048))],
        out_specs=(_rows(tm, D_MODEL), _rows(tm, 1024), _rows(tm, 1024), _rows(tm, 256), _rows(tm, 2048)),
        compiler_params=_params("parallel"),
    )(x, attn_norm, win_t, b_gate)


def _pooled(ext_ref, tm, row0):
    t = (row0 + lax.broadcasted_iota(jnp.int32, (tm, 1), 0)).astype(F32)
    out = []
    for gi, w in enumerate(POOL_WINDOWS):
        cols = slice(gi * POOL_GROUP, (gi + 1) * POOL_GROUP)
        acc = ext_ref[pl.ds(POOL_HALO, tm), cols]
        for k in range(1, w):
            acc = acc + ext_ref[pl.ds(POOL_HALO - k, tm), cols]
        cnt = jnp.minimum(t + 1.0, float(w))
        out.append(acc / cnt - ext_ref[pl.ds(POOL_HALO, tm), cols])
    return out


def _pool_fwd(u, wpool, pool_scale, tm):
    s = u.shape[0]
    hb = tm // POOL_HALO

    def body(u_ref, halo_ref, wp_ref, ps_ref, a_ref, ext_ref):
        i = pl.program_id(0)
        ext_ref[pl.ds(0, POOL_HALO), :] = jnp.where(i > 0, halo_ref[...], 0.0)
        ext_ref[pl.ds(POOL_HALO, tm), :] = u_ref[...]
        pooled = _pooled(ext_ref, tm, i * tm)
        for gi in range(4):
            cols = slice(gi * POOL_GROUP, (gi + 1) * POOL_GROUP)
            a_ref[:, cols] = _mm(pooled[gi], wp_ref[gi]) * ps_ref[:, cols]

    return pl.pallas_call(
        body, name="pool_fwd", grid=(s // tm,), out_shape=jax.ShapeDtypeStruct((s, 1024), F32),
        in_specs=[_rows(tm, 1024), pl.BlockSpec((POOL_HALO, 1024), lambda i: (jnp.maximum(i * hb - 1, 0), 0)),
                  _resident((4, POOL_GROUP, POOL_GROUP)), _resident((1, 1024))],
        out_specs=_rows(tm, 1024), scratch_shapes=[pltpu.VMEM((POOL_HALO + tm, 1024), F32)],
        compiler_params=_params("parallel"),
    )(u, u, wpool, pool_scale)


PAIRS = GQA_GROUP // 2
PAIR_COLS = PAIRS * BLOCK


def _attn_mask_t(n):
    shape = (2 * BLOCK, PAIR_COLS)
    kj = lax.broadcasted_iota(jnp.int32, shape, 0)
    qi = lax.broadcasted_iota(jnp.int32, shape, 1) % BLOCK
    return (kj > qi) & (kj <= qi + BLOCK) & ((n > 0) | (kj >= BLOCK))


def _stack_pairs(x, hk):
    return jnp.concatenate([x[:, (PAIRS * hk + j) * LANES:(PAIRS * hk + j + 1) * LANES] for j in range(PAIRS)], axis=0)


def _parity_bands(t, hk):
    low = lax.broadcasted_iota(jnp.int32, t.shape, 1) < HEAD_DIM
    own = jnp.where(low if hk == 0 else ~low, t, 0.0)
    other = pltpu.roll(own, HEAD_DIM, 1)
    return (own, other) if hk == 0 else (other, own)


def _fold_parity(even, odd, hk):
    low = lax.broadcasted_iota(jnp.int32, even.shape, 1) < HEAD_DIM
    comb = jnp.where(low, even, odd)
    comb = comb + pltpu.roll(comb, HEAD_DIM, 1)
    return jnp.where(low if hk == 0 else ~low, comb, 0.0)


def _softmax_sink_t(s, sink):
    m = jnp.maximum(jnp.max(s, axis=0, keepdims=True), sink)
    p = jnp.exp(s - m)
    es = jnp.exp(sink - m)
    inv = 1.0 / (jnp.sum(p, axis=0, keepdims=True) + es)
    return p * inv, es * inv


def _attn_fwd(q, kv, tabs, q_norm_t, k_norm_t, sink_rows, bd):
    s = q.shape[0]
    nb = s // BLOCK
    scale = HEAD_DIM ** -0.5
    cur = lambda n: (n, 0)
    prv = lambda n: (jnp.maximum(n - 1, 0), 0)

    def body(q_ref, kvc_ref, kvp_ref, c_ref, s1_ref, s2_ref, cp_ref, s1p_ref, s2p_ref, qn_ref, kn_ref, sink_ref,
             bd_ref, o_ref):
        n = pl.program_id(0)
        bdm = bd_ref[...]
        c, s1, s2 = c_ref[...], s1_ref[...], s2_ref[...]
        qh, _ = _head_norm_fwd(q_ref[...], qn_ref[...], bdm)
        qr = (_rope(qh, _tile_lanes(c, 8), _tile_lanes(s1, 8), _tile_lanes(s2, 8)) * scale).astype(MM)
        kc, _ = _head_norm_fwd(kvc_ref[:, 0:128], kn_ref[...], bdm)
        kp, _ = _head_norm_fwd(kvp_ref[:, 0:128], kn_ref[...], bdm)
        k2 = jnp.concatenate([_rope(kp, cp_ref[...], s1p_ref[...], s2p_ref[...]), _rope(kc, c, s1, s2)], axis=0)
        v2 = jnp.concatenate([kvp_ref[:, 128:256], kvc_ref[:, 128:256]], axis=0)
        mask = _attn_mask_t(n)
        for hk in range(N_KV_HEADS):
            qs = _stack_pairs(qr, hk)
            ot = jnp.zeros((LANES, PAIR_COLS), F32)
            for par, (kb, vb) in enumerate(zip(_parity_bands(k2, hk), _parity_bands(v2, hk))):
                sc = jnp.where(mask, _mm_nt(kb, qs), NEG)
                pr, _ = _softmax_sink_t(sc, sink_ref[hk, par])
                ot = ot + _mm(vb.T, pr)
            for j in range(PAIRS):
                col = (PAIRS * hk + j) * LANES
                o_ref[:, col:col + LANES] = ot[:, j * BLOCK:(j + 1) * BLOCK].T

    tab = lambda im: pl.BlockSpec((BLOCK, LANES), im)
    return pl.pallas_call(
        body, name="attn_fwd", grid=(nb,), out_shape=jax.ShapeDtypeStruct((s, 1024), F32),
        in_specs=[pl.BlockSpec((BLOCK, 1024), cur), pl.BlockSpec((BLOCK, 256), cur), pl.BlockSpec((BLOCK, 256), prv),
                  tab(cur), tab(cur), tab(cur), tab(prv), tab(prv), tab(prv),
                  _resident((1, 1024)), _resident((1, 128)), _resident((N_KV_HEADS, 2, 1, PAIR_COLS)),
                  _resident((LANES, LANES))],
        out_specs=pl.BlockSpec((BLOCK, 1024), cur), compiler_params=_params("parallel"),
    )(q, kv, kv, *tabs, *tabs, q_norm_t, k_norm_t, sink_rows, bd)


def _mix_out_fwd(x, g, a, b, wout, tm):
    s = x.shape[0]

    def body(x_ref, g_ref, a_ref, b_ref, w_ref, x1_ref, mix_ref):
        mix = (g_ref[:, 0:1024] * a_ref[...] + g_ref[:, 1024:2048] * b_ref[...]).astype(MM)
        mix_ref[...] = mix
        x1_ref[...] = x_ref[...] + _mm(mix, w_ref[...])

    return pl.pallas_call(
        body, name="mix_out_fwd", grid=(s // tm,),
        out_shape=(jax.ShapeDtypeStruct((s, D_MODEL), F32), jax.ShapeDtypeStruct((s, D_MODEL), MM)),
        in_specs=[_rows(tm, 1024), _rows(tm, 2048), _rows(tm, 1024), _rows(tm, 1024), _resident((1024, 1024))],
        out_specs=(_rows(tm, 1024), _rows(tm, 1024)), compiler_params=_params("parallel"),
    )(x, g, a, b, wout)


SHIFT_ROWS = 16


def _sublane_major_matrices(tm):
    r = jnp.arange(tm)
    pm = (r[None, :] == ((tm // 8) * (r % 8) + r // 8)[:, None]).astype(MM)
    return pm, pm.T


def _to_sublane_major(pm, v):
    return jnp.dot(pm, v, preferred_element_type=F32).astype(MM)


def _to_time_order(pmt, v):
    hi = v.astype(MM)
    r1 = v - hi.astype(F32)
    mid = r1.astype(MM)
    lo = (r1 - mid.astype(F32)).astype(MM)
    dot = functools.partial(jnp.dot, preferred_element_type=F32)
    return dot(pmt, hi) + dot(pmt, mid) + dot(pmt, lo)


def _step_back(vreg_rows, before):
    sub = lax.broadcasted_iota(jnp.int32, vreg_rows.shape, 0)
    return jnp.where(sub == 0, before[7:8, :], pltpu.roll(vreg_rows, 1, 0))


def _step_ahead(vreg_rows, after):
    sub = lax.broadcasted_iota(jnp.int32, vreg_rows.shape, 0)
    return jnp.where(sub == 7, after[0:1, :], pltpu.roll(vreg_rows, 7, 0))


def _fill_back_rows(ext_ref, before, tm, cols):
    last = ext_ref[pl.ds(SHIFT_ROWS + tm - 8, 8), cols]
    pen = ext_ref[pl.ds(SHIFT_ROWS + tm - 16, 8), cols]
    ext_ref[pl.ds(8, 8), cols] = _step_back(last, before[8:16, :])
    ext_ref[pl.ds(0, 8), cols] = _step_back(pen, before[0:8, :])


def _conv_glu(ext_ref, cw_ref, cb_ref, tm, c):
    out = []
    for base in (c * FF_CHUNK, D_FF + c * FF_CHUNK):
        cols = slice(base, base + FF_CHUNK)
        y = cb_ref[:, cols] + cw_ref[0:1, cols] * ext_ref[pl.ds(0, tm), cols]
        y = y + cw_ref[1:2, cols] * ext_ref[pl.ds(8, tm), cols]
        y = y + cw_ref[2:3, cols] * ext_ref[pl.ds(SHIFT_ROWS, tm), cols]
        out.append(y)
    return out


def _ffn_fwd(x1, ffn_norm, wup_t, conv_w, conv_b, wdown, target, tm):
    s = x1.shape[0]
    inv_d = 1.0 / D_MODEL
    pm, pmt = _sublane_major_matrices(tm)

    def body(x1_ref, gn_ref, wu_ref, cw_ref, cb_ref, wd_ref, tgt_ref, pm_ref, pmt_ref, h2_ref, u_ref, dy_ref, dyb_ref,
             loss_ref, ext_ref, carry_ref):
        i = pl.program_id(0)

        @pl.when(i == 0)
        def _():
            carry_ref[...] = jnp.zeros_like(carry_ref)
            loss_ref[...] = jnp.zeros_like(loss_ref)

        x1 = x1_ref[...]
        h2, _ = _rmsnorm_fwd(x1, gn_ref[...])
        h2 = _to_sublane_major(pm_ref[...], h2.astype(MM))
        h2_ref[...] = h2
        for c in range(4):
            cols = slice(c * FF_CHUNK, (c + 1) * FF_CHUNK)
            uc = _mm_nt(h2, wu_ref[cols, :])
            u_ref[:, cols] = uc
            ext_ref[pl.ds(SHIFT_ROWS, tm), cols] = uc
            _fill_back_rows(ext_ref, carry_ref[:, cols], tm, cols)
            carry_ref[:, cols] = uc[tm - SHIFT_ROWS:tm, :]
        down = jnp.zeros((tm, D_MODEL), F32)
        for c in range(2):
            gate, val = _conv_glu(ext_ref, cw_ref, cb_ref, tm, c)
            act = gate * jax.nn.sigmoid(gate) * val
            down = down + _mm(act, wd_ref[c * FF_CHUNK:(c + 1) * FF_CHUNK, :])
        err = x1 + _to_time_order(pmt_ref[...], down) - tgt_ref[...]
        loss_ref[...] += jnp.full(loss_ref.shape, 0.5 * inv_d * jnp.sum(err * err), F32)
        dy = err * inv_d
        dy_ref[...] = dy
        dyb_ref[...] = _to_sublane_major(pm_ref[...], dy.astype(MM))

    sd = jax.ShapeDtypeStruct
    return pl.pallas_call(
        body, name="ffn_fwd", grid=(s // tm,),
        out_shape=(sd((s, D_MODEL), MM), sd((s, 2 * D_FF), F32), sd((s, D_MODEL), F32), sd((s, D_MODEL), MM),
                   sd((8, LANES), F32)),
        in_specs=[_rows(tm, 1024), _resident((1, 1024)), _resident((2 * D_FF, D_MODEL)), _resident((3, 2 * D_FF)),
                  _resident((1, 2 * D_FF)), _resident((D_FF, D_MODEL)), _rows(tm, 1024), _resident((tm, tm)),
                  _resident((tm, tm))],
        out_specs=(_rows(tm, 1024), _rows(tm, 2 * D_FF), _rows(tm, 1024), _rows(tm, 1024),
                   pl.BlockSpec((8, LANES), lambda i: (0, 0))),
        scratch_shapes=[pltpu.VMEM((SHIFT_ROWS + tm, 2 * D_FF), F32), pltpu.VMEM((SHIFT_ROWS, 2 * D_FF), F32)],
        compiler_params=_params("arbitrary"),
    )(x1, ffn_norm, wup_t, conv_w, conv_b, wdown, target, pm, pmt)


def _ffn_bwd_a(dyb, u, conv_w, conv_b, wdown, tm):
    s = dyb.shape[0]
    nt = s // tm
    hb = tm // SHIFT_ROWS
    rev = lambda i: (nt - 1 - i, 0)

    def body(dy_ref, u_ref, before_ref, cw_ref, cb_ref, wd_ref, du_ref, act_ref, dcw_ref, dcb_ref, ext_ref, extd_ref,
             ahead_ref):
        i = pl.program_id(0)
        first_tile = i == nt - 1

        @pl.when(i == 0)
        def _():
            ahead_ref[...] = jnp.zeros_like(ahead_ref)
            dcw_ref[...] = jnp.zeros_like(dcw_ref)
            dcb_ref[...] = jnp.zeros_like(dcb_ref)

        ext_ref[pl.ds(SHIFT_ROWS, tm), :] = u_ref[...]
        for c in range(4):
            cols = slice(c * FF_CHUNK, (c + 1) * FF_CHUNK)
            _fill_back_rows(ext_ref, jnp.where(first_tile, 0.0, before_ref[:, cols]), tm, cols)
        dy = dy_ref[...]
        for c in range(2):
            gate, val = _conv_glu(ext_ref, cw_ref, cb_ref, tm, c)
            sg = jax.nn.sigmoid(gate)
            sl = gate * sg
            act_ref[:, c * FF_CHUNK:(c + 1) * FF_CHUNK] = (sl * val).astype(MM)
            d_act = _mm_nt(dy, wd_ref[c * FF_CHUNK:(c + 1) * FF_CHUNK, :])
            extd_ref[pl.ds(0, tm), c * FF_CHUNK:(c + 1) * FF_CHUNK] = d_act * val * (sg * (1.0 + gate * (1.0 - sg)))
            extd_ref[pl.ds(0, tm), D_FF + c * FF_CHUNK:D_FF + (c + 1) * FF_CHUNK] = d_act * sl
        for c in range(4):
            cols = slice(c * FF_CHUNK, (c + 1) * FF_CHUNK)
            ahead = ahead_ref[:, cols]
            first2 = extd_ref[pl.ds(0, SHIFT_ROWS), cols]
            extd_ref[pl.ds(tm, 8), cols] = _step_ahead(first2[0:8, :], ahead[0:8, :])
            extd_ref[pl.ds(tm + 8, 8), cols] = _step_ahead(first2[8:16, :], ahead[8:16, :])
            ahead_ref[:, cols] = first2
            d0 = extd_ref[pl.ds(0, tm), cols]
            dcb_ref[:, cols] += jnp.sum(d0, axis=0, keepdims=True)
            for j in range(3):
                dcw_ref[j:j + 1, cols] += jnp.sum(d0 * ext_ref[pl.ds(8 * j, tm), cols], axis=0, keepdims=True)
            du = cw_ref[2:3, cols] * d0 + cw_ref[1:2, cols] * extd_ref[pl.ds(8, tm), cols]
            du = du + cw_ref[0:1, cols] * extd_ref[pl.ds(SHIFT_ROWS, tm), cols]
            du_ref[:, cols] = du.astype(MM)

    sd = jax.ShapeDtypeStruct
    return pl.pallas_call(
        body, name="ffn_bwd_a", grid=(nt,),
        out_shape=(sd((s, 2 * D_FF), MM), sd((s, D_FF), MM), sd((3, 2 * D_FF), F32), sd((1, 2 * D_FF), F32)),
        in_specs=[pl.BlockSpec((tm, D_MODEL), rev), pl.BlockSpec((tm, 2 * D_FF), rev),
                  pl.BlockSpec((SHIFT_ROWS, 2 * D_FF), lambda i: (jnp.maximum((nt - 1 - i) * hb - 1, 0), 0)),
                  _resident((3, 2 * D_FF)), _resident((1, 2 * D_FF)), _resident((D_FF, D_MODEL))],
        out_specs=(pl.BlockSpec((tm, 2 * D_FF), rev), pl.BlockSpec((tm, D_FF), rev),
                   pl.BlockSpec((3, 2 * D_FF), lambda i: (0, 0)), pl.BlockSpec((1, 2 * D_FF), lambda i: (0, 0))),
        scratch_shapes=[pltpu.VMEM((SHIFT_ROWS + tm, 2 * D_FF), F32), pltpu.VMEM((tm + SHIFT_ROWS, 2 * D_FF), F32),
                        pltpu.VMEM((SHIFT_ROWS, 2 * D_FF), F32)],
        compiler_params=_params("arbitrary"),
    )(dyb, u, u, conv_w, conv_b, wdown)


def _after(after):
    tie = [] if after is None else list(after) if isinstance(after, (list, tuple)) else [after]
    return tie, [pl.BlockSpec(memory_space=pl.ANY)] * len(tie)


def _matmul_tn(a, b, tmo, tk, name, after=None, into=None, row0=0):
    s, m = a.shape
    n = b.shape[1]
    nk = s // tk
    tie, tie_spec = _after(after)
    rows = m if into is None else into if isinstance(into, int) else into.shape[0]
    assert row0 % LANES == 0 and tmo % LANES == 0
    grown, grown_spec = ([], []) if into is None or isinstance(into, int) else ([into], [ANY])

    def body(a_ref, b_ref, *rest):
        o_ref = rest[-1]
        k = pl.program_id(1)

        @pl.when(k == 0)
        def _():
            o_ref[...] = jnp.zeros_like(o_ref)

        o_ref[...] += _mm_tn(a_ref[...], b_ref[pl.ds(pl.multiple_of(k * tk, tk), tk), :])

    return pl.pallas_call(
        body, name=name, grid=(m // tmo, nk), out_shape=jax.ShapeDtypeStruct((rows, n), F32),
        in_specs=[pl.BlockSpec((tk, tmo), lambda i, k: (k, i)), _resident((s, n))] + tie_spec + grown_spec,
        out_specs=pl.BlockSpec((pl.Element(tmo), pl.Element(n)), lambda i, k: (pl.multiple_of(row0 + i * tmo, LANES), 0)),
        input_output_aliases={2 + len(tie): 0} if grown else {},
        compiler_params=_params("parallel", "arbitrary"),
    )(a, b, *tie, *grown)


def _ffn_bwd_b(du, wup_t, x1, ffn_norm, dy, tm):
    s = du.shape[0]

    def body(du_ref, wu_ref, x1_ref, gn_ref, dy_ref, pmt_ref, dx1_ref, dx1b_ref, dg_ref):
        @pl.when(pl.program_id(0) == 0)
        def _():
            dg_ref[...] = jnp.zeros_like(dg_ref)

        dh2 = _to_time_order(pmt_ref[...], _mm(du_ref[...], wu_ref[...]))
        x1 = x1_ref[...]
        _, r = _rmsnorm_fwd(x1, gn_ref[...])
        dx, dgr = _rmsnorm_bwd(x1, r, gn_ref[...], dh2)
        dg_ref[...] += jnp.sum(dgr, axis=0, keepdims=True)
        dx1 = dy_ref[...] + dx
        dx1_ref[...] = dx1
        dx1b_ref[...] = dx1.astype(MM)

    return pl.pallas_call(
        body, name="ffn_bwd_b", grid=(s // tm,),
        out_shape=(jax.ShapeDtypeStruct((s, D_MODEL), F32), jax.ShapeDtypeStruct((s, D_MODEL), MM),
                   jax.ShapeDtypeStruct((1, D_MODEL), F32)),
        in_specs=[_rows(tm, 2 * D_FF), _resident((2 * D_FF, D_MODEL)), _rows(tm, 1024), _resident((1, 1024)),
                  _rows(tm, 1024), _resident((tm, tm))],
        out_specs=(_rows(tm, 1024), _rows(tm, 1024), pl.BlockSpec((1, D_MODEL), lambda i: (0, 0))),
        compiler_params=_params("arbitrary"),
    )(du, wup_t, x1, ffn_norm, dy, _sublane_major_matrices(tm)[1])


def _mix_bwd(dx1b, wout, g, a, b, tm):
    s = dx1b.shape[0]

    def body(dx_ref, w_ref, g_ref, a_ref, b_ref, da_ref, db_ref, dzg_ref, dbg_ref):
        @pl.when(pl.program_id(0) == 0)
        def _():
            dbg_ref[...] = jnp.zeros_like(dbg_ref)

        dmix = _mm_nt(dx_ref[...], w_ref[...])
        for half, src, dst in ((0, a_ref, da_ref), (1, b_ref, db_ref)):
            cols = slice(half * 1024, (half + 1) * 1024)
            gt = g_ref[:, cols]
            dst[...] = (dmix * gt).astype(dst.dtype)
            dz = dmix * src[...] * gt * (1.0 - gt)
            dzg_ref[:, cols] = dz.astype(MM)
            dbg_ref[:, cols] += jnp.sum(dz, axis=0, keepdims=True)

    sd = jax.ShapeDtypeStruct
    return pl.pallas_call(
        body, name="mix_bwd", grid=(s // tm,),
        out_shape=(sd((s, 1024), F32), sd((s, 1024), MM), sd((s, 2048), MM), sd((1, 2048), F32)),
        in_specs=[_rows(tm, 1024), _resident((1024, 1024)), _rows(tm, 2048), _rows(tm, 1024), _rows(tm, 1024)],
        out_specs=(_rows(tm, 1024), _rows(tm, 1024), _rows(tm, 2048), pl.BlockSpec((1, 2048), lambda i: (0, 0))),
        compiler_params=_params("arbitrary"),
    )(dx1b, wout, g, a, b)


def _pool_bwd(u, da, wpool, pool_scale, tm, after=None):
    s = u.shape[0]
    nt = s // tm
    hb = tm // POOL_HALO

    tie, tie_spec = _after(after)

    def body(u_ref, uh_ref, da_ref, dah_ref, wp_ref, ps_ref, *rest):
        dzu_ref, dwp_ref, dps_ref, ext_ref, exte_ref = rest[-5:]
        i = pl.program_id(0)

        @pl.when(i == 0)
        def _():
            dwp_ref[...] = jnp.zeros_like(dwp_ref)
            dps_ref[...] = jnp.zeros_like(dps_ref)

        ext_ref[pl.ds(0, POOL_HALO), :] = jnp.where(i > 0, uh_ref[...], 0.0)
        ext_ref[pl.ds(POOL_HALO, tm), :] = u_ref[...]
        pooled = _pooled(ext_ref, tm, i * tm)
        da = da_ref[...]
        dah = jnp.where(i < nt - 1, dah_ref[...], 0.0)
        t = (i * tm + lax.broadcasted_iota(jnp.int32, (tm + POOL_HALO, 1), 0)).astype(F32)
        for gi, w in enumerate(POOL_WINDOWS):
            cols = slice(gi * POOL_GROUP, (gi + 1) * POOL_GROUP)
            pg = pooled[gi].astype(MM)
            wg = wp_ref[gi]
            mixed = _mm(pg, wg)
            dps_ref[:, cols] += jnp.sum(da[:, cols] * mixed, axis=0, keepdims=True)
            dmx = (da[:, cols] * ps_ref[:, cols]).astype(MM)
            dwp_ref[gi] += _mm_tn(pg, dmx)
            dpl = _mm_nt(dmx, wg)
            dplh = _mm_nt(dah[:, cols] * ps_ref[:, cols], wg)
            cnt = jnp.minimum(t + 1.0, float(w))
            exte_ref[pl.ds(0, tm), cols] = dpl / cnt[0:tm]
            exte_ref[pl.ds(tm, POOL_HALO), cols] = dplh / cnt[tm:tm + POOL_HALO]
            acc = exte_ref[pl.ds(0, tm), cols]
            for k in range(1, w):
                acc = acc + exte_ref[pl.ds(k, tm), cols]
            dzu_ref[:, cols] = (acc - dpl).astype(MM)

    sd = jax.ShapeDtypeStruct
    last_halo = s // POOL_HALO - 1
    return pl.pallas_call(
        body, name="pool_bwd", grid=(nt,),
        out_shape=(sd((s, 1024), MM), sd((4, POOL_GROUP, POOL_GROUP), F32), sd((1, 1024), F32)),
        in_specs=[_rows(tm, 1024), pl.BlockSpec((POOL_HALO, 1024), lambda i: (jnp.maximum(i * hb - 1, 0), 0)),
                  _rows(tm, 1024),
                  pl.BlockSpec((POOL_HALO, 1024), lambda i: (jnp.minimum((i + 1) * hb, last_halo), 0)),
                  _resident((4, POOL_GROUP, POOL_GROUP)), _resident((1, 1024))] + tie_spec,
        out_specs=(_rows(tm, 1024), pl.BlockSpec((4, POOL_GROUP, POOL_GROUP), lambda i: (0, 0, 0)),
                   pl.BlockSpec((1, 1024), lambda i: (0, 0))),
        scratch_shapes=[pltpu.VMEM((POOL_HALO + tm, 1024), F32), pltpu.VMEM((tm + POOL_HALO, 1024), F32)],
        compiler_params=_params("arbitrary"),
    )(u, u, da, da, wpool, pool_scale, *tie)


def _attn_bwd(q, kv, db, tabs, q_norm_t, k_norm_t, sink_rows, bd):
    s = q.shape[0]
    nb = s // BLOCK
    scale = HEAD_DIM ** -0.5
    cur = lambda n: (jnp.minimum(n, nb - 1), 0)
    prv = lambda n: (jnp.maximum(n - 1, 0), 0)

    def body(q_ref, kvc_ref, kvp_ref, db_ref, c_ref, s1_ref, s2_ref, cp_ref, s1p_ref, s2p_ref, qn_ref, kn_ref,
             sink_ref, bd_ref, dzq_ref, dzkv_ref, dqn_ref, dkn_ref, dsk_ref,
             carry_ref, tot_ref, dqr_ref, qacc_ref, kacc_ref, sacc_ref):
        n = pl.program_id(0)
        bdm = bd_ref[...]
        kn = kn_ref[...]

        @pl.when(n == 0)
        def _():
            carry_ref[...] = jnp.zeros_like(carry_ref)
            qacc_ref[...] = jnp.zeros_like(qacc_ref)
            kacc_ref[...] = jnp.zeros_like(kacc_ref)
            sacc_ref[...] = jnp.zeros_like(sacc_ref)

        kp_raw = kvp_ref[:, 0:128]
        kph, rp = _head_norm_fwd(kp_raw, kn, bdm)
        cp, s1p, s2p = cp_ref[...], s1p_ref[...], s2p_ref[...]

        @pl.when(n < nb)
        def _():
            c, s1, s2 = c_ref[...], s1_ref[...], s2_ref[...]
            c8, s18, s28 = _tile_lanes(c, 8), _tile_lanes(s1, 8), _tile_lanes(s2, 8)
            q_raw = q_ref[...]
            qh, rq = _head_norm_fwd(q_raw, qn_ref[...], bdm)
            qr = (_rope(qh, c8, s18, s28) * scale).astype(MM)
            kc, _ = _head_norm_fwd(kvc_ref[:, 0:128], kn, bdm)
            k2 = jnp.concatenate([_rope(kph, cp, s1p, s2p), _rope(kc, c, s1, s2)], axis=0)
            v2 = jnp.concatenate([kvp_ref[:, 128:256], kvc_ref[:, 128:256]], axis=0)
            dob = db_ref[...].astype(MM)
            mask = _attn_mask_t(n)
            lane = lax.broadcasted_iota(jnp.int32, (1, LANES), 1)
            dsk = jnp.zeros((1, LANES), F32)
            dk2 = jnp.zeros((2 * BLOCK, LANES), F32)
            dv2 = jnp.zeros((2 * BLOCK, LANES), F32)
            for hk in range(N_KV_HEADS):
                qs = _stack_pairs(qr, hk)
                do = _stack_pairs(dob, hk)
                dqt = jnp.zeros((LANES, PAIR_COLS), F32)
                dkb, dvb = [], []
                for par, (kb, vb) in enumerate(zip(_parity_bands(k2, hk), _parity_bands(v2, hk))):
                    sc = jnp.where(mask, _mm_nt(kb, qs), NEG)
                    pr, psink = _softmax_sink_t(sc, sink_ref[hk, par])
                    dp = _mm_nt(vb, do)
                    coldot = jnp.sum(pr * dp, axis=0, keepdims=True)
                    ds = (pr * (dp - coldot)).astype(MM)
                    dsr = -psink * coldot
                    for j in range(PAIRS):
                        h = hk * GQA_GROUP + 2 * j + par
                        dsk = dsk + jnp.where(lane == h, jnp.sum(dsr[:, j * BLOCK:(j + 1) * BLOCK]), 0.0)
                    dqt = dqt + _mm(kb.T, ds)
                    dkb.append(_mm(ds, qs))
                    dvb.append(_mm(pr, do))
                for j in range(PAIRS):
                    col = (PAIRS * hk + j) * LANES
                    dqr_ref[:, col:col + LANES] = dqt[:, j * BLOCK:(j + 1) * BLOCK].T
                dk2 = dk2 + _fold_parity(dkb[0], dkb[1], hk)
                dv2 = dv2 + _fold_parity(dvb[0], dvb[1], hk)
            tot_ref[:, 0:128] = carry_ref[:, 0:128] + dk2[0:BLOCK, :]
            tot_ref[:, 128:256] = carry_ref[:, 128:256] + dv2[0:BLOCK, :]
            carry_ref[:, 0:128] = dk2[BLOCK:2 * BLOCK, :]
            carry_ref[:, 128:256] = dv2[BLOCK:2 * BLOCK, :]
            sacc_ref[...] += dsk
            dqh = _rope_bwd(dqr_ref[...] * scale, c8, s18, s28)
            dq, dgq = _head_norm_bwd(q_raw, rq, qn_ref[...], dqh, bdm)
            dzq_ref[...] = dq.astype(MM)
            qacc_ref[...] += jnp.sum(dgq, axis=0, keepdims=True)

        @pl.when(n == nb)
        def _():
            tot_ref[...] = carry_ref[...]

        dkh = _rope_bwd(tot_ref[:, 0:128], cp, s1p, s2p)
        dkr, dgk = _head_norm_bwd(kp_raw, rp, kn, dkh, bdm)
        dzkv_ref[:, 0:128] = dkr.astype(MM)
        dzkv_ref[:, 128:256] = tot_ref[:, 128:256].astype(MM)
        kacc_ref[...] += jnp.where(n > 0, jnp.sum(dgk, axis=0, keepdims=True), 0.0)

        @pl.when(n == nb)
        def _():
            fold = qacc_ref[:, 0:HEAD_DIM]
            for h in range(1, N_Q_HEADS):
                fold = fold + qacc_ref[:, h * HEAD_DIM:(h + 1) * HEAD_DIM]
            dqn_ref[...] = fold
            dkn_ref[...] = kacc_ref[:, 0:HEAD_DIM] + kacc_ref[:, HEAD_DIM:2 * HEAD_DIM]
            dsk_ref[...] = sacc_ref[...]

    tab = lambda im: pl.BlockSpec((BLOCK, LANES), im)
    sd = jax.ShapeDtypeStruct
    const = lambda n: (0, 0)
    return pl.pallas_call(
        body, name="attn_bwd", grid=(nb + 1,),
        out_shape=(sd((s, 1024), MM), sd((s, 256), MM), sd((1, HEAD_DIM), F32), sd((1, HEAD_DIM), F32),
                   sd((1, LANES), F32)),
        in_specs=[pl.BlockSpec((BLOCK, 1024), cur), pl.BlockSpec((BLOCK, 256), cur), pl.BlockSpec((BLOCK, 256), prv),
                  pl.BlockSpec((BLOCK, 1024), cur), tab(cur), tab(cur), tab(cur), tab(prv), tab(prv), tab(prv),
                  _resident((1, 1024)), _resident((1, 128)), _resident((N_KV_HEADS, 2, 1, PAIR_COLS)),
                  _resident((LANES, LANES))],
        out_specs=(pl.BlockSpec((BLOCK, 1024), cur), pl.BlockSpec((BLOCK, 256), prv),
                   pl.BlockSpec((1, HEAD_DIM), const), pl.BlockSpec((1, HEAD_DIM), const),
                   pl.BlockSpec((1, LANES), const)),
        scratch_shapes=[pltpu.VMEM((BLOCK, 256), F32), pltpu.VMEM((BLOCK, 256), F32), pltpu.VMEM((BLOCK, 1024), F32),
                        pltpu.VMEM((1, 1024), F32), pltpu.VMEM((1, 128), F32), pltpu.VMEM((1, LANES), F32)],
        compiler_params=_params("arbitrary"),
    )(q, kv, kv, db, *tabs, *tabs, q_norm_t, k_norm_t, sink_rows, bd)


def _inproj_bwd(dzu, dzq, dzkv, dzg, win_t, x, attn_norm, dx1, tm, after=None):
    s = x.shape[0]
    tie, tie_spec = _after(after)

    def body(du_ref, dq_ref, dkv_ref, dg_ref, w_ref, x_ref, gn_ref, dx1_ref, *rest):
        gx_ref, dgn_ref = rest[-2:]

        @pl.when(pl.program_id(0) == 0)
        def _():
            dgn_ref[...] = jnp.zeros_like(dgn_ref)

        dh = _mm(du_ref[...], w_ref[O_U:O_Q, :]) + _mm(dq_ref[...], w_ref[O_Q:O_KV, :])
        dh = dh + _mm(dkv_ref[...], w_ref[O_KV:O_G, :]) + _mm(dg_ref[...], w_ref[O_G:IN_WIDTH, :])
        x = x_ref[...]
        _, r = _rmsnorm_fwd(x, gn_ref[...])
        dx, dgr = _rmsnorm_bwd(x, r, gn_ref[...], dh)
        dgn_ref[...] += jnp.sum(dgr, axis=0, keepdims=True)
        gx_ref[...] = dx1_ref[...] + dx

    return pl.pallas_call(
        body, name="inproj_bwd", grid=(s // tm,),
        out_shape=(jax.ShapeDtypeStruct((s, D_MODEL), F32), jax.ShapeDtypeStruct((1, D_MODEL), F32)),
        in_specs=[_rows(tm, 1024), _rows(tm, 1024), _rows(tm, 256), _rows(tm, 2048),
                  _resident((IN_WIDTH, D_MODEL)), _rows(tm, 1024), _resident((1, 1024)), _rows(tm, 1024)] + tie_spec,
        out_specs=(_rows(tm, 1024), pl.BlockSpec((1, D_MODEL), lambda i: (0, 0))),
        compiler_params=_params("arbitrary"),
    )(dzu, dzq, dzkv, dzg, win_t, x, attn_norm, dx1, *tie)


def _attention_constants(q_norm, k_norm, sinks):
    inv_freq = ROPE_THETA ** (-jnp.arange(0, ROPE_DIM, 2, dtype=F32) / ROPE_DIM)
    lane = jnp.arange(LANES) % HEAD_DIM
    invf = jnp.where(lane < ROPE_DIM, inv_freq[lane % (ROPE_DIM // 2)], 0.0).reshape(1, LANES).astype(F32)
    bd = (jnp.arange(LANES)[:, None] // HEAD_DIM == jnp.arange(LANES)[None, :] // HEAD_DIM).astype(MM)
    q_norm_t = jnp.tile(q_norm, (1, N_Q_HEADS))
    k_norm_t = jnp.tile(k_norm, (1, N_KV_HEADS))
    sink_rows = jnp.repeat(sinks.reshape(N_KV_HEADS, PAIRS, 2).transpose(0, 2, 1), BLOCK, axis=2)
    sink_rows = sink_rows.reshape(N_KV_HEADS, 2, 1, PAIR_COLS)
    return invf, bd, q_norm_t, k_norm_t, sink_rows


ANY = pl.BlockSpec(memory_space=pl.ANY)


def _position():
    return lax.axis_index("x"), lax.axis_index("y"), lax.axis_index("c")


def _all_gather(shards):
    k = len(shards)

    def body(*refs):
        ins, outs = refs[:k], refs[k:2 * k]
        send_sems, recv_sems, local_sems = refs[2 * k:]
        x, y, c = _position()
        me, sibling = (x, y, c), (x, y, 1 - c)
        chips = [(1 - x, y), (x, 1 - y), (1 - x, 1 - y)]

        def copy(a, kk, block, to, src=None):
            dst = outs[a].at[4 * block[0] + 2 * block[1] + block[2]]
            return pltpu.make_async_remote_copy(
                src_ref=dst if src is None else src, dst_ref=dst, send_sem=send_sems.at[a * 7 + kk],
                recv_sem=recv_sems.at[a * 7 + kk], device_id=to, device_id_type=MESH)

        mine = [pltpu.make_async_copy(ins[a], outs[a].at[4 * x + 2 * y + c], local_sems.at[a]) for a in range(k)]
        for cp in mine:
            cp.start()
        first = []
        for a in range(k):
            first.append(copy(a, 0, me, sibling, src=ins[a]))
            first += [copy(a, 1 + j, me, (*chip, c), src=ins[a]) for j, chip in enumerate(chips)]
        for cp in first:
            cp.start()
        passed = []
        for j, chip in enumerate(chips):
            for a in range(k):
                copy(a, 1 + j, (*chip, c), me).wait_recv()
                cp = copy(a, 4 + j, (*chip, c), sibling)
                cp.start()
                passed.append(cp)
        for a in range(k):
            copy(a, 0, sibling, me).wait_recv()
            for j, chip in enumerate(chips):
                copy(a, 4 + j, (*chip, 1 - c), me).wait_recv()
        for cp in first + passed:
            cp.wait_send()
        for cp in mine:
            cp.wait()

    return pl.pallas_call(
        body, name="all_gather_weights",
        out_shape=tuple(jax.ShapeDtypeStruct((N_DEV,) + s.shape, s.dtype) for s in shards),
        in_specs=[ANY] * k, out_specs=(ANY,) * k,
        scratch_shapes=[pltpu.SemaphoreType.DMA((7 * k,)), pltpu.SemaphoreType.DMA((7 * k,)),
                        pltpu.SemaphoreType.DMA((k,))],
    )(*shards)


HBM = pl.BlockSpec(memory_space=pltpu.HBM)
SEM = pl.BlockSpec(memory_space=pltpu.SEMAPHORE)
EFFECT = pltpu.SideEffectType.DATAFLOW_SIDE_EFFECTING


def _exchange_start(name, bufs, n_sems, copies, after=None):
    k = len(bufs)
    tie, tie_spec = _after(after)
    n_in = k + len(tie)

    def body(*refs):
        for cp in copies(refs[:k], refs[n_in], refs[n_in + 1]):
            cp.start()
        refs[-1][...] = jnp.zeros_like(refs[-1])

    dma = pltpu.SemaphoreType.DMA((n_sems,))
    out = pl.pallas_call(
        body, name=name,
        out_shape=(dma, dma, *[pltpu.HBM(b.shape, b.dtype) for b in bufs], jax.ShapeDtypeStruct((8, LANES), F32)),
        in_specs=[HBM] * k + tie_spec, out_specs=(SEM, SEM, *[HBM] * k, pl.BlockSpec(memory_space=pltpu.VMEM)),
        input_output_aliases={i: 2 + i for i in range(k)},
        compiler_params=pltpu.CompilerParams(has_side_effects=EFFECT),
    )(*[pltpu.with_memory_space_constraint(b, pltpu.HBM) for b in bufs], *tie)
    return out[0], out[1], list(out[2:2 + k]), out[-1]


def _exchange_mid(name, bufs, sems_in, n_sems, waits, copies, after):
    k, ns = len(bufs), len(sems_in)

    def body(*refs):
        ins = refs[:k]
        waits(ins, *refs[k:k + ns])
        for cp in copies(ins, refs[k + ns + 1], refs[k + ns + 2]):
            cp.start()

    dma = pltpu.SemaphoreType.DMA((n_sems,))
    out = pl.pallas_call(
        body, name=name, out_shape=(dma, dma, *[pltpu.HBM(b.shape, b.dtype) for b in bufs]),
        in_specs=[HBM] * k + [SEM] * ns + [ANY], out_specs=(SEM, SEM, *[HBM] * k),
        input_output_aliases={i: 2 + i for i in range(k)},
        compiler_params=pltpu.CompilerParams(has_side_effects=EFFECT),
    )(*bufs, *sems_in, after)
    return out[0], out[1], list(out[2:])


def _exchange_wait(name, bufs, sems, waits, after=None):
    k, ns = len(bufs), len(sems)
    tie, tie_spec = _after(after)

    def body(*refs):
        waits(refs[:k], *refs[k:k + ns])

    out = pl.pallas_call(
        body, name=name, out_shape=tuple(pltpu.HBM(b.shape, b.dtype) for b in bufs),
        in_specs=[HBM] * k + [SEM] * ns + tie_spec, out_specs=(HBM,) * k,
        input_output_aliases={i: i for i in range(k)},
        compiler_params=pltpu.CompilerParams(has_side_effects=EFFECT),
    )(*bufs, *sems, *tie)
    return list(out)


def _gather_copies(k, direct):
    def copies(refs, send_sems, recv_sems):
        x, y, c = _position()
        chips = [(1 - x, y), (x, 1 - y), (1 - x, 1 - y)]
        out = []
        for a in range(k):
            land = refs[k + a]
            if direct:
                mine = land.at[4 * x + 2 * y + c]
                for kk, to in enumerate([(x, y, 1 - c)] + [(*chip, c) for chip in chips]):
                    out.append(pltpu.make_async_remote_copy(
                        src_ref=refs[a], dst_ref=mine, send_sem=send_sems.at[4 * a + kk],
                        recv_sem=recv_sems.at[4 * a + kk], device_id=to, device_id_type=MESH))
            else:
                for j, (px, py) in enumerate(chips):
                    slot = land.at[4 * px + 2 * py + c]
                    out.append(pltpu.make_async_remote_copy(
                        src_ref=slot, dst_ref=slot, send_sem=send_sems.at[3 * a + j], recv_sem=recv_sems.at[3 * a + j],
                        device_id=(x, y, 1 - c), device_id_type=MESH))
        return out
    return copies


def _all_gather_behind(shards, start_after, mid_after):
    k = len(shards)
    me = 4 * lax.axis_index("x") + 2 * lax.axis_index("y") + lax.axis_index("c")
    lands = [lax.dynamic_update_slice(lax.empty((N_DEV,) + s.shape, s.dtype), s[None], (me, 0, 0)) for s in shards]
    direct, passed = _gather_copies(k, True), _gather_copies(k, False)

    send_a, recv_a, bufs, token = _exchange_start("gather_start", list(shards) + lands, 4 * k, direct, start_after)

    def finish():
        def wait_ici(refs, send_sems, recv_sems):
            for i, cp in enumerate(direct(refs, send_sems, recv_sems)):
                if i % 4:
                    cp.wait_recv()

        send_b, recv_b, bufs2 = _exchange_mid("gather_pass", bufs, [send_a, recv_a], 3 * k, wait_ici, passed,
                                              mid_after())

        def wait_all(refs, sa, ra, sb, rb):
            for i, cp in enumerate(direct(refs, sa, ra)):
                cp.wait_send()
                if i % 4 == 0:
                    cp.wait_recv()
            for cp in passed(refs, sb, rb):
                cp.wait()

        return _exchange_wait("gather_wait", bufs2, [send_a, recv_a, send_b, recv_b], wait_all)[k:]

    return token, finish


def _pair_copies(k):
    def copies(refs, send_sems, recv_sems):
        x, y, c = _position()
        return [pltpu.make_async_remote_copy(
            src_ref=refs[a].at[2 * ch + 1 - c], dst_ref=refs[k + a].at[ch], send_sem=send_sems.at[4 * a + ch],
            recv_sem=recv_sems.at[4 * a + ch], device_id=(x, y, 1 - c), device_id_type=MESH)
            for a in range(k) for ch in range(4)]
    return copies


def _chip_copies(k):
    def copies(refs, send_sems, recv_sems):
        x, y, c = _position()
        return [pltpu.make_async_remote_copy(
            src_ref=refs[a].at[2 * px + py], dst_ref=refs[k + a].at[rel], send_sem=send_sems.at[3 * a + rel],
            recv_sem=recv_sems.at[3 * a + rel], device_id=(px, py, c), device_id_type=MESH)
            for a in range(k) for rel, (px, py) in enumerate([(1 - x, y), (x, 1 - y), (1 - x, 1 - y)])]
    return copies


def _symmetric_exchange(name, srcs, n_land, copies_of):
    k = len(srcs)
    lands = [lax.empty((n_land,) + s.shape[1:], s.dtype) for s in srcs]
    copies = copies_of(k)
    send_sems, recv_sems, bufs, token = _exchange_start(name + "_start", list(srcs) + lands, n_land * k, copies)

    def finish(after):
        def wait_all(refs, ss, rs):
            for cp in copies(refs, ss, rs):
                cp.wait()

        done = _exchange_wait(name + "_wait", bufs, [send_sems, recv_sems], wait_all, after)
        return done[:k], done[k:]

    return token, finish


def _pair_add(full, recv, wire):
    _, r, c_ = full.shape

    def body(f_ref, r_ref, pw_ref, own_ref):
        ch = pl.program_id(0)
        x, y, c = _position()
        tot = f_ref[0, c] + r_ref[0]
        pw_ref[0] = tot.astype(pw_ref.dtype)

        @pl.when(ch == 2 * x + y)
        def _():
            own_ref[...] = tot

    return pl.pallas_call(
        body, name="grad_pair_add", grid=(4,),
        out_shape=(jax.ShapeDtypeStruct((4, r, c_), wire), jax.ShapeDtypeStruct((r, c_), F32)),
        in_specs=[pl.BlockSpec((1, 2, r, c_), lambda i: (i, 0, 0, 0)), pl.BlockSpec((1, r, c_), lambda i: (i, 0, 0))],
        out_specs=(pl.BlockSpec((1, r, c_), lambda i: (i, 0, 0)), pl.BlockSpec((r, c_), lambda i: (0, 0))),
        compiler_params=_params("arbitrary"),
    )(full.reshape(4, 2, r, c_), recv)


def _adamw_math(w, g, m, v):
    m = ADAM_B1 * m + (1.0 - ADAM_B1) * g
    v = ADAM_B2 * v + (1.0 - ADAM_B2) * (g * g)
    m_hat = m / (1.0 - ADAM_B1 ** ADAM_STEP)
    v_hat = v / (1.0 - ADAM_B2 ** ADAM_STEP)
    delta = -ADAM_LR * (m_hat / (jnp.sqrt(v_hat) + ADAM_EPS) + ADAM_WD * w)
    return delta, m, v


def _row_tile(r):
    for t in (256, 176, 128):
        if r % t == 0 and r > t:
            return t
    return r


def _adamw(g_own, recv, w, m, v):
    r, c_ = w.shape
    t = _row_tile(r)
    blk = pl.BlockSpec((t, c_), lambda i: (i, 0))

    def body(g_ref, r_ref, w_ref, m_ref, v_ref, go_ref, d_ref, mo_ref, vo_ref):
        g = g_ref[...]
        for i in range(3):
            g = g + r_ref[i].astype(F32)
        go_ref[...] = g
        d_ref[...], mo_ref[...], vo_ref[...] = _adamw_math(w_ref[...], g, m_ref[...], v_ref[...])

    return pl.pallas_call(
        body, name="adamw", grid=(r // t,), out_shape=(jax.ShapeDtypeStruct((r, c_), F32),) * 4,
        in_specs=[blk, pl.BlockSpec((3, t, c_), lambda i: (0, i, 0)), blk, blk, blk], out_specs=(blk,) * 4,
        compiler_params=_params("parallel"),
    )(g_own, recv, w, m, v)


def _adamw_transposed(g_own, recv, w, m, v):
    r = g_own.shape[0]
    full, rem = divmod(r, LANES)
    nat = pl.BlockSpec((LANES, r), lambda i: (i, 0))

    def body(g_ref, r_ref, w_ref, m_ref, v_ref, go_ref, d_ref, mo_ref, vo_ref, gt_ref):
        gsum = g_ref[...]
        for i in range(3):
            gsum = gsum + r_ref[i].astype(F32)
        for b in range(full):
            gt_ref[:, b * LANES:(b + 1) * LANES] = gsum[b * LANES:(b + 1) * LANES, :].T
        if rem:
            tail = jnp.concatenate([gsum[full * LANES:r, :], jnp.zeros((LANES - rem, LANES), F32)], axis=0).T
            gt_ref[:, full * LANES:r] = tail[:, 0:rem]
        g = gt_ref[...]
        go_ref[...] = g
        d_ref[...], mo_ref[...], vo_ref[...] = _adamw_math(w_ref[...], g, m_ref[...], v_ref[...])

    return pl.pallas_call(
        body, name="adamw_transposed", grid=(D_MODEL // LANES,),
        out_shape=(jax.ShapeDtypeStruct((D_MODEL, r), F32),) * 4,
        in_specs=[pl.BlockSpec((r, LANES), lambda i: (0, i)), pl.BlockSpec((3, r, LANES), lambda i: (0, 0, i)),
                  nat, nat, nat],
        out_specs=(nat,) * 4, scratch_shapes=[pltpu.VMEM((LANES, r), F32)], compiler_params=_params("parallel"),
    )(g_own, recv, w, m, v)


SMALL_ROWS = 88


def _small_allreduce_adamw(gp, wp, mp, vp):
    def body(g_ref, w_ref, m_ref, v_ref, go_ref, d_ref, mo_ref, vo_ref, slots_ref, send_sems, recv_sems):
        x, y, c = _position()
        me = 4 * x + 2 * y + c
        slots_ref[me] = g_ref[...]
        cps = []
        for rel in range(1, N_DEV):
            fx, fy, fc = (rel >> 2) & 1, (rel >> 1) & 1, rel & 1
            to = (1 - x if fx else x, 1 - y if fy else y, 1 - c if fc else c)
            cps.append(pltpu.make_async_remote_copy(
                src_ref=g_ref, dst_ref=slots_ref.at[me], send_sem=send_sems.at[rel - 1],
                recv_sem=recv_sems.at[rel - 1], device_id=to, device_id_type=MESH))
        for cp in cps:
            cp.start()
        for cp in cps:
            cp.wait()
        g = slots_ref[0]
        for i in range(1, N_DEV):
            g = g + slots_ref[i]
        go_ref[...] = g
        d_ref[...], mo_ref[...], vo_ref[...] = _adamw_math(w_ref[...], g, m_ref[...], v_ref[...])

    vm = pl.BlockSpec(memory_space=pltpu.VMEM)
    return pl.pallas_call(
        body, name="small_allreduce_adamw", out_shape=(jax.ShapeDtypeStruct((SMALL_ROWS, LANES), F32),) * 4,
        in_specs=[vm] * 4, out_specs=(vm,) * 4,
        scratch_shapes=[pltpu.VMEM((N_DEV, SMALL_ROWS, LANES), F32), pltpu.SemaphoreType.DMA((N_DEV - 1,)),
                        pltpu.SemaphoreType.DMA((N_DEV - 1,))],
    )(gp, wp, mp, vp)


SMALL = ("attn_norm", "b_gate", "pool_scale", "q_norm", "k_norm", "sinks", "ffn_norm", "conv_b")
SHARDED = ("w_in", "w_pool", "w_out", "w_up", "conv_w", "w_down")
WEIGHTS = ("attn_norm", "w_in", "b_gate", "w_pool", "pool_scale", "q_norm", "k_norm", "sinks", "w_out", "ffn_norm",
           "w_up", "conv_w", "conv_b", "w_down")


N_SMALL = 10896


def _pack_small(d, loss=None):
    parts = [d[n].reshape(-1) for n in SMALL] + ([] if loss is None else [loss.reshape(1)])
    flat = jnp.concatenate(parts)
    return jnp.pad(flat, (0, SMALL_ROWS * LANES - flat.shape[0])).reshape(SMALL_ROWS, LANES)


def _unpack_small(p, like):
    flat, out, o = p.reshape(-1), {}, 0
    for n in SMALL:
        out[n] = flat[o:o + like[n].size].reshape(like[n].shape)
        o += like[n].size
    return out


def kernel(x, positions, attn_norm, w_in, b_gate, w_pool, pool_scale, q_norm, k_norm, sinks, w_out, ffn_norm, w_up, conv_w, conv_b, w_down, loss_target, m_attn_norm, m_w_in, m_b_gate, m_w_pool, m_pool_scale, m_q_norm, m_k_norm, m_sinks, m_w_out, m_ffn_norm, m_w_up, m_conv_w, m_conv_b, m_w_down, v_attn_norm, v_w_in, v_b_gate, v_w_pool, v_pool_scale, v_q_norm, v_k_norm, v_sinks, v_w_out, v_ffn_norm, v_w_up, v_conv_w, v_conv_b, v_w_down):
    w = dict(attn_norm=attn_norm, w_in=w_in, b_gate=b_gate, w_pool=w_pool, pool_scale=pool_scale, q_norm=q_norm,
             k_norm=k_norm, sinks=sinks, w_out=w_out, ffn_norm=ffn_norm, w_up=w_up, conv_w=conv_w, conv_b=conv_b,
             w_down=w_down)
    m = dict(attn_norm=m_attn_norm, w_in=m_w_in, b_gate=m_b_gate, w_pool=m_w_pool, pool_scale=m_pool_scale,
             q_norm=m_q_norm, k_norm=m_k_norm, sinks=m_sinks, w_out=m_w_out, ffn_norm=m_ffn_norm, w_up=m_w_up,
             conv_w=m_conv_w, conv_b=m_conv_b, w_down=m_w_down)
    v = dict(attn_norm=v_attn_norm, w_in=v_w_in, b_gate=v_b_gate, w_pool=v_w_pool, pool_scale=v_pool_scale,
             q_norm=v_q_norm, k_norm=v_k_norm, sinks=v_sinks, w_out=v_w_out, ffn_norm=v_ffn_norm, w_up=v_w_up,
             conv_w=v_conv_w, conv_b=v_conv_b, w_down=v_w_down)
    seq = x.shape[1]
    tm = 256
    tk = min(seq, 1024)
    xs, target, pos_col = x[0], loss_target[0], positions.reshape(seq, 1)
    invf, bd, q_norm_t, k_norm_t, sink_rows = _attention_constants(q_norm, k_norm, sinks)
    out = {}
    nat = {"w_in": (D_MODEL, 544), "w_pool": (128, POOL_GROUP), "w_out": (128, D_MODEL), "w_up": (D_MODEL, 704),
           "conv_w": (3, 704), "w_down": (352, D_MODEL)}

    def update(names, owns, recvs):
        for name, own, recv in zip(names, owns, recvs):
            w2, m2, v2 = (t[name].reshape(nat[name]) for t in (w, m, v))
            if name in ("w_in", "w_up"):
                res = _adamw_transposed(own, recv, w2, m2, v2)
            else:
                res = _adamw(own, recv, w2, m2, v2)
            out[name] = [t.reshape(w[name].shape) for t in res]

    (g_win,) = _all_gather([w_in[0].T.astype(MM)])
    win_t = g_win.reshape(IN_WIDTH, D_MODEL)
    fwd = {}
    token, gather_rest = _all_gather_behind(
        [w_pool[0].astype(MM).reshape(128, POOL_GROUP), w_out[0].astype(MM), w_up[0].T.astype(MM), conv_w[0],
         w_down[0].astype(MM)], g_win, lambda: fwd["b"])

    tabs = _rope_tables(pos_col, invf)
    h, u, q, kv, g = _inproj_fwd(xs, attn_norm + token[0:1, 0:1], win_t, b_gate, tm)
    fwd["b"] = b = _attn_fwd(q, kv, tabs, q_norm_t, k_norm_t, sink_rows, bd)
    g_wpool, g_wout, g_wup, g_convw, g_wdown = gather_rest()
    wpool = g_wpool.reshape(N_DEV, 4, 32, POOL_GROUP).transpose(1, 0, 2, 3).reshape(4, POOL_GROUP, POOL_GROUP)
    wout = g_wout.reshape(D_MODEL, D_MODEL)
    wup_t = g_wup.reshape(2 * D_FF, D_MODEL)
    convw = g_convw.transpose(1, 0, 2).reshape(3, 2 * D_FF)
    wdown = g_wdown.reshape(D_FF, D_MODEL)
    a = _pool_fwd(u, wpool, pool_scale, tm)
    x1, mix = _mix_out_fwd(xs, g, a, b, wout, tm)
    h2, uff, dy, dyb, lossp = _ffn_fwd(x1, ffn_norm, wup_t, convw, conv_b, wdown, target, tm)

    du, act, d_conv_w, d_conv_b = _ffn_bwd_a(dyb, uff, convw, conv_b, wdown, tm)
    d_wdown = _matmul_tn(act, dyb, FF_CHUNK, tk, "dw_down")
    dx1, dx1b, d_ffn_norm = _ffn_bwd_b(du, wup_t, x1, ffn_norm, dy, tm)
    d_wup_t = _matmul_tn(du, h2, FF_CHUNK, tk, "dw_up")
    late = ("w_down", "w_up", "conv_w")
    late_wire = (WIRE, WIRE, F32)
    late_full = [d_wdown.reshape(N_DEV, 352, D_MODEL), d_wup_t.reshape(N_DEV, 704, D_MODEL),
                 d_conv_w.reshape(3, N_DEV, 704).transpose(1, 0, 2)]
    token, late_pair = _symmetric_exchange("late_pair", late_full, 4, _pair_copies)
    d_wout = _matmul_tn(mix, dx1b, 1024, tk, "dw_out", after=token)
    da, db, dzg, d_b_gate = _mix_bwd(dx1b, wout, g, a, b, tm)
    late_pw, late_own = zip(*[_pair_add(f, r, wd) for f, r, wd in zip(*late_pair(dzg), late_wire)])
    token, late_chip = _symmetric_exchange("late_chip", list(late_pw), 3, _chip_copies)
    d_win_t = _matmul_tn(dzg, h, 1024, tk, "dw_in_g", after=token, into=IN_WIDTH, row0=O_G)
    dzu, d_wpool, d_pool_scale = _pool_bwd(u, da, wpool, pool_scale, tm)
    d_win_t = _matmul_tn(dzu, h, 1024, tk, "dw_in_u", into=d_win_t, row0=O_U)
    dzq, dzkv, d_q_norm, d_k_norm, d_sinks = _attn_bwd(q, kv, db, tabs, q_norm_t, k_norm_t, sink_rows, bd)
    d_win_t = _matmul_tn(dzq, h, 1024, tk, "dw_in_q", into=d_win_t, row0=O_Q)
    d_win_t = _matmul_tn(dzkv, h, 256, tk, "dw_in_kv", into=d_win_t, row0=O_KV)
    early = ("w_in", "w_pool", "w_out")
    early_full = [d_win_t.reshape(N_DEV, 544, D_MODEL),
                  d_wpool.reshape(4, N_DEV, 32, POOL_GROUP).transpose(1, 0, 2, 3).reshape(N_DEV, 128, POOL_GROUP),
                  d_wout.reshape(N_DEV, 128, D_MODEL)]
    token, early_pair = _symmetric_exchange("early_pair", early_full, 4, _pair_copies)
    update(late, late_own, late_chip(token)[1])
    early_pw, early_own = zip(*[_pair_add(f, r, WIRE) for f, r in zip(*early_pair([out[n][1] for n in late]))])
    token, early_chip = _symmetric_exchange("early_chip", list(early_pw), 3, _chip_copies)
    grad_x, d_attn_norm = _inproj_bwd(dzu, dzq, dzkv, dzg, win_t, xs, attn_norm, dx1, tm, after=token)
    gr = dict(attn_norm=d_attn_norm, b_gate=d_b_gate, pool_scale=d_pool_scale, q_norm=d_q_norm, k_norm=d_k_norm,
              sinks=d_sinks[:, 0:N_Q_HEADS], ffn_norm=d_ffn_norm, conv_b=d_conv_b)
    small = _small_allreduce_adamw(_pack_small(gr, lossp[0, 0]), _pack_small(w), _pack_small(m), _pack_small(v))
    loss = small[0].reshape(-1)[N_SMALL]
    unpacked = [_unpack_small(p, w) for p in small]
    for name in SMALL:
        out[name] = [t[name] for t in unpacked]
    update(early, early_own, early_chip(small[1])[1])

    return (loss, grad_x[None], *[out[n][0] for n in WEIGHTS], *[out[n][1] for n in WEIGHTS],
            *[out[n][2] for n in WEIGHTS], *[out[n][3] for n in WEIGHTS])
```

```python
import functools

import jax
import jax.numpy as jnp
from jax import lax
from jax.experimental import pallas as pl
from jax.experimental.pallas import tpu as pltpu

F32 = jnp.float32
MM = jnp.bfloat16
WIRE = jnp.bfloat16

D_MODEL = 1024
D_FF = 2816
HEAD_DIM = 64
N_Q_HEADS = 16
N_KV_HEADS = 2
GQA_GROUP = 8
BLOCK = 128
ROPE_DIM = 16
ROPE_THETA = 500000.0
POOL_WINDOWS = (2, 4, 8, 16)
POOL_GROUP = 256
POOL_HALO = 16
CONV_HALO = 8
EPS = 1e-6
NEG = -1e30
O_U, O_Q, O_KV, O_G, IN_WIDTH = 0, 1024, 2048, 2304, 4352
FF_CHUNK = 1408

ADAM_LR, ADAM_B1, ADAM_B2, ADAM_EPS, ADAM_WD, ADAM_STEP = 0.001, 0.9, 0.999, 1e-08, 0.01, 10

N_DEV = 8
LANES = 128
VMEM_LIMIT_BYTES = 56 * 1024 * 1024
MESH = pl.DeviceIdType.MESH


def _params(*sem):
    return pltpu.CompilerParams(dimension_semantics=sem, vmem_limit_bytes=VMEM_LIMIT_BYTES)


def _resident(shape):
    nd = len(shape)
    return pl.BlockSpec(shape, lambda *_: (0,) * nd, pipeline_mode=pl.Buffered(1))


def _rows(tm, width):
    return pl.BlockSpec((tm, width), lambda i: (i, 0))


def _mm(a, b):
    return jnp.dot(a.astype(MM), b.astype(MM), preferred_element_type=F32)


def _mm_nt(a, b):
    return lax.dot_general(a.astype(MM), b.astype(MM), (((1,), (1,)), ((), ())), preferred_element_type=F32)


def _mm_tn(a, b):
    return lax.dot_general(a.astype(MM), b.astype(MM), (((0,), (0,)), ((), ())), preferred_element_type=F32)


def _rmsnorm_fwd(x, g):
    r = lax.rsqrt(jnp.mean(x * x, axis=-1, keepdims=True) + EPS)
    return x * r * g, r


def _rmsnorm_bwd(x, r, g, dy):
    xn = x * r
    dxn = dy * g
    dx = r * (dxn - xn * jnp.mean(dxn * xn, axis=-1, keepdims=True))
    return dx, dy * xn


def _group_sum64(v, bd):
    hi = v.astype(MM)
    lo = (v - hi.astype(F32)).astype(MM)
    outs = []
    for t in range(v.shape[1] // LANES):
        sl = slice(LANES * t, LANES * (t + 1))
        outs.append(jnp.dot(hi[:, sl], bd, preferred_element_type=F32)
                    + jnp.dot(lo[:, sl], bd, preferred_element_type=F32))
    return outs[0] if len(outs) == 1 else jnp.concatenate(outs, axis=1)


def _head_norm_fwd(x, g, bd):
    r = lax.rsqrt(_group_sum64(x * x, bd) * (1.0 / HEAD_DIM) + EPS)
    return x * r * g, r


def _head_norm_bwd(x, r, g, dy, bd):
    xn = x * r
    dxn = dy * g
    dx = r * (dxn - xn * (_group_sum64(dxn * xn, bd) * (1.0 / HEAD_DIM)))
    return dx, dy * xn


def _rope(x, c, s1, s2):
    w = x.shape[1]
    return x * c + pltpu.roll(x, w - ROPE_DIM // 2, 1) * s1 + pltpu.roll(x, ROPE_DIM // 2, 1) * s2


def _rope_bwd(dy, c, s1, s2):
    w = dy.shape[1]
    return dy * c + pltpu.roll(dy * s1, ROPE_DIM // 2, 1) + pltpu.roll(dy * s2, w - ROPE_DIM // 2, 1)


def _tile_lanes(t, reps):
    return t if reps == 1 else jnp.concatenate([t] * reps, axis=1)


def _rope_tables(pos_col, invf):
    s = pos_col.shape[0]
    tm = min(s, 1024)

    def body(pos_ref, invf_ref, c_ref, s1_ref, s2_ref):
        ang = pos_ref[...].astype(F32) * invf_ref[...]
        lane = lax.broadcasted_iota(jnp.int32, ang.shape, 1) % HEAD_DIM
        sn = jnp.sin(ang)
        c_ref[...] = jnp.cos(ang)
        s1_ref[...] = jnp.where(lane < ROPE_DIM // 2, -sn, 0.0)
        s2_ref[...] = jnp.where((lane >= ROPE_DIM // 2) & (lane < ROPE_DIM), sn, 0.0)

    out = jax.ShapeDtypeStruct((s, LANES), F32)
    return pl.pallas_call(
        body, name="rope_tables", grid=(s // tm,), out_shape=(out, out, out),
        in_specs=[_rows(tm, 1), _resident((1, LANES))],
        out_specs=(_rows(tm, LANES),) * 3, compiler_params=_params("parallel"),
    )(pos_col, invf)


def _inproj_fwd(x, attn_norm, win_t, b_gate, tm):
    s = x.shape[0]

    def body(x_ref, gn_ref, w_ref, bg_ref, h_ref, u_ref, q_ref, kv_ref, g_ref):
        h, _ = _rmsnorm_fwd(x_ref[...], gn_ref[...])
        h = h.astype(MM)
        h_ref[...] = h
        u_ref[...] = _mm_nt(h, w_ref[O_U:O_Q, :])
        q_ref[...] = _mm_nt(h, w_ref[O_Q:O_KV, :])
        kv_ref[...] = _mm_nt(h, w_ref[O_KV:O_G, :])
        g_ref[...] = jax.nn.sigmoid(_mm_nt(h, w_ref[O_G:IN_WIDTH, :]) + bg_ref[...])

    sd = jax.ShapeDtypeStruct
    return pl.pallas_call(
        body, name="inproj_fwd", grid=(s // tm,),
        out_shape=(sd((s, D_MODEL), MM), sd((s, 1024), F32), sd((s, 1024), F32), sd((s, 256), F32),
                   sd((s, 2048), F32)),
        in_specs=[_rows(tm, D_MODEL), _resident((1, D_MODEL)), _resident((IN_WIDTH, D_MODEL)), _resident((1, 2048))],
        out_specs=(_rows(tm, D_MODEL), _rows(tm, 1024), _rows(tm, 1024), _rows(tm, 256), _rows(tm, 2048)),
        compiler_params=_params("parallel"),
    )(x, attn_norm, win_t, b_gate)


def _pooled(ext_ref, tm, row0):
    t = (row0 + lax.broadcasted_iota(jnp.int32, (tm, 1), 0)).astype(F32)
    out = []
    for gi, w in enumerate(POOL_WINDOWS):
        cols = slice(gi * POOL_GROUP, (gi + 1) * POOL_GROUP)
        acc = ext_ref[pl.ds(POOL_HALO, tm), cols]
        for k in range(1, w):
            acc = acc + ext_ref[pl.ds(POOL_HALO - k, tm), cols]
        cnt = jnp.minimum(t + 1.0, float(w))
        out.append(acc / cnt - ext_ref[pl.ds(POOL_HALO, tm), cols])
    return out


def _pool_fwd(u, wpool, pool_scale, tm):
    s = u.shape[0]
    hb = tm // POOL_HALO

    def body(u_ref, halo_ref, wp_ref, ps_ref, a_ref, ext_ref):
        i = pl.program_id(0)
        ext_ref[pl.ds(0, POOL_HALO), :] = jnp.where(i > 0, halo_ref[...], 0.0)
        ext_ref[pl.ds(POOL_HALO, tm), :] = u_ref[...]
        pooled = _pooled(ext_ref, tm, i * tm)
        for gi in range(4):
            cols = slice(gi * POOL_GROUP, (gi + 1) * POOL_GROUP)
            a_ref[:, cols] = _mm(pooled[gi], wp_ref[gi]) * ps_ref[:, cols]

    return pl.pallas_call(
        body, name="pool_fwd", grid=(s // tm,), out_shape=jax.ShapeDtypeStruct((s, 1024), F32),
        in_specs=[_rows(tm, 1024), pl.BlockSpec((POOL_HALO, 1024), lambda i: (jnp.maximum(i * hb - 1, 0), 0)),
                  _resident((4, POOL_GROUP, POOL_GROUP)), _resident((1, 1024))],
        out_specs=_rows(tm, 1024), scratch_shapes=[pltpu.VMEM((POOL_HALO + tm, 1024), F32)],
        compiler_params=_params("parallel"),
    )(u, u, wpool, pool_scale)


PAIRS = GQA_GROUP // 2
PAIR_COLS = PAIRS * BLOCK


def _attn_mask_t(n):
    shape = (2 * BLOCK, PAIR_COLS)
    kj = lax.broadcasted_iota(jnp.int32, shape, 0)
    qi = lax.broadcasted_iota(jnp.int32, shape, 1) % BLOCK
    return (kj > qi) & (kj <= qi + BLOCK) & ((n > 0) | (kj >= BLOCK))


def _stack_pairs(x, hk):
    return jnp.concatenate([x[:, (PAIRS * hk + j) * LANES:(PAIRS * hk + j + 1) * LANES] for j in range(PAIRS)], axis=0)


def _parity_bands(t, hk):
    low = lax.broadcasted_iota(jnp.int32, t.shape, 1) < HEAD_DIM
    own = jnp.where(low if hk == 0 else ~low, t, 0.0)
    other = pltpu.roll(own, HEAD_DIM, 1)
    return (own, other) if hk == 0 else (other, own)


def _fold_parity(even, odd, hk):
    low = lax.broadcasted_iota(jnp.int32, even.shape, 1) < HEAD_DIM
    comb = jnp.where(low, even, odd)
    comb = comb + pltpu.roll(comb, HEAD_DIM, 1)
    return jnp.where(low if hk == 0 else ~low, comb, 0.0)


def _softmax_sink_t(s, sink):
    m = jnp.maximum(jnp.max(s, axis=0, keepdims=True), sink)
    p = jnp.exp(s - m)
    es = jnp.exp(sink - m)
    inv = 1.0 / (jnp.sum(p, axis=0, keepdims=True) + es)
    return p * inv, es * inv


def _attn_fwd(q, kv, tabs, q_norm_t, k_norm_t, sink_rows, bd):
    s = q.shape[0]
    nb = s // BLOCK
    scale = HEAD_DIM ** -0.5
    cur = lambda n: (n, 0)
    prv = lambda n: (jnp.maximum(n - 1, 0), 0)

    def body(q_ref, kvc_ref, kvp_ref, c_ref, s1_ref, s2_ref, cp_ref, s1p_ref, s2p_ref, qn_ref, kn_ref, sink_ref,
             bd_ref, o_ref):
        n = pl.program_id(0)
        bdm = bd_ref[...]
        c, s1, s2 = c_ref[...], s1_ref[...], s2_ref[...]
        qh, _ = _head_norm_fwd(q_ref[...], qn_ref[...], bdm)
        qr = (_rope(qh, _tile_lanes(c, 8), _tile_lanes(s1, 8), _tile_lanes(s2, 8)) * scale).astype(MM)
        kc, _ = _head_norm_fwd(kvc_ref[:, 0:128], kn_ref[...], bdm)
        kp, _ = _head_norm_fwd(kvp_ref[:, 0:128], kn_ref[...], bdm)
        k2 = jnp.concatenate([_rope(kp, cp_ref[...], s1p_ref[...], s2p_ref[...]), _rope(kc, c, s1, s2)], axis=0)
        v2 = jnp.concatenate([kvp_ref[:, 128:256], kvc_ref[:, 128:256]], axis=0)
        mask = _attn_mask_t(n)
        for hk in range(N_KV_HEADS):
            qs = _stack_pairs(qr, hk)
            ot = jnp.zeros((LANES, PAIR_COLS), F32)
            for par, (kb, vb) in enumerate(zip(_parity_bands(k2, hk), _parity_bands(v2, hk))):
                sc = jnp.where(mask, _mm_nt(kb, qs), NEG)
                pr, _ = _softmax_sink_t(sc, sink_ref[hk, par])
                ot = ot + _mm(vb.T, pr)
            for j in range(PAIRS):
                col = (PAIRS * hk + j) * LANES
                o_ref[:, col:col + LANES] = ot[:, j * BLOCK:(j + 1) * BLOCK].T

    tab = lambda im: pl.BlockSpec((BLOCK, LANES), im)
    return pl.pallas_call(
        body, name="attn_fwd", grid=(nb,), out_shape=jax.ShapeDtypeStruct((s, 1024), F32),
        in_specs=[pl.BlockSpec((BLOCK, 1024), cur), pl.BlockSpec((BLOCK, 256), cur), pl.BlockSpec((BLOCK, 256), prv),
                  tab(cur), tab(cur), tab(cur), tab(prv), tab(prv), tab(prv),
                  _resident((1, 1024)), _resident((1, 128)), _resident((N_KV_HEADS, 2, 1, PAIR_COLS)),
                  _resident((LANES, LANES))],
        out_specs=pl.BlockSpec((BLOCK, 1024), cur), compiler_params=_params("parallel"),
    )(q, kv, kv, *tabs, *tabs, q_norm_t, k_norm_t, sink_rows, bd)


def _mix_out_fwd(x, g, a, b, wout, tm):
    s = x.shape[0]

    def body(x_ref, g_ref, a_ref, b_ref, w_ref, x1_ref, mix_ref):
        mix = (g_ref[:, 0:1024] * a_ref[...] + g_ref[:, 1024:2048] * b_ref[...]).astype(MM)
        mix_ref[...] = mix
        x1_ref[...] = x_ref[...] + _mm(mix, w_ref[...])

    return pl.pallas_call(
        body, name="mix_out_fwd", grid=(s // tm,),
        out_shape=(jax.ShapeDtypeStruct((s, D_MODEL), F32), jax.ShapeDtypeStruct((s, D_MODEL), MM)),
        in_specs=[_rows(tm, 1024), _rows(tm, 2048), _rows(tm, 1024), _rows(tm, 1024), _resident((1024, 1024))],
        out_specs=(_rows(tm, 1024), _rows(tm, 1024)), compiler_params=_params("parallel"),
    )(x, g, a, b, wout)


SHIFT_ROWS = 16


def _sublane_major_matrices(tm):
    r = jnp.arange(tm)
    pm = (r[None, :] == ((tm // 8) * (r % 8) + r // 8)[:, None]).astype(MM)
    return pm, pm.T


def _to_sublane_major(pm, v):
    return jnp.dot(pm, v, preferred_element_type=F32).astype(MM)


def _to_time_order(pmt, v):
    hi = v.astype(MM)
    r1 = v - hi.astype(F32)
    mid = r1.astype(MM)
    lo = (r1 - mid.astype(F32)).astype(MM)
    dot = functools.partial(jnp.dot, preferred_element_type=F32)
    return dot(pmt, hi) + dot(pmt, mid) + dot(pmt, lo)


def _step_back(vreg_rows, before):
    sub = lax.broadcasted_iota(jnp.int32, vreg_rows.shape, 0)
    return jnp.where(sub == 0, before[7:8, :], pltpu.roll(vreg_rows, 1, 0))


def _step_ahead(vreg_rows, after):
    sub = lax.broadcasted_iota(jnp.int32, vreg_rows.shape, 0)
    return jnp.where(sub == 7, after[0:1, :], pltpu.roll(vreg_rows, 7, 0))


def _fill_back_rows(ext_ref, before, tm, cols):
    last = ext_ref[pl.ds(SHIFT_ROWS + tm - 8, 8), cols]
    pen = ext_ref[pl.ds(SHIFT_ROWS + tm - 16, 8), cols]
    ext_ref[pl.ds(8, 8), cols] = _step_back(last, before[8:16, :])
    ext_ref[pl.ds(0, 8), cols] = _step_back(pen, before[0:8, :])


def _conv_glu(ext_ref, cw_ref, cb_ref, tm, c):
    out = []
    for base in (c * FF_CHUNK, D_FF + c * FF_CHUNK):
        cols = slice(base, base + FF_CHUNK)
        y = cb_ref[:, cols] + cw_ref[0:1, cols] * ext_ref[pl.ds(0, tm), cols]
        y = y + cw_ref[1:2, cols] * ext_ref[pl.ds(8, tm), cols]
        y = y + cw_ref[2:3, cols] * ext_ref[pl.ds(SHIFT_ROWS, tm), cols]
        out.append(y)
    return out


def _ffn_fwd(x1, ffn_norm, wup_t, conv_w, conv_b, wdown, target, tm):
    s = x1.shape[0]
    inv_d = 1.0 / D_MODEL
    pm, pmt = _sublane_major_matrices(tm)

    def body(x1_ref, gn_ref, wu_ref, cw_ref, cb_ref, wd_ref, tgt_ref, pm_ref, pmt_ref, h2_ref, u_ref, dy_ref, dyb_ref,
             loss_ref, ext_ref, carry_ref):
        i = pl.program_id(0)

        @pl.when(i == 0)
        def _():
            carry_ref[...] = jnp.zeros_like(carry_ref)
            loss_ref[...] = jnp.zeros_like(loss_ref)

        x1 = x1_ref[...]
        h2, _ = _rmsnorm_fwd(x1, gn_ref[...])
        h2 = _to_sublane_major(pm_ref[...], h2.astype(MM))
        h2_ref[...] = h2
        for c in range(4):
            cols = slice(c * FF_CHUNK, (c + 1) * FF_CHUNK)
            uc = _mm_nt(h2, wu_ref[cols, :])
            u_ref[:, cols] = uc
            ext_ref[pl.ds(SHIFT_ROWS, tm), cols] = uc
            _fill_back_rows(ext_ref, carry_ref[:, cols], tm, cols)
            carry_ref[:, cols] = uc[tm - SHIFT_ROWS:tm, :]
        down = jnp.zeros((tm, D_MODEL), F32)
        for c in range(2):
            gate, val = _conv_glu(ext_ref, cw_ref, cb_ref, tm, c)
            act = gate * jax.nn.sigmoid(gate) * val
            down = down + _mm(act, wd_ref[c * FF_CHUNK:(c + 1) * FF_CHUNK, :])
        err = x1 + _to_time_order(pmt_ref[...], down) - tgt_ref[...]
        loss_ref[...] += jnp.full(loss_ref.shape, 0.5 * inv_d * jnp.sum(err * err), F32)
        dy = err * inv_d
        dy_ref[...] = dy
        dyb_ref[...] = _to_sublane_major(pm_ref[...], dy.astype(MM))

    sd = jax.ShapeDtypeStruct
    return pl.pallas_call(
        body, name="ffn_fwd", grid=(s // tm,),
        out_shape=(sd((s, D_MODEL), MM), sd((s, 2 * D_FF), F32), sd((s, D_MODEL), F32), sd((s, D_MODEL), MM),
                   sd((8, LANES), F32)),
        in_specs=[_rows(tm, 1024), _resident((1, 1024)), _resident((2 * D_FF, D_MODEL)), _resident((3, 2 * D_FF)),
                  _resident((1, 2 * D_FF)), _resident((D_FF, D_MODEL)), _rows(tm, 1024), _resident((tm, tm)),
                  _resident((tm, tm))],
        out_specs=(_rows(tm, 1024), _rows(tm, 2 * D_FF), _rows(tm, 1024), _rows(tm, 1024),
                   pl.BlockSpec((8, LANES), lambda i: (0, 0))),
        scratch_shapes=[pltpu.VMEM((SHIFT_ROWS + tm, 2 * D_FF), F32), pltpu.VMEM((SHIFT_ROWS, 2 * D_FF), F32)],
        compiler_params=_params("arbitrary"),
    )(x1, ffn_norm, wup_t, conv_w, conv_b, wdown, target, pm, pmt)


def _ffn_bwd_a(dyb, u, conv_w, conv_b, wdown, tm):
    s = dyb.shape[0]
    nt = s // tm
    hb = tm // SHIFT_ROWS
    rev = lambda i: (nt - 1 - i, 0)

    def body(dy_ref, u_ref, before_ref, cw_ref, cb_ref, wd_ref, du_ref, act_ref, dcw_ref, dcb_ref, ext_ref, extd_ref,
             ahead_ref):
        i = pl.program_id(0)
        first_tile = i == nt - 1

        @pl.when(i == 0)
        def _():
            ahead_ref[...] = jnp.zeros_like(ahead_ref)
            dcw_ref[...] = jnp.zeros_like(dcw_ref)
            dcb_ref[...] = jnp.zeros_like(dcb_ref)

        ext_ref[pl.ds(SHIFT_ROWS, tm), :] = u_ref[...]
        for c in range(4):
            cols = slice(c * FF_CHUNK, (c + 1) * FF_CHUNK)
            _fill_back_rows(ext_ref, jnp.where(first_tile, 0.0, before_ref[:, cols]), tm, cols)
        dy = dy_ref[...]
        for c in range(2):
            gate, val = _conv_glu(ext_ref, cw_ref, cb_ref, tm, c)
            sg = jax.nn.sigmoid(gate)
            sl = gate * sg
            act_ref[:, c * FF_CHUNK:(c + 1) * FF_CHUNK] = (sl * val).astype(MM)
            d_act = _mm_nt(dy, wd_ref[c * FF_CHUNK:(c + 1) * FF_CHUNK, :])
            extd_ref[pl.ds(0, tm), c * FF_CHUNK:(c + 1) * FF_CHUNK] = d_act * val * (sg * (1.0 + gate * (1.0 - sg)))
            extd_ref[pl.ds(0, tm), D_FF + c * FF_CHUNK:D_FF + (c + 1) * FF_CHUNK] = d_act * sl
        for c in range(4):
            cols = slice(c * FF_CHUNK, (c + 1) * FF_CHUNK)
            ahead = ahead_ref[:, cols]
            first2 = extd_ref[pl.ds(0, SHIFT_ROWS), cols]
            extd_ref[pl.ds(tm, 8), cols] = _step_ahead(first2[0:8, :], ahead[0:8, :])
            extd_ref[pl.ds(tm + 8, 8), cols] = _step_ahead(first2[8:16, :], ahead[8:16, :])
            ahead_ref[:, cols] = first2
            d0 = extd_ref[pl.ds(0, tm), cols]
            dcb_ref[:, cols] += jnp.sum(d0, axis=0, keepdims=True)
            for j in range(3):
                dcw_ref[j:j + 1, cols] += jnp.sum(d0 * ext_ref[pl.ds(8 * j, tm), cols], axis=0, keepdims=True)
            du = cw_ref[2:3, cols] * d0 + cw_ref[1:2, cols] * extd_ref[pl.ds(8, tm), cols]
            du = du + cw_ref[0:1, cols] * extd_ref[pl.ds(SHIFT_ROWS, tm), cols]
            du_ref[:, cols] = du.astype(MM)

    sd = jax.ShapeDtypeStruct
    return pl.pallas_call(
        body, name="ffn_bwd_a", grid=(nt,),
        out_shape=(sd((s, 2 * D_FF), MM), sd((s, D_FF), MM), sd((3, 2 * D_FF), F32), sd((1, 2 * D_FF), F32)),
        in_specs=[pl.BlockSpec((tm, D_MODEL), rev), pl.BlockSpec((tm, 2 * D_FF), rev),
                  pl.BlockSpec((SHIFT_ROWS, 2 * D_FF), lambda i: (jnp.maximum((nt - 1 - i) * hb - 1, 0), 0)),
                  _resident((3, 2 * D_FF)), _resident((1, 2 * D_FF)), _resident((D_FF, D_MODEL))],
        out_specs=(pl.BlockSpec((tm, 2 * D_FF), rev), pl.BlockSpec((tm, D_FF), rev),
                   pl.BlockSpec((3, 2 * D_FF), lambda i: (0, 0)), pl.BlockSpec((1, 2 * D_FF), lambda i: (0, 0))),
        scratch_shapes=[pltpu.VMEM((SHIFT_ROWS + tm, 2 * D_FF), F32), pltpu.VMEM((tm + SHIFT_ROWS, 2 * D_FF), F32),
                        pltpu.VMEM((SHIFT_ROWS, 2 * D_FF), F32)],
        compiler_params=_params("arbitrary"),
    )(dyb, u, u, conv_w, conv_b, wdown)


def _after(after):
    tie = [] if after is None else list(after) if isinstance(after, (list, tuple)) else [after]
    return tie, [pl.BlockSpec(memory_space=pl.ANY)] * len(tie)


def _matmul_tn(a, b, tmo, tk, name, after=None, into=None, row0=0):
    s, m = a.shape
    n = b.shape[1]
    nk = s // tk
    tie, tie_spec = _after(after)
    rows = m if into is None else into if isinstance(into, int) else into.shape[0]
    assert row0 % LANES == 0 and tmo % LANES == 0
    grown, grown_spec = ([], []) if into is None or isinstance(into, int) else ([into], [ANY])

    def body(a_ref, b_ref, *rest):
        o_ref = rest[-1]
        k = pl.program_id(1)

        @pl.when(k == 0)
        def _():
            o_ref[...] = jnp.zeros_like(o_ref)

        o_ref[...] += _mm_tn(a_ref[...], b_ref[pl.ds(pl.multiple_of(k * tk, tk), tk), :])

    return pl.pallas_call(
        body, name=name, grid=(m // tmo, nk), out_shape=jax.ShapeDtypeStruct((rows, n), F32),
        in_specs=[pl.BlockSpec((tk, tmo), lambda i, k: (k, i)), _resident((s, n))] + tie_spec + grown_spec,
        out_specs=pl.BlockSpec((pl.Element(tmo), pl.Element(n)), lambda i, k: (pl.multiple_of(row0 + i * tmo, LANES), 0)),
        input_output_aliases={2 + len(tie): 0} if grown else {},
        compiler_params=_params("parallel", "arbitrary"),
    )(a, b, *tie, *grown)


def _ffn_bwd_b(du, wup_t, x1, ffn_norm, dy, tm):
    s = du.shape[0]

    def body(du_ref, wu_ref, x1_ref, gn_ref, dy_ref, pmt_ref, dx1_ref, dx1b_ref, dg_ref):
        @pl.when(pl.program_id(0) == 0)
        def _():
            dg_ref[...] = jnp.zeros_like(dg_ref)

        dh2 = _to_time_order(pmt_ref[...], _mm(du_ref[...], wu_ref[...]))
        x1 = x1_ref[...]
        _, r = _rmsnorm_fwd(x1, gn_ref[...])
        dx, dgr = _rmsnorm_bwd(x1, r, gn_ref[...], dh2)
        dg_ref[...] += jnp.sum(dgr, axis=0, keepdims=True)
        dx1 = dy_ref[...] + dx
        dx1_ref[...] = dx1
        dx1b_ref[...] = dx1.astype(MM)

    return pl.pallas_call(
        body, name="ffn_bwd_b", grid=(s // tm,),
        out_shape=(jax.ShapeDtypeStruct((s, D_MODEL), F32), jax.ShapeDtypeStruct((s, D_MODEL), MM),
                   jax.ShapeDtypeStruct((1, D_MODEL), F32)),
        in_specs=[_rows(tm, 2 * D_FF), _resident((2 * D_FF, D_MODEL)), _rows(tm, 1024), _resident((1, 1024)),
                  _rows(tm, 1024), _resident((tm, tm))],
        out_specs=(_rows(tm, 1024), _rows(tm, 1024), pl.BlockSpec((1, D_MODEL), lambda i: (0, 0))),
        compiler_params=_params("arbitrary"),
    )(du, wup_t, x1, ffn_norm, dy, _sublane_major_matrices(tm)[1])


def _mix_bwd(dx1b, wout, g, a, b, tm):
    s = dx1b.shape[0]

    def body(dx_ref, w_ref, g_ref, a_ref, b_ref, da_ref, db_ref, dzg_ref, dbg_ref):
        @pl.when(pl.program_id(0) == 0)
        def _():
            dbg_ref[...] = jnp.zeros_like(dbg_ref)

        dmix = _mm_nt(dx_ref[...], w_ref[...])
        for half, src, dst in ((0, a_ref, da_ref), (1, b_ref, db_ref)):
            cols = slice(half * 1024, (half + 1) * 1024)
            gt = g_ref[:, cols]
            dst[...] = (dmix * gt).astype(dst.dtype)
            dz = dmix * src[...] * gt * (1.0 - gt)
            dzg_ref[:, cols] = dz.astype(MM)
            dbg_ref[:, cols] += jnp.sum(dz, axis=0, keepdims=True)

    sd = jax.ShapeDtypeStruct
    return pl.pallas_call(
        body, name="mix_bwd", grid=(s // tm,),
        out_shape=(sd((s, 1024), F32), sd((s, 1024), MM), sd((s, 2048), MM), sd((1, 2048), F32)),
        in_specs=[_rows(tm, 1024), _resident((1024, 1024)), _rows(tm, 2048), _rows(tm, 1024), _rows(tm, 1024)],
        out_specs=(_rows(tm, 1024), _rows(tm, 1024), _rows(tm, 2048), pl.BlockSpec((1, 2048), lambda i: (0, 0))),
        compiler_params=_params("arbitrary"),
    )(dx1b, wout, g, a, b)


def _pool_bwd(u, da, wpool, pool_scale, tm, after=None):
    s = u.shape[0]
    nt = s // tm
    hb = tm // POOL_HALO

    tie, tie_spec = _after(after)

    def body(u_ref, uh_ref, da_ref, dah_ref, wp_ref, ps_ref, *rest):
        dzu_ref, dwp_ref, dps_ref, ext_ref, exte_ref = rest[-5:]
        i = pl.program_id(0)

        @pl.when(i == 0)
        def _():
            dwp_ref[...] = jnp.zeros_like(dwp_ref)
            dps_ref[...] = jnp.zeros_like(dps_ref)

        ext_ref[pl.ds(0, POOL_HALO), :] = jnp.where(i > 0, uh_ref[...], 0.0)
        ext_ref[pl.ds(POOL_HALO, tm), :] = u_ref[...]
        pooled = _pooled(ext_ref, tm, i * tm)
        da = da_ref[...]
        dah = jnp.where(i < nt - 1, dah_ref[...], 0.0)
        t = (i * tm + lax.broadcasted_iota(jnp.int32, (tm + POOL_HALO, 1), 0)).astype(F32)
        for gi, w in enumerate(POOL_WINDOWS):
            cols = slice(gi * POOL_GROUP, (gi + 1) * POOL_GROUP)
            pg = pooled[gi].astype(MM)
            wg = wp_ref[gi]
            mixed = _mm(pg, wg)
            dps_ref[:, cols] += jnp.sum(da[:, cols] * mixed, axis=0, keepdims=True)
            dmx = (da[:, cols] * ps_ref[:, cols]).astype(MM)
            dwp_ref[gi] += _mm_tn(pg, dmx)
            dpl = _mm_nt(dmx, wg)
            dplh = _mm_nt(dah[:, cols] * ps_ref[:, cols], wg)
            cnt = jnp.minimum(t + 1.0, float(w))
            exte_ref[pl.ds(0, tm), cols] = dpl / cnt[0:tm]
            exte_ref[pl.ds(tm, POOL_HALO), cols] = dplh / cnt[tm:tm + POOL_HALO]
            acc = exte_ref[pl.ds(0, tm), cols]
            for k in range(1, w):
                acc = acc + exte_ref[pl.ds(k, tm), cols]
            dzu_ref[:, cols] = (acc - dpl).astype(MM)

    sd = jax.ShapeDtypeStruct
    last_halo = s // POOL_HALO - 1
    return pl.pallas_call(
        body, name="pool_bwd", grid=(nt,),
        out_shape=(sd((s, 1024), MM), sd((4, POOL_GROUP, POOL_GROUP), F32), sd((1, 1024), F32)),
        in_specs=[_rows(tm, 1024), pl.BlockSpec((POOL_HALO, 1024), lambda i: (jnp.maximum(i * hb - 1, 0), 0)),
                  _rows(tm, 1024),
                  pl.BlockSpec((POOL_HALO, 1024), lambda i: (jnp.minimum((i + 1) * hb, last_halo), 0)),
                  _resident((4, POOL_GROUP, POOL_GROUP)), _resident((1, 1024))] + tie_spec,
        out_specs=(_rows(tm, 1024), pl.BlockSpec((4, POOL_GROUP, POOL_GROUP), lambda i: (0, 0, 0)),
                   pl.BlockSpec((1, 1024), lambda i: (0, 0))),
        scratch_shapes=[pltpu.VMEM((POOL_HALO + tm, 1024), F32), pltpu.VMEM((tm + POOL_HALO, 1024), F32)],
        compiler_params=_params("arbitrary"),
    )(u, u, da, da, wpool, pool_scale, *tie)


def _attn_bwd(q, kv, db, tabs, q_norm_t, k_norm_t, sink_rows, bd):
    s = q.shape[0]
    nb = s // BLOCK
    scale = HEAD_DIM ** -0.5
    cur = lambda n: (jnp.minimum(n, nb - 1), 0)
    prv = lambda n: (jnp.maximum(n - 1, 0), 0)

    def body(q_ref, kvc_ref, kvp_ref, db_ref, c_ref, s1_ref, s2_ref, cp_ref, s1p_ref, s2p_ref, qn_ref, kn_ref,
             sink_ref, bd_ref, dzq_ref, dzkv_ref, dqn_ref, dkn_ref, dsk_ref,
             carry_ref, tot_ref, dqr_ref, qacc_ref, kacc_ref, sacc_ref):
        n = pl.program_id(0)
        bdm = bd_ref[...]
        kn = kn_ref[...]

        @pl.when(n == 0)
        def _():
            carry_ref[...] = jnp.zeros_like(carry_ref)
            qacc_ref[...] = jnp.zeros_like(qacc_ref)
            kacc_ref[...] = jnp.zeros_like(kacc_ref)
            sacc_ref[...] = jnp.zeros_like(sacc_ref)

        kp_raw = kvp_ref[:, 0:128]
        kph, rp = _head_norm_fwd(kp_raw, kn, bdm)
        cp, s1p, s2p = cp_ref[...], s1p_ref[...], s2p_ref[...]

        @pl.when(n < nb)
        def _():
            c, s1, s2 = c_ref[...], s1_ref[...], s2_ref[...]
            c8, s18, s28 = _tile_lanes(c, 8), _tile_lanes(s1, 8), _tile_lanes(s2, 8)
            q_raw = q_ref[...]
            qh, rq = _head_norm_fwd(q_raw, qn_ref[...], bdm)
            qr = (_rope(qh, c8, s18, s28) * scale).astype(MM)
            kc, _ = _head_norm_fwd(kvc_ref[:, 0:128], kn, bdm)
            k2 = jnp.concatenate([_rope(kph, cp, s1p, s2p), _rope(kc, c, s1, s2)], axis=0)
            v2 = jnp.concatenate([kvp_ref[:, 128:256], kvc_ref[:, 128:256]], axis=0)
            dob = db_ref[...].astype(MM)
            mask = _attn_mask_t(n)
            lane = lax.broadcasted_iota(jnp.int32, (1, LANES), 1)
            dsk = jnp.zeros((1, LANES), F32)
            dk2 = jnp.zeros((2 * BLOCK, LANES), F32)
            dv2 = jnp.zeros((2 * BLOCK, LANES), F32)
            for hk in range(N_KV_HEADS):
                qs = _stack_pairs(qr, hk)
                do = _stack_pairs(dob, hk)
                dqt = jnp.zeros((LANES, PAIR_COLS), F32)
                dkb, dvb = [], []
                for par, (kb, vb) in enumerate(zip(_parity_bands(k2, hk), _parity_bands(v2, hk))):
                    sc = jnp.where(mask, _mm_nt(kb, qs), NEG)
                    pr, psink = _softmax_sink_t(sc, sink_ref[hk, par])
                    dp = _mm_nt(vb, do)
                    coldot = jnp.sum(pr * dp, axis=0, keepdims=True)
                    ds = (pr * (dp - coldot)).astype(MM)
                    dsr = -psink * coldot
                    for j in range(PAIRS):
                        h = hk * GQA_GROUP + 2 * j + par
                        dsk = dsk + jnp.where(lane == h, jnp.sum(dsr[:, j * BLOCK:(j + 1) * BLOCK]), 0.0)
                    dqt = dqt + _mm(kb.T, ds)
                    dkb.append(_mm(ds, qs))
                    dvb.append(_mm(pr, do))
                for j in range(PAIRS):
                    col = (PAIRS * hk + j) * LANES
                    dqr_ref[:, col:col + LANES] = dqt[:, j * BLOCK:(j + 1) * BLOCK].T
                dk2 = dk2 + _fold_parity(dkb[0], dkb[1], hk)
                dv2 = dv2 + _fold_parity(dvb[0], dvb[1], hk)
            tot_ref[:, 0:128] = carry_ref[:, 0:128] + dk2[0:BLOCK, :]
            tot_ref[:, 128:256] = carry_ref[:, 128:256] + dv2[0:BLOCK, :]
            carry_ref[:, 0:128] = dk2[BLOCK:2 * BLOCK, :]
            carry_ref[:, 128:256] = dv2[BLOCK:2 * BLOCK, :]
            sacc_ref[...] += dsk
            dqh = _rope_bwd(dqr_ref[...] * scale, c8, s18, s28)
            dq, dgq = _head_norm_bwd(q_raw, rq, qn_ref[...], dqh, bdm)
            dzq_ref[...] = dq.astype(MM)
            qacc_ref[...] += jnp.sum(dgq, axis=0, keepdims=True)

        @pl.when(n == nb)
        def _():
            tot_ref[...] = carry_ref[...]

        dkh = _rope_bwd(tot_ref[:, 0:128], cp, s1p, s2p)
        dkr, dgk = _head_norm_bwd(kp_raw, rp, kn, dkh, bdm)
        dzkv_ref[:, 0:128] = dkr.astype(MM)
        dzkv_ref[:, 128:256] = tot_ref[:, 128:256].astype(MM)
        kacc_ref[...] += jnp.where(n > 0, jnp.sum(dgk, axis=0, keepdims=True), 0.0)

        @pl.when(n == nb)
        def _():
            fold = qacc_ref[:, 0:HEAD_DIM]
            for h in range(1, N_Q_HEADS):
                fold = fold + qacc_ref[:, h * HEAD_DIM:(h + 1) * HEAD_DIM]
            dqn_ref[...] = fold
            dkn_ref[...] = kacc_ref[:, 0:HEAD_DIM] + kacc_ref[:, HEAD_DIM:2 * HEAD_DIM]
            dsk_ref[...] = sacc_ref[...]

    tab = lambda im: pl.BlockSpec((BLOCK, LANES), im)
    sd = jax.ShapeDtypeStruct
    const = lambda n: (0, 0)
    return pl.pallas_call(
        body, name="attn_bwd", grid=(nb + 1,),
        out_shape=(sd((s, 1024), MM), sd((s, 256), MM), sd((1, HEAD_DIM), F32), sd((1, HEAD_DIM), F32),
                   sd((1, LANES), F32)),
        in_specs=[pl.BlockSpec((BLOCK, 1024), cur), pl.BlockSpec((BLOCK, 256), cur), pl.BlockSpec((BLOCK, 256), prv),
                  pl.BlockSpec((BLOCK, 1024), cur), tab(cur), tab(cur), tab(cur), tab(prv), tab(prv), tab(prv),
                  _resident((1, 1024)), _resident((1, 128)), _resident((N_KV_HEADS, 2, 1, PAIR_COLS)),
                  _resident((LANES, LANES))],
        out_specs=(pl.BlockSpec((BLOCK, 1024), cur), pl.BlockSpec((BLOCK, 256), prv),
                   pl.BlockSpec((1, HEAD_DIM), const), pl.BlockSpec((1, HEAD_DIM), const),
                   pl.BlockSpec((1, LANES), const)),
        scratch_shapes=[pltpu.VMEM((BLOCK, 256), F32), pltpu.VMEM((BLOCK, 256), F32), pltpu.VMEM((BLOCK, 1024), F32),
                        pltpu.VMEM((1, 1024), F32), pltpu.VMEM((1, 128), F32), pltpu.VMEM((1, LANES), F32)],
        compiler_params=_params("arbitrary"),
    )(q, kv, kv, db, *tabs, *tabs, q_norm_t, k_norm_t, sink_rows, bd)


def _inproj_bwd(dzu, dzq, dzkv, dzg, win_t, x, attn_norm, dx1, tm, after=None):
    s = x.shape[0]
    tie, tie_spec = _after(after)

    def body(du_ref, dq_ref, dkv_ref, dg_ref, w_ref, x_ref, gn_ref, dx1_ref, *rest):
        gx_ref, dgn_ref = rest[-2:]

        @pl.when(pl.program_id(0) == 0)
        def _():
            dgn_ref[...] = jnp.zeros_like(dgn_ref)

        dh = _mm(du_ref[...], w_ref[O_U:O_Q, :]) + _mm(dq_ref[...], w_ref[O_Q:O_KV, :])
        dh = dh + _mm(dkv_ref[...], w_ref[O_KV:O_G, :]) + _mm(dg_ref[...], w_ref[O_G:IN_WIDTH, :])
        x = x_ref[...]
        _, r = _rmsnorm_fwd(x, gn_ref[...])
        dx, dgr = _rmsnorm_bwd(x, r, gn_ref[...], dh)
        dgn_ref[...] += jnp.sum(dgr, axis=0, keepdims=True)
        gx_ref[...] = dx1_ref[...] + dx

    return pl.pallas_call(
        body, name="inproj_bwd", grid=(s // tm,),
        out_shape=(jax.ShapeDtypeStruct((s, D_MODEL), F32), jax.ShapeDtypeStruct((1, D_MODEL), F32)),
        in_specs=[_rows(tm, 1024), _rows(tm, 1024), _rows(tm, 256), _rows(tm, 2048),
                  _resident((IN_WIDTH, D_MODEL)), _rows(tm, 1024), _resident((1, 1024)), _rows(tm, 1024)] + tie_spec,
        out_specs=(_rows(tm, 1024), pl.BlockSpec((1, D_MODEL), lambda i: (0, 0))),
        compiler_params=_params("arbitrary"),
    )(dzu, dzq, dzkv, dzg, win_t, x, attn_norm, dx1, *tie)


def _attention_constants(q_norm, k_norm, sinks):
    inv_freq = ROPE_THETA ** (-jnp.arange(0, ROPE_DIM, 2, dtype=F32) / ROPE_DIM)
    lane = jnp.arange(LANES) % HEAD_DIM
    invf = jnp.where(lane < ROPE_DIM, inv_freq[lane % (ROPE_DIM // 2)], 0.0).reshape(1, LANES).astype(F32)
    bd = (jnp.arange(LANES)[:, None] // HEAD_DIM == jnp.arange(LANES)[None, :] // HEAD_DIM).astype(MM)
    q_norm_t = jnp.tile(q_norm, (1, N_Q_HEADS))
    k_norm_t = jnp.tile(k_norm, (1, N_KV_HEADS))
    sink_rows = jnp.repeat(sinks.reshape(N_KV_HEADS, PAIRS, 2).transpose(0, 2, 1), BLOCK, axis=2)
    sink_rows = sink_rows.reshape(N_KV_HEADS, 2, 1, PAIR_COLS)
    return invf, bd, q_norm_t, k_norm_t, sink_rows


ANY = pl.BlockSpec(memory_space=pl.ANY)


def _position():
    return lax.axis_index("x"), lax.axis_index("y"), lax.axis_index("c")


def _all_gather(shards):
    k = len(shards)

    def body(*refs):
        ins, outs = refs[:k], refs[k:2 * k]
        send_sems, recv_sems, local_sems = refs[2 * k:]
        x, y, c = _position()
        me, sibling = (x, y, c), (x, y, 1 - c)
        chips = [(1 - x, y), (x, 1 - y), (1 - x, 1 - y)]

        def copy(a, kk, block, to, src=None):
            dst = outs[a].at[4 * block[0] + 2 * block[1] + block[2]]
            return pltpu.make_async_remote_copy(
                src_ref=dst if src is None else src, dst_ref=dst, send_sem=send_sems.at[a * 7 + kk],
                recv_sem=recv_sems.at[a * 7 + kk], device_id=to, device_id_type=MESH)

        mine = [pltpu.make_async_copy(ins[a], outs[a].at[4 * x + 2 * y + c], local_sems.at[a]) for a in range(k)]
        for cp in mine:
            cp.start()
        first = []
        for a in range(k):
            first.append(copy(a, 0, me, sibling, src=ins[a]))
            first += [copy(a, 1 + j, me, (*chip, c), src=ins[a]) for j, chip in enumerate(chips)]
        for cp in first:
            cp.start()
        passed = []
        for j, chip in enumerate(chips):
            for a in range(k):
                copy(a, 1 + j, (*chip, c), me).wait_recv()
                cp = copy(a, 4 + j, (*chip, c), sibling)
                cp.start()
                passed.append(cp)
        for a in range(k):
            copy(a, 0, sibling, me).wait_recv()
            for j, chip in enumerate(chips):
                copy(a, 4 + j, (*chip, 1 - c), me).wait_recv()
        for cp in first + passed:
            cp.wait_send()
        for cp in mine:
            cp.wait()

    return pl.pallas_call(
        body, name="all_gather_weights",
        out_shape=tuple(jax.ShapeDtypeStruct((N_DEV,) + s.shape, s.dtype) for s in shards),
        in_specs=[ANY] * k, out_specs=(ANY,) * k,
        scratch_shapes=[pltpu.SemaphoreType.DMA((7 * k,)), pltpu.SemaphoreType.DMA((7 * k,)),
                        pltpu.SemaphoreType.DMA((k,))],
    )(*shards)


HBM = pl.BlockSpec(memory_space=pltpu.HBM)
SEM = pl.BlockSpec(memory_space=pltpu.SEMAPHORE)
EFFECT = pltpu.SideEffectType.DATAFLOW_SIDE_EFFECTING


def _exchange_start(name, bufs, n_sems, copies, after=None):
    k = len(bufs)
    tie, tie_spec = _after(after)
    n_in = k + len(tie)

    def body(*refs):
        for cp in copies(refs[:k], refs[n_in], refs[n_in + 1]):
            cp.start()
        refs[-1][...] = jnp.zeros_like(refs[-1])

    dma = pltpu.SemaphoreType.DMA((n_sems,))
    out = pl.pallas_call(
        body, name=name,
        out_shape=(dma, dma, *[pltpu.HBM(b.shape, b.dtype) for b in bufs], jax.ShapeDtypeStruct((8, LANES), F32)),
        in_specs=[HBM] * k + tie_spec, out_specs=(SEM, SEM, *[HBM] * k, pl.BlockSpec(memory_space=pltpu.VMEM)),
        input_output_aliases={i: 2 + i for i in range(k)},
        compiler_params=pltpu.CompilerParams(has_side_effects=EFFECT),
    )(*[pltpu.with_memory_space_constraint(b, pltpu.HBM) for b in bufs], *tie)
    return out[0], out[1], list(out[2:2 + k]), out[-1]


def _exchange_mid(name, bufs, sems_in, n_sems, waits, copies, after):
    k, ns = len(bufs), len(sems_in)

    def body(*refs):
        ins = refs[:k]
        waits(ins, *refs[k:k + ns])
        for cp in copies(ins, refs[k + ns + 1], refs[k + ns + 2]):
            cp.start()

    dma = pltpu.SemaphoreType.DMA((n_sems,))
    out = pl.pallas_call(
        body, name=name, out_shape=(dma, dma, *[pltpu.HBM(b.shape, b.dtype) for b in bufs]),
        in_specs=[HBM] * k + [SEM] * ns + [ANY], out_specs=(SEM, SEM, *[HBM] * k),
        input_output_aliases={i: 2 + i for i in range(k)},
        compiler_params=pltpu.CompilerParams(has_side_effects=EFFECT),
    )(*bufs, *sems_in, after)
    return out[0], out[1], list(out[2:])


def _exchange_wait(name, bufs, sems, waits, after=None):
    k, ns = len(bufs), len(sems)
    tie, tie_spec = _after(after)

    def body(*refs):
        waits(refs[:k], *refs[k:k + ns])

    out = pl.pallas_call(
        body, name=name, out_shape=tuple(pltpu.HBM(b.shape, b.dtype) for b in bufs),
        in_specs=[HBM] * k + [SEM] * ns + tie_spec, out_specs=(HBM,) * k,
        input_output_aliases={i: i for i in range(k)},
        compiler_params=pltpu.CompilerParams(has_side_effects=EFFECT),
    )(*bufs, *sems, *tie)
    return list(out)


def _gather_copies(k, direct):
    def copies(refs, send_sems, recv_sems):
        x, y, c = _position()
        chips = [(1 - x, y), (x, 1 - y), (1 - x, 1 - y)]
        out = []
        for a in range(k):
            land = refs[k + a]
            if direct:
                mine = land.at[4 * x + 2 * y + c]
                for kk, to in enumerate([(x, y, 1 - c)] + [(*chip, c) for chip in chips]):
                    out.append(pltpu.make_async_remote_copy(
                        src_ref=refs[a], dst_ref=mine, send_sem=send_sems.at[4 * a + kk],
                        recv_sem=recv_sems.at[4 * a + kk], device_id=to, device_id_type=MESH))
            else:
                for j, (px, py) in enumerate(chips):
                    slot = land.at[4 * px + 2 * py + c]
                    out.append(pltpu.make_async_remote_copy(
                        src_ref=slot, dst_ref=slot, send_sem=send_sems.at[3 * a + j], recv_sem=recv_sems.at[3 * a + j],
                        device_id=(x, y, 1 - c), device_id_type=MESH))
        return out
    return copies


def _all_gather_behind(shards, start_after, mid_after):
    k = len(shards)
    me = 4 * lax.axis_index("x") + 2 * lax.axis_index("y") + lax.axis_index("c")
    lands = [lax.dynamic_update_slice(lax.empty((N_DEV,) + s.shape, s.dtype), s[None], (me, 0, 0)) for s in shards]
    direct, passed = _gather_copies(k, True), _gather_copies(k, False)

    send_a, recv_a, bufs, token = _exchange_start("gather_start", list(shards) + lands, 4 * k, direct, start_after)

    def finish():
        def wait_ici(refs, send_sems, recv_sems):
            for i, cp in enumerate(direct(refs, send_sems, recv_sems)):
                if i % 4:
                    cp.wait_recv()

        send_b, recv_b, bufs2 = _exchange_mid("gather_pass", bufs, [send_a, recv_a], 3 * k, wait_ici, passed,
                                              mid_after())

        def wait_all(refs, sa, ra, sb, rb):
            for i, cp in enumerate(direct(refs, sa, ra)):
                cp.wait_send()
                if i % 4 == 0:
                    cp.wait_recv()
            for cp in passed(refs, sb, rb):
                cp.wait()

        return _exchange_wait("gather_wait", bufs2, [send_a, recv_a, send_b, recv_b], wait_all)[k:]

    return token, finish


def _pair_copies(k):
    def copies(refs, send_sems, recv_sems):
        x, y, c = _position()
        return [pltpu.make_async_remote_copy(
            src_ref=refs[a].at[2 * ch + 1 - c], dst_ref=refs[k + a].at[ch], send_sem=send_sems.at[4 * a + ch],
            recv_sem=recv_sems.at[4 * a + ch], device_id=(x, y, 1 - c), device_id_type=MESH)
            for a in range(k) for ch in range(4)]
    return copies


def _chip_copies(k):
    def copies(refs, send_sems, recv_sems):
        x, y, c = _position()
        return [pltpu.make_async_remote_copy(
            src_ref=refs[a].at[2 * px + py], dst_ref=refs[k + a].at[rel], send_sem=send_sems.at[3 * a + rel],
            recv_sem=recv_sems.at[3 * a + rel], device_id=(px, py, c), device_id_type=MESH)
            for a in range(k) for rel, (px, py) in enumerate([(1 - x, y), (x, 1 - y), (1 - x, 1 - y)])]
    return copies


def _symmetric_exchange(name, srcs, n_land, copies_of):
    k = len(srcs)
    lands = [lax.empty((n_land,) + s.shape[1:], s.dtype) for s in srcs]
    copies = copies_of(k)
    send_sems, recv_sems, bufs, token = _exchange_start(name + "_start", list(srcs) + lands, n_land * k, copies)

    def finish(after):
        def wait_all(refs, ss, rs):
            for cp in copies(refs, ss, rs):
                cp.wait()

        done = _exchange_wait(name + "_wait", bufs, [send_sems, recv_sems], wait_all, after)
        return done[:k], done[k:]

    return token, finish


def _pair_add(full, recv, wire):
    _, r, c_ = full.shape

    def body(f_ref, r_ref, pw_ref, own_ref):
        ch = pl.program_id(0)
        x, y, c = _position()
        tot = f_ref[0, c] + r_ref[0]
        pw_ref[0] = tot.astype(pw_ref.dtype)

        @pl.when(ch == 2 * x + y)
        def _():
            own_ref[...] = tot

    return pl.pallas_call(
        body, name="grad_pair_add", grid=(4,),
        out_shape=(jax.ShapeDtypeStruct((4, r, c_), wire), jax.ShapeDtypeStruct((r, c_), F32)),
        in_specs=[pl.BlockSpec((1, 2, r, c_), lambda i: (i, 0, 0, 0)), pl.BlockSpec((1, r, c_), lambda i: (i, 0, 0))],
        out_specs=(pl.BlockSpec((1, r, c_), lambda i: (i, 0, 0)), pl.BlockSpec((r, c_), lambda i: (0, 0))),
        compiler_params=_params("arbitrary"),
    )(full.reshape(4, 2, r, c_), recv)


def _adamw_math(w, g, m, v):
    m = ADAM_B1 * m + (1.0 - ADAM_B1) * g
    v = ADAM_B2 * v + (1.0 - ADAM_B2) * (g * g)
    m_hat = m / (1.0 - ADAM_B1 ** ADAM_STEP)
    v_hat = v / (1.0 - ADAM_B2 ** ADAM_STEP)
    delta = -ADAM_LR * (m_hat / (jnp.sqrt(v_hat) + ADAM_EPS) + ADAM_WD * w)
    return delta, m, v


def _row_tile(r):
    for t in (256, 272, 176, 128):
        if r % t == 0 and r > t:
            return t
    return r


def _adamw(g_own, recv, w, m, v):
    r, c_ = w.shape
    t = _row_tile(r)
    blk = pl.BlockSpec((t, c_), lambda i: (i, 0))

    def body(g_ref, r_ref, w_ref, m_ref, v_ref, go_ref, d_ref, mo_ref, vo_ref):
        g = g_ref[...]
        for i in range(3):
            g = g + r_ref[i].astype(F32)
        go_ref[...] = g
        d_ref[...], mo_ref[...], vo_ref[...] = _adamw_math(w_ref[...], g, m_ref[...], v_ref[...])

    return pl.pallas_call(
        body, name="adamw", grid=(r // t,), out_shape=(jax.ShapeDtypeStruct((r, c_), F32),) * 4,
        in_specs=[blk, pl.BlockSpec((3, t, c_), lambda i: (0, i, 0)), blk, blk, blk], out_specs=(blk,) * 4,
        compiler_params=_params("parallel"),
    )(g_own, recv, w, m, v)


SMALL = ("attn_norm", "b_gate", "pool_scale", "q_norm", "k_norm", "sinks", "ffn_norm", "conv_b")
SMALL_SIZES = (1024, 2048, 1024, 64, 64, 16, 1024, 5632)
SMALL_OFFSETS = tuple(sum(-(-s // LANES) * LANES for s in SMALL_SIZES[:i]) for i in range(len(SMALL_SIZES) + 1))
SMALL_WIDTH = SMALL_OFFSETS[-1] + LANES


def _pack_small(d, loss=None):
    parts = [jnp.pad(d[n].reshape(1, -1), ((0, 0), (0, -s % LANES))) for n, s in zip(SMALL, SMALL_SIZES)]
    last = jnp.zeros((1, LANES), F32) if loss is None else jnp.pad(loss.reshape(1, 1), ((0, 0), (0, LANES - 1)))
    return jnp.concatenate(parts + [last], axis=1)


def _small_allreduce(gp):
    def body(g_ref, sum_ref, slots_ref, send_sems, recv_sems):
        x, y, c = _position()
        me = 4 * x + 2 * y + c
        slots_ref[me] = g_ref[...]
        cps = []
        for rel in range(1, N_DEV):
            fx, fy, fc = (rel >> 2) & 1, (rel >> 1) & 1, rel & 1
            to = (1 - x if fx else x, 1 - y if fy else y, 1 - c if fc else c)
            cps.append(pltpu.make_async_remote_copy(
                src_ref=g_ref, dst_ref=slots_ref.at[me], send_sem=send_sems.at[rel - 1],
                recv_sem=recv_sems.at[rel - 1], device_id=to, device_id_type=MESH))
        for cp in cps:
            cp.start()
        for cp in cps:
            cp.wait()
        g = slots_ref[0]
        for i in range(1, N_DEV):
            g = g + slots_ref[i]
        sum_ref[...] = g

    vm = pl.BlockSpec(memory_space=pltpu.VMEM)
    return pl.pallas_call(
        body, name="small_allreduce", out_shape=jax.ShapeDtypeStruct((1, SMALL_WIDTH), F32),
        in_specs=[vm], out_specs=vm,
        scratch_shapes=[pltpu.VMEM((N_DEV, 1, SMALL_WIDTH), F32), pltpu.SemaphoreType.DMA((N_DEV - 1,)),
                        pltpu.SemaphoreType.DMA((N_DEV - 1,))],
    )(gp)


def _small_adamw(gsum, ws, ms, vs):
    n = len(SMALL)

    def body(g_ref, *rest):
        w_refs, m_refs, v_refs, outs = rest[:n], rest[n:2 * n], rest[2 * n:3 * n], rest[3 * n:]
        for j, size in enumerate(SMALL_SIZES):
            g = g_ref[:, SMALL_OFFSETS[j]:SMALL_OFFSETS[j] + size]
            results = (g,) + _adamw_math(w_refs[j][...], g, m_refs[j][...], v_refs[j][...])
            for kind, val in enumerate(results):
                outs[kind * n + j][...] = val
        outs[4 * n][...] = g_ref[:, SMALL_OFFSETS[-1]:SMALL_WIDTH]

    vm = pl.BlockSpec(memory_space=pltpu.VMEM)
    shapes = tuple(jax.ShapeDtypeStruct((1, s), F32) for s in SMALL_SIZES) * 4
    return pl.pallas_call(
        body, name="small_adamw", out_shape=shapes + (jax.ShapeDtypeStruct((1, LANES), F32),),
        in_specs=[vm] * (1 + 3 * n), out_specs=(vm,) * (4 * n + 1),
    )(gsum, *ws, *ms, *vs)


WEIGHTS = ("attn_norm", "w_in", "b_gate", "w_pool", "pool_scale", "q_norm", "k_norm", "sinks", "w_out", "ffn_norm",
           "w_up", "conv_w", "conv_b", "w_down")


def kernel(x, positions, attn_norm, w_in, b_gate, w_pool, pool_scale, q_norm, k_norm, sinks, w_out, ffn_norm, w_up, conv_w, conv_b, w_down, loss_target, m_attn_norm, m_w_in, m_b_gate, m_w_pool, m_pool_scale, m_q_norm, m_k_norm, m_sinks, m_w_out, m_ffn_norm, m_w_up, m_conv_w, m_conv_b, m_w_down, v_attn_norm, v_w_in, v_b_gate, v_w_pool, v_pool_scale, v_q_norm, v_k_norm, v_sinks, v_w_out, v_ffn_norm, v_w_up, v_conv_w, v_conv_b, v_w_down):
    w = dict(attn_norm=attn_norm, w_in=w_in, b_gate=b_gate, w_pool=w_pool, pool_scale=pool_scale, q_norm=q_norm,
             k_norm=k_norm, sinks=sinks, w_out=w_out, ffn_norm=ffn_norm, w_up=w_up, conv_w=conv_w, conv_b=conv_b,
             w_down=w_down)
    m = dict(attn_norm=m_attn_norm, w_in=m_w_in, b_gate=m_b_gate, w_pool=m_w_pool, pool_scale=m_pool_scale,
             q_norm=m_q_norm, k_norm=m_k_norm, sinks=m_sinks, w_out=m_w_out, ffn_norm=m_ffn_norm, w_up=m_w_up,
             conv_w=m_conv_w, conv_b=m_conv_b, w_down=m_w_down)
    v = dict(attn_norm=v_attn_norm, w_in=v_w_in, b_gate=v_b_gate, w_pool=v_w_pool, pool_scale=v_pool_scale,
             q_norm=v_q_norm, k_norm=v_k_norm, sinks=v_sinks, w_out=v_w_out, ffn_norm=v_ffn_norm, w_up=v_w_up,
             conv_w=v_conv_w, conv_b=v_conv_b, w_down=v_w_down)
    seq = x.shape[1]
    tm = 256
    tw = min(seq, 512)
    tk = min(seq, 1024)
    xs, target, pos_col = x[0], loss_target[0], positions.reshape(seq, 1)
    invf, bd, q_norm_t, k_norm_t, sink_rows = _attention_constants(q_norm, k_norm, sinks)
    out, done = {}, {}
    nat = {"w_in": (D_MODEL, 544), "w_pool": (128, POOL_GROUP), "w_out": (128, D_MODEL), "w_up": (D_MODEL, 704),
           "conv_w": (3, 704), "w_down": (352, D_MODEL)}

    def update(names, owns, recvs):
        for name, own, recv in zip(names, owns, recvs):
            w2, m2, v2 = (t[name].reshape(nat[name]) for t in (w, m, v))
            if name in ("w_in", "w_up"):
                res = _adamw(own, recv, w2.T, m2.T, v2.T)
                done[name] = res[1]
                res = [t.T for t in res]
            else:
                res = _adamw(own, recv, w2, m2, v2)
                done[name] = res[1]
            out[name] = [t.reshape(w[name].shape) for t in res]

    (g_win,) = _all_gather([w_in[0].T.astype(MM)])
    win_t = g_win.reshape(IN_WIDTH, D_MODEL)
    fwd = {}
    token, gather_rest = _all_gather_behind(
        [w_pool[0].astype(MM).reshape(128, POOL_GROUP), w_out[0].astype(MM), w_up[0].T.astype(MM), conv_w[0],
         w_down[0].astype(MM)], g_win, lambda: fwd["b"])

    tabs = _rope_tables(pos_col, invf)
    h, u, q, kv, g = _inproj_fwd(xs, attn_norm + token[0:1, 0:1], win_t, b_gate, tw)
    fwd["b"] = b = _attn_fwd(q, kv, tabs, q_norm_t, k_norm_t, sink_rows, bd)
    g_wpool, g_wout, g_wup, g_convw, g_wdown = gather_rest()
    wpool = g_wpool.reshape(N_DEV, 4, 32, POOL_GROUP).transpose(1, 0, 2, 3).reshape(4, POOL_GROUP, POOL_GROUP)
    wout = g_wout.reshape(D_MODEL, D_MODEL)
    wup_t = g_wup.reshape(2 * D_FF, D_MODEL)
    convw = g_convw.transpose(1, 0, 2).reshape(3, 2 * D_FF)
    wdown = g_wdown.reshape(D_FF, D_MODEL)
    a = _pool_fwd(u, wpool, pool_scale, tw)
    x1, mix = _mix_out_fwd(xs, g, a, b, wout, tw)
    h2, uff, dy, dyb, lossp = _ffn_fwd(x1, ffn_norm, wup_t, convw, conv_b, wdown, target, tm)

    du, act, d_conv_w, d_conv_b = _ffn_bwd_a(dyb, uff, convw, conv_b, wdown, tm)
    d_wdown = _matmul_tn(act, dyb, FF_CHUNK, tk, "dw_down")
    dx1, dx1b, d_ffn_norm = _ffn_bwd_b(du, wup_t, x1, ffn_norm, dy, tm)
    d_wup_t = _matmul_tn(du, h2, FF_CHUNK, tk, "dw_up")
    late = ("w_down", "w_up", "conv_w")
    late_wire = (WIRE, WIRE, F32)
    late_full = [d_wdown.reshape(N_DEV, 352, D_MODEL), d_wup_t.reshape(N_DEV, 704, D_MODEL),
                 d_conv_w.reshape(3, N_DEV, 704).transpose(1, 0, 2)]
    token, late_pair = _symmetric_exchange("late_pair", late_full, 4, _pair_copies)
    d_wout = _matmul_tn(mix, dx1b, 1024, tk, "dw_out", after=token)
    da, db, dzg, d_b_gate = _mix_bwd(dx1b, wout, g, a, b, tw)
    late_pw, late_own = zip(*[_pair_add(f, r, wd) for f, r, wd in zip(*late_pair(dzg), late_wire)])
    token, late_chip = _symmetric_exchange("late_chip", list(late_pw), 3, _chip_copies)
    d_win_t = _matmul_tn(dzg, h, 1024, tk, "dw_in_g", after=token, into=IN_WIDTH, row0=O_G)
    dzu, d_wpool, d_pool_scale = _pool_bwd(u, da, wpool, pool_scale, tw)
    d_win_t = _matmul_tn(dzu, h, 1024, tk, "dw_in_u", into=d_win_t, row0=O_U)
    dzq, dzkv, d_q_norm, d_k_norm, d_sinks = _attn_bwd(q, kv, db, tabs, q_norm_t, k_norm_t, sink_rows, bd)
    d_win_t = _matmul_tn(dzq, h, 1024, tk, "dw_in_q", into=d_win_t, row0=O_Q)
    d_win_t = _matmul_tn(dzkv, h, 256, tk, "dw_in_kv", into=d_win_t, row0=O_KV)
    early = ("w_in", "w_pool", "w_out")
    early_full = [d_win_t.reshape(N_DEV, 544, D_MODEL),
                  d_wpool.reshape(4, N_DEV, 32, POOL_GROUP).transpose(1, 0, 2, 3).reshape(N_DEV, 128, POOL_GROUP),
                  d_wout.reshape(N_DEV, 128, D_MODEL)]
    token, early_pair = _symmetric_exchange("early_pair", early_full, 4, _pair_copies)
    update(late, late_own, late_chip(token)[1])
    early_pw, early_own = zip(*[_pair_add(f, r, WIRE) for f, r in zip(*early_pair([done[n] for n in late]))])
    token, early_chip = _symmetric_exchange("early_chip", list(early_pw), 3, _chip_copies)
    grad_x, d_attn_norm = _inproj_bwd(dzu, dzq, dzkv, dzg, win_t, xs, attn_norm, dx1, tw, after=token)
    gr = dict(attn_norm=d_attn_norm, b_gate=d_b_gate, pool_scale=d_pool_scale, q_norm=d_q_norm, k_norm=d_k_norm,
              sinks=d_sinks[:, 0:N_Q_HEADS], ffn_norm=d_ffn_norm, conv_b=d_conv_b)
    small = _small_adamw(_small_allreduce(_pack_small(gr, lossp[0, 0])), *[[t[n] for n in SMALL] for t in (w, m, v)])
    loss = small[-1][0, 0]
    for j, name in enumerate(SMALL):
        out[name] = [small[kind * len(SMALL) + j] for kind in range(4)]
    update(early, early_own, early_chip(small[0])[1])

    return (loss, grad_x[None], *[out[n][0] for n in WEIGHTS], *[out[n][1] for n in WEIGHTS],
            *[out[n][2] for n in WEIGHTS], *[out[n][3] for n in WEIGHTS])
```

```python
import functools

import numpy as np
import jax
import jax.numpy as jnp
from jax import lax
from jax.experimental import pallas as pl
from jax.experimental.pallas import tpu as pltpu

F32 = jnp.float32
MM = jnp.bfloat16
WIRE = jnp.bfloat16
ACT = jnp.bfloat16

D_MODEL = 1024
D_FF = 2816
HEAD_DIM = 64
N_Q_HEADS = 16
N_KV_HEADS = 2
GQA_GROUP = 8
BLOCK = 128
ROPE_DIM = 16
ROPE_THETA = 500000.0
POOL_WINDOWS = (2, 4, 8, 16)
POOL_GROUP = 256
POOL_HALO = 16
CONV_HALO = 8
EPS = 1e-6
NEG = -1e30
O_U, O_Q, O_KV, O_G, IN_WIDTH = 0, 1024, 2048, 2304, 4352
FF_CHUNK = 1408

ADAM_LR, ADAM_B1, ADAM_B2, ADAM_EPS, ADAM_WD, ADAM_STEP = 0.001, 0.9, 0.999, 1e-08, 0.01, 10

N_DEV = 8
LANES = 128
VMEM_LIMIT_BYTES = 56 * 1024 * 1024
MESH = pl.DeviceIdType.MESH


def _params(*sem):
    return pltpu.CompilerParams(dimension_semantics=sem, vmem_limit_bytes=VMEM_LIMIT_BYTES)


def _resident(shape):
    nd = len(shape)
    return pl.BlockSpec(shape, lambda *_: (0,) * nd, pipeline_mode=pl.Buffered(1))


def _rows(tm, width):
    return pl.BlockSpec((tm, width), lambda i: (i, 0))


def _mm(a, b):
    return jnp.dot(a.astype(MM), b.astype(MM), preferred_element_type=F32)


def _mm_nt(a, b):
    return lax.dot_general(a.astype(MM), b.astype(MM), (((1,), (1,)), ((), ())), preferred_element_type=F32)


def _mm_tn(a, b):
    return lax.dot_general(a.astype(MM), b.astype(MM), (((0,), (0,)), ((), ())), preferred_element_type=F32)


def _rmsnorm_fwd(x, g):
    r = lax.rsqrt(jnp.mean(x * x, axis=-1, keepdims=True) + EPS)
    return x * r * g, r


def _rmsnorm_bwd(x, r, g, dy):
    xn = x * r
    dxn = dy * g
    dx = r * (dxn - xn * jnp.mean(dxn * xn, axis=-1, keepdims=True))
    return dx, dy * xn


def _group_sum64(v, bd):
    hi = v.astype(MM)
    lo = (v - hi.astype(F32)).astype(MM)
    outs = []
    for t in range(v.shape[1] // LANES):
        sl = slice(LANES * t, LANES * (t + 1))
        outs.append(jnp.dot(hi[:, sl], bd, preferred_element_type=F32)
                    + jnp.dot(lo[:, sl], bd, preferred_element_type=F32))
    return outs[0] if len(outs) == 1 else jnp.concatenate(outs, axis=1)


def _head_norm_fwd(x, g, bd):
    r = lax.rsqrt(_group_sum64(x * x, bd) * (1.0 / HEAD_DIM) + EPS)
    return x * r * g, r


def _head_norm_bwd(x, r, g, dy, bd):
    xn = x * r
    dxn = dy * g
    dx = r * (dxn - xn * (_group_sum64(dxn * xn, bd) * (1.0 / HEAD_DIM)))
    return dx, dy * xn


def _rope(x, c, s1, s2):
    w = x.shape[1]
    return x * c + pltpu.roll(x, w - ROPE_DIM // 2, 1) * s1 + pltpu.roll(x, ROPE_DIM // 2, 1) * s2


def _rope_bwd(dy, c, s1, s2):
    w = dy.shape[1]
    return dy * c + pltpu.roll(dy * s1, ROPE_DIM // 2, 1) + pltpu.roll(dy * s2, w - ROPE_DIM // 2, 1)


def _tile_lanes(t, reps):
    return t if reps == 1 else jnp.concatenate([t] * reps, axis=1)


def _rope_tables(pos_col, invf):
    s = pos_col.shape[0]
    tm = min(s, 1024)

    def body(pos_ref, invf_ref, c_ref, s1_ref, s2_ref):
        ang = pos_ref[...].astype(F32) * invf_ref[...]
        lane = lax.broadcasted_iota(jnp.int32, ang.shape, 1) % HEAD_DIM
        sn = jnp.sin(ang)
        c_ref[...] = jnp.cos(ang)
        s1_ref[...] = jnp.where(lane < ROPE_DIM // 2, -sn, 0.0)
        s2_ref[...] = jnp.where((lane >= ROPE_DIM // 2) & (lane < ROPE_DIM), sn, 0.0)

    out = jax.ShapeDtypeStruct((s, LANES), F32)
    return pl.pallas_call(
        body, name="rope_tables", grid=(s // tm,), out_shape=(out, out, out),
        in_specs=[_rows(tm, 1), _resident((1, LANES))],
        out_specs=(_rows(tm, LANES),) * 3, compiler_params=_params("parallel"),
    )(pos_col, invf)


def _inproj_fwd(x, attn_norm, win_t, b_gate, tm):
    s = x.shape[0]

    def body(x_ref, gn_ref, w_ref, bg_ref, h_ref, u_ref, q_ref, kv_ref, g_ref):
        h, _ = _rmsnorm_fwd(x_ref[...], gn_ref[...])
        h = h.astype(MM)
        h_ref[...] = h
        u_ref[...] = _mm_nt(h, w_ref[O_U:O_Q, :])
        q_ref[...] = _mm_nt(h, w_ref[O_Q:O_KV, :])
        kv_ref[...] = _mm_nt(h, w_ref[O_KV:O_G, :])
        g_ref[...] = jax.nn.sigmoid(_mm_nt(h, w_ref[O_G:IN_WIDTH, :]) + bg_ref[...])

    sd = jax.ShapeDtypeStruct
    return pl.pallas_call(
        body, name="inproj_fwd", grid=(s // tm,),
        out_shape=(sd((s, D_MODEL), MM), sd((s, 1024), F32), sd((s, 1024), F32), sd((s, 256), F32),
                   sd((s, 2048), F32)),
        in_specs=[_rows(tm, D_MODEL), _resident((1, D_MODEL)), _resident((IN_WIDTH, D_MODEL)), _resident((1, 2048))],
        out_specs=(_rows(tm, D_MODEL), _rows(tm, 1024), _rows(tm, 1024), _rows(tm, 256), _rows(tm, 2048)),
        compiler_params=_params("parallel"),
    )(x, attn_norm, win_t, b_gate)


def _pooled(ext_ref, tm, row0):
    t = (row0 + lax.broadcasted_iota(jnp.int32, (tm, 1), 0)).astype(F32)
    out = []
    for gi, w in enumerate(POOL_WINDOWS):
        cols = slice(gi * POOL_GROUP, (gi + 1) * POOL_GROUP)
        acc = ext_ref[pl.ds(POOL_HALO, tm), cols]
        for k in range(1, w):
            acc = acc + ext_ref[pl.ds(POOL_HALO - k, tm), cols]
        cnt = jnp.minimum(t + 1.0, float(w))
        out.append(acc / cnt - ext_ref[pl.ds(POOL_HALO, tm), cols])
    return out


def _pool_fwd(u, wpool, pool_scale, tm):
    s = u.shape[0]
    hb = tm // POOL_HALO

    def body(u_ref, halo_ref, wp_ref, ps_ref, a_ref, ext_ref):
        i = pl.program_id(0)
        ext_ref[pl.ds(0, POOL_HALO), :] = jnp.where(i > 0, halo_ref[...], 0.0)
        ext_ref[pl.ds(POOL_HALO, tm), :] = u_ref[...]
        pooled = _pooled(ext_ref, tm, i * tm)
        for gi in range(4):
            cols = slice(gi * POOL_GROUP, (gi + 1) * POOL_GROUP)
            a_ref[:, cols] = (_mm(pooled[gi], wp_ref[gi]) * ps_ref[:, cols]).astype(a_ref.dtype)

    return pl.pallas_call(
        body, name="pool_fwd", grid=(s // tm,), out_shape=jax.ShapeDtypeStruct((s, 1024), ACT),
        in_specs=[_rows(tm, 1024), pl.BlockSpec((POOL_HALO, 1024), lambda i: (jnp.maximum(i * hb - 1, 0), 0)),
                  _resident((4, POOL_GROUP, POOL_GROUP)), _resident((1, 1024))],
        out_specs=_rows(tm, 1024), scratch_shapes=[pltpu.VMEM((POOL_HALO + tm, 1024), F32)],
        compiler_params=_params("parallel"),
    )(u, u, wpool, pool_scale)


PAIRS = GQA_GROUP // 2
PAIR_COLS = PAIRS * BLOCK


def _fold_masks(n):
    r = lax.broadcasted_iota(jnp.int32, (BLOCK, BLOCK), 0)
    i = lax.broadcasted_iota(jnp.int32, (BLOCK, BLOCK), 1)
    prev = r > i
    return prev, jnp.where(prev & (n == 0), NEG, 0.0)


def _fold(band, prev):
    return jnp.where(prev, band[0:BLOCK, :], band[BLOCK:2 * BLOCK, :])


def _unfold(folded, prev):
    top = jnp.where(prev, folded, 0.0)
    return jnp.concatenate([top, folded - top], axis=0).astype(MM)


def _probs_by_pair(sc, sink, masks):
    prev, bias = masks
    out = []
    for j in range(PAIRS):
        cols = slice(j * BLOCK, (j + 1) * BLOCK)
        out.append(_softmax_sink_t(_fold(sc[:, cols], prev) + bias, sink[:, cols]))
    return out


def _stack_pairs(x, hk):
    return jnp.concatenate([x[:, (PAIRS * hk + j) * LANES:(PAIRS * hk + j + 1) * LANES] for j in range(PAIRS)], axis=0)


def _parity_bands(t, hk):
    low = lax.broadcasted_iota(jnp.int32, t.shape, 1) < HEAD_DIM
    own = jnp.where(low if hk == 0 else ~low, t, 0.0)
    other = pltpu.roll(own, HEAD_DIM, 1)
    return (own, other) if hk == 0 else (other, own)


def _fold_parity(even, odd, hk):
    low = lax.broadcasted_iota(jnp.int32, even.shape, 1) < HEAD_DIM
    comb = jnp.where(low, even, odd)
    comb = comb + pltpu.roll(comb, HEAD_DIM, 1)
    return jnp.where(low if hk == 0 else ~low, comb, 0.0)


def _softmax_sink_t(s, sink):
    m = jnp.maximum(jnp.max(s, axis=0, keepdims=True), sink)
    p = jnp.exp(s - m)
    es = jnp.exp(sink - m)
    inv = 1.0 / (jnp.sum(p, axis=0, keepdims=True) + es)
    return p * inv, es * inv


def _attn_fwd(q, kv, tabs, q_norm_t, k_norm_t, sink_rows, bd):
    s = q.shape[0]
    nb = s // BLOCK
    scale = HEAD_DIM ** -0.5
    cur = lambda n: (n, 0)
    prv = lambda n: (jnp.maximum(n - 1, 0), 0)

    def body(q_ref, kvc_ref, kvp_ref, c_ref, s1_ref, s2_ref, cp_ref, s1p_ref, s2p_ref, qn_ref, kn_ref, sink_ref,
             bd_ref, o_ref):
        n = pl.program_id(0)
        bdm = bd_ref[...]
        c, s1, s2 = c_ref[...], s1_ref[...], s2_ref[...]
        qh, _ = _head_norm_fwd(q_ref[...], qn_ref[...], bdm)
        qr = (_rope(qh, _tile_lanes(c, 8), _tile_lanes(s1, 8), _tile_lanes(s2, 8)) * scale).astype(MM)
        kc, _ = _head_norm_fwd(kvc_ref[:, 0:128], kn_ref[...], bdm)
        kp, _ = _head_norm_fwd(kvp_ref[:, 0:128], kn_ref[...], bdm)
        k2 = jnp.concatenate([_rope(kp, cp_ref[...], s1p_ref[...], s2p_ref[...]), _rope(kc, c, s1, s2)], axis=0)
        v2 = jnp.concatenate([kvp_ref[:, 128:256], kvc_ref[:, 128:256]], axis=0)
        mask = _fold_masks(n)
        for hk in range(N_KV_HEADS):
            qs = _stack_pairs(qr, hk)
            ot = jnp.zeros((LANES, PAIR_COLS), F32)
            for par, (kb, vb) in enumerate(zip(_parity_bands(k2, hk), _parity_bands(v2, hk))):
                probs = _probs_by_pair(_mm_nt(kb, qs), sink_ref[hk, par], mask)
                ot = ot + _mm(vb.T, jnp.concatenate([_unfold(pr, mask[0]) for pr, _ in probs], axis=1))
            for j in range(PAIRS):
                col = (PAIRS * hk + j) * LANES
                o_ref[:, col:col + LANES] = ot[:, j * BLOCK:(j + 1) * BLOCK].T.astype(o_ref.dtype)

    tab = lambda im: pl.BlockSpec((BLOCK, LANES), im)
    return pl.pallas_call(
        body, name="attn_fwd", grid=(nb,), out_shape=jax.ShapeDtypeStruct((s, 1024), ACT),
        in_specs=[pl.BlockSpec((BLOCK, 1024), cur), pl.BlockSpec((BLOCK, 256), cur), pl.BlockSpec((BLOCK, 256), prv),
                  tab(cur), tab(cur), tab(cur), tab(prv), tab(prv), tab(prv),
                  _resident((1, 1024)), _resident((1, 128)), _resident((N_KV_HEADS, 2, 1, PAIR_COLS)),
                  _resident((LANES, LANES))],
        out_specs=pl.BlockSpec((BLOCK, 1024), cur), compiler_params=_params("parallel"),
    )(q, kv, kv, *tabs, *tabs, q_norm_t, k_norm_t, sink_rows, bd)


def _mix_out_fwd(x, g, a, b, wout, tm):
    s = x.shape[0]

    def body(x_ref, g_ref, a_ref, b_ref, w_ref, x1_ref, mix_ref):
        mix = (g_ref[:, 0:1024] * a_ref[...] + g_ref[:, 1024:2048] * b_ref[...]).astype(MM)
        mix_ref[...] = mix
        x1_ref[...] = x_ref[...] + _mm(mix, w_ref[...])

    return pl.pallas_call(
        body, name="mix_out_fwd", grid=(s // tm,),
        out_shape=(jax.ShapeDtypeStruct((s, D_MODEL), F32), jax.ShapeDtypeStruct((s, D_MODEL), MM)),
        in_specs=[_rows(tm, 1024), _rows(tm, 2048), _rows(tm, 1024), _rows(tm, 1024), _resident((1024, 1024))],
        out_specs=(_rows(tm, 1024), _rows(tm, 1024)), compiler_params=_params("parallel"),
    )(x, g, a, b, wout)


SHIFT_ROWS = 16


def _sublane_major_matrices(tm):
    r = np.arange(tm)
    pm = r[None, :] == ((tm // 8) * (r % 8) + r // 8)[:, None]
    return jnp.asarray(pm, MM), jnp.asarray(pm.T, MM)


def _to_sublane_major(pm, v):
    return jnp.dot(pm, v, preferred_element_type=F32).astype(MM)


def _to_time_order(pmt, v):
    hi = v.astype(MM)
    r1 = v - hi.astype(F32)
    mid = r1.astype(MM)
    lo = (r1 - mid.astype(F32)).astype(MM)
    dot = functools.partial(jnp.dot, preferred_element_type=F32)
    return dot(pmt, hi) + dot(pmt, mid) + dot(pmt, lo)


def _step_back(vreg_rows, before):
    sub = lax.broadcasted_iota(jnp.int32, vreg_rows.shape, 0)
    return jnp.where(sub == 0, before[7:8, :], pltpu.roll(vreg_rows, 1, 0))


def _step_ahead(vreg_rows, after):
    sub = lax.broadcasted_iota(jnp.int32, vreg_rows.shape, 0)
    return jnp.where(sub == 7, after[0:1, :], pltpu.roll(vreg_rows, 7, 0))


def _fill_back_rows(ext_ref, before, tm, cols):
    last = ext_ref[pl.ds(SHIFT_ROWS + tm - 8, 8), cols]
    pen = ext_ref[pl.ds(SHIFT_ROWS + tm - 16, 8), cols]
    ext_ref[pl.ds(8, 8), cols] = _step_back(last, before[8:16, :])
    ext_ref[pl.ds(0, 8), cols] = _step_back(pen, before[0:8, :])


def _conv_glu(ext_ref, cw_ref, cb_ref, tm, c):
    out = []
    for base in (c * FF_CHUNK, D_FF + c * FF_CHUNK):
        cols = slice(base, base + FF_CHUNK)
        y = cb_ref[:, cols] + cw_ref[0:1, cols] * ext_ref[pl.ds(0, tm), cols]
        y = y + cw_ref[1:2, cols] * ext_ref[pl.ds(8, tm), cols]
        y = y + cw_ref[2:3, cols] * ext_ref[pl.ds(SHIFT_ROWS, tm), cols]
        out.append(y)
    return out


def _ffn_fwd(x1, ffn_norm, wup_t, conv_w, conv_b, wdown, target, tm):
    s = x1.shape[0]
    inv_d = 1.0 / D_MODEL
    pm, pmt = _sublane_major_matrices(tm)

    def body(x1_ref, gn_ref, wu_ref, cw_ref, cb_ref, wd_ref, tgt_ref, pm_ref, pmt_ref, h2_ref, u_ref, dy_ref, dyb_ref,
             loss_ref, ext_ref, carry_ref):
        i = pl.program_id(0)

        @pl.when(i == 0)
        def _():
            carry_ref[...] = jnp.zeros_like(carry_ref)
            loss_ref[...] = jnp.zeros_like(loss_ref)

        x1 = x1_ref[...]
        h2, _ = _rmsnorm_fwd(x1, gn_ref[...])
        h2 = _to_sublane_major(pm_ref[...], h2.astype(MM))
        h2_ref[...] = h2
        for c in range(4):
            cols = slice(c * FF_CHUNK, (c + 1) * FF_CHUNK)
            uc = _mm_nt(h2, wu_ref[cols, :])
            u_ref[:, cols] = uc
            ext_ref[pl.ds(SHIFT_ROWS, tm), cols] = uc
            _fill_back_rows(ext_ref, carry_ref[:, cols], tm, cols)
            carry_ref[:, cols] = uc[tm - SHIFT_ROWS:tm, :]
        down = jnp.zeros((tm, D_MODEL), F32)
        for c in range(2):
            gate, val = _conv_glu(ext_ref, cw_ref, cb_ref, tm, c)
            act = gate * jax.nn.sigmoid(gate) * val
            down = down + _mm(act, wd_ref[c * FF_CHUNK:(c + 1) * FF_CHUNK, :])
        err = x1 + _to_time_order(pmt_ref[...], down) - tgt_ref[...]
        loss_ref[...] += jnp.full(loss_ref.shape, 0.5 * inv_d * jnp.sum(err * err), F32)
        dy = err * inv_d
        dy_ref[...] = dy
        dyb_ref[...] = _to_sublane_major(pm_ref[...], dy.astype(MM))

    sd = jax.ShapeDtypeStruct
    return pl.pallas_call(
        body, name="ffn_fwd", grid=(s // tm,),
        out_shape=(sd((s, D_MODEL), MM), sd((s, 2 * D_FF), F32), sd((s, D_MODEL), F32), sd((s, D_MODEL), MM),
                   sd((8, LANES), F32)),
        in_specs=[_rows(tm, 1024), _resident((1, 1024)), _resident((2 * D_FF, D_MODEL)), _resident((3, 2 * D_FF)),
                  _resident((1, 2 * D_FF)), _resident((D_FF, D_MODEL)), _rows(tm, 1024), _resident((tm, tm)),
                  _resident((tm, tm))],
        out_specs=(_rows(tm, 1024), _rows(tm, 2 * D_FF), _rows(tm, 1024), _rows(tm, 1024),
                   pl.BlockSpec((8, LANES), lambda i: (0, 0))),
        scratch_shapes=[pltpu.VMEM((SHIFT_ROWS + tm, 2 * D_FF), F32), pltpu.VMEM((SHIFT_ROWS, 2 * D_FF), F32)],
        compiler_params=_params("arbitrary"),
    )(x1, ffn_norm, wup_t, conv_w, conv_b, wdown, target, pm, pmt)


def _ffn_bwd_a(dyb, u, conv_w, conv_b, wdown, tm):
    s = dyb.shape[0]
    nt = s // tm
    hb = tm // SHIFT_ROWS
    rev = lambda i: (nt - 1 - i, 0)

    def body(dy_ref, u_ref, before_ref, cw_ref, cb_ref, wd_ref, du_ref, act_ref, dcw_ref, dcb_ref, ext_ref, extd_ref,
             ahead_ref):
        i = pl.program_id(0)
        first_tile = i == nt - 1

        @pl.when(i == 0)
        def _():
            ahead_ref[...] = jnp.zeros_like(ahead_ref)
            dcw_ref[...] = jnp.zeros_like(dcw_ref)
            dcb_ref[...] = jnp.zeros_like(dcb_ref)

        ext_ref[pl.ds(SHIFT_ROWS, tm), :] = u_ref[...]
        for c in range(4):
            cols = slice(c * FF_CHUNK, (c + 1) * FF_CHUNK)
            _fill_back_rows(ext_ref, jnp.where(first_tile, 0.0, before_ref[:, cols]), tm, cols)
        dy = dy_ref[...]
        for c in range(2):
            gate, val = _conv_glu(ext_ref, cw_ref, cb_ref, tm, c)
            sg = jax.nn.sigmoid(gate)
            sl = gate * sg
            act_ref[:, c * FF_CHUNK:(c + 1) * FF_CHUNK] = (sl * val).astype(MM)
            d_act = _mm_nt(dy, wd_ref[c * FF_CHUNK:(c + 1) * FF_CHUNK, :])
            extd_ref[pl.ds(0, tm), c * FF_CHUNK:(c + 1) * FF_CHUNK] = d_act * val * (sg * (1.0 + gate * (1.0 - sg)))
            extd_ref[pl.ds(0, tm), D_FF + c * FF_CHUNK:D_FF + (c + 1) * FF_CHUNK] = d_act * sl
        for c in range(4):
            cols = slice(c * FF_CHUNK, (c + 1) * FF_CHUNK)
            ahead = ahead_ref[:, cols]
            first2 = extd_ref[pl.ds(0, SHIFT_ROWS), cols]
            extd_ref[pl.ds(tm, 8), cols] = _step_ahead(first2[0:8, :], ahead[0:8, :])
            extd_ref[pl.ds(tm + 8, 8), cols] = _step_ahead(first2[8:16, :], ahead[8:16, :])
            ahead_ref[:, cols] = first2
            d0 = extd_ref[pl.ds(0, tm), cols]
            dcb_ref[:, cols] += jnp.sum(d0, axis=0, keepdims=True)
            for j in range(3):
                dcw_ref[j:j + 1, cols] += jnp.sum(d0 * ext_ref[pl.ds(8 * j, tm), cols], axis=0, keepdims=True)
            du = cw_ref[2:3, cols] * d0 + cw_ref[1:2, cols] * extd_ref[pl.ds(8, tm), cols]
            du = du + cw_ref[0:1, cols] * extd_ref[pl.ds(SHIFT_ROWS, tm), cols]
            du_ref[:, cols] = du.astype(MM)

    sd = jax.ShapeDtypeStruct
    return pl.pallas_call(
        body, name="ffn_bwd_a", grid=(nt,),
        out_shape=(sd((s, 2 * D_FF), MM), sd((s, D_FF), MM), sd((3, 2 * D_FF), F32), sd((1, 2 * D_FF), F32)),
        in_specs=[pl.BlockSpec((tm, D_MODEL), rev), pl.BlockSpec((tm, 2 * D_FF), rev),
                  pl.BlockSpec((SHIFT_ROWS, 2 * D_FF), lambda i: (jnp.maximum((nt - 1 - i) * hb - 1, 0), 0)),
                  _resident((3, 2 * D_FF)), _resident((1, 2 * D_FF)), _resident((D_FF, D_MODEL))],
        out_specs=(pl.BlockSpec((tm, 2 * D_FF), rev), pl.BlockSpec((tm, D_FF), rev),
                   pl.BlockSpec((3, 2 * D_FF), lambda i: (0, 0)), pl.BlockSpec((1, 2 * D_FF), lambda i: (0, 0))),
        scratch_shapes=[pltpu.VMEM((SHIFT_ROWS + tm, 2 * D_FF), F32), pltpu.VMEM((tm + SHIFT_ROWS, 2 * D_FF), F32),
                        pltpu.VMEM((SHIFT_ROWS, 2 * D_FF), F32)],
        compiler_params=_params("arbitrary"),
    )(dyb, u, u, conv_w, conv_b, wdown)


def _after(after):
    tie = [] if after is None else list(after) if isinstance(after, (list, tuple)) else [after]
    return tie, [pl.BlockSpec(memory_space=pl.ANY)] * len(tie)


def _matmul_tn(a, b, tmo, tk, name, after=None, into=None, row0=0):
    s, m = a.shape
    n = b.shape[1]
    nk = s // tk
    tie, tie_spec = _after(after)
    rows = m if into is None else into if isinstance(into, int) else into.shape[0]
    assert row0 % LANES == 0 and tmo % LANES == 0
    grown, grown_spec = ([], []) if into is None or isinstance(into, int) else ([into], [ANY])

    def body(a_ref, b_ref, *rest):
        o_ref = rest[-1]
        k = pl.program_id(1)

        @pl.when(k == 0)
        def _():
            o_ref[...] = jnp.zeros_like(o_ref)

        o_ref[...] += _mm_tn(a_ref[...], b_ref[pl.ds(pl.multiple_of(k * tk, tk), tk), :])

    return pl.pallas_call(
        body, name=name, grid=(m // tmo, nk), out_shape=jax.ShapeDtypeStruct((rows, n), F32),
        in_specs=[pl.BlockSpec((tk, tmo), lambda i, k: (k, i)), _resident((s, n))] + tie_spec + grown_spec,
        out_specs=pl.BlockSpec((pl.Element(tmo), pl.Element(n)), lambda i, k: (pl.multiple_of(row0 + i * tmo, LANES), 0)),
        input_output_aliases={2 + len(tie): 0} if grown else {},
        compiler_params=_params("parallel", "arbitrary"),
    )(a, b, *tie, *grown)


def _ffn_bwd_b(du, wup_t, x1, ffn_norm, dy, tm):
    s = du.shape[0]

    def body(du_ref, wu_ref, x1_ref, gn_ref, dy_ref, pmt_ref, dx1_ref, dx1b_ref, dg_ref):
        @pl.when(pl.program_id(0) == 0)
        def _():
            dg_ref[...] = jnp.zeros_like(dg_ref)

        dh2 = _to_time_order(pmt_ref[...], _mm(du_ref[...], wu_ref[...]))
        x1 = x1_ref[...]
        _, r = _rmsnorm_fwd(x1, gn_ref[...])
        dx, dgr = _rmsnorm_bwd(x1, r, gn_ref[...], dh2)
        dg_ref[...] += jnp.sum(dgr, axis=0, keepdims=True)
        dx1 = dy_ref[...] + dx
        dx1_ref[...] = dx1
        dx1b_ref[...] = dx1.astype(MM)

    return pl.pallas_call(
        body, name="ffn_bwd_b", grid=(s // tm,),
        out_shape=(jax.ShapeDtypeStruct((s, D_MODEL), F32), jax.ShapeDtypeStruct((s, D_MODEL), MM),
                   jax.ShapeDtypeStruct((1, D_MODEL), F32)),
        in_specs=[_rows(tm, 2 * D_FF), _resident((2 * D_FF, D_MODEL)), _rows(tm, 1024), _resident((1, 1024)),
                  _rows(tm, 1024), _resident((tm, tm))],
        out_specs=(_rows(tm, 1024), _rows(tm, 1024), pl.BlockSpec((1, D_MODEL), lambda i: (0, 0))),
        compiler_params=_params("arbitrary"),
    )(du, wup_t, x1, ffn_norm, dy, _sublane_major_matrices(tm)[1])


def _mix_bwd(dx1b, wout, g, a, b, tm):
    s = dx1b.shape[0]

    def body(dx_ref, w_ref, g_ref, a_ref, b_ref, da_ref, db_ref, dzg_ref, dbg_ref):
        @pl.when(pl.program_id(0) == 0)
        def _():
            dbg_ref[...] = jnp.zeros_like(dbg_ref)

        dmix = _mm_nt(dx_ref[...], w_ref[...])
        for half, src, dst in ((0, a_ref, da_ref), (1, b_ref, db_ref)):
            cols = slice(half * 1024, (half + 1) * 1024)
            gt = g_ref[:, cols]
            dst[...] = (dmix * gt).astype(dst.dtype)
            dz = dmix * src[...] * gt * (1.0 - gt)
            dzg_ref[:, cols] = dz.astype(MM)
            dbg_ref[:, cols] += jnp.sum(dz, axis=0, keepdims=True)

    sd = jax.ShapeDtypeStruct
    return pl.pallas_call(
        body, name="mix_bwd", grid=(s // tm,),
        out_shape=(sd((s, 1024), F32), sd((s, 1024), MM), sd((s, 2048), MM), sd((1, 2048), F32)),
        in_specs=[_rows(tm, 1024), _resident((1024, 1024)), _rows(tm, 2048), _rows(tm, 1024), _rows(tm, 1024)],
        out_specs=(_rows(tm, 1024), _rows(tm, 1024), _rows(tm, 2048), pl.BlockSpec((1, 2048), lambda i: (0, 0))),
        compiler_params=_params("arbitrary"),
    )(dx1b, wout, g, a, b)


def _pool_bwd(u, da, wpool, pool_scale, tm, after=None):
    s = u.shape[0]
    nt = s // tm
    hb = tm // POOL_HALO

    tie, tie_spec = _after(after)

    def body(u_ref, uh_ref, da_ref, dah_ref, wp_ref, ps_ref, *rest):
        dzu_ref, dwp_ref, dps_ref, ext_ref, exte_ref = rest[-5:]
        i = pl.program_id(0)

        @pl.when(i == 0)
        def _():
            dwp_ref[...] = jnp.zeros_like(dwp_ref)
            dps_ref[...] = jnp.zeros_like(dps_ref)

        ext_ref[pl.ds(0, POOL_HALO), :] = jnp.where(i > 0, uh_ref[...], 0.0)
        ext_ref[pl.ds(POOL_HALO, tm), :] = u_ref[...]
        pooled = _pooled(ext_ref, tm, i * tm)
        da = da_ref[...]
        dah = jnp.where(i < nt - 1, dah_ref[...], 0.0)
        t = (i * tm + lax.broadcasted_iota(jnp.int32, (tm + POOL_HALO, 1), 0)).astype(F32)
        for gi, w in enumerate(POOL_WINDOWS):
            cols = slice(gi * POOL_GROUP, (gi + 1) * POOL_GROUP)
            pg = pooled[gi].astype(MM)
            wg = wp_ref[gi]
            mixed = _mm(pg, wg)
            dps_ref[:, cols] += jnp.sum(da[:, cols] * mixed, axis=0, keepdims=True)
            dmx = (da[:, cols] * ps_ref[:, cols]).astype(MM)
            dwp_ref[gi] += _mm_tn(pg, dmx)
            dpl = _mm_nt(dmx, wg)
            dplh = _mm_nt(dah[:, cols] * ps_ref[:, cols], wg)
            cnt = jnp.minimum(t + 1.0, float(w))
            exte_ref[pl.ds(0, tm), cols] = dpl / cnt[0:tm]
            exte_ref[pl.ds(tm, POOL_HALO), cols] = dplh / cnt[tm:tm + POOL_HALO]
            acc = exte_ref[pl.ds(0, tm), cols]
            for k in range(1, w):
                acc = acc + exte_ref[pl.ds(k, tm), cols]
            dzu_ref[:, cols] = (acc - dpl).astype(MM)

    sd = jax.ShapeDtypeStruct
    last_halo = s // POOL_HALO - 1
    return pl.pallas_call(
        body, name="pool_bwd", grid=(nt,),
        out_shape=(sd((s, 1024), MM), sd((4, POOL_GROUP, POOL_GROUP), F32), sd((1, 1024), F32)),
        in_specs=[_rows(tm, 1024), pl.BlockSpec((POOL_HALO, 1024), lambda i: (jnp.maximum(i * hb - 1, 0), 0)),
                  _rows(tm, 1024),
                  pl.BlockSpec((POOL_HALO, 1024), lambda i: (jnp.minimum((i + 1) * hb, last_halo), 0)),
                  _resident((4, POOL_GROUP, POOL_GROUP)), _resident((1, 1024))] + tie_spec,
        out_specs=(_rows(tm, 1024), pl.BlockSpec((4, POOL_GROUP, POOL_GROUP), lambda i: (0, 0, 0)),
                   pl.BlockSpec((1, 1024), lambda i: (0, 0))),
        scratch_shapes=[pltpu.VMEM((POOL_HALO + tm, 1024), F32), pltpu.VMEM((tm + POOL_HALO, 1024), F32)],
        compiler_params=_params("arbitrary"),
    )(u, u, da, da, wpool, pool_scale, *tie)


def _attn_bwd(q, kv, db, tabs, q_norm_t, k_norm_t, sink_rows, bd):
    s = q.shape[0]
    nb = s // BLOCK
    scale = HEAD_DIM ** -0.5
    cur = lambda n: (jnp.minimum(n, nb - 1), 0)
    prv = lambda n: (jnp.maximum(n - 1, 0), 0)

    def body(q_ref, kvc_ref, kvp_ref, db_ref, c_ref, s1_ref, s2_ref, cp_ref, s1p_ref, s2p_ref, qn_ref, kn_ref,
             sink_ref, bd_ref, dzq_ref, dzkv_ref, dqn_ref, dkn_ref, dsk_ref,
             carry_ref, tot_ref, dqr_ref, qacc_ref, kacc_ref, sacc_ref):
        n = pl.program_id(0)
        bdm = bd_ref[...]
        kn = kn_ref[...]

        @pl.when(n == 0)
        def _():
            carry_ref[...] = jnp.zeros_like(carry_ref)
            qacc_ref[...] = jnp.zeros_like(qacc_ref)
            kacc_ref[...] = jnp.zeros_like(kacc_ref)
            sacc_ref[...] = jnp.zeros_like(sacc_ref)

        kp_raw = kvp_ref[:, 0:128]
        kph, rp = _head_norm_fwd(kp_raw, kn, bdm)
        cp, s1p, s2p = cp_ref[...], s1p_ref[...], s2p_ref[...]

        @pl.when(n < nb)
        def _():
            c, s1, s2 = c_ref[...], s1_ref[...], s2_ref[...]
            c8, s18, s28 = _tile_lanes(c, 8), _tile_lanes(s1, 8), _tile_lanes(s2, 8)
            q_raw = q_ref[...]
            qh, rq = _head_norm_fwd(q_raw, qn_ref[...], bdm)
            qr = (_rope(qh, c8, s18, s28) * scale).astype(MM)
            kc, _ = _head_norm_fwd(kvc_ref[:, 0:128], kn, bdm)
            k2 = jnp.concatenate([_rope(kph, cp, s1p, s2p), _rope(kc, c, s1, s2)], axis=0)
            v2 = jnp.concatenate([kvp_ref[:, 128:256], kvc_ref[:, 128:256]], axis=0)
            dob = db_ref[...].astype(MM)
            mask = _fold_masks(n)
            lane = lax.broadcasted_iota(jnp.int32, (1, LANES), 1)
            dsk = jnp.zeros((1, LANES), F32)
            dk2 = jnp.zeros((2 * BLOCK, LANES), F32)
            dv2 = jnp.zeros((2 * BLOCK, LANES), F32)
            for hk in range(N_KV_HEADS):
                qs = _stack_pairs(qr, hk)
                do = _stack_pairs(dob, hk)
                dqt = jnp.zeros((LANES, PAIR_COLS), F32)
                dkb, dvb = [], []
                for par, (kb, vb) in enumerate(zip(_parity_bands(k2, hk), _parity_bands(v2, hk))):
                    probs = _probs_by_pair(_mm_nt(kb, qs), sink_ref[hk, par], mask)
                    dp = _mm_nt(vb, do)
                    prs, dss = [], []
                    for j, (pr, psink) in enumerate(probs):
                        dpj = _fold(dp[:, j * BLOCK:(j + 1) * BLOCK], mask[0])
                        coldot = jnp.sum(pr * dpj, axis=0, keepdims=True)
                        dss.append(_unfold(pr * (dpj - coldot), mask[0]))
                        prs.append(_unfold(pr, mask[0]))
                        h = hk * GQA_GROUP + 2 * j + par
                        dsk = dsk + jnp.where(lane == h, jnp.sum(-psink * coldot), 0.0)
                    ds, pr = jnp.concatenate(dss, axis=1), jnp.concatenate(prs, axis=1)
                    dqt = dqt + _mm(kb.T, ds)
                    dkb.append(_mm(ds, qs))
                    dvb.append(_mm(pr, do))
                for j in range(PAIRS):
                    col = (PAIRS * hk + j) * LANES
                    dqr_ref[:, col:col + LANES] = dqt[:, j * BLOCK:(j + 1) * BLOCK].T
                dk2 = dk2 + _fold_parity(dkb[0], dkb[1], hk)
                dv2 = dv2 + _fold_parity(dvb[0], dvb[1], hk)
            tot_ref[:, 0:128] = carry_ref[:, 0:128] + dk2[0:BLOCK, :]
            tot_ref[:, 128:256] = carry_ref[:, 128:256] + dv2[0:BLOCK, :]
            carry_ref[:, 0:128] = dk2[BLOCK:2 * BLOCK, :]
            carry_ref[:, 128:256] = dv2[BLOCK:2 * BLOCK, :]
            sacc_ref[...] += dsk
            dqh = _rope_bwd(dqr_ref[...] * scale, c8, s18, s28)
            dq, dgq = _head_norm_bwd(q_raw, rq, qn_ref[...], dqh, bdm)
            dzq_ref[...] = dq.astype(MM)
            qacc_ref[...] += jnp.sum(dgq, axis=0, keepdims=True)

        @pl.when(n == nb)
        def _():
            tot_ref[...] = carry_ref[...]

        dkh = _rope_bwd(tot_ref[:, 0:128], cp, s1p, s2p)
        dkr, dgk = _head_norm_bwd(kp_raw, rp, kn, dkh, bdm)
        dzkv_ref[:, 0:128] = dkr.astype(MM)
        dzkv_ref[:, 128:256] = tot_ref[:, 128:256].astype(MM)
        kacc_ref[...] += jnp.where(n > 0, jnp.sum(dgk, axis=0, keepdims=True), 0.0)

        @pl.when(n == nb)
        def _():
            fold = qacc_ref[:, 0:HEAD_DIM]
            for h in range(1, N_Q_HEADS):
                fold = fold + qacc_ref[:, h * HEAD_DIM:(h + 1) * HEAD_DIM]
            dqn_ref[...] = fold
            dkn_ref[...] = kacc_ref[:, 0:HEAD_DIM] + kacc_ref[:, HEAD_DIM:2 * HEAD_DIM]
            dsk_ref[...] = sacc_ref[...]

    tab = lambda im: pl.BlockSpec((BLOCK, LANES), im)
    sd = jax.ShapeDtypeStruct
    const = lambda n: (0, 0)
    return pl.pallas_call(
        body, name="attn_bwd", grid=(nb + 1,),
        out_shape=(sd((s, 1024), MM), sd((s, 256), MM), sd((1, HEAD_DIM), F32), sd((1, HEAD_DIM), F32),
                   sd((1, LANES), F32)),
        in_specs=[pl.BlockSpec((BLOCK, 1024), cur), pl.BlockSpec((BLOCK, 256), cur), pl.BlockSpec((BLOCK, 256), prv),
                  pl.BlockSpec((BLOCK, 1024), cur), tab(cur), tab(cur), tab(cur), tab(prv), tab(prv), tab(prv),
                  _resident((1, 1024)), _resident((1, 128)), _resident((N_KV_HEADS, 2, 1, PAIR_COLS)),
                  _resident((LANES, LANES))],
        out_specs=(pl.BlockSpec((BLOCK, 1024), cur), pl.BlockSpec((BLOCK, 256), prv),
                   pl.BlockSpec((1, HEAD_DIM), const), pl.BlockSpec((1, HEAD_DIM), const),
                   pl.BlockSpec((1, LANES), const)),
        scratch_shapes=[pltpu.VMEM((BLOCK, 256), F32), pltpu.VMEM((BLOCK, 256), F32), pltpu.VMEM((BLOCK, 1024), F32),
                        pltpu.VMEM((1, 1024), F32), pltpu.VMEM((1, 128), F32), pltpu.VMEM((1, LANES), F32)],
        compiler_params=_params("arbitrary"),
    )(q, kv, kv, db, *tabs, *tabs, q_norm_t, k_norm_t, sink_rows, bd)


def _inproj_bwd(dzu, dzq, dzkv, dzg, win_t, x, attn_norm, dx1, tm, after=None):
    s = x.shape[0]
    tie, tie_spec = _after(after)

    def body(du_ref, dq_ref, dkv_ref, dg_ref, w_ref, x_ref, gn_ref, dx1_ref, *rest):
        gx_ref, dgn_ref = rest[-2:]

        @pl.when(pl.program_id(0) == 0)
        def _():
            dgn_ref[...] = jnp.zeros_like(dgn_ref)

        dh = _mm(du_ref[...], w_ref[O_U:O_Q, :]) + _mm(dq_ref[...], w_ref[O_Q:O_KV, :])
        dh = dh + _mm(dkv_ref[...], w_ref[O_KV:O_G, :]) + _mm(dg_ref[...], w_ref[O_G:IN_WIDTH, :])
        x = x_ref[...]
        _, r = _rmsnorm_fwd(x, gn_ref[...])
        dx, dgr = _rmsnorm_bwd(x, r, gn_ref[...], dh)
        dgn_ref[...] += jnp.sum(dgr, axis=0, keepdims=True)
        gx_ref[...] = dx1_ref[...] + dx

    return pl.pallas_call(
        body, name="inproj_bwd", grid=(s // tm,),
        out_shape=(jax.ShapeDtypeStruct((s, D_MODEL), F32), jax.ShapeDtypeStruct((1, D_MODEL), F32)),
        in_specs=[_rows(tm, 1024), _rows(tm, 1024), _rows(tm, 256), _rows(tm, 2048),
                  _resident((IN_WIDTH, D_MODEL)), _rows(tm, 1024), _resident((1, 1024)), _rows(tm, 1024)] + tie_spec,
        out_specs=(_rows(tm, 1024), pl.BlockSpec((1, D_MODEL), lambda i: (0, 0))),
        compiler_params=_params("arbitrary"),
    )(dzu, dzq, dzkv, dzg, win_t, x, attn_norm, dx1, *tie)


def _attention_constants(q_norm, k_norm, sinks):
    inv_freq = np.float32(ROPE_THETA) ** (-np.arange(0, ROPE_DIM, 2, dtype=np.float32) / np.float32(ROPE_DIM))
    lane = np.arange(LANES) % HEAD_DIM
    invf = jnp.asarray(np.where(lane < ROPE_DIM, inv_freq[lane % (ROPE_DIM // 2)], 0.0).reshape(1, LANES), F32)
    bd = jnp.asarray(np.arange(LANES)[:, None] // HEAD_DIM == np.arange(LANES)[None, :] // HEAD_DIM, MM)
    q_norm_t = jnp.tile(q_norm, (1, N_Q_HEADS))
    k_norm_t = jnp.tile(k_norm, (1, N_KV_HEADS))
    sink_rows = jnp.repeat(sinks.reshape(N_KV_HEADS, PAIRS, 2).transpose(0, 2, 1), BLOCK, axis=2)
    sink_rows = sink_rows.reshape(N_KV_HEADS, 2, 1, PAIR_COLS)
    return invf, bd, q_norm_t, k_norm_t, sink_rows


ANY = pl.BlockSpec(memory_space=pl.ANY)


def _position():
    return lax.axis_index("x"), lax.axis_index("y"), lax.axis_index("c")


def _all_gather(shards):
    k = len(shards)

    def body(*refs):
        ins, outs = refs[:k], refs[k:2 * k]
        send_sems, recv_sems, local_sems = refs[2 * k:]
        x, y, c = _position()
        me, sibling = (x, y, c), (x, y, 1 - c)
        chips = [(1 - x, y), (x, 1 - y), (1 - x, 1 - y)]

        def copy(a, kk, block, to, src=None):
            dst = outs[a].at[4 * block[0] + 2 * block[1] + block[2]]
            return pltpu.make_async_remote_copy(
                src_ref=dst if src is None else src, dst_ref=dst, send_sem=send_sems.at[a * 7 + kk],
                recv_sem=recv_sems.at[a * 7 + kk], device_id=to, device_id_type=MESH)

        mine = [pltpu.make_async_copy(ins[a], outs[a].at[4 * x + 2 * y + c], local_sems.at[a]) for a in range(k)]
        for cp in mine:
            cp.start()
        first = []
        for a in range(k):
            first.append(copy(a, 0, me, sibling, src=ins[a]))
            first += [copy(a, 1 + j, me, (*chip, c), src=ins[a]) for j, chip in enumerate(chips)]
        for cp in first:
            cp.start()
        passed = []
        for j, chip in enumerate(chips):
            for a in range(k):
                copy(a, 1 + j, (*chip, c), me).wait_recv()
                cp = copy(a, 4 + j, (*chip, c), sibling)
                cp.start()
                passed.append(cp)
        for a in range(k):
            copy(a, 0, sibling, me).wait_recv()
            for j, chip in enumerate(chips):
                copy(a, 4 + j, (*chip, 1 - c), me).wait_recv()
        for cp in first + passed:
            cp.wait_send()
        for cp in mine:
            cp.wait()

    return pl.pallas_call(
        body, name="all_gather_weights",
        out_shape=tuple(jax.ShapeDtypeStruct((N_DEV,) + s.shape, s.dtype) for s in shards),
        in_specs=[ANY] * k, out_specs=(ANY,) * k,
        scratch_shapes=[pltpu.SemaphoreType.DMA((7 * k,)), pltpu.SemaphoreType.DMA((7 * k,)),
                        pltpu.SemaphoreType.DMA((k,))],
    )(*shards)


HBM = pl.BlockSpec(memory_space=pltpu.HBM)
SEM = pl.BlockSpec(memory_space=pltpu.SEMAPHORE)
EFFECT = pltpu.SideEffectType.DATAFLOW_SIDE_EFFECTING


def _exchange_start(name, bufs, n_sems, copies, after=None):
    k = len(bufs)
    tie, tie_spec = _after(after)
    n_in = k + len(tie)

    def body(*refs):
        for cp in copies(refs[:k], refs[n_in], refs[n_in + 1]):
            cp.start()
        refs[-1][...] = jnp.zeros_like(refs[-1])

    dma = pltpu.SemaphoreType.DMA((n_sems,))
    out = pl.pallas_call(
        body, name=name,
        out_shape=(dma, dma, *[pltpu.HBM(b.shape, b.dtype) for b in bufs], jax.ShapeDtypeStruct((8, LANES), F32)),
        in_specs=[HBM] * k + tie_spec, out_specs=(SEM, SEM, *[HBM] * k, pl.BlockSpec(memory_space=pltpu.VMEM)),
        input_output_aliases={i: 2 + i for i in range(k)},
        compiler_params=pltpu.CompilerParams(has_side_effects=EFFECT),
    )(*[pltpu.with_memory_space_constraint(b, pltpu.HBM) for b in bufs], *tie)
    return out[0], out[1], list(out[2:2 + k]), out[-1]


def _exchange_mid(name, bufs, sems_in, n_sems, waits, copies, after):
    k, ns = len(bufs), len(sems_in)

    def body(*refs):
        ins = refs[:k]
        waits(ins, *refs[k:k + ns])
        for cp in copies(ins, refs[k + ns + 1], refs[k + ns + 2]):
            cp.start()

    dma = pltpu.SemaphoreType.DMA((n_sems,))
    out = pl.pallas_call(
        body, name=name, out_shape=(dma, dma, *[pltpu.HBM(b.shape, b.dtype) for b in bufs]),
        in_specs=[HBM] * k + [SEM] * ns + [ANY], out_specs=(SEM, SEM, *[HBM] * k),
        input_output_aliases={i: 2 + i for i in range(k)},
        compiler_params=pltpu.CompilerParams(has_side_effects=EFFECT),
    )(*bufs, *sems_in, after)
    return out[0], out[1], list(out[2:])


def _exchange_wait(name, bufs, sems, waits, after=None):
    k, ns = len(bufs), len(sems)
    tie, tie_spec = _after(after)

    def body(*refs):
        waits(refs[:k], *refs[k:k + ns])

    out = pl.pallas_call(
        body, name=name, out_shape=tuple(pltpu.HBM(b.shape, b.dtype) for b in bufs),
        in_specs=[HBM] * k + [SEM] * ns + tie_spec, out_specs=(HBM,) * k,
        input_output_aliases={i: i for i in range(k)},
        compiler_params=pltpu.CompilerParams(has_side_effects=EFFECT),
    )(*bufs, *sems, *tie)
    return list(out)


def _gather_copies(k, direct):
    def copies(refs, send_sems, recv_sems):
        x, y, c = _position()
        chips = [(1 - x, y), (x, 1 - y), (1 - x, 1 - y)]
        out = []
        for a in range(k):
            land = refs[k + a]
            if direct:
                mine = land.at[4 * x + 2 * y + c]
                for kk, to in enumerate([(x, y, 1 - c)] + [(*chip, c) for chip in chips]):
                    out.append(pltpu.make_async_remote_copy(
                        src_ref=refs[a], dst_ref=mine, send_sem=send_sems.at[4 * a + kk],
                        recv_sem=recv_sems.at[4 * a + kk], device_id=to, device_id_type=MESH))
            else:
                for j, (px, py) in enumerate(chips):
                    slot = land.at[4 * px + 2 * py + c]
                    out.append(pltpu.make_async_remote_copy(
                        src_ref=slot, dst_ref=slot, send_sem=send_sems.at[3 * a + j], recv_sem=recv_sems.at[3 * a + j],
                        device_id=(x, y, 1 - c), device_id_type=MESH))
        return out
    return copies


def _all_gather_behind(shards, start_after, mid_after):
    k = len(shards)
    me = 4 * lax.axis_index("x") + 2 * lax.axis_index("y") + lax.axis_index("c")
    lands = [lax.dynamic_update_slice(lax.empty((N_DEV,) + s.shape, s.dtype), s[None], (me, 0, 0)) for s in shards]
    direct, passed = _gather_copies(k, True), _gather_copies(k, False)

    send_a, recv_a, bufs, token = _exchange_start("gather_start", list(shards) + lands, 4 * k, direct, start_after)

    def finish():
        def wait_ici(refs, send_sems, recv_sems):
            for i, cp in enumerate(direct(refs, send_sems, recv_sems)):
                if i % 4:
                    cp.wait_recv()

        send_b, recv_b, bufs2 = _exchange_mid("gather_pass", bufs, [send_a, recv_a], 3 * k, wait_ici, passed,
                                              mid_after())

        def wait_all(refs, sa, ra, sb, rb):
            for i, cp in enumerate(direct(refs, sa, ra)):
                cp.wait_send()
                if i % 4 == 0:
                    cp.wait_recv()
            for cp in passed(refs, sb, rb):
                cp.wait()

        return _exchange_wait("gather_wait", bufs2, [send_a, recv_a, send_b, recv_b], wait_all)[k:]

    return token, finish


def _pair_copies(k):
    def copies(refs, send_sems, recv_sems):
        x, y, c = _position()
        return [pltpu.make_async_remote_copy(
            src_ref=refs[a].at[2 * ch + 1 - c], dst_ref=refs[k + a].at[ch], send_sem=send_sems.at[4 * a + ch],
            recv_sem=recv_sems.at[4 * a + ch], device_id=(x, y, 1 - c), device_id_type=MESH)
            for a in range(k) for ch in range(4)]
    return copies


def _chip_copies(k):
    def copies(refs, send_sems, recv_sems):
        x, y, c = _position()
        return [pltpu.make_async_remote_copy(
            src_ref=refs[a].at[2 * px + py], dst_ref=refs[k + a].at[rel], send_sem=send_sems.at[3 * a + rel],
            recv_sem=recv_sems.at[3 * a + rel], device_id=(px, py, c), device_id_type=MESH)
            for a in range(k) for rel, (px, py) in enumerate([(1 - x, y), (x, 1 - y), (1 - x, 1 - y)])]
    return copies


def _symmetric_exchange(name, srcs, n_land, copies_of):
    k = len(srcs)
    lands = [lax.empty((n_land,) + s.shape[1:], s.dtype) for s in srcs]
    copies = copies_of(k)
    send_sems, recv_sems, bufs, token = _exchange_start(name + "_start", list(srcs) + lands, n_land * k, copies)

    def finish(after):
        def wait_all(refs, ss, rs):
            for cp in copies(refs, ss, rs):
                cp.wait()

        done = _exchange_wait(name + "_wait", bufs, [send_sems, recv_sems], wait_all, after)
        return done[:k], done[k:]

    return token, finish


def _pair_add(full, recv, wire):
    _, r, c_ = full.shape
    core = lax.axis_index("c").astype(jnp.int32).reshape(1)

    def body(core_ref, f_ref, r_ref, pw_ref, own_ref):
        ch = pl.program_id(0)
        x, y, _ = _position()
        tot = f_ref[0, 0] + r_ref[0]
        pw_ref[0] = tot.astype(pw_ref.dtype)

        @pl.when(ch == 2 * x + y)
        def _():
            own_ref[...] = tot

    return pl.pallas_call(
        body, name="grad_pair_add",
        grid_spec=pltpu.PrefetchScalarGridSpec(
            num_scalar_prefetch=1, grid=(4,),
            in_specs=[pl.BlockSpec((1, 1, r, c_), lambda i, core_ref: (i, core_ref[0], 0, 0)),
                      pl.BlockSpec((1, r, c_), lambda i, core_ref: (i, 0, 0))],
            out_specs=(pl.BlockSpec((1, r, c_), lambda i, core_ref: (i, 0, 0)),
                       pl.BlockSpec((r, c_), lambda i, core_ref: (0, 0)))),
        out_shape=(jax.ShapeDtypeStruct((4, r, c_), wire), jax.ShapeDtypeStruct((r, c_), F32)),
        compiler_params=_params("arbitrary"),
    )(core, full.reshape(4, 2, r, c_), recv)


def _adamw_math(w, g, m, v):
    m = ADAM_B1 * m + (1.0 - ADAM_B1) * g
    v = ADAM_B2 * v + (1.0 - ADAM_B2) * (g * g)
    m_hat = m / (1.0 - ADAM_B1 ** ADAM_STEP)
    v_hat = v / (1.0 - ADAM_B2 ** ADAM_STEP)
    delta = -ADAM_LR * (m_hat / (jnp.sqrt(v_hat) + ADAM_EPS) + ADAM_WD * w)
    return delta, m, v


def _row_tile(r):
    for t in (256, 272, 176, 128):
        if r % t == 0 and r > t:
            return t
    return r


def _adamw(g_own, recv, w, m, v):
    r, c_ = w.shape
    t = _row_tile(r)
    blk = pl.BlockSpec((t, c_), lambda i: (i, 0))

    def body(g_ref, r_ref, w_ref, m_ref, v_ref, go_ref, d_ref, mo_ref, vo_ref):
        g = g_ref[...]
        for i in range(3):
            g = g + r_ref[i].astype(F32)
        go_ref[...] = g
        d_ref[...], mo_ref[...], vo_ref[...] = _adamw_math(w_ref[...], g, m_ref[...], v_ref[...])

    return pl.pallas_call(
        body, name="adamw", grid=(r // t,), out_shape=(jax.ShapeDtypeStruct((r, c_), F32),) * 4,
        in_specs=[blk, pl.BlockSpec((3, t, c_), lambda i: (0, i, 0)), blk, blk, blk], out_specs=(blk,) * 4,
        compiler_params=_params("parallel"),
    )(g_own, recv, w, m, v)


SMALL = ("attn_norm", "b_gate", "pool_scale", "q_norm", "k_norm", "sinks", "ffn_norm", "conv_b")
SMALL_SIZES = (1024, 2048, 1024, 64, 64, 16, 1024, 5632)
SMALL_OFFSETS = tuple(sum(-(-s // LANES) * LANES for s in SMALL_SIZES[:i]) for i in range(len(SMALL_SIZES) + 1))
SMALL_WIDTH = SMALL_OFFSETS[-1] + LANES


def _pack_small(d, loss=None):
    parts = [jnp.pad(d[n].reshape(1, -1), ((0, 0), (0, -s % LANES))) for n, s in zip(SMALL, SMALL_SIZES)]
    last = jnp.zeros((1, LANES), F32) if loss is None else jnp.pad(loss.reshape(1, 1), ((0, 0), (0, LANES - 1)))
    return jnp.concatenate(parts + [last], axis=1)


def _small_allreduce(gp):
    def body(g_ref, sum_ref, slots_ref, send_sems, recv_sems):
        x, y, c = _position()
        me = 4 * x + 2 * y + c
        slots_ref[me] = g_ref[...]
        cps = []
        for rel in range(1, N_DEV):
            fx, fy, fc = (rel >> 2) & 1, (rel >> 1) & 1, rel & 1
            to = (1 - x if fx else x, 1 - y if fy else y, 1 - c if fc else c)
            cps.append(pltpu.make_async_remote_copy(
                src_ref=g_ref, dst_ref=slots_ref.at[me], send_sem=send_sems.at[rel - 1],
                recv_sem=recv_sems.at[rel - 1], device_id=to, device_id_type=MESH))
        for cp in cps:
            cp.start()
        for cp in cps:
            cp.wait()
        g = slots_ref[0]
        for i in range(1, N_DEV):
            g = g + slots_ref[i]
        sum_ref[...] = g

    vm = pl.BlockSpec(memory_space=pltpu.VMEM)
    return pl.pallas_call(
        body, name="small_allreduce", out_shape=jax.ShapeDtypeStruct((1, SMALL_WIDTH), F32),
        in_specs=[vm], out_specs=vm,
        scratch_shapes=[pltpu.VMEM((N_DEV, 1, SMALL_WIDTH), F32), pltpu.SemaphoreType.DMA((N_DEV - 1,)),
                        pltpu.SemaphoreType.DMA((N_DEV - 1,))],
    )(gp)


def _small_adamw(gsum, ws, ms, vs):
    n = len(SMALL)

    def body(g_ref, *rest):
        w_refs, m_refs, v_refs, outs = rest[:n], rest[n:2 * n], rest[2 * n:3 * n], rest[3 * n:]
        for j, size in enumerate(SMALL_SIZES):
            g = g_ref[:, SMALL_OFFSETS[j]:SMALL_OFFSETS[j] + size]
            results = (g,) + _adamw_math(w_refs[j][...], g, m_refs[j][...], v_refs[j][...])
            for kind, val in enumerate(results):
                outs[kind * n + j][...] = val
        outs[4 * n][...] = g_ref[:, SMALL_OFFSETS[-1]:SMALL_WIDTH]

    vm = pl.BlockSpec(memory_space=pltpu.VMEM)
    shapes = tuple(jax.ShapeDtypeStruct((1, s), F32) for s in SMALL_SIZES) * 4
    return pl.pallas_call(
        body, name="small_adamw", out_shape=shapes + (jax.ShapeDtypeStruct((1, LANES), F32),),
        in_specs=[vm] * (1 + 3 * n), out_specs=(vm,) * (4 * n + 1),
    )(gsum, *ws, *ms, *vs)


WEIGHTS = ("attn_norm", "w_in", "b_gate", "w_pool", "pool_scale", "q_norm", "k_norm", "sinks", "w_out", "ffn_norm",
           "w_up", "conv_w", "conv_b", "w_down")


def kernel(x, positions, attn_norm, w_in, b_gate, w_pool, pool_scale, q_norm, k_norm, sinks, w_out, ffn_norm, w_up, conv_w, conv_b, w_down, loss_target, m_attn_norm, m_w_in, m_b_gate, m_w_pool, m_pool_scale, m_q_norm, m_k_norm, m_sinks, m_w_out, m_ffn_norm, m_w_up, m_conv_w, m_conv_b, m_w_down, v_attn_norm, v_w_in, v_b_gate, v_w_pool, v_pool_scale, v_q_norm, v_k_norm, v_sinks, v_w_out, v_ffn_norm, v_w_up, v_conv_w, v_conv_b, v_w_down):
    w = dict(attn_norm=attn_norm, w_in=w_in, b_gate=b_gate, w_pool=w_pool, pool_scale=pool_scale, q_norm=q_norm,
             k_norm=k_norm, sinks=sinks, w_out=w_out, ffn_norm=ffn_norm, w_up=w_up, conv_w=conv_w, conv_b=conv_b,
             w_down=w_down)
    m = dict(attn_norm=m_attn_norm, w_in=m_w_in, b_gate=m_b_gate, w_pool=m_w_pool, pool_scale=m_pool_scale,
             q_norm=m_q_norm, k_norm=m_k_norm, sinks=m_sinks, w_out=m_w_out, ffn_norm=m_ffn_norm, w_up=m_w_up,
             conv_w=m_conv_w, conv_b=m_conv_b, w_down=m_w_down)
    v = dict(attn_norm=v_attn_norm, w_in=v_w_in, b_gate=v_b_gate, w_pool=v_w_pool, pool_scale=v_pool_scale,
             q_norm=v_q_norm, k_norm=v_k_norm, sinks=v_sinks, w_out=v_w_out, ffn_norm=v_ffn_norm, w_up=v_w_up,
             conv_w=v_conv_w, conv_b=v_conv_b, w_down=v_w_down)
    seq = x.shape[1]
    tm = 256
    tw = min(seq, 512)
    tk = min(seq, 1024)
    xs, target, pos_col = x[0], loss_target[0], positions.reshape(seq, 1)
    invf, bd, q_norm_t, k_norm_t, sink_rows = _attention_constants(q_norm, k_norm, sinks)
    out, done = {}, {}
    nat = {"w_in": (D_MODEL, 544), "w_pool": (128, POOL_GROUP), "w_out": (128, D_MODEL), "w_up": (D_MODEL, 704),
           "conv_w": (3, 704), "w_down": (352, D_MODEL)}

    def update(names, owns, recvs):
        for name, own, recv in zip(names, owns, recvs):
            w2, m2, v2 = (t[name].reshape(nat[name]) for t in (w, m, v))
            if name in ("w_in", "w_up"):
                res = _adamw(own, recv, w2.T, m2.T, v2.T)
                done[name] = res[1]
                res = [t.T for t in res]
            else:
                res = _adamw(own, recv, w2, m2, v2)
                done[name] = res[1]
            out[name] = [t.reshape(w[name].shape) for t in res]

    (g_win,) = _all_gather([w_in[0].T.astype(MM)])
    win_t = g_win.reshape(IN_WIDTH, D_MODEL)
    fwd = {}
    token, gather_rest = _all_gather_behind(
        [w_pool[0].astype(MM).reshape(128, POOL_GROUP), w_out[0].astype(MM), w_up[0].T.astype(MM), conv_w[0],
         w_down[0].astype(MM)], g_win, lambda: fwd["b"])

    tabs = _rope_tables(pos_col, invf)
    h, u, q, kv, g = _inproj_fwd(xs, attn_norm + token[0:1, 0:1], win_t, b_gate, tw)
    fwd["b"] = b = _attn_fwd(q, kv, tabs, q_norm_t, k_norm_t, sink_rows, bd)
    g_wpool, g_wout, g_wup, g_convw, g_wdown = gather_rest()
    wpool = g_wpool.reshape(N_DEV, 4, 32, POOL_GROUP).transpose(1, 0, 2, 3).reshape(4, POOL_GROUP, POOL_GROUP)
    wout = g_wout.reshape(D_MODEL, D_MODEL)
    wup_t = g_wup.reshape(2 * D_FF, D_MODEL)
    convw = g_convw.transpose(1, 0, 2).reshape(3, 2 * D_FF)
    wdown = g_wdown.reshape(D_FF, D_MODEL)
    a = _pool_fwd(u, wpool, pool_scale, tw)
    x1, mix = _mix_out_fwd(xs, g, a, b, wout, tw)
    h2, uff, dy, dyb, lossp = _ffn_fwd(x1, ffn_norm, wup_t, convw, conv_b, wdown, target, tm)

    du, act, d_conv_w, d_conv_b = _ffn_bwd_a(dyb, uff, convw, conv_b, wdown, tm)
    d_wdown = _matmul_tn(act, dyb, FF_CHUNK, tk, "dw_down")
    dx1, dx1b, d_ffn_norm = _ffn_bwd_b(du, wup_t, x1, ffn_norm, dy, tm)
    d_wup_t = _matmul_tn(du, h2, FF_CHUNK, tk, "dw_up")
    late = ("w_down", "w_up", "conv_w")
    late_wire = (WIRE, WIRE, F32)
    late_full = [d_wdown.reshape(N_DEV, 352, D_MODEL), d_wup_t.reshape(N_DEV, 704, D_MODEL),
                 d_conv_w.reshape(3, N_DEV, 704).transpose(1, 0, 2)]
    token, late_pair = _symmetric_exchange("late_pair", late_full, 4, _pair_copies)
    d_wout = _matmul_tn(mix, dx1b, 1024, tk, "dw_out", after=token)
    da, db, dzg, d_b_gate = _mix_bwd(dx1b, wout, g, a, b, tw)
    late_pw, late_own = zip(*[_pair_add(f, r, wd) for f, r, wd in zip(*late_pair(dzg), late_wire)])
    token, late_chip = _symmetric_exchange("late_chip", list(late_pw), 3, _chip_copies)
    d_win_t = _matmul_tn(dzg, h, 1024, tk, "dw_in_g", after=token, into=IN_WIDTH, row0=O_G)
    dzu, d_wpool, d_pool_scale = _pool_bwd(u, da, wpool, pool_scale, tw)
    d_win_t = _matmul_tn(dzu, h, 1024, tk, "dw_in_u", into=d_win_t, row0=O_U)
    dzq, dzkv, d_q_norm, d_k_norm, d_sinks = _attn_bwd(q, kv, db, tabs, q_norm_t, k_norm_t, sink_rows, bd)
    d_win_t = _matmul_tn(dzq, h, 1024, tk, "dw_in_q", into=d_win_t, row0=O_Q)
    d_win_t = _matmul_tn(dzkv, h, 256, tk, "dw_in_kv", into=d_win_t, row0=O_KV)
    early = ("w_in", "w_pool", "w_out")
    early_full = [d_win_t.reshape(N_DEV, 544, D_MODEL),
                  d_wpool.reshape(4, N_DEV, 32, POOL_GROUP).transpose(1, 0, 2, 3).reshape(N_DEV, 128, POOL_GROUP),
                  d_wout.reshape(N_DEV, 128, D_MODEL)]
    token, early_pair = _symmetric_exchange("early_pair", early_full, 4, _pair_copies)
    update(late, late_own, late_chip(token)[1])
    early_pw, early_own = zip(*[_pair_add(f, r, WIRE) for f, r in zip(*early_pair([done[n] for n in late]))])
    token, early_chip = _symmetric_exchange("early_chip", list(early_pw), 3, _chip_copies)
    grad_x, d_attn_norm = _inproj_bwd(dzu, dzq, dzkv, dzg, win_t, xs, attn_norm, dx1, tw, after=token)
    gr = dict(attn_norm=d_attn_norm, b_gate=d_b_gate, pool_scale=d_pool_scale, q_norm=d_q_norm, k_norm=d_k_norm,
              sinks=d_sinks[:, 0:N_Q_HEADS], ffn_norm=d_ffn_norm, conv_b=d_conv_b)
    small = _small_adamw(_small_allreduce(_pack_small(gr, lossp[0, 0])), *[[t[n] for n in SMALL] for t in (w, m, v)])
    loss = small[-1][0, 0]
    for j, name in enumerate(SMALL):
        out[name] = [small[kind * len(SMALL) + j] for kind in range(4)]
    update(early, early_own, early_chip(small[0])[1])

    return (loss, grad_x[None], *[out[n][0] for n in WEIGHTS], *[out[n][1] for n in WEIGHTS],
            *[out[n][2] for n in WEIGHTS], *[out[n][3] for n in WEIGHTS])
```

```python
import functools

import numpy as np
import jax
import jax.numpy as jnp
from jax import lax
from jax.experimental import pallas as pl
from jax.experimental.pallas import tpu as pltpu

F32 = jnp.float32
MM = jnp.bfloat16
WIRE = jnp.bfloat16
ACT = jnp.bfloat16

D_MODEL = 1024
D_FF = 2816
HEAD_DIM = 64
N_Q_HEADS = 16
N_KV_HEADS = 2
GQA_GROUP = 8
BLOCK = 128
ROPE_DIM = 16
ROPE_THETA = 500000.0
POOL_WINDOWS = (2, 4, 8, 16)
POOL_GROUP = 256
POOL_HALO = 16
CONV_HALO = 8
EPS = 1e-6
NEG = -1e30
O_U, O_Q, O_KV, O_G, IN_WIDTH = 0, 1024, 2048, 2304, 4352
FF_CHUNK = 1408

ADAM_LR, ADAM_B1, ADAM_B2, ADAM_EPS, ADAM_WD, ADAM_STEP = 0.001, 0.9, 0.999, 1e-08, 0.01, 10

N_DEV = 8
LANES = 128
VMEM_LIMIT_BYTES = 56 * 1024 * 1024
MESH = pl.DeviceIdType.MESH


def _params(*sem):
    return pltpu.CompilerParams(dimension_semantics=sem, vmem_limit_bytes=VMEM_LIMIT_BYTES)


def _resident(shape):
    nd = len(shape)
    return pl.BlockSpec(shape, lambda *_: (0,) * nd, pipeline_mode=pl.Buffered(1))


def _rows(tm, width):
    return pl.BlockSpec((tm, width), lambda i: (i, 0))


def _mm(a, b):
    return jnp.dot(a.astype(MM), b.astype(MM), preferred_element_type=F32)


def _mm_nt(a, b):
    return lax.dot_general(a.astype(MM), b.astype(MM), (((1,), (1,)), ((), ())), preferred_element_type=F32)


def _mm_tn(a, b):
    return lax.dot_general(a.astype(MM), b.astype(MM), (((0,), (0,)), ((), ())), preferred_element_type=F32)


def _rmsnorm_fwd(x, g):
    r = lax.rsqrt(jnp.mean(x * x, axis=-1, keepdims=True) + EPS)
    return x * r * g, r


def _rmsnorm_bwd(x, r, g, dy):
    xn = x * r
    dxn = dy * g
    dx = r * (dxn - xn * jnp.mean(dxn * xn, axis=-1, keepdims=True))
    return dx, dy * xn


def _group_sum64(v, bd):
    hi = v.astype(MM)
    lo = (v - hi.astype(F32)).astype(MM)
    outs = []
    for t in range(v.shape[1] // LANES):
        sl = slice(LANES * t, LANES * (t + 1))
        outs.append(jnp.dot(hi[:, sl], bd, preferred_element_type=F32)
                    + jnp.dot(lo[:, sl], bd, preferred_element_type=F32))
    return outs[0] if len(outs) == 1 else jnp.concatenate(outs, axis=1)


def _head_norm_fwd(x, g, bd):
    r = lax.rsqrt(_group_sum64(x * x, bd) * (1.0 / HEAD_DIM) + EPS)
    return x * r * g, r


def _head_norm_bwd(x, r, g, dy, bd):
    xn = x * r
    dxn = dy * g
    dx = r * (dxn - xn * (_group_sum64(dxn * xn, bd) * (1.0 / HEAD_DIM)))
    return dx, dy * xn


def _rope(x, c, s1, s2):
    w = x.shape[1]
    return x * c + pltpu.roll(x, w - ROPE_DIM // 2, 1) * s1 + pltpu.roll(x, ROPE_DIM // 2, 1) * s2


def _rope_bwd(dy, c, s1, s2):
    w = dy.shape[1]
    return dy * c + pltpu.roll(dy * s1, ROPE_DIM // 2, 1) + pltpu.roll(dy * s2, w - ROPE_DIM // 2, 1)


def _tile_lanes(t, reps):
    return t if reps == 1 else jnp.concatenate([t] * reps, axis=1)


def _rope_tables(pos_col, invf):
    s = pos_col.shape[0]
    tm = min(s, 1024)

    def body(pos_ref, invf_ref, c_ref, s1_ref, s2_ref):
        ang = pos_ref[...].astype(F32) * invf_ref[...]
        lane = lax.broadcasted_iota(jnp.int32, ang.shape, 1) % HEAD_DIM
        sn = jnp.sin(ang)
        c_ref[...] = jnp.cos(ang)
        s1_ref[...] = jnp.where(lane < ROPE_DIM // 2, -sn, 0.0)
        s2_ref[...] = jnp.where((lane >= ROPE_DIM // 2) & (lane < ROPE_DIM), sn, 0.0)

    out = jax.ShapeDtypeStruct((s, LANES), F32)
    return pl.pallas_call(
        body, name="rope_tables", grid=(s // tm,), out_shape=(out, out, out),
        in_specs=[_rows(tm, 1), _resident((1, LANES))],
        out_specs=(_rows(tm, LANES),) * 3, compiler_params=_params("parallel"),
    )(pos_col, invf)


def _inproj_fwd(x, attn_norm, win_t, b_gate, tm):
    s = x.shape[0]

    def body(x_ref, gn_ref, w_ref, bg_ref, h_ref, u_ref, q_ref, kv_ref, g_ref):
        h, _ = _rmsnorm_fwd(x_ref[...], gn_ref[...])
        h = h.astype(MM)
        h_ref[...] = h
        u_ref[...] = _mm_nt(h, w_ref[O_U:O_Q, :])
        q_ref[...] = _mm_nt(h, w_ref[O_Q:O_KV, :])
        kv_ref[...] = _mm_nt(h, w_ref[O_KV:O_G, :])
        g_ref[...] = jax.nn.sigmoid(_mm_nt(h, w_ref[O_G:IN_WIDTH, :]) + bg_ref[...])

    sd = jax.ShapeDtypeStruct
    return pl.pallas_call(
        body, name="inproj_fwd", grid=(s // tm,),
        out_shape=(sd((s, D_MODEL), MM), sd((s, 1024), F32), sd((s, 1024), F32), sd((s, 256), F32),
                   sd((s, 2048), F32)),
        in_specs=[_rows(tm, D_MODEL), _resident((1, D_MODEL)), _resident((IN_WIDTH, D_MODEL)), _resident((1, 2048))],
        out_specs=(_rows(tm, D_MODEL), _rows(tm, 1024), _rows(tm, 1024), _rows(tm, 256), _rows(tm, 2048)),
        compiler_params=_params("parallel"),
    )(x, attn_norm, win_t, b_gate)


def _pooled(ext_ref, tm, row0):
    t = (row0 + lax.broadcasted_iota(jnp.int32, (tm, 1), 0)).astype(F32)
    out = []
    for gi, w in enumerate(POOL_WINDOWS):
        cols = slice(gi * POOL_GROUP, (gi + 1) * POOL_GROUP)
        acc = ext_ref[pl.ds(POOL_HALO, tm), cols]
        for k in range(1, w):
            acc = acc + ext_ref[pl.ds(POOL_HALO - k, tm), cols]
        cnt = jnp.minimum(t + 1.0, float(w))
        out.append(acc / cnt - ext_ref[pl.ds(POOL_HALO, tm), cols])
    return out


def _pool_fwd(u, wpool, pool_scale, tm):
    s = u.shape[0]
    hb = tm // POOL_HALO

    def body(u_ref, halo_ref, wp_ref, ps_ref, a_ref, ext_ref):
        i = pl.program_id(0)
        ext_ref[pl.ds(0, POOL_HALO), :] = jnp.where(i > 0, halo_ref[...], 0.0)
        ext_ref[pl.ds(POOL_HALO, tm), :] = u_ref[...]
        pooled = _pooled(ext_ref, tm, i * tm)
        for gi in range(4):
            cols = slice(gi * POOL_GROUP, (gi + 1) * POOL_GROUP)
            a_ref[:, cols] = (_mm(pooled[gi], wp_ref[gi]) * ps_ref[:, cols]).astype(a_ref.dtype)

    return pl.pallas_call(
        body, name="pool_fwd", grid=(s // tm,), out_shape=jax.ShapeDtypeStruct((s, 1024), ACT),
        in_specs=[_rows(tm, 1024), pl.BlockSpec((POOL_HALO, 1024), lambda i: (jnp.maximum(i * hb - 1, 0), 0)),
                  _resident((4, POOL_GROUP, POOL_GROUP)), _resident((1, 1024))],
        out_specs=_rows(tm, 1024), scratch_shapes=[pltpu.VMEM((POOL_HALO + tm, 1024), F32)],
        compiler_params=_params("parallel"),
    )(u, u, wpool, pool_scale)


PAIRS = GQA_GROUP // 2
PAIR_COLS = PAIRS * BLOCK


def _fold_masks(n):
    r = lax.broadcasted_iota(jnp.int32, (BLOCK, BLOCK), 0)
    i = lax.broadcasted_iota(jnp.int32, (BLOCK, BLOCK), 1)
    prev = r > i
    return prev, jnp.where(prev & (n == 0), NEG, 0.0)


def _fold(band, prev):
    return jnp.where(prev, band[0:BLOCK, :], band[BLOCK:2 * BLOCK, :])


def _unfold(folded, prev):
    top = jnp.where(prev, folded, 0.0)
    return jnp.concatenate([top, folded - top], axis=0).astype(MM)


def _probs_by_pair(sc, sink, masks):
    prev, bias = masks
    out = []
    for j in range(PAIRS):
        cols = slice(j * BLOCK, (j + 1) * BLOCK)
        out.append(_softmax_sink_t(_fold(sc[:, cols], prev) + bias, sink[:, cols]))
    return out


def _stack_pairs(x, hk):
    return jnp.concatenate([x[:, (PAIRS * hk + j) * LANES:(PAIRS * hk + j + 1) * LANES] for j in range(PAIRS)], axis=0)


def _parity_bands(t, hk):
    low = lax.broadcasted_iota(jnp.int32, t.shape, 1) < HEAD_DIM
    own = jnp.where(low if hk == 0 else ~low, t, 0.0)
    other = pltpu.roll(own, HEAD_DIM, 1)
    return (own, other) if hk == 0 else (other, own)


def _fold_parity(even, odd, hk):
    low = lax.broadcasted_iota(jnp.int32, even.shape, 1) < HEAD_DIM
    comb = jnp.where(low, even, odd)
    comb = comb + pltpu.roll(comb, HEAD_DIM, 1)
    return jnp.where(low if hk == 0 else ~low, comb, 0.0)


def _softmax_sink_t(s, sink):
    m = jnp.maximum(jnp.max(s, axis=0, keepdims=True), sink)
    p = jnp.exp(s - m)
    es = jnp.exp(sink - m)
    inv = 1.0 / (jnp.sum(p, axis=0, keepdims=True) + es)
    return p * inv, es * inv


def _attn_fwd(q, kv, tabs, q_norm_t, k_norm_t, sink_rows, bd):
    s = q.shape[0]
    nb = s // BLOCK
    scale = HEAD_DIM ** -0.5
    cur = lambda n: (n, 0)
    prv = lambda n: (jnp.maximum(n - 1, 0), 0)

    def body(q_ref, kvc_ref, kvp_ref, c_ref, s1_ref, s2_ref, cp_ref, s1p_ref, s2p_ref, qn_ref, kn_ref, sink_ref,
             bd_ref, o_ref):
        n = pl.program_id(0)
        bdm = bd_ref[...]
        c, s1, s2 = c_ref[...], s1_ref[...], s2_ref[...]
        qh, _ = _head_norm_fwd(q_ref[...], qn_ref[...], bdm)
        qr = (_rope(qh, _tile_lanes(c, 8), _tile_lanes(s1, 8), _tile_lanes(s2, 8)) * scale).astype(MM)
        kc, _ = _head_norm_fwd(kvc_ref[:, 0:128], kn_ref[...], bdm)
        kp, _ = _head_norm_fwd(kvp_ref[:, 0:128], kn_ref[...], bdm)
        k2 = jnp.concatenate([_rope(kp, cp_ref[...], s1p_ref[...], s2p_ref[...]), _rope(kc, c, s1, s2)], axis=0)
        v2 = jnp.concatenate([kvp_ref[:, 128:256], kvc_ref[:, 128:256]], axis=0)
        mask = _fold_masks(n)
        for hk in range(N_KV_HEADS):
            qs = _stack_pairs(qr, hk)
            ot = jnp.zeros((LANES, PAIR_COLS), F32)
            for par, (kb, vb) in enumerate(zip(_parity_bands(k2, hk), _parity_bands(v2, hk))):
                probs = _probs_by_pair(_mm_nt(kb, qs), sink_ref[hk, par], mask)
                ot = ot + _mm(vb.T, jnp.concatenate([_unfold(pr, mask[0]) for pr, _ in probs], axis=1))
            for j in range(PAIRS):
                col = (PAIRS * hk + j) * LANES
                o_ref[:, col:col + LANES] = ot[:, j * BLOCK:(j + 1) * BLOCK].T.astype(o_ref.dtype)

    tab = lambda im: pl.BlockSpec((BLOCK, LANES), im)
    return pl.pallas_call(
        body, name="attn_fwd", grid=(nb,), out_shape=jax.ShapeDtypeStruct((s, 1024), ACT),
        in_specs=[pl.BlockSpec((BLOCK, 1024), cur), pl.BlockSpec((BLOCK, 256), cur), pl.BlockSpec((BLOCK, 256), prv),
                  tab(cur), tab(cur), tab(cur), tab(prv), tab(prv), tab(prv),
                  _resident((1, 1024)), _resident((1, 128)), _resident((N_KV_HEADS, 2, 1, PAIR_COLS)),
                  _resident((LANES, LANES))],
        out_specs=pl.BlockSpec((BLOCK, 1024), cur), compiler_params=_params("parallel"),
    )(q, kv, kv, *tabs, *tabs, q_norm_t, k_norm_t, sink_rows, bd)


def _mix_out_fwd(x, g, a, b, wout, tm):
    s = x.shape[0]

    def body(x_ref, g_ref, a_ref, b_ref, w_ref, x1_ref, mix_ref):
        mix = (g_ref[:, 0:1024] * a_ref[...] + g_ref[:, 1024:2048] * b_ref[...]).astype(MM)
        mix_ref[...] = mix
        x1_ref[...] = x_ref[...] + _mm(mix, w_ref[...])

    return pl.pallas_call(
        body, name="mix_out_fwd", grid=(s // tm,),
        out_shape=(jax.ShapeDtypeStruct((s, D_MODEL), F32), jax.ShapeDtypeStruct((s, D_MODEL), MM)),
        in_specs=[_rows(tm, 1024), _rows(tm, 2048), _rows(tm, 1024), _rows(tm, 1024), _resident((1024, 1024))],
        out_specs=(_rows(tm, 1024), _rows(tm, 1024)), compiler_params=_params("parallel"),
    )(x, g, a, b, wout)


SHIFT_ROWS = 16


def _sublane_major_matrices(tm):
    r = np.arange(tm)
    pm = r[None, :] == ((tm // 8) * (r % 8) + r // 8)[:, None]
    return jnp.asarray(pm, MM), jnp.asarray(pm.T, MM)


def _to_sublane_major(pm, v):
    return jnp.dot(pm, v, preferred_element_type=F32).astype(MM)


def _to_time_order(pmt, v):
    hi = v.astype(MM)
    r1 = v - hi.astype(F32)
    mid = r1.astype(MM)
    lo = (r1 - mid.astype(F32)).astype(MM)
    dot = functools.partial(jnp.dot, preferred_element_type=F32)
    return dot(pmt, hi) + dot(pmt, mid) + dot(pmt, lo)


def _step_back(vreg_rows, before):
    sub = lax.broadcasted_iota(jnp.int32, vreg_rows.shape, 0)
    return jnp.where(sub == 0, before[7:8, :], pltpu.roll(vreg_rows, 1, 0))


def _step_ahead(vreg_rows, after):
    sub = lax.broadcasted_iota(jnp.int32, vreg_rows.shape, 0)
    return jnp.where(sub == 7, after[0:1, :], pltpu.roll(vreg_rows, 7, 0))


def _fill_back_rows(ext_ref, before, tm, cols):
    last = ext_ref[pl.ds(SHIFT_ROWS + tm - 8, 8), cols]
    pen = ext_ref[pl.ds(SHIFT_ROWS + tm - 16, 8), cols]
    ext_ref[pl.ds(8, 8), cols] = _step_back(last, before[8:16, :])
    ext_ref[pl.ds(0, 8), cols] = _step_back(pen, before[0:8, :])


def _conv_glu(ext_ref, cw_ref, cb_ref, tm, c):
    out = []
    for base in (c * FF_CHUNK, D_FF + c * FF_CHUNK):
        cols = slice(base, base + FF_CHUNK)
        y = cb_ref[:, cols] + cw_ref[0:1, cols] * ext_ref[pl.ds(0, tm), cols]
        y = y + cw_ref[1:2, cols] * ext_ref[pl.ds(8, tm), cols]
        y = y + cw_ref[2:3, cols] * ext_ref[pl.ds(SHIFT_ROWS, tm), cols]
        out.append(y)
    return out


def _ffn_fwd(x1, ffn_norm, wup_t, conv_w, conv_b, wdown, target, tm):
    s = x1.shape[0]
    inv_d = 1.0 / D_MODEL
    pm, pmt = _sublane_major_matrices(tm)

    def body(x1_ref, gn_ref, wu_ref, cw_ref, cb_ref, wd_ref, tgt_ref, pm_ref, pmt_ref, h2_ref, u_ref, dy_ref, dyb_ref,
             loss_ref, ext_ref, carry_ref):
        i = pl.program_id(0)

        @pl.when(i == 0)
        def _():
            carry_ref[...] = jnp.zeros_like(carry_ref)
            loss_ref[...] = jnp.zeros_like(loss_ref)

        x1 = x1_ref[...]
        h2, _ = _rmsnorm_fwd(x1, gn_ref[...])
        h2 = _to_sublane_major(pm_ref[...], h2.astype(MM))
        h2_ref[...] = h2
        for c in range(4):
            cols = slice(c * FF_CHUNK, (c + 1) * FF_CHUNK)
            uc = _mm_nt(h2, wu_ref[cols, :])
            u_ref[:, cols] = uc
            ext_ref[pl.ds(SHIFT_ROWS, tm), cols] = uc
            _fill_back_rows(ext_ref, carry_ref[:, cols], tm, cols)
            carry_ref[:, cols] = uc[tm - SHIFT_ROWS:tm, :]
        down = jnp.zeros((tm, D_MODEL), F32)
        for c in range(2):
            gate, val = _conv_glu(ext_ref, cw_ref, cb_ref, tm, c)
            act = gate * jax.nn.sigmoid(gate) * val
            down = down + _mm(act, wd_ref[c * FF_CHUNK:(c + 1) * FF_CHUNK, :])
        err = x1 + _to_time_order(pmt_ref[...], down) - tgt_ref[...]
        loss_ref[...] += jnp.full(loss_ref.shape, 0.5 * inv_d * jnp.sum(err * err), F32)
        dy = err * inv_d
        dy_ref[...] = dy
        dyb_ref[...] = _to_sublane_major(pm_ref[...], dy.astype(MM))

    sd = jax.ShapeDtypeStruct
    return pl.pallas_call(
        body, name="ffn_fwd", grid=(s // tm,),
        out_shape=(sd((s, D_MODEL), MM), sd((s, 2 * D_FF), F32), sd((s, D_MODEL), F32), sd((s, D_MODEL), MM),
                   sd((8, LANES), F32)),
        in_specs=[_rows(tm, 1024), _resident((1, 1024)), _resident((2 * D_FF, D_MODEL)), _resident((3, 2 * D_FF)),
                  _resident((1, 2 * D_FF)), _resident((D_FF, D_MODEL)), _rows(tm, 1024), _resident((tm, tm)),
                  _resident((tm, tm))],
        out_specs=(_rows(tm, 1024), _rows(tm, 2 * D_FF), _rows(tm, 1024), _rows(tm, 1024),
                   pl.BlockSpec((8, LANES), lambda i: (0, 0))),
        scratch_shapes=[pltpu.VMEM((SHIFT_ROWS + tm, 2 * D_FF), F32), pltpu.VMEM((SHIFT_ROWS, 2 * D_FF), F32)],
        compiler_params=_params("arbitrary"),
    )(x1, ffn_norm, wup_t, conv_w, conv_b, wdown, target, pm, pmt)


def _ffn_bwd_a(dyb, u, conv_w, conv_b, wdown, tm):
    s = dyb.shape[0]
    nt = s // tm
    hb = tm // SHIFT_ROWS
    rev = lambda i: (nt - 1 - i, 0)

    def body(dy_ref, u_ref, before_ref, cw_ref, cb_ref, wd_ref, du_ref, act_ref, dcw_ref, dcb_ref, ext_ref, extd_ref,
             ahead_ref):
        i = pl.program_id(0)
        first_tile = i == nt - 1

        @pl.when(i == 0)
        def _():
            ahead_ref[...] = jnp.zeros_like(ahead_ref)
            dcw_ref[...] = jnp.zeros_like(dcw_ref)
            dcb_ref[...] = jnp.zeros_like(dcb_ref)

        ext_ref[pl.ds(SHIFT_ROWS, tm), :] = u_ref[...]
        for c in range(4):
            cols = slice(c * FF_CHUNK, (c + 1) * FF_CHUNK)
            _fill_back_rows(ext_ref, jnp.where(first_tile, 0.0, before_ref[:, cols]), tm, cols)
        dy = dy_ref[...]
        for c in range(2):
            gate, val = _conv_glu(ext_ref, cw_ref, cb_ref, tm, c)
            sg = jax.nn.sigmoid(gate)
            sl = gate * sg
            act_ref[:, c * FF_CHUNK:(c + 1) * FF_CHUNK] = (sl * val).astype(MM)
            d_act = _mm_nt(dy, wd_ref[c * FF_CHUNK:(c + 1) * FF_CHUNK, :])
            extd_ref[pl.ds(0, tm), c * FF_CHUNK:(c + 1) * FF_CHUNK] = d_act * val * (sg * (1.0 + gate * (1.0 - sg)))
            extd_ref[pl.ds(0, tm), D_FF + c * FF_CHUNK:D_FF + (c + 1) * FF_CHUNK] = d_act * sl
        for c in range(4):
            cols = slice(c * FF_CHUNK, (c + 1) * FF_CHUNK)
            ahead = ahead_ref[:, cols]
            first2 = extd_ref[pl.ds(0, SHIFT_ROWS), cols]
            extd_ref[pl.ds(tm, 8), cols] = _step_ahead(first2[0:8, :], ahead[0:8, :])
            extd_ref[pl.ds(tm + 8, 8), cols] = _step_ahead(first2[8:16, :], ahead[8:16, :])
            ahead_ref[:, cols] = first2
            d0 = extd_ref[pl.ds(0, tm), cols]
            dcb_ref[:, cols] += jnp.sum(d0, axis=0, keepdims=True)
            for j in range(3):
                dcw_ref[j:j + 1, cols] += jnp.sum(d0 * ext_ref[pl.ds(8 * j, tm), cols], axis=0, keepdims=True)
            du = cw_ref[2:3, cols] * d0 + cw_ref[1:2, cols] * extd_ref[pl.ds(8, tm), cols]
            du = du + cw_ref[0:1, cols] * extd_ref[pl.ds(SHIFT_ROWS, tm), cols]
            du_ref[:, cols] = du.astype(MM)

    sd = jax.ShapeDtypeStruct
    return pl.pallas_call(
        body, name="ffn_bwd_a", grid=(nt,),
        out_shape=(sd((s, 2 * D_FF), MM), sd((s, D_FF), MM), sd((3, 2 * D_FF), F32), sd((1, 2 * D_FF), F32)),
        in_specs=[pl.BlockSpec((tm, D_MODEL), rev), pl.BlockSpec((tm, 2 * D_FF), rev),
                  pl.BlockSpec((SHIFT_ROWS, 2 * D_FF), lambda i: (jnp.maximum((nt - 1 - i) * hb - 1, 0), 0)),
                  _resident((3, 2 * D_FF)), _resident((1, 2 * D_FF)), _resident((D_FF, D_MODEL))],
        out_specs=(pl.BlockSpec((tm, 2 * D_FF), rev), pl.BlockSpec((tm, D_FF), rev),
                   pl.BlockSpec((3, 2 * D_FF), lambda i: (0, 0)), pl.BlockSpec((1, 2 * D_FF), lambda i: (0, 0))),
        scratch_shapes=[pltpu.VMEM((SHIFT_ROWS + tm, 2 * D_FF), F32), pltpu.VMEM((tm + SHIFT_ROWS, 2 * D_FF), F32),
                        pltpu.VMEM((SHIFT_ROWS, 2 * D_FF), F32)],
        compiler_params=_params("arbitrary"),
    )(dyb, u, u, conv_w, conv_b, wdown)


def _after(after):
    tie = [] if after is None else list(after) if isinstance(after, (list, tuple)) else [after]
    return tie, [pl.BlockSpec(memory_space=pl.ANY)] * len(tie)


def _matmul_tn(a, b, tmo, tk, name, after=None, into=None, row0=0):
    s, m = a.shape
    n = b.shape[1]
    nk = s // tk
    tie, tie_spec = _after(after)
    rows = m if into is None else into if isinstance(into, int) else into.shape[0]
    assert row0 % LANES == 0 and tmo % LANES == 0
    grown, grown_spec = ([], []) if into is None or isinstance(into, int) else ([into], [ANY])

    def body(a_ref, b_ref, *rest):
        o_ref = rest[-1]
        k = pl.program_id(1)

        @pl.when(k == 0)
        def _():
            o_ref[...] = jnp.zeros_like(o_ref)

        o_ref[...] += _mm_tn(a_ref[...], b_ref[pl.ds(pl.multiple_of(k * tk, tk), tk), :])

    return pl.pallas_call(
        body, name=name, grid=(m // tmo, nk), out_shape=jax.ShapeDtypeStruct((rows, n), F32),
        in_specs=[pl.BlockSpec((tk, tmo), lambda i, k: (k, i)), _resident((s, n))] + tie_spec + grown_spec,
        out_specs=pl.BlockSpec((pl.Element(tmo), pl.Element(n)), lambda i, k: (pl.multiple_of(row0 + i * tmo, LANES), 0)),
        input_output_aliases={2 + len(tie): 0} if grown else {},
        compiler_params=_params("parallel", "arbitrary"),
    )(a, b, *tie, *grown)


def _ffn_bwd_b(du, wup_t, x1, ffn_norm, dy, tm):
    s = du.shape[0]

    def body(du_ref, wu_ref, x1_ref, gn_ref, dy_ref, pmt_ref, dx1_ref, dx1b_ref, dg_ref):
        @pl.when(pl.program_id(0) == 0)
        def _():
            dg_ref[...] = jnp.zeros_like(dg_ref)

        dh2 = _to_time_order(pmt_ref[...], _mm(du_ref[...], wu_ref[...]))
        x1 = x1_ref[...]
        _, r = _rmsnorm_fwd(x1, gn_ref[...])
        dx, dgr = _rmsnorm_bwd(x1, r, gn_ref[...], dh2)
        dg_ref[...] += jnp.sum(dgr, axis=0, keepdims=True)
        dx1 = dy_ref[...] + dx
        dx1_ref[...] = dx1
        dx1b_ref[...] = dx1.astype(MM)

    return pl.pallas_call(
        body, name="ffn_bwd_b", grid=(s // tm,),
        out_shape=(jax.ShapeDtypeStruct((s, D_MODEL), F32), jax.ShapeDtypeStruct((s, D_MODEL), MM),
                   jax.ShapeDtypeStruct((1, D_MODEL), F32)),
        in_specs=[_rows(tm, 2 * D_FF), _resident((2 * D_FF, D_MODEL)), _rows(tm, 1024), _resident((1, 1024)),
                  _rows(tm, 1024), _resident((tm, tm))],
        out_specs=(_rows(tm, 1024), _rows(tm, 1024), pl.BlockSpec((1, D_MODEL), lambda i: (0, 0))),
        compiler_params=_params("arbitrary"),
    )(du, wup_t, x1, ffn_norm, dy, _sublane_major_matrices(tm)[1])


def _win_rows(row0, rows):
    return pl.BlockSpec((pl.Element(rows), pl.Element(D_MODEL)), lambda i: (row0, 0), pipeline_mode=pl.Buffered(1))


def _mix_bwd(dx1b, wout, g, a, b, mix, h, tm):
    s = dx1b.shape[0]

    def body(dx_ref, w_ref, g_ref, a_ref, b_ref, mix_ref, h_ref, da_ref, db_ref, dzg_ref, dbg_ref, dwo_ref, dwin_ref):
        @pl.when(pl.program_id(0) == 0)
        def _():
            dbg_ref[...] = jnp.zeros_like(dbg_ref)
            dwo_ref[...] = jnp.zeros_like(dwo_ref)
            dwin_ref[...] = jnp.zeros_like(dwin_ref)

        dx = dx_ref[...]
        dwo_ref[...] += _mm_tn(mix_ref[...], dx)
        dmix = _mm_nt(dx, w_ref[...])
        for half, src, dst in ((0, a_ref, da_ref), (1, b_ref, db_ref)):
            cols = slice(half * 1024, (half + 1) * 1024)
            gt = g_ref[:, cols]
            dst[...] = (dmix * gt).astype(dst.dtype)
            dz = dmix * src[...] * gt * (1.0 - gt)
            dzb = dz.astype(MM)
            dzg_ref[:, cols] = dzb
            dwin_ref[cols, :] += _mm_tn(dzb, h_ref[...])
            dbg_ref[:, cols] += jnp.sum(dz, axis=0, keepdims=True)

    sd = jax.ShapeDtypeStruct
    return pl.pallas_call(
        body, name="mix_bwd", grid=(s // tm,),
        out_shape=(sd((s, 1024), F32), sd((s, 1024), MM), sd((s, 2048), MM), sd((1, 2048), F32),
                   sd((D_MODEL, D_MODEL), F32), sd((IN_WIDTH, D_MODEL), F32)),
        in_specs=[_rows(tm, 1024), _resident((1024, 1024)), _rows(tm, 2048), _rows(tm, 1024), _rows(tm, 1024),
                  _rows(tm, 1024), _rows(tm, 1024)],
        out_specs=(_rows(tm, 1024), _rows(tm, 1024), _rows(tm, 2048), pl.BlockSpec((1, 2048), lambda i: (0, 0)),
                   _resident((D_MODEL, D_MODEL)), _win_rows(O_G, IN_WIDTH - O_G)),
        compiler_params=_params("arbitrary"),
    )(dx1b, wout, g, a, b, mix, h)


def _pool_bwd(u, da, wpool, pool_scale, h, d_win_t, tm, after=None):
    s = u.shape[0]
    nt = s // tm
    hb = tm // POOL_HALO

    tie, tie_spec = _after(after)

    def body(u_ref, uh_ref, da_ref, dah_ref, wp_ref, ps_ref, h_ref, *rest):
        dzu_ref, dwp_ref, dps_ref, dwin_ref, ext_ref, exte_ref = rest[-6:]
        i = pl.program_id(0)

        @pl.when(i == 0)
        def _():
            dwp_ref[...] = jnp.zeros_like(dwp_ref)
            dps_ref[...] = jnp.zeros_like(dps_ref)
            dwin_ref[...] = jnp.zeros_like(dwin_ref)

        ext_ref[pl.ds(0, POOL_HALO), :] = jnp.where(i > 0, uh_ref[...], 0.0)
        ext_ref[pl.ds(POOL_HALO, tm), :] = u_ref[...]
        pooled = _pooled(ext_ref, tm, i * tm)
        da = da_ref[...]
        dah = jnp.where(i < nt - 1, dah_ref[...], 0.0)
        t = (i * tm + lax.broadcasted_iota(jnp.int32, (tm + POOL_HALO, 1), 0)).astype(F32)
        for gi, w in enumerate(POOL_WINDOWS):
            cols = slice(gi * POOL_GROUP, (gi + 1) * POOL_GROUP)
            pg = pooled[gi].astype(MM)
            wg = wp_ref[gi]
            mixed = _mm(pg, wg)
            dps_ref[:, cols] += jnp.sum(da[:, cols] * mixed, axis=0, keepdims=True)
            dmx = (da[:, cols] * ps_ref[:, cols]).astype(MM)
            dwp_ref[gi] += _mm_tn(pg, dmx)
            dpl = _mm_nt(dmx, wg)
            dplh = _mm_nt(dah[:, cols] * ps_ref[:, cols], wg)
            cnt = jnp.minimum(t + 1.0, float(w))
            exte_ref[pl.ds(0, tm), cols] = dpl / cnt[0:tm]
            exte_ref[pl.ds(tm, POOL_HALO), cols] = dplh / cnt[tm:tm + POOL_HALO]
            acc = exte_ref[pl.ds(0, tm), cols]
            for k in range(1, w):
                acc = acc + exte_ref[pl.ds(k, tm), cols]
            dzu = (acc - dpl).astype(MM)
            dzu_ref[:, cols] = dzu
            dwin_ref[cols, :] += _mm_tn(dzu, h_ref[...])

    sd = jax.ShapeDtypeStruct
    last_halo = s // POOL_HALO - 1
    return pl.pallas_call(
        body, name="pool_bwd", grid=(nt,),
        out_shape=(sd((s, 1024), MM), sd((4, POOL_GROUP, POOL_GROUP), F32), sd((1, 1024), F32),
                   sd((IN_WIDTH, D_MODEL), F32)),
        in_specs=[_rows(tm, 1024), pl.BlockSpec((POOL_HALO, 1024), lambda i: (jnp.maximum(i * hb - 1, 0), 0)),
                  _rows(tm, 1024),
                  pl.BlockSpec((POOL_HALO, 1024), lambda i: (jnp.minimum((i + 1) * hb, last_halo), 0)),
                  _resident((4, POOL_GROUP, POOL_GROUP)), _resident((1, 1024)), _rows(tm, 1024)] + tie_spec + [ANY],
        out_specs=(_rows(tm, 1024), pl.BlockSpec((4, POOL_GROUP, POOL_GROUP), lambda i: (0, 0, 0)),
                   pl.BlockSpec((1, 1024), lambda i: (0, 0)), _win_rows(O_U, O_Q - O_U)),
        input_output_aliases={7 + len(tie): 3},
        scratch_shapes=[pltpu.VMEM((POOL_HALO + tm, 1024), F32), pltpu.VMEM((tm + POOL_HALO, 1024), F32)],
        compiler_params=_params("arbitrary"),
    )(u, u, da, da, wpool, pool_scale, h, *tie, d_win_t)


def _attn_bwd(q, kv, db, tabs, q_norm_t, k_norm_t, sink_rows, bd):
    s = q.shape[0]
    nb = s // BLOCK
    scale = HEAD_DIM ** -0.5
    cur = lambda n: (jnp.minimum(n, nb - 1), 0)
    prv = lambda n: (jnp.maximum(n - 1, 0), 0)

    def body(q_ref, kvc_ref, kvp_ref, db_ref, c_ref, s1_ref, s2_ref, cp_ref, s1p_ref, s2p_ref, qn_ref, kn_ref,
             sink_ref, bd_ref, dzq_ref, dzkv_ref, dqn_ref, dkn_ref, dsk_ref,
             carry_ref, tot_ref, dqr_ref, qacc_ref, kacc_ref, sacc_ref):
        n = pl.program_id(0)
        bdm = bd_ref[...]
        kn = kn_ref[...]

        @pl.when(n == 0)
        def _():
            carry_ref[...] = jnp.zeros_like(carry_ref)
            qacc_ref[...] = jnp.zeros_like(qacc_ref)
            kacc_ref[...] = jnp.zeros_like(kacc_ref)
            sacc_ref[...] = jnp.zeros_like(sacc_ref)

        kp_raw = kvp_ref[:, 0:128]
        kph, rp = _head_norm_fwd(kp_raw, kn, bdm)
        cp, s1p, s2p = cp_ref[...], s1p_ref[...], s2p_ref[...]

        @pl.when(n < nb)
        def _():
            c, s1, s2 = c_ref[...], s1_ref[...], s2_ref[...]
            c8, s18, s28 = _tile_lanes(c, 8), _tile_lanes(s1, 8), _tile_lanes(s2, 8)
            q_raw = q_ref[...]
            qh, rq = _head_norm_fwd(q_raw, qn_ref[...], bdm)
            qr = (_rope(qh, c8, s18, s28) * scale).astype(MM)
            kc, _ = _head_norm_fwd(kvc_ref[:, 0:128], kn, bdm)
            k2 = jnp.concatenate([_rope(kph, cp, s1p, s2p), _rope(kc, c, s1, s2)], axis=0)
            v2 = jnp.concatenate([kvp_ref[:, 128:256], kvc_ref[:, 128:256]], axis=0)
            dob = db_ref[...].astype(MM)
            mask = _fold_masks(n)
            lane = lax.broadcasted_iota(jnp.int32, (1, LANES), 1)
            dsk = jnp.zeros((1, LANES), F32)
            dk2 = jnp.zeros((2 * BLOCK, LANES), F32)
            dv2 = jnp.zeros((2 * BLOCK, LANES), F32)
            for hk in range(N_KV_HEADS):
                qs = _stack_pairs(qr, hk)
                do = _stack_pairs(dob, hk)
                dqt = jnp.zeros((LANES, PAIR_COLS), F32)
                dkb, dvb = [], []
                for par, (kb, vb) in enumerate(zip(_parity_bands(k2, hk), _parity_bands(v2, hk))):
                    probs = _probs_by_pair(_mm_nt(kb, qs), sink_ref[hk, par], mask)
                    dp = _mm_nt(vb, do)
                    prs, dss = [], []
                    for j, (pr, psink) in enumerate(probs):
                        dpj = _fold(dp[:, j * BLOCK:(j + 1) * BLOCK], mask[0])
                        coldot = jnp.sum(pr * dpj, axis=0, keepdims=True)
                        dss.append(_unfold(pr * (dpj - coldot), mask[0]))
                        prs.append(_unfold(pr, mask[0]))
                        h = hk * GQA_GROUP + 2 * j + par
                        dsk = dsk + jnp.where(lane == h, jnp.sum(-psink * coldot), 0.0)
                    ds, pr = jnp.concatenate(dss, axis=1), jnp.concatenate(prs, axis=1)
                    dqt = dqt + _mm(kb.T, ds)
                    dkb.append(_mm(ds, qs))
                    dvb.append(_mm(pr, do))
                for j in range(PAIRS):
                    col = (PAIRS * hk + j) * LANES
                    dqr_ref[:, col:col + LANES] = dqt[:, j * BLOCK:(j + 1) * BLOCK].T
                dk2 = dk2 + _fold_parity(dkb[0], dkb[1], hk)
                dv2 = dv2 + _fold_parity(dvb[0], dvb[1], hk)
            tot_ref[:, 0:128] = carry_ref[:, 0:128] + dk2[0:BLOCK, :]
            tot_ref[:, 128:256] = carry_ref[:, 128:256] + dv2[0:BLOCK, :]
            carry_ref[:, 0:128] = dk2[BLOCK:2 * BLOCK, :]
            carry_ref[:, 128:256] = dv2[BLOCK:2 * BLOCK, :]
            sacc_ref[...] += dsk
            dqh = _rope_bwd(dqr_ref[...] * scale, c8, s18, s28)
            dq, dgq = _head_norm_bwd(q_raw, rq, qn_ref[...], dqh, bdm)
            dzq_ref[...] = dq.astype(MM)
            qacc_ref[...] += jnp.sum(dgq, axis=0, keepdims=True)

        @pl.when(n == nb)
        def _():
            tot_ref[...] = carry_ref[...]

        dkh = _rope_bwd(tot_ref[:, 0:128], cp, s1p, s2p)
        dkr, dgk = _head_norm_bwd(kp_raw, rp, kn, dkh, bdm)
        dzkv_ref[:, 0:128] = dkr.astype(MM)
        dzkv_ref[:, 128:256] = tot_ref[:, 128:256].astype(MM)
        kacc_ref[...] += jnp.where(n > 0, jnp.sum(dgk, axis=0, keepdims=True), 0.0)

        @pl.when(n == nb)
        def _():
            fold = qacc_ref[:, 0:HEAD_DIM]
            for h in range(1, N_Q_HEADS):
                fold = fold + qacc_ref[:, h * HEAD_DIM:(h + 1) * HEAD_DIM]
            dqn_ref[...] = fold
            dkn_ref[...] = kacc_ref[:, 0:HEAD_DIM] + kacc_ref[:, HEAD_DIM:2 * HEAD_DIM]
            dsk_ref[...] = sacc_ref[...]

    tab = lambda im: pl.BlockSpec((BLOCK, LANES), im)
    sd = jax.ShapeDtypeStruct
    const = lambda n: (0, 0)
    return pl.pallas_call(
        body, name="attn_bwd", grid=(nb + 1,),
        out_shape=(sd((s, 1024), MM), sd((s, 256), MM), sd((1, HEAD_DIM), F32), sd((1, HEAD_DIM), F32),
                   sd((1, LANES), F32)),
        in_specs=[pl.BlockSpec((BLOCK, 1024), cur), pl.BlockSpec((BLOCK, 256), cur), pl.BlockSpec((BLOCK, 256), prv),
                  pl.BlockSpec((BLOCK, 1024), cur), tab(cur), tab(cur), tab(cur), tab(prv), tab(prv), tab(prv),
                  _resident((1, 1024)), _resident((1, 128)), _resident((N_KV_HEADS, 2, 1, PAIR_COLS)),
                  _resident((LANES, LANES))],
        out_specs=(pl.BlockSpec((BLOCK, 1024), cur), pl.BlockSpec((BLOCK, 256), prv),
                   pl.BlockSpec((1, HEAD_DIM), const), pl.BlockSpec((1, HEAD_DIM), const),
                   pl.BlockSpec((1, LANES), const)),
        scratch_shapes=[pltpu.VMEM((BLOCK, 256), F32), pltpu.VMEM((BLOCK, 256), F32), pltpu.VMEM((BLOCK, 1024), F32),
                        pltpu.VMEM((1, 1024), F32), pltpu.VMEM((1, 128), F32), pltpu.VMEM((1, LANES), F32)],
        compiler_params=_params("arbitrary"),
    )(q, kv, kv, db, *tabs, *tabs, q_norm_t, k_norm_t, sink_rows, bd)


def _inproj_bwd(dzu, dzq, dzkv, dzg, win_t, x, attn_norm, dx1, tm, after=None):
    s = x.shape[0]
    tie, tie_spec = _after(after)

    def body(du_ref, dq_ref, dkv_ref, dg_ref, w_ref, x_ref, gn_ref, dx1_ref, *rest):
        gx_ref, dgn_ref = rest[-2:]

        @pl.when(pl.program_id(0) == 0)
        def _():
            dgn_ref[...] = jnp.zeros_like(dgn_ref)

        dh = _mm(du_ref[...], w_ref[O_U:O_Q, :]) + _mm(dq_ref[...], w_ref[O_Q:O_KV, :])
        dh = dh + _mm(dkv_ref[...], w_ref[O_KV:O_G, :]) + _mm(dg_ref[...], w_ref[O_G:IN_WIDTH, :])
        x = x_ref[...]
        _, r = _rmsnorm_fwd(x, gn_ref[...])
        dx, dgr = _rmsnorm_bwd(x, r, gn_ref[...], dh)
        dgn_ref[...] += jnp.sum(dgr, axis=0, keepdims=True)
        gx_ref[...] = dx1_ref[...] + dx

    return pl.pallas_call(
        body, name="inproj_bwd", grid=(s // tm,),
        out_shape=(jax.ShapeDtypeStruct((s, D_MODEL), F32), jax.ShapeDtypeStruct((1, D_MODEL), F32)),
        in_specs=[_rows(tm, 1024), _rows(tm, 1024), _rows(tm, 256), _rows(tm, 2048),
                  _resident((IN_WIDTH, D_MODEL)), _rows(tm, 1024), _resident((1, 1024)), _rows(tm, 1024)] + tie_spec,
        out_specs=(_rows(tm, 1024), pl.BlockSpec((1, D_MODEL), lambda i: (0, 0))),
        compiler_params=_params("arbitrary"),
    )(dzu, dzq, dzkv, dzg, win_t, x, attn_norm, dx1, *tie)


def _attention_constants(q_norm, k_norm, sinks):
    inv_freq = np.float32(ROPE_THETA) ** (-np.arange(0, ROPE_DIM, 2, dtype=np.float32) / np.float32(ROPE_DIM))
    lane = np.arange(LANES) % HEAD_DIM
    invf = jnp.asarray(np.where(lane < ROPE_DIM, inv_freq[lane % (ROPE_DIM // 2)], 0.0).reshape(1, LANES), F32)
    bd = jnp.asarray(np.arange(LANES)[:, None] // HEAD_DIM == np.arange(LANES)[None, :] // HEAD_DIM, MM)
    q_norm_t = jnp.tile(q_norm, (1, N_Q_HEADS))
    k_norm_t = jnp.tile(k_norm, (1, N_KV_HEADS))
    sink_rows = jnp.repeat(sinks.reshape(N_KV_HEADS, PAIRS, 2).transpose(0, 2, 1), BLOCK, axis=2)
    sink_rows = sink_rows.reshape(N_KV_HEADS, 2, 1, PAIR_COLS)
    return invf, bd, q_norm_t, k_norm_t, sink_rows


ANY = pl.BlockSpec(memory_space=pl.ANY)


def _position():
    return lax.axis_index("x"), lax.axis_index("y"), lax.axis_index("c")


def _all_gather(shards):
    k = len(shards)

    def body(*refs):
        ins, outs = refs[:k], refs[k:2 * k]
        send_sems, recv_sems, local_sems = refs[2 * k:]
        x, y, c = _position()
        me, sibling = (x, y, c), (x, y, 1 - c)
        chips = [(1 - x, y), (x, 1 - y), (1 - x, 1 - y)]

        def copy(a, kk, block, to, src=None):
            dst = outs[a].at[4 * block[0] + 2 * block[1] + block[2]]
            return pltpu.make_async_remote_copy(
                src_ref=dst if src is None else src, dst_ref=dst, send_sem=send_sems.at[a * 7 + kk],
                recv_sem=recv_sems.at[a * 7 + kk], device_id=to, device_id_type=MESH)

        mine = [pltpu.make_async_copy(ins[a], outs[a].at[4 * x + 2 * y + c], local_sems.at[a]) for a in range(k)]
        for cp in mine:
            cp.start()
        first = []
        for a in range(k):
            first.append(copy(a, 0, me, sibling, src=ins[a]))
            first += [copy(a, 1 + j, me, (*chip, c), src=ins[a]) for j, chip in enumerate(chips)]
        for cp in first:
            cp.start()
        passed = []
        for j, chip in enumerate(chips):
            for a in range(k):
                copy(a, 1 + j, (*chip, c), me).wait_recv()
                cp = copy(a, 4 + j, (*chip, c), sibling)
                cp.start()
                passed.append(cp)
        for a in range(k):
            copy(a, 0, sibling, me).wait_recv()
            for j, chip in enumerate(chips):
                copy(a, 4 + j, (*chip, 1 - c), me).wait_recv()
        for cp in first + passed:
            cp.wait_send()
        for cp in mine:
            cp.wait()

    return pl.pallas_call(
        body, name="all_gather_weights",
        out_shape=tuple(jax.ShapeDtypeStruct((N_DEV,) + s.shape, s.dtype) for s in shards),
        in_specs=[ANY] * k, out_specs=(ANY,) * k,
        scratch_shapes=[pltpu.SemaphoreType.DMA((7 * k,)), pltpu.SemaphoreType.DMA((7 * k,)),
                        pltpu.SemaphoreType.DMA((k,))],
    )(*shards)


HBM = pl.BlockSpec(memory_space=pltpu.HBM)
SEM = pl.BlockSpec(memory_space=pltpu.SEMAPHORE)
EFFECT = pltpu.SideEffectType.DATAFLOW_SIDE_EFFECTING


def _exchange_start(name, bufs, n_sems, copies, after=None):
    k = len(bufs)
    tie, tie_spec = _after(after)
    n_in = k + len(tie)

    def body(*refs):
        for cp in copies(refs[:k], refs[n_in], refs[n_in + 1]):
            cp.start()
        refs[-1][...] = jnp.zeros_like(refs[-1])

    dma = pltpu.SemaphoreType.DMA((n_sems,))
    out = pl.pallas_call(
        body, name=name,
        out_shape=(dma, dma, *[pltpu.HBM(b.shape, b.dtype) for b in bufs], jax.ShapeDtypeStruct((8, LANES), F32)),
        in_specs=[HBM] * k + tie_spec, out_specs=(SEM, SEM, *[HBM] * k, pl.BlockSpec(memory_space=pltpu.VMEM)),
        input_output_aliases={i: 2 + i for i in range(k)},
        compiler_params=pltpu.CompilerParams(has_side_effects=EFFECT),
    )(*[pltpu.with_memory_space_constraint(b, pltpu.HBM) for b in bufs], *tie)
    return out[0], out[1], list(out[2:2 + k]), out[-1]


def _exchange_mid(name, bufs, sems_in, n_sems, waits, copies, after):
    k, ns = len(bufs), len(sems_in)

    def body(*refs):
        ins = refs[:k]
        waits(ins, *refs[k:k + ns])
        for cp in copies(ins, refs[k + ns + 1], refs[k + ns + 2]):
            cp.start()

    dma = pltpu.SemaphoreType.DMA((n_sems,))
    out = pl.pallas_call(
        body, name=name, out_shape=(dma, dma, *[pltpu.HBM(b.shape, b.dtype) for b in bufs]),
        in_specs=[HBM] * k + [SEM] * ns + [ANY], out_specs=(SEM, SEM, *[HBM] * k),
        input_output_aliases={i: 2 + i for i in range(k)},
        compiler_params=pltpu.CompilerParams(has_side_effects=EFFECT),
    )(*bufs, *sems_in, after)
    return out[0], out[1], list(out[2:])


def _exchange_wait(name, bufs, sems, waits, after=None):
    k, ns = len(bufs), len(sems)
    tie, tie_spec = _after(after)

    def body(*refs):
        waits(refs[:k], *refs[k:k + ns])

    out = pl.pallas_call(
        body, name=name, out_shape=tuple(pltpu.HBM(b.shape, b.dtype) for b in bufs),
        in_specs=[HBM] * k + [SEM] * ns + tie_spec, out_specs=(HBM,) * k,
        input_output_aliases={i: i for i in range(k)},
        compiler_params=pltpu.CompilerParams(has_side_effects=EFFECT),
    )(*bufs, *sems, *tie)
    return list(out)


def _gather_copies(k, direct):
    def copies(refs, send_sems, recv_sems):
        x, y, c = _position()
        chips = [(1 - x, y), (x, 1 - y), (1 - x, 1 - y)]
        out = []
        for a in range(k):
            land = refs[k + a]
            if direct:
                mine = land.at[4 * x + 2 * y + c]
                for kk, to in enumerate([(x, y, 1 - c)] + [(*chip, c) for chip in chips]):
                    out.append(pltpu.make_async_remote_copy(
                        src_ref=refs[a], dst_ref=mine, send_sem=send_sems.at[4 * a + kk],
                        recv_sem=recv_sems.at[4 * a + kk], device_id=to, device_id_type=MESH))
            else:
                for j, (px, py) in enumerate(chips):
                    slot = land.at[4 * px + 2 * py + c]
                    out.append(pltpu.make_async_remote_copy(
                        src_ref=slot, dst_ref=slot, send_sem=send_sems.at[3 * a + j], recv_sem=recv_sems.at[3 * a + j],
                        device_id=(x, y, 1 - c), device_id_type=MESH))
        return out
    return copies


def _all_gather_behind(shards, start_after, mid_after):
    k = len(shards)
    me = 4 * lax.axis_index("x") + 2 * lax.axis_index("y") + lax.axis_index("c")
    lands = [lax.dynamic_update_slice(lax.empty((N_DEV,) + s.shape, s.dtype), s[None], (me, 0, 0)) for s in shards]
    direct, passed = _gather_copies(k, True), _gather_copies(k, False)

    send_a, recv_a, bufs, token = _exchange_start("gather_start", list(shards) + lands, 4 * k, direct, start_after)

    def finish():
        def wait_ici(refs, send_sems, recv_sems):
            for i, cp in enumerate(direct(refs, send_sems, recv_sems)):
                if i % 4:
                    cp.wait_recv()

        send_b, recv_b, bufs2 = _exchange_mid("gather_pass", bufs, [send_a, recv_a], 3 * k, wait_ici, passed,
                                              mid_after())

        def wait_all(refs, sa, ra, sb, rb):
            for i, cp in enumerate(direct(refs, sa, ra)):
                cp.wait_send()
                if i % 4 == 0:
                    cp.wait_recv()
            for cp in passed(refs, sb, rb):
                cp.wait()

        return _exchange_wait("gather_wait", bufs2, [send_a, recv_a, send_b, recv_b], wait_all)[k:]

    return token, finish


def _pair_copies(k):
    def copies(refs, send_sems, recv_sems):
        x, y, c = _position()
        return [pltpu.make_async_remote_copy(
            src_ref=refs[a].at[2 * ch + 1 - c], dst_ref=refs[k + a].at[ch], send_sem=send_sems.at[4 * a + ch],
            recv_sem=recv_sems.at[4 * a + ch], device_id=(x, y, 1 - c), device_id_type=MESH)
            for a in range(k) for ch in range(4)]
    return copies


def _chip_copies(k):
    def copies(refs, send_sems, recv_sems):
        x, y, c = _position()
        return [pltpu.make_async_remote_copy(
            src_ref=refs[a].at[2 * px + py], dst_ref=refs[k + a].at[rel], send_sem=send_sems.at[3 * a + rel],
            recv_sem=recv_sems.at[3 * a + rel], device_id=(px, py, c), device_id_type=MESH)
            for a in range(k) for rel, (px, py) in enumerate([(1 - x, y), (x, 1 - y), (1 - x, 1 - y)])]
    return copies


def _symmetric_exchange(name, srcs, n_land, copies_of):
    k = len(srcs)
    lands = [lax.empty((n_land,) + s.shape[1:], s.dtype) for s in srcs]
    copies = copies_of(k)
    send_sems, recv_sems, bufs, token = _exchange_start(name + "_start", list(srcs) + lands, n_land * k, copies)

    def finish(after):
        def wait_all(refs, ss, rs):
            for cp in copies(refs, ss, rs):
                cp.wait()

        done = _exchange_wait(name + "_wait", bufs, [send_sems, recv_sems], wait_all, after)
        return done[:k], done[k:]

    return token, finish


def _pair_add(full, recv, wire):
    _, r, c_ = full.shape
    core = lax.axis_index("c").astype(jnp.int32).reshape(1)

    def body(core_ref, f_ref, r_ref, pw_ref, own_ref):
        ch = pl.program_id(0)
        x, y, _ = _position()
        tot = f_ref[0, 0] + r_ref[0]
        pw_ref[0] = tot.astype(pw_ref.dtype)

        @pl.when(ch == 2 * x + y)
        def _():
            own_ref[...] = tot

    return pl.pallas_call(
        body, name="grad_pair_add",
        grid_spec=pltpu.PrefetchScalarGridSpec(
            num_scalar_prefetch=1, grid=(4,),
            in_specs=[pl.BlockSpec((1, 1, r, c_), lambda i, core_ref: (i, core_ref[0], 0, 0)),
                      pl.BlockSpec((1, r, c_), lambda i, core_ref: (i, 0, 0))],
            out_specs=(pl.BlockSpec((1, r, c_), lambda i, core_ref: (i, 0, 0)),
                       pl.BlockSpec((r, c_), lambda i, core_ref: (0, 0)))),
        out_shape=(jax.ShapeDtypeStruct((4, r, c_), wire), jax.ShapeDtypeStruct((r, c_), F32)),
        compiler_params=_params("arbitrary"),
    )(core, full.reshape(4, 2, r, c_), recv)


def _adamw_math(w, g, m, v):
    m = ADAM_B1 * m + (1.0 - ADAM_B1) * g
    v = ADAM_B2 * v + (1.0 - ADAM_B2) * (g * g)
    m_hat = m / (1.0 - ADAM_B1 ** ADAM_STEP)
    v_hat = v / (1.0 - ADAM_B2 ** ADAM_STEP)
    delta = -ADAM_LR * (m_hat / (jnp.sqrt(v_hat) + ADAM_EPS) + ADAM_WD * w)
    return delta, m, v


def _row_tile(r):
    for t in (256, 272, 176, 128):
        if r % t == 0 and r > t:
            return t
    return r


def _adamw(g_own, recv, w, m, v):
    r, c_ = w.shape
    t = _row_tile(r)
    blk = pl.BlockSpec((t, c_), lambda i: (i, 0))

    def body(g_ref, r_ref, w_ref, m_ref, v_ref, go_ref, d_ref, mo_ref, vo_ref):
        g = g_ref[...]
        for i in range(3):
            g = g + r_ref[i].astype(F32)
        go_ref[...] = g
        d_ref[...], mo_ref[...], vo_ref[...] = _adamw_math(w_ref[...], g, m_ref[...], v_ref[...])

    return pl.pallas_call(
        body, name="adamw", grid=(r // t,), out_shape=(jax.ShapeDtypeStruct((r, c_), F32),) * 4,
        in_specs=[blk, pl.BlockSpec((3, t, c_), lambda i: (0, i, 0)), blk, blk, blk], out_specs=(blk,) * 4,
        compiler_params=_params("parallel"),
    )(g_own, recv, w, m, v)


SMALL = ("attn_norm", "b_gate", "pool_scale", "q_norm", "k_norm", "sinks", "ffn_norm", "conv_b")
SMALL_SIZES = (1024, 2048, 1024, 64, 64, 16, 1024, 5632)
SMALL_OFFSETS = tuple(sum(-(-s // LANES) * LANES for s in SMALL_SIZES[:i]) for i in range(len(SMALL_SIZES) + 1))
SMALL_WIDTH = SMALL_OFFSETS[-1] + LANES


def _pack_small(d, loss=None):
    parts = [jnp.pad(d[n].reshape(1, -1), ((0, 0), (0, -s % LANES))) for n, s in zip(SMALL, SMALL_SIZES)]
    last = jnp.zeros((1, LANES), F32) if loss is None else jnp.pad(loss.reshape(1, 1), ((0, 0), (0, LANES - 1)))
    return jnp.concatenate(parts + [last], axis=1)


def _small_allreduce(gp):
    def body(g_ref, sum_ref, slots_ref, send_sems, recv_sems):
        x, y, c = _position()
        me = 4 * x + 2 * y + c
        slots_ref[me] = g_ref[...]
        cps = []
        for rel in range(1, N_DEV):
            fx, fy, fc = (rel >> 2) & 1, (rel >> 1) & 1, rel & 1
            to = (1 - x if fx else x, 1 - y if fy else y, 1 - c if fc else c)
            cps.append(pltpu.make_async_remote_copy(
                src_ref=g_ref, dst_ref=slots_ref.at[me], send_sem=send_sems.at[rel - 1],
                recv_sem=recv_sems.at[rel - 1], device_id=to, device_id_type=MESH))
        for cp in cps:
            cp.start()
        for cp in cps:
            cp.wait()
        g = slots_ref[0]
        for i in range(1, N_DEV):
            g = g + slots_ref[i]
        sum_ref[...] = g

    vm = pl.BlockSpec(memory_space=pltpu.VMEM)
    return pl.pallas_call(
        body, name="small_allreduce", out_shape=jax.ShapeDtypeStruct((1, SMALL_WIDTH), F32),
        in_specs=[vm], out_specs=vm,
        scratch_shapes=[pltpu.VMEM((N_DEV, 1, SMALL_WIDTH), F32), pltpu.SemaphoreType.DMA((N_DEV - 1,)),
                        pltpu.SemaphoreType.DMA((N_DEV - 1,))],
    )(gp)


def _small_adamw(gsum, ws, ms, vs):
    n = len(SMALL)

    def body(g_ref, *rest):
        w_refs, m_refs, v_refs, outs = rest[:n], rest[n:2 * n], rest[2 * n:3 * n], rest[3 * n:]
        for j, size in enumerate(SMALL_SIZES):
            g = g_ref[:, SMALL_OFFSETS[j]:SMALL_OFFSETS[j] + size]
            results = (g,) + _adamw_math(w_refs[j][...], g, m_refs[j][...], v_refs[j][...])
            for kind, val in enumerate(results):
                outs[kind * n + j][...] = val
        outs[4 * n][...] = g_ref[:, SMALL_OFFSETS[-1]:SMALL_WIDTH]

    vm = pl.BlockSpec(memory_space=pltpu.VMEM)
    shapes = tuple(jax.ShapeDtypeStruct((1, s), F32) for s in SMALL_SIZES) * 4
    return pl.pallas_call(
        body, name="small_adamw", out_shape=shapes + (jax.ShapeDtypeStruct((1, LANES), F32),),
        in_specs=[vm] * (1 + 3 * n), out_specs=(vm,) * (4 * n + 1),
    )(gsum, *ws, *ms, *vs)


WEIGHTS = ("attn_norm", "w_in", "b_gate", "w_pool", "pool_scale", "q_norm", "k_norm", "sinks", "w_out", "ffn_norm",
           "w_up", "conv_w", "conv_b", "w_down")


def kernel(x, positions, attn_norm, w_in, b_gate, w_pool, pool_scale, q_norm, k_norm, sinks, w_out, ffn_norm, w_up, conv_w, conv_b, w_down, loss_target, m_attn_norm, m_w_in, m_b_gate, m_w_pool, m_pool_scale, m_q_norm, m_k_norm, m_sinks, m_w_out, m_ffn_norm, m_w_up, m_conv_w, m_conv_b, m_w_down, v_attn_norm, v_w_in, v_b_gate, v_w_pool, v_pool_scale, v_q_norm, v_k_norm, v_sinks, v_w_out, v_ffn_norm, v_w_up, v_conv_w, v_conv_b, v_w_down):
    w = dict(attn_norm=attn_norm, w_in=w_in, b_gate=b_gate, w_pool=w_pool, pool_scale=pool_scale, q_norm=q_norm,
             k_norm=k_norm, sinks=sinks, w_out=w_out, ffn_norm=ffn_norm, w_up=w_up, conv_w=conv_w, conv_b=conv_b,
             w_down=w_down)
    m = dict(attn_norm=m_attn_norm, w_in=m_w_in, b_gate=m_b_gate, w_pool=m_w_pool, pool_scale=m_pool_scale,
             q_norm=m_q_norm, k_norm=m_k_norm, sinks=m_sinks, w_out=m_w_out, ffn_norm=m_ffn_norm, w_up=m_w_up,
             conv_w=m_conv_w, conv_b=m_conv_b, w_down=m_w_down)
    v = dict(attn_norm=v_attn_norm, w_in=v_w_in, b_gate=v_b_gate, w_pool=v_w_pool, pool_scale=v_pool_scale,
             q_norm=v_q_norm, k_norm=v_k_norm, sinks=v_sinks, w_out=v_w_out, ffn_norm=v_ffn_norm, w_up=v_w_up,
             conv_w=v_conv_w, conv_b=v_conv_b, w_down=v_w_down)
    seq = x.shape[1]
    tm = 256
    tw = min(seq, 512)
    tk = min(seq, 1024)
    xs, target, pos_col = x[0], loss_target[0], positions.reshape(seq, 1)
    invf, bd, q_norm_t, k_norm_t, sink_rows = _attention_constants(q_norm, k_norm, sinks)
    out, done = {}, {}
    nat = {"w_in": (D_MODEL, 544), "w_pool": (128, POOL_GROUP), "w_out": (128, D_MODEL), "w_up": (D_MODEL, 704),
           "conv_w": (3, 704), "w_down": (352, D_MODEL)}

    def update(names, owns, recvs):
        for name, own, recv in zip(names, owns, recvs):
            w2, m2, v2 = (t[name].reshape(nat[name]) for t in (w, m, v))
            if name in ("w_in", "w_up"):
                res = _adamw(own, recv, w2.T, m2.T, v2.T)
                done[name] = res[1]
                res = [t.T for t in res]
            else:
                res = _adamw(own, recv, w2, m2, v2)
                done[name] = res[1]
            out[name] = [t.reshape(w[name].shape) for t in res]

    (g_win,) = _all_gather([w_in[0].T.astype(MM)])
    win_t = g_win.reshape(IN_WIDTH, D_MODEL)
    fwd = {}
    token, gather_rest = _all_gather_behind(
        [w_pool[0].astype(MM).reshape(128, POOL_GROUP), w_out[0].astype(MM), w_up[0].T.astype(MM), conv_w[0],
         w_down[0].astype(MM)], g_win, lambda: fwd["b"])

    tabs = _rope_tables(pos_col, invf)
    h, u, q, kv, g = _inproj_fwd(xs, attn_norm + token[0:1, 0:1], win_t, b_gate, tw)
    fwd["b"] = b = _attn_fwd(q, kv, tabs, q_norm_t, k_norm_t, sink_rows, bd)
    g_wpool, g_wout, g_wup, g_convw, g_wdown = gather_rest()
    wpool = g_wpool.reshape(N_DEV, 4, 32, POOL_GROUP).transpose(1, 0, 2, 3).reshape(4, POOL_GROUP, POOL_GROUP)
    wout = g_wout.reshape(D_MODEL, D_MODEL)
    wup_t = g_wup.reshape(2 * D_FF, D_MODEL)
    convw = g_convw.transpose(1, 0, 2).reshape(3, 2 * D_FF)
    wdown = g_wdown.reshape(D_FF, D_MODEL)
    a = _pool_fwd(u, wpool, pool_scale, tw)
    x1, mix = _mix_out_fwd(xs, g, a, b, wout, tw)
    h2, uff, dy, dyb, lossp = _ffn_fwd(x1, ffn_norm, wup_t, convw, conv_b, wdown, target, tm)

    du, act, d_conv_w, d_conv_b = _ffn_bwd_a(dyb, uff, convw, conv_b, wdown, tm)
    d_wdown = _matmul_tn(act, dyb, FF_CHUNK, tk, "dw_down")
    dx1, dx1b, d_ffn_norm = _ffn_bwd_b(du, wup_t, x1, ffn_norm, dy, tm)
    d_wup_t = _matmul_tn(du, h2, FF_CHUNK, tk, "dw_up")
    late = ("w_down", "w_up", "conv_w")
    late_wire = (WIRE, WIRE, F32)
    late_full = [d_wdown.reshape(N_DEV, 352, D_MODEL), d_wup_t.reshape(N_DEV, 704, D_MODEL),
                 d_conv_w.reshape(3, N_DEV, 704).transpose(1, 0, 2)]
    token, late_pair = _symmetric_exchange("late_pair", late_full, 4, _pair_copies)
    da, db, dzg, d_b_gate, d_wout, d_win_t = _mix_bwd(dx1b, wout, g, a, b, mix, h, tm)
    late_pw, late_own = zip(*[_pair_add(f, r, wd) for f, r, wd in zip(*late_pair(dzg), late_wire)])
    token, late_chip = _symmetric_exchange("late_chip", list(late_pw), 3, _chip_copies)
    dzu, d_wpool, d_pool_scale, d_win_t = _pool_bwd(u, da, wpool, pool_scale, h, d_win_t, tw, after=token)
    dzq, dzkv, d_q_norm, d_k_norm, d_sinks = _attn_bwd(q, kv, db, tabs, q_norm_t, k_norm_t, sink_rows, bd)
    d_win_t = _matmul_tn(dzq, h, 1024, tk, "dw_in_q", into=d_win_t, row0=O_Q)
    d_win_t = _matmul_tn(dzkv, h, 256, tk, "dw_in_kv", into=d_win_t, row0=O_KV)
    early = ("w_in", "w_pool", "w_out")
    early_full = [d_win_t.reshape(N_DEV, 544, D_MODEL),
                  d_wpool.reshape(4, N_DEV, 32, POOL_GROUP).transpose(1, 0, 2, 3).reshape(N_DEV, 128, POOL_GROUP),
                  d_wout.reshape(N_DEV, 128, D_MODEL)]
    token, early_pair = _symmetric_exchange("early_pair", early_full, 4, _pair_copies)
    update(late, late_own, late_chip(token)[1])
    early_pw, early_own = zip(*[_pair_add(f, r, WIRE) for f, r in zip(*early_pair([done[n] for n in late]))])
    token, early_chip = _symmetric_exchange("early_chip", list(early_pw), 3, _chip_copies)
    grad_x, d_attn_norm = _inproj_bwd(dzu, dzq, dzkv, dzg, win_t, xs, attn_norm, dx1, tw, after=token)
    gr = dict(attn_norm=d_attn_norm, b_gate=d_b_gate, pool_scale=d_pool_scale, q_norm=d_q_norm, k_norm=d_k_norm,
              sinks=d_sinks[:, 0:N_Q_HEADS], ffn_norm=d_ffn_norm, conv_b=d_conv_b)
    small = _small_adamw(_small_allreduce(_pack_small(gr, lossp[0, 0])), *[[t[n] for n in SMALL] for t in (w, m, v)])
    loss = small[-1][0, 0]
    for j, name in enumerate(SMALL):
        out[name] = [small[kind * len(SMALL) + j] for kind in range(4)]
    update(early, early_own, early_chip(small[0])[1])

    return (loss, grad_x[None], *[out[n][0] for n in WEIGHTS], *[out[n][1] for n in WEIGHTS],
            *[out[n][2] for n in WEIGHTS], *[out[n][3] for n in WEIGHTS])
```

```python
import functools

import numpy as np
import jax
import jax.numpy as jnp
from jax import lax
from jax.experimental import pallas as pl
from jax.experimental.pallas import tpu as pltpu

F32 = jnp.float32
MM = jnp.bfloat16
WIRE = jnp.bfloat16
ACT = jnp.bfloat16

D_MODEL = 1024
D_FF = 2816
HEAD_DIM = 64
N_Q_HEADS = 16
N_KV_HEADS = 2
GQA_GROUP = 8
BLOCK = 128
ROPE_DIM = 16
ROPE_THETA = 500000.0
POOL_WINDOWS = (2, 4, 8, 16)
POOL_GROUP = 256
POOL_HALO = 16
CONV_HALO = 8
EPS = 1e-6
NEG = -1e30
O_U, O_Q, O_KV, O_G, IN_WIDTH = 0, 1024, 2048, 2304, 4352
FF_CHUNK = 1408

ADAM_LR, ADAM_B1, ADAM_B2, ADAM_EPS, ADAM_WD, ADAM_STEP = 0.001, 0.9, 0.999, 1e-08, 0.01, 10

N_DEV = 8
LANES = 128
VMEM_LIMIT_BYTES = 56 * 1024 * 1024
MESH = pl.DeviceIdType.MESH


def _params(*sem):
    return pltpu.CompilerParams(dimension_semantics=sem, vmem_limit_bytes=VMEM_LIMIT_BYTES)


def _resident(shape):
    nd = len(shape)
    return pl.BlockSpec(shape, lambda *_: (0,) * nd, pipeline_mode=pl.Buffered(1))


def _rows(tm, width):
    return pl.BlockSpec((tm, width), lambda i: (i, 0))


def _mm(a, b):
    return jnp.dot(a.astype(MM), b.astype(MM), preferred_element_type=F32)


def _mm_nt(a, b):
    return lax.dot_general(a.astype(MM), b.astype(MM), (((1,), (1,)), ((), ())), preferred_element_type=F32)


def _mm_tn(a, b):
    return lax.dot_general(a.astype(MM), b.astype(MM), (((0,), (0,)), ((), ())), preferred_element_type=F32)


def _rmsnorm_fwd(x, g):
    r = lax.rsqrt(jnp.mean(x * x, axis=-1, keepdims=True) + EPS)
    return x * r * g, r


def _rmsnorm_bwd(x, r, g, dy):
    xn = x * r
    dxn = dy * g
    dx = r * (dxn - xn * jnp.mean(dxn * xn, axis=-1, keepdims=True))
    return dx, dy * xn


def _group_sum64(v, bd):
    hi = v.astype(MM)
    lo = (v - hi.astype(F32)).astype(MM)
    outs = []
    for t in range(v.shape[1] // LANES):
        sl = slice(LANES * t, LANES * (t + 1))
        outs.append(jnp.dot(hi[:, sl], bd, preferred_element_type=F32)
                    + jnp.dot(lo[:, sl], bd, preferred_element_type=F32))
    return outs[0] if len(outs) == 1 else jnp.concatenate(outs, axis=1)


def _head_norm_fwd(x, g, bd):
    r = lax.rsqrt(_group_sum64(x * x, bd) * (1.0 / HEAD_DIM) + EPS)
    return x * r * g, r


def _head_norm_bwd(x, r, g, dy, bd):
    xn = x * r
    dxn = dy * g
    dx = r * (dxn - xn * (_group_sum64(dxn * xn, bd) * (1.0 / HEAD_DIM)))
    return dx, dy * xn


def _rope(x, c, s1, s2):
    w = x.shape[1]
    return x * c + pltpu.roll(x, w - ROPE_DIM // 2, 1) * s1 + pltpu.roll(x, ROPE_DIM // 2, 1) * s2


def _rope_bwd(dy, c, s1, s2):
    w = dy.shape[1]
    return dy * c + pltpu.roll(dy * s1, ROPE_DIM // 2, 1) + pltpu.roll(dy * s2, w - ROPE_DIM // 2, 1)


def _tile_lanes(t, reps):
    return t if reps == 1 else jnp.concatenate([t] * reps, axis=1)


def _rope_tables(pos_col, invf):
    s = pos_col.shape[0]
    tm = min(s, 1024)

    def body(pos_ref, invf_ref, c_ref, s1_ref, s2_ref):
        ang = pos_ref[...].astype(F32) * invf_ref[...]
        lane = lax.broadcasted_iota(jnp.int32, ang.shape, 1) % HEAD_DIM
        sn = jnp.sin(ang)
        c_ref[...] = jnp.cos(ang)
        s1_ref[...] = jnp.where(lane < ROPE_DIM // 2, -sn, 0.0)
        s2_ref[...] = jnp.where((lane >= ROPE_DIM // 2) & (lane < ROPE_DIM), sn, 0.0)

    out = jax.ShapeDtypeStruct((s, LANES), F32)
    return pl.pallas_call(
        body, name="rope_tables", grid=(s // tm,), out_shape=(out, out, out),
        in_specs=[_rows(tm, 1), _resident((1, LANES))],
        out_specs=(_rows(tm, LANES),) * 3, compiler_params=_params("parallel"),
    )(pos_col, invf)


def _inproj_fwd(x, attn_norm, win_t, b_gate, tm):
    s = x.shape[0]

    def body(x_ref, gn_ref, w_ref, bg_ref, h_ref, u_ref, q_ref, kv_ref, g_ref):
        h, _ = _rmsnorm_fwd(x_ref[...], gn_ref[...])
        h = h.astype(MM)
        h_ref[...] = h
        u_ref[...] = _mm_nt(h, w_ref[O_U:O_Q, :])
        q_ref[...] = _mm_nt(h, w_ref[O_Q:O_KV, :])
        kv_ref[...] = _mm_nt(h, w_ref[O_KV:O_G, :])
        g_ref[...] = jax.nn.sigmoid(_mm_nt(h, w_ref[O_G:IN_WIDTH, :]) + bg_ref[...])

    sd = jax.ShapeDtypeStruct
    return pl.pallas_call(
        body, name="inproj_fwd", grid=(s // tm,),
        out_shape=(sd((s, D_MODEL), MM), sd((s, 1024), F32), sd((s, 1024), F32), sd((s, 256), F32),
                   sd((s, 2048), F32)),
        in_specs=[_rows(tm, D_MODEL), _resident((1, D_MODEL)), _resident((IN_WIDTH, D_MODEL)), _resident((1, 2048))],
        out_specs=(_rows(tm, D_MODEL), _rows(tm, 1024), _rows(tm, 1024), _rows(tm, 256), _rows(tm, 2048)),
        compiler_params=_params("parallel"),
    )(x, attn_norm, win_t, b_gate)


def _pooled(ext_ref, tm, row0):
    t = (row0 + lax.broadcasted_iota(jnp.int32, (tm, 1), 0)).astype(F32)
    out = []
    for gi, w in enumerate(POOL_WINDOWS):
        cols = slice(gi * POOL_GROUP, (gi + 1) * POOL_GROUP)
        acc = ext_ref[pl.ds(POOL_HALO, tm), cols]
        for k in range(1, w):
            acc = acc + ext_ref[pl.ds(POOL_HALO - k, tm), cols]
        cnt = jnp.minimum(t + 1.0, float(w))
        out.append(acc / cnt - ext_ref[pl.ds(POOL_HALO, tm), cols])
    return out


def _pool_fwd(u, wpool, pool_scale, tm):
    s = u.shape[0]
    hb = tm // POOL_HALO

    def body(u_ref, halo_ref, wp_ref, ps_ref, a_ref, ext_ref):
        i = pl.program_id(0)
        ext_ref[pl.ds(0, POOL_HALO), :] = jnp.where(i > 0, halo_ref[...], 0.0)
        ext_ref[pl.ds(POOL_HALO, tm), :] = u_ref[...]
        pooled = _pooled(ext_ref, tm, i * tm)
        for gi in range(4):
            cols = slice(gi * POOL_GROUP, (gi + 1) * POOL_GROUP)
            a_ref[:, cols] = (_mm(pooled[gi], wp_ref[gi]) * ps_ref[:, cols]).astype(a_ref.dtype)

    return pl.pallas_call(
        body, name="pool_fwd", grid=(s // tm,), out_shape=jax.ShapeDtypeStruct((s, 1024), ACT),
        in_specs=[_rows(tm, 1024), pl.BlockSpec((POOL_HALO, 1024), lambda i: (jnp.maximum(i * hb - 1, 0), 0)),
                  _resident((4, POOL_GROUP, POOL_GROUP)), _resident((1, 1024))],
        out_specs=_rows(tm, 1024), scratch_shapes=[pltpu.VMEM((POOL_HALO + tm, 1024), F32)],
        compiler_params=_params("parallel"),
    )(u, u, wpool, pool_scale)


PAIRS = GQA_GROUP // 2
PAIR_COLS = PAIRS * BLOCK


def _fold_masks(n):
    r = lax.broadcasted_iota(jnp.int32, (BLOCK, BLOCK), 0)
    i = lax.broadcasted_iota(jnp.int32, (BLOCK, BLOCK), 1)
    prev = r > i
    return prev, jnp.where(prev & (n == 0), NEG, 0.0)


def _fold(band, prev):
    return jnp.where(prev, band[0:BLOCK, :], band[BLOCK:2 * BLOCK, :])


def _unfold(folded, prev):
    top = jnp.where(prev, folded, 0.0)
    return jnp.concatenate([top, folded - top], axis=0).astype(MM)


def _probs_by_pair(sc, sink, masks):
    prev, bias = masks
    out = []
    for j in range(PAIRS):
        cols = slice(j * BLOCK, (j + 1) * BLOCK)
        out.append(_softmax_sink_t(_fold(sc[:, cols], prev) + bias, sink[:, cols]))
    return out


def _stack_pairs(x, hk):
    return jnp.concatenate([x[:, (PAIRS * hk + j) * LANES:(PAIRS * hk + j + 1) * LANES] for j in range(PAIRS)], axis=0)


def _parity_bands(t, hk):
    low = lax.broadcasted_iota(jnp.int32, t.shape, 1) < HEAD_DIM
    own = jnp.where(low if hk == 0 else ~low, t, 0.0)
    other = pltpu.roll(own, HEAD_DIM, 1)
    return (own, other) if hk == 0 else (other, own)


def _fold_parity(even, odd, hk):
    low = lax.broadcasted_iota(jnp.int32, even.shape, 1) < HEAD_DIM
    comb = jnp.where(low, even, odd)
    comb = comb + pltpu.roll(comb, HEAD_DIM, 1)
    return jnp.where(low if hk == 0 else ~low, comb, 0.0)


def _softmax_sink_t(s, sink):
    m = jnp.maximum(jnp.max(s, axis=0, keepdims=True), sink)
    p = jnp.exp(s - m)
    es = jnp.exp(sink - m)
    inv = 1.0 / (jnp.sum(p, axis=0, keepdims=True) + es)
    return p * inv, es * inv


def _attn_fwd(q, kv, tabs, q_norm_t, k_norm_t, sink_rows, bd):
    s = q.shape[0]
    nb = s // BLOCK
    scale = HEAD_DIM ** -0.5
    cur = lambda n: (n, 0)
    prv = lambda n: (jnp.maximum(n - 1, 0), 0)

    def body(q_ref, kvc_ref, kvp_ref, c_ref, s1_ref, s2_ref, cp_ref, s1p_ref, s2p_ref, qn_ref, kn_ref, sink_ref,
             bd_ref, o_ref):
        n = pl.program_id(0)
        bdm = bd_ref[...]
        c, s1, s2 = c_ref[...], s1_ref[...], s2_ref[...]
        qh, _ = _head_norm_fwd(q_ref[...], qn_ref[...], bdm)
        qr = (_rope(qh, _tile_lanes(c, 8), _tile_lanes(s1, 8), _tile_lanes(s2, 8)) * scale).astype(MM)
        kc, _ = _head_norm_fwd(kvc_ref[:, 0:128], kn_ref[...], bdm)
        kp, _ = _head_norm_fwd(kvp_ref[:, 0:128], kn_ref[...], bdm)
        k2 = jnp.concatenate([_rope(kp, cp_ref[...], s1p_ref[...], s2p_ref[...]), _rope(kc, c, s1, s2)], axis=0)
        v2 = jnp.concatenate([kvp_ref[:, 128:256], kvc_ref[:, 128:256]], axis=0)
        mask = _fold_masks(n)
        for hk in range(N_KV_HEADS):
            qs = _stack_pairs(qr, hk)
            ot = jnp.zeros((LANES, PAIR_COLS), F32)
            for par, (kb, vb) in enumerate(zip(_parity_bands(k2, hk), _parity_bands(v2, hk))):
                probs = _probs_by_pair(_mm_nt(kb, qs), sink_ref[hk, par], mask)
                ot = ot + _mm(vb.T, jnp.concatenate([_unfold(pr, mask[0]) for pr, _ in probs], axis=1))
            for j in range(PAIRS):
                col = (PAIRS * hk + j) * LANES
                o_ref[:, col:col + LANES] = ot[:, j * BLOCK:(j + 1) * BLOCK].T.astype(o_ref.dtype)

    tab = lambda im: pl.BlockSpec((BLOCK, LANES), im)
    return pl.pallas_call(
        body, name="attn_fwd", grid=(nb,), out_shape=jax.ShapeDtypeStruct((s, 1024), ACT),
        in_specs=[pl.BlockSpec((BLOCK, 1024), cur), pl.BlockSpec((BLOCK, 256), cur), pl.BlockSpec((BLOCK, 256), prv),
                  tab(cur), tab(cur), tab(cur), tab(prv), tab(prv), tab(prv),
                  _resident((1, 1024)), _resident((1, 128)), _resident((N_KV_HEADS, 2, 1, PAIR_COLS)),
                  _resident((LANES, LANES))],
        out_specs=pl.BlockSpec((BLOCK, 1024), cur), compiler_params=_params("parallel"),
    )(q, kv, kv, *tabs, *tabs, q_norm_t, k_norm_t, sink_rows, bd)


def _mix_out_fwd(x, g, a, b, wout, tm):
    s = x.shape[0]

    def body(x_ref, g_ref, a_ref, b_ref, w_ref, x1_ref, mix_ref):
        mix = (g_ref[:, 0:1024] * a_ref[...] + g_ref[:, 1024:2048] * b_ref[...]).astype(MM)
        mix_ref[...] = mix
        x1_ref[...] = x_ref[...] + _mm(mix, w_ref[...])

    return pl.pallas_call(
        body, name="mix_out_fwd", grid=(s // tm,),
        out_shape=(jax.ShapeDtypeStruct((s, D_MODEL), F32), jax.ShapeDtypeStruct((s, D_MODEL), MM)),
        in_specs=[_rows(tm, 1024), _rows(tm, 2048), _rows(tm, 1024), _rows(tm, 1024), _resident((1024, 1024))],
        out_specs=(_rows(tm, 1024), _rows(tm, 1024)), compiler_params=_params("parallel"),
    )(x, g, a, b, wout)


SHIFT_ROWS = 16


def _sublane_major_matrices(tm):
    r = np.arange(tm)
    pm = r[None, :] == ((tm // 8) * (r % 8) + r // 8)[:, None]
    return jnp.asarray(pm, MM), jnp.asarray(pm.T, MM)


def _to_sublane_major(pm, v):
    return jnp.dot(pm, v, preferred_element_type=F32).astype(MM)


def _to_time_order(pmt, v):
    hi = v.astype(MM)
    r1 = v - hi.astype(F32)
    mid = r1.astype(MM)
    lo = (r1 - mid.astype(F32)).astype(MM)
    dot = functools.partial(jnp.dot, preferred_element_type=F32)
    return dot(pmt, hi) + dot(pmt, mid) + dot(pmt, lo)


def _step_back(vreg_rows, before):
    sub = lax.broadcasted_iota(jnp.int32, vreg_rows.shape, 0)
    return jnp.where(sub == 0, before[7:8, :], pltpu.roll(vreg_rows, 1, 0))


def _step_ahead(vreg_rows, after):
    sub = lax.broadcasted_iota(jnp.int32, vreg_rows.shape, 0)
    return jnp.where(sub == 7, after[0:1, :], pltpu.roll(vreg_rows, 7, 0))


def _fill_back_rows(ext_ref, before, tm, cols):
    last = ext_ref[pl.ds(SHIFT_ROWS + tm - 8, 8), cols]
    pen = ext_ref[pl.ds(SHIFT_ROWS + tm - 16, 8), cols]
    ext_ref[pl.ds(8, 8), cols] = _step_back(last, before[8:16, :])
    ext_ref[pl.ds(0, 8), cols] = _step_back(pen, before[0:8, :])


def _conv_glu(ext_ref, cw_ref, cb_ref, tm, c):
    out = []
    for base in (c * FF_CHUNK, D_FF + c * FF_CHUNK):
        cols = slice(base, base + FF_CHUNK)
        y = cb_ref[:, cols] + cw_ref[0:1, cols] * ext_ref[pl.ds(0, tm), cols]
        y = y + cw_ref[1:2, cols] * ext_ref[pl.ds(8, tm), cols]
        y = y + cw_ref[2:3, cols] * ext_ref[pl.ds(SHIFT_ROWS, tm), cols]
        out.append(y)
    return out


def _ffn_fwd(x1, ffn_norm, wup_t, conv_w, conv_b, wdown, target, tm):
    s = x1.shape[0]
    inv_d = 1.0 / D_MODEL
    pm, pmt = _sublane_major_matrices(tm)

    def body(x1_ref, gn_ref, wu_ref, cw_ref, cb_ref, wd_ref, tgt_ref, pm_ref, pmt_ref, h2_ref, u_ref, dy_ref, dyb_ref,
             loss_ref, ext_ref, carry_ref):
        i = pl.program_id(0)

        @pl.when(i == 0)
        def _():
            carry_ref[...] = jnp.zeros_like(carry_ref)
            loss_ref[...] = jnp.zeros_like(loss_ref)

        x1 = x1_ref[...]
        h2, _ = _rmsnorm_fwd(x1, gn_ref[...])
        h2 = _to_sublane_major(pm_ref[...], h2.astype(MM))
        h2_ref[...] = h2
        for c in range(4):
            cols = slice(c * FF_CHUNK, (c + 1) * FF_CHUNK)
            uc = _mm_nt(h2, wu_ref[cols, :])
            u_ref[:, cols] = uc
            ext_ref[pl.ds(SHIFT_ROWS, tm), cols] = uc
            _fill_back_rows(ext_ref, carry_ref[:, cols], tm, cols)
            carry_ref[:, cols] = uc[tm - SHIFT_ROWS:tm, :]
        down = jnp.zeros((tm, D_MODEL), F32)
        for c in range(2):
            gate, val = _conv_glu(ext_ref, cw_ref, cb_ref, tm, c)
            act = gate * jax.nn.sigmoid(gate) * val
            down = down + _mm(act, wd_ref[c * FF_CHUNK:(c + 1) * FF_CHUNK, :])
        err = x1 + _to_time_order(pmt_ref[...], down) - tgt_ref[...]
        loss_ref[...] += jnp.full(loss_ref.shape, 0.5 * inv_d * jnp.sum(err * err), F32)
        dy = err * inv_d
        dy_ref[...] = dy
        dyb_ref[...] = _to_sublane_major(pm_ref[...], dy.astype(MM))

    sd = jax.ShapeDtypeStruct
    return pl.pallas_call(
        body, name="ffn_fwd", grid=(s // tm,),
        out_shape=(sd((s, D_MODEL), MM), sd((s, 2 * D_FF), F32), sd((s, D_MODEL), F32), sd((s, D_MODEL), MM),
                   sd((8, LANES), F32)),
        in_specs=[_rows(tm, 1024), _resident((1, 1024)), _resident((2 * D_FF, D_MODEL)), _resident((3, 2 * D_FF)),
                  _resident((1, 2 * D_FF)), _resident((D_FF, D_MODEL)), _rows(tm, 1024), _resident((tm, tm)),
                  _resident((tm, tm))],
        out_specs=(_rows(tm, 1024), _rows(tm, 2 * D_FF), _rows(tm, 1024), _rows(tm, 1024),
                   pl.BlockSpec((8, LANES), lambda i: (0, 0))),
        scratch_shapes=[pltpu.VMEM((SHIFT_ROWS + tm, 2 * D_FF), F32), pltpu.VMEM((SHIFT_ROWS, 2 * D_FF), F32)],
        compiler_params=_params("arbitrary"),
    )(x1, ffn_norm, wup_t, conv_w, conv_b, wdown, target, pm, pmt)


def _ffn_bwd_a(dyb, u, conv_w, conv_b, wdown, tm):
    s = dyb.shape[0]
    nt = s // tm
    hb = tm // SHIFT_ROWS
    rev = lambda i: (nt - 1 - i, 0)

    def body(dy_ref, u_ref, before_ref, cw_ref, cb_ref, wd_ref, du_ref, act_ref, dcw_ref, dcb_ref, ext_ref, extd_ref,
             ahead_ref):
        i = pl.program_id(0)
        first_tile = i == nt - 1

        @pl.when(i == 0)
        def _():
            ahead_ref[...] = jnp.zeros_like(ahead_ref)
            dcw_ref[...] = jnp.zeros_like(dcw_ref)
            dcb_ref[...] = jnp.zeros_like(dcb_ref)

        ext_ref[pl.ds(SHIFT_ROWS, tm), :] = u_ref[...]
        for c in range(4):
            cols = slice(c * FF_CHUNK, (c + 1) * FF_CHUNK)
            _fill_back_rows(ext_ref, jnp.where(first_tile, 0.0, before_ref[:, cols]), tm, cols)
        dy = dy_ref[...]
        for c in range(2):
            gate, val = _conv_glu(ext_ref, cw_ref, cb_ref, tm, c)
            sg = jax.nn.sigmoid(gate)
            sl = gate * sg
            act_ref[:, c * FF_CHUNK:(c + 1) * FF_CHUNK] = (sl * val).astype(MM)
            d_act = _mm_nt(dy, wd_ref[c * FF_CHUNK:(c + 1) * FF_CHUNK, :])
            extd_ref[pl.ds(0, tm), c * FF_CHUNK:(c + 1) * FF_CHUNK] = d_act * val * (sg * (1.0 + gate * (1.0 - sg)))
            extd_ref[pl.ds(0, tm), D_FF + c * FF_CHUNK:D_FF + (c + 1) * FF_CHUNK] = d_act * sl
        for c in range(4):
            cols = slice(c * FF_CHUNK, (c + 1) * FF_CHUNK)
            ahead = ahead_ref[:, cols]
            first2 = extd_ref[pl.ds(0, SHIFT_ROWS), cols]
            extd_ref[pl.ds(tm, 8), cols] = _step_ahead(first2[0:8, :], ahead[0:8, :])
            extd_ref[pl.ds(tm + 8, 8), cols] = _step_ahead(first2[8:16, :], ahead[8:16, :])
            ahead_ref[:, cols] = first2
            d0 = extd_ref[pl.ds(0, tm), cols]
            dcb_ref[:, cols] += jnp.sum(d0, axis=0, keepdims=True)
            for j in range(3):
                dcw_ref[j:j + 1, cols] += jnp.sum(d0 * ext_ref[pl.ds(8 * j, tm), cols], axis=0, keepdims=True)
            du = cw_ref[2:3, cols] * d0 + cw_ref[1:2, cols] * extd_ref[pl.ds(8, tm), cols]
            du = du + cw_ref[0:1, cols] * extd_ref[pl.ds(SHIFT_ROWS, tm), cols]
            du_ref[:, cols] = du.astype(MM)

    sd = jax.ShapeDtypeStruct
    return pl.pallas_call(
        body, name="ffn_bwd_a", grid=(nt,),
        out_shape=(sd((s, 2 * D_FF), MM), sd((s, D_FF), MM), sd((3, 2 * D_FF), F32), sd((1, 2 * D_FF), F32)),
        in_specs=[pl.BlockSpec((tm, D_MODEL), rev), pl.BlockSpec((tm, 2 * D_FF), rev),
                  pl.BlockSpec((SHIFT_ROWS, 2 * D_FF), lambda i: (jnp.maximum((nt - 1 - i) * hb - 1, 0), 0)),
                  _resident((3, 2 * D_FF)), _resident((1, 2 * D_FF)), _resident((D_FF, D_MODEL))],
        out_specs=(pl.BlockSpec((tm, 2 * D_FF), rev), pl.BlockSpec((tm, D_FF), rev),
                   pl.BlockSpec((3, 2 * D_FF), lambda i: (0, 0)), pl.BlockSpec((1, 2 * D_FF), lambda i: (0, 0))),
        scratch_shapes=[pltpu.VMEM((SHIFT_ROWS + tm, 2 * D_FF), F32), pltpu.VMEM((tm + SHIFT_ROWS, 2 * D_FF), F32),
                        pltpu.VMEM((SHIFT_ROWS, 2 * D_FF), F32)],
        compiler_params=_params("arbitrary"),
    )(dyb, u, u, conv_w, conv_b, wdown)


def _after(after):
    tie = [] if after is None else list(after) if isinstance(after, (list, tuple)) else [after]
    return tie, [pl.BlockSpec(memory_space=pl.ANY)] * len(tie)


def _matmul_tn(a, b, tmo, tk, name, after=None, into=None, row0=0):
    s, m = a.shape
    n = b.shape[1]
    nk = s // tk
    tie, tie_spec = _after(after)
    rows = m if into is None else into if isinstance(into, int) else into.shape[0]
    assert row0 % LANES == 0 and tmo % LANES == 0
    grown, grown_spec = ([], []) if into is None or isinstance(into, int) else ([into], [ANY])

    def body(a_ref, b_ref, *rest):
        o_ref = rest[-1]
        k = pl.program_id(1)

        @pl.when(k == 0)
        def _():
            o_ref[...] = jnp.zeros_like(o_ref)

        o_ref[...] += _mm_tn(a_ref[...], b_ref[pl.ds(pl.multiple_of(k * tk, tk), tk), :])

    return pl.pallas_call(
        body, name=name, grid=(m // tmo, nk), out_shape=jax.ShapeDtypeStruct((rows, n), F32),
        in_specs=[pl.BlockSpec((tk, tmo), lambda i, k: (k, i)), _resident((s, n))] + tie_spec + grown_spec,
        out_specs=pl.BlockSpec((pl.Element(tmo), pl.Element(n)), lambda i, k: (pl.multiple_of(row0 + i * tmo, LANES), 0)),
        input_output_aliases={2 + len(tie): 0} if grown else {},
        compiler_params=_params("parallel", "arbitrary"),
    )(a, b, *tie, *grown)


def _ffn_bwd_b(du, wup_t, x1, ffn_norm, dy, tm):
    s = du.shape[0]

    def body(du_ref, wu_ref, x1_ref, gn_ref, dy_ref, pmt_ref, dx1_ref, dx1b_ref, dg_ref):
        @pl.when(pl.program_id(0) == 0)
        def _():
            dg_ref[...] = jnp.zeros_like(dg_ref)

        dh2 = _to_time_order(pmt_ref[...], _mm(du_ref[...], wu_ref[...]))
        x1 = x1_ref[...]
        _, r = _rmsnorm_fwd(x1, gn_ref[...])
        dx, dgr = _rmsnorm_bwd(x1, r, gn_ref[...], dh2)
        dg_ref[...] += jnp.sum(dgr, axis=0, keepdims=True)
        dx1 = dy_ref[...] + dx
        dx1_ref[...] = dx1
        dx1b_ref[...] = dx1.astype(MM)

    return pl.pallas_call(
        body, name="ffn_bwd_b", grid=(s // tm,),
        out_shape=(jax.ShapeDtypeStruct((s, D_MODEL), F32), jax.ShapeDtypeStruct((s, D_MODEL), MM),
                   jax.ShapeDtypeStruct((1, D_MODEL), F32)),
        in_specs=[_rows(tm, 2 * D_FF), _resident((2 * D_FF, D_MODEL)), _rows(tm, 1024), _resident((1, 1024)),
                  _rows(tm, 1024), _resident((tm, tm))],
        out_specs=(_rows(tm, 1024), _rows(tm, 1024), pl.BlockSpec((1, D_MODEL), lambda i: (0, 0))),
        compiler_params=_params("arbitrary"),
    )(du, wup_t, x1, ffn_norm, dy, _sublane_major_matrices(tm)[1])


def _win_rows(row0, rows):
    return pl.BlockSpec((pl.Element(rows), pl.Element(D_MODEL)), lambda i: (row0, 0), pipeline_mode=pl.Buffered(1))


def _mix_bwd(dx1b, wout, g, a, b, mix, h, tm, after=None):
    s = dx1b.shape[0]
    tie, tie_spec = _after(after)

    def body(dx_ref, w_ref, g_ref, a_ref, b_ref, mix_ref, h_ref, *rest):
        da_ref, db_ref, dzg_ref, dbg_ref, dwo_ref, dwin_ref = rest[-6:]

        @pl.when(pl.program_id(0) == 0)
        def _():
            dbg_ref[...] = jnp.zeros_like(dbg_ref)
            dwo_ref[...] = jnp.zeros_like(dwo_ref)
            dwin_ref[...] = jnp.zeros_like(dwin_ref)

        dx = dx_ref[...]
        dwo_ref[...] += _mm_tn(mix_ref[...], dx)
        dmix = _mm_nt(dx, w_ref[...])
        for half, src, dst in ((0, a_ref, da_ref), (1, b_ref, db_ref)):
            cols = slice(half * 1024, (half + 1) * 1024)
            gt = g_ref[:, cols]
            dst[...] = (dmix * gt).astype(dst.dtype)
            dz = dmix * src[...] * gt * (1.0 - gt)
            dzb = dz.astype(MM)
            dzg_ref[:, cols] = dzb
            dwin_ref[cols, :] += _mm_tn(dzb, h_ref[...])
            dbg_ref[:, cols] += jnp.sum(dz, axis=0, keepdims=True)

    sd = jax.ShapeDtypeStruct
    return pl.pallas_call(
        body, name="mix_bwd", grid=(s // tm,),
        out_shape=(sd((s, 1024), F32), sd((s, 1024), MM), sd((s, 2048), MM), sd((1, 2048), F32),
                   sd((D_MODEL, D_MODEL), F32), sd((IN_WIDTH, D_MODEL), F32)),
        in_specs=[_rows(tm, 1024), _resident((1024, 1024)), _rows(tm, 2048), _rows(tm, 1024), _rows(tm, 1024),
                  _rows(tm, 1024), _rows(tm, 1024)] + tie_spec,
        out_specs=(_rows(tm, 1024), _rows(tm, 1024), _rows(tm, 2048), pl.BlockSpec((1, 2048), lambda i: (0, 0)),
                   _resident((D_MODEL, D_MODEL)), _win_rows(O_G, IN_WIDTH - O_G)),
        compiler_params=_params("arbitrary"),
    )(dx1b, wout, g, a, b, mix, h, *tie)


def _pool_bwd(u, da, wpool, pool_scale, h, d_win_t, tm, after=None):
    s = u.shape[0]
    nt = s // tm
    hb = tm // POOL_HALO

    tie, tie_spec = _after(after)

    def body(u_ref, uh_ref, da_ref, dah_ref, wp_ref, ps_ref, h_ref, *rest):
        dzu_ref, dwp_ref, dps_ref, dwin_ref, ext_ref, exte_ref = rest[-6:]
        i = pl.program_id(0)

        @pl.when(i == 0)
        def _():
            dwp_ref[...] = jnp.zeros_like(dwp_ref)
            dps_ref[...] = jnp.zeros_like(dps_ref)
            dwin_ref[...] = jnp.zeros_like(dwin_ref)

        ext_ref[pl.ds(0, POOL_HALO), :] = jnp.where(i > 0, uh_ref[...], 0.0)
        ext_ref[pl.ds(POOL_HALO, tm), :] = u_ref[...]
        pooled = _pooled(ext_ref, tm, i * tm)
        da = da_ref[...]
        dah = jnp.where(i < nt - 1, dah_ref[...], 0.0)
        t = (i * tm + lax.broadcasted_iota(jnp.int32, (tm + POOL_HALO, 1), 0)).astype(F32)
        for gi, w in enumerate(POOL_WINDOWS):
            cols = slice(gi * POOL_GROUP, (gi + 1) * POOL_GROUP)
            pg = pooled[gi].astype(MM)
            wg = wp_ref[gi]
            mixed = _mm(pg, wg)
            dps_ref[:, cols] += jnp.sum(da[:, cols] * mixed, axis=0, keepdims=True)
            dmx = (da[:, cols] * ps_ref[:, cols]).astype(MM)
            dwp_ref[gi] += _mm_tn(pg, dmx)
            dpl = _mm_nt(dmx, wg)
            dplh = _mm_nt(dah[:, cols] * ps_ref[:, cols], wg)
            cnt = jnp.minimum(t + 1.0, float(w))
            exte_ref[pl.ds(0, tm), cols] = dpl / cnt[0:tm]
            exte_ref[pl.ds(tm, POOL_HALO), cols] = dplh / cnt[tm:tm + POOL_HALO]
            acc = exte_ref[pl.ds(0, tm), cols]
            for k in range(1, w):
                acc = acc + exte_ref[pl.ds(k, tm), cols]
            dzu = (acc - dpl).astype(MM)
            dzu_ref[:, cols] = dzu
            dwin_ref[cols, :] += _mm_tn(dzu, h_ref[...])

    sd = jax.ShapeDtypeStruct
    last_halo = s // POOL_HALO - 1
    return pl.pallas_call(
        body, name="pool_bwd", grid=(nt,),
        out_shape=(sd((s, 1024), MM), sd((4, POOL_GROUP, POOL_GROUP), F32), sd((1, 1024), F32),
                   sd((IN_WIDTH, D_MODEL), F32)),
        in_specs=[_rows(tm, 1024), pl.BlockSpec((POOL_HALO, 1024), lambda i: (jnp.maximum(i * hb - 1, 0), 0)),
                  _rows(tm, 1024),
                  pl.BlockSpec((POOL_HALO, 1024), lambda i: (jnp.minimum((i + 1) * hb, last_halo), 0)),
                  _resident((4, POOL_GROUP, POOL_GROUP)), _resident((1, 1024)), _rows(tm, 1024)] + tie_spec + [ANY],
        out_specs=(_rows(tm, 1024), pl.BlockSpec((4, POOL_GROUP, POOL_GROUP), lambda i: (0, 0, 0)),
                   pl.BlockSpec((1, 1024), lambda i: (0, 0)), _win_rows(O_U, O_Q - O_U)),
        input_output_aliases={7 + len(tie): 3},
        scratch_shapes=[pltpu.VMEM((POOL_HALO + tm, 1024), F32), pltpu.VMEM((tm + POOL_HALO, 1024), F32)],
        compiler_params=_params("arbitrary"),
    )(u, u, da, da, wpool, pool_scale, h, *tie, d_win_t)


def _attn_bwd(q, kv, db, tabs, q_norm_t, k_norm_t, sink_rows, bd):
    s = q.shape[0]
    nb = s // BLOCK
    scale = HEAD_DIM ** -0.5
    cur = lambda n: (jnp.minimum(n, nb - 1), 0)
    prv = lambda n: (jnp.maximum(n - 1, 0), 0)

    def body(q_ref, kvc_ref, kvp_ref, db_ref, c_ref, s1_ref, s2_ref, cp_ref, s1p_ref, s2p_ref, qn_ref, kn_ref,
             sink_ref, bd_ref, dzq_ref, dzkv_ref, dqn_ref, dkn_ref, dsk_ref,
             carry_ref, tot_ref, dqr_ref, qacc_ref, kacc_ref, sacc_ref):
        n = pl.program_id(0)
        bdm = bd_ref[...]
        kn = kn_ref[...]

        @pl.when(n == 0)
        def _():
            carry_ref[...] = jnp.zeros_like(carry_ref)
            qacc_ref[...] = jnp.zeros_like(qacc_ref)
            kacc_ref[...] = jnp.zeros_like(kacc_ref)
            sacc_ref[...] = jnp.zeros_like(sacc_ref)

        kp_raw = kvp_ref[:, 0:128]
        kph, rp = _head_norm_fwd(kp_raw, kn, bdm)
        cp, s1p, s2p = cp_ref[...], s1p_ref[...], s2p_ref[...]

        @pl.when(n < nb)
        def _():
            c, s1, s2 = c_ref[...], s1_ref[...], s2_ref[...]
            c8, s18, s28 = _tile_lanes(c, 8), _tile_lanes(s1, 8), _tile_lanes(s2, 8)
            q_raw = q_ref[...]
            qh, rq = _head_norm_fwd(q_raw, qn_ref[...], bdm)
            qr = (_rope(qh, c8, s18, s28) * scale).astype(MM)
            kc, _ = _head_norm_fwd(kvc_ref[:, 0:128], kn, bdm)
            k2 = jnp.concatenate([_rope(kph, cp, s1p, s2p), _rope(kc, c, s1, s2)], axis=0)
            v2 = jnp.concatenate([kvp_ref[:, 128:256], kvc_ref[:, 128:256]], axis=0)
            dob = db_ref[...].astype(MM)
            mask = _fold_masks(n)
            lane = lax.broadcasted_iota(jnp.int32, (1, LANES), 1)
            dsk = jnp.zeros((1, LANES), F32)
            dk2 = jnp.zeros((2 * BLOCK, LANES), F32)
            dv2 = jnp.zeros((2 * BLOCK, LANES), F32)
            for hk in range(N_KV_HEADS):
                qs = _stack_pairs(qr, hk)
                do = _stack_pairs(dob, hk)
                dqt = jnp.zeros((LANES, PAIR_COLS), F32)
                dkb, dvb = [], []
                for par, (kb, vb) in enumerate(zip(_parity_bands(k2, hk), _parity_bands(v2, hk))):
                    probs = _probs_by_pair(_mm_nt(kb, qs), sink_ref[hk, par], mask)
                    dp = _mm_nt(vb, do)
                    prs, dss = [], []
                    for j, (pr, psink) in enumerate(probs):
                        dpj = _fold(dp[:, j * BLOCK:(j + 1) * BLOCK], mask[0])
                        coldot = jnp.sum(pr * dpj, axis=0, keepdims=True)
                        dss.append(_unfold(pr * (dpj - coldot), mask[0]))
                        prs.append(_unfold(pr, mask[0]))
                        h = hk * GQA_GROUP + 2 * j + par
                        dsk = dsk + jnp.where(lane == h, jnp.sum(-psink * coldot), 0.0)
                    ds, pr = jnp.concatenate(dss, axis=1), jnp.concatenate(prs, axis=1)
                    dqt = dqt + _mm(kb.T, ds)
                    dkb.append(_mm(ds, qs))
                    dvb.append(_mm(pr, do))
                for j in range(PAIRS):
                    col = (PAIRS * hk + j) * LANES
                    dqr_ref[:, col:col + LANES] = dqt[:, j * BLOCK:(j + 1) * BLOCK].T
                dk2 = dk2 + _fold_parity(dkb[0], dkb[1], hk)
                dv2 = dv2 + _fold_parity(dvb[0], dvb[1], hk)
            tot_ref[:, 0:128] = carry_ref[:, 0:128] + dk2[0:BLOCK, :]
            tot_ref[:, 128:256] = carry_ref[:, 128:256] + dv2[0:BLOCK, :]
            carry_ref[:, 0:128] = dk2[BLOCK:2 * BLOCK, :]
            carry_ref[:, 128:256] = dv2[BLOCK:2 * BLOCK, :]
            sacc_ref[...] += dsk
            dqh = _rope_bwd(dqr_ref[...] * scale, c8, s18, s28)
            dq, dgq = _head_norm_bwd(q_raw, rq, qn_ref[...], dqh, bdm)
            dzq_ref[...] = dq.astype(MM)
            qacc_ref[...] += jnp.sum(dgq, axis=0, keepdims=True)

        @pl.when(n == nb)
        def _():
            tot_ref[...] = carry_ref[...]

        dkh = _rope_bwd(tot_ref[:, 0:128], cp, s1p, s2p)
        dkr, dgk = _head_norm_bwd(kp_raw, rp, kn, dkh, bdm)
        dzkv_ref[:, 0:128] = dkr.astype(MM)
        dzkv_ref[:, 128:256] = tot_ref[:, 128:256].astype(MM)
        kacc_ref[...] += jnp.where(n > 0, jnp.sum(dgk, axis=0, keepdims=True), 0.0)

        @pl.when(n == nb)
        def _():
            fold = qacc_ref[:, 0:HEAD_DIM]
            for h in range(1, N_Q_HEADS):
                fold = fold + qacc_ref[:, h * HEAD_DIM:(h + 1) * HEAD_DIM]
            dqn_ref[...] = fold
            dkn_ref[...] = kacc_ref[:, 0:HEAD_DIM] + kacc_ref[:, HEAD_DIM:2 * HEAD_DIM]
            dsk_ref[...] = sacc_ref[...]

    tab = lambda im: pl.BlockSpec((BLOCK, LANES), im)
    sd = jax.ShapeDtypeStruct
    const = lambda n: (0, 0)
    return pl.pallas_call(
        body, name="attn_bwd", grid=(nb + 1,),
        out_shape=(sd((s, 1024), MM), sd((s, 256), MM), sd((1, HEAD_DIM), F32), sd((1, HEAD_DIM), F32),
                   sd((1, LANES), F32)),
        in_specs=[pl.BlockSpec((BLOCK, 1024), cur), pl.BlockSpec((BLOCK, 256), cur), pl.BlockSpec((BLOCK, 256), prv),
                  pl.BlockSpec((BLOCK, 1024), cur), tab(cur), tab(cur), tab(cur), tab(prv), tab(prv), tab(prv),
                  _resident((1, 1024)), _resident((1, 128)), _resident((N_KV_HEADS, 2, 1, PAIR_COLS)),
                  _resident((LANES, LANES))],
        out_specs=(pl.BlockSpec((BLOCK, 1024), cur), pl.BlockSpec((BLOCK, 256), prv),
                   pl.BlockSpec((1, HEAD_DIM), const), pl.BlockSpec((1, HEAD_DIM), const),
                   pl.BlockSpec((1, LANES), const)),
        scratch_shapes=[pltpu.VMEM((BLOCK, 256), F32), pltpu.VMEM((BLOCK, 256), F32), pltpu.VMEM((BLOCK, 1024), F32),
                        pltpu.VMEM((1, 1024), F32), pltpu.VMEM((1, 128), F32), pltpu.VMEM((1, LANES), F32)],
        compiler_params=_params("arbitrary"),
    )(q, kv, kv, db, *tabs, *tabs, q_norm_t, k_norm_t, sink_rows, bd)


def _inproj_bwd(dzu, dzq, dzkv, dzg, win_t, x, attn_norm, dx1, tm, after=None):
    s = x.shape[0]
    tie, tie_spec = _after(after)

    def body(du_ref, dq_ref, dkv_ref, dg_ref, w_ref, x_ref, gn_ref, dx1_ref, *rest):
        gx_ref, dgn_ref = rest[-2:]

        @pl.when(pl.program_id(0) == 0)
        def _():
            dgn_ref[...] = jnp.zeros_like(dgn_ref)

        dh = _mm(du_ref[...], w_ref[O_U:O_Q, :]) + _mm(dq_ref[...], w_ref[O_Q:O_KV, :])
        dh = dh + _mm(dkv_ref[...], w_ref[O_KV:O_G, :]) + _mm(dg_ref[...], w_ref[O_G:IN_WIDTH, :])
        x = x_ref[...]
        _, r = _rmsnorm_fwd(x, gn_ref[...])
        dx, dgr = _rmsnorm_bwd(x, r, gn_ref[...], dh)
        dgn_ref[...] += jnp.sum(dgr, axis=0, keepdims=True)
        gx_ref[...] = dx1_ref[...] + dx

    return pl.pallas_call(
        body, name="inproj_bwd", grid=(s // tm,),
        out_shape=(jax.ShapeDtypeStruct((s, D_MODEL), F32), jax.ShapeDtypeStruct((1, D_MODEL), F32)),
        in_specs=[_rows(tm, 1024), _rows(tm, 1024), _rows(tm, 256), _rows(tm, 2048),
                  _resident((IN_WIDTH, D_MODEL)), _rows(tm, 1024), _resident((1, 1024)), _rows(tm, 1024)] + tie_spec,
        out_specs=(_rows(tm, 1024), pl.BlockSpec((1, D_MODEL), lambda i: (0, 0))),
        compiler_params=_params("arbitrary"),
    )(dzu, dzq, dzkv, dzg, win_t, x, attn_norm, dx1, *tie)


def _attention_constants(q_norm, k_norm, sinks):
    inv_freq = np.float32(ROPE_THETA) ** (-np.arange(0, ROPE_DIM, 2, dtype=np.float32) / np.float32(ROPE_DIM))
    lane = np.arange(LANES) % HEAD_DIM
    invf = jnp.asarray(np.where(lane < ROPE_DIM, inv_freq[lane % (ROPE_DIM // 2)], 0.0).reshape(1, LANES), F32)
    bd = jnp.asarray(np.arange(LANES)[:, None] // HEAD_DIM == np.arange(LANES)[None, :] // HEAD_DIM, MM)
    q_norm_t = jnp.tile(q_norm, (1, N_Q_HEADS))
    k_norm_t = jnp.tile(k_norm, (1, N_KV_HEADS))
    sink_rows = jnp.repeat(sinks.reshape(N_KV_HEADS, PAIRS, 2).transpose(0, 2, 1), BLOCK, axis=2)
    sink_rows = sink_rows.reshape(N_KV_HEADS, 2, 1, PAIR_COLS)
    return invf, bd, q_norm_t, k_norm_t, sink_rows


ANY = pl.BlockSpec(memory_space=pl.ANY)


def _position():
    return lax.axis_index("x"), lax.axis_index("y"), lax.axis_index("c")


def _all_gather(shards):
    k = len(shards)

    def body(*refs):
        ins, outs = refs[:k], refs[k:2 * k]
        send_sems, recv_sems, local_sems = refs[2 * k:]
        x, y, c = _position()
        me, sibling = (x, y, c), (x, y, 1 - c)
        chips = [(1 - x, y), (x, 1 - y), (1 - x, 1 - y)]

        def copy(a, kk, block, to, src=None):
            dst = outs[a].at[4 * block[0] + 2 * block[1] + block[2]]
            return pltpu.make_async_remote_copy(
                src_ref=dst if src is None else src, dst_ref=dst, send_sem=send_sems.at[a * 7 + kk],
                recv_sem=recv_sems.at[a * 7 + kk], device_id=to, device_id_type=MESH)

        mine = [pltpu.make_async_copy(ins[a], outs[a].at[4 * x + 2 * y + c], local_sems.at[a]) for a in range(k)]
        for cp in mine:
            cp.start()
        first = []
        for a in range(k):
            first.append(copy(a, 0, me, sibling, src=ins[a]))
            first += [copy(a, 1 + j, me, (*chip, c), src=ins[a]) for j, chip in enumerate(chips)]
        for cp in first:
            cp.start()
        passed = []
        for j, chip in enumerate(chips):
            for a in range(k):
                copy(a, 1 + j, (*chip, c), me).wait_recv()
                cp = copy(a, 4 + j, (*chip, c), sibling)
                cp.start()
                passed.append(cp)
        for a in range(k):
            copy(a, 0, sibling, me).wait_recv()
            for j, chip in enumerate(chips):
                copy(a, 4 + j, (*chip, 1 - c), me).wait_recv()
        for cp in first + passed:
            cp.wait_send()
        for cp in mine:
            cp.wait()

    return pl.pallas_call(
        body, name="all_gather_weights",
        out_shape=tuple(jax.ShapeDtypeStruct((N_DEV,) + s.shape, s.dtype) for s in shards),
        in_specs=[ANY] * k, out_specs=(ANY,) * k,
        scratch_shapes=[pltpu.SemaphoreType.DMA((7 * k,)), pltpu.SemaphoreType.DMA((7 * k,)),
                        pltpu.SemaphoreType.DMA((k,))],
    )(*shards)


HBM = pl.BlockSpec(memory_space=pltpu.HBM)
SEM = pl.BlockSpec(memory_space=pltpu.SEMAPHORE)
EFFECT = pltpu.SideEffectType.DATAFLOW_SIDE_EFFECTING


def _exchange_start(name, bufs, n_sems, copies, after=None):
    k = len(bufs)
    tie, tie_spec = _after(after)
    n_in = k + len(tie)

    def body(*refs):
        for cp in copies(refs[:k], refs[n_in], refs[n_in + 1]):
            cp.start()
        refs[-1][...] = jnp.zeros_like(refs[-1])

    dma = pltpu.SemaphoreType.DMA((n_sems,))
    out = pl.pallas_call(
        body, name=name,
        out_shape=(dma, dma, *[pltpu.HBM(b.shape, b.dtype) for b in bufs], jax.ShapeDtypeStruct((8, LANES), F32)),
        in_specs=[HBM] * k + tie_spec, out_specs=(SEM, SEM, *[HBM] * k, pl.BlockSpec(memory_space=pltpu.VMEM)),
        input_output_aliases={i: 2 + i for i in range(k)},
        compiler_params=pltpu.CompilerParams(has_side_effects=EFFECT),
    )(*[pltpu.with_memory_space_constraint(b, pltpu.HBM) for b in bufs], *tie)
    return out[0], out[1], list(out[2:2 + k]), out[-1]


def _exchange_mid(name, bufs, sems_in, n_sems, waits, copies, after):
    k, ns = len(bufs), len(sems_in)

    def body(*refs):
        ins = refs[:k]
        waits(ins, *refs[k:k + ns])
        for cp in copies(ins, refs[k + ns + 1], refs[k + ns + 2]):
            cp.start()

    dma = pltpu.SemaphoreType.DMA((n_sems,))
    out = pl.pallas_call(
        body, name=name, out_shape=(dma, dma, *[pltpu.HBM(b.shape, b.dtype) for b in bufs]),
        in_specs=[HBM] * k + [SEM] * ns + [ANY], out_specs=(SEM, SEM, *[HBM] * k),
        input_output_aliases={i: 2 + i for i in range(k)},
        compiler_params=pltpu.CompilerParams(has_side_effects=EFFECT),
    )(*bufs, *sems_in, after)
    return out[0], out[1], list(out[2:])


def _exchange_wait(name, bufs, sems, waits, after=None):
    k, ns = len(bufs), len(sems)
    tie, tie_spec = _after(after)

    def body(*refs):
        waits(refs[:k], *refs[k:k + ns])

    out = pl.pallas_call(
        body, name=name, out_shape=tuple(pltpu.HBM(b.shape, b.dtype) for b in bufs),
        in_specs=[HBM] * k + [SEM] * ns + tie_spec, out_specs=(HBM,) * k,
        input_output_aliases={i: i for i in range(k)},
        compiler_params=pltpu.CompilerParams(has_side_effects=EFFECT),
    )(*bufs, *sems, *tie)
    return list(out)


def _gather_copies(k, direct):
    def copies(refs, send_sems, recv_sems):
        x, y, c = _position()
        chips = [(1 - x, y), (x, 1 - y), (1 - x, 1 - y)]
        out = []
        for a in range(k):
            land = refs[k + a]
            if direct:
                mine = land.at[4 * x + 2 * y + c]
                for kk, to in enumerate([(x, y, 1 - c)] + [(*chip, c) for chip in chips]):
                    out.append(pltpu.make_async_remote_copy(
                        src_ref=refs[a], dst_ref=mine, send_sem=send_sems.at[4 * a + kk],
                        recv_sem=recv_sems.at[4 * a + kk], device_id=to, device_id_type=MESH))
            else:
                for j, (px, py) in enumerate(chips):
                    slot = land.at[4 * px + 2 * py + c]
                    out.append(pltpu.make_async_remote_copy(
                        src_ref=slot, dst_ref=slot, send_sem=send_sems.at[3 * a + j], recv_sem=recv_sems.at[3 * a + j],
                        device_id=(x, y, 1 - c), device_id_type=MESH))
        return out
    return copies


def _all_gather_behind(shards, start_after, mid_after):
    k = len(shards)
    me = 4 * lax.axis_index("x") + 2 * lax.axis_index("y") + lax.axis_index("c")
    lands = [lax.dynamic_update_slice(lax.empty((N_DEV,) + s.shape, s.dtype), s[None], (me, 0, 0)) for s in shards]
    direct, passed = _gather_copies(k, True), _gather_copies(k, False)

    send_a, recv_a, bufs, token = _exchange_start("gather_start", list(shards) + lands, 4 * k, direct, start_after)

    def finish():
        def wait_ici(refs, send_sems, recv_sems):
            for i, cp in enumerate(direct(refs, send_sems, recv_sems)):
                if i % 4:
                    cp.wait_recv()

        send_b, recv_b, bufs2 = _exchange_mid("gather_pass", bufs, [send_a, recv_a], 3 * k, wait_ici, passed,
                                              mid_after())

        def wait_all(refs, sa, ra, sb, rb):
            for i, cp in enumerate(direct(refs, sa, ra)):
                cp.wait_send()
                if i % 4 == 0:
                    cp.wait_recv()
            for cp in passed(refs, sb, rb):
                cp.wait()

        return _exchange_wait("gather_wait", bufs2, [send_a, recv_a, send_b, recv_b], wait_all)[k:]

    return token, finish


def _pair_copies(k):
    def copies(refs, send_sems, recv_sems):
        x, y, c = _position()
        return [pltpu.make_async_remote_copy(
            src_ref=refs[a].at[2 * ch + 1 - c], dst_ref=refs[k + a].at[ch], send_sem=send_sems.at[4 * a + ch],
            recv_sem=recv_sems.at[4 * a + ch], device_id=(x, y, 1 - c), device_id_type=MESH)
            for a in range(k) for ch in range(4)]
    return copies


def _chip_copies(k):
    def copies(refs, send_sems, recv_sems):
        x, y, c = _position()
        return [pltpu.make_async_remote_copy(
            src_ref=refs[a].at[2 * px + py], dst_ref=refs[k + a].at[rel], send_sem=send_sems.at[3 * a + rel],
            recv_sem=recv_sems.at[3 * a + rel], device_id=(px, py, c), device_id_type=MESH)
            for a in range(k) for rel, (px, py) in enumerate([(1 - x, y), (x, 1 - y), (1 - x, 1 - y)])]
    return copies


def _symmetric_exchange(name, srcs, n_land, copies_of):
    k = len(srcs)
    lands = [lax.empty((n_land,) + s.shape[1:], s.dtype) for s in srcs]
    copies = copies_of(k)
    send_sems, recv_sems, bufs, token = _exchange_start(name + "_start", list(srcs) + lands, n_land * k, copies)

    def finish(after):
        def wait_all(refs, ss, rs):
            for cp in copies(refs, ss, rs):
                cp.wait()

        done = _exchange_wait(name + "_wait", bufs, [send_sems, recv_sems], wait_all, after)
        return done[:k], done[k:]

    return token, finish


def _pair_add(full, recv, wire):
    _, r, c_ = full.shape
    core = lax.axis_index("c").astype(jnp.int32).reshape(1)

    def body(core_ref, f_ref, r_ref, pw_ref, own_ref):
        ch = pl.program_id(0)
        x, y, _ = _position()
        tot = f_ref[0, 0] + r_ref[0]
        pw_ref[0] = tot.astype(pw_ref.dtype)

        @pl.when(ch == 2 * x + y)
        def _():
            own_ref[...] = tot

    return pl.pallas_call(
        body, name="grad_pair_add",
        grid_spec=pltpu.PrefetchScalarGridSpec(
            num_scalar_prefetch=1, grid=(4,),
            in_specs=[pl.BlockSpec((1, 1, r, c_), lambda i, core_ref: (i, core_ref[0], 0, 0)),
                      pl.BlockSpec((1, r, c_), lambda i, core_ref: (i, 0, 0))],
            out_specs=(pl.BlockSpec((1, r, c_), lambda i, core_ref: (i, 0, 0)),
                       pl.BlockSpec((r, c_), lambda i, core_ref: (0, 0)))),
        out_shape=(jax.ShapeDtypeStruct((4, r, c_), wire), jax.ShapeDtypeStruct((r, c_), F32)),
        compiler_params=_params("arbitrary"),
    )(core, full.reshape(4, 2, r, c_), recv)


def _adamw_math(w, g, m, v):
    m = ADAM_B1 * m + (1.0 - ADAM_B1) * g
    v = ADAM_B2 * v + (1.0 - ADAM_B2) * (g * g)
    m_hat = m / (1.0 - ADAM_B1 ** ADAM_STEP)
    v_hat = v / (1.0 - ADAM_B2 ** ADAM_STEP)
    delta = -ADAM_LR * (m_hat / (jnp.sqrt(v_hat) + ADAM_EPS) + ADAM_WD * w)
    return delta, m, v


def _row_tile(r):
    for t in (256, 272, 176, 128):
        if r % t == 0 and r > t:
            return t
    return r


def _adamw(g_own, recv, w, m, v):
    r, c_ = w.shape
    t = _row_tile(r)
    blk = pl.BlockSpec((t, c_), lambda i: (i, 0))

    def body(g_ref, r_ref, w_ref, m_ref, v_ref, go_ref, d_ref, mo_ref, vo_ref):
        g = g_ref[...]
        for i in range(3):
            g = g + r_ref[i].astype(F32)
        go_ref[...] = g
        d_ref[...], mo_ref[...], vo_ref[...] = _adamw_math(w_ref[...], g, m_ref[...], v_ref[...])

    return pl.pallas_call(
        body, name="adamw", grid=(r // t,), out_shape=(jax.ShapeDtypeStruct((r, c_), F32),) * 4,
        in_specs=[blk, pl.BlockSpec((3, t, c_), lambda i: (0, i, 0)), blk, blk, blk], out_specs=(blk,) * 4,
        compiler_params=_params("parallel"),
    )(g_own, recv, w, m, v)


SMALL = ("attn_norm", "b_gate", "pool_scale", "q_norm", "k_norm", "sinks", "ffn_norm", "conv_b")
SMALL_SIZES = (1024, 2048, 1024, 64, 64, 16, 1024, 5632)
SMALL_OFFSETS = tuple(sum(-(-s // LANES) * LANES for s in SMALL_SIZES[:i]) for i in range(len(SMALL_SIZES) + 1))
SMALL_WIDTH = SMALL_OFFSETS[-1] + LANES


def _pack_small(d, loss=None):
    parts = [jnp.pad(d[n].reshape(1, -1), ((0, 0), (0, -s % LANES))) for n, s in zip(SMALL, SMALL_SIZES)]
    last = jnp.zeros((1, LANES), F32) if loss is None else jnp.pad(loss.reshape(1, 1), ((0, 0), (0, LANES - 1)))
    return jnp.concatenate(parts + [last], axis=1)


def _small_allreduce(gp):
    def body(g_ref, sum_ref, slots_ref, send_sems, recv_sems):
        x, y, c = _position()
        me = 4 * x + 2 * y + c
        slots_ref[me] = g_ref[...]
        cps = []
        for rel in range(1, N_DEV):
            fx, fy, fc = (rel >> 2) & 1, (rel >> 1) & 1, rel & 1
            to = (1 - x if fx else x, 1 - y if fy else y, 1 - c if fc else c)
            cps.append(pltpu.make_async_remote_copy(
                src_ref=g_ref, dst_ref=slots_ref.at[me], send_sem=send_sems.at[rel - 1],
                recv_sem=recv_sems.at[rel - 1], device_id=to, device_id_type=MESH))
        for cp in cps:
            cp.start()
        for cp in cps:
            cp.wait()
        g = slots_ref[0]
        for i in range(1, N_DEV):
            g = g + slots_ref[i]
        sum_ref[...] = g

    vm = pl.BlockSpec(memory_space=pltpu.VMEM)
    return pl.pallas_call(
        body, name="small_allreduce", out_shape=jax.ShapeDtypeStruct((1, SMALL_WIDTH), F32),
        in_specs=[vm], out_specs=vm,
        scratch_shapes=[pltpu.VMEM((N_DEV, 1, SMALL_WIDTH), F32), pltpu.SemaphoreType.DMA((N_DEV - 1,)),
                        pltpu.SemaphoreType.DMA((N_DEV - 1,))],
    )(gp)


def _small_adamw(gsum, ws, ms, vs):
    n = len(SMALL)

    def body(g_ref, *rest):
        w_refs, m_refs, v_refs, outs = rest[:n], rest[n:2 * n], rest[2 * n:3 * n], rest[3 * n:]
        for j, size in enumerate(SMALL_SIZES):
            g = g_ref[:, SMALL_OFFSETS[j]:SMALL_OFFSETS[j] + size]
            results = (g,) + _adamw_math(w_refs[j][...], g, m_refs[j][...], v_refs[j][...])
            for kind, val in enumerate(results):
                outs[kind * n + j][...] = val
        outs[4 * n][...] = g_ref[:, SMALL_OFFSETS[-1]:SMALL_WIDTH]

    vm = pl.BlockSpec(memory_space=pltpu.VMEM)
    shapes = tuple(jax.ShapeDtypeStruct((1, s), F32) for s in SMALL_SIZES) * 4
    return pl.pallas_call(
        body, name="small_adamw", out_shape=shapes + (jax.ShapeDtypeStruct((1, LANES), F32),),
        in_specs=[vm] * (1 + 3 * n), out_specs=(vm,) * (4 * n + 1),
    )(gsum, *ws, *ms, *vs)


WEIGHTS = ("attn_norm", "w_in", "b_gate", "w_pool", "pool_scale", "q_norm", "k_norm", "sinks", "w_out", "ffn_norm",
           "w_up", "conv_w", "conv_b", "w_down")


def kernel(x, positions, attn_norm, w_in, b_gate, w_pool, pool_scale, q_norm, k_norm, sinks, w_out, ffn_norm, w_up, conv_w, conv_b, w_down, loss_target, m_attn_norm, m_w_in, m_b_gate, m_w_pool, m_pool_scale, m_q_norm, m_k_norm, m_sinks, m_w_out, m_ffn_norm, m_w_up, m_conv_w, m_conv_b, m_w_down, v_attn_norm, v_w_in, v_b_gate, v_w_pool, v_pool_scale, v_q_norm, v_k_norm, v_sinks, v_w_out, v_ffn_norm, v_w_up, v_conv_w, v_conv_b, v_w_down):
    w = dict(attn_norm=attn_norm, w_in=w_in, b_gate=b_gate, w_pool=w_pool, pool_scale=pool_scale, q_norm=q_norm,
             k_norm=k_norm, sinks=sinks, w_out=w_out, ffn_norm=ffn_norm, w_up=w_up, conv_w=conv_w, conv_b=conv_b,
             w_down=w_down)
    m = dict(attn_norm=m_attn_norm, w_in=m_w_in, b_gate=m_b_gate, w_pool=m_w_pool, pool_scale=m_pool_scale,
             q_norm=m_q_norm, k_norm=m_k_norm, sinks=m_sinks, w_out=m_w_out, ffn_norm=m_ffn_norm, w_up=m_w_up,
             conv_w=m_conv_w, conv_b=m_conv_b, w_down=m_w_down)
    v = dict(attn_norm=v_attn_norm, w_in=v_w_in, b_gate=v_b_gate, w_pool=v_w_pool, pool_scale=v_pool_scale,
             q_norm=v_q_norm, k_norm=v_k_norm, sinks=v_sinks, w_out=v_w_out, ffn_norm=v_ffn_norm, w_up=v_w_up,
             conv_w=v_conv_w, conv_b=v_conv_b, w_down=v_w_down)
    seq = x.shape[1]
    tm = 256
    tw = min(seq, 512)
    tk = min(seq, 1024)
    xs, target, pos_col = x[0], loss_target[0], positions.reshape(seq, 1)
    invf, bd, q_norm_t, k_norm_t, sink_rows = _attention_constants(q_norm, k_norm, sinks)
    out, done = {}, {}
    nat = {"w_in": (D_MODEL, 544), "w_pool": (128, POOL_GROUP), "w_out": (128, D_MODEL), "w_up": (D_MODEL, 704),
           "conv_w": (3, 704), "w_down": (352, D_MODEL)}

    def update(names, owns, recvs):
        for name, own, recv in zip(names, owns, recvs):
            w2, m2, v2 = (t[name].reshape(nat[name]) for t in (w, m, v))
            if name in ("w_in", "w_up"):
                res = _adamw(own, recv, w2.T, m2.T, v2.T)
                done[name] = res[1]
                res = [t.T for t in res]
            else:
                res = _adamw(own, recv, w2, m2, v2)
                done[name] = res[1]
            out[name] = [t.reshape(w[name].shape) for t in res]

    (g_win,) = _all_gather([w_in[0].T.astype(MM)])
    win_t = g_win.reshape(IN_WIDTH, D_MODEL)
    fwd = {}
    token, gather_rest = _all_gather_behind(
        [w_pool[0].astype(MM).reshape(128, POOL_GROUP), w_out[0].astype(MM), w_up[0].T.astype(MM), conv_w[0],
         w_down[0].astype(MM)], g_win, lambda: fwd["b"])

    tabs = _rope_tables(pos_col, invf)
    h, u, q, kv, g = _inproj_fwd(xs, attn_norm + token[0:1, 0:1], win_t, b_gate, tw)
    fwd["b"] = b = _attn_fwd(q, kv, tabs, q_norm_t, k_norm_t, sink_rows, bd)
    g_wpool, g_wout, g_wup, g_convw, g_wdown = gather_rest()
    wpool = g_wpool.reshape(N_DEV, 4, 32, POOL_GROUP).transpose(1, 0, 2, 3).reshape(4, POOL_GROUP, POOL_GROUP)
    wout = g_wout.reshape(D_MODEL, D_MODEL)
    wup_t = g_wup.reshape(2 * D_FF, D_MODEL)
    convw = g_convw.transpose(1, 0, 2).reshape(3, 2 * D_FF)
    wdown = g_wdown.reshape(D_FF, D_MODEL)
    a = _pool_fwd(u, wpool, pool_scale, tw)
    x1, mix = _mix_out_fwd(xs, g, a, b, wout, tw)
    h2, uff, dy, dyb, lossp = _ffn_fwd(x1, ffn_norm, wup_t, convw, conv_b, wdown, target, tm)

    du, act, d_conv_w, d_conv_b = _ffn_bwd_a(dyb, uff, convw, conv_b, wdown, tm)
    d_wdown = _matmul_tn(act, dyb, FF_CHUNK, tk, "dw_down")
    dx1, dx1b, d_ffn_norm = _ffn_bwd_b(du, wup_t, x1, ffn_norm, dy, tm)
    d_wup_t = _matmul_tn(du, h2, FF_CHUNK, tk, "dw_up")
    late = ("w_down", "w_up", "conv_w")
    late_wire = (WIRE, WIRE, F32)
    late_full = [d_wdown.reshape(N_DEV, 352, D_MODEL), d_wup_t.reshape(N_DEV, 704, D_MODEL),
                 d_conv_w.reshape(3, N_DEV, 704).transpose(1, 0, 2)]
    token, late_pair = _symmetric_exchange("late_pair", late_full, 4, _pair_copies)
    da, db, dzg, d_b_gate, d_wout, d_win_t = _mix_bwd(dx1b, wout, g, a, b, mix, h, tm, after=token)
    late_pw, late_own = zip(*[_pair_add(f, r, wd) for f, r, wd in zip(*late_pair(dzg), late_wire)])
    token, late_chip = _symmetric_exchange("late_chip", list(late_pw), 3, _chip_copies)
    dzu, d_wpool, d_pool_scale, d_win_t = _pool_bwd(u, da, wpool, pool_scale, h, d_win_t, tw, after=token)
    dzq, dzkv, d_q_norm, d_k_norm, d_sinks = _attn_bwd(q, kv, db, tabs, q_norm_t, k_norm_t, sink_rows, bd)
    d_win_t = _matmul_tn(dzq, h, 1024, tk, "dw_in_q", into=d_win_t, row0=O_Q)
    d_win_t = _matmul_tn(dzkv, h, 256, tk, "dw_in_kv", into=d_win_t, row0=O_KV)
    early = ("w_in", "w_pool", "w_out")
    early_full = [d_win_t.reshape(N_DEV, 544, D_MODEL),
                  d_wpool.reshape(4, N_DEV, 32, POOL_GROUP).transpose(1, 0, 2, 3).reshape(N_DEV, 128, POOL_GROUP),
                  d_wout.reshape(N_DEV, 128, D_MODEL)]
    token, early_pair = _symmetric_exchange("early_pair", early_full, 4, _pair_copies)
    update(late, late_own, late_chip(token)[1])
    early_pw, early_own = zip(*[_pair_add(f, r, WIRE) for f, r in zip(*early_pair([done[n] for n in late]))])
    token, early_chip = _symmetric_exchange("early_chip", list(early_pw), 3, _chip_copies)
    grad_x, d_attn_norm = _inproj_bwd(dzu, dzq, dzkv, dzg, win_t, xs, attn_norm, dx1, tw, after=token)
    gr = dict(attn_norm=d_attn_norm, b_gate=d_b_gate, pool_scale=d_pool_scale, q_norm=d_q_norm, k_norm=d_k_norm,
              sinks=d_sinks[:, 0:N_Q_HEADS], ffn_norm=d_ffn_norm, conv_b=d_conv_b)
    small = _small_adamw(_small_allreduce(_pack_small(gr, lossp[0, 0])), *[[t[n] for n in SMALL] for t in (w, m, v)])
    loss = small[-1][0, 0]
    for j, name in enumerate(SMALL):
        out[name] = [small[kind * len(SMALL) + j] for kind in range(4)]
    update(early, early_own, early_chip(small[0])[1])

    return (loss, grad_x[None], *[out[n][0] for n in WEIGHTS], *[out[n][1] for n in WEIGHTS],
            *[out[n][2] for n in WEIGHTS], *[out[n][3] for n in WEIGHTS])
```

```python
import functools

import numpy as np
import jax
import jax.numpy as jnp
from jax import lax
from jax.experimental import pallas as pl
from jax.experimental.pallas import tpu as pltpu

F32 = jnp.float32
MM = jnp.bfloat16
WIRE = jnp.bfloat16
ACT = jnp.bfloat16

D_MODEL = 1024
D_FF = 2816
HEAD_DIM = 64
N_Q_HEADS = 16
N_KV_HEADS = 2
GQA_GROUP = 8
BLOCK = 128
ROPE_DIM = 16
ROPE_THETA = 500000.0
POOL_WINDOWS = (2, 4, 8, 16)
POOL_GROUP = 256
POOL_HALO = 16
EPS = 1e-6
NEG = -1e30
O_U, O_Q, O_KV, O_G, IN_WIDTH = 0, 1024, 2048, 2304, 4352
FF_CHUNK = 1408

ADAM_LR, ADAM_B1, ADAM_B2, ADAM_EPS, ADAM_WD, ADAM_STEP = 0.001, 0.9, 0.999, 1e-08, 0.01, 10

N_DEV = 8
LANES = 128
VMEM_LIMIT_BYTES = 56 * 1024 * 1024
MESH = pl.DeviceIdType.MESH


def _params(*sem):
    return pltpu.CompilerParams(dimension_semantics=sem, vmem_limit_bytes=VMEM_LIMIT_BYTES)


def _resident(shape):
    nd = len(shape)
    return pl.BlockSpec(shape, lambda *_: (0,) * nd, pipeline_mode=pl.Buffered(1))


def _rows(tm, width):
    return pl.BlockSpec((tm, width), lambda i: (i, 0))


def _mm(a, b):
    return jnp.dot(a.astype(MM), b.astype(MM), preferred_element_type=F32)


def _mm_nt(a, b):
    return lax.dot_general(a.astype(MM), b.astype(MM), (((1,), (1,)), ((), ())), preferred_element_type=F32)


def _mm_tn(a, b):
    return lax.dot_general(a.astype(MM), b.astype(MM), (((0,), (0,)), ((), ())), preferred_element_type=F32)


def _rmsnorm_fwd(x, g):
    r = lax.rsqrt(jnp.mean(x * x, axis=-1, keepdims=True) + EPS)
    return x * r * g, r


def _rmsnorm_bwd(x, r, g, dy):
    xn = x * r
    dxn = dy * g
    dx = r * (dxn - xn * jnp.mean(dxn * xn, axis=-1, keepdims=True))
    return dx, dy * xn


def _group_sum64(v, bd):
    hi = v.astype(MM)
    lo = (v - hi.astype(F32)).astype(MM)
    outs = []
    for t in range(v.shape[1] // LANES):
        sl = slice(LANES * t, LANES * (t + 1))
        outs.append(jnp.dot(hi[:, sl], bd, preferred_element_type=F32)
                    + jnp.dot(lo[:, sl], bd, preferred_element_type=F32))
    return outs[0] if len(outs) == 1 else jnp.concatenate(outs, axis=1)


def _head_norm_fwd(x, g, bd):
    r = lax.rsqrt(_group_sum64(x * x, bd) * (1.0 / HEAD_DIM) + EPS)
    return x * r * g, r


def _head_norm_bwd(x, r, g, dy, bd):
    xn = x * r
    dxn = dy * g
    dx = r * (dxn - xn * (_group_sum64(dxn * xn, bd) * (1.0 / HEAD_DIM)))
    return dx, dy * xn


def _rope(x, c, s1, s2):
    w = x.shape[1]
    return x * c + pltpu.roll(x, w - ROPE_DIM // 2, 1) * s1 + pltpu.roll(x, ROPE_DIM // 2, 1) * s2


def _rope_bwd(dy, c, s1, s2):
    w = dy.shape[1]
    return dy * c + pltpu.roll(dy * s1, ROPE_DIM // 2, 1) + pltpu.roll(dy * s2, w - ROPE_DIM // 2, 1)


def _tile_lanes(t, reps):
    return t if reps == 1 else jnp.concatenate([t] * reps, axis=1)


def _rope_tables(pos_col, invf):
    s = pos_col.shape[0]
    tm = min(s, 1024)

    def body(pos_ref, invf_ref, c_ref, s1_ref, s2_ref):
        ang = pos_ref[...].astype(F32) * invf_ref[...]
        lane = lax.broadcasted_iota(jnp.int32, ang.shape, 1) % HEAD_DIM
        sn = jnp.sin(ang)
        c_ref[...] = jnp.cos(ang)
        s1_ref[...] = jnp.where(lane < ROPE_DIM // 2, -sn, 0.0)
        s2_ref[...] = jnp.where((lane >= ROPE_DIM // 2) & (lane < ROPE_DIM), sn, 0.0)

    out = jax.ShapeDtypeStruct((s, LANES), F32)
    return pl.pallas_call(
        body, name="rope_tables", grid=(s // tm,), out_shape=(out, out, out),
        in_specs=[_rows(tm, 1), _resident((1, LANES))],
        out_specs=(_rows(tm, LANES),) * 3, compiler_params=_params("parallel"),
    )(pos_col, invf)


def _inproj_fwd(x, attn_norm, win_t, b_gate, tm):
    s = x.shape[0]

    def body(x_ref, gn_ref, w_ref, bg_ref, h_ref, u_ref, q_ref, kv_ref, g_ref):
        h, _ = _rmsnorm_fwd(x_ref[...], gn_ref[...])
        h = h.astype(MM)
        h_ref[...] = h
        u_ref[...] = _mm_nt(h, w_ref[O_U:O_Q, :])
        q_ref[...] = _mm_nt(h, w_ref[O_Q:O_KV, :])
        kv_ref[...] = _mm_nt(h, w_ref[O_KV:O_G, :])
        g_ref[...] = jax.nn.sigmoid(_mm_nt(h, w_ref[O_G:IN_WIDTH, :]) + bg_ref[...])

    sd = jax.ShapeDtypeStruct
    return pl.pallas_call(
        body, name="inproj_fwd", grid=(s // tm,),
        out_shape=(sd((s, D_MODEL), MM), sd((s, 1024), F32), sd((s, 1024), F32), sd((s, 256), F32),
                   sd((s, 2048), F32)),
        in_specs=[_rows(tm, D_MODEL), _resident((1, D_MODEL)), _resident((IN_WIDTH, D_MODEL)), _resident((1, 2048))],
        out_specs=(_rows(tm, D_MODEL), _rows(tm, 1024), _rows(tm, 1024), _rows(tm, 256), _rows(tm, 2048)),
        compiler_params=_params("parallel"),
    )(x, attn_norm, win_t, b_gate)


def _pooled(ext_ref, tm, row0):
    t = (row0 + lax.broadcasted_iota(jnp.int32, (tm, 1), 0)).astype(F32)
    out = []
    for gi, w in enumerate(POOL_WINDOWS):
        cols = slice(gi * POOL_GROUP, (gi + 1) * POOL_GROUP)
        acc = ext_ref[pl.ds(POOL_HALO, tm), cols]
        for k in range(1, w):
            acc = acc + ext_ref[pl.ds(POOL_HALO - k, tm), cols]
        cnt = jnp.minimum(t + 1.0, float(w))
        out.append(acc / cnt - ext_ref[pl.ds(POOL_HALO, tm), cols])
    return out


def _pool_fwd(u, wpool, pool_scale, tm):
    s = u.shape[0]
    hb = tm // POOL_HALO

    def body(u_ref, halo_ref, wp_ref, ps_ref, a_ref, ext_ref):
        i = pl.program_id(0)
        ext_ref[pl.ds(0, POOL_HALO), :] = jnp.where(i > 0, halo_ref[...], 0.0)
        ext_ref[pl.ds(POOL_HALO, tm), :] = u_ref[...]
        pooled = _pooled(ext_ref, tm, i * tm)
        for gi in range(4):
            cols = slice(gi * POOL_GROUP, (gi + 1) * POOL_GROUP)
            a_ref[:, cols] = (_mm(pooled[gi], wp_ref[gi]) * ps_ref[:, cols]).astype(a_ref.dtype)

    return pl.pallas_call(
        body, name="pool_fwd", grid=(s // tm,), out_shape=jax.ShapeDtypeStruct((s, 1024), ACT),
        in_specs=[_rows(tm, 1024), pl.BlockSpec((POOL_HALO, 1024), lambda i: (jnp.maximum(i * hb - 1, 0), 0)),
                  _resident((4, POOL_GROUP, POOL_GROUP)), _resident((1, 1024))],
        out_specs=_rows(tm, 1024), scratch_shapes=[pltpu.VMEM((POOL_HALO + tm, 1024), F32)],
        compiler_params=_params("parallel"),
    )(u, u, wpool, pool_scale)


PAIRS = GQA_GROUP // 2
PAIR_COLS = PAIRS * BLOCK


def _fold_masks(n):
    r = lax.broadcasted_iota(jnp.int32, (BLOCK, BLOCK), 0)
    i = lax.broadcasted_iota(jnp.int32, (BLOCK, BLOCK), 1)
    prev = r > i
    return prev, jnp.where(prev & (n == 0), NEG, 0.0)


def _fold(band, prev):
    return jnp.where(prev, band[0:BLOCK, :], band[BLOCK:2 * BLOCK, :])


def _unfold(folded, prev):
    top = jnp.where(prev, folded, 0.0)
    return jnp.concatenate([top, folded - top], axis=0).astype(MM)


def _probs_by_pair(sc, sink, masks):
    prev, bias = masks
    out = []
    for j in range(PAIRS):
        cols = slice(j * BLOCK, (j + 1) * BLOCK)
        out.append(_softmax_sink_t(_fold(sc[:, cols], prev) + bias, sink[:, cols]))
    return out


def _stack_pairs(x, hk):
    return jnp.concatenate([x[:, (PAIRS * hk + j) * LANES:(PAIRS * hk + j + 1) * LANES] for j in range(PAIRS)], axis=0)


def _parity_bands(t, hk):
    low = lax.broadcasted_iota(jnp.int32, t.shape, 1) < HEAD_DIM
    own = jnp.where(low if hk == 0 else ~low, t, 0.0)
    other = pltpu.roll(own, HEAD_DIM, 1)
    return (own, other) if hk == 0 else (other, own)


def _fold_parity(even, odd, hk):
    low = lax.broadcasted_iota(jnp.int32, even.shape, 1) < HEAD_DIM
    comb = jnp.where(low, even, odd)
    comb = comb + pltpu.roll(comb, HEAD_DIM, 1)
    return jnp.where(low if hk == 0 else ~low, comb, 0.0)


def _softmax_sink_t(s, sink):
    m = jnp.maximum(jnp.max(s, axis=0, keepdims=True), sink)
    p = jnp.exp(s - m)
    es = jnp.exp(sink - m)
    inv = 1.0 / (jnp.sum(p, axis=0, keepdims=True) + es)
    return p * inv, es * inv


def _attn_fwd(q, kv, tabs, q_norm_t, k_norm_t, sink_rows, bd):
    s = q.shape[0]
    nb = s // BLOCK
    scale = HEAD_DIM ** -0.5
    cur = lambda n: (n, 0)
    prv = lambda n: (jnp.maximum(n - 1, 0), 0)

    def body(q_ref, kvc_ref, kvp_ref, c_ref, s1_ref, s2_ref, cp_ref, s1p_ref, s2p_ref, qn_ref, kn_ref, sink_ref,
             bd_ref, o_ref, pr_ref, ps_ref):
        n = pl.program_id(0)
        bdm = bd_ref[...]
        c, s1, s2 = c_ref[...], s1_ref[...], s2_ref[...]
        qh, _ = _head_norm_fwd(q_ref[...], qn_ref[...], bdm)
        qr = (_rope(qh, _tile_lanes(c, 8), _tile_lanes(s1, 8), _tile_lanes(s2, 8)) * scale).astype(MM)
        kc, _ = _head_norm_fwd(kvc_ref[:, 0:128], kn_ref[...], bdm)
        kp, _ = _head_norm_fwd(kvp_ref[:, 0:128], kn_ref[...], bdm)
        k2 = jnp.concatenate([_rope(kp, cp_ref[...], s1p_ref[...], s2p_ref[...]), _rope(kc, c, s1, s2)], axis=0)
        v2 = jnp.concatenate([kvp_ref[:, 128:256], kvc_ref[:, 128:256]], axis=0)
        mask = _fold_masks(n)
        for hk in range(N_KV_HEADS):
            qs = _stack_pairs(qr, hk)
            ot = jnp.zeros((LANES, PAIR_COLS), F32)
            for par, (kb, vb) in enumerate(zip(_parity_bands(k2, hk), _parity_bands(v2, hk))):
                probs = _probs_by_pair(_mm_nt(kb, qs), sink_ref[hk, par], mask)
                pr_ref[0, 2 * hk + par] = jnp.concatenate([pr for pr, _ in probs], axis=1).astype(MM)
                ps_ref[0, 2 * hk + par] = jnp.concatenate([ps for _, ps in probs], axis=1)
                ot = ot + _mm(vb.T, jnp.concatenate([_unfold(pr, mask[0]) for pr, _ in probs], axis=1))
            for j in range(PAIRS):
                col = (PAIRS * hk + j) * LANES
                o_ref[:, col:col + LANES] = ot[:, j * BLOCK:(j + 1) * BLOCK].T.astype(o_ref.dtype)

    tab = lambda im: pl.BlockSpec((BLOCK, LANES), im)
    sd = jax.ShapeDtypeStruct
    return pl.pallas_call(
        body, name="attn_fwd", grid=(nb,),
        out_shape=(sd((s, 1024), ACT), sd((nb, 4, BLOCK, PAIR_COLS), MM), sd((nb, 4, 1, PAIR_COLS), F32)),
        in_specs=[pl.BlockSpec((BLOCK, 1024), cur), pl.BlockSpec((BLOCK, 256), cur), pl.BlockSpec((BLOCK, 256), prv),
                  tab(cur), tab(cur), tab(cur), tab(prv), tab(prv), tab(prv),
                  _resident((1, 1024)), _resident((1, 128)), _resident((N_KV_HEADS, 2, 1, PAIR_COLS)),
                  _resident((LANES, LANES))],
        out_specs=(pl.BlockSpec((BLOCK, 1024), cur), pl.BlockSpec((1, 4, BLOCK, PAIR_COLS), lambda n: (n, 0, 0, 0)),
                   pl.BlockSpec((1, 4, 1, PAIR_COLS), lambda n: (n, 0, 0, 0))),
        compiler_params=_params("parallel"),
    )(q, kv, kv, *tabs, *tabs, q_norm_t, k_norm_t, sink_rows, bd)


def _mix_out_fwd(x, g, a, b, wout, tm):
    s = x.shape[0]

    def body(x_ref, g_ref, a_ref, b_ref, w_ref, x1_ref, mix_ref):
        mix = (g_ref[:, 0:1024] * a_ref[...] + g_ref[:, 1024:2048] * b_ref[...]).astype(MM)
        mix_ref[...] = mix
        x1_ref[...] = x_ref[...] + _mm(mix, w_ref[...])

    return pl.pallas_call(
        body, name="mix_out_fwd", grid=(s // tm,),
        out_shape=(jax.ShapeDtypeStruct((s, D_MODEL), F32), jax.ShapeDtypeStruct((s, D_MODEL), MM)),
        in_specs=[_rows(tm, 1024), _rows(tm, 2048), _rows(tm, 1024), _rows(tm, 1024), _resident((1024, 1024))],
        out_specs=(_rows(tm, 1024), _rows(tm, 1024)), compiler_params=_params("parallel"),
    )(x, g, a, b, wout)


SHIFT_ROWS = 16


def _sublane_major_matrices(tm):
    r = np.arange(tm)
    pm = r[None, :] == ((tm // 8) * (r % 8) + r // 8)[:, None]
    return jnp.asarray(pm, MM), jnp.asarray(pm.T, MM)


def _to_sublane_major(pm, v):
    return jnp.dot(pm, v, preferred_element_type=F32).astype(MM)


def _to_time_order(pmt, v):
    hi = v.astype(MM)
    r1 = v - hi.astype(F32)
    mid = r1.astype(MM)
    lo = (r1 - mid.astype(F32)).astype(MM)
    dot = functools.partial(jnp.dot, preferred_element_type=F32)
    return dot(pmt, hi) + dot(pmt, mid) + dot(pmt, lo)


def _step_back(vreg_rows, before):
    sub = lax.broadcasted_iota(jnp.int32, vreg_rows.shape, 0)
    return jnp.where(sub == 0, before[7:8, :], pltpu.roll(vreg_rows, 1, 0))


def _step_ahead(vreg_rows, after):
    sub = lax.broadcasted_iota(jnp.int32, vreg_rows.shape, 0)
    return jnp.where(sub == 7, after[0:1, :], pltpu.roll(vreg_rows, 7, 0))


def _fill_back_rows(ext_ref, before, tm, cols):
    last = ext_ref[pl.ds(SHIFT_ROWS + tm - 8, 8), cols]
    pen = ext_ref[pl.ds(SHIFT_ROWS + tm - 16, 8), cols]
    ext_ref[pl.ds(8, 8), cols] = _step_back(last, before[8:16, :])
    ext_ref[pl.ds(0, 8), cols] = _step_back(pen, before[0:8, :])


def _conv_glu(ext_ref, cw_ref, cb_ref, tm, c):
    out = []
    for base in (c * FF_CHUNK, D_FF + c * FF_CHUNK):
        cols = slice(base, base + FF_CHUNK)
        y = cb_ref[:, cols] + cw_ref[0:1, cols] * ext_ref[pl.ds(0, tm), cols]
        y = y + cw_ref[1:2, cols] * ext_ref[pl.ds(8, tm), cols]
        y = y + cw_ref[2:3, cols] * ext_ref[pl.ds(SHIFT_ROWS, tm), cols]
        out.append(y)
    return out


def _ffn_fwd(x1, ffn_norm, wup_t, conv_w, conv_b, wdown, target, tm):
    s = x1.shape[0]
    inv_d = 1.0 / D_MODEL
    pm, pmt = _sublane_major_matrices(tm)

    def body(x1_ref, gn_ref, wu_ref, cw_ref, cb_ref, wd_ref, tgt_ref, pm_ref, pmt_ref, h2_ref, u_ref, dy_ref, dyb_ref,
             loss_ref, ext_ref, carry_ref):
        i = pl.program_id(0)

        @pl.when(i == 0)
        def _():
            carry_ref[...] = jnp.zeros_like(carry_ref)
            loss_ref[...] = jnp.zeros_like(loss_ref)

        x1 = x1_ref[...]
        h2, _ = _rmsnorm_fwd(x1, gn_ref[...])
        h2 = _to_sublane_major(pm_ref[...], h2.astype(MM))
        h2_ref[...] = h2
        for c in range(4):
            cols = slice(c * FF_CHUNK, (c + 1) * FF_CHUNK)
            uc = _mm_nt(h2, wu_ref[cols, :])
            u_ref[:, cols] = uc
            ext_ref[pl.ds(SHIFT_ROWS, tm), cols] = uc
            _fill_back_rows(ext_ref, carry_ref[:, cols], tm, cols)
            carry_ref[:, cols] = uc[tm - SHIFT_ROWS:tm, :]
        down = jnp.zeros((tm, D_MODEL), F32)
        for c in range(2):
            gate, val = _conv_glu(ext_ref, cw_ref, cb_ref, tm, c)
            act = gate * jax.nn.sigmoid(gate) * val
            down = down + _mm(act, wd_ref[c * FF_CHUNK:(c + 1) * FF_CHUNK, :])
        err = x1 + _to_time_order(pmt_ref[...], down) - tgt_ref[...]
        loss_ref[...] += jnp.full(loss_ref.shape, 0.5 * inv_d * jnp.sum(err * err), F32)
        dy = err * inv_d
        dy_ref[...] = dy
        dyb_ref[...] = _to_sublane_major(pm_ref[...], dy.astype(MM))

    sd = jax.ShapeDtypeStruct
    return pl.pallas_call(
        body, name="ffn_fwd", grid=(s // tm,),
        out_shape=(sd((s, D_MODEL), MM), sd((s, 2 * D_FF), F32), sd((s, D_MODEL), F32), sd((s, D_MODEL), MM),
                   sd((8, LANES), F32)),
        in_specs=[_rows(tm, 1024), _resident((1, 1024)), _resident((2 * D_FF, D_MODEL)), _resident((3, 2 * D_FF)),
                  _resident((1, 2 * D_FF)), _resident((D_FF, D_MODEL)), _rows(tm, 1024), _resident((tm, tm)),
                  _resident((tm, tm))],
        out_specs=(_rows(tm, 1024), _rows(tm, 2 * D_FF), _rows(tm, 1024), _rows(tm, 1024),
                   pl.BlockSpec((8, LANES), lambda i: (0, 0))),
        scratch_shapes=[pltpu.VMEM((SHIFT_ROWS + tm, 2 * D_FF), F32), pltpu.VMEM((SHIFT_ROWS, 2 * D_FF), F32)],
        compiler_params=_params("arbitrary"),
    )(x1, ffn_norm, wup_t, conv_w, conv_b, wdown, target, pm, pmt)


def _ffn_bwd_a(dyb, u, conv_w, conv_b, wdown, tm):
    s = dyb.shape[0]
    nt = s // tm
    hb = tm // SHIFT_ROWS
    rev = lambda i: (nt - 1 - i, 0)

    def body(dy_ref, u_ref, before_ref, cw_ref, cb_ref, wd_ref, du_ref, act_ref, dcw_ref, dcb_ref, ext_ref, extd_ref,
             ahead_ref):
        i = pl.program_id(0)
        first_tile = i == nt - 1

        @pl.when(i == 0)
        def _():
            ahead_ref[...] = jnp.zeros_like(ahead_ref)
            dcw_ref[...] = jnp.zeros_like(dcw_ref)
            dcb_ref[...] = jnp.zeros_like(dcb_ref)

        ext_ref[pl.ds(SHIFT_ROWS, tm), :] = u_ref[...]
        for c in range(4):
            cols = slice(c * FF_CHUNK, (c + 1) * FF_CHUNK)
            _fill_back_rows(ext_ref, jnp.where(first_tile, 0.0, before_ref[:, cols]), tm, cols)
        dy = dy_ref[...]
        for c in range(2):
            gate, val = _conv_glu(ext_ref, cw_ref, cb_ref, tm, c)
            sg = jax.nn.sigmoid(gate)
            sl = gate * sg
            act_ref[:, c * FF_CHUNK:(c + 1) * FF_CHUNK] = (sl * val).astype(MM)
            d_act = _mm_nt(dy, wd_ref[c * FF_CHUNK:(c + 1) * FF_CHUNK, :])
            extd_ref[pl.ds(0, tm), c * FF_CHUNK:(c + 1) * FF_CHUNK] = d_act * val * (sg * (1.0 + gate * (1.0 - sg)))
            extd_ref[pl.ds(0, tm), D_FF + c * FF_CHUNK:D_FF + (c + 1) * FF_CHUNK] = d_act * sl
        for c in range(4):
            cols = slice(c * FF_CHUNK, (c + 1) * FF_CHUNK)
            ahead = ahead_ref[:, cols]
            first2 = extd_ref[pl.ds(0, SHIFT_ROWS), cols]
            extd_ref[pl.ds(tm, 8), cols] = _step_ahead(first2[0:8, :], ahead[0:8, :])
            extd_ref[pl.ds(tm + 8, 8), cols] = _step_ahead(first2[8:16, :], ahead[8:16, :])
            ahead_ref[:, cols] = first2
            d0 = extd_ref[pl.ds(0, tm), cols]
            dcb_ref[:, cols] += jnp.sum(d0, axis=0, keepdims=True)
            for j in range(3):
                dcw_ref[j:j + 1, cols] += jnp.sum(d0 * ext_ref[pl.ds(8 * j, tm), cols], axis=0, keepdims=True)
            du = cw_ref[2:3, cols] * d0 + cw_ref[1:2, cols] * extd_ref[pl.ds(8, tm), cols]
            du = du + cw_ref[0:1, cols] * extd_ref[pl.ds(SHIFT_ROWS, tm), cols]
            du_ref[:, cols] = du.astype(MM)

    sd = jax.ShapeDtypeStruct
    return pl.pallas_call(
        body, name="ffn_bwd_a", grid=(nt,),
        out_shape=(sd((s, 2 * D_FF), MM), sd((s, D_FF), MM), sd((3, 2 * D_FF), F32), sd((1, 2 * D_FF), F32)),
        in_specs=[pl.BlockSpec((tm, D_MODEL), rev), pl.BlockSpec((tm, 2 * D_FF), rev),
                  pl.BlockSpec((SHIFT_ROWS, 2 * D_FF), lambda i: (jnp.maximum((nt - 1 - i) * hb - 1, 0), 0)),
                  _resident((3, 2 * D_FF)), _resident((1, 2 * D_FF)), _resident((D_FF, D_MODEL))],
        out_specs=(pl.BlockSpec((tm, 2 * D_FF), rev), pl.BlockSpec((tm, D_FF), rev),
                   pl.BlockSpec((3, 2 * D_FF), lambda i: (0, 0)), pl.BlockSpec((1, 2 * D_FF), lambda i: (0, 0))),
        scratch_shapes=[pltpu.VMEM((SHIFT_ROWS + tm, 2 * D_FF), F32), pltpu.VMEM((tm + SHIFT_ROWS, 2 * D_FF), F32),
                        pltpu.VMEM((SHIFT_ROWS, 2 * D_FF), F32)],
        compiler_params=_params("arbitrary"),
    )(dyb, u, u, conv_w, conv_b, wdown)


def _after(after):
    tie = [] if after is None else list(after) if isinstance(after, (list, tuple)) else [after]
    return tie, [pl.BlockSpec(memory_space=pl.ANY)] * len(tie)


def _matmul_tn(a, b, tmo, tk, name, into=None, row0=0):
    s, m = a.shape
    n = b.shape[1]
    nk = s // tk
    rows = m if into is None else into.shape[0]
    assert row0 % LANES == 0 and tmo % LANES == 0
    grown, grown_spec = ([], []) if into is None else ([into], [ANY])

    def body(a_ref, b_ref, *rest):
        o_ref = rest[-1]
        k = pl.program_id(1)

        @pl.when(k == 0)
        def _():
            o_ref[...] = jnp.zeros_like(o_ref)

        o_ref[...] += _mm_tn(a_ref[...], b_ref[pl.ds(pl.multiple_of(k * tk, tk), tk), :])

    return pl.pallas_call(
        body, name=name, grid=(m // tmo, nk), out_shape=jax.ShapeDtypeStruct((rows, n), F32),
        in_specs=[pl.BlockSpec((tk, tmo), lambda i, k: (k, i)), _resident((s, n))] + grown_spec,
        out_specs=pl.BlockSpec((pl.Element(tmo), pl.Element(n)), lambda i, k: (pl.multiple_of(row0 + i * tmo, LANES), 0)),
        input_output_aliases={2: 0} if grown else {},
        compiler_params=_params("parallel", "arbitrary"),
    )(a, b, *grown)


def _ffn_bwd_b(du, wup_t, x1, ffn_norm, dy, tm):
    s = du.shape[0]

    def body(du_ref, wu_ref, x1_ref, gn_ref, dy_ref, pmt_ref, dx1_ref, dx1b_ref, dg_ref):
        @pl.when(pl.program_id(0) == 0)
        def _():
            dg_ref[...] = jnp.zeros_like(dg_ref)

        dh2 = _to_time_order(pmt_ref[...], _mm(du_ref[...], wu_ref[...]))
        x1 = x1_ref[...]
        _, r = _rmsnorm_fwd(x1, gn_ref[...])
        dx, dgr = _rmsnorm_bwd(x1, r, gn_ref[...], dh2)
        dg_ref[...] += jnp.sum(dgr, axis=0, keepdims=True)
        dx1 = dy_ref[...] + dx
        dx1_ref[...] = dx1
        dx1b_ref[...] = dx1.astype(MM)

    return pl.pallas_call(
        body, name="ffn_bwd_b", grid=(s // tm,),
        out_shape=(jax.ShapeDtypeStruct((s, D_MODEL), F32), jax.ShapeDtypeStruct((s, D_MODEL), MM),
                   jax.ShapeDtypeStruct((1, D_MODEL), F32)),
        in_specs=[_rows(tm, 2 * D_FF), _resident((2 * D_FF, D_MODEL)), _rows(tm, 1024), _resident((1, 1024)),
                  _rows(tm, 1024), _resident((tm, tm))],
        out_specs=(_rows(tm, 1024), _rows(tm, 1024), pl.BlockSpec((1, D_MODEL), lambda i: (0, 0))),
        compiler_params=_params("arbitrary"),
    )(du, wup_t, x1, ffn_norm, dy, _sublane_major_matrices(tm)[1])


def _win_rows(row0, rows):
    return pl.BlockSpec((pl.Element(rows), pl.Element(D_MODEL)), lambda i: (row0, 0), pipeline_mode=pl.Buffered(1))


def _mix_bwd(dx1b, wout, g, a, b, mix, h, tm, after=None):
    s = dx1b.shape[0]
    tie, tie_spec = _after(after)

    def body(dx_ref, w_ref, g_ref, a_ref, b_ref, mix_ref, h_ref, *rest):
        da_ref, db_ref, dzg_ref, dbg_ref, dwo_ref, dwin_ref = rest[-6:]

        @pl.when(pl.program_id(0) == 0)
        def _():
            dbg_ref[...] = jnp.zeros_like(dbg_ref)
            dwo_ref[...] = jnp.zeros_like(dwo_ref)
            dwin_ref[...] = jnp.zeros_like(dwin_ref)

        dx = dx_ref[...]
        dwo_ref[...] += _mm_tn(mix_ref[...], dx)
        dmix = _mm_nt(dx, w_ref[...])
        for half, src, dst in ((0, a_ref, da_ref), (1, b_ref, db_ref)):
            cols = slice(half * 1024, (half + 1) * 1024)
            gt = g_ref[:, cols]
            dst[...] = (dmix * gt).astype(dst.dtype)
            dz = dmix * src[...] * gt * (1.0 - gt)
            dzb = dz.astype(MM)
            dzg_ref[:, cols] = dzb
            dwin_ref[cols, :] += _mm_tn(dzb, h_ref[...])
            dbg_ref[:, cols] += jnp.sum(dz, axis=0, keepdims=True)

    sd = jax.ShapeDtypeStruct
    return pl.pallas_call(
        body, name="mix_bwd", grid=(s // tm,),
        out_shape=(sd((s, 1024), F32), sd((s, 1024), MM), sd((s, 2048), MM), sd((1, 2048), F32),
                   sd((D_MODEL, D_MODEL), F32), sd((IN_WIDTH, D_MODEL), F32)),
        in_specs=[_rows(tm, 1024), _resident((1024, 1024)), _rows(tm, 2048), _rows(tm, 1024), _rows(tm, 1024),
                  _rows(tm, 1024), _rows(tm, 1024)] + tie_spec,
        out_specs=(_rows(tm, 1024), _rows(tm, 1024), _rows(tm, 2048), pl.BlockSpec((1, 2048), lambda i: (0, 0)),
                   _resident((D_MODEL, D_MODEL)), _win_rows(O_G, IN_WIDTH - O_G)),
        compiler_params=_params("arbitrary"),
    )(dx1b, wout, g, a, b, mix, h, *tie)


def _pool_bwd(u, da, wpool, pool_scale, h, d_win_t, tm, after=None):
    s = u.shape[0]
    nt = s // tm
    hb = tm // POOL_HALO

    tie, tie_spec = _after(after)

    def body(u_ref, uh_ref, da_ref, dah_ref, wp_ref, ps_ref, h_ref, *rest):
        dzu_ref, dwp_ref, dps_ref, dwin_ref, ext_ref, exte_ref = rest[-6:]
        i = pl.program_id(0)

        @pl.when(i == 0)
        def _():
            dwp_ref[...] = jnp.zeros_like(dwp_ref)
            dps_ref[...] = jnp.zeros_like(dps_ref)
            dwin_ref[...] = jnp.zeros_like(dwin_ref)

        ext_ref[pl.ds(0, POOL_HALO), :] = jnp.where(i > 0, uh_ref[...], 0.0)
        ext_ref[pl.ds(POOL_HALO, tm), :] = u_ref[...]
        pooled = _pooled(ext_ref, tm, i * tm)
        da = da_ref[...]
        dah = jnp.where(i < nt - 1, dah_ref[...], 0.0)
        t = (i * tm + lax.broadcasted_iota(jnp.int32, (tm + POOL_HALO, 1), 0)).astype(F32)
        for gi, w in enumerate(POOL_WINDOWS):
            cols = slice(gi * POOL_GROUP, (gi + 1) * POOL_GROUP)
            pg = pooled[gi].astype(MM)
            wg = wp_ref[gi]
            mixed = _mm(pg, wg)
            dps_ref[:, cols] += jnp.sum(da[:, cols] * mixed, axis=0, keepdims=True)
            dmx = (da[:, cols] * ps_ref[:, cols]).astype(MM)
            dwp_ref[gi] += _mm_tn(pg, dmx)
            dpl = _mm_nt(dmx, wg)
            dplh = _mm_nt(dah[:, cols] * ps_ref[:, cols], wg)
            cnt = jnp.minimum(t + 1.0, float(w))
            exte_ref[pl.ds(0, tm), cols] = dpl / cnt[0:tm]
            exte_ref[pl.ds(tm, POOL_HALO), cols] = dplh / cnt[tm:tm + POOL_HALO]
            acc = exte_ref[pl.ds(0, tm), cols]
            for k in range(1, w):
                acc = acc + exte_ref[pl.ds(k, tm), cols]
            dzu = (acc - dpl).astype(MM)
            dzu_ref[:, cols] = dzu
            dwin_ref[cols, :] += _mm_tn(dzu, h_ref[...])

    sd = jax.ShapeDtypeStruct
    last_halo = s // POOL_HALO - 1
    return pl.pallas_call(
        body, name="pool_bwd", grid=(nt,),
        out_shape=(sd((s, 1024), MM), sd((4, POOL_GROUP, POOL_GROUP), F32), sd((1, 1024), F32),
                   sd((IN_WIDTH, D_MODEL), F32)),
        in_specs=[_rows(tm, 1024), pl.BlockSpec((POOL_HALO, 1024), lambda i: (jnp.maximum(i * hb - 1, 0), 0)),
                  _rows(tm, 1024),
                  pl.BlockSpec((POOL_HALO, 1024), lambda i: (jnp.minimum((i + 1) * hb, last_halo), 0)),
                  _resident((4, POOL_GROUP, POOL_GROUP)), _resident((1, 1024)), _rows(tm, 1024)] + tie_spec + [ANY],
        out_specs=(_rows(tm, 1024), pl.BlockSpec((4, POOL_GROUP, POOL_GROUP), lambda i: (0, 0, 0)),
                   pl.BlockSpec((1, 1024), lambda i: (0, 0)), _win_rows(O_U, O_Q - O_U)),
        input_output_aliases={7 + len(tie): 3},
        scratch_shapes=[pltpu.VMEM((POOL_HALO + tm, 1024), F32), pltpu.VMEM((tm + POOL_HALO, 1024), F32)],
        compiler_params=_params("arbitrary"),
    )(u, u, da, da, wpool, pool_scale, h, *tie, d_win_t)


def _attn_bwd(q, kv, db, tabs, q_norm_t, k_norm_t, probs, sink_probs, bd, after=None):
    s = q.shape[0]
    nb = s // BLOCK
    scale = HEAD_DIM ** -0.5
    cur = lambda n: (jnp.minimum(n, nb - 1), 0)
    prv = lambda n: (jnp.maximum(n - 1, 0), 0)
    tie, tie_spec = _after(after)

    def body(q_ref, kvc_ref, kvp_ref, db_ref, c_ref, s1_ref, s2_ref, cp_ref, s1p_ref, s2p_ref, qn_ref, kn_ref,
             pr_ref, ps_ref, bd_ref, *rest):
        (dzq_ref, dzkv_ref, dqn_ref, dkn_ref, dsk_ref,
         carry_ref, tot_ref, dqr_ref, qacc_ref, kacc_ref, sacc_ref) = rest[-11:]
        n = pl.program_id(0)
        bdm = bd_ref[...]
        kn = kn_ref[...]

        @pl.when(n == 0)
        def _():
            carry_ref[...] = jnp.zeros_like(carry_ref)
            qacc_ref[...] = jnp.zeros_like(qacc_ref)
            kacc_ref[...] = jnp.zeros_like(kacc_ref)
            sacc_ref[...] = jnp.zeros_like(sacc_ref)

        kp_raw = kvp_ref[:, 0:128]
        kph, rp = _head_norm_fwd(kp_raw, kn, bdm)
        cp, s1p, s2p = cp_ref[...], s1p_ref[...], s2p_ref[...]

        @pl.when(n < nb)
        def _():
            c, s1, s2 = c_ref[...], s1_ref[...], s2_ref[...]
            c8, s18, s28 = _tile_lanes(c, 8), _tile_lanes(s1, 8), _tile_lanes(s2, 8)
            q_raw = q_ref[...]
            qh, rq = _head_norm_fwd(q_raw, qn_ref[...], bdm)
            qr = (_rope(qh, c8, s18, s28) * scale).astype(MM)
            kc, _ = _head_norm_fwd(kvc_ref[:, 0:128], kn, bdm)
            k2 = jnp.concatenate([_rope(kph, cp, s1p, s2p), _rope(kc, c, s1, s2)], axis=0)
            v2 = jnp.concatenate([kvp_ref[:, 128:256], kvc_ref[:, 128:256]], axis=0)
            dob = db_ref[...].astype(MM)
            mask = _fold_masks(n)
            lane = lax.broadcasted_iota(jnp.int32, (1, LANES), 1)
            dsk = jnp.zeros((1, LANES), F32)
            dk2 = jnp.zeros((2 * BLOCK, LANES), F32)
            dv2 = jnp.zeros((2 * BLOCK, LANES), F32)
            for hk in range(N_KV_HEADS):
                qs = _stack_pairs(qr, hk)
                do = _stack_pairs(dob, hk)
                dqt = jnp.zeros((LANES, PAIR_COLS), F32)
                dkb, dvb = [], []
                for par, (kb, vb) in enumerate(zip(_parity_bands(k2, hk), _parity_bands(v2, hk))):
                    dp = _mm_nt(vb, do)
                    prs, dss = [], []
                    for j in range(PAIRS):
                        cols = slice(j * BLOCK, (j + 1) * BLOCK)
                        pr = pr_ref[0, 2 * hk + par, :, cols].astype(F32)
                        psink = ps_ref[0, 2 * hk + par, :, cols]
                        dpj = _fold(dp[:, cols], mask[0])
                        coldot = jnp.sum(pr * dpj, axis=0, keepdims=True)
                        dss.append(_unfold(pr * (dpj - coldot), mask[0]))
                        prs.append(_unfold(pr, mask[0]))
                        h = hk * GQA_GROUP + 2 * j + par
                        dsk = dsk + jnp.where(lane == h, jnp.sum(-psink * coldot), 0.0)
                    ds, pr = jnp.concatenate(dss, axis=1), jnp.concatenate(prs, axis=1)
                    dqt = dqt + _mm(kb.T, ds)
                    dkb.append(_mm(ds, qs))
                    dvb.append(_mm(pr, do))
                for j in range(PAIRS):
                    col = (PAIRS * hk + j) * LANES
                    dqr_ref[:, col:col + LANES] = dqt[:, j * BLOCK:(j + 1) * BLOCK].T
                dk2 = dk2 + _fold_parity(dkb[0], dkb[1], hk)
                dv2 = dv2 + _fold_parity(dvb[0], dvb[1], hk)
            tot_ref[:, 0:128] = carry_ref[:, 0:128] + dk2[0:BLOCK, :]
            tot_ref[:, 128:256] = carry_ref[:, 128:256] + dv2[0:BLOCK, :]
            carry_ref[:, 0:128] = dk2[BLOCK:2 * BLOCK, :]
            carry_ref[:, 128:256] = dv2[BLOCK:2 * BLOCK, :]
            sacc_ref[...] += dsk
            dqh = _rope_bwd(dqr_ref[...] * scale, c8, s18, s28)
            dq, dgq = _head_norm_bwd(q_raw, rq, qn_ref[...], dqh, bdm)
            dzq_ref[...] = dq.astype(MM)
            qacc_ref[...] += jnp.sum(dgq, axis=0, keepdims=True)

        @pl.when(n == nb)
        def _():
            tot_ref[...] = carry_ref[...]

        dkh = _rope_bwd(tot_ref[:, 0:128], cp, s1p, s2p)
        dkr, dgk = _head_norm_bwd(kp_raw, rp, kn, dkh, bdm)
        dzkv_ref[:, 0:128] = dkr.astype(MM)
        dzkv_ref[:, 128:256] = tot_ref[:, 128:256].astype(MM)
        kacc_ref[...] += jnp.where(n > 0, jnp.sum(dgk, axis=0, keepdims=True), 0.0)

        @pl.when(n == nb)
        def _():
            fold = qacc_ref[:, 0:HEAD_DIM]
            for h in range(1, N_Q_HEADS):
                fold = fold + qacc_ref[:, h * HEAD_DIM:(h + 1) * HEAD_DIM]
            dqn_ref[...] = fold
            dkn_ref[...] = kacc_ref[:, 0:HEAD_DIM] + kacc_ref[:, HEAD_DIM:2 * HEAD_DIM]
            dsk_ref[...] = sacc_ref[...]

    tab = lambda im: pl.BlockSpec((BLOCK, LANES), im)
    sd = jax.ShapeDtypeStruct
    const = lambda n: (0, 0)
    return pl.pallas_call(
        body, name="attn_bwd", grid=(nb + 1,),
        out_shape=(sd((s, 1024), MM), sd((s, 256), MM), sd((1, HEAD_DIM), F32), sd((1, HEAD_DIM), F32),
                   sd((1, LANES), F32)),
        in_specs=[pl.BlockSpec((BLOCK, 1024), cur), pl.BlockSpec((BLOCK, 256), cur), pl.BlockSpec((BLOCK, 256), prv),
                  pl.BlockSpec((BLOCK, 1024), cur), tab(cur), tab(cur), tab(cur), tab(prv), tab(prv), tab(prv),
                  _resident((1, 1024)), _resident((1, 128)),
                  pl.BlockSpec((1, 4, BLOCK, PAIR_COLS), lambda n: (jnp.minimum(n, nb - 1), 0, 0, 0)),
                  pl.BlockSpec((1, 4, 1, PAIR_COLS), lambda n: (jnp.minimum(n, nb - 1), 0, 0, 0)),
                  _resident((LANES, LANES))] + tie_spec,
        out_specs=(pl.BlockSpec((BLOCK, 1024), cur), pl.BlockSpec((BLOCK, 256), prv),
                   pl.BlockSpec((1, HEAD_DIM), const), pl.BlockSpec((1, HEAD_DIM), const),
                   pl.BlockSpec((1, LANES), const)),
        scratch_shapes=[pltpu.VMEM((BLOCK, 256), F32), pltpu.VMEM((BLOCK, 256), F32), pltpu.VMEM((BLOCK, 1024), F32),
                        pltpu.VMEM((1, 1024), F32), pltpu.VMEM((1, 128), F32), pltpu.VMEM((1, LANES), F32)],
        compiler_params=_params("arbitrary"),
    )(q, kv, kv, db, *tabs, *tabs, q_norm_t, k_norm_t, probs, sink_probs, bd, *tie)


def _inproj_bwd(dzu, dzq, dzkv, dzg, win_t, x, attn_norm, dx1, tm, after=None):
    s = x.shape[0]
    tie, tie_spec = _after(after)

    def body(du_ref, dq_ref, dkv_ref, dg_ref, w_ref, x_ref, gn_ref, dx1_ref, *rest):
        gx_ref, dgn_ref = rest[-2:]

        @pl.when(pl.program_id(0) == 0)
        def _():
            dgn_ref[...] = jnp.zeros_like(dgn_ref)

        dh = _mm(du_ref[...], w_ref[O_U:O_Q, :]) + _mm(dq_ref[...], w_ref[O_Q:O_KV, :])
        dh = dh + _mm(dkv_ref[...], w_ref[O_KV:O_G, :]) + _mm(dg_ref[...], w_ref[O_G:IN_WIDTH, :])
        x = x_ref[...]
        _, r = _rmsnorm_fwd(x, gn_ref[...])
        dx, dgr = _rmsnorm_bwd(x, r, gn_ref[...], dh)
        dgn_ref[...] += jnp.sum(dgr, axis=0, keepdims=True)
        gx_ref[...] = dx1_ref[...] + dx

    return pl.pallas_call(
        body, name="inproj_bwd", grid=(s // tm,),
        out_shape=(jax.ShapeDtypeStruct((s, D_MODEL), F32), jax.ShapeDtypeStruct((1, D_MODEL), F32)),
        in_specs=[_rows(tm, 1024), _rows(tm, 1024), _rows(tm, 256), _rows(tm, 2048),
                  _resident((IN_WIDTH, D_MODEL)), _rows(tm, 1024), _resident((1, 1024)), _rows(tm, 1024)] + tie_spec,
        out_specs=(_rows(tm, 1024), pl.BlockSpec((1, D_MODEL), lambda i: (0, 0))),
        compiler_params=_params("arbitrary"),
    )(dzu, dzq, dzkv, dzg, win_t, x, attn_norm, dx1, *tie)


def _attention_constants(q_norm, k_norm, sinks):
    inv_freq = np.float32(ROPE_THETA) ** (-np.arange(0, ROPE_DIM, 2, dtype=np.float32) / np.float32(ROPE_DIM))
    lane = np.arange(LANES) % HEAD_DIM
    invf = jnp.asarray(np.where(lane < ROPE_DIM, inv_freq[lane % (ROPE_DIM // 2)], 0.0).reshape(1, LANES), F32)
    bd = jnp.asarray(np.arange(LANES)[:, None] // HEAD_DIM == np.arange(LANES)[None, :] // HEAD_DIM, MM)
    q_norm_t = jnp.tile(q_norm, (1, N_Q_HEADS))
    k_norm_t = jnp.tile(k_norm, (1, N_KV_HEADS))
    sink_rows = jnp.repeat(sinks.reshape(N_KV_HEADS, PAIRS, 2).transpose(0, 2, 1), BLOCK, axis=2)
    sink_rows = sink_rows.reshape(N_KV_HEADS, 2, 1, PAIR_COLS)
    return invf, bd, q_norm_t, k_norm_t, sink_rows


ANY = pl.BlockSpec(memory_space=pl.ANY)


def _position():
    return lax.axis_index("x"), lax.axis_index("y"), lax.axis_index("c")


def _all_gather(shards):
    k = len(shards)

    def body(*refs):
        ins, outs = refs[:k], refs[k:2 * k]
        send_sems, recv_sems, local_sems = refs[2 * k:]
        x, y, c = _position()
        me, sibling = (x, y, c), (x, y, 1 - c)
        chips = [(1 - x, y), (x, 1 - y), (1 - x, 1 - y)]

        def copy(a, kk, block, to, src=None):
            dst = outs[a].at[4 * block[0] + 2 * block[1] + block[2]]
            return pltpu.make_async_remote_copy(
                src_ref=dst if src is None else src, dst_ref=dst, send_sem=send_sems.at[a * 7 + kk],
                recv_sem=recv_sems.at[a * 7 + kk], device_id=to, device_id_type=MESH)

        mine = [pltpu.make_async_copy(ins[a], outs[a].at[4 * x + 2 * y + c], local_sems.at[a]) for a in range(k)]
        for cp in mine:
            cp.start()
        first = []
        for a in range(k):
            first.append(copy(a, 0, me, sibling, src=ins[a]))
            first += [copy(a, 1 + j, me, (*chip, c), src=ins[a]) for j, chip in enumerate(chips)]
        for cp in first:
            cp.start()
        passed = []
        for j, chip in enumerate(chips):
            for a in range(k):
                copy(a, 1 + j, (*chip, c), me).wait_recv()
                cp = copy(a, 4 + j, (*chip, c), sibling)
                cp.start()
                passed.append(cp)
        for a in range(k):
            copy(a, 0, sibling, me).wait_recv()
            for j, chip in enumerate(chips):
                copy(a, 4 + j, (*chip, 1 - c), me).wait_recv()
        for cp in first + passed:
            cp.wait_send()
        for cp in mine:
            cp.wait()

    return pl.pallas_call(
        body, name="all_gather_weights",
        out_shape=tuple(jax.ShapeDtypeStruct((N_DEV,) + s.shape, s.dtype) for s in shards),
        in_specs=[ANY] * k, out_specs=(ANY,) * k,
        scratch_shapes=[pltpu.SemaphoreType.DMA((7 * k,)), pltpu.SemaphoreType.DMA((7 * k,)),
                        pltpu.SemaphoreType.DMA((k,))],
    )(*shards)


HBM = pl.BlockSpec(memory_space=pltpu.HBM)
SEM = pl.BlockSpec(memory_space=pltpu.SEMAPHORE)
EFFECT = pltpu.SideEffectType.DATAFLOW_SIDE_EFFECTING


def _exchange_start(name, bufs, n_sems, copies, after=None):
    k = len(bufs)
    tie, tie_spec = _after(after)
    n_in = k + len(tie)

    def body(*refs):
        for cp in copies(refs[:k], refs[n_in], refs[n_in + 1]):
            cp.start()
        refs[-1][...] = jnp.zeros_like(refs[-1])

    dma = pltpu.SemaphoreType.DMA((n_sems,))
    out = pl.pallas_call(
        body, name=name,
        out_shape=(dma, dma, *[pltpu.HBM(b.shape, b.dtype) for b in bufs], jax.ShapeDtypeStruct((8, LANES), F32)),
        in_specs=[HBM] * k + tie_spec, out_specs=(SEM, SEM, *[HBM] * k, pl.BlockSpec(memory_space=pltpu.VMEM)),
        input_output_aliases={i: 2 + i for i in range(k)},
        compiler_params=pltpu.CompilerParams(has_side_effects=EFFECT),
    )(*[pltpu.with_memory_space_constraint(b, pltpu.HBM) for b in bufs], *tie)
    return out[0], out[1], list(out[2:2 + k]), out[-1]


def _exchange_mid(name, bufs, sems_in, n_sems, waits, copies, after):
    k, ns = len(bufs), len(sems_in)

    def body(*refs):
        ins = refs[:k]
        waits(ins, *refs[k:k + ns])
        for cp in copies(ins, refs[k + ns + 1], refs[k + ns + 2]):
            cp.start()

    dma = pltpu.SemaphoreType.DMA((n_sems,))
    out = pl.pallas_call(
        body, name=name, out_shape=(dma, dma, *[pltpu.HBM(b.shape, b.dtype) for b in bufs]),
        in_specs=[HBM] * k + [SEM] * ns + [ANY], out_specs=(SEM, SEM, *[HBM] * k),
        input_output_aliases={i: 2 + i for i in range(k)},
        compiler_params=pltpu.CompilerParams(has_side_effects=EFFECT),
    )(*bufs, *sems_in, after)
    return out[0], out[1], list(out[2:])


def _exchange_wait(name, bufs, sems, waits, after=None):
    k, ns = len(bufs), len(sems)
    tie, tie_spec = _after(after)

    def body(*refs):
        waits(refs[:k], *refs[k:k + ns])

    out = pl.pallas_call(
        body, name=name, out_shape=tuple(pltpu.HBM(b.shape, b.dtype) for b in bufs),
        in_specs=[HBM] * k + [SEM] * ns + tie_spec, out_specs=(HBM,) * k,
        input_output_aliases={i: i for i in range(k)},
        compiler_params=pltpu.CompilerParams(has_side_effects=EFFECT),
    )(*bufs, *sems, *tie)
    return list(out)


def _gather_copies(k, direct):
    def copies(refs, send_sems, recv_sems):
        x, y, c = _position()
        chips = [(1 - x, y), (x, 1 - y), (1 - x, 1 - y)]
        out = []
        for a in range(k):
            land = refs[k + a]
            if direct:
                mine = land.at[4 * x + 2 * y + c]
                for kk, to in enumerate([(x, y, 1 - c)] + [(*chip, c) for chip in chips]):
                    out.append(pltpu.make_async_remote_copy(
                        src_ref=refs[a], dst_ref=mine, send_sem=send_sems.at[4 * a + kk],
                        recv_sem=recv_sems.at[4 * a + kk], device_id=to, device_id_type=MESH))
            else:
                for j, (px, py) in enumerate(chips):
                    slot = land.at[4 * px + 2 * py + c]
                    out.append(pltpu.make_async_remote_copy(
                        src_ref=slot, dst_ref=slot, send_sem=send_sems.at[3 * a + j], recv_sem=recv_sems.at[3 * a + j],
                        device_id=(x, y, 1 - c), device_id_type=MESH))
        return out
    return copies


def _all_gather_behind(shards, start_after, mid_after):
    k = len(shards)
    me = 4 * lax.axis_index("x") + 2 * lax.axis_index("y") + lax.axis_index("c")
    lands = [lax.dynamic_update_slice(lax.empty((N_DEV,) + s.shape, s.dtype), s[None], (me, 0, 0)) for s in shards]
    direct, passed = _gather_copies(k, True), _gather_copies(k, False)

    send_a, recv_a, bufs, token = _exchange_start("gather_start", list(shards) + lands, 4 * k, direct, start_after)

    def finish():
        def wait_ici(refs, send_sems, recv_sems):
            for i, cp in enumerate(direct(refs, send_sems, recv_sems)):
                if i % 4:
                    cp.wait_recv()

        send_b, recv_b, bufs2 = _exchange_mid("gather_pass", bufs, [send_a, recv_a], 3 * k, wait_ici, passed,
                                              mid_after())

        def wait_all(refs, sa, ra, sb, rb):
            for i, cp in enumerate(direct(refs, sa, ra)):
                cp.wait_send()
                if i % 4 == 0:
                    cp.wait_recv()
            for cp in passed(refs, sb, rb):
                cp.wait()

        return _exchange_wait("gather_wait", bufs2, [send_a, recv_a, send_b, recv_b], wait_all)[k:]

    return token, finish


def _pair_copies(k):
    def copies(refs, send_sems, recv_sems):
        x, y, c = _position()
        return [pltpu.make_async_remote_copy(
            src_ref=refs[a].at[2 * ch + 1 - c], dst_ref=refs[k + a].at[ch], send_sem=send_sems.at[4 * a + ch],
            recv_sem=recv_sems.at[4 * a + ch], device_id=(x, y, 1 - c), device_id_type=MESH)
            for a in range(k) for ch in range(4)]
    return copies


def _chip_copies(k):
    def copies(refs, send_sems, recv_sems):
        x, y, c = _position()
        return [pltpu.make_async_remote_copy(
            src_ref=refs[a].at[2 * px + py], dst_ref=refs[k + a].at[rel], send_sem=send_sems.at[3 * a + rel],
            recv_sem=recv_sems.at[3 * a + rel], device_id=(px, py, c), device_id_type=MESH)
            for a in range(k) for rel, (px, py) in enumerate([(1 - x, y), (x, 1 - y), (1 - x, 1 - y)])]
    return copies


def _symmetric_exchange(name, srcs, n_land, copies_of):
    k = len(srcs)
    lands = [lax.empty((n_land,) + s.shape[1:], s.dtype) for s in srcs]
    copies = copies_of(k)
    send_sems, recv_sems, bufs, token = _exchange_start(name + "_start", list(srcs) + lands, n_land * k, copies)

    def finish(after):
        def wait_all(refs, ss, rs):
            for cp in copies(refs, ss, rs):
                cp.wait()

        done = _exchange_wait(name + "_wait", bufs, [send_sems, recv_sems], wait_all, after)
        return done[:k], done[k:]

    return token, finish


def _pair_add(full, recv, wire):
    _, r, c_ = full.shape
    core = lax.axis_index("c").astype(jnp.int32).reshape(1)

    def body(core_ref, f_ref, r_ref, pw_ref, own_ref):
        ch = pl.program_id(0)
        x, y, _ = _position()
        tot = f_ref[0, 0] + r_ref[0]
        pw_ref[0] = tot.astype(pw_ref.dtype)

        @pl.when(ch == 2 * x + y)
        def _():
            own_ref[...] = tot

    return pl.pallas_call(
        body, name="grad_pair_add",
        grid_spec=pltpu.PrefetchScalarGridSpec(
            num_scalar_prefetch=1, grid=(4,),
            in_specs=[pl.BlockSpec((1, 1, r, c_), lambda i, core_ref: (i, core_ref[0], 0, 0)),
                      pl.BlockSpec((1, r, c_), lambda i, core_ref: (i, 0, 0))],
            out_specs=(pl.BlockSpec((1, r, c_), lambda i, core_ref: (i, 0, 0)),
                       pl.BlockSpec((r, c_), lambda i, core_ref: (0, 0)))),
        out_shape=(jax.ShapeDtypeStruct((4, r, c_), wire), jax.ShapeDtypeStruct((r, c_), F32)),
        compiler_params=_params("arbitrary"),
    )(core, full.reshape(4, 2, r, c_), recv)


def _adamw_math(w, g, m, v):
    m = ADAM_B1 * m + (1.0 - ADAM_B1) * g
    v = ADAM_B2 * v + (1.0 - ADAM_B2) * (g * g)
    m_hat = m / (1.0 - ADAM_B1 ** ADAM_STEP)
    v_hat = v / (1.0 - ADAM_B2 ** ADAM_STEP)
    delta = -ADAM_LR * (m_hat / (jnp.sqrt(v_hat) + ADAM_EPS) + ADAM_WD * w)
    return delta, m, v


def _row_tile(r):
    for t in (256, 272, 176, 128):
        if r % t == 0 and r > t:
            return t
    return r


def _adamw(g_own, recv, w, m, v):
    r, c_ = w.shape
    t = _row_tile(r)
    blk = pl.BlockSpec((t, c_), lambda i: (i, 0))

    def body(g_ref, r_ref, w_ref, m_ref, v_ref, go_ref, d_ref, mo_ref, vo_ref):
        g = g_ref[...]
        for i in range(3):
            g = g + r_ref[i].astype(F32)
        go_ref[...] = g
        d_ref[...], mo_ref[...], vo_ref[...] = _adamw_math(w_ref[...], g, m_ref[...], v_ref[...])

    return pl.pallas_call(
        body, name="adamw", grid=(r // t,), out_shape=(jax.ShapeDtypeStruct((r, c_), F32),) * 4,
        in_specs=[blk, pl.BlockSpec((3, t, c_), lambda i: (0, i, 0)), blk, blk, blk], out_specs=(blk,) * 4,
        compiler_params=_params("parallel"),
    )(g_own, recv, w, m, v)


SMALL = ("attn_norm", "b_gate", "pool_scale", "q_norm", "k_norm", "sinks", "ffn_norm", "conv_b")
SMALL_SIZES = (1024, 2048, 1024, 64, 64, 16, 1024, 5632)
SMALL_OFFSETS = tuple(sum(-(-s // LANES) * LANES for s in SMALL_SIZES[:i]) for i in range(len(SMALL_SIZES) + 1))
SMALL_WIDTH = SMALL_OFFSETS[-1] + LANES


def _pack_small(d, loss=None):
    parts = [jnp.pad(d[n].reshape(1, -1), ((0, 0), (0, -s % LANES))) for n, s in zip(SMALL, SMALL_SIZES)]
    last = jnp.zeros((1, LANES), F32) if loss is None else jnp.pad(loss.reshape(1, 1), ((0, 0), (0, LANES - 1)))
    return jnp.concatenate(parts + [last], axis=1)


def _small_allreduce(gp):
    def body(g_ref, sum_ref, slots_ref, send_sems, recv_sems):
        x, y, c = _position()
        me = 4 * x + 2 * y + c
        slots_ref[me] = g_ref[...]
        cps = []
        for rel in range(1, N_DEV):
            fx, fy, fc = (rel >> 2) & 1, (rel >> 1) & 1, rel & 1
            to = (1 - x if fx else x, 1 - y if fy else y, 1 - c if fc else c)
            cps.append(pltpu.make_async_remote_copy(
                src_ref=g_ref, dst_ref=slots_ref.at[me], send_sem=send_sems.at[rel - 1],
                recv_sem=recv_sems.at[rel - 1], device_id=to, device_id_type=MESH))
        for cp in cps:
            cp.start()
        for cp in cps:
            cp.wait()
        g = slots_ref[0]
        for i in range(1, N_DEV):
            g = g + slots_ref[i]
        sum_ref[...] = g

    vm = pl.BlockSpec(memory_space=pltpu.VMEM)
    return pl.pallas_call(
        body, name="small_allreduce", out_shape=jax.ShapeDtypeStruct((1, SMALL_WIDTH), F32),
        in_specs=[vm], out_specs=vm,
        scratch_shapes=[pltpu.VMEM((N_DEV, 1, SMALL_WIDTH), F32), pltpu.SemaphoreType.DMA((N_DEV - 1,)),
                        pltpu.SemaphoreType.DMA((N_DEV - 1,))],
    )(gp)


def _small_adamw(gsum, ws, ms, vs):
    n = len(SMALL)

    def body(g_ref, *rest):
        w_refs, m_refs, v_refs, outs = rest[:n], rest[n:2 * n], rest[2 * n:3 * n], rest[3 * n:]
        for j, size in enumerate(SMALL_SIZES):
            g = g_ref[:, SMALL_OFFSETS[j]:SMALL_OFFSETS[j] + size]
            results = (g,) + _adamw_math(w_refs[j][...], g, m_refs[j][...], v_refs[j][...])
            for kind, val in enumerate(results):
                outs[kind * n + j][...] = val
        outs[4 * n][...] = g_ref[:, SMALL_OFFSETS[-1]:SMALL_WIDTH]

    vm = pl.BlockSpec(memory_space=pltpu.VMEM)
    shapes = tuple(jax.ShapeDtypeStruct((1, s), F32) for s in SMALL_SIZES) * 4
    return pl.pallas_call(
        body, name="small_adamw", out_shape=shapes + (jax.ShapeDtypeStruct((1, LANES), F32),),
        in_specs=[vm] * (1 + 3 * n), out_specs=(vm,) * (4 * n + 1),
    )(gsum, *ws, *ms, *vs)


WEIGHTS = ("attn_norm", "w_in", "b_gate", "w_pool", "pool_scale", "q_norm", "k_norm", "sinks", "w_out", "ffn_norm",
           "w_up", "conv_w", "conv_b", "w_down")


def kernel(x, positions, attn_norm, w_in, b_gate, w_pool, pool_scale, q_norm, k_norm, sinks, w_out, ffn_norm, w_up, conv_w, conv_b, w_down, loss_target, m_attn_norm, m_w_in, m_b_gate, m_w_pool, m_pool_scale, m_q_norm, m_k_norm, m_sinks, m_w_out, m_ffn_norm, m_w_up, m_conv_w, m_conv_b, m_w_down, v_attn_norm, v_w_in, v_b_gate, v_w_pool, v_pool_scale, v_q_norm, v_k_norm, v_sinks, v_w_out, v_ffn_norm, v_w_up, v_conv_w, v_conv_b, v_w_down):
    w = dict(attn_norm=attn_norm, w_in=w_in, b_gate=b_gate, w_pool=w_pool, pool_scale=pool_scale, q_norm=q_norm,
             k_norm=k_norm, sinks=sinks, w_out=w_out, ffn_norm=ffn_norm, w_up=w_up, conv_w=conv_w, conv_b=conv_b,
             w_down=w_down)
    m = dict(attn_norm=m_attn_norm, w_in=m_w_in, b_gate=m_b_gate, w_pool=m_w_pool, pool_scale=m_pool_scale,
             q_norm=m_q_norm, k_norm=m_k_norm, sinks=m_sinks, w_out=m_w_out, ffn_norm=m_ffn_norm, w_up=m_w_up,
             conv_w=m_conv_w, conv_b=m_conv_b, w_down=m_w_down)
    v = dict(attn_norm=v_attn_norm, w_in=v_w_in, b_gate=v_b_gate, w_pool=v_w_pool, pool_scale=v_pool_scale,
             q_norm=v_q_norm, k_norm=v_k_norm, sinks=v_sinks, w_out=v_w_out, ffn_norm=v_ffn_norm, w_up=v_w_up,
             conv_w=v_conv_w, conv_b=v_conv_b, w_down=v_w_down)
    seq = x.shape[1]
    tm = 256
    tw = min(seq, 512)
    tk = min(seq, 1024)
    xs, target, pos_col = x[0], loss_target[0], positions.reshape(seq, 1)
    invf, bd, q_norm_t, k_norm_t, sink_rows = _attention_constants(q_norm, k_norm, sinks)
    out, done = {}, {}
    nat = {"w_in": (D_MODEL, 544), "w_pool": (128, POOL_GROUP), "w_out": (128, D_MODEL), "w_up": (D_MODEL, 704),
           "conv_w": (3, 704), "w_down": (352, D_MODEL)}

    def update(names, owns, recvs):
        for name, own, recv in zip(names, owns, recvs):
            w2, m2, v2 = (t[name].reshape(nat[name]) for t in (w, m, v))
            if name in ("w_in", "w_up"):
                res = _adamw(own, recv, w2.T, m2.T, v2.T)
                done[name] = res[1]
                res = [t.T for t in res]
            else:
                res = _adamw(own, recv, w2, m2, v2)
                done[name] = res[1]
            out[name] = [t.reshape(w[name].shape) for t in res]

    (g_win,) = _all_gather([w_in[0].T.astype(MM)])
    win_t = g_win.reshape(IN_WIDTH, D_MODEL)
    fwd = {}
    token, gather_rest = _all_gather_behind(
        [w_pool[0].astype(MM).reshape(128, POOL_GROUP), w_out[0].astype(MM), w_up[0].T.astype(MM), conv_w[0],
         w_down[0].astype(MM)], g_win, lambda: fwd["b"])

    tabs = _rope_tables(pos_col, invf)
    h, u, q, kv, g = _inproj_fwd(xs, attn_norm + token[0:1, 0:1], win_t, b_gate, tw)
    b, probs, sink_probs = _attn_fwd(q, kv, tabs, q_norm_t, k_norm_t, sink_rows, bd)
    fwd["b"] = b
    g_wpool, g_wout, g_wup, g_convw, g_wdown = gather_rest()
    wpool = g_wpool.reshape(N_DEV, 4, 32, POOL_GROUP).transpose(1, 0, 2, 3).reshape(4, POOL_GROUP, POOL_GROUP)
    wout = g_wout.reshape(D_MODEL, D_MODEL)
    wup_t = g_wup.reshape(2 * D_FF, D_MODEL)
    convw = g_convw.transpose(1, 0, 2).reshape(3, 2 * D_FF)
    wdown = g_wdown.reshape(D_FF, D_MODEL)
    a = _pool_fwd(u, wpool, pool_scale, tw)
    x1, mix = _mix_out_fwd(xs, g, a, b, wout, tw)
    h2, uff, dy, dyb, lossp = _ffn_fwd(x1, ffn_norm, wup_t, convw, conv_b, wdown, target, tm)

    du, act, d_conv_w, d_conv_b = _ffn_bwd_a(dyb, uff, convw, conv_b, wdown, tm)
    d_wdown = _matmul_tn(act, dyb, FF_CHUNK, tk, "dw_down")
    dx1, dx1b, d_ffn_norm = _ffn_bwd_b(du, wup_t, x1, ffn_norm, dy, tm)
    d_wup_t = _matmul_tn(du, h2, FF_CHUNK, tk, "dw_up")
    late = ("w_down", "w_up", "conv_w")
    late_wire = (WIRE, WIRE, F32)
    late_full = [d_wdown.reshape(N_DEV, 352, D_MODEL), d_wup_t.reshape(N_DEV, 704, D_MODEL),
                 d_conv_w.reshape(3, N_DEV, 704).transpose(1, 0, 2)]
    token, late_pair = _symmetric_exchange("late_pair", late_full, 4, _pair_copies)
    da, db, dzg, d_b_gate, d_wout, d_win_t = _mix_bwd(dx1b, wout, g, a, b, mix, h, tm, after=token)
    late_pw, late_own = zip(*[_pair_add(f, r, wd) for f, r, wd in zip(*late_pair(dzg), late_wire)])
    token, late_chip = _symmetric_exchange("late_chip", list(late_pw), 3, _chip_copies)
    dzu, d_wpool, d_pool_scale, d_win_t = _pool_bwd(u, da, wpool, pool_scale, h, d_win_t, tw, after=token)
    dzq, dzkv, d_q_norm, d_k_norm, d_sinks = _attn_bwd(q, kv, db, tabs, q_norm_t, k_norm_t, probs, sink_probs, bd,
                                                       after=token)
    d_win_t = _matmul_tn(dzq, h, 1024, tk, "dw_in_q", into=d_win_t, row0=O_Q)
    d_win_t = _matmul_tn(dzkv, h, 256, tk, "dw_in_kv", into=d_win_t, row0=O_KV)
    early = ("w_in", "w_pool", "w_out")
    early_full = [d_win_t.reshape(N_DEV, 544, D_MODEL),
                  d_wpool.reshape(4, N_DEV, 32, POOL_GROUP).transpose(1, 0, 2, 3).reshape(N_DEV, 128, POOL_GROUP),
                  d_wout.reshape(N_DEV, 128, D_MODEL)]
    token, early_pair = _symmetric_exchange("early_pair", early_full, 4, _pair_copies)
    update(late, late_own, late_chip(token)[1])
    early_pw, early_own = zip(*[_pair_add(f, r, WIRE) for f, r in zip(*early_pair([done[n] for n in late]))])
    token, early_chip = _symmetric_exchange("early_chip", list(early_pw), 3, _chip_copies)
    grad_x, d_attn_norm = _inproj_bwd(dzu, dzq, dzkv, dzg, win_t, xs, attn_norm, dx1, tw, after=token)
    gr = dict(attn_norm=d_attn_norm, b_gate=d_b_gate, pool_scale=d_pool_scale, q_norm=d_q_norm, k_norm=d_k_norm,
              sinks=d_sinks[:, 0:N_Q_HEADS], ffn_norm=d_ffn_norm, conv_b=d_conv_b)
    small = _small_adamw(_small_allreduce(_pack_small(gr, lossp[0, 0])), *[[t[n] for n in SMALL] for t in (w, m, v)])
    loss = small[-1][0, 0]
    for j, name in enumerate(SMALL):
        out[name] = [small[kind * len(SMALL) + j] for kind in range(4)]
    update(early, early_own, early_chip(small[0])[1])

    return (loss, grad_x[None], *[out[n][0] for n in WEIGHTS], *[out[n][1] for n in WEIGHTS],
            *[out[n][2] for n in WEIGHTS], *[out[n][3] for n in WEIGHTS])
```

```python
import functools

import numpy as np
import jax
import jax.numpy as jnp
from jax import lax
from jax.experimental import pallas as pl
from jax.experimental.pallas import tpu as pltpu

F32 = jnp.float32
MM = jnp.bfloat16
WIRE = jnp.bfloat16
ACT = jnp.bfloat16

D_MODEL = 1024
D_FF = 2816
HEAD_DIM = 64
N_Q_HEADS = 16
N_KV_HEADS = 2
GQA_GROUP = 8
BLOCK = 128
ROPE_DIM = 16
ROPE_THETA = 500000.0
POOL_WINDOWS = (2, 4, 8, 16)
POOL_GROUP = 256
POOL_HALO = 32
EPS = 1e-6
NEG = -1e30
O_U, O_Q, O_KV, O_G, IN_WIDTH = 0, 1024, 2048, 2304, 4352
FF_CHUNK = 1408

ADAM_LR, ADAM_B1, ADAM_B2, ADAM_EPS, ADAM_WD, ADAM_STEP = 0.001, 0.9, 0.999, 1e-08, 0.01, 10

N_DEV = 8
LANES = 128
VMEM_LIMIT_BYTES = 56 * 1024 * 1024
MESH = pl.DeviceIdType.MESH


def _params(*sem):
    return pltpu.CompilerParams(dimension_semantics=sem, vmem_limit_bytes=VMEM_LIMIT_BYTES)


def _resident(shape):
    nd = len(shape)
    return pl.BlockSpec(shape, lambda *_: (0,) * nd, pipeline_mode=pl.Buffered(1))


def _rows(tm, width):
    return pl.BlockSpec((tm, width), lambda i: (i, 0))


def _mm(a, b):
    return jnp.dot(a.astype(MM), b.astype(MM), preferred_element_type=F32)


def _mm_nt(a, b):
    return lax.dot_general(a.astype(MM), b.astype(MM), (((1,), (1,)), ((), ())), preferred_element_type=F32)


def _mm_tn(a, b):
    return lax.dot_general(a.astype(MM), b.astype(MM), (((0,), (0,)), ((), ())), preferred_element_type=F32)


def _rmsnorm_fwd(x, g):
    r = lax.rsqrt(jnp.mean(x * x, axis=-1, keepdims=True) + EPS)
    return x * r * g, r


def _rmsnorm_bwd(x, r, g, dy):
    xn = x * r
    dxn = dy * g
    dx = r * (dxn - xn * jnp.mean(dxn * xn, axis=-1, keepdims=True))
    return dx, dy * xn


def _group_sum64(v, bd):
    hi = v.astype(MM)
    lo = (v - hi.astype(F32)).astype(MM)
    outs = []
    for t in range(v.shape[1] // LANES):
        sl = slice(LANES * t, LANES * (t + 1))
        outs.append(jnp.dot(hi[:, sl], bd, preferred_element_type=F32)
                    + jnp.dot(lo[:, sl], bd, preferred_element_type=F32))
    return outs[0] if len(outs) == 1 else jnp.concatenate(outs, axis=1)


def _head_norm_fwd(x, g, bd):
    r = lax.rsqrt(_group_sum64(x * x, bd) * (1.0 / HEAD_DIM) + EPS)
    return x * r * g, r


def _head_norm_bwd(x, r, g, dy, bd):
    xn = x * r
    dxn = dy * g
    dx = r * (dxn - xn * (_group_sum64(dxn * xn, bd) * (1.0 / HEAD_DIM)))
    return dx, dy * xn


def _rope(x, c, s1, s2):
    w = x.shape[1]
    return x * c + pltpu.roll(x, w - ROPE_DIM // 2, 1) * s1 + pltpu.roll(x, ROPE_DIM // 2, 1) * s2


def _rope_bwd(dy, c, s1, s2):
    w = dy.shape[1]
    return dy * c + pltpu.roll(dy * s1, ROPE_DIM // 2, 1) + pltpu.roll(dy * s2, w - ROPE_DIM // 2, 1)


def _tile_lanes(t, reps):
    return t if reps == 1 else jnp.concatenate([t] * reps, axis=1)


def _rope_tables(pos_col, invf):
    s = pos_col.shape[0]
    tm = min(s, 1024)

    def body(pos_ref, invf_ref, c_ref, s1_ref, s2_ref):
        ang = pos_ref[...].astype(F32) * invf_ref[...]
        lane = lax.broadcasted_iota(jnp.int32, ang.shape, 1) % HEAD_DIM
        sn = jnp.sin(ang)
        c_ref[...] = jnp.cos(ang)
        s1_ref[...] = jnp.where(lane < ROPE_DIM // 2, -sn, 0.0)
        s2_ref[...] = jnp.where((lane >= ROPE_DIM // 2) & (lane < ROPE_DIM), sn, 0.0)

    out = jax.ShapeDtypeStruct((s, LANES), F32)
    return pl.pallas_call(
        body, name="rope_tables", grid=(s // tm,), out_shape=(out, out, out),
        in_specs=[_rows(tm, 1), _resident((1, LANES))],
        out_specs=(_rows(tm, LANES),) * 3, compiler_params=_params("parallel"),
    )(pos_col, invf)


def _inproj_fwd(x, attn_norm, win_t, b_gate, tm):
    s = x.shape[0]

    def body(x_ref, gn_ref, w_ref, bg_ref, h_ref, u_ref, q_ref, kv_ref, g_ref):
        h, _ = _rmsnorm_fwd(x_ref[...], gn_ref[...])
        h = h.astype(MM)
        h_ref[...] = h
        u_ref[...] = _mm_nt(h, w_ref[O_U:O_Q, :])
        q_ref[...] = _mm_nt(h, w_ref[O_Q:O_KV, :])
        kv_ref[...] = _mm_nt(h, w_ref[O_KV:O_G, :])
        g_ref[...] = jax.nn.sigmoid(_mm_nt(h, w_ref[O_G:IN_WIDTH, :]) + bg_ref[...])

    sd = jax.ShapeDtypeStruct
    return pl.pallas_call(
        body, name="inproj_fwd", grid=(s // tm,),
        out_shape=(sd((s, D_MODEL), MM), sd((s, 1024), F32), sd((s, 1024), F32), sd((s, 256), F32),
                   sd((s, 2048), F32)),
        in_specs=[_rows(tm, D_MODEL), _resident((1, D_MODEL)), _resident((IN_WIDTH, D_MODEL)), _resident((1, 2048))],
        out_specs=(_rows(tm, D_MODEL), _rows(tm, 1024), _rows(tm, 1024), _rows(tm, 256), _rows(tm, 2048)),
        compiler_params=_params("parallel"),
    )(x, attn_norm, win_t, b_gate)


def _window_sums(src_ref, s1_ref, s2_ref, cols, w, tm, ahead):
    n = POOL_HALO + tm

    def level(src, width, lo):
        if ahead:
            return src(0, n - lo) + src(width, n - lo)
        return src(lo, n - lo) + src(lo - width, n - lo)

    def final(src, width):
        if ahead:
            return src(0, tm) + src(width, tm)
        return src(POOL_HALO, tm) + src(POOL_HALO - width, tm)

    levels = [lambda r0, rows: src_ref[pl.ds(r0, rows), cols]]
    for k, ref in ((1, s1_ref), (2, s2_ref), (3, s1_ref)):
        if w == 2 ** k:
            break
        lo = 8 * k
        ref[pl.ds(0 if ahead else lo, n - lo), :] = level(levels[-1], 2 ** (k - 1), lo)
        levels.append(functools.partial(lambda ref, r0, rows: ref[pl.ds(r0, rows), :], ref))
    return final(levels[-1], w // 2)


def _pooled(ext_ref, s1_ref, s2_ref, tm, row0):
    t = (row0 + lax.broadcasted_iota(jnp.int32, (tm, 1), 0)).astype(F32)
    out = []
    for gi, w in enumerate(POOL_WINDOWS):
        cols = slice(gi * POOL_GROUP, (gi + 1) * POOL_GROUP)
        acc = _window_sums(ext_ref, s1_ref, s2_ref, cols, w, tm, ahead=False)
        cnt = jnp.minimum(t + 1.0, float(w))
        out.append(acc / cnt - ext_ref[pl.ds(POOL_HALO, tm), cols])
    return out


def _pool_fwd(u, wpool, pool_scale, tm):
    s = u.shape[0]
    hb = tm // POOL_HALO

    def body(u_ref, halo_ref, wp_ref, ps_ref, a_ref, ext_ref, s1_ref, s2_ref):
        i = pl.program_id(0)
        ext_ref[pl.ds(0, POOL_HALO), :] = jnp.where(i > 0, halo_ref[...], 0.0)
        ext_ref[pl.ds(POOL_HALO, tm), :] = u_ref[...]
        pooled = _pooled(ext_ref, s1_ref, s2_ref, tm, i * tm)
        for gi in range(4):
            cols = slice(gi * POOL_GROUP, (gi + 1) * POOL_GROUP)
            a_ref[:, cols] = (_mm(pooled[gi], wp_ref[gi]) * ps_ref[:, cols]).astype(a_ref.dtype)

    return pl.pallas_call(
        body, name="pool_fwd", grid=(s // tm,), out_shape=jax.ShapeDtypeStruct((s, 1024), ACT),
        in_specs=[_rows(tm, 1024), pl.BlockSpec((POOL_HALO, 1024), lambda i: (jnp.maximum(i * hb - 1, 0), 0)),
                  _resident((4, POOL_GROUP, POOL_GROUP)), _resident((1, 1024))],
        out_specs=_rows(tm, 1024),
        scratch_shapes=[pltpu.VMEM((POOL_HALO + tm, 1024), F32), pltpu.VMEM((POOL_HALO + tm, POOL_GROUP), F32),
                        pltpu.VMEM((POOL_HALO + tm, POOL_GROUP), F32)],
        compiler_params=_params("parallel"),
    )(u, u, wpool, pool_scale)


PAIRS = GQA_GROUP // 2
PAIR_COLS = PAIRS * BLOCK


def _fold_masks(n):
    r = lax.broadcasted_iota(jnp.int32, (BLOCK, BLOCK), 0)
    i = lax.broadcasted_iota(jnp.int32, (BLOCK, BLOCK), 1)
    prev = r > i
    return prev, jnp.where(prev & (n == 0), NEG, 0.0)


def _fold(band, prev):
    return jnp.where(prev, band[0:BLOCK, :], band[BLOCK:2 * BLOCK, :])


def _unfold(folded, prev):
    top = jnp.where(prev, folded, 0.0)
    return jnp.concatenate([top, folded - top], axis=0).astype(MM)


def _probs_by_pair(sc, sink, masks):
    prev, bias = masks
    out = []
    for j in range(PAIRS):
        cols = slice(j * BLOCK, (j + 1) * BLOCK)
        out.append(_softmax_sink_t(_fold(sc[:, cols], prev) + bias, sink[:, cols]))
    return out


def _stack_pairs(x, hk):
    return jnp.concatenate([x[:, (PAIRS * hk + j) * LANES:(PAIRS * hk + j + 1) * LANES] for j in range(PAIRS)], axis=0)


def _parity_bands(t, hk):
    low = lax.broadcasted_iota(jnp.int32, t.shape, 1) < HEAD_DIM
    own = jnp.where(low if hk == 0 else ~low, t, 0.0)
    other = pltpu.roll(own, HEAD_DIM, 1)
    return (own, other) if hk == 0 else (other, own)


def _fold_parity(even, odd, hk):
    low = lax.broadcasted_iota(jnp.int32, even.shape, 1) < HEAD_DIM
    comb = jnp.where(low, even, odd)
    comb = comb + pltpu.roll(comb, HEAD_DIM, 1)
    return jnp.where(low if hk == 0 else ~low, comb, 0.0)


def _softmax_sink_t(s, sink):
    m = jnp.maximum(jnp.max(s, axis=0, keepdims=True), sink)
    p = jnp.exp(s - m)
    es = jnp.exp(sink - m)
    inv = 1.0 / (jnp.sum(p, axis=0, keepdims=True) + es)
    return p * inv, es * inv


def _attn_fwd(q, kv, tabs, q_norm_t, k_norm_t, sink_rows, bd):
    s = q.shape[0]
    nb = s // BLOCK
    scale = HEAD_DIM ** -0.5
    cur = lambda n: (n, 0)
    prv = lambda n: (jnp.maximum(n - 1, 0), 0)

    def body(q_ref, kvc_ref, kvp_ref, c_ref, s1_ref, s2_ref, cp_ref, s1p_ref, s2p_ref, qn_ref, kn_ref, sink_ref,
             bd_ref, o_ref, pr_ref, ps_ref):
        n = pl.program_id(0)
        bdm = bd_ref[...]
        c, s1, s2 = c_ref[...], s1_ref[...], s2_ref[...]
        qh, _ = _head_norm_fwd(q_ref[...], qn_ref[...], bdm)
        qr = (_rope(qh, _tile_lanes(c, 8), _tile_lanes(s1, 8), _tile_lanes(s2, 8)) * scale).astype(MM)
        kc, _ = _head_norm_fwd(kvc_ref[:, 0:128], kn_ref[...], bdm)
        kp, _ = _head_norm_fwd(kvp_ref[:, 0:128], kn_ref[...], bdm)
        k2 = jnp.concatenate([_rope(kp, cp_ref[...], s1p_ref[...], s2p_ref[...]), _rope(kc, c, s1, s2)], axis=0)
        v2 = jnp.concatenate([kvp_ref[:, 128:256], kvc_ref[:, 128:256]], axis=0)
        mask = _fold_masks(n)
        for hk in range(N_KV_HEADS):
            qs = _stack_pairs(qr, hk)
            ot = jnp.zeros((LANES, PAIR_COLS), F32)
            for par, (kb, vb) in enumerate(zip(_parity_bands(k2, hk), _parity_bands(v2, hk))):
                probs = _probs_by_pair(_mm_nt(kb, qs), sink_ref[hk, par], mask)
                pr_ref[0, 2 * hk + par] = jnp.concatenate([pr for pr, _ in probs], axis=1).astype(MM)
                ps_ref[0, 2 * hk + par] = jnp.concatenate([ps for _, ps in probs], axis=1)
                ot = ot + _mm(vb.T, jnp.concatenate([_unfold(pr, mask[0]) for pr, _ in probs], axis=1))
            for j in range(PAIRS):
                col = (PAIRS * hk + j) * LANES
                o_ref[:, col:col + LANES] = ot[:, j * BLOCK:(j + 1) * BLOCK].T.astype(o_ref.dtype)

    tab = lambda im: pl.BlockSpec((BLOCK, LANES), im)
    sd = jax.ShapeDtypeStruct
    return pl.pallas_call(
        body, name="attn_fwd", grid=(nb,),
        out_shape=(sd((s, 1024), ACT), sd((nb, 4, BLOCK, PAIR_COLS), MM), sd((nb, 4, 1, PAIR_COLS), F32)),
        in_specs=[pl.BlockSpec((BLOCK, 1024), cur), pl.BlockSpec((BLOCK, 256), cur), pl.BlockSpec((BLOCK, 256), prv),
                  tab(cur), tab(cur), tab(cur), tab(prv), tab(prv), tab(prv),
                  _resident((1, 1024)), _resident((1, 128)), _resident((N_KV_HEADS, 2, 1, PAIR_COLS)),
                  _resident((LANES, LANES))],
        out_specs=(pl.BlockSpec((BLOCK, 1024), cur), pl.BlockSpec((1, 4, BLOCK, PAIR_COLS), lambda n: (n, 0, 0, 0)),
                   pl.BlockSpec((1, 4, 1, PAIR_COLS), lambda n: (n, 0, 0, 0))),
        compiler_params=_params("parallel"),
    )(q, kv, kv, *tabs, *tabs, q_norm_t, k_norm_t, sink_rows, bd)


def _mix_out_fwd(x, g, a, b, wout, tm):
    s = x.shape[0]

    def body(x_ref, g_ref, a_ref, b_ref, w_ref, x1_ref, mix_ref):
        mix = (g_ref[:, 0:1024] * a_ref[...] + g_ref[:, 1024:2048] * b_ref[...]).astype(MM)
        mix_ref[...] = mix
        x1_ref[...] = x_ref[...] + _mm(mix, w_ref[...])

    return pl.pallas_call(
        body, name="mix_out_fwd", grid=(s // tm,),
        out_shape=(jax.ShapeDtypeStruct((s, D_MODEL), F32), jax.ShapeDtypeStruct((s, D_MODEL), MM)),
        in_specs=[_rows(tm, 1024), _rows(tm, 2048), _rows(tm, 1024), _rows(tm, 1024), _resident((1024, 1024))],
        out_specs=(_rows(tm, 1024), _rows(tm, 1024)), compiler_params=_params("parallel"),
    )(x, g, a, b, wout)


SHIFT_ROWS = 16


def _sublane_major_matrices(tm):
    r = np.arange(tm)
    pm = r[None, :] == ((tm // 8) * (r % 8) + r // 8)[:, None]
    return jnp.asarray(pm, MM), jnp.asarray(pm.T, MM)


def _to_sublane_major(pm, v):
    return jnp.dot(pm, v, preferred_element_type=F32).astype(MM)


def _to_time_order(pmt, v):
    hi = v.astype(MM)
    r1 = v - hi.astype(F32)
    mid = r1.astype(MM)
    lo = (r1 - mid.astype(F32)).astype(MM)
    dot = functools.partial(jnp.dot, preferred_element_type=F32)
    return dot(pmt, hi) + dot(pmt, mid) + dot(pmt, lo)


def _step_back(vreg_rows, before):
    sub = lax.broadcasted_iota(jnp.int32, vreg_rows.shape, 0)
    return jnp.where(sub == 0, before[7:8, :], pltpu.roll(vreg_rows, 1, 0))


def _step_ahead(vreg_rows, after):
    sub = lax.broadcasted_iota(jnp.int32, vreg_rows.shape, 0)
    return jnp.where(sub == 7, after[0:1, :], pltpu.roll(vreg_rows, 7, 0))


def _fill_back_rows(ext_ref, before, tm, cols):
    last = ext_ref[pl.ds(SHIFT_ROWS + tm - 8, 8), cols]
    pen = ext_ref[pl.ds(SHIFT_ROWS + tm - 16, 8), cols]
    ext_ref[pl.ds(8, 8), cols] = _step_back(last, before[8:16, :])
    ext_ref[pl.ds(0, 8), cols] = _step_back(pen, before[0:8, :])


def _causal_conv(uc, before, w, b, tm):
    back1 = _step_back(uc[tm - 8:tm, :], before[8:16, :])
    back2 = _step_back(uc[tm - 16:tm - 8, :], before[0:8, :])
    w0, w1, w2 = w[0:1, :], w[1:2, :], w[2:3, :]
    first = b + w0 * back2 + w1 * back1 + w2 * uc[0:8, :]
    second = b + w0 * back1 + w1 * uc[0:8, :] + w2 * uc[8:16, :]
    rest = b + w0 * uc[0:tm - 16, :] + w1 * uc[8:tm - 8, :] + w2 * uc[16:tm, :]
    return jnp.concatenate([first, second, rest], axis=0)


def _ffn_fwd(x1, ffn_norm, wup_t, conv_w, conv_b, wdown, target, tm):
    s = x1.shape[0]
    inv_d = 1.0 / D_MODEL
    pm, pmt = _sublane_major_matrices(tm)

    def body(x1_ref, gn_ref, wu_ref, cw_ref, cb_ref, wd_ref, tgt_ref, pm_ref, pmt_ref, h2_ref, u_ref, uc_ref, dy_ref,
             dyb_ref, loss_ref, carry_ref):
        i = pl.program_id(0)

        @pl.when(i == 0)
        def _():
            carry_ref[...] = jnp.zeros_like(carry_ref)
            loss_ref[...] = jnp.zeros_like(loss_ref)

        x1 = x1_ref[...]
        h2, _ = _rmsnorm_fwd(x1, gn_ref[...])
        h2 = _to_sublane_major(pm_ref[...], h2.astype(MM))
        h2_ref[...] = h2
        for c in range(4):
            cols = slice(c * FF_CHUNK, (c + 1) * FF_CHUNK)
            uc = _mm_nt(h2, wu_ref[cols, :])
            u_ref[:, cols] = uc
            uc_ref[:, cols] = _causal_conv(uc, carry_ref[:, cols], cw_ref[:, cols], cb_ref[:, cols], tm)
            carry_ref[:, cols] = uc[tm - SHIFT_ROWS:tm, :]
        down = jnp.zeros((tm, D_MODEL), F32)
        for c in range(2):
            gate = uc_ref[:, c * FF_CHUNK:(c + 1) * FF_CHUNK]
            val = uc_ref[:, D_FF + c * FF_CHUNK:D_FF + (c + 1) * FF_CHUNK]
            act = gate * jax.nn.sigmoid(gate) * val
            down = down + _mm(act, wd_ref[c * FF_CHUNK:(c + 1) * FF_CHUNK, :])
        err = x1 + _to_time_order(pmt_ref[...], down) - tgt_ref[...]
        loss_ref[...] += jnp.full(loss_ref.shape, 0.5 * inv_d * jnp.sum(err * err), F32)
        dy = err * inv_d
        dy_ref[...] = dy
        dyb_ref[...] = _to_sublane_major(pm_ref[...], dy.astype(MM))

    sd = jax.ShapeDtypeStruct
    return pl.pallas_call(
        body, name="ffn_fwd", grid=(s // tm,),
        out_shape=(sd((s, D_MODEL), MM), sd((s, 2 * D_FF), F32), sd((s, 2 * D_FF), F32), sd((s, D_MODEL), F32),
                   sd((s, D_MODEL), MM), sd((8, LANES), F32)),
        in_specs=[_rows(tm, 1024), _resident((1, 1024)), _resident((2 * D_FF, D_MODEL)), _resident((3, 2 * D_FF)),
                  _resident((1, 2 * D_FF)), _resident((D_FF, D_MODEL)), _rows(tm, 1024), _resident((tm, tm)),
                  _resident((tm, tm))],
        out_specs=(_rows(tm, 1024), _rows(tm, 2 * D_FF), _rows(tm, 2 * D_FF), _rows(tm, 1024), _rows(tm, 1024),
                   pl.BlockSpec((8, LANES), lambda i: (0, 0))),
        scratch_shapes=[pltpu.VMEM((SHIFT_ROWS, 2 * D_FF), F32)],
        compiler_params=_params("arbitrary"),
    )(x1, ffn_norm, wup_t, conv_w, conv_b, wdown, target, pm, pmt)


def _ffn_bwd_a(dyb, u, uc, conv_w, wdown, tm):
    s = dyb.shape[0]
    nt = s // tm
    hb = tm // SHIFT_ROWS
    rev = lambda i: (nt - 1 - i, 0)

    def body(dy_ref, u_ref, before_ref, uc_ref, cw_ref, wd_ref, du_ref, act_ref, dcw_ref, dcb_ref, ext_ref, extd_ref,
             ahead_ref):
        i = pl.program_id(0)
        first_tile = i == nt - 1

        @pl.when(i == 0)
        def _():
            ahead_ref[...] = jnp.zeros_like(ahead_ref)
            dcw_ref[...] = jnp.zeros_like(dcw_ref)
            dcb_ref[...] = jnp.zeros_like(dcb_ref)

        ext_ref[pl.ds(SHIFT_ROWS, tm), :] = u_ref[...]
        for c in range(4):
            cols = slice(c * FF_CHUNK, (c + 1) * FF_CHUNK)
            _fill_back_rows(ext_ref, jnp.where(first_tile, 0.0, before_ref[:, cols]), tm, cols)
        dy = dy_ref[...]
        for c in range(2):
            gate = uc_ref[:, c * FF_CHUNK:(c + 1) * FF_CHUNK]
            val = uc_ref[:, D_FF + c * FF_CHUNK:D_FF + (c + 1) * FF_CHUNK]
            sg = jax.nn.sigmoid(gate)
            sl = gate * sg
            act_ref[:, c * FF_CHUNK:(c + 1) * FF_CHUNK] = (sl * val).astype(MM)
            d_act = _mm_nt(dy, wd_ref[c * FF_CHUNK:(c + 1) * FF_CHUNK, :])
            extd_ref[pl.ds(0, tm), c * FF_CHUNK:(c + 1) * FF_CHUNK] = d_act * val * (sg * (1.0 + gate * (1.0 - sg)))
            extd_ref[pl.ds(0, tm), D_FF + c * FF_CHUNK:D_FF + (c + 1) * FF_CHUNK] = d_act * sl
        for c in range(4):
            cols = slice(c * FF_CHUNK, (c + 1) * FF_CHUNK)
            ahead = ahead_ref[:, cols]
            first2 = extd_ref[pl.ds(0, SHIFT_ROWS), cols]
            extd_ref[pl.ds(tm, 8), cols] = _step_ahead(first2[0:8, :], ahead[0:8, :])
            extd_ref[pl.ds(tm + 8, 8), cols] = _step_ahead(first2[8:16, :], ahead[8:16, :])
            ahead_ref[:, cols] = first2
            d0 = extd_ref[pl.ds(0, tm), cols]
            dcb_ref[:, cols] += jnp.sum(d0, axis=0, keepdims=True)
            for j in range(3):
                dcw_ref[j:j + 1, cols] += jnp.sum(d0 * ext_ref[pl.ds(8 * j, tm), cols], axis=0, keepdims=True)
            du = cw_ref[2:3, cols] * d0 + cw_ref[1:2, cols] * extd_ref[pl.ds(8, tm), cols]
            du = du + cw_ref[0:1, cols] * extd_ref[pl.ds(SHIFT_ROWS, tm), cols]
            du_ref[:, cols] = du.astype(MM)

    sd = jax.ShapeDtypeStruct
    return pl.pallas_call(
        body, name="ffn_bwd_a", grid=(nt,),
        out_shape=(sd((s, 2 * D_FF), MM), sd((s, D_FF), MM), sd((3, 2 * D_FF), F32), sd((1, 2 * D_FF), F32)),
        in_specs=[pl.BlockSpec((tm, D_MODEL), rev), pl.BlockSpec((tm, 2 * D_FF), rev),
                  pl.BlockSpec((SHIFT_ROWS, 2 * D_FF), lambda i: (jnp.maximum((nt - 1 - i) * hb - 1, 0), 0)),
                  pl.BlockSpec((tm, 2 * D_FF), rev), _resident((3, 2 * D_FF)), _resident((D_FF, D_MODEL))],
        out_specs=(pl.BlockSpec((tm, 2 * D_FF), rev), pl.BlockSpec((tm, D_FF), rev),
                   pl.BlockSpec((3, 2 * D_FF), lambda i: (0, 0)), pl.BlockSpec((1, 2 * D_FF), lambda i: (0, 0))),
        scratch_shapes=[pltpu.VMEM((SHIFT_ROWS + tm, 2 * D_FF), F32), pltpu.VMEM((tm + SHIFT_ROWS, 2 * D_FF), F32),
                        pltpu.VMEM((SHIFT_ROWS, 2 * D_FF), F32)],
        compiler_params=_params("arbitrary"),
    )(dyb, u, u, uc, conv_w, wdown)


def _after(after):
    tie = [] if after is None else list(after) if isinstance(after, (list, tuple)) else [after]
    return tie, [pl.BlockSpec(memory_space=pl.ANY)] * len(tie)


def _matmul_tn(a, b, tmo, tk, name, into=None, row0=0):
    s, m = a.shape
    n = b.shape[1]
    nk = s // tk
    rows = m if into is None else into.shape[0]
    assert row0 % LANES == 0 and tmo % LANES == 0
    grown, grown_spec = ([], []) if into is None else ([into], [ANY])

    def body(a_ref, b_ref, *rest):
        o_ref = rest[-1]
        k = pl.program_id(1)

        @pl.when(k == 0)
        def _():
            o_ref[...] = jnp.zeros_like(o_ref)

        o_ref[...] += _mm_tn(a_ref[...], b_ref[pl.ds(pl.multiple_of(k * tk, tk), tk), :])

    return pl.pallas_call(
        body, name=name, grid=(m // tmo, nk), out_shape=jax.ShapeDtypeStruct((rows, n), F32),
        in_specs=[pl.BlockSpec((tk, tmo), lambda i, k: (k, i)), _resident((s, n))] + grown_spec,
        out_specs=pl.BlockSpec((pl.Element(tmo), pl.Element(n)), lambda i, k: (pl.multiple_of(row0 + i * tmo, LANES), 0)),
        input_output_aliases={2: 0} if grown else {},
        compiler_params=_params("parallel", "arbitrary"),
    )(a, b, *grown)


def _ffn_bwd_b(du, wup_t, x1, ffn_norm, dy, tm):
    s = du.shape[0]

    def body(du_ref, wu_ref, x1_ref, gn_ref, dy_ref, pmt_ref, dx1_ref, dx1b_ref, dg_ref):
        @pl.when(pl.program_id(0) == 0)
        def _():
            dg_ref[...] = jnp.zeros_like(dg_ref)

        dh2 = _to_time_order(pmt_ref[...], _mm(du_ref[...], wu_ref[...]))
        x1 = x1_ref[...]
        _, r = _rmsnorm_fwd(x1, gn_ref[...])
        dx, dgr = _rmsnorm_bwd(x1, r, gn_ref[...], dh2)
        dg_ref[...] += jnp.sum(dgr, axis=0, keepdims=True)
        dx1 = dy_ref[...] + dx
        dx1_ref[...] = dx1
        dx1b_ref[...] = dx1.astype(MM)

    return pl.pallas_call(
        body, name="ffn_bwd_b", grid=(s // tm,),
        out_shape=(jax.ShapeDtypeStruct((s, D_MODEL), F32), jax.ShapeDtypeStruct((s, D_MODEL), MM),
                   jax.ShapeDtypeStruct((1, D_MODEL), F32)),
        in_specs=[_rows(tm, 2 * D_FF), _resident((2 * D_FF, D_MODEL)), _rows(tm, 1024), _resident((1, 1024)),
                  _rows(tm, 1024), _resident((tm, tm))],
        out_specs=(_rows(tm, 1024), _rows(tm, 1024), pl.BlockSpec((1, D_MODEL), lambda i: (0, 0))),
        compiler_params=_params("arbitrary"),
    )(du, wup_t, x1, ffn_norm, dy, _sublane_major_matrices(tm)[1])


def _win_rows(row0, rows):
    return pl.BlockSpec((pl.Element(rows), pl.Element(D_MODEL)), lambda i: (row0, 0), pipeline_mode=pl.Buffered(1))


def _mix_bwd(dx1b, wout, g, a, b, mix, h, tm, after=None):
    s = dx1b.shape[0]
    tie, tie_spec = _after(after)

    def body(dx_ref, w_ref, g_ref, a_ref, b_ref, mix_ref, h_ref, *rest):
        da_ref, db_ref, dzg_ref, dbg_ref, dwo_ref, dwin_ref = rest[-6:]

        @pl.when(pl.program_id(0) == 0)
        def _():
            dbg_ref[...] = jnp.zeros_like(dbg_ref)
            dwo_ref[...] = jnp.zeros_like(dwo_ref)
            dwin_ref[...] = jnp.zeros_like(dwin_ref)

        dx = dx_ref[...]
        dwo_ref[...] += _mm_tn(mix_ref[...], dx)
        dmix = _mm_nt(dx, w_ref[...])
        for half, src, dst in ((0, a_ref, da_ref), (1, b_ref, db_ref)):
            cols = slice(half * 1024, (half + 1) * 1024)
            gt = g_ref[:, cols]
            dst[...] = (dmix * gt).astype(dst.dtype)
            dz = dmix * src[...] * gt * (1.0 - gt)
            dzb = dz.astype(MM)
            dzg_ref[:, cols] = dzb
            dwin_ref[cols, :] += _mm_tn(dzb, h_ref[...])
            dbg_ref[:, cols] += jnp.sum(dz, axis=0, keepdims=True)

    sd = jax.ShapeDtypeStruct
    return pl.pallas_call(
        body, name="mix_bwd", grid=(s // tm,),
        out_shape=(sd((s, 1024), F32), sd((s, 1024), MM), sd((s, 2048), MM), sd((1, 2048), F32),
                   sd((D_MODEL, D_MODEL), F32), sd((IN_WIDTH, D_MODEL), F32)),
        in_specs=[_rows(tm, 1024), _resident((1024, 1024)), _rows(tm, 2048), _rows(tm, 1024), _rows(tm, 1024),
                  _rows(tm, 1024), _rows(tm, 1024)] + tie_spec,
        out_specs=(_rows(tm, 1024), _rows(tm, 1024), _rows(tm, 2048), pl.BlockSpec((1, 2048), lambda i: (0, 0)),
                   _resident((D_MODEL, D_MODEL)), _win_rows(O_G, IN_WIDTH - O_G)),
        compiler_params=_params("arbitrary"),
    )(dx1b, wout, g, a, b, mix, h, *tie)


def _pool_bwd(u, da, wpool, pool_scale, h, d_win_t, tm, after=None):
    s = u.shape[0]
    nt = s // tm
    hb = tm // POOL_HALO

    tie, tie_spec = _after(after)

    def body(u_ref, uh_ref, da_ref, dah_ref, wp_ref, ps_ref, h_ref, *rest):
        dzu_ref, dwp_ref, dps_ref, dwin_ref, ext_ref, exte_ref, s1_ref, s2_ref = rest[-8:]
        i = pl.program_id(0)

        @pl.when(i == 0)
        def _():
            dwp_ref[...] = jnp.zeros_like(dwp_ref)
            dps_ref[...] = jnp.zeros_like(dps_ref)
            dwin_ref[...] = jnp.zeros_like(dwin_ref)

        ext_ref[pl.ds(0, POOL_HALO), :] = jnp.where(i > 0, uh_ref[...], 0.0)
        ext_ref[pl.ds(POOL_HALO, tm), :] = u_ref[...]
        pooled = _pooled(ext_ref, s1_ref, s2_ref, tm, i * tm)
        da = da_ref[...]
        dah = jnp.where(i < nt - 1, dah_ref[...], 0.0)
        t = (i * tm + lax.broadcasted_iota(jnp.int32, (tm + POOL_HALO, 1), 0)).astype(F32)
        for gi, w in enumerate(POOL_WINDOWS):
            cols = slice(gi * POOL_GROUP, (gi + 1) * POOL_GROUP)
            pg = pooled[gi].astype(MM)
            wg = wp_ref[gi]
            mixed = _mm(pg, wg)
            dps_ref[:, cols] += jnp.sum(da[:, cols] * mixed, axis=0, keepdims=True)
            dmx = (da[:, cols] * ps_ref[:, cols]).astype(MM)
            dwp_ref[gi] += _mm_tn(pg, dmx)
            dpl = _mm_nt(dmx, wg)
            dplh = _mm_nt(dah[:, cols] * ps_ref[:, cols], wg)
            cnt = jnp.minimum(t + 1.0, float(w))
            exte_ref[pl.ds(0, tm), cols] = dpl / cnt[0:tm]
            exte_ref[pl.ds(tm, POOL_HALO), cols] = dplh / cnt[tm:tm + POOL_HALO]
            acc = _window_sums(exte_ref, s1_ref, s2_ref, cols, w, tm, ahead=True)
            dzu = (acc - dpl).astype(MM)
            dzu_ref[:, cols] = dzu
            dwin_ref[cols, :] += _mm_tn(dzu, h_ref[...])

    sd = jax.ShapeDtypeStruct
    last_halo = s // POOL_HALO - 1
    return pl.pallas_call(
        body, name="pool_bwd", grid=(nt,),
        out_shape=(sd((s, 1024), MM), sd((4, POOL_GROUP, POOL_GROUP), F32), sd((1, 1024), F32),
                   sd((IN_WIDTH, D_MODEL), F32)),
        in_specs=[_rows(tm, 1024), pl.BlockSpec((POOL_HALO, 1024), lambda i: (jnp.maximum(i * hb - 1, 0), 0)),
                  _rows(tm, 1024),
                  pl.BlockSpec((POOL_HALO, 1024), lambda i: (jnp.minimum((i + 1) * hb, last_halo), 0)),
                  _resident((4, POOL_GROUP, POOL_GROUP)), _resident((1, 1024)), _rows(tm, 1024)] + tie_spec + [ANY],
        out_specs=(_rows(tm, 1024), pl.BlockSpec((4, POOL_GROUP, POOL_GROUP), lambda i: (0, 0, 0)),
                   pl.BlockSpec((1, 1024), lambda i: (0, 0)), _win_rows(O_U, O_Q - O_U)),
        input_output_aliases={7 + len(tie): 3},
        scratch_shapes=[pltpu.VMEM((POOL_HALO + tm, 1024), F32), pltpu.VMEM((tm + POOL_HALO, 1024), F32),
                        pltpu.VMEM((POOL_HALO + tm, POOL_GROUP), F32), pltpu.VMEM((POOL_HALO + tm, POOL_GROUP), F32)],
        compiler_params=_params("arbitrary"),
    )(u, u, da, da, wpool, pool_scale, h, *tie, d_win_t)


def _attn_bwd(q, kv, db, tabs, q_norm_t, k_norm_t, probs, sink_probs, bd, after=None):
    s = q.shape[0]
    nb = s // BLOCK
    scale = HEAD_DIM ** -0.5
    cur = lambda n: (jnp.minimum(n, nb - 1), 0)
    prv = lambda n: (jnp.maximum(n - 1, 0), 0)
    tie, tie_spec = _after(after)

    def body(q_ref, kvc_ref, kvp_ref, db_ref, c_ref, s1_ref, s2_ref, cp_ref, s1p_ref, s2p_ref, qn_ref, kn_ref,
             pr_ref, ps_ref, bd_ref, *rest):
        (dzq_ref, dzkv_ref, dqn_ref, dkn_ref, dsk_ref,
         carry_ref, tot_ref, dqr_ref, qacc_ref, kacc_ref, sacc_ref) = rest[-11:]
        n = pl.program_id(0)
        bdm = bd_ref[...]
        kn = kn_ref[...]

        @pl.when(n == 0)
        def _():
            carry_ref[...] = jnp.zeros_like(carry_ref)
            qacc_ref[...] = jnp.zeros_like(qacc_ref)
            kacc_ref[...] = jnp.zeros_like(kacc_ref)
            sacc_ref[...] = jnp.zeros_like(sacc_ref)

        kp_raw = kvp_ref[:, 0:128]
        kph, rp = _head_norm_fwd(kp_raw, kn, bdm)
        cp, s1p, s2p = cp_ref[...], s1p_ref[...], s2p_ref[...]

        @pl.when(n < nb)
        def _():
            c, s1, s2 = c_ref[...], s1_ref[...], s2_ref[...]
            c8, s18, s28 = _tile_lanes(c, 8), _tile_lanes(s1, 8), _tile_lanes(s2, 8)
            q_raw = q_ref[...]
            qh, rq = _head_norm_fwd(q_raw, qn_ref[...], bdm)
            qr = (_rope(qh, c8, s18, s28) * scale).astype(MM)
            kc, _ = _head_norm_fwd(kvc_ref[:, 0:128], kn, bdm)
            k2 = jnp.concatenate([_rope(kph, cp, s1p, s2p), _rope(kc, c, s1, s2)], axis=0)
            v2 = jnp.concatenate([kvp_ref[:, 128:256], kvc_ref[:, 128:256]], axis=0)
            dob = db_ref[...].astype(MM)
            mask = _fold_masks(n)
            lane = lax.broadcasted_iota(jnp.int32, (1, LANES), 1)
            dsk = jnp.zeros((1, LANES), F32)
            dk2 = jnp.zeros((2 * BLOCK, LANES), F32)
            dv2 = jnp.zeros((2 * BLOCK, LANES), F32)
            for hk in range(N_KV_HEADS):
                qs = _stack_pairs(qr, hk)
                do = _stack_pairs(dob, hk)
                dqt = jnp.zeros((LANES, PAIR_COLS), F32)
                dkb, dvb = [], []
                for par, (kb, vb) in enumerate(zip(_parity_bands(k2, hk), _parity_bands(v2, hk))):
                    dp = _mm_nt(vb, do)
                    prs, dss = [], []
                    for j in range(PAIRS):
                        cols = slice(j * BLOCK, (j + 1) * BLOCK)
                        pr = pr_ref[0, 2 * hk + par, :, cols].astype(F32)
                        psink = ps_ref[0, 2 * hk + par, :, cols]
                        dpj = _fold(dp[:, cols], mask[0])
                        coldot = jnp.sum(pr * dpj, axis=0, keepdims=True)
                        dss.append(_unfold(pr * (dpj - coldot), mask[0]))
                        prs.append(_unfold(pr, mask[0]))
                        h = hk * GQA_GROUP + 2 * j + par
                        dsk = dsk + jnp.where(lane == h, jnp.sum(-psink * coldot), 0.0)
                    ds, pr = jnp.concatenate(dss, axis=1), jnp.concatenate(prs, axis=1)
                    dqt = dqt + _mm(kb.T, ds)
                    dkb.append(_mm(ds, qs))
                    dvb.append(_mm(pr, do))
                for j in range(PAIRS):
                    col = (PAIRS * hk + j) * LANES
                    dqr_ref[:, col:col + LANES] = dqt[:, j * BLOCK:(j + 1) * BLOCK].T
                dk2 = dk2 + _fold_parity(dkb[0], dkb[1], hk)
                dv2 = dv2 + _fold_parity(dvb[0], dvb[1], hk)
            tot_ref[:, 0:128] = carry_ref[:, 0:128] + dk2[0:BLOCK, :]
            tot_ref[:, 128:256] = carry_ref[:, 128:256] + dv2[0:BLOCK, :]
            carry_ref[:, 0:128] = dk2[BLOCK:2 * BLOCK, :]
            carry_ref[:, 128:256] = dv2[BLOCK:2 * BLOCK, :]
            sacc_ref[...] += dsk
            dqh = _rope_bwd(dqr_ref[...] * scale, c8, s18, s28)
            dq, dgq = _head_norm_bwd(q_raw, rq, qn_ref[...], dqh, bdm)
            dzq_ref[...] = dq.astype(MM)
            qacc_ref[...] += jnp.sum(dgq, axis=0, keepdims=True)

        @pl.when(n == nb)
        def _():
            tot_ref[...] = carry_ref[...]

        dkh = _rope_bwd(tot_ref[:, 0:128], cp, s1p, s2p)
        dkr, dgk = _head_norm_bwd(kp_raw, rp, kn, dkh, bdm)
        dzkv_ref[:, 0:128] = dkr.astype(MM)
        dzkv_ref[:, 128:256] = tot_ref[:, 128:256].astype(MM)
        kacc_ref[...] += jnp.where(n > 0, jnp.sum(dgk, axis=0, keepdims=True), 0.0)

        @pl.when(n == nb)
        def _():
            fold = qacc_ref[:, 0:HEAD_DIM]
            for h in range(1, N_Q_HEADS):
                fold = fold + qacc_ref[:, h * HEAD_DIM:(h + 1) * HEAD_DIM]
            dqn_ref[...] = fold
            dkn_ref[...] = kacc_ref[:, 0:HEAD_DIM] + kacc_ref[:, HEAD_DIM:2 * HEAD_DIM]
            dsk_ref[...] = sacc_ref[...]

    tab = lambda im: pl.BlockSpec((BLOCK, LANES), im)
    sd = jax.ShapeDtypeStruct
    const = lambda n: (0, 0)
    return pl.pallas_call(
        body, name="attn_bwd", grid=(nb + 1,),
        out_shape=(sd((s, 1024), MM), sd((s, 256), MM), sd((1, HEAD_DIM), F32), sd((1, HEAD_DIM), F32),
                   sd((1, LANES), F32)),
        in_specs=[pl.BlockSpec((BLOCK, 1024), cur), pl.BlockSpec((BLOCK, 256), cur), pl.BlockSpec((BLOCK, 256), prv),
                  pl.BlockSpec((BLOCK, 1024), cur), tab(cur), tab(cur), tab(cur), tab(prv), tab(prv), tab(prv),
                  _resident((1, 1024)), _resident((1, 128)),
                  pl.BlockSpec((1, 4, BLOCK, PAIR_COLS), lambda n: (jnp.minimum(n, nb - 1), 0, 0, 0)),
                  pl.BlockSpec((1, 4, 1, PAIR_COLS), lambda n: (jnp.minimum(n, nb - 1), 0, 0, 0)),
                  _resident((LANES, LANES))] + tie_spec,
        out_specs=(pl.BlockSpec((BLOCK, 1024), cur), pl.BlockSpec((BLOCK, 256), prv),
                   pl.BlockSpec((1, HEAD_DIM), const), pl.BlockSpec((1, HEAD_DIM), const),
                   pl.BlockSpec((1, LANES), const)),
        scratch_shapes=[pltpu.VMEM((BLOCK, 256), F32), pltpu.VMEM((BLOCK, 256), F32), pltpu.VMEM((BLOCK, 1024), F32),
                        pltpu.VMEM((1, 1024), F32), pltpu.VMEM((1, 128), F32), pltpu.VMEM((1, LANES), F32)],
        compiler_params=_params("arbitrary"),
    )(q, kv, kv, db, *tabs, *tabs, q_norm_t, k_norm_t, probs, sink_probs, bd, *tie)


def _inproj_bwd(dzu, dzq, dzkv, dzg, win_t, x, attn_norm, dx1, tm, after=None):
    s = x.shape[0]
    tie, tie_spec = _after(after)

    def body(du_ref, dq_ref, dkv_ref, dg_ref, w_ref, x_ref, gn_ref, dx1_ref, *rest):
        gx_ref, dgn_ref = rest[-2:]

        @pl.when(pl.program_id(0) == 0)
        def _():
            dgn_ref[...] = jnp.zeros_like(dgn_ref)

        dh = _mm(du_ref[...], w_ref[O_U:O_Q, :]) + _mm(dq_ref[...], w_ref[O_Q:O_KV, :])
        dh = dh + _mm(dkv_ref[...], w_ref[O_KV:O_G, :]) + _mm(dg_ref[...], w_ref[O_G:IN_WIDTH, :])
        x = x_ref[...]
        _, r = _rmsnorm_fwd(x, gn_ref[...])
        dx, dgr = _rmsnorm_bwd(x, r, gn_ref[...], dh)
        dgn_ref[...] += jnp.sum(dgr, axis=0, keepdims=True)
        gx_ref[...] = dx1_ref[...] + dx

    return pl.pallas_call(
        body, name="inproj_bwd", grid=(s // tm,),
        out_shape=(jax.ShapeDtypeStruct((s, D_MODEL), F32), jax.ShapeDtypeStruct((1, D_MODEL), F32)),
        in_specs=[_rows(tm, 1024), _rows(tm, 1024), _rows(tm, 256), _rows(tm, 2048),
                  _resident((IN_WIDTH, D_MODEL)), _rows(tm, 1024), _resident((1, 1024)), _rows(tm, 1024)] + tie_spec,
        out_specs=(_rows(tm, 1024), pl.BlockSpec((1, D_MODEL), lambda i: (0, 0))),
        compiler_params=_params("arbitrary"),
    )(dzu, dzq, dzkv, dzg, win_t, x, attn_norm, dx1, *tie)


def _attention_constants(q_norm, k_norm, sinks):
    inv_freq = np.float32(ROPE_THETA) ** (-np.arange(0, ROPE_DIM, 2, dtype=np.float32) / np.float32(ROPE_DIM))
    lane = np.arange(LANES) % HEAD_DIM
    invf = jnp.asarray(np.where(lane < ROPE_DIM, inv_freq[lane % (ROPE_DIM // 2)], 0.0).reshape(1, LANES), F32)
    bd = jnp.asarray(np.arange(LANES)[:, None] // HEAD_DIM == np.arange(LANES)[None, :] // HEAD_DIM, MM)
    q_norm_t = jnp.tile(q_norm, (1, N_Q_HEADS))
    k_norm_t = jnp.tile(k_norm, (1, N_KV_HEADS))
    sink_rows = jnp.repeat(sinks.reshape(N_KV_HEADS, PAIRS, 2).transpose(0, 2, 1), BLOCK, axis=2)
    sink_rows = sink_rows.reshape(N_KV_HEADS, 2, 1, PAIR_COLS)
    return invf, bd, q_norm_t, k_norm_t, sink_rows


ANY = pl.BlockSpec(memory_space=pl.ANY)


def _position():
    return lax.axis_index("x"), lax.axis_index("y"), lax.axis_index("c")


def _all_gather(shards):
    k = len(shards)

    def body(*refs):
        ins, outs = refs[:k], refs[k:2 * k]
        send_sems, recv_sems, local_sems = refs[2 * k:]
        x, y, c = _position()
        me, sibling = (x, y, c), (x, y, 1 - c)
        chips = [(1 - x, y), (x, 1 - y), (1 - x, 1 - y)]

        def copy(a, kk, block, to, src=None):
            dst = outs[a].at[4 * block[0] + 2 * block[1] + block[2]]
            return pltpu.make_async_remote_copy(
                src_ref=dst if src is None else src, dst_ref=dst, send_sem=send_sems.at[a * 7 + kk],
                recv_sem=recv_sems.at[a * 7 + kk], device_id=to, device_id_type=MESH)

        mine = [pltpu.make_async_copy(ins[a], outs[a].at[4 * x + 2 * y + c], local_sems.at[a]) for a in range(k)]
        for cp in mine:
            cp.start()
        first = []
        for a in range(k):
            first.append(copy(a, 0, me, sibling, src=ins[a]))
            first += [copy(a, 1 + j, me, (*chip, c), src=ins[a]) for j, chip in enumerate(chips)]
        for cp in first:
            cp.start()
        passed = []
        for j, chip in enumerate(chips):
            for a in range(k):
                copy(a, 1 + j, (*chip, c), me).wait_recv()
                cp = copy(a, 4 + j, (*chip, c), sibling)
                cp.start()
                passed.append(cp)
        for a in range(k):
            copy(a, 0, sibling, me).wait_recv()
            for j, chip in enumerate(chips):
                copy(a, 4 + j, (*chip, 1 - c), me).wait_recv()
        for cp in first + passed:
            cp.wait_send()
        for cp in mine:
            cp.wait()

    return pl.pallas_call(
        body, name="all_gather_weights",
        out_shape=tuple(jax.ShapeDtypeStruct((N_DEV,) + s.shape, s.dtype) for s in shards),
        in_specs=[ANY] * k, out_specs=(ANY,) * k,
        scratch_shapes=[pltpu.SemaphoreType.DMA((7 * k,)), pltpu.SemaphoreType.DMA((7 * k,)),
                        pltpu.SemaphoreType.DMA((k,))],
    )(*shards)


HBM = pl.BlockSpec(memory_space=pltpu.HBM)
SEM = pl.BlockSpec(memory_space=pltpu.SEMAPHORE)
EFFECT = pltpu.SideEffectType.DATAFLOW_SIDE_EFFECTING


def _exchange_start(name, bufs, n_sems, copies, after=None):
    k = len(bufs)
    tie, tie_spec = _after(after)
    n_in = k + len(tie)

    def body(*refs):
        for cp in copies(refs[:k], refs[n_in], refs[n_in + 1]):
            cp.start()
        refs[-1][...] = jnp.zeros_like(refs[-1])

    dma = pltpu.SemaphoreType.DMA((n_sems,))
    out = pl.pallas_call(
        body, name=name,
        out_shape=(dma, dma, *[pltpu.HBM(b.shape, b.dtype) for b in bufs], jax.ShapeDtypeStruct((8, LANES), F32)),
        in_specs=[HBM] * k + tie_spec, out_specs=(SEM, SEM, *[HBM] * k, pl.BlockSpec(memory_space=pltpu.VMEM)),
        input_output_aliases={i: 2 + i for i in range(k)},
        compiler_params=pltpu.CompilerParams(has_side_effects=EFFECT),
    )(*[pltpu.with_memory_space_constraint(b, pltpu.HBM) for b in bufs], *tie)
    return out[0], out[1], list(out[2:2 + k]), out[-1]


def _exchange_mid(name, bufs, sems_in, n_sems, waits, copies, after):
    k, ns = len(bufs), len(sems_in)

    def body(*refs):
        ins = refs[:k]
        waits(ins, *refs[k:k + ns])
        for cp in copies(ins, refs[k + ns + 1], refs[k + ns + 2]):
            cp.start()

    dma = pltpu.SemaphoreType.DMA((n_sems,))
    out = pl.pallas_call(
        body, name=name, out_shape=(dma, dma, *[pltpu.HBM(b.shape, b.dtype) for b in bufs]),
        in_specs=[HBM] * k + [SEM] * ns + [ANY], out_specs=(SEM, SEM, *[HBM] * k),
        input_output_aliases={i: 2 + i for i in range(k)},
        compiler_params=pltpu.CompilerParams(has_side_effects=EFFECT),
    )(*bufs, *sems_in, after)
    return out[0], out[1], list(out[2:])


def _exchange_wait(name, bufs, sems, waits, after=None):
    k, ns = len(bufs), len(sems)
    tie, tie_spec = _after(after)

    def body(*refs):
        waits(refs[:k], *refs[k:k + ns])

    out = pl.pallas_call(
        body, name=name, out_shape=tuple(pltpu.HBM(b.shape, b.dtype) for b in bufs),
        in_specs=[HBM] * k + [SEM] * ns + tie_spec, out_specs=(HBM,) * k,
        input_output_aliases={i: i for i in range(k)},
        compiler_params=pltpu.CompilerParams(has_side_effects=EFFECT),
    )(*bufs, *sems, *tie)
    return list(out)


def _gather_copies(k, direct):
    def copies(refs, send_sems, recv_sems):
        x, y, c = _position()
        chips = [(1 - x, y), (x, 1 - y), (1 - x, 1 - y)]
        out = []
        for a in range(k):
            land = refs[k + a]
            if direct:
                mine = land.at[4 * x + 2 * y + c]
                for kk, to in enumerate([(x, y, 1 - c)] + [(*chip, c) for chip in chips]):
                    out.append(pltpu.make_async_remote_copy(
                        src_ref=refs[a], dst_ref=mine, send_sem=send_sems.at[4 * a + kk],
                        recv_sem=recv_sems.at[4 * a + kk], device_id=to, device_id_type=MESH))
            else:
                for j, (px, py) in enumerate(chips):
                    slot = land.at[4 * px + 2 * py + c]
                    out.append(pltpu.make_async_remote_copy(
                        src_ref=slot, dst_ref=slot, send_sem=send_sems.at[3 * a + j], recv_sem=recv_sems.at[3 * a + j],
                        device_id=(x, y, 1 - c), device_id_type=MESH))
        return out
    return copies


def _all_gather_behind(shards, start_after, mid_after):
    k = len(shards)
    me = 4 * lax.axis_index("x") + 2 * lax.axis_index("y") + lax.axis_index("c")
    lands = [lax.dynamic_update_slice(lax.empty((N_DEV,) + s.shape, s.dtype), s[None], (me, 0, 0)) for s in shards]
    direct, passed = _gather_copies(k, True), _gather_copies(k, False)

    send_a, recv_a, bufs, token = _exchange_start("gather_start", list(shards) + lands, 4 * k, direct, start_after)

    def finish():
        def wait_ici(refs, send_sems, recv_sems):
            for i, cp in enumerate(direct(refs, send_sems, recv_sems)):
                if i % 4:
                    cp.wait_recv()

        send_b, recv_b, bufs2 = _exchange_mid("gather_pass", bufs, [send_a, recv_a], 3 * k, wait_ici, passed,
                                              mid_after())

        def wait_all(refs, sa, ra, sb, rb):
            for i, cp in enumerate(direct(refs, sa, ra)):
                cp.wait_send()
                if i % 4 == 0:
                    cp.wait_recv()
            for cp in passed(refs, sb, rb):
                cp.wait()

        return _exchange_wait("gather_wait", bufs2, [send_a, recv_a, send_b, recv_b], wait_all)[k:]

    return token, finish


def _pair_copies(k):
    def copies(refs, send_sems, recv_sems):
        x, y, c = _position()
        return [pltpu.make_async_remote_copy(
            src_ref=refs[a].at[2 * ch + 1 - c], dst_ref=refs[k + a].at[ch], send_sem=send_sems.at[4 * a + ch],
            recv_sem=recv_sems.at[4 * a + ch], device_id=(x, y, 1 - c), device_id_type=MESH)
            for a in range(k) for ch in range(4)]
    return copies


def _chip_copies(k):
    def copies(refs, send_sems, recv_sems):
        x, y, c = _position()
        return [pltpu.make_async_remote_copy(
            src_ref=refs[a].at[2 * px + py], dst_ref=refs[k + a].at[rel], send_sem=send_sems.at[3 * a + rel],
            recv_sem=recv_sems.at[3 * a + rel], device_id=(px, py, c), device_id_type=MESH)
            for a in range(k) for rel, (px, py) in enumerate([(1 - x, y), (x, 1 - y), (1 - x, 1 - y)])]
    return copies


def _symmetric_exchange(name, srcs, n_land, copies_of):
    k = len(srcs)
    lands = [lax.empty((n_land,) + s.shape[1:], s.dtype) for s in srcs]
    copies = copies_of(k)
    send_sems, recv_sems, bufs, token = _exchange_start(name + "_start", list(srcs) + lands, n_land * k, copies)

    def finish(after):
        def wait_all(refs, ss, rs):
            for cp in copies(refs, ss, rs):
                cp.wait()

        done = _exchange_wait(name + "_wait", bufs, [send_sems, recv_sems], wait_all, after)
        return done[:k], done[k:]

    return token, finish


def _pair_add(full, recv, wire):
    _, r, c_ = full.shape
    core = lax.axis_index("c").astype(jnp.int32).reshape(1)

    def body(core_ref, f_ref, r_ref, pw_ref, own_ref):
        ch = pl.program_id(0)
        x, y, _ = _position()
        tot = f_ref[0, 0] + r_ref[0]
        pw_ref[0] = tot.astype(pw_ref.dtype)

        @pl.when(ch == 2 * x + y)
        def _():
            own_ref[...] = tot

    return pl.pallas_call(
        body, name="grad_pair_add",
        grid_spec=pltpu.PrefetchScalarGridSpec(
            num_scalar_prefetch=1, grid=(4,),
            in_specs=[pl.BlockSpec((1, 1, r, c_), lambda i, core_ref: (i, core_ref[0], 0, 0)),
                      pl.BlockSpec((1, r, c_), lambda i, core_ref: (i, 0, 0))],
            out_specs=(pl.BlockSpec((1, r, c_), lambda i, core_ref: (i, 0, 0)),
                       pl.BlockSpec((r, c_), lambda i, core_ref: (0, 0)))),
        out_shape=(jax.ShapeDtypeStruct((4, r, c_), wire), jax.ShapeDtypeStruct((r, c_), F32)),
        compiler_params=_params("arbitrary"),
    )(core, full.reshape(4, 2, r, c_), recv)


def _adamw_math(w, g, m, v):
    m = ADAM_B1 * m + (1.0 - ADAM_B1) * g
    v = ADAM_B2 * v + (1.0 - ADAM_B2) * (g * g)
    m_hat = m / (1.0 - ADAM_B1 ** ADAM_STEP)
    v_hat = v / (1.0 - ADAM_B2 ** ADAM_STEP)
    delta = -ADAM_LR * (m_hat / (jnp.sqrt(v_hat) + ADAM_EPS) + ADAM_WD * w)
    return delta, m, v


def _row_tile(r):
    for t in (256, 272, 176, 128):
        if r % t == 0 and r > t:
            return t
    return r


def _adamw(g_own, recv, w, m, v):
    r, c_ = w.shape
    t = _row_tile(r)
    blk = pl.BlockSpec((t, c_), lambda i: (i, 0))

    def body(g_ref, r_ref, w_ref, m_ref, v_ref, go_ref, d_ref, mo_ref, vo_ref):
        g = g_ref[...]
        for i in range(3):
            g = g + r_ref[i].astype(F32)
        go_ref[...] = g
        d_ref[...], mo_ref[...], vo_ref[...] = _adamw_math(w_ref[...], g, m_ref[...], v_ref[...])

    return pl.pallas_call(
        body, name="adamw", grid=(r // t,), out_shape=(jax.ShapeDtypeStruct((r, c_), F32),) * 4,
        in_specs=[blk, pl.BlockSpec((3, t, c_), lambda i: (0, i, 0)), blk, blk, blk], out_specs=(blk,) * 4,
        compiler_params=_params("parallel"),
    )(g_own, recv, w, m, v)


SMALL = ("attn_norm", "b_gate", "pool_scale", "q_norm", "k_norm", "sinks", "ffn_norm", "conv_b")
SMALL_SIZES = (1024, 2048, 1024, 64, 64, 16, 1024, 5632)
SMALL_OFFSETS = tuple(sum(-(-s // LANES) * LANES for s in SMALL_SIZES[:i]) for i in range(len(SMALL_SIZES) + 1))
SMALL_WIDTH = SMALL_OFFSETS[-1] + LANES


def _pack_small(d, loss=None):
    parts = [jnp.pad(d[n].reshape(1, -1), ((0, 0), (0, -s % LANES))) for n, s in zip(SMALL, SMALL_SIZES)]
    last = jnp.zeros((1, LANES), F32) if loss is None else jnp.pad(loss.reshape(1, 1), ((0, 0), (0, LANES - 1)))
    return jnp.concatenate(parts + [last], axis=1)


def _small_allreduce(gp):
    def body(g_ref, sum_ref, slots_ref, send_sems, recv_sems):
        x, y, c = _position()
        me = 4 * x + 2 * y + c
        slots_ref[me] = g_ref[...]
        cps = []
        for rel in range(1, N_DEV):
            fx, fy, fc = (rel >> 2) & 1, (rel >> 1) & 1, rel & 1
            to = (1 - x if fx else x, 1 - y if fy else y, 1 - c if fc else c)
            cps.append(pltpu.make_async_remote_copy(
                src_ref=g_ref, dst_ref=slots_ref.at[me], send_sem=send_sems.at[rel - 1],
                recv_sem=recv_sems.at[rel - 1], device_id=to, device_id_type=MESH))
        for cp in cps:
            cp.start()
        for cp in cps:
            cp.wait()
        g = slots_ref[0]
        for i in range(1, N_DEV):
            g = g + slots_ref[i]
        sum_ref[...] = g

    vm = pl.BlockSpec(memory_space=pltpu.VMEM)
    return pl.pallas_call(
        body, name="small_allreduce", out_shape=jax.ShapeDtypeStruct((1, SMALL_WIDTH), F32),
        in_specs=[vm], out_specs=vm,
        scratch_shapes=[pltpu.VMEM((N_DEV, 1, SMALL_WIDTH), F32), pltpu.SemaphoreType.DMA((N_DEV - 1,)),
                        pltpu.SemaphoreType.DMA((N_DEV - 1,))],
    )(gp)


def _small_adamw(gsum, ws, ms, vs):
    n = len(SMALL)

    def body(g_ref, *rest):
        w_refs, m_refs, v_refs, outs = rest[:n], rest[n:2 * n], rest[2 * n:3 * n], rest[3 * n:]
        for j, size in enumerate(SMALL_SIZES):
            g = g_ref[:, SMALL_OFFSETS[j]:SMALL_OFFSETS[j] + size]
            results = (g,) + _adamw_math(w_refs[j][...], g, m_refs[j][...], v_refs[j][...])
            for kind, val in enumerate(results):
                outs[kind * n + j][...] = val
        outs[4 * n][...] = g_ref[:, SMALL_OFFSETS[-1]:SMALL_WIDTH]

    vm = pl.BlockSpec(memory_space=pltpu.VMEM)
    shapes = tuple(jax.ShapeDtypeStruct((1, s), F32) for s in SMALL_SIZES) * 4
    return pl.pallas_call(
        body, name="small_adamw", out_shape=shapes + (jax.ShapeDtypeStruct((1, LANES), F32),),
        in_specs=[vm] * (1 + 3 * n), out_specs=(vm,) * (4 * n + 1),
    )(gsum, *ws, *ms, *vs)


WEIGHTS = ("attn_norm", "w_in", "b_gate", "w_pool", "pool_scale", "q_norm", "k_norm", "sinks", "w_out", "ffn_norm",
           "w_up", "conv_w", "conv_b", "w_down")


def kernel(x, positions, attn_norm, w_in, b_gate, w_pool, pool_scale, q_norm, k_norm, sinks, w_out, ffn_norm, w_up, conv_w, conv_b, w_down, loss_target, m_attn_norm, m_w_in, m_b_gate, m_w_pool, m_pool_scale, m_q_norm, m_k_norm, m_sinks, m_w_out, m_ffn_norm, m_w_up, m_conv_w, m_conv_b, m_w_down, v_attn_norm, v_w_in, v_b_gate, v_w_pool, v_pool_scale, v_q_norm, v_k_norm, v_sinks, v_w_out, v_ffn_norm, v_w_up, v_conv_w, v_conv_b, v_w_down):
    w = dict(attn_norm=attn_norm, w_in=w_in, b_gate=b_gate, w_pool=w_pool, pool_scale=pool_scale, q_norm=q_norm,
             k_norm=k_norm, sinks=sinks, w_out=w_out, ffn_norm=ffn_norm, w_up=w_up, conv_w=conv_w, conv_b=conv_b,
             w_down=w_down)
    m = dict(attn_norm=m_attn_norm, w_in=m_w_in, b_gate=m_b_gate, w_pool=m_w_pool, pool_scale=m_pool_scale,
             q_norm=m_q_norm, k_norm=m_k_norm, sinks=m_sinks, w_out=m_w_out, ffn_norm=m_ffn_norm, w_up=m_w_up,
             conv_w=m_conv_w, conv_b=m_conv_b, w_down=m_w_down)
    v = dict(attn_norm=v_attn_norm, w_in=v_w_in, b_gate=v_b_gate, w_pool=v_w_pool, pool_scale=v_pool_scale,
             q_norm=v_q_norm, k_norm=v_k_norm, sinks=v_sinks, w_out=v_w_out, ffn_norm=v_ffn_norm, w_up=v_w_up,
             conv_w=v_conv_w, conv_b=v_conv_b, w_down=v_w_down)
    seq = x.shape[1]
    tm = 256
    tw = min(seq, 512)
    tk = min(seq, 1024)
    xs, target, pos_col = x[0], loss_target[0], positions.reshape(seq, 1)
    invf, bd, q_norm_t, k_norm_t, sink_rows = _attention_constants(q_norm, k_norm, sinks)
    out, done = {}, {}
    nat = {"w_in": (D_MODEL, 544), "w_pool": (128, POOL_GROUP), "w_out": (128, D_MODEL), "w_up": (D_MODEL, 704),
           "conv_w": (3, 704), "w_down": (352, D_MODEL)}

    def update(names, owns, recvs):
        for name, own, recv in zip(names, owns, recvs):
            w2, m2, v2 = (t[name].reshape(nat[name]) for t in (w, m, v))
            if name in ("w_in", "w_up"):
                res = _adamw(own, recv, w2.T, m2.T, v2.T)
                done[name] = res[1]
                res = [t.T for t in res]
            else:
                res = _adamw(own, recv, w2, m2, v2)
                done[name] = res[1]
            out[name] = [t.reshape(w[name].shape) for t in res]

    (g_win,) = _all_gather([w_in[0].T.astype(MM)])
    win_t = g_win.reshape(IN_WIDTH, D_MODEL)
    fwd = {}
    token, gather_rest = _all_gather_behind(
        [w_pool[0].astype(MM).reshape(128, POOL_GROUP), w_out[0].astype(MM), w_up[0].T.astype(MM), conv_w[0],
         w_down[0].astype(MM)], g_win, lambda: fwd["b"])

    tabs = _rope_tables(pos_col, invf)
    h, u, q, kv, g = _inproj_fwd(xs, attn_norm + token[0:1, 0:1], win_t, b_gate, tw)
    b, probs, sink_probs = _attn_fwd(q, kv, tabs, q_norm_t, k_norm_t, sink_rows, bd)
    fwd["b"] = b
    g_wpool, g_wout, g_wup, g_convw, g_wdown = gather_rest()
    wpool = g_wpool.reshape(N_DEV, 4, 32, POOL_GROUP).transpose(1, 0, 2, 3).reshape(4, POOL_GROUP, POOL_GROUP)
    wout = g_wout.reshape(D_MODEL, D_MODEL)
    wup_t = g_wup.reshape(2 * D_FF, D_MODEL)
    convw = g_convw.transpose(1, 0, 2).reshape(3, 2 * D_FF)
    wdown = g_wdown.reshape(D_FF, D_MODEL)
    a = _pool_fwd(u, wpool, pool_scale, tw)
    x1, mix = _mix_out_fwd(xs, g, a, b, wout, tw)
    h2, uff, ucv, dy, dyb, lossp = _ffn_fwd(x1, ffn_norm, wup_t, convw, conv_b, wdown, target, tm)

    du, act, d_conv_w, d_conv_b = _ffn_bwd_a(dyb, uff, ucv, convw, wdown, tm)
    d_wdown = _matmul_tn(act, dyb, FF_CHUNK, tk, "dw_down")
    dx1, dx1b, d_ffn_norm = _ffn_bwd_b(du, wup_t, x1, ffn_norm, dy, tm)
    d_wup_t = _matmul_tn(du, h2, FF_CHUNK, tk, "dw_up")
    late = ("w_down", "w_up", "conv_w")
    late_wire = (WIRE, WIRE, F32)
    late_full = [d_wdown.reshape(N_DEV, 352, D_MODEL), d_wup_t.reshape(N_DEV, 704, D_MODEL),
                 d_conv_w.reshape(3, N_DEV, 704).transpose(1, 0, 2)]
    token, late_pair = _symmetric_exchange("late_pair", late_full, 4, _pair_copies)
    da, db, dzg, d_b_gate, d_wout, d_win_t = _mix_bwd(dx1b, wout, g, a, b, mix, h, tm, after=token)
    late_pw, late_own = zip(*[_pair_add(f, r, wd) for f, r, wd in zip(*late_pair(dzg), late_wire)])
    token, late_chip = _symmetric_exchange("late_chip", list(late_pw), 3, _chip_copies)
    dzu, d_wpool, d_pool_scale, d_win_t = _pool_bwd(u, da, wpool, pool_scale, h, d_win_t, tw, after=token)
    dzq, dzkv, d_q_norm, d_k_norm, d_sinks = _attn_bwd(q, kv, db, tabs, q_norm_t, k_norm_t, probs, sink_probs, bd,
                                                       after=token)
    d_win_t = _matmul_tn(dzq, h, 1024, tk, "dw_in_q", into=d_win_t, row0=O_Q)
    d_win_t = _matmul_tn(dzkv, h, 256, tk, "dw_in_kv", into=d_win_t, row0=O_KV)
    early = ("w_in", "w_pool", "w_out")
    early_full = [d_win_t.reshape(N_DEV, 544, D_MODEL),
                  d_wpool.reshape(4, N_DEV, 32, POOL_GROUP).transpose(1, 0, 2, 3).reshape(N_DEV, 128, POOL_GROUP),
                  d_wout.reshape(N_DEV, 128, D_MODEL)]
    token, early_pair = _symmetric_exchange("early_pair", early_full, 4, _pair_copies)
    update(late, late_own, late_chip(token)[1])
    early_pw, early_own = zip(*[_pair_add(f, r, WIRE) for f, r in zip(*early_pair([done[n] for n in late]))])
    token, early_chip = _symmetric_exchange("early_chip", list(early_pw), 3, _chip_copies)
    grad_x, d_attn_norm = _inproj_bwd(dzu, dzq, dzkv, dzg, win_t, xs, attn_norm, dx1, tw, after=token)
    gr = dict(attn_norm=d_attn_norm, b_gate=d_b_gate, pool_scale=d_pool_scale, q_norm=d_q_norm, k_norm=d_k_norm,
              sinks=d_sinks[:, 0:N_Q_HEADS], ffn_norm=d_ffn_norm, conv_b=d_conv_b)
    small = _small_adamw(_small_allreduce(_pack_small(gr, lossp[0, 0])), *[[t[n] for n in SMALL] for t in (w, m, v)])
    loss = small[-1][0, 0]
    for j, name in enumerate(SMALL):
        out[name] = [small[kind * len(SMALL) + j] for kind in range(4)]
    update(early, early_own, early_chip(small[0])[1])

    return (loss, grad_x[None], *[out[n][0] for n in WEIGHTS], *[out[n][1] for n in WEIGHTS],
            *[out[n][2] for n in WEIGHTS], *[out[n][3] for n in WEIGHTS])
```

```python
import functools

import numpy as np
import jax
import jax.numpy as jnp
from jax import lax
from jax.experimental import pallas as pl
from jax.experimental.pallas import tpu as pltpu

F32 = jnp.float32
MM = jnp.bfloat16
WIRE = jnp.bfloat16
ACT = jnp.bfloat16

D_MODEL = 1024
D_FF = 2816
HEAD_DIM = 64
N_Q_HEADS = 16
N_KV_HEADS = 2
GQA_GROUP = 8
BLOCK = 128
ROPE_DIM = 16
ROPE_THETA = 500000.0
POOL_WINDOWS = (2, 4, 8, 16)
POOL_GROUP = 256
POOL_HALO = 32
EPS = 1e-6
NEG = -1e30
O_U, O_Q, O_KV, O_G, IN_WIDTH = 0, 1024, 2048, 2304, 4352
FF_CHUNK = 1408

ADAM_LR, ADAM_B1, ADAM_B2, ADAM_EPS, ADAM_WD, ADAM_STEP = 0.001, 0.9, 0.999, 1e-08, 0.01, 10

N_DEV = 8
LANES = 128
VMEM_LIMIT_BYTES = 56 * 1024 * 1024
MESH = pl.DeviceIdType.MESH


def _params(*sem):
    return pltpu.CompilerParams(dimension_semantics=sem, vmem_limit_bytes=VMEM_LIMIT_BYTES)


def _resident(shape):
    nd = len(shape)
    return pl.BlockSpec(shape, lambda *_: (0,) * nd, pipeline_mode=pl.Buffered(1))


def _rows(tm, width):
    return pl.BlockSpec((tm, width), lambda i: (i, 0))


def _mm(a, b):
    return jnp.dot(a.astype(MM), b.astype(MM), preferred_element_type=F32)


def _mm_nt(a, b):
    return lax.dot_general(a.astype(MM), b.astype(MM), (((1,), (1,)), ((), ())), preferred_element_type=F32)


def _mm_tn(a, b):
    return lax.dot_general(a.astype(MM), b.astype(MM), (((0,), (0,)), ((), ())), preferred_element_type=F32)


def _rmsnorm_fwd(x, g):
    r = lax.rsqrt(jnp.mean(x * x, axis=-1, keepdims=True) + EPS)
    return x * r * g, r


def _rmsnorm_bwd(x, r, g, dy):
    xn = x * r
    dxn = dy * g
    dx = r * (dxn - xn * jnp.mean(dxn * xn, axis=-1, keepdims=True))
    return dx, dy * xn


def _group_sum64(v, bd):
    hi = v.astype(MM)
    lo = (v - hi.astype(F32)).astype(MM)
    outs = []
    for t in range(v.shape[1] // LANES):
        sl = slice(LANES * t, LANES * (t + 1))
        outs.append(jnp.dot(hi[:, sl], bd, preferred_element_type=F32)
                    + jnp.dot(lo[:, sl], bd, preferred_element_type=F32))
    return outs[0] if len(outs) == 1 else jnp.concatenate(outs, axis=1)


def _head_norm_fwd(x, g, bd):
    r = lax.rsqrt(_group_sum64(x * x, bd) * (1.0 / HEAD_DIM) + EPS)
    return x * r * g, r


def _head_norm_bwd(x, r, g, dy, bd):
    xn = x * r
    dxn = dy * g
    dx = r * (dxn - xn * (_group_sum64(dxn * xn, bd) * (1.0 / HEAD_DIM)))
    return dx, dy * xn


def _rope(x, c, s1, s2):
    w = x.shape[1]
    return x * c + pltpu.roll(x, w - ROPE_DIM // 2, 1) * s1 + pltpu.roll(x, ROPE_DIM // 2, 1) * s2


def _rope_bwd(dy, c, s1, s2):
    w = dy.shape[1]
    return dy * c + pltpu.roll(dy * s1, ROPE_DIM // 2, 1) + pltpu.roll(dy * s2, w - ROPE_DIM // 2, 1)


def _tile_lanes(t, reps):
    return t if reps == 1 else jnp.concatenate([t] * reps, axis=1)


def _rope_tables(pos_col, invf):
    s = pos_col.shape[0]
    tm = min(s, 1024)

    def body(pos_ref, invf_ref, c_ref, s1_ref, s2_ref):
        ang = pos_ref[...].astype(F32) * invf_ref[...]
        lane = lax.broadcasted_iota(jnp.int32, ang.shape, 1) % HEAD_DIM
        sn = jnp.sin(ang)
        c_ref[...] = jnp.cos(ang)
        s1_ref[...] = jnp.where(lane < ROPE_DIM // 2, -sn, 0.0)
        s2_ref[...] = jnp.where((lane >= ROPE_DIM // 2) & (lane < ROPE_DIM), sn, 0.0)

    out = jax.ShapeDtypeStruct((s, LANES), F32)
    return pl.pallas_call(
        body, name="rope_tables", grid=(s // tm,), out_shape=(out, out, out),
        in_specs=[_rows(tm, 1), _resident((1, LANES))],
        out_specs=(_rows(tm, LANES),) * 3, compiler_params=_params("parallel"),
    )(pos_col, invf)


def _inproj_fwd(x, attn_norm, win_t, b_gate, tm):
    s = x.shape[0]

    def body(x_ref, gn_ref, w_ref, bg_ref, h_ref, u_ref, q_ref, kv_ref, g_ref):
        h, _ = _rmsnorm_fwd(x_ref[...], gn_ref[...])
        h = h.astype(MM)
        h_ref[...] = h
        u_ref[...] = _mm_nt(h, w_ref[O_U:O_Q, :])
        q_ref[...] = _mm_nt(h, w_ref[O_Q:O_KV, :])
        kv_ref[...] = _mm_nt(h, w_ref[O_KV:O_G, :])
        g_ref[...] = jax.nn.sigmoid(_mm_nt(h, w_ref[O_G:IN_WIDTH, :]) + bg_ref[...])

    sd = jax.ShapeDtypeStruct
    return pl.pallas_call(
        body, name="inproj_fwd", grid=(s // tm,),
        out_shape=(sd((s, D_MODEL), MM), sd((s, 1024), F32), sd((s, 1024), F32), sd((s, 256), F32),
                   sd((s, 2048), F32)),
        in_specs=[_rows(tm, D_MODEL), _resident((1, D_MODEL)), _resident((IN_WIDTH, D_MODEL)), _resident((1, 2048))],
        out_specs=(_rows(tm, D_MODEL), _rows(tm, 1024), _rows(tm, 1024), _rows(tm, 256), _rows(tm, 2048)),
        compiler_params=_params("parallel"),
    )(x, attn_norm, win_t, b_gate)


def _window_sums(src_ref, s1_ref, s2_ref, cols, w, tm, ahead):
    n = POOL_HALO + tm

    def level(src, width, lo):
        if ahead:
            return src(0, n - lo) + src(width, n - lo)
        return src(lo, n - lo) + src(lo - width, n - lo)

    def final(src, width):
        if ahead:
            return src(0, tm) + src(width, tm)
        return src(POOL_HALO, tm) + src(POOL_HALO - width, tm)

    levels = [lambda r0, rows: src_ref[pl.ds(r0, rows), cols]]
    for k, ref in ((1, s1_ref), (2, s2_ref), (3, s1_ref)):
        if w == 2 ** k:
            break
        lo = 8 * k
        ref[pl.ds(0 if ahead else lo, n - lo), :] = level(levels[-1], 2 ** (k - 1), lo)
        levels.append(functools.partial(lambda ref, r0, rows: ref[pl.ds(r0, rows), :], ref))
    return final(levels[-1], w // 2)


def _pooled(ext_ref, s1_ref, s2_ref, tm, row0):
    t = (row0 + lax.broadcasted_iota(jnp.int32, (tm, 1), 0)).astype(F32)
    out = []
    for gi, w in enumerate(POOL_WINDOWS):
        cols = slice(gi * POOL_GROUP, (gi + 1) * POOL_GROUP)
        acc = _window_sums(ext_ref, s1_ref, s2_ref, cols, w, tm, ahead=False)
        cnt = jnp.minimum(t + 1.0, float(w))
        out.append(acc / cnt - ext_ref[pl.ds(POOL_HALO, tm), cols])
    return out


def _pool_fwd(u, wpool, pool_scale, tm):
    s = u.shape[0]
    hb = tm // POOL_HALO

    def body(u_ref, halo_ref, wp_ref, ps_ref, a_ref, ext_ref, s1_ref, s2_ref):
        i = pl.program_id(0)
        ext_ref[pl.ds(0, POOL_HALO), :] = jnp.where(i > 0, halo_ref[...], 0.0)
        ext_ref[pl.ds(POOL_HALO, tm), :] = u_ref[...]
        pooled = _pooled(ext_ref, s1_ref, s2_ref, tm, i * tm)
        for gi in range(4):
            cols = slice(gi * POOL_GROUP, (gi + 1) * POOL_GROUP)
            a_ref[:, cols] = (_mm(pooled[gi], wp_ref[gi]) * ps_ref[:, cols]).astype(a_ref.dtype)

    return pl.pallas_call(
        body, name="pool_fwd", grid=(s // tm,), out_shape=jax.ShapeDtypeStruct((s, 1024), ACT),
        in_specs=[_rows(tm, 1024), pl.BlockSpec((POOL_HALO, 1024), lambda i: (jnp.maximum(i * hb - 1, 0), 0)),
                  _resident((4, POOL_GROUP, POOL_GROUP)), _resident((1, 1024))],
        out_specs=_rows(tm, 1024),
        scratch_shapes=[pltpu.VMEM((POOL_HALO + tm, 1024), F32), pltpu.VMEM((POOL_HALO + tm, POOL_GROUP), F32),
                        pltpu.VMEM((POOL_HALO + tm, POOL_GROUP), F32)],
        compiler_params=_params("parallel"),
    )(u, u, wpool, pool_scale)


PAIRS = GQA_GROUP // 2
PAIR_COLS = PAIRS * BLOCK


def _fold_masks(n):
    r = lax.broadcasted_iota(jnp.int32, (BLOCK, BLOCK), 0)
    i = lax.broadcasted_iota(jnp.int32, (BLOCK, BLOCK), 1)
    prev = r > i
    return prev, jnp.where(prev & (n == 0), NEG, 0.0)


def _fold(band, prev):
    return jnp.where(prev, band[0:BLOCK, :], band[BLOCK:2 * BLOCK, :])


def _unfold(folded, prev):
    top = jnp.where(prev, folded, 0.0)
    return jnp.concatenate([top, folded - top], axis=0).astype(MM)


def _probs_by_pair(sc, sink, masks):
    prev, bias = masks
    out = []
    for j in range(PAIRS):
        cols = slice(j * BLOCK, (j + 1) * BLOCK)
        out.append(_softmax_sink_t(_fold(sc[:, cols], prev) + bias, sink[:, cols]))
    return out


def _stack_pairs(x, hk):
    return jnp.concatenate([x[:, (PAIRS * hk + j) * LANES:(PAIRS * hk + j + 1) * LANES] for j in range(PAIRS)], axis=0)


def _parity_bands(t, hk):
    low = lax.broadcasted_iota(jnp.int32, t.shape, 1) < HEAD_DIM
    own = jnp.where(low if hk == 0 else ~low, t, 0.0)
    other = pltpu.roll(own, HEAD_DIM, 1)
    return (own, other) if hk == 0 else (other, own)


def _fold_parity(even, odd, hk):
    low = lax.broadcasted_iota(jnp.int32, even.shape, 1) < HEAD_DIM
    comb = jnp.where(low, even, odd)
    comb = comb + pltpu.roll(comb, HEAD_DIM, 1)
    return jnp.where(low if hk == 0 else ~low, comb, 0.0)


def _softmax_sink_t(s, sink):
    m = jnp.maximum(jnp.max(s, axis=0, keepdims=True), sink)
    p = jnp.exp(s - m)
    es = jnp.exp(sink - m)
    inv = 1.0 / (jnp.sum(p, axis=0, keepdims=True) + es)
    return p * inv, es * inv


def _attn_fwd(q, kv, tabs, q_norm_t, k_norm_t, sink_rows, bd):
    s = q.shape[0]
    nb = s // BLOCK
    scale = HEAD_DIM ** -0.5
    cur = lambda n: (n, 0)
    prv = lambda n: (jnp.maximum(n - 1, 0), 0)

    def body(q_ref, kvc_ref, kvp_ref, c_ref, s1_ref, s2_ref, cp_ref, s1p_ref, s2p_ref, qn_ref, kn_ref, sink_ref,
             bd_ref, o_ref, pr_ref, ps_ref):
        n = pl.program_id(0)
        bdm = bd_ref[...]
        c, s1, s2 = c_ref[...], s1_ref[...], s2_ref[...]
        qh, _ = _head_norm_fwd(q_ref[...], qn_ref[...], bdm)
        qr = (_rope(qh, _tile_lanes(c, 8), _tile_lanes(s1, 8), _tile_lanes(s2, 8)) * scale).astype(MM)
        kc, _ = _head_norm_fwd(kvc_ref[:, 0:128], kn_ref[...], bdm)
        kp, _ = _head_norm_fwd(kvp_ref[:, 0:128], kn_ref[...], bdm)
        k2 = jnp.concatenate([_rope(kp, cp_ref[...], s1p_ref[...], s2p_ref[...]), _rope(kc, c, s1, s2)], axis=0)
        v2 = jnp.concatenate([kvp_ref[:, 128:256], kvc_ref[:, 128:256]], axis=0)
        mask = _fold_masks(n)
        for hk in range(N_KV_HEADS):
            qs = _stack_pairs(qr, hk)
            ot = jnp.zeros((LANES, PAIR_COLS), F32)
            for par, (kb, vb) in enumerate(zip(_parity_bands(k2, hk), _parity_bands(v2, hk))):
                probs = _probs_by_pair(_mm_nt(kb, qs), sink_ref[hk, par], mask)
                pr_ref[0, 2 * hk + par] = jnp.concatenate([pr for pr, _ in probs], axis=1).astype(MM)
                ps_ref[0, 2 * hk + par] = jnp.concatenate([ps for _, ps in probs], axis=1)
                ot = ot + _mm(vb.T, jnp.concatenate([_unfold(pr, mask[0]) for pr, _ in probs], axis=1))
            for j in range(PAIRS):
                col = (PAIRS * hk + j) * LANES
                o_ref[:, col:col + LANES] = ot[:, j * BLOCK:(j + 1) * BLOCK].T.astype(o_ref.dtype)

    tab = lambda im: pl.BlockSpec((BLOCK, LANES), im)
    sd = jax.ShapeDtypeStruct
    return pl.pallas_call(
        body, name="attn_fwd", grid=(nb,),
        out_shape=(sd((s, 1024), ACT), sd((nb, 4, BLOCK, PAIR_COLS), MM), sd((nb, 4, 1, PAIR_COLS), F32)),
        in_specs=[pl.BlockSpec((BLOCK, 1024), cur), pl.BlockSpec((BLOCK, 256), cur), pl.BlockSpec((BLOCK, 256), prv),
                  tab(cur), tab(cur), tab(cur), tab(prv), tab(prv), tab(prv),
                  _resident((1, 1024)), _resident((1, 128)), _resident((N_KV_HEADS, 2, 1, PAIR_COLS)),
                  _resident((LANES, LANES))],
        out_specs=(pl.BlockSpec((BLOCK, 1024), cur), pl.BlockSpec((1, 4, BLOCK, PAIR_COLS), lambda n: (n, 0, 0, 0)),
                   pl.BlockSpec((1, 4, 1, PAIR_COLS), lambda n: (n, 0, 0, 0))),
        compiler_params=_params("parallel"),
    )(q, kv, kv, *tabs, *tabs, q_norm_t, k_norm_t, sink_rows, bd)


def _mix_out_fwd(x, g, a, b, wout, tm):
    s = x.shape[0]

    def body(x_ref, g_ref, a_ref, b_ref, w_ref, x1_ref, mix_ref):
        mix = (g_ref[:, 0:1024] * a_ref[...] + g_ref[:, 1024:2048] * b_ref[...]).astype(MM)
        mix_ref[...] = mix
        x1_ref[...] = x_ref[...] + _mm(mix, w_ref[...])

    return pl.pallas_call(
        body, name="mix_out_fwd", grid=(s // tm,),
        out_shape=(jax.ShapeDtypeStruct((s, D_MODEL), F32), jax.ShapeDtypeStruct((s, D_MODEL), MM)),
        in_specs=[_rows(tm, 1024), _rows(tm, 2048), _rows(tm, 1024), _rows(tm, 1024), _resident((1024, 1024))],
        out_specs=(_rows(tm, 1024), _rows(tm, 1024)), compiler_params=_params("parallel"),
    )(x, g, a, b, wout)


SHIFT_ROWS = 16


def _sublane_major_matrices(tm):
    r = np.arange(tm)
    pm = r[None, :] == ((tm // 8) * (r % 8) + r // 8)[:, None]
    return jnp.asarray(pm, MM), jnp.asarray(pm.T, MM)


def _to_sublane_major(pm, v):
    return jnp.dot(pm, v, preferred_element_type=F32).astype(MM)


def _to_time_order(pmt, v):
    hi = v.astype(MM)
    r1 = v - hi.astype(F32)
    mid = r1.astype(MM)
    lo = (r1 - mid.astype(F32)).astype(MM)
    dot = functools.partial(jnp.dot, preferred_element_type=F32)
    return dot(pmt, hi) + dot(pmt, mid) + dot(pmt, lo)


def _step_back(vreg_rows, before):
    sub = lax.broadcasted_iota(jnp.int32, vreg_rows.shape, 0)
    return jnp.where(sub == 0, before[7:8, :], pltpu.roll(vreg_rows, 1, 0))


def _step_ahead(vreg_rows, after):
    sub = lax.broadcasted_iota(jnp.int32, vreg_rows.shape, 0)
    return jnp.where(sub == 7, after[0:1, :], pltpu.roll(vreg_rows, 7, 0))


def _fill_back_rows(ext_ref, before, tm, cols):
    last = ext_ref[pl.ds(SHIFT_ROWS + tm - 8, 8), cols]
    pen = ext_ref[pl.ds(SHIFT_ROWS + tm - 16, 8), cols]
    ext_ref[pl.ds(8, 8), cols] = _step_back(last, before[8:16, :])
    ext_ref[pl.ds(0, 8), cols] = _step_back(pen, before[0:8, :])


def _causal_conv(uc, before, w, b, tm):
    back1 = _step_back(uc[tm - 8:tm, :], before[8:16, :])
    back2 = _step_back(uc[tm - 16:tm - 8, :], before[0:8, :])
    w0, w1, w2 = w[0:1, :], w[1:2, :], w[2:3, :]
    first = b + w0 * back2 + w1 * back1 + w2 * uc[0:8, :]
    second = b + w0 * back1 + w1 * uc[0:8, :] + w2 * uc[8:16, :]
    rest = b + w0 * uc[0:tm - 16, :] + w1 * uc[8:tm - 8, :] + w2 * uc[16:tm, :]
    return jnp.concatenate([first, second, rest], axis=0)


def _ffn_fwd(x1, ffn_norm, wup_t, conv_w, conv_b, wdown, target, tm):
    s = x1.shape[0]
    inv_d = 1.0 / D_MODEL
    pm, pmt = _sublane_major_matrices(tm)

    def body(x1_ref, gn_ref, wu_ref, cw_ref, cb_ref, wd_ref, tgt_ref, pm_ref, pmt_ref, h2_ref, u_ref, uc_ref, dy_ref,
             dyb_ref, loss_ref, carry_ref):
        i = pl.program_id(0)

        @pl.when(i == 0)
        def _():
            carry_ref[...] = jnp.zeros_like(carry_ref)
            loss_ref[...] = jnp.zeros_like(loss_ref)

        x1 = x1_ref[...]
        h2, _ = _rmsnorm_fwd(x1, gn_ref[...])
        h2 = _to_sublane_major(pm_ref[...], h2.astype(MM))
        h2_ref[...] = h2
        for c in range(4):
            cols = slice(c * FF_CHUNK, (c + 1) * FF_CHUNK)
            uc = _mm_nt(h2, wu_ref[cols, :])
            u_ref[:, cols] = uc
            uc_ref[:, cols] = _causal_conv(uc, carry_ref[:, cols], cw_ref[:, cols], cb_ref[:, cols], tm)
            carry_ref[:, cols] = uc[tm - SHIFT_ROWS:tm, :]
        down = jnp.zeros((tm, D_MODEL), F32)
        for c in range(2):
            gate = uc_ref[:, c * FF_CHUNK:(c + 1) * FF_CHUNK]
            val = uc_ref[:, D_FF + c * FF_CHUNK:D_FF + (c + 1) * FF_CHUNK]
            act = gate * jax.nn.sigmoid(gate) * val
            down = down + _mm(act, wd_ref[c * FF_CHUNK:(c + 1) * FF_CHUNK, :])
        err = x1 + _to_time_order(pmt_ref[...], down) - tgt_ref[...]
        loss_ref[...] += jnp.full(loss_ref.shape, 0.5 * inv_d * jnp.sum(err * err), F32)
        dy = err * inv_d
        dy_ref[...] = dy
        dyb_ref[...] = _to_sublane_major(pm_ref[...], dy.astype(MM))

    sd = jax.ShapeDtypeStruct
    return pl.pallas_call(
        body, name="ffn_fwd", grid=(s // tm,),
        out_shape=(sd((s, D_MODEL), MM), sd((s, 2 * D_FF), F32), sd((s, 2 * D_FF), F32), sd((s, D_MODEL), F32),
                   sd((s, D_MODEL), MM), sd((8, LANES), F32)),
        in_specs=[_rows(tm, 1024), _resident((1, 1024)), _resident((2 * D_FF, D_MODEL)), _resident((3, 2 * D_FF)),
                  _resident((1, 2 * D_FF)), _resident((D_FF, D_MODEL)), _rows(tm, 1024), _resident((tm, tm)),
                  _resident((tm, tm))],
        out_specs=(_rows(tm, 1024), _rows(tm, 2 * D_FF), _rows(tm, 2 * D_FF), _rows(tm, 1024), _rows(tm, 1024),
                   pl.BlockSpec((8, LANES), lambda i: (0, 0))),
        scratch_shapes=[pltpu.VMEM((SHIFT_ROWS, 2 * D_FF), F32)],
        compiler_params=_params("arbitrary"),
    )(x1, ffn_norm, wup_t, conv_w, conv_b, wdown, target, pm, pmt)


def _ffn_bwd_a(dyb, u, uc, conv_w, wdown, tm):
    s = dyb.shape[0]
    nt = s // tm
    hb = tm // SHIFT_ROWS
    rev = lambda i: (nt - 1 - i, 0)

    def body(dy_ref, u_ref, before_ref, uc_ref, cw_ref, wd_ref, du_ref, act_ref, dcw_ref, dcb_ref, ext_ref, extd_ref,
             ahead_ref):
        i = pl.program_id(0)
        first_tile = i == nt - 1

        @pl.when(i == 0)
        def _():
            ahead_ref[...] = jnp.zeros_like(ahead_ref)
            dcw_ref[...] = jnp.zeros_like(dcw_ref)
            dcb_ref[...] = jnp.zeros_like(dcb_ref)

        ext_ref[pl.ds(SHIFT_ROWS, tm), :] = u_ref[...]
        for c in range(4):
            cols = slice(c * FF_CHUNK, (c + 1) * FF_CHUNK)
            _fill_back_rows(ext_ref, jnp.where(first_tile, 0.0, before_ref[:, cols]), tm, cols)
        dy = dy_ref[...]
        for c in range(2):
            gate = uc_ref[:, c * FF_CHUNK:(c + 1) * FF_CHUNK]
            val = uc_ref[:, D_FF + c * FF_CHUNK:D_FF + (c + 1) * FF_CHUNK]
            sg = jax.nn.sigmoid(gate)
            sl = gate * sg
            act_ref[:, c * FF_CHUNK:(c + 1) * FF_CHUNK] = (sl * val).astype(MM)
            d_act = _mm_nt(dy, wd_ref[c * FF_CHUNK:(c + 1) * FF_CHUNK, :])
            extd_ref[pl.ds(0, tm), c * FF_CHUNK:(c + 1) * FF_CHUNK] = d_act * val * (sg * (1.0 + gate * (1.0 - sg)))
            extd_ref[pl.ds(0, tm), D_FF + c * FF_CHUNK:D_FF + (c + 1) * FF_CHUNK] = d_act * sl
        for c in range(4):
            cols = slice(c * FF_CHUNK, (c + 1) * FF_CHUNK)
            ahead = ahead_ref[:, cols]
            first2 = extd_ref[pl.ds(0, SHIFT_ROWS), cols]
            extd_ref[pl.ds(tm, 8), cols] = _step_ahead(first2[0:8, :], ahead[0:8, :])
            extd_ref[pl.ds(tm + 8, 8), cols] = _step_ahead(first2[8:16, :], ahead[8:16, :])
            ahead_ref[:, cols] = first2
            d0 = extd_ref[pl.ds(0, tm), cols]
            dcb_ref[:, cols] += jnp.sum(d0, axis=0, keepdims=True)
            for j in range(3):
                dcw_ref[j:j + 1, cols] += jnp.sum(d0 * ext_ref[pl.ds(8 * j, tm), cols], axis=0, keepdims=True)
            du = cw_ref[2:3, cols] * d0 + cw_ref[1:2, cols] * extd_ref[pl.ds(8, tm), cols]
            du = du + cw_ref[0:1, cols] * extd_ref[pl.ds(SHIFT_ROWS, tm), cols]
            du_ref[:, cols] = du.astype(MM)

    sd = jax.ShapeDtypeStruct
    return pl.pallas_call(
        body, name="ffn_bwd_a", grid=(nt,),
        out_shape=(sd((s, 2 * D_FF), MM), sd((s, D_FF), MM), sd((3, 2 * D_FF), F32), sd((1, 2 * D_FF), F32)),
        in_specs=[pl.BlockSpec((tm, D_MODEL), rev), pl.BlockSpec((tm, 2 * D_FF), rev),
                  pl.BlockSpec((SHIFT_ROWS, 2 * D_FF), lambda i: (jnp.maximum((nt - 1 - i) * hb - 1, 0), 0)),
                  pl.BlockSpec((tm, 2 * D_FF), rev), _resident((3, 2 * D_FF)), _resident((D_FF, D_MODEL))],
        out_specs=(pl.BlockSpec((tm, 2 * D_FF), rev), pl.BlockSpec((tm, D_FF), rev),
                   pl.BlockSpec((3, 2 * D_FF), lambda i: (0, 0)), pl.BlockSpec((1, 2 * D_FF), lambda i: (0, 0))),
        scratch_shapes=[pltpu.VMEM((SHIFT_ROWS + tm, 2 * D_FF), F32), pltpu.VMEM((tm + SHIFT_ROWS, 2 * D_FF), F32),
                        pltpu.VMEM((SHIFT_ROWS, 2 * D_FF), F32)],
        compiler_params=_params("arbitrary"),
    )(dyb, u, u, uc, conv_w, wdown)


def _after(after):
    tie = [] if after is None else list(after) if isinstance(after, (list, tuple)) else [after]
    return tie, [pl.BlockSpec(memory_space=pl.ANY)] * len(tie)


def _matmul_tn(a, b, tmo, tk, name, into=None, row0=0):
    s, m = a.shape
    n = b.shape[1]
    nk = s // tk
    rows = m if into is None else into.shape[0]
    assert row0 % LANES == 0 and tmo % LANES == 0
    grown, grown_spec = ([], []) if into is None else ([into], [ANY])

    def body(a_ref, b_ref, *rest):
        o_ref = rest[-1]
        k = pl.program_id(1)

        @pl.when(k == 0)
        def _():
            o_ref[...] = jnp.zeros_like(o_ref)

        o_ref[...] += _mm_tn(a_ref[...], b_ref[pl.ds(pl.multiple_of(k * tk, tk), tk), :])

    return pl.pallas_call(
        body, name=name, grid=(m // tmo, nk), out_shape=jax.ShapeDtypeStruct((rows, n), F32),
        in_specs=[pl.BlockSpec((tk, tmo), lambda i, k: (k, i)), _resident((s, n))] + grown_spec,
        out_specs=pl.BlockSpec((pl.Element(tmo), pl.Element(n)), lambda i, k: (pl.multiple_of(row0 + i * tmo, LANES), 0)),
        input_output_aliases={2: 0} if grown else {},
        compiler_params=_params("parallel", "arbitrary"),
    )(a, b, *grown)


def _ffn_bwd_b(du, wup_t, x1, ffn_norm, dy, tm):
    s = du.shape[0]

    def body(du_ref, wu_ref, x1_ref, gn_ref, dy_ref, pmt_ref, dx1_ref, dx1b_ref, dg_ref):
        @pl.when(pl.program_id(0) == 0)
        def _():
            dg_ref[...] = jnp.zeros_like(dg_ref)

        dh2 = _to_time_order(pmt_ref[...], _mm(du_ref[...], wu_ref[...]))
        x1 = x1_ref[...]
        _, r = _rmsnorm_fwd(x1, gn_ref[...])
        dx, dgr = _rmsnorm_bwd(x1, r, gn_ref[...], dh2)
        dg_ref[...] += jnp.sum(dgr, axis=0, keepdims=True)
        dx1 = dy_ref[...] + dx
        dx1_ref[...] = dx1
        dx1b_ref[...] = dx1.astype(MM)

    return pl.pallas_call(
        body, name="ffn_bwd_b", grid=(s // tm,),
        out_shape=(jax.ShapeDtypeStruct((s, D_MODEL), F32), jax.ShapeDtypeStruct((s, D_MODEL), MM),
                   jax.ShapeDtypeStruct((1, D_MODEL), F32)),
        in_specs=[_rows(tm, 2 * D_FF), _resident((2 * D_FF, D_MODEL)), _rows(tm, 1024), _resident((1, 1024)),
                  _rows(tm, 1024), _resident((tm, tm))],
        out_specs=(_rows(tm, 1024), _rows(tm, 1024), pl.BlockSpec((1, D_MODEL), lambda i: (0, 0))),
        compiler_params=_params("arbitrary"),
    )(du, wup_t, x1, ffn_norm, dy, _sublane_major_matrices(tm)[1])


def _win_rows(row0, rows):
    return pl.BlockSpec((pl.Element(rows), pl.Element(D_MODEL)), lambda i: (row0, 0), pipeline_mode=pl.Buffered(1))


def _mix_bwd(dx1b, wout, g, a, b, mix, h, tm, after=None):
    s = dx1b.shape[0]
    tie, tie_spec = _after(after)

    def body(dx_ref, w_ref, g_ref, a_ref, b_ref, mix_ref, h_ref, *rest):
        da_ref, db_ref, dzg_ref, dbg_ref, dwo_ref, dwin_ref = rest[-6:]

        @pl.when(pl.program_id(0) == 0)
        def _():
            dbg_ref[...] = jnp.zeros_like(dbg_ref)
            dwo_ref[...] = jnp.zeros_like(dwo_ref)
            dwin_ref[...] = jnp.zeros_like(dwin_ref)

        dx = dx_ref[...]
        dwo_ref[...] += _mm_tn(mix_ref[...], dx)
        dmix = _mm_nt(dx, w_ref[...])
        for half, src, dst in ((0, a_ref, da_ref), (1, b_ref, db_ref)):
            cols = slice(half * 1024, (half + 1) * 1024)
            gt = g_ref[:, cols]
            dst[...] = (dmix * gt).astype(dst.dtype)
            dz = dmix * src[...] * gt * (1.0 - gt)
            dzb = dz.astype(MM)
            dzg_ref[:, cols] = dzb
            dwin_ref[cols, :] += _mm_tn(dzb, h_ref[...])
            dbg_ref[:, cols] += jnp.sum(dz, axis=0, keepdims=True)

    sd = jax.ShapeDtypeStruct
    return pl.pallas_call(
        body, name="mix_bwd", grid=(s // tm,),
        out_shape=(sd((s, 1024), F32), sd((s, 1024), MM), sd((s, 2048), MM), sd((1, 2048), F32),
                   sd((D_MODEL, D_MODEL), F32), sd((IN_WIDTH, D_MODEL), F32)),
        in_specs=[_rows(tm, 1024), _resident((1024, 1024)), _rows(tm, 2048), _rows(tm, 1024), _rows(tm, 1024),
                  _rows(tm, 1024), _rows(tm, 1024)] + tie_spec,
        out_specs=(_rows(tm, 1024), _rows(tm, 1024), _rows(tm, 2048), pl.BlockSpec((1, 2048), lambda i: (0, 0)),
                   _resident((D_MODEL, D_MODEL)), _win_rows(O_G, IN_WIDTH - O_G)),
        compiler_params=_params("arbitrary"),
    )(dx1b, wout, g, a, b, mix, h, *tie)


def _pool_bwd(u, da, wpool, pool_scale, h, d_win_t, tm, after=None):
    s = u.shape[0]
    nt = s // tm
    hb = tm // POOL_HALO

    tie, tie_spec = _after(after)

    def body(u_ref, uh_ref, da_ref, dah_ref, wp_ref, ps_ref, h_ref, *rest):
        dzu_ref, dwp_ref, dps_ref, dwin_ref, ext_ref, exte_ref, s1_ref, s2_ref = rest[-8:]
        i = pl.program_id(0)

        @pl.when(i == 0)
        def _():
            dwp_ref[...] = jnp.zeros_like(dwp_ref)
            dps_ref[...] = jnp.zeros_like(dps_ref)
            dwin_ref[...] = jnp.zeros_like(dwin_ref)

        ext_ref[pl.ds(0, POOL_HALO), :] = jnp.where(i > 0, uh_ref[...], 0.0)
        ext_ref[pl.ds(POOL_HALO, tm), :] = u_ref[...]
        pooled = _pooled(ext_ref, s1_ref, s2_ref, tm, i * tm)
        da = da_ref[...]
        dah = jnp.where(i < nt - 1, dah_ref[...], 0.0)
        t = (i * tm + lax.broadcasted_iota(jnp.int32, (tm + POOL_HALO, 1), 0)).astype(F32)
        for gi, w in enumerate(POOL_WINDOWS):
            cols = slice(gi * POOL_GROUP, (gi + 1) * POOL_GROUP)
            pg = pooled[gi].astype(MM)
            wg = wp_ref[gi]
            mixed = _mm(pg, wg)
            dps_ref[:, cols] += jnp.sum(da[:, cols] * mixed, axis=0, keepdims=True)
            dmx = (da[:, cols] * ps_ref[:, cols]).astype(MM)
            dwp_ref[gi] += _mm_tn(pg, dmx)
            dpl = _mm_nt(dmx, wg)
            dplh = _mm_nt(dah[:, cols] * ps_ref[:, cols], wg)
            cnt = jnp.minimum(t + 1.0, float(w))
            exte_ref[pl.ds(0, tm), cols] = dpl / cnt[0:tm]
            exte_ref[pl.ds(tm, POOL_HALO), cols] = dplh / cnt[tm:tm + POOL_HALO]
            acc = _window_sums(exte_ref, s1_ref, s2_ref, cols, w, tm, ahead=True)
            dzu = (acc - dpl).astype(MM)
            dzu_ref[:, cols] = dzu
            dwin_ref[cols, :] += _mm_tn(dzu, h_ref[...])

    sd = jax.ShapeDtypeStruct
    last_halo = s // POOL_HALO - 1
    return pl.pallas_call(
        body, name="pool_bwd", grid=(nt,),
        out_shape=(sd((s, 1024), MM), sd((4, POOL_GROUP, POOL_GROUP), F32), sd((1, 1024), F32),
                   sd((IN_WIDTH, D_MODEL), F32)),
        in_specs=[_rows(tm, 1024), pl.BlockSpec((POOL_HALO, 1024), lambda i: (jnp.maximum(i * hb - 1, 0), 0)),
                  _rows(tm, 1024),
                  pl.BlockSpec((POOL_HALO, 1024), lambda i: (jnp.minimum((i + 1) * hb, last_halo), 0)),
                  _resident((4, POOL_GROUP, POOL_GROUP)), _resident((1, 1024)), _rows(tm, 1024)] + tie_spec + [ANY],
        out_specs=(_rows(tm, 1024), pl.BlockSpec((4, POOL_GROUP, POOL_GROUP), lambda i: (0, 0, 0)),
                   pl.BlockSpec((1, 1024), lambda i: (0, 0)), _win_rows(O_U, O_Q - O_U)),
        input_output_aliases={7 + len(tie): 3},
        scratch_shapes=[pltpu.VMEM((POOL_HALO + tm, 1024), F32), pltpu.VMEM((tm + POOL_HALO, 1024), F32),
                        pltpu.VMEM((POOL_HALO + tm, POOL_GROUP), F32), pltpu.VMEM((POOL_HALO + tm, POOL_GROUP), F32)],
        compiler_params=_params("arbitrary"),
    )(u, u, da, da, wpool, pool_scale, h, *tie, d_win_t)


def _attn_bwd(q, kv, db, tabs, q_norm_t, k_norm_t, probs, sink_probs, bd, after=None):
    s = q.shape[0]
    nb = s // BLOCK
    scale = HEAD_DIM ** -0.5
    cur = lambda n: (jnp.minimum(n, nb - 1), 0)
    prv = lambda n: (jnp.maximum(n - 1, 0), 0)
    tie, tie_spec = _after(after)

    def body(q_ref, kvc_ref, kvp_ref, db_ref, c_ref, s1_ref, s2_ref, cp_ref, s1p_ref, s2p_ref, qn_ref, kn_ref,
             pr_ref, ps_ref, bd_ref, *rest):
        (dzq_ref, dzkv_ref, dqn_ref, dkn_ref, dsk_ref,
         carry_ref, tot_ref, dqr_ref, qacc_ref, kacc_ref, sacc_ref) = rest[-11:]
        n = pl.program_id(0)
        bdm = bd_ref[...]
        kn = kn_ref[...]

        @pl.when(n == 0)
        def _():
            carry_ref[...] = jnp.zeros_like(carry_ref)
            qacc_ref[...] = jnp.zeros_like(qacc_ref)
            kacc_ref[...] = jnp.zeros_like(kacc_ref)
            sacc_ref[...] = jnp.zeros_like(sacc_ref)

        kp_raw = kvp_ref[:, 0:128]
        kph, rp = _head_norm_fwd(kp_raw, kn, bdm)
        cp, s1p, s2p = cp_ref[...], s1p_ref[...], s2p_ref[...]

        @pl.when(n < nb)
        def _():
            c, s1, s2 = c_ref[...], s1_ref[...], s2_ref[...]
            c8, s18, s28 = _tile_lanes(c, 8), _tile_lanes(s1, 8), _tile_lanes(s2, 8)
            q_raw = q_ref[...]
            qh, rq = _head_norm_fwd(q_raw, qn_ref[...], bdm)
            qr = (_rope(qh, c8, s18, s28) * scale).astype(MM)
            kc, _ = _head_norm_fwd(kvc_ref[:, 0:128], kn, bdm)
            k2 = jnp.concatenate([_rope(kph, cp, s1p, s2p), _rope(kc, c, s1, s2)], axis=0)
            v2 = jnp.concatenate([kvp_ref[:, 128:256], kvc_ref[:, 128:256]], axis=0)
            dob = db_ref[...].astype(MM)
            mask = _fold_masks(n)
            lane = lax.broadcasted_iota(jnp.int32, (1, LANES), 1)
            dsk = jnp.zeros((1, LANES), F32)
            dk2 = jnp.zeros((2 * BLOCK, LANES), F32)
            dv2 = jnp.zeros((2 * BLOCK, LANES), F32)
            for hk in range(N_KV_HEADS):
                qs = _stack_pairs(qr, hk)
                do = _stack_pairs(dob, hk)
                dqt = jnp.zeros((LANES, PAIR_COLS), F32)
                dkb, dvb = [], []
                for par, (kb, vb) in enumerate(zip(_parity_bands(k2, hk), _parity_bands(v2, hk))):
                    dp = _mm_nt(vb, do)
                    prs, dss = [], []
                    for j in range(PAIRS):
                        cols = slice(j * BLOCK, (j + 1) * BLOCK)
                        pr = pr_ref[0, 2 * hk + par, :, cols].astype(F32)
                        psink = ps_ref[0, 2 * hk + par, :, cols]
                        dpj = _fold(dp[:, cols], mask[0])
                        coldot = jnp.sum(pr * dpj, axis=0, keepdims=True)
                        dss.append(_unfold(pr * (dpj - coldot), mask[0]))
                        prs.append(_unfold(pr, mask[0]))
                        h = hk * GQA_GROUP + 2 * j + par
                        dsk = dsk + jnp.where(lane == h, jnp.sum(-psink * coldot), 0.0)
                    ds, pr = jnp.concatenate(dss, axis=1), jnp.concatenate(prs, axis=1)
                    dqt = dqt + _mm(kb.T, ds)
                    dkb.append(_mm(ds, qs))
                    dvb.append(_mm(pr, do))
                for j in range(PAIRS):
                    col = (PAIRS * hk + j) * LANES
                    dqr_ref[:, col:col + LANES] = dqt[:, j * BLOCK:(j + 1) * BLOCK].T
                dk2 = dk2 + _fold_parity(dkb[0], dkb[1], hk)
                dv2 = dv2 + _fold_parity(dvb[0], dvb[1], hk)
            tot_ref[:, 0:128] = carry_ref[:, 0:128] + dk2[0:BLOCK, :]
            tot_ref[:, 128:256] = carry_ref[:, 128:256] + dv2[0:BLOCK, :]
            carry_ref[:, 0:128] = dk2[BLOCK:2 * BLOCK, :]
            carry_ref[:, 128:256] = dv2[BLOCK:2 * BLOCK, :]
            sacc_ref[...] += dsk
            dqh = _rope_bwd(dqr_ref[...] * scale, c8, s18, s28)
            dq, dgq = _head_norm_bwd(q_raw, rq, qn_ref[...], dqh, bdm)
            dzq_ref[...] = dq.astype(MM)
            qacc_ref[...] += jnp.sum(dgq, axis=0, keepdims=True)

        @pl.when(n == nb)
        def _():
            tot_ref[...] = carry_ref[...]

        dkh = _rope_bwd(tot_ref[:, 0:128], cp, s1p, s2p)
        dkr, dgk = _head_norm_bwd(kp_raw, rp, kn, dkh, bdm)
        dzkv_ref[:, 0:128] = dkr.astype(MM)
        dzkv_ref[:, 128:256] = tot_ref[:, 128:256].astype(MM)
        kacc_ref[...] += jnp.where(n > 0, jnp.sum(dgk, axis=0, keepdims=True), 0.0)

        @pl.when(n == nb)
        def _():
            fold = qacc_ref[:, 0:HEAD_DIM]
            for h in range(1, N_Q_HEADS):
                fold = fold + qacc_ref[:, h * HEAD_DIM:(h + 1) * HEAD_DIM]
            dqn_ref[...] = fold
            dkn_ref[...] = kacc_ref[:, 0:HEAD_DIM] + kacc_ref[:, HEAD_DIM:2 * HEAD_DIM]
            dsk_ref[...] = sacc_ref[...]

    tab = lambda im: pl.BlockSpec((BLOCK, LANES), im)
    sd = jax.ShapeDtypeStruct
    const = lambda n: (0, 0)
    return pl.pallas_call(
        body, name="attn_bwd", grid=(nb + 1,),
        out_shape=(sd((s, 1024), MM), sd((s, 256), MM), sd((1, HEAD_DIM), F32), sd((1, HEAD_DIM), F32),
                   sd((1, LANES), F32)),
        in_specs=[pl.BlockSpec((BLOCK, 1024), cur), pl.BlockSpec((BLOCK, 256), cur), pl.BlockSpec((BLOCK, 256), prv),
                  pl.BlockSpec((BLOCK, 1024), cur), tab(cur), tab(cur), tab(cur), tab(prv), tab(prv), tab(prv),
                  _resident((1, 1024)), _resident((1, 128)),
                  pl.BlockSpec((1, 4, BLOCK, PAIR_COLS), lambda n: (jnp.minimum(n, nb - 1), 0, 0, 0)),
                  pl.BlockSpec((1, 4, 1, PAIR_COLS), lambda n: (jnp.minimum(n, nb - 1), 0, 0, 0)),
                  _resident((LANES, LANES))] + tie_spec,
        out_specs=(pl.BlockSpec((BLOCK, 1024), cur), pl.BlockSpec((BLOCK, 256), prv),
                   pl.BlockSpec((1, HEAD_DIM), const), pl.BlockSpec((1, HEAD_DIM), const),
                   pl.BlockSpec((1, LANES), const)),
        scratch_shapes=[pltpu.VMEM((BLOCK, 256), F32), pltpu.VMEM((BLOCK, 256), F32), pltpu.VMEM((BLOCK, 1024), F32),
                        pltpu.VMEM((1, 1024), F32), pltpu.VMEM((1, 128), F32), pltpu.VMEM((1, LANES), F32)],
        compiler_params=_params("arbitrary"),
    )(q, kv, kv, db, *tabs, *tabs, q_norm_t, k_norm_t, probs, sink_probs, bd, *tie)


def _inproj_bwd(dzu, dzq, dzkv, dzg, win_t, x, attn_norm, dx1, tm, after=None):
    s = x.shape[0]
    tie, tie_spec = _after(after)

    def body(du_ref, dq_ref, dkv_ref, dg_ref, w_ref, x_ref, gn_ref, dx1_ref, *rest):
        gx_ref, dgn_ref = rest[-2:]

        @pl.when(pl.program_id(0) == 0)
        def _():
            dgn_ref[...] = jnp.zeros_like(dgn_ref)

        dh = _mm(du_ref[...], w_ref[O_U:O_Q, :]) + _mm(dq_ref[...], w_ref[O_Q:O_KV, :])
        dh = dh + _mm(dkv_ref[...], w_ref[O_KV:O_G, :]) + _mm(dg_ref[...], w_ref[O_G:IN_WIDTH, :])
        x = x_ref[...]
        _, r = _rmsnorm_fwd(x, gn_ref[...])
        dx, dgr = _rmsnorm_bwd(x, r, gn_ref[...], dh)
        dgn_ref[...] += jnp.sum(dgr, axis=0, keepdims=True)
        gx_ref[...] = dx1_ref[...] + dx

    return pl.pallas_call(
        body, name="inproj_bwd", grid=(s // tm,),
        out_shape=(jax.ShapeDtypeStruct((s, D_MODEL), F32), jax.ShapeDtypeStruct((1, D_MODEL), F32)),
        in_specs=[_rows(tm, 1024), _rows(tm, 1024), _rows(tm, 256), _rows(tm, 2048),
                  _resident((IN_WIDTH, D_MODEL)), _rows(tm, 1024), _resident((1, 1024)), _rows(tm, 1024)] + tie_spec,
        out_specs=(_rows(tm, 1024), pl.BlockSpec((1, D_MODEL), lambda i: (0, 0))),
        compiler_params=_params("arbitrary"),
    )(dzu, dzq, dzkv, dzg, win_t, x, attn_norm, dx1, *tie)


def _attention_constants(q_norm, k_norm, sinks):
    inv_freq = np.float32(ROPE_THETA) ** (-np.arange(0, ROPE_DIM, 2, dtype=np.float32) / np.float32(ROPE_DIM))
    lane = np.arange(LANES) % HEAD_DIM
    invf = jnp.asarray(np.where(lane < ROPE_DIM, inv_freq[lane % (ROPE_DIM // 2)], 0.0).reshape(1, LANES), F32)
    bd = jnp.asarray(np.arange(LANES)[:, None] // HEAD_DIM == np.arange(LANES)[None, :] // HEAD_DIM, MM)
    q_norm_t = jnp.tile(q_norm, (1, N_Q_HEADS))
    k_norm_t = jnp.tile(k_norm, (1, N_KV_HEADS))
    sink_rows = jnp.repeat(sinks.reshape(N_KV_HEADS, PAIRS, 2).transpose(0, 2, 1), BLOCK, axis=2)
    sink_rows = sink_rows.reshape(N_KV_HEADS, 2, 1, PAIR_COLS)
    return invf, bd, q_norm_t, k_norm_t, sink_rows


ANY = pl.BlockSpec(memory_space=pl.ANY)


def _position():
    return lax.axis_index("x"), lax.axis_index("y"), lax.axis_index("c")


def _all_gather(shards):
    k = len(shards)

    def body(*refs):
        ins, outs = refs[:k], refs[k:2 * k]
        send_sems, recv_sems, local_sems = refs[2 * k:]
        x, y, c = _position()
        me, sibling = (x, y, c), (x, y, 1 - c)
        chips = [(1 - x, y), (x, 1 - y), (1 - x, 1 - y)]

        def copy(a, kk, block, to, src=None):
            dst = outs[a].at[4 * block[0] + 2 * block[1] + block[2]]
            return pltpu.make_async_remote_copy(
                src_ref=dst if src is None else src, dst_ref=dst, send_sem=send_sems.at[a * 7 + kk],
                recv_sem=recv_sems.at[a * 7 + kk], device_id=to, device_id_type=MESH)

        mine = [pltpu.make_async_copy(ins[a], outs[a].at[4 * x + 2 * y + c], local_sems.at[a]) for a in range(k)]
        for cp in mine:
            cp.start()
        first = []
        for a in range(k):
            first.append(copy(a, 0, me, sibling, src=ins[a]))
            first += [copy(a, 1 + j, me, (*chip, c), src=ins[a]) for j, chip in enumerate(chips)]
        for cp in first:
            cp.start()
        passed = []
        for j, chip in enumerate(chips):
            for a in range(k):
                copy(a, 1 + j, (*chip, c), me).wait_recv()
                cp = copy(a, 4 + j, (*chip, c), sibling)
                cp.start()
                passed.append(cp)
        for a in range(k):
            copy(a, 0, sibling, me).wait_recv()
            for j, chip in enumerate(chips):
                copy(a, 4 + j, (*chip, 1 - c), me).wait_recv()
        for cp in first + passed:
            cp.wait_send()
        for cp in mine:
            cp.wait()

    return pl.pallas_call(
        body, name="all_gather_weights",
        out_shape=tuple(jax.ShapeDtypeStruct((N_DEV,) + s.shape, s.dtype) for s in shards),
        in_specs=[ANY] * k, out_specs=(ANY,) * k,
        scratch_shapes=[pltpu.SemaphoreType.DMA((7 * k,)), pltpu.SemaphoreType.DMA((7 * k,)),
                        pltpu.SemaphoreType.DMA((k,))],
    )(*shards)


HBM = pl.BlockSpec(memory_space=pltpu.HBM)
SEM = pl.BlockSpec(memory_space=pltpu.SEMAPHORE)
EFFECT = pltpu.SideEffectType.DATAFLOW_SIDE_EFFECTING


def _exchange_start(name, bufs, n_sems, copies, after=None):
    k = len(bufs)
    tie, tie_spec = _after(after)
    n_in = k + len(tie)

    def body(*refs):
        for cp in copies(refs[:k], refs[n_in], refs[n_in + 1]):
            cp.start()
        refs[-1][...] = jnp.zeros_like(refs[-1])

    dma = pltpu.SemaphoreType.DMA((n_sems,))
    out = pl.pallas_call(
        body, name=name,
        out_shape=(dma, dma, *[pltpu.HBM(b.shape, b.dtype) for b in bufs], jax.ShapeDtypeStruct((8, LANES), F32)),
        in_specs=[HBM] * k + tie_spec, out_specs=(SEM, SEM, *[HBM] * k, pl.BlockSpec(memory_space=pltpu.VMEM)),
        input_output_aliases={i: 2 + i for i in range(k)},
        compiler_params=pltpu.CompilerParams(has_side_effects=EFFECT),
    )(*[pltpu.with_memory_space_constraint(b, pltpu.HBM) for b in bufs], *tie)
    return out[0], out[1], list(out[2:2 + k]), out[-1]


def _exchange_mid(name, bufs, sems_in, n_sems, waits, copies, after):
    k, ns = len(bufs), len(sems_in)

    def body(*refs):
        ins = refs[:k]
        waits(ins, *refs[k:k + ns])
        for cp in copies(ins, refs[k + ns + 1], refs[k + ns + 2]):
            cp.start()

    dma = pltpu.SemaphoreType.DMA((n_sems,))
    out = pl.pallas_call(
        body, name=name, out_shape=(dma, dma, *[pltpu.HBM(b.shape, b.dtype) for b in bufs]),
        in_specs=[HBM] * k + [SEM] * ns + [ANY], out_specs=(SEM, SEM, *[HBM] * k),
        input_output_aliases={i: 2 + i for i in range(k)},
        compiler_params=pltpu.CompilerParams(has_side_effects=EFFECT),
    )(*bufs, *sems_in, after)
    return out[0], out[1], list(out[2:])


def _exchange_wait(name, bufs, sems, waits, after=None):
    k, ns = len(bufs), len(sems)
    tie, tie_spec = _after(after)

    def body(*refs):
        waits(refs[:k], *refs[k:k + ns])

    out = pl.pallas_call(
        body, name=name, out_shape=tuple(pltpu.HBM(b.shape, b.dtype) for b in bufs),
        in_specs=[HBM] * k + [SEM] * ns + tie_spec, out_specs=(HBM,) * k,
        input_output_aliases={i: i for i in range(k)},
        compiler_params=pltpu.CompilerParams(has_side_effects=EFFECT),
    )(*bufs, *sems, *tie)
    return list(out)


def _gather_copies(k, direct):
    def copies(refs, send_sems, recv_sems):
        x, y, c = _position()
        chips = [(1 - x, y), (x, 1 - y), (1 - x, 1 - y)]
        out = []
        for a in range(k):
            land = refs[k + a]
            if direct:
                mine = land.at[4 * x + 2 * y + c]
                for kk, to in enumerate([(x, y, 1 - c)] + [(*chip, c) for chip in chips]):
                    out.append(pltpu.make_async_remote_copy(
                        src_ref=refs[a], dst_ref=mine, send_sem=send_sems.at[4 * a + kk],
                        recv_sem=recv_sems.at[4 * a + kk], device_id=to, device_id_type=MESH))
            else:
                for j, (px, py) in enumerate(chips):
                    slot = land.at[4 * px + 2 * py + c]
                    out.append(pltpu.make_async_remote_copy(
                        src_ref=slot, dst_ref=slot, send_sem=send_sems.at[3 * a + j], recv_sem=recv_sems.at[3 * a + j],
                        device_id=(x, y, 1 - c), device_id_type=MESH))
        return out
    return copies


def _all_gather_behind(shards, start_after, mid_after):
    k = len(shards)
    me = 4 * lax.axis_index("x") + 2 * lax.axis_index("y") + lax.axis_index("c")
    lands = [lax.dynamic_update_slice(lax.empty((N_DEV,) + s.shape, s.dtype), s[None], (me, 0, 0)) for s in shards]
    direct, passed = _gather_copies(k, True), _gather_copies(k, False)

    send_a, recv_a, bufs, token = _exchange_start("gather_start", list(shards) + lands, 4 * k, direct, start_after)

    def finish():
        def wait_ici(refs, send_sems, recv_sems):
            for i, cp in enumerate(direct(refs, send_sems, recv_sems)):
                if i % 4:
                    cp.wait_recv()

        send_b, recv_b, bufs2 = _exchange_mid("gather_pass", bufs, [send_a, recv_a], 3 * k, wait_ici, passed,
                                              mid_after())

        def wait_all(refs, sa, ra, sb, rb):
            for i, cp in enumerate(direct(refs, sa, ra)):
                cp.wait_send()
                if i % 4 == 0:
                    cp.wait_recv()
            for cp in passed(refs, sb, rb):
                cp.wait()

        return _exchange_wait("gather_wait", bufs2, [send_a, recv_a, send_b, recv_b], wait_all)[k:]

    return token, finish


def _pair_copies(k):
    def copies(refs, send_sems, recv_sems):
        x, y, c = _position()
        return [pltpu.make_async_remote_copy(
            src_ref=refs[a].at[2 * ch + 1 - c], dst_ref=refs[k + a].at[ch], send_sem=send_sems.at[4 * a + ch],
            recv_sem=recv_sems.at[4 * a + ch], device_id=(x, y, 1 - c), device_id_type=MESH)
            for a in range(k) for ch in range(4)]
    return copies


def _chip_copies(k):
    def copies(refs, send_sems, recv_sems):
        x, y, c = _position()
        return [pltpu.make_async_remote_copy(
            src_ref=refs[a].at[2 * px + py], dst_ref=refs[k + a].at[rel], send_sem=send_sems.at[3 * a + rel],
            recv_sem=recv_sems.at[3 * a + rel], device_id=(px, py, c), device_id_type=MESH)
            for a in range(k) for rel, (px, py) in enumerate([(1 - x, y), (x, 1 - y), (1 - x, 1 - y)])]
    return copies


def _symmetric_exchange(name, srcs, n_land, copies_of):
    k = len(srcs)
    lands = [lax.empty((n_land,) + s.shape[1:], s.dtype) for s in srcs]
    copies = copies_of(k)
    send_sems, recv_sems, bufs, token = _exchange_start(name + "_start", list(srcs) + lands, n_land * k, copies)

    def finish(after):
        def wait_all(refs, ss, rs):
            for cp in copies(refs, ss, rs):
                cp.wait()

        done = _exchange_wait(name + "_wait", bufs, [send_sems, recv_sems], wait_all, after)
        return done[:k], done[k:]

    return token, finish


def _pair_add(fulls, recvs, wires):
    k = len(fulls)
    core = lax.axis_index("c").astype(jnp.int32).reshape(1)
    shapes = [f.shape[1:] for f in fulls]

    def body(core_ref, *refs):
        f_refs, r_refs, pw_refs, own_refs = (refs[j * k:(j + 1) * k] for j in range(4))
        x, y, _ = _position()
        mine = pl.program_id(0) == 2 * x + y
        for f_ref, r_ref, pw_ref, own_ref in zip(f_refs, r_refs, pw_refs, own_refs):
            tot = f_ref[0, 0] + r_ref[0]
            pw_ref[0] = tot.astype(pw_ref.dtype)

            @pl.when(mine)
            def _():
                own_ref[...] = tot

    out = pl.pallas_call(
        body, name="grad_pair_add",
        grid_spec=pltpu.PrefetchScalarGridSpec(
            num_scalar_prefetch=1, grid=(4,),
            in_specs=[pl.BlockSpec((1, 1, r, c), lambda i, core_ref: (i, core_ref[0], 0, 0)) for r, c in shapes]
            + [pl.BlockSpec((1, r, c), lambda i, core_ref: (i, 0, 0)) for r, c in shapes],
            out_specs=tuple([pl.BlockSpec((1, r, c), lambda i, core_ref: (i, 0, 0)) for r, c in shapes]
                            + [pl.BlockSpec((r, c), lambda i, core_ref: (0, 0)) for r, c in shapes])),
        out_shape=tuple([jax.ShapeDtypeStruct((4, r, c), wd) for (r, c), wd in zip(shapes, wires)]
                        + [jax.ShapeDtypeStruct((r, c), F32) for r, c in shapes]),
        compiler_params=_params("arbitrary"),
    )(core, *[f.reshape(4, 2, *f.shape[1:]) for f in fulls], *recvs)
    return out[:k], out[k:]


def _adamw_math(w, g, m, v):
    m = ADAM_B1 * m + (1.0 - ADAM_B1) * g
    v = ADAM_B2 * v + (1.0 - ADAM_B2) * (g * g)
    m_hat = m / (1.0 - ADAM_B1 ** ADAM_STEP)
    v_hat = v / (1.0 - ADAM_B2 ** ADAM_STEP)
    delta = -ADAM_LR * (m_hat / (jnp.sqrt(v_hat) + ADAM_EPS) + ADAM_WD * w)
    return delta, m, v


def _row_tile(r):
    for t in (256, 272, 176, 128):
        if r % t == 0 and r > t:
            return t
    return r


def _adamw(g_own, recv, w, m, v):
    r, c_ = w.shape
    t = _row_tile(r)
    blk = pl.BlockSpec((t, c_), lambda i: (i, 0))

    def body(g_ref, r_ref, w_ref, m_ref, v_ref, go_ref, d_ref, mo_ref, vo_ref):
        g = g_ref[...]
        for i in range(3):
            g = g + r_ref[i].astype(F32)
        go_ref[...] = g
        d_ref[...], mo_ref[...], vo_ref[...] = _adamw_math(w_ref[...], g, m_ref[...], v_ref[...])

    return pl.pallas_call(
        body, name="adamw", grid=(r // t,), out_shape=(jax.ShapeDtypeStruct((r, c_), F32),) * 4,
        in_specs=[blk, pl.BlockSpec((3, t, c_), lambda i: (0, i, 0)), blk, blk, blk], out_specs=(blk,) * 4,
        compiler_params=_params("parallel"),
    )(g_own, recv, w, m, v)


SMALL = ("attn_norm", "b_gate", "pool_scale", "q_norm", "k_norm", "sinks", "ffn_norm", "conv_b")
SMALL_SIZES = (1024, 2048, 1024, 64, 64, 16, 1024, 5632)
SMALL_OFFSETS = tuple(sum(-(-s // LANES) * LANES for s in SMALL_SIZES[:i]) for i in range(len(SMALL_SIZES) + 1))
SMALL_WIDTH = SMALL_OFFSETS[-1] + LANES


def _pack_small(d, loss=None):
    parts = [jnp.pad(d[n].reshape(1, -1), ((0, 0), (0, -s % LANES))) for n, s in zip(SMALL, SMALL_SIZES)]
    last = jnp.zeros((1, LANES), F32) if loss is None else jnp.pad(loss.reshape(1, 1), ((0, 0), (0, LANES - 1)))
    return jnp.concatenate(parts + [last], axis=1)


def _small_allreduce(gp):
    def body(g_ref, sum_ref, slots_ref, send_sems, recv_sems):
        x, y, c = _position()
        me = 4 * x + 2 * y + c
        slots_ref[me] = g_ref[...]
        cps = []
        for rel in range(1, N_DEV):
            fx, fy, fc = (rel >> 2) & 1, (rel >> 1) & 1, rel & 1
            to = (1 - x if fx else x, 1 - y if fy else y, 1 - c if fc else c)
            cps.append(pltpu.make_async_remote_copy(
                src_ref=g_ref, dst_ref=slots_ref.at[me], send_sem=send_sems.at[rel - 1],
                recv_sem=recv_sems.at[rel - 1], device_id=to, device_id_type=MESH))
        for cp in cps:
            cp.start()
        for cp in cps:
            cp.wait()
        g = slots_ref[0]
        for i in range(1, N_DEV):
            g = g + slots_ref[i]
        sum_ref[...] = g

    vm = pl.BlockSpec(memory_space=pltpu.VMEM)
    return pl.pallas_call(
        body, name="small_allreduce", out_shape=jax.ShapeDtypeStruct((1, SMALL_WIDTH), F32),
        in_specs=[vm], out_specs=vm,
        scratch_shapes=[pltpu.VMEM((N_DEV, 1, SMALL_WIDTH), F32), pltpu.SemaphoreType.DMA((N_DEV - 1,)),
                        pltpu.SemaphoreType.DMA((N_DEV - 1,))],
    )(gp)


def _small_adamw(gsum, ws, ms, vs):
    n = len(SMALL)

    def body(g_ref, *rest):
        w_refs, m_refs, v_refs, outs = rest[:n], rest[n:2 * n], rest[2 * n:3 * n], rest[3 * n:]
        for j, size in enumerate(SMALL_SIZES):
            g = g_ref[:, SMALL_OFFSETS[j]:SMALL_OFFSETS[j] + size]
            results = (g,) + _adamw_math(w_refs[j][...], g, m_refs[j][...], v_refs[j][...])
            for kind, val in enumerate(results):
                outs[kind * n + j][...] = val
        outs[4 * n][...] = g_ref[:, SMALL_OFFSETS[-1]:SMALL_WIDTH]

    vm = pl.BlockSpec(memory_space=pltpu.VMEM)
    shapes = tuple(jax.ShapeDtypeStruct((1, s), F32) for s in SMALL_SIZES) * 4
    return pl.pallas_call(
        body, name="small_adamw", out_shape=shapes + (jax.ShapeDtypeStruct((1, LANES), F32),),
        in_specs=[vm] * (1 + 3 * n), out_specs=(vm,) * (4 * n + 1),
    )(gsum, *ws, *ms, *vs)


WEIGHTS = ("attn_norm", "w_in", "b_gate", "w_pool", "pool_scale", "q_norm", "k_norm", "sinks", "w_out", "ffn_norm",
           "w_up", "conv_w", "conv_b", "w_down")


def kernel(x, positions, attn_norm, w_in, b_gate, w_pool, pool_scale, q_norm, k_norm, sinks, w_out, ffn_norm, w_up, conv_w, conv_b, w_down, loss_target, m_attn_norm, m_w_in, m_b_gate, m_w_pool, m_pool_scale, m_q_norm, m_k_norm, m_sinks, m_w_out, m_ffn_norm, m_w_up, m_conv_w, m_conv_b, m_w_down, v_attn_norm, v_w_in, v_b_gate, v_w_pool, v_pool_scale, v_q_norm, v_k_norm, v_sinks, v_w_out, v_ffn_norm, v_w_up, v_conv_w, v_conv_b, v_w_down):
    w = dict(attn_norm=attn_norm, w_in=w_in, b_gate=b_gate, w_pool=w_pool, pool_scale=pool_scale, q_norm=q_norm,
             k_norm=k_norm, sinks=sinks, w_out=w_out, ffn_norm=ffn_norm, w_up=w_up, conv_w=conv_w, conv_b=conv_b,
             w_down=w_down)
    m = dict(attn_norm=m_attn_norm, w_in=m_w_in, b_gate=m_b_gate, w_pool=m_w_pool, pool_scale=m_pool_scale,
             q_norm=m_q_norm, k_norm=m_k_norm, sinks=m_sinks, w_out=m_w_out, ffn_norm=m_ffn_norm, w_up=m_w_up,
             conv_w=m_conv_w, conv_b=m_conv_b, w_down=m_w_down)
    v = dict(attn_norm=v_attn_norm, w_in=v_w_in, b_gate=v_b_gate, w_pool=v_w_pool, pool_scale=v_pool_scale,
             q_norm=v_q_norm, k_norm=v_k_norm, sinks=v_sinks, w_out=v_w_out, ffn_norm=v_ffn_norm, w_up=v_w_up,
             conv_w=v_conv_w, conv_b=v_conv_b, w_down=v_w_down)
    seq = x.shape[1]
    tm = 256
    tw = min(seq, 512)
    tk, tk_ff = min(seq, 1024), min(seq, 2048)
    xs, target, pos_col = x[0], loss_target[0], positions.reshape(seq, 1)
    invf, bd, q_norm_t, k_norm_t, sink_rows = _attention_constants(q_norm, k_norm, sinks)
    out, done = {}, {}
    nat = {"w_in": (D_MODEL, 544), "w_pool": (128, POOL_GROUP), "w_out": (128, D_MODEL), "w_up": (D_MODEL, 704),
           "conv_w": (3, 704), "w_down": (352, D_MODEL)}

    def update(names, owns, recvs):
        for name, own, recv in zip(names, owns, recvs):
            w2, m2, v2 = (t[name].reshape(nat[name]) for t in (w, m, v))
            if name in ("w_in", "w_up"):
                res = _adamw(own, recv, w2.T, m2.T, v2.T)
                done[name] = res[1]
                res = [t.T for t in res]
            else:
                res = _adamw(own, recv, w2, m2, v2)
                done[name] = res[1]
            out[name] = [t.reshape(w[name].shape) for t in res]

    (g_win,) = _all_gather([w_in[0].T.astype(MM)])
    win_t = g_win.reshape(IN_WIDTH, D_MODEL)
    fwd = {}
    token, gather_rest = _all_gather_behind(
        [w_pool[0].astype(MM).reshape(128, POOL_GROUP), w_out[0].astype(MM), w_up[0].T.astype(MM), conv_w[0],
         w_down[0].astype(MM)], g_win, lambda: fwd["b"])

    tabs = _rope_tables(pos_col, invf)
    h, u, q, kv, g = _inproj_fwd(xs, attn_norm + token[0:1, 0:1], win_t, b_gate, tw)
    b, probs, sink_probs = _attn_fwd(q, kv, tabs, q_norm_t, k_norm_t, sink_rows, bd)
    fwd["b"] = b
    g_wpool, g_wout, g_wup, g_convw, g_wdown = gather_rest()
    wpool = g_wpool.reshape(N_DEV, 4, 32, POOL_GROUP).transpose(1, 0, 2, 3).reshape(4, POOL_GROUP, POOL_GROUP)
    wout = g_wout.reshape(D_MODEL, D_MODEL)
    wup_t = g_wup.reshape(2 * D_FF, D_MODEL)
    convw = g_convw.transpose(1, 0, 2).reshape(3, 2 * D_FF)
    wdown = g_wdown.reshape(D_FF, D_MODEL)
    a = _pool_fwd(u, wpool, pool_scale, tw)
    x1, mix = _mix_out_fwd(xs, g, a, b, wout, tw)
    h2, uff, ucv, dy, dyb, lossp = _ffn_fwd(x1, ffn_norm, wup_t, convw, conv_b, wdown, target, tm)

    du, act, d_conv_w, d_conv_b = _ffn_bwd_a(dyb, uff, ucv, convw, wdown, tm)
    d_wdown = _matmul_tn(act, dyb, FF_CHUNK, tk_ff, "dw_down")
    dx1, dx1b, d_ffn_norm = _ffn_bwd_b(du, wup_t, x1, ffn_norm, dy, tm)
    d_wup_t = _matmul_tn(du, h2, FF_CHUNK, tk_ff, "dw_up")
    late = ("w_down", "w_up", "conv_w")
    late_wire = (WIRE, WIRE, F32)
    late_full = [d_wdown.reshape(N_DEV, 352, D_MODEL), d_wup_t.reshape(N_DEV, 704, D_MODEL),
                 d_conv_w.reshape(3, N_DEV, 704).transpose(1, 0, 2)]
    token, late_pair = _symmetric_exchange("late_pair", late_full, 4, _pair_copies)
    da, db, dzg, d_b_gate, d_wout, d_win_t = _mix_bwd(dx1b, wout, g, a, b, mix, h, tm, after=token)
    late_pw, late_own = _pair_add(*late_pair(dzg), late_wire)
    token, late_chip = _symmetric_exchange("late_chip", list(late_pw), 3, _chip_copies)
    dzu, d_wpool, d_pool_scale, d_win_t = _pool_bwd(u, da, wpool, pool_scale, h, d_win_t, tw, after=token)
    dzq, dzkv, d_q_norm, d_k_norm, d_sinks = _attn_bwd(q, kv, db, tabs, q_norm_t, k_norm_t, probs, sink_probs, bd,
                                                       after=token)
    d_win_t = _matmul_tn(dzq, h, 1024, tk, "dw_in_q", into=d_win_t, row0=O_Q)
    d_win_t = _matmul_tn(dzkv, h, 256, tk, "dw_in_kv", into=d_win_t, row0=O_KV)
    early = ("w_in", "w_pool", "w_out")
    early_full = [d_win_t.reshape(N_DEV, 544, D_MODEL),
                  d_wpool.reshape(4, N_DEV, 32, POOL_GROUP).transpose(1, 0, 2, 3).reshape(N_DEV, 128, POOL_GROUP),
                  d_wout.reshape(N_DEV, 128, D_MODEL)]
    token, early_pair = _symmetric_exchange("early_pair", early_full, 4, _pair_copies)
    update(late, late_own, late_chip(token)[1])
    early_pw, early_own = _pair_add(*early_pair([done[n] for n in late]), (WIRE,) * 3)
    token, early_chip = _symmetric_exchange("early_chip", list(early_pw), 3, _chip_copies)
    grad_x, d_attn_norm = _inproj_bwd(dzu, dzq, dzkv, dzg, win_t, xs, attn_norm, dx1, tw, after=token)
    gr = dict(attn_norm=d_attn_norm, b_gate=d_b_gate, pool_scale=d_pool_scale, q_norm=d_q_norm, k_norm=d_k_norm,
              sinks=d_sinks[:, 0:N_Q_HEADS], ffn_norm=d_ffn_norm, conv_b=d_conv_b)
    small = _small_adamw(_small_allreduce(_pack_small(gr, lossp[0, 0])), *[[t[n] for n in SMALL] for t in (w, m, v)])
    loss = small[-1][0, 0]
    for j, name in enumerate(SMALL):
        out[name] = [small[kind * len(SMALL) + j] for kind in range(4)]
    update(early, early_own, early_chip(small[0])[1])

    return (loss, grad_x[None], *[out[n][0] for n in WEIGHTS], *[out[n][1] for n in WEIGHTS],
            *[out[n][2] for n in WEIGHTS], *[out[n][3] for n in WEIGHTS])
```

```python
import functools

import numpy as np
import jax
import jax.numpy as jnp
from jax import lax
from jax.experimental import pallas as pl
from jax.experimental.pallas import tpu as pltpu

F32 = jnp.float32
MM = jnp.bfloat16
WIRE = jnp.bfloat16
ACT = jnp.bfloat16

D_MODEL = 1024
D_FF = 2816
HEAD_DIM = 64
N_Q_HEADS = 16
N_KV_HEADS = 2
GQA_GROUP = 8
BLOCK = 128
ROPE_DIM = 16
ROPE_THETA = 500000.0
POOL_WINDOWS = (2, 4, 8, 16)
POOL_GROUP = 256
POOL_HALO = 32
EPS = 1e-6
NEG = -1e30
O_U, O_Q, O_KV, O_G, IN_WIDTH = 0, 1024, 2048, 2304, 4352
FF_CHUNK = 1408

ADAM_LR, ADAM_B1, ADAM_B2, ADAM_EPS, ADAM_WD, ADAM_STEP = 0.001, 0.9, 0.999, 1e-08, 0.01, 10

N_DEV = 8
LANES = 128
VMEM_LIMIT_BYTES = 56 * 1024 * 1024
MESH = pl.DeviceIdType.MESH


def _params(*sem):
    return pltpu.CompilerParams(dimension_semantics=sem, vmem_limit_bytes=VMEM_LIMIT_BYTES)


def _resident(shape):
    nd = len(shape)
    return pl.BlockSpec(shape, lambda *_: (0,) * nd, pipeline_mode=pl.Buffered(1))


def _rows(tm, width):
    return pl.BlockSpec((tm, width), lambda i: (i, 0))


def _mm(a, b):
    return jnp.dot(a.astype(MM), b.astype(MM), preferred_element_type=F32)


def _mm_nt(a, b):
    return lax.dot_general(a.astype(MM), b.astype(MM), (((1,), (1,)), ((), ())), preferred_element_type=F32)


def _mm_tn(a, b):
    return lax.dot_general(a.astype(MM), b.astype(MM), (((0,), (0,)), ((), ())), preferred_element_type=F32)


def _rmsnorm_fwd(x, g):
    r = lax.rsqrt(jnp.mean(x * x, axis=-1, keepdims=True) + EPS)
    return x * r * g, r


def _rmsnorm_bwd(x, r, g, dy):
    xn = x * r
    dxn = dy * g
    dx = r * (dxn - xn * jnp.mean(dxn * xn, axis=-1, keepdims=True))
    return dx, dy * xn


def _group_sum64(v, bd):
    hi = v.astype(MM)
    lo = (v - hi.astype(F32)).astype(MM)
    outs = []
    for t in range(v.shape[1] // LANES):
        sl = slice(LANES * t, LANES * (t + 1))
        outs.append(jnp.dot(hi[:, sl], bd, preferred_element_type=F32)
                    + jnp.dot(lo[:, sl], bd, preferred_element_type=F32))
    return outs[0] if len(outs) == 1 else jnp.concatenate(outs, axis=1)


def _head_norm_fwd(x, g, bd):
    r = lax.rsqrt(_group_sum64(x * x, bd) * (1.0 / HEAD_DIM) + EPS)
    return x * r * g, r


def _head_norm_bwd(x, r, g, dy, bd):
    xn = x * r
    dxn = dy * g
    dx = r * (dxn - xn * (_group_sum64(dxn * xn, bd) * (1.0 / HEAD_DIM)))
    return dx, dy * xn


def _rope(x, c, s1, s2):
    w = x.shape[1]
    return x * c + pltpu.roll(x, w - ROPE_DIM // 2, 1) * s1 + pltpu.roll(x, ROPE_DIM // 2, 1) * s2


def _rope_bwd(dy, c, s1, s2):
    w = dy.shape[1]
    return dy * c + pltpu.roll(dy * s1, ROPE_DIM // 2, 1) + pltpu.roll(dy * s2, w - ROPE_DIM // 2, 1)


def _tile_lanes(t, reps):
    return t if reps == 1 else jnp.concatenate([t] * reps, axis=1)


def _rope_tables(pos_col, invf):
    s = pos_col.shape[0]
    tm = min(s, 1024)

    def body(pos_ref, invf_ref, c_ref, s1_ref, s2_ref):
        ang = pos_ref[...].astype(F32) * invf_ref[...]
        lane = lax.broadcasted_iota(jnp.int32, ang.shape, 1) % HEAD_DIM
        sn = jnp.sin(ang)
        c_ref[...] = jnp.cos(ang)
        s1_ref[...] = jnp.where(lane < ROPE_DIM // 2, -sn, 0.0)
        s2_ref[...] = jnp.where((lane >= ROPE_DIM // 2) & (lane < ROPE_DIM), sn, 0.0)

    out = jax.ShapeDtypeStruct((s, LANES), F32)
    return pl.pallas_call(
        body, name="rope_tables", grid=(s // tm,), out_shape=(out, out, out),
        in_specs=[_rows(tm, 1), _resident((1, LANES))],
        out_specs=(_rows(tm, LANES),) * 3, compiler_params=_params("parallel"),
    )(pos_col, invf)


def _inproj_fwd(x, attn_norm, win_t, b_gate, tm):
    s = x.shape[0]

    def body(x_ref, gn_ref, w_ref, bg_ref, h_ref, u_ref, q_ref, kv_ref, g_ref):
        h, _ = _rmsnorm_fwd(x_ref[...], gn_ref[...])
        h = h.astype(MM)
        h_ref[...] = h
        u_ref[...] = _mm_nt(h, w_ref[O_U:O_Q, :])
        q_ref[...] = _mm_nt(h, w_ref[O_Q:O_KV, :])
        kv_ref[...] = _mm_nt(h, w_ref[O_KV:O_G, :])
        g_ref[...] = jax.nn.sigmoid(_mm_nt(h, w_ref[O_G:IN_WIDTH, :]) + bg_ref[...])

    sd = jax.ShapeDtypeStruct
    return pl.pallas_call(
        body, name="inproj_fwd", grid=(s // tm,),
        out_shape=(sd((s, D_MODEL), MM), sd((s, 1024), F32), sd((s, 1024), F32), sd((s, 256), F32),
                   sd((s, 2048), F32)),
        in_specs=[_rows(tm, D_MODEL), _resident((1, D_MODEL)), _resident((IN_WIDTH, D_MODEL)), _resident((1, 2048))],
        out_specs=(_rows(tm, D_MODEL), _rows(tm, 1024), _rows(tm, 1024), _rows(tm, 256), _rows(tm, 2048)),
        compiler_params=_params("parallel"),
    )(x, attn_norm, win_t, b_gate)


def _window_sums(src_ref, s1_ref, s2_ref, cols, w, tm, ahead):
    n = POOL_HALO + tm

    def level(src, width, lo):
        if ahead:
            return src(0, n - lo) + src(width, n - lo)
        return src(lo, n - lo) + src(lo - width, n - lo)

    def final(src, width):
        if ahead:
            return src(0, tm) + src(width, tm)
        return src(POOL_HALO, tm) + src(POOL_HALO - width, tm)

    levels = [lambda r0, rows: src_ref[pl.ds(r0, rows), cols]]
    for k, ref in ((1, s1_ref), (2, s2_ref), (3, s1_ref)):
        if w == 2 ** k:
            break
        lo = 8 * k
        ref[pl.ds(0 if ahead else lo, n - lo), :] = level(levels[-1], 2 ** (k - 1), lo)
        levels.append(functools.partial(lambda ref, r0, rows: ref[pl.ds(r0, rows), :], ref))
    return final(levels[-1], w // 2)


def _pooled(ext_ref, s1_ref, s2_ref, tm, row0):
    t = (row0 + lax.broadcasted_iota(jnp.int32, (tm, 1), 0)).astype(F32)
    out = []
    for gi, w in enumerate(POOL_WINDOWS):
        cols = slice(gi * POOL_GROUP, (gi + 1) * POOL_GROUP)
        acc = _window_sums(ext_ref, s1_ref, s2_ref, cols, w, tm, ahead=False)
        cnt = jnp.minimum(t + 1.0, float(w))
        out.append(acc / cnt - ext_ref[pl.ds(POOL_HALO, tm), cols])
    return out


def _pool_fwd(u, wpool, pool_scale, tm):
    s = u.shape[0]
    hb = tm // POOL_HALO

    def body(u_ref, halo_ref, wp_ref, ps_ref, a_ref, ext_ref, s1_ref, s2_ref):
        i = pl.program_id(0)
        ext_ref[pl.ds(0, POOL_HALO), :] = jnp.where(i > 0, halo_ref[...], 0.0)
        ext_ref[pl.ds(POOL_HALO, tm), :] = u_ref[...]
        pooled = _pooled(ext_ref, s1_ref, s2_ref, tm, i * tm)
        for gi in range(4):
            cols = slice(gi * POOL_GROUP, (gi + 1) * POOL_GROUP)
            a_ref[:, cols] = (_mm(pooled[gi], wp_ref[gi]) * ps_ref[:, cols]).astype(a_ref.dtype)

    return pl.pallas_call(
        body, name="pool_fwd", grid=(s // tm,), out_shape=jax.ShapeDtypeStruct((s, 1024), ACT),
        in_specs=[_rows(tm, 1024), pl.BlockSpec((POOL_HALO, 1024), lambda i: (jnp.maximum(i * hb - 1, 0), 0)),
                  _resident((4, POOL_GROUP, POOL_GROUP)), _resident((1, 1024))],
        out_specs=_rows(tm, 1024),
        scratch_shapes=[pltpu.VMEM((POOL_HALO + tm, 1024), F32), pltpu.VMEM((POOL_HALO + tm, POOL_GROUP), F32),
                        pltpu.VMEM((POOL_HALO + tm, POOL_GROUP), F32)],
        compiler_params=_params("parallel"),
    )(u, u, wpool, pool_scale)


PAIRS = GQA_GROUP // 2
PAIR_COLS = PAIRS * BLOCK


def _fold_masks(n):
    r = lax.broadcasted_iota(jnp.int32, (BLOCK, BLOCK), 0)
    i = lax.broadcasted_iota(jnp.int32, (BLOCK, BLOCK), 1)
    prev = r > i
    return prev, jnp.where(prev & (n == 0), NEG, 0.0)


def _fold(band, prev):
    return jnp.where(prev, band[0:BLOCK, :], band[BLOCK:2 * BLOCK, :])


def _unfold(folded, prev):
    top = jnp.where(prev, folded, 0.0)
    return jnp.concatenate([top, folded - top], axis=0).astype(MM)


def _probs_by_pair(sc, sink, masks):
    prev, bias = masks
    out = []
    for j in range(PAIRS):
        cols = slice(j * BLOCK, (j + 1) * BLOCK)
        out.append(_softmax_sink_t(_fold(sc[:, cols], prev) + bias, sink[:, cols]))
    return out


def _stack_pairs(x, hk):
    return jnp.concatenate([x[:, (PAIRS * hk + j) * LANES:(PAIRS * hk + j + 1) * LANES] for j in range(PAIRS)], axis=0)


def _parity_bands(t, hk):
    low = lax.broadcasted_iota(jnp.int32, t.shape, 1) < HEAD_DIM
    own = jnp.where(low if hk == 0 else ~low, t, 0.0)
    other = pltpu.roll(own, HEAD_DIM, 1)
    return (own, other) if hk == 0 else (other, own)


def _fold_parity(even, odd, hk):
    low = lax.broadcasted_iota(jnp.int32, even.shape, 1) < HEAD_DIM
    comb = jnp.where(low, even, odd)
    comb = comb + pltpu.roll(comb, HEAD_DIM, 1)
    return jnp.where(low if hk == 0 else ~low, comb, 0.0)


def _softmax_sink_t(s, sink):
    m = jnp.maximum(jnp.max(s, axis=0, keepdims=True), sink)
    p = jnp.exp(s - m)
    es = jnp.exp(sink - m)
    inv = 1.0 / (jnp.sum(p, axis=0, keepdims=True) + es)
    return p * inv, es * inv


def _attn_fwd(q, kv, tabs, q_norm_t, k_norm_t, sink_rows, bd):
    s = q.shape[0]
    nb = s // BLOCK
    scale = HEAD_DIM ** -0.5
    cur = lambda n: (n, 0)
    prv = lambda n: (jnp.maximum(n - 1, 0), 0)

    def body(q_ref, kvc_ref, kvp_ref, c_ref, s1_ref, s2_ref, cp_ref, s1p_ref, s2p_ref, qn_ref, kn_ref, sink_ref,
             bd_ref, o_ref, pr_ref, ps_ref):
        n = pl.program_id(0)
        bdm = bd_ref[...]
        c, s1, s2 = c_ref[...], s1_ref[...], s2_ref[...]
        qh, _ = _head_norm_fwd(q_ref[...], qn_ref[...], bdm)
        qr = (_rope(qh, _tile_lanes(c, 8), _tile_lanes(s1, 8), _tile_lanes(s2, 8)) * scale).astype(MM)
        kc, _ = _head_norm_fwd(kvc_ref[:, 0:128], kn_ref[...], bdm)
        kp, _ = _head_norm_fwd(kvp_ref[:, 0:128], kn_ref[...], bdm)
        k2 = jnp.concatenate([_rope(kp, cp_ref[...], s1p_ref[...], s2p_ref[...]), _rope(kc, c, s1, s2)], axis=0)
        v2 = jnp.concatenate([kvp_ref[:, 128:256], kvc_ref[:, 128:256]], axis=0)
        mask = _fold_masks(n)
        for hk in range(N_KV_HEADS):
            qs = _stack_pairs(qr, hk)
            ot = jnp.zeros((LANES, PAIR_COLS), F32)
            for par, (kb, vb) in enumerate(zip(_parity_bands(k2, hk), _parity_bands(v2, hk))):
                probs = _probs_by_pair(_mm_nt(kb, qs), sink_ref[hk, par], mask)
                pr_ref[0, 2 * hk + par] = jnp.concatenate([pr for pr, _ in probs], axis=1).astype(MM)
                ps_ref[0, 2 * hk + par] = jnp.concatenate([ps for _, ps in probs], axis=1)
                ot = ot + _mm(vb.T, jnp.concatenate([_unfold(pr, mask[0]) for pr, _ in probs], axis=1))
            for j in range(PAIRS):
                col = (PAIRS * hk + j) * LANES
                o_ref[:, col:col + LANES] = ot[:, j * BLOCK:(j + 1) * BLOCK].T.astype(o_ref.dtype)

    tab = lambda im: pl.BlockSpec((BLOCK, LANES), im)
    sd = jax.ShapeDtypeStruct
    return pl.pallas_call(
        body, name="attn_fwd", grid=(nb,),
        out_shape=(sd((s, 1024), ACT), sd((nb, 4, BLOCK, PAIR_COLS), MM), sd((nb, 4, 1, PAIR_COLS), F32)),
        in_specs=[pl.BlockSpec((BLOCK, 1024), cur), pl.BlockSpec((BLOCK, 256), cur), pl.BlockSpec((BLOCK, 256), prv),
                  tab(cur), tab(cur), tab(cur), tab(prv), tab(prv), tab(prv),
                  _resident((1, 1024)), _resident((1, 128)), _resident((N_KV_HEADS, 2, 1, PAIR_COLS)),
                  _resident((LANES, LANES))],
        out_specs=(pl.BlockSpec((BLOCK, 1024), cur), pl.BlockSpec((1, 4, BLOCK, PAIR_COLS), lambda n: (n, 0, 0, 0)),
                   pl.BlockSpec((1, 4, 1, PAIR_COLS), lambda n: (n, 0, 0, 0))),
        compiler_params=_params("parallel"),
    )(q, kv, kv, *tabs, *tabs, q_norm_t, k_norm_t, sink_rows, bd)


def _mix_out_fwd(x, g, a, b, wout, tm):
    s = x.shape[0]

    def body(x_ref, g_ref, a_ref, b_ref, w_ref, x1_ref, mix_ref):
        mix = (g_ref[:, 0:1024] * a_ref[...] + g_ref[:, 1024:2048] * b_ref[...]).astype(MM)
        mix_ref[...] = mix
        x1_ref[...] = x_ref[...] + _mm(mix, w_ref[...])

    return pl.pallas_call(
        body, name="mix_out_fwd", grid=(s // tm,),
        out_shape=(jax.ShapeDtypeStruct((s, D_MODEL), F32), jax.ShapeDtypeStruct((s, D_MODEL), MM)),
        in_specs=[_rows(tm, 1024), _rows(tm, 2048), _rows(tm, 1024), _rows(tm, 1024), _resident((1024, 1024))],
        out_specs=(_rows(tm, 1024), _rows(tm, 1024)), compiler_params=_params("parallel"),
    )(x, g, a, b, wout)


SHIFT_ROWS = 16


def _sublane_major_matrices(tm):
    r = np.arange(tm)
    pm = r[None, :] == ((tm // 8) * (r % 8) + r // 8)[:, None]
    return jnp.asarray(pm, MM), jnp.asarray(pm.T, MM)


def _to_sublane_major(pm, v):
    return jnp.dot(pm, v, preferred_element_type=F32).astype(MM)


def _to_time_order(pmt, v):
    hi = v.astype(MM)
    r1 = v - hi.astype(F32)
    mid = r1.astype(MM)
    lo = (r1 - mid.astype(F32)).astype(MM)
    dot = functools.partial(jnp.dot, preferred_element_type=F32)
    return dot(pmt, hi) + dot(pmt, mid) + dot(pmt, lo)


def _step_back(vreg_rows, before):
    sub = lax.broadcasted_iota(jnp.int32, vreg_rows.shape, 0)
    return jnp.where(sub == 0, before[7:8, :], pltpu.roll(vreg_rows, 1, 0))


def _step_ahead(vreg_rows, after):
    sub = lax.broadcasted_iota(jnp.int32, vreg_rows.shape, 0)
    return jnp.where(sub == 7, after[0:1, :], pltpu.roll(vreg_rows, 7, 0))


def _fill_back_rows(ext_ref, before, tm, cols):
    last = ext_ref[pl.ds(SHIFT_ROWS + tm - 8, 8), cols]
    pen = ext_ref[pl.ds(SHIFT_ROWS + tm - 16, 8), cols]
    ext_ref[pl.ds(8, 8), cols] = _step_back(last, before[8:16, :])
    ext_ref[pl.ds(0, 8), cols] = _step_back(pen, before[0:8, :])


def _causal_conv(uc, before, w, b, tm):
    back1 = _step_back(uc[tm - 8:tm, :], before[8:16, :])
    back2 = _step_back(uc[tm - 16:tm - 8, :], before[0:8, :])
    w0, w1, w2 = w[0:1, :], w[1:2, :], w[2:3, :]
    first = b + w0 * back2 + w1 * back1 + w2 * uc[0:8, :]
    second = b + w0 * back1 + w1 * uc[0:8, :] + w2 * uc[8:16, :]
    rest = b + w0 * uc[0:tm - 16, :] + w1 * uc[8:tm - 8, :] + w2 * uc[16:tm, :]
    return jnp.concatenate([first, second, rest], axis=0)


def _ffn_fwd(x1, ffn_norm, wup_t, conv_w, conv_b, wdown, target, tm):
    s = x1.shape[0]
    inv_d = 1.0 / D_MODEL
    pm, pmt = _sublane_major_matrices(tm)

    def body(x1_ref, gn_ref, wu_ref, cw_ref, cb_ref, wd_ref, tgt_ref, pm_ref, pmt_ref, h2_ref, u_ref, uc_ref, dy_ref,
             dyb_ref, loss_ref, carry_ref):
        i = pl.program_id(0)

        @pl.when(i == 0)
        def _():
            carry_ref[...] = jnp.zeros_like(carry_ref)
            loss_ref[...] = jnp.zeros_like(loss_ref)

        x1 = x1_ref[...]
        h2, _ = _rmsnorm_fwd(x1, gn_ref[...])
        h2 = _to_sublane_major(pm_ref[...], h2.astype(MM))
        h2_ref[...] = h2
        for c in range(4):
            cols = slice(c * FF_CHUNK, (c + 1) * FF_CHUNK)
            uc = _mm_nt(h2, wu_ref[cols, :])
            u_ref[:, cols] = uc
            uc_ref[:, cols] = _causal_conv(uc, carry_ref[:, cols], cw_ref[:, cols], cb_ref[:, cols], tm)
            carry_ref[:, cols] = uc[tm - SHIFT_ROWS:tm, :]
        down = jnp.zeros((tm, D_MODEL), F32)
        for c in range(2):
            gate = uc_ref[:, c * FF_CHUNK:(c + 1) * FF_CHUNK]
            val = uc_ref[:, D_FF + c * FF_CHUNK:D_FF + (c + 1) * FF_CHUNK]
            act = gate * jax.nn.sigmoid(gate) * val
            down = down + _mm(act, wd_ref[c * FF_CHUNK:(c + 1) * FF_CHUNK, :])
        err = x1 + _to_time_order(pmt_ref[...], down) - tgt_ref[...]
        loss_ref[...] += jnp.full(loss_ref.shape, 0.5 * inv_d * jnp.sum(err * err), F32)
        dy = err * inv_d
        dy_ref[...] = dy
        dyb_ref[...] = _to_sublane_major(pm_ref[...], dy.astype(MM))

    sd = jax.ShapeDtypeStruct
    return pl.pallas_call(
        body, name="ffn_fwd", grid=(s // tm,),
        out_shape=(sd((s, D_MODEL), MM), sd((s, 2 * D_FF), F32), sd((s, 2 * D_FF), F32), sd((s, D_MODEL), F32),
                   sd((s, D_MODEL), MM), sd((8, LANES), F32)),
        in_specs=[_rows(tm, 1024), _resident((1, 1024)), _resident((2 * D_FF, D_MODEL)), _resident((3, 2 * D_FF)),
                  _resident((1, 2 * D_FF)), _resident((D_FF, D_MODEL)), _rows(tm, 1024), _resident((tm, tm)),
                  _resident((tm, tm))],
        out_specs=(_rows(tm, 1024), _rows(tm, 2 * D_FF), _rows(tm, 2 * D_FF), _rows(tm, 1024), _rows(tm, 1024),
                   pl.BlockSpec((8, LANES), lambda i: (0, 0))),
        scratch_shapes=[pltpu.VMEM((SHIFT_ROWS, 2 * D_FF), F32)],
        compiler_params=_params("arbitrary"),
    )(x1, ffn_norm, wup_t, conv_w, conv_b, wdown, target, pm, pmt)


def _ffn_bwd_a(dyb, u, uc, conv_w, wdown, tm):
    s = dyb.shape[0]
    nt = s // tm
    hb = tm // SHIFT_ROWS
    rev = lambda i: (nt - 1 - i, 0)

    def body(dy_ref, u_ref, before_ref, uc_ref, cw_ref, wd_ref, du_ref, act_ref, dcw_ref, dcb_ref, ext_ref, extd_ref,
             ahead_ref):
        i = pl.program_id(0)
        first_tile = i == nt - 1

        @pl.when(i == 0)
        def _():
            ahead_ref[...] = jnp.zeros_like(ahead_ref)
            dcw_ref[...] = jnp.zeros_like(dcw_ref)
            dcb_ref[...] = jnp.zeros_like(dcb_ref)

        ext_ref[pl.ds(SHIFT_ROWS, tm), :] = u_ref[...]
        for c in range(4):
            cols = slice(c * FF_CHUNK, (c + 1) * FF_CHUNK)
            _fill_back_rows(ext_ref, jnp.where(first_tile, 0.0, before_ref[:, cols]), tm, cols)
        dy = dy_ref[...]
        for c in range(2):
            gate = uc_ref[:, c * FF_CHUNK:(c + 1) * FF_CHUNK]
            val = uc_ref[:, D_FF + c * FF_CHUNK:D_FF + (c + 1) * FF_CHUNK]
            sg = jax.nn.sigmoid(gate)
            sl = gate * sg
            act_ref[:, c * FF_CHUNK:(c + 1) * FF_CHUNK] = (sl * val).astype(MM)
            d_act = _mm_nt(dy, wd_ref[c * FF_CHUNK:(c + 1) * FF_CHUNK, :])
            extd_ref[pl.ds(0, tm), c * FF_CHUNK:(c + 1) * FF_CHUNK] = d_act * val * (sg * (1.0 + gate * (1.0 - sg)))
            extd_ref[pl.ds(0, tm), D_FF + c * FF_CHUNK:D_FF + (c + 1) * FF_CHUNK] = d_act * sl
        for c in range(4):
            cols = slice(c * FF_CHUNK, (c + 1) * FF_CHUNK)
            ahead = ahead_ref[:, cols]
            first2 = extd_ref[pl.ds(0, SHIFT_ROWS), cols]
            extd_ref[pl.ds(tm, 8), cols] = _step_ahead(first2[0:8, :], ahead[0:8, :])
            extd_ref[pl.ds(tm + 8, 8), cols] = _step_ahead(first2[8:16, :], ahead[8:16, :])
            ahead_ref[:, cols] = first2
            d0 = extd_ref[pl.ds(0, tm), cols]
            dcb_ref[:, cols] += jnp.sum(d0, axis=0, keepdims=True)
            for j in range(3):
                dcw_ref[j:j + 1, cols] += jnp.sum(d0 * ext_ref[pl.ds(8 * j, tm), cols], axis=0, keepdims=True)
            du = cw_ref[2:3, cols] * d0 + cw_ref[1:2, cols] * extd_ref[pl.ds(8, tm), cols]
            du = du + cw_ref[0:1, cols] * extd_ref[pl.ds(SHIFT_ROWS, tm), cols]
            du_ref[:, cols] = du.astype(MM)

    sd = jax.ShapeDtypeStruct
    return pl.pallas_call(
        body, name="ffn_bwd_a", grid=(nt,),
        out_shape=(sd((s, 2 * D_FF), MM), sd((s, D_FF), MM), sd((3, 2 * D_FF), F32), sd((1, 2 * D_FF), F32)),
        in_specs=[pl.BlockSpec((tm, D_MODEL), rev), pl.BlockSpec((tm, 2 * D_FF), rev),
                  pl.BlockSpec((SHIFT_ROWS, 2 * D_FF), lambda i: (jnp.maximum((nt - 1 - i) * hb - 1, 0), 0)),
                  pl.BlockSpec((tm, 2 * D_FF), rev), _resident((3, 2 * D_FF)), _resident((D_FF, D_MODEL))],
        out_specs=(pl.BlockSpec((tm, 2 * D_FF), rev), pl.BlockSpec((tm, D_FF), rev),
                   pl.BlockSpec((3, 2 * D_FF), lambda i: (0, 0)), pl.BlockSpec((1, 2 * D_FF), lambda i: (0, 0))),
        scratch_shapes=[pltpu.VMEM((SHIFT_ROWS + tm, 2 * D_FF), F32), pltpu.VMEM((tm + SHIFT_ROWS, 2 * D_FF), F32),
                        pltpu.VMEM((SHIFT_ROWS, 2 * D_FF), F32)],
        compiler_params=_params("arbitrary"),
    )(dyb, u, u, uc, conv_w, wdown)


def _after(after):
    tie = [] if after is None else list(after) if isinstance(after, (list, tuple)) else [after]
    return tie, [pl.BlockSpec(memory_space=pl.ANY)] * len(tie)


def _matmul_tn(a, b, tmo, tk, name, into=None, row0=0):
    s, m = a.shape
    n = b.shape[1]
    nk = s // tk
    rows = m if into is None else into.shape[0]
    assert row0 % LANES == 0 and tmo % LANES == 0
    grown, grown_spec = ([], []) if into is None else ([into], [ANY])

    def body(a_ref, b_ref, *rest):
        o_ref = rest[-1]
        k = pl.program_id(1)

        @pl.when(k == 0)
        def _():
            o_ref[...] = jnp.zeros_like(o_ref)

        o_ref[...] += _mm_tn(a_ref[...], b_ref[pl.ds(pl.multiple_of(k * tk, tk), tk), :])

    return pl.pallas_call(
        body, name=name, grid=(m // tmo, nk), out_shape=jax.ShapeDtypeStruct((rows, n), F32),
        in_specs=[pl.BlockSpec((tk, tmo), lambda i, k: (k, i)), _resident((s, n))] + grown_spec,
        out_specs=pl.BlockSpec((pl.Element(tmo), pl.Element(n)), lambda i, k: (pl.multiple_of(row0 + i * tmo, LANES), 0)),
        input_output_aliases={2: 0} if grown else {},
        compiler_params=_params("parallel", "arbitrary"),
    )(a, b, *grown)


def _ffn_bwd_b(du, wup_t, x1, ffn_norm, dy, tm):
    s = du.shape[0]

    def body(du_ref, wu_ref, x1_ref, gn_ref, dy_ref, pmt_ref, dx1_ref, dx1b_ref, dg_ref):
        @pl.when(pl.program_id(0) == 0)
        def _():
            dg_ref[...] = jnp.zeros_like(dg_ref)

        dh2 = _to_time_order(pmt_ref[...], _mm(du_ref[...], wu_ref[...]))
        x1 = x1_ref[...]
        _, r = _rmsnorm_fwd(x1, gn_ref[...])
        dx, dgr = _rmsnorm_bwd(x1, r, gn_ref[...], dh2)
        dg_ref[...] += jnp.sum(dgr, axis=0, keepdims=True)
        dx1 = dy_ref[...] + dx
        dx1_ref[...] = dx1
        dx1b_ref[...] = dx1.astype(MM)

    return pl.pallas_call(
        body, name="ffn_bwd_b", grid=(s // tm,),
        out_shape=(jax.ShapeDtypeStruct((s, D_MODEL), F32), jax.ShapeDtypeStruct((s, D_MODEL), MM),
                   jax.ShapeDtypeStruct((1, D_MODEL), F32)),
        in_specs=[_rows(tm, 2 * D_FF), _resident((2 * D_FF, D_MODEL)), _rows(tm, 1024), _resident((1, 1024)),
                  _rows(tm, 1024), _resident((tm, tm))],
        out_specs=(_rows(tm, 1024), _rows(tm, 1024), pl.BlockSpec((1, D_MODEL), lambda i: (0, 0))),
        compiler_params=_params("arbitrary"),
    )(du, wup_t, x1, ffn_norm, dy, _sublane_major_matrices(tm)[1])


def _win_rows(row0, rows):
    return pl.BlockSpec((pl.Element(rows), pl.Element(D_MODEL)), lambda i: (row0, 0), pipeline_mode=pl.Buffered(1))


def _mix_bwd(dx1b, wout, g, a, b, mix, h, tm, after=None):
    s = dx1b.shape[0]
    tie, tie_spec = _after(after)

    def body(dx_ref, w_ref, g_ref, a_ref, b_ref, mix_ref, h_ref, *rest):
        da_ref, db_ref, dzg_ref, dbg_ref, dwo_ref, dwin_ref = rest[-6:]

        @pl.when(pl.program_id(0) == 0)
        def _():
            dbg_ref[...] = jnp.zeros_like(dbg_ref)
            dwo_ref[...] = jnp.zeros_like(dwo_ref)
            dwin_ref[...] = jnp.zeros_like(dwin_ref)

        dx = dx_ref[...]
        dwo_ref[...] += _mm_tn(mix_ref[...], dx)
        dmix = _mm_nt(dx, w_ref[...])
        for half, src, dst in ((0, a_ref, da_ref), (1, b_ref, db_ref)):
            cols = slice(half * 1024, (half + 1) * 1024)
            gt = g_ref[:, cols]
            dst[...] = (dmix * gt).astype(dst.dtype)
            dz = dmix * src[...] * gt * (1.0 - gt)
            dzb = dz.astype(MM)
            dzg_ref[:, cols] = dzb
            dwin_ref[cols, :] += _mm_tn(dzb, h_ref[...])
            dbg_ref[:, cols] += jnp.sum(dz, axis=0, keepdims=True)

    sd = jax.ShapeDtypeStruct
    return pl.pallas_call(
        body, name="mix_bwd", grid=(s // tm,),
        out_shape=(sd((s, 1024), F32), sd((s, 1024), MM), sd((s, 2048), MM), sd((1, 2048), F32),
                   sd((D_MODEL, D_MODEL), F32), sd((IN_WIDTH, D_MODEL), F32)),
        in_specs=[_rows(tm, 1024), _resident((1024, 1024)), _rows(tm, 2048), _rows(tm, 1024), _rows(tm, 1024),
                  _rows(tm, 1024), _rows(tm, 1024)] + tie_spec,
        out_specs=(_rows(tm, 1024), _rows(tm, 1024), _rows(tm, 2048), pl.BlockSpec((1, 2048), lambda i: (0, 0)),
                   _resident((D_MODEL, D_MODEL)), _win_rows(O_G, IN_WIDTH - O_G)),
        compiler_params=_params("arbitrary"),
    )(dx1b, wout, g, a, b, mix, h, *tie)


def _pool_bwd(u, da, wpool, pool_scale, h, d_win_t, tm, after=None):
    s = u.shape[0]
    nt = s // tm
    hb = tm // POOL_HALO

    tie, tie_spec = _after(after)

    def body(u_ref, uh_ref, da_ref, dah_ref, wp_ref, ps_ref, h_ref, *rest):
        dzu_ref, dwp_ref, dps_ref, dwin_ref, ext_ref, exte_ref, s1_ref, s2_ref = rest[-8:]
        i = pl.program_id(0)

        @pl.when(i == 0)
        def _():
            dwp_ref[...] = jnp.zeros_like(dwp_ref)
            dps_ref[...] = jnp.zeros_like(dps_ref)
            dwin_ref[...] = jnp.zeros_like(dwin_ref)

        ext_ref[pl.ds(0, POOL_HALO), :] = jnp.where(i > 0, uh_ref[...], 0.0)
        ext_ref[pl.ds(POOL_HALO, tm), :] = u_ref[...]
        pooled = _pooled(ext_ref, s1_ref, s2_ref, tm, i * tm)
        da = da_ref[...]
        dah = jnp.where(i < nt - 1, dah_ref[...], 0.0)
        t = (i * tm + lax.broadcasted_iota(jnp.int32, (tm + POOL_HALO, 1), 0)).astype(F32)
        for gi, w in enumerate(POOL_WINDOWS):
            cols = slice(gi * POOL_GROUP, (gi + 1) * POOL_GROUP)
            pg = pooled[gi].astype(MM)
            wg = wp_ref[gi]
            mixed = _mm(pg, wg)
            dps_ref[:, cols] += jnp.sum(da[:, cols] * mixed, axis=0, keepdims=True)
            dmx = (da[:, cols] * ps_ref[:, cols]).astype(MM)
            dwp_ref[gi] += _mm_tn(pg, dmx)
            dpl = _mm_nt(dmx, wg)
            dplh = _mm_nt(dah[:, cols] * ps_ref[:, cols], wg)
            cnt = jnp.minimum(t + 1.0, float(w))
            exte_ref[pl.ds(0, tm), cols] = dpl / cnt[0:tm]
            exte_ref[pl.ds(tm, POOL_HALO), cols] = dplh / cnt[tm:tm + POOL_HALO]
            acc = _window_sums(exte_ref, s1_ref, s2_ref, cols, w, tm, ahead=True)
            dzu = (acc - dpl).astype(MM)
            dzu_ref[:, cols] = dzu
            dwin_ref[cols, :] += _mm_tn(dzu, h_ref[...])

    sd = jax.ShapeDtypeStruct
    last_halo = s // POOL_HALO - 1
    return pl.pallas_call(
        body, name="pool_bwd", grid=(nt,),
        out_shape=(sd((s, 1024), MM), sd((4, POOL_GROUP, POOL_GROUP), F32), sd((1, 1024), F32),
                   sd((IN_WIDTH, D_MODEL), F32)),
        in_specs=[_rows(tm, 1024), pl.BlockSpec((POOL_HALO, 1024), lambda i: (jnp.maximum(i * hb - 1, 0), 0)),
                  _rows(tm, 1024),
                  pl.BlockSpec((POOL_HALO, 1024), lambda i: (jnp.minimum((i + 1) * hb, last_halo), 0)),
                  _resident((4, POOL_GROUP, POOL_GROUP)), _resident((1, 1024)), _rows(tm, 1024)] + tie_spec + [ANY],
        out_specs=(_rows(tm, 1024), pl.BlockSpec((4, POOL_GROUP, POOL_GROUP), lambda i: (0, 0, 0)),
                   pl.BlockSpec((1, 1024), lambda i: (0, 0)), _win_rows(O_U, O_Q - O_U)),
        input_output_aliases={7 + len(tie): 3},
        scratch_shapes=[pltpu.VMEM((POOL_HALO + tm, 1024), F32), pltpu.VMEM((tm + POOL_HALO, 1024), F32),
                        pltpu.VMEM((POOL_HALO + tm, POOL_GROUP), F32), pltpu.VMEM((POOL_HALO + tm, POOL_GROUP), F32)],
        compiler_params=_params("arbitrary"),
    )(u, u, da, da, wpool, pool_scale, h, *tie, d_win_t)


def _attn_bwd(q, kv, db, tabs, q_norm_t, k_norm_t, probs, sink_probs, bd, after=None):
    s = q.shape[0]
    nb = s // BLOCK
    scale = HEAD_DIM ** -0.5
    cur = lambda n: (jnp.minimum(n, nb - 1), 0)
    prv = lambda n: (jnp.maximum(n - 1, 0), 0)
    tie, tie_spec = _after(after)

    def body(q_ref, kvc_ref, kvp_ref, db_ref, c_ref, s1_ref, s2_ref, cp_ref, s1p_ref, s2p_ref, qn_ref, kn_ref,
             pr_ref, ps_ref, bd_ref, *rest):
        (dzq_ref, dzkv_ref, dqn_ref, dkn_ref, dsk_ref,
         carry_ref, tot_ref, dqr_ref, qacc_ref, kacc_ref, sacc_ref) = rest[-11:]
        n = pl.program_id(0)
        bdm = bd_ref[...]
        kn = kn_ref[...]

        @pl.when(n == 0)
        def _():
            carry_ref[...] = jnp.zeros_like(carry_ref)
            qacc_ref[...] = jnp.zeros_like(qacc_ref)
            kacc_ref[...] = jnp.zeros_like(kacc_ref)
            sacc_ref[...] = jnp.zeros_like(sacc_ref)

        kp_raw = kvp_ref[:, 0:128]
        kph, rp = _head_norm_fwd(kp_raw, kn, bdm)
        cp, s1p, s2p = cp_ref[...], s1p_ref[...], s2p_ref[...]

        @pl.when(n < nb)
        def _():
            c, s1, s2 = c_ref[...], s1_ref[...], s2_ref[...]
            c8, s18, s28 = _tile_lanes(c, 8), _tile_lanes(s1, 8), _tile_lanes(s2, 8)
            q_raw = q_ref[...]
            qh, rq = _head_norm_fwd(q_raw, qn_ref[...], bdm)
            qr = (_rope(qh, c8, s18, s28) * scale).astype(MM)
            kc, _ = _head_norm_fwd(kvc_ref[:, 0:128], kn, bdm)
            k2 = jnp.concatenate([_rope(kph, cp, s1p, s2p), _rope(kc, c, s1, s2)], axis=0)
            v2 = jnp.concatenate([kvp_ref[:, 128:256], kvc_ref[:, 128:256]], axis=0)
            dob = db_ref[...].astype(MM)
            mask = _fold_masks(n)
            lane = lax.broadcasted_iota(jnp.int32, (1, LANES), 1)
            dsk = jnp.zeros((1, LANES), F32)
            dk2 = jnp.zeros((2 * BLOCK, LANES), F32)
            dv2 = jnp.zeros((2 * BLOCK, LANES), F32)
            for hk in range(N_KV_HEADS):
                qs = _stack_pairs(qr, hk)
                do = _stack_pairs(dob, hk)
                dqt = jnp.zeros((LANES, PAIR_COLS), F32)
                dkb, dvb = [], []
                for par, (kb, vb) in enumerate(zip(_parity_bands(k2, hk), _parity_bands(v2, hk))):
                    dp = _mm_nt(vb, do)
                    prs, dss = [], []
                    for j in range(PAIRS):
                        cols = slice(j * BLOCK, (j + 1) * BLOCK)
                        pr = pr_ref[0, 2 * hk + par, :, cols].astype(F32)
                        psink = ps_ref[0, 2 * hk + par, :, cols]
                        dpj = _fold(dp[:, cols], mask[0])
                        coldot = jnp.sum(pr * dpj, axis=0, keepdims=True)
                        dss.append(_unfold(pr * (dpj - coldot), mask[0]))
                        prs.append(_unfold(pr, mask[0]))
                        h = hk * GQA_GROUP + 2 * j + par
                        dsk = dsk + jnp.where(lane == h, jnp.sum(-psink * coldot), 0.0)
                    ds, pr = jnp.concatenate(dss, axis=1), jnp.concatenate(prs, axis=1)
                    dqt = dqt + _mm(kb.T, ds)
                    dkb.append(_mm(ds, qs))
                    dvb.append(_mm(pr, do))
                for j in range(PAIRS):
                    col = (PAIRS * hk + j) * LANES
                    dqr_ref[:, col:col + LANES] = dqt[:, j * BLOCK:(j + 1) * BLOCK].T
                dk2 = dk2 + _fold_parity(dkb[0], dkb[1], hk)
                dv2 = dv2 + _fold_parity(dvb[0], dvb[1], hk)
            tot_ref[:, 0:128] = carry_ref[:, 0:128] + dk2[0:BLOCK, :]
            tot_ref[:, 128:256] = carry_ref[:, 128:256] + dv2[0:BLOCK, :]
            carry_ref[:, 0:128] = dk2[BLOCK:2 * BLOCK, :]
            carry_ref[:, 128:256] = dv2[BLOCK:2 * BLOCK, :]
            sacc_ref[...] += dsk
            dqh = _rope_bwd(dqr_ref[...] * scale, c8, s18, s28)
            dq, dgq = _head_norm_bwd(q_raw, rq, qn_ref[...], dqh, bdm)
            dzq_ref[...] = dq.astype(MM)
            qacc_ref[...] += jnp.sum(dgq, axis=0, keepdims=True)

        @pl.when(n == nb)
        def _():
            tot_ref[...] = carry_ref[...]

        dkh = _rope_bwd(tot_ref[:, 0:128], cp, s1p, s2p)
        dkr, dgk = _head_norm_bwd(kp_raw, rp, kn, dkh, bdm)
        dzkv_ref[:, 0:128] = dkr.astype(MM)
        dzkv_ref[:, 128:256] = tot_ref[:, 128:256].astype(MM)
        kacc_ref[...] += jnp.where(n > 0, jnp.sum(dgk, axis=0, keepdims=True), 0.0)

        @pl.when(n == nb)
        def _():
            fold = qacc_ref[:, 0:HEAD_DIM]
            for h in range(1, N_Q_HEADS):
                fold = fold + qacc_ref[:, h * HEAD_DIM:(h + 1) * HEAD_DIM]
            dqn_ref[...] = fold
            dkn_ref[...] = kacc_ref[:, 0:HEAD_DIM] + kacc_ref[:, HEAD_DIM:2 * HEAD_DIM]
            dsk_ref[...] = sacc_ref[...]

    tab = lambda im: pl.BlockSpec((BLOCK, LANES), im)
    sd = jax.ShapeDtypeStruct
    const = lambda n: (0, 0)
    return pl.pallas_call(
        body, name="attn_bwd", grid=(nb + 1,),
        out_shape=(sd((s, 1024), MM), sd((s, 256), MM), sd((1, HEAD_DIM), F32), sd((1, HEAD_DIM), F32),
                   sd((1, LANES), F32)),
        in_specs=[pl.BlockSpec((BLOCK, 1024), cur), pl.BlockSpec((BLOCK, 256), cur), pl.BlockSpec((BLOCK, 256), prv),
                  pl.BlockSpec((BLOCK, 1024), cur), tab(cur), tab(cur), tab(cur), tab(prv), tab(prv), tab(prv),
                  _resident((1, 1024)), _resident((1, 128)),
                  pl.BlockSpec((1, 4, BLOCK, PAIR_COLS), lambda n: (jnp.minimum(n, nb - 1), 0, 0, 0)),
                  pl.BlockSpec((1, 4, 1, PAIR_COLS), lambda n: (jnp.minimum(n, nb - 1), 0, 0, 0)),
                  _resident((LANES, LANES))] + tie_spec,
        out_specs=(pl.BlockSpec((BLOCK, 1024), cur), pl.BlockSpec((BLOCK, 256), prv),
                   pl.BlockSpec((1, HEAD_DIM), const), pl.BlockSpec((1, HEAD_DIM), const),
                   pl.BlockSpec((1, LANES), const)),
        scratch_shapes=[pltpu.VMEM((BLOCK, 256), F32), pltpu.VMEM((BLOCK, 256), F32), pltpu.VMEM((BLOCK, 1024), F32),
                        pltpu.VMEM((1, 1024), F32), pltpu.VMEM((1, 128), F32), pltpu.VMEM((1, LANES), F32)],
        compiler_params=_params("arbitrary"),
    )(q, kv, kv, db, *tabs, *tabs, q_norm_t, k_norm_t, probs, sink_probs, bd, *tie)


def _inproj_bwd(dzu, dzq, dzkv, dzg, win_t, x, attn_norm, dx1, tm, after=None):
    s = x.shape[0]
    tie, tie_spec = _after(after)

    def body(du_ref, dq_ref, dkv_ref, dg_ref, w_ref, x_ref, gn_ref, dx1_ref, *rest):
        gx_ref, dgn_ref = rest[-2:]

        @pl.when(pl.program_id(0) == 0)
        def _():
            dgn_ref[...] = jnp.zeros_like(dgn_ref)

        dh = _mm(du_ref[...], w_ref[O_U:O_Q, :]) + _mm(dq_ref[...], w_ref[O_Q:O_KV, :])
        dh = dh + _mm(dkv_ref[...], w_ref[O_KV:O_G, :]) + _mm(dg_ref[...], w_ref[O_G:IN_WIDTH, :])
        x = x_ref[...]
        _, r = _rmsnorm_fwd(x, gn_ref[...])
        dx, dgr = _rmsnorm_bwd(x, r, gn_ref[...], dh)
        dgn_ref[...] += jnp.sum(dgr, axis=0, keepdims=True)
        gx_ref[...] = dx1_ref[...] + dx

    return pl.pallas_call(
        body, name="inproj_bwd", grid=(s // tm,),
        out_shape=(jax.ShapeDtypeStruct((s, D_MODEL), F32), jax.ShapeDtypeStruct((1, D_MODEL), F32)),
        in_specs=[_rows(tm, 1024), _rows(tm, 1024), _rows(tm, 256), _rows(tm, 2048),
                  _resident((IN_WIDTH, D_MODEL)), _rows(tm, 1024), _resident((1, 1024)), _rows(tm, 1024)] + tie_spec,
        out_specs=(_rows(tm, 1024), pl.BlockSpec((1, D_MODEL), lambda i: (0, 0))),
        compiler_params=_params("arbitrary"),
    )(dzu, dzq, dzkv, dzg, win_t, x, attn_norm, dx1, *tie)


def _attention_constants(q_norm, k_norm, sinks):
    inv_freq = np.float32(ROPE_THETA) ** (-np.arange(0, ROPE_DIM, 2, dtype=np.float32) / np.float32(ROPE_DIM))
    lane = np.arange(LANES) % HEAD_DIM
    invf = jnp.asarray(np.where(lane < ROPE_DIM, inv_freq[lane % (ROPE_DIM // 2)], 0.0).reshape(1, LANES), F32)
    bd = jnp.asarray(np.arange(LANES)[:, None] // HEAD_DIM == np.arange(LANES)[None, :] // HEAD_DIM, MM)
    q_norm_t = jnp.tile(q_norm, (1, N_Q_HEADS))
    k_norm_t = jnp.tile(k_norm, (1, N_KV_HEADS))
    sink_rows = jnp.repeat(sinks.reshape(N_KV_HEADS, PAIRS, 2).transpose(0, 2, 1), BLOCK, axis=2)
    sink_rows = sink_rows.reshape(N_KV_HEADS, 2, 1, PAIR_COLS)
    return invf, bd, q_norm_t, k_norm_t, sink_rows


ANY = pl.BlockSpec(memory_space=pl.ANY)


def _position():
    return lax.axis_index("x"), lax.axis_index("y"), lax.axis_index("c")


def _all_gather(shards):
    k = len(shards)

    def body(*refs):
        ins, outs = refs[:k], refs[k:2 * k]
        send_sems, recv_sems, local_sems = refs[2 * k:]
        x, y, c = _position()
        me, sibling = (x, y, c), (x, y, 1 - c)
        chips = [(1 - x, y), (x, 1 - y), (1 - x, 1 - y)]

        def copy(a, kk, block, to, src=None):
            dst = outs[a].at[4 * block[0] + 2 * block[1] + block[2]]
            return pltpu.make_async_remote_copy(
                src_ref=dst if src is None else src, dst_ref=dst, send_sem=send_sems.at[a * 7 + kk],
                recv_sem=recv_sems.at[a * 7 + kk], device_id=to, device_id_type=MESH)

        mine = [pltpu.make_async_copy(ins[a], outs[a].at[4 * x + 2 * y + c], local_sems.at[a]) for a in range(k)]
        for cp in mine:
            cp.start()
        first = []
        for a in range(k):
            first.append(copy(a, 0, me, sibling, src=ins[a]))
            first += [copy(a, 1 + j, me, (*chip, c), src=ins[a]) for j, chip in enumerate(chips)]
        for cp in first:
            cp.start()
        passed = []
        for j, chip in enumerate(chips):
            for a in range(k):
                copy(a, 1 + j, (*chip, c), me).wait_recv()
                cp = copy(a, 4 + j, (*chip, c), sibling)
                cp.start()
                passed.append(cp)
        for a in range(k):
            copy(a, 0, sibling, me).wait_recv()
            for j, chip in enumerate(chips):
                copy(a, 4 + j, (*chip, 1 - c), me).wait_recv()
        for cp in first + passed:
            cp.wait_send()
        for cp in mine:
            cp.wait()

    return pl.pallas_call(
        body, name="all_gather_weights",
        out_shape=tuple(jax.ShapeDtypeStruct((N_DEV,) + s.shape, s.dtype) for s in shards),
        in_specs=[ANY] * k, out_specs=(ANY,) * k,
        scratch_shapes=[pltpu.SemaphoreType.DMA((7 * k,)), pltpu.SemaphoreType.DMA((7 * k,)),
                        pltpu.SemaphoreType.DMA((k,))],
    )(*shards)


HBM = pl.BlockSpec(memory_space=pltpu.HBM)
SEM = pl.BlockSpec(memory_space=pltpu.SEMAPHORE)
EFFECT = pltpu.SideEffectType.DATAFLOW_SIDE_EFFECTING


def _exchange_start(name, bufs, n_sems, copies, after=None):
    k = len(bufs)
    tie, tie_spec = _after(after)
    n_in = k + len(tie)

    def body(*refs):
        for cp in copies(refs[:k], refs[n_in], refs[n_in + 1]):
            cp.start()
        refs[-1][...] = jnp.zeros_like(refs[-1])

    dma = pltpu.SemaphoreType.DMA((n_sems,))
    out = pl.pallas_call(
        body, name=name,
        out_shape=(dma, dma, *[pltpu.HBM(b.shape, b.dtype) for b in bufs], jax.ShapeDtypeStruct((8, LANES), F32)),
        in_specs=[HBM] * k + tie_spec, out_specs=(SEM, SEM, *[HBM] * k, pl.BlockSpec(memory_space=pltpu.VMEM)),
        input_output_aliases={i: 2 + i for i in range(k)},
        compiler_params=pltpu.CompilerParams(has_side_effects=EFFECT),
    )(*[pltpu.with_memory_space_constraint(b, pltpu.HBM) for b in bufs], *tie)
    return out[0], out[1], list(out[2:2 + k]), out[-1]


def _exchange_mid(name, bufs, sems_in, n_sems, waits, copies, after):
    k, ns = len(bufs), len(sems_in)

    def body(*refs):
        ins = refs[:k]
        waits(ins, *refs[k:k + ns])
        for cp in copies(ins, refs[k + ns + 1], refs[k + ns + 2]):
            cp.start()

    dma = pltpu.SemaphoreType.DMA((n_sems,))
    out = pl.pallas_call(
        body, name=name, out_shape=(dma, dma, *[pltpu.HBM(b.shape, b.dtype) for b in bufs]),
        in_specs=[HBM] * k + [SEM] * ns + [ANY], out_specs=(SEM, SEM, *[HBM] * k),
        input_output_aliases={i: 2 + i for i in range(k)},
        compiler_params=pltpu.CompilerParams(has_side_effects=EFFECT),
    )(*bufs, *sems_in, after)
    return out[0], out[1], list(out[2:])


def _exchange_wait(name, bufs, sems, waits, after=None):
    k, ns = len(bufs), len(sems)
    tie, tie_spec = _after(after)

    def body(*refs):
        waits(refs[:k], *refs[k:k + ns])

    out = pl.pallas_call(
        body, name=name, out_shape=tuple(pltpu.HBM(b.shape, b.dtype) for b in bufs),
        in_specs=[HBM] * k + [SEM] * ns + tie_spec, out_specs=(HBM,) * k,
        input_output_aliases={i: i for i in range(k)},
        compiler_params=pltpu.CompilerParams(has_side_effects=EFFECT),
    )(*bufs, *sems, *tie)
    return list(out)


def _gather_copies(k, direct):
    def copies(refs, send_sems, recv_sems):
        x, y, c = _position()
        chips = [(1 - x, y), (x, 1 - y), (1 - x, 1 - y)]
        out = []
        for a in range(k):
            land = refs[k + a]
            if direct:
                mine = land.at[4 * x + 2 * y + c]
                for kk, to in enumerate([(x, y, 1 - c)] + [(*chip, c) for chip in chips]):
                    out.append(pltpu.make_async_remote_copy(
                        src_ref=refs[a], dst_ref=mine, send_sem=send_sems.at[4 * a + kk],
                        recv_sem=recv_sems.at[4 * a + kk], device_id=to, device_id_type=MESH))
            else:
                for j, (px, py) in enumerate(chips):
                    slot = land.at[4 * px + 2 * py + c]
                    out.append(pltpu.make_async_remote_copy(
                        src_ref=slot, dst_ref=slot, send_sem=send_sems.at[3 * a + j], recv_sem=recv_sems.at[3 * a + j],
                        device_id=(x, y, 1 - c), device_id_type=MESH))
        return out
    return copies


def _all_gather_behind(shards, start_after, mid_after):
    k = len(shards)
    me = 4 * lax.axis_index("x") + 2 * lax.axis_index("y") + lax.axis_index("c")
    lands = [lax.dynamic_update_slice(lax.empty((N_DEV,) + s.shape, s.dtype), s[None], (me, 0, 0)) for s in shards]
    direct, passed = _gather_copies(k, True), _gather_copies(k, False)

    send_a, recv_a, bufs, token = _exchange_start("gather_start", list(shards) + lands, 4 * k, direct, start_after)

    def finish():
        def wait_ici(refs, send_sems, recv_sems):
            for i, cp in enumerate(direct(refs, send_sems, recv_sems)):
                if i % 4:
                    cp.wait_recv()

        send_b, recv_b, bufs2 = _exchange_mid("gather_pass", bufs, [send_a, recv_a], 3 * k, wait_ici, passed,
                                              mid_after())

        def wait_all(refs, sa, ra, sb, rb):
            for i, cp in enumerate(direct(refs, sa, ra)):
                cp.wait_send()
                if i % 4 == 0:
                    cp.wait_recv()
            for cp in passed(refs, sb, rb):
                cp.wait()

        return _exchange_wait("gather_wait", bufs2, [send_a, recv_a, send_b, recv_b], wait_all)[k:]

    return token, finish


def _pair_copies(k):
    def copies(refs, send_sems, recv_sems):
        x, y, c = _position()
        return [pltpu.make_async_remote_copy(
            src_ref=refs[a].at[2 * ch + 1 - c], dst_ref=refs[k + a].at[ch], send_sem=send_sems.at[4 * a + ch],
            recv_sem=recv_sems.at[4 * a + ch], device_id=(x, y, 1 - c), device_id_type=MESH)
            for a in range(k) for ch in range(4)]
    return copies


def _chip_copies(k):
    def copies(refs, send_sems, recv_sems):
        x, y, c = _position()
        return [pltpu.make_async_remote_copy(
            src_ref=refs[a].at[2 * px + py], dst_ref=refs[k + a].at[rel], send_sem=send_sems.at[3 * a + rel],
            recv_sem=recv_sems.at[3 * a + rel], device_id=(px, py, c), device_id_type=MESH)
            for a in range(k) for rel, (px, py) in enumerate([(1 - x, y), (x, 1 - y), (1 - x, 1 - y)])]
    return copies


def _symmetric_exchange(name, srcs, n_land, copies_of):
    k = len(srcs)
    lands = [lax.empty((n_land,) + s.shape[1:], s.dtype) for s in srcs]
    copies = copies_of(k)
    send_sems, recv_sems, bufs, token = _exchange_start(name + "_start", list(srcs) + lands, n_land * k, copies)

    def finish(after):
        def wait_all(refs, ss, rs):
            for cp in copies(refs, ss, rs):
                cp.wait()

        done = _exchange_wait(name + "_wait", bufs, [send_sems, recv_sems], wait_all, after)
        return done[:k], done[k:]

    return token, finish


def _pair_add(fulls, recvs, wires):
    k = len(fulls)
    core = lax.axis_index("c").astype(jnp.int32).reshape(1)
    shapes = [f.shape[1:] for f in fulls]

    def body(core_ref, *refs):
        f_refs, r_refs, pw_refs, own_refs = (refs[j * k:(j + 1) * k] for j in range(4))
        x, y, _ = _position()
        mine = pl.program_id(0) == 2 * x + y
        for f_ref, r_ref, pw_ref, own_ref in zip(f_refs, r_refs, pw_refs, own_refs):
            tot = f_ref[0, 0] + r_ref[0]
            pw_ref[0] = tot.astype(pw_ref.dtype)

            @pl.when(mine)
            def _():
                own_ref[...] = tot

    out = pl.pallas_call(
        body, name="grad_pair_add",
        grid_spec=pltpu.PrefetchScalarGridSpec(
            num_scalar_prefetch=1, grid=(4,),
            in_specs=[pl.BlockSpec((1, 1, r, c), lambda i, core_ref: (i, core_ref[0], 0, 0)) for r, c in shapes]
            + [pl.BlockSpec((1, r, c), lambda i, core_ref: (i, 0, 0)) for r, c in shapes],
            out_specs=tuple([pl.BlockSpec((1, r, c), lambda i, core_ref: (i, 0, 0)) for r, c in shapes]
                            + [pl.BlockSpec((r, c), lambda i, core_ref: (0, 0)) for r, c in shapes])),
        out_shape=tuple([jax.ShapeDtypeStruct((4, r, c), wd) for (r, c), wd in zip(shapes, wires)]
                        + [jax.ShapeDtypeStruct((r, c), F32) for r, c in shapes]),
        compiler_params=_params("arbitrary"),
    )(core, *[f.reshape(4, 2, *f.shape[1:]) for f in fulls], *recvs)
    return out[:k], out[k:]


def _adamw_math(w, g, m, v):
    m = ADAM_B1 * m + (1.0 - ADAM_B1) * g
    v = ADAM_B2 * v + (1.0 - ADAM_B2) * (g * g)
    m_hat = m / (1.0 - ADAM_B1 ** ADAM_STEP)
    v_hat = v / (1.0 - ADAM_B2 ** ADAM_STEP)
    delta = -ADAM_LR * (m_hat / (jnp.sqrt(v_hat) + ADAM_EPS) + ADAM_WD * w)
    return delta, m, v


ADAMW_STEPS = 2


def _adamw(owns, recvs, ws, ms, vs):
    k = len(ws)
    tiles = [w.shape[0] // ADAMW_STEPS if w.shape[0] % (16 * ADAMW_STEPS) == 0 else w.shape[0] for w in ws]
    whole = [t == w.shape[0] for t, w in zip(tiles, ws)]

    def body(*refs):
        g_refs, r_refs, w_refs, m_refs, v_refs = (refs[j * k:(j + 1) * k] for j in range(5))
        outs = refs[5 * k:]
        for j in range(k):
            g = g_refs[j][...]
            for i in range(3):
                g = g + r_refs[j][i].astype(F32)
            outs[4 * j][...] = g
            outs[4 * j + 1][...], outs[4 * j + 2][...], outs[4 * j + 3][...] = _adamw_math(
                w_refs[j][...], g, m_refs[j][...], v_refs[j][...])

    def blk(j, lead=()):
        shape = lead + (tiles[j], ws[j].shape[1])
        if whole[j]:
            return pl.BlockSpec(shape, lambda i: (0,) * len(shape))
        return pl.BlockSpec(shape, lambda i: (0,) * len(lead) + (i, 0))

    plain = [blk(j) for j in range(k)]
    out = pl.pallas_call(
        body, name="adamw", grid=(ADAMW_STEPS,),
        out_shape=tuple(jax.ShapeDtypeStruct(w.shape, F32) for w in ws for _ in range(4)),
        in_specs=plain + [blk(j, (3,)) for j in range(k)] + plain * 3,
        out_specs=tuple(plain[j] for j in range(k) for _ in range(4)),
        compiler_params=_params("arbitrary"),
    )(*owns, *recvs, *ws, *ms, *vs)
    return [out[4 * j:4 * j + 4] for j in range(k)]


SMALL = ("attn_norm", "b_gate", "pool_scale", "q_norm", "k_norm", "sinks", "ffn_norm", "conv_b")
SMALL_SIZES = (1024, 2048, 1024, 64, 64, 16, 1024, 5632)
SMALL_OFFSETS = tuple(sum(-(-s // LANES) * LANES for s in SMALL_SIZES[:i]) for i in range(len(SMALL_SIZES) + 1))
SMALL_WIDTH = SMALL_OFFSETS[-1] + LANES


def _pack_small(d, loss=None):
    parts = [jnp.pad(d[n].reshape(1, -1), ((0, 0), (0, -s % LANES))) for n, s in zip(SMALL, SMALL_SIZES)]
    last = jnp.zeros((1, LANES), F32) if loss is None else jnp.pad(loss.reshape(1, 1), ((0, 0), (0, LANES - 1)))
    return jnp.concatenate(parts + [last], axis=1)


def _small_allreduce(gp):
    def body(g_ref, sum_ref, slots_ref, send_sems, recv_sems):
        x, y, c = _position()
        me = 4 * x + 2 * y + c
        slots_ref[me] = g_ref[...]
        cps = []
        for rel in range(1, N_DEV):
            fx, fy, fc = (rel >> 2) & 1, (rel >> 1) & 1, rel & 1
            to = (1 - x if fx else x, 1 - y if fy else y, 1 - c if fc else c)
            cps.append(pltpu.make_async_remote_copy(
                src_ref=g_ref, dst_ref=slots_ref.at[me], send_sem=send_sems.at[rel - 1],
                recv_sem=recv_sems.at[rel - 1], device_id=to, device_id_type=MESH))
        for cp in cps:
            cp.start()
        for cp in cps:
            cp.wait()
        g = slots_ref[0]
        for i in range(1, N_DEV):
            g = g + slots_ref[i]
        sum_ref[...] = g

    vm = pl.BlockSpec(memory_space=pltpu.VMEM)
    return pl.pallas_call(
        body, name="small_allreduce", out_shape=jax.ShapeDtypeStruct((1, SMALL_WIDTH), F32),
        in_specs=[vm], out_specs=vm,
        scratch_shapes=[pltpu.VMEM((N_DEV, 1, SMALL_WIDTH), F32), pltpu.SemaphoreType.DMA((N_DEV - 1,)),
                        pltpu.SemaphoreType.DMA((N_DEV - 1,))],
    )(gp)


def _small_adamw(gsum, ws, ms, vs):
    n = len(SMALL)

    def body(g_ref, *rest):
        w_refs, m_refs, v_refs, outs = rest[:n], rest[n:2 * n], rest[2 * n:3 * n], rest[3 * n:]
        for j, size in enumerate(SMALL_SIZES):
            g = g_ref[:, SMALL_OFFSETS[j]:SMALL_OFFSETS[j] + size]
            results = (g,) + _adamw_math(w_refs[j][...], g, m_refs[j][...], v_refs[j][...])
            for kind, val in enumerate(results):
                outs[kind * n + j][...] = val
        outs[4 * n][...] = g_ref[:, SMALL_OFFSETS[-1]:SMALL_WIDTH]

    vm = pl.BlockSpec(memory_space=pltpu.VMEM)
    shapes = tuple(jax.ShapeDtypeStruct((1, s), F32) for s in SMALL_SIZES) * 4
    return pl.pallas_call(
        body, name="small_adamw", out_shape=shapes + (jax.ShapeDtypeStruct((1, LANES), F32),),
        in_specs=[vm] * (1 + 3 * n), out_specs=(vm,) * (4 * n + 1),
    )(gsum, *ws, *ms, *vs)


WEIGHTS = ("attn_norm", "w_in", "b_gate", "w_pool", "pool_scale", "q_norm", "k_norm", "sinks", "w_out", "ffn_norm",
           "w_up", "conv_w", "conv_b", "w_down")


def kernel(x, positions, attn_norm, w_in, b_gate, w_pool, pool_scale, q_norm, k_norm, sinks, w_out, ffn_norm, w_up, conv_w, conv_b, w_down, loss_target, m_attn_norm, m_w_in, m_b_gate, m_w_pool, m_pool_scale, m_q_norm, m_k_norm, m_sinks, m_w_out, m_ffn_norm, m_w_up, m_conv_w, m_conv_b, m_w_down, v_attn_norm, v_w_in, v_b_gate, v_w_pool, v_pool_scale, v_q_norm, v_k_norm, v_sinks, v_w_out, v_ffn_norm, v_w_up, v_conv_w, v_conv_b, v_w_down):
    w = dict(attn_norm=attn_norm, w_in=w_in, b_gate=b_gate, w_pool=w_pool, pool_scale=pool_scale, q_norm=q_norm,
             k_norm=k_norm, sinks=sinks, w_out=w_out, ffn_norm=ffn_norm, w_up=w_up, conv_w=conv_w, conv_b=conv_b,
             w_down=w_down)
    m = dict(attn_norm=m_attn_norm, w_in=m_w_in, b_gate=m_b_gate, w_pool=m_w_pool, pool_scale=m_pool_scale,
             q_norm=m_q_norm, k_norm=m_k_norm, sinks=m_sinks, w_out=m_w_out, ffn_norm=m_ffn_norm, w_up=m_w_up,
             conv_w=m_conv_w, conv_b=m_conv_b, w_down=m_w_down)
    v = dict(attn_norm=v_attn_norm, w_in=v_w_in, b_gate=v_b_gate, w_pool=v_w_pool, pool_scale=v_pool_scale,
             q_norm=v_q_norm, k_norm=v_k_norm, sinks=v_sinks, w_out=v_w_out, ffn_norm=v_ffn_norm, w_up=v_w_up,
             conv_w=v_conv_w, conv_b=v_conv_b, w_down=v_w_down)
    seq = x.shape[1]
    tm = 256
    tw = min(seq, 512)
    tk, tk_ff = min(seq, 1024), min(seq, 2048)
    xs, target, pos_col = x[0], loss_target[0], positions.reshape(seq, 1)
    invf, bd, q_norm_t, k_norm_t, sink_rows = _attention_constants(q_norm, k_norm, sinks)
    out, done = {}, {}
    nat = {"w_in": (D_MODEL, 544), "w_pool": (128, POOL_GROUP), "w_out": (128, D_MODEL), "w_up": (D_MODEL, 704),
           "conv_w": (3, 704), "w_down": (352, D_MODEL)}

    def update(names, owns, recvs):
        flip = [name in ("w_in", "w_up") for name in names]
        shards = [[t[name].reshape(nat[name]).T if f else t[name].reshape(nat[name]) for name, f in zip(names, flip)]
                  for t in (w, m, v)]
        for name, f, res in zip(names, flip, _adamw(owns, recvs, *shards)):
            done[name] = res[1]
            out[name] = [(t.T if f else t).reshape(w[name].shape) for t in res]

    (g_win,) = _all_gather([w_in[0].T.astype(MM)])
    win_t = g_win.reshape(IN_WIDTH, D_MODEL)
    fwd = {}
    token, gather_rest = _all_gather_behind(
        [w_pool[0].astype(MM).reshape(128, POOL_GROUP), w_out[0].astype(MM), w_up[0].T.astype(MM), conv_w[0],
         w_down[0].astype(MM)], g_win, lambda: fwd["b"])

    tabs = _rope_tables(pos_col, invf)
    h, u, q, kv, g = _inproj_fwd(xs, attn_norm + token[0:1, 0:1], win_t, b_gate, tw)
    b, probs, sink_probs = _attn_fwd(q, kv, tabs, q_norm_t, k_norm_t, sink_rows, bd)
    fwd["b"] = b
    g_wpool, g_wout, g_wup, g_convw, g_wdown = gather_rest()
    wpool = g_wpool.reshape(N_DEV, 4, 32, POOL_GROUP).transpose(1, 0, 2, 3).reshape(4, POOL_GROUP, POOL_GROUP)
    wout = g_wout.reshape(D_MODEL, D_MODEL)
    wup_t = g_wup.reshape(2 * D_FF, D_MODEL)
    convw = g_convw.transpose(1, 0, 2).reshape(3, 2 * D_FF)
    wdown = g_wdown.reshape(D_FF, D_MODEL)
    a = _pool_fwd(u, wpool, pool_scale, tw)
    x1, mix = _mix_out_fwd(xs, g, a, b, wout, tw)
    h2, uff, ucv, dy, dyb, lossp = _ffn_fwd(x1, ffn_norm, wup_t, convw, conv_b, wdown, target, tm)

    du, act, d_conv_w, d_conv_b = _ffn_bwd_a(dyb, uff, ucv, convw, wdown, tm)
    d_wdown = _matmul_tn(act, dyb, FF_CHUNK, tk_ff, "dw_down")
    dx1, dx1b, d_ffn_norm = _ffn_bwd_b(du, wup_t, x1, ffn_norm, dy, tm)
    d_wup_t = _matmul_tn(du, h2, FF_CHUNK, tk_ff, "dw_up")
    late = ("w_down", "w_up", "conv_w")
    late_wire = (WIRE, WIRE, F32)
    late_full = [d_wdown.reshape(N_DEV, 352, D_MODEL), d_wup_t.reshape(N_DEV, 704, D_MODEL),
                 d_conv_w.reshape(3, N_DEV, 704).transpose(1, 0, 2)]
    token, late_pair = _symmetric_exchange("late_pair", late_full, 4, _pair_copies)
    da, db, dzg, d_b_gate, d_wout, d_win_t = _mix_bwd(dx1b, wout, g, a, b, mix, h, tm, after=token)
    late_pw, late_own = _pair_add(*late_pair(dzg), late_wire)
    token, late_chip = _symmetric_exchange("late_chip", list(late_pw), 3, _chip_copies)
    dzu, d_wpool, d_pool_scale, d_win_t = _pool_bwd(u, da, wpool, pool_scale, h, d_win_t, tw, after=token)
    dzq, dzkv, d_q_norm, d_k_norm, d_sinks = _attn_bwd(q, kv, db, tabs, q_norm_t, k_norm_t, probs, sink_probs, bd,
                                                       after=token)
    d_win_t = _matmul_tn(dzq, h, 1024, tk, "dw_in_q", into=d_win_t, row0=O_Q)
    d_win_t = _matmul_tn(dzkv, h, 256, tk, "dw_in_kv", into=d_win_t, row0=O_KV)
    early = ("w_in", "w_pool", "w_out")
    early_full = [d_win_t.reshape(N_DEV, 544, D_MODEL),
                  d_wpool.reshape(4, N_DEV, 32, POOL_GROUP).transpose(1, 0, 2, 3).reshape(N_DEV, 128, POOL_GROUP),
                  d_wout.reshape(N_DEV, 128, D_MODEL)]
    token, early_pair = _symmetric_exchange("early_pair", early_full, 4, _pair_copies)
    update(late, late_own, late_chip(token)[1])
    early_pw, early_own = _pair_add(*early_pair([done[n] for n in late]), (WIRE,) * 3)
    token, early_chip = _symmetric_exchange("early_chip", list(early_pw), 3, _chip_copies)
    grad_x, d_attn_norm = _inproj_bwd(dzu, dzq, dzkv, dzg, win_t, xs, attn_norm, dx1, tw, after=token)
    gr = dict(attn_norm=d_attn_norm, b_gate=d_b_gate, pool_scale=d_pool_scale, q_norm=d_q_norm, k_norm=d_k_norm,
              sinks=d_sinks[:, 0:N_Q_HEADS], ffn_norm=d_ffn_norm, conv_b=d_conv_b)
    small = _small_adamw(_small_allreduce(_pack_small(gr, lossp[0, 0])), *[[t[n] for n in SMALL] for t in (w, m, v)])
    loss = small[-1][0, 0]
    for j, name in enumerate(SMALL):
        out[name] = [small[kind * len(SMALL) + j] for kind in range(4)]
    update(early, early_own, early_chip(small[0])[1])

    return (loss, grad_x[None], *[out[n][0] for n in WEIGHTS], *[out[n][1] for n in WEIGHTS],
            *[out[n][2] for n in WEIGHTS], *[out[n][3] for n in WEIGHTS])
```

```python
import functools

import numpy as np
import jax
import jax.numpy as jnp
from jax import lax
from jax.experimental import pallas as pl
from jax.experimental.pallas import tpu as pltpu

F32 = jnp.float32
MM = jnp.bfloat16
WIRE = jnp.bfloat16
ACT = jnp.bfloat16

D_MODEL = 1024
D_FF = 2816
HEAD_DIM = 64
N_Q_HEADS = 16
N_KV_HEADS = 2
GQA_GROUP = 8
BLOCK = 128
ROPE_DIM = 16
ROPE_THETA = 500000.0
POOL_WINDOWS = (2, 4, 8, 16)
POOL_GROUP = 256
POOL_HALO = 32
EPS = 1e-6
NEG = -1e30
O_U, O_Q, O_KV, O_G, IN_WIDTH = 0, 1024, 2048, 2304, 4352
FF_CHUNK = 1408

ADAM_LR, ADAM_B1, ADAM_B2, ADAM_EPS, ADAM_WD, ADAM_STEP = 0.001, 0.9, 0.999, 1e-08, 0.01, 10

N_DEV = 8
LANES = 128
VMEM_LIMIT_BYTES = 56 * 1024 * 1024
MESH = pl.DeviceIdType.MESH


def _params(*sem):
    return pltpu.CompilerParams(dimension_semantics=sem, vmem_limit_bytes=VMEM_LIMIT_BYTES)


def _resident(shape):
    nd = len(shape)
    return pl.BlockSpec(shape, lambda *_: (0,) * nd, pipeline_mode=pl.Buffered(1))


def _rows(tm, width):
    return pl.BlockSpec((tm, width), lambda i: (i, 0))


def _mm(a, b):
    return jnp.dot(a.astype(MM), b.astype(MM), preferred_element_type=F32)


def _mm_nt(a, b):
    return lax.dot_general(a.astype(MM), b.astype(MM), (((1,), (1,)), ((), ())), preferred_element_type=F32)


def _mm_tn(a, b):
    return lax.dot_general(a.astype(MM), b.astype(MM), (((0,), (0,)), ((), ())), preferred_element_type=F32)


def _rmsnorm_fwd(x, g):
    r = lax.rsqrt(jnp.mean(x * x, axis=-1, keepdims=True) + EPS)
    return x * r * g, r


def _rmsnorm_bwd(x, r, g, dy):
    xn = x * r
    dxn = dy * g
    dx = r * (dxn - xn * jnp.mean(dxn * xn, axis=-1, keepdims=True))
    return dx, dy * xn


def _group_sum64(v, bd):
    hi = v.astype(MM)
    lo = (v - hi.astype(F32)).astype(MM)
    outs = []
    for t in range(v.shape[1] // LANES):
        sl = slice(LANES * t, LANES * (t + 1))
        outs.append(jnp.dot(hi[:, sl], bd, preferred_element_type=F32)
                    + jnp.dot(lo[:, sl], bd, preferred_element_type=F32))
    return outs[0] if len(outs) == 1 else jnp.concatenate(outs, axis=1)


def _head_norm_fwd(x, g, bd):
    r = lax.rsqrt(_group_sum64(x * x, bd) * (1.0 / HEAD_DIM) + EPS)
    return x * r * g, r


def _head_norm_bwd(x, r, g, dy, bd):
    xn = x * r
    dxn = dy * g
    dx = r * (dxn - xn * (_group_sum64(dxn * xn, bd) * (1.0 / HEAD_DIM)))
    return dx, dy * xn


def _rope(x, c, s1, s2):
    w = x.shape[1]
    return x * c + pltpu.roll(x, w - ROPE_DIM // 2, 1) * s1 + pltpu.roll(x, ROPE_DIM // 2, 1) * s2


def _rope_bwd(dy, c, s1, s2):
    w = dy.shape[1]
    return dy * c + pltpu.roll(dy * s1, ROPE_DIM // 2, 1) + pltpu.roll(dy * s2, w - ROPE_DIM // 2, 1)


def _tile_lanes(t, reps):
    return t if reps == 1 else jnp.concatenate([t] * reps, axis=1)


def _rope_tables(pos_col, invf):
    s = pos_col.shape[0]
    tm = min(s, 1024)

    def body(pos_ref, invf_ref, c_ref, s1_ref, s2_ref):
        ang = pos_ref[...].astype(F32) * invf_ref[...]
        lane = lax.broadcasted_iota(jnp.int32, ang.shape, 1) % HEAD_DIM
        sn = jnp.sin(ang)
        c_ref[...] = jnp.cos(ang)
        s1_ref[...] = jnp.where(lane < ROPE_DIM // 2, -sn, 0.0)
        s2_ref[...] = jnp.where((lane >= ROPE_DIM // 2) & (lane < ROPE_DIM), sn, 0.0)

    out = jax.ShapeDtypeStruct((s, LANES), F32)
    return pl.pallas_call(
        body, name="rope_tables", grid=(s // tm,), out_shape=(out, out, out),
        in_specs=[_rows(tm, 1), _resident((1, LANES))],
        out_specs=(_rows(tm, LANES),) * 3, compiler_params=_params("parallel"),
    )(pos_col, invf)


def _inproj_fwd(x, attn_norm, win_t, b_gate, tm):
    s = x.shape[0]

    def body(x_ref, gn_ref, w_ref, bg_ref, h_ref, u_ref, q_ref, kv_ref, g_ref):
        h, _ = _rmsnorm_fwd(x_ref[...], gn_ref[...])
        h = h.astype(MM)
        h_ref[...] = h
        u_ref[...] = _mm_nt(h, w_ref[O_U:O_Q, :])
        q_ref[...] = _mm_nt(h, w_ref[O_Q:O_KV, :])
        kv_ref[...] = _mm_nt(h, w_ref[O_KV:O_G, :])
        g_ref[...] = jax.nn.sigmoid(_mm_nt(h, w_ref[O_G:IN_WIDTH, :]) + bg_ref[...])

    sd = jax.ShapeDtypeStruct
    return pl.pallas_call(
        body, name="inproj_fwd", grid=(s // tm,),
        out_shape=(sd((s, D_MODEL), MM), sd((s, 1024), F32), sd((s, 1024), F32), sd((s, 256), F32),
                   sd((s, 2048), F32)),
        in_specs=[_rows(tm, D_MODEL), _resident((1, D_MODEL)), _resident((IN_WIDTH, D_MODEL)), _resident((1, 2048))],
        out_specs=(_rows(tm, D_MODEL), _rows(tm, 1024), _rows(tm, 1024), _rows(tm, 256), _rows(tm, 2048)),
        compiler_params=_params("parallel"),
    )(x, attn_norm, win_t, b_gate)


def _window_sums(src_ref, s1_ref, s2_ref, cols, w, tm, ahead):
    n = POOL_HALO + tm

    def level(src, width, lo):
        if ahead:
            return src(0, n - lo) + src(width, n - lo)
        return src(lo, n - lo) + src(lo - width, n - lo)

    def final(src, width):
        if ahead:
            return src(0, tm) + src(width, tm)
        return src(POOL_HALO, tm) + src(POOL_HALO - width, tm)

    levels = [lambda r0, rows: src_ref[pl.ds(r0, rows), cols]]
    for k, ref in ((1, s1_ref), (2, s2_ref), (3, s1_ref)):
        if w == 2 ** k:
            break
        lo = 8 * k
        ref[pl.ds(0 if ahead else lo, n - lo), :] = level(levels[-1], 2 ** (k - 1), lo)
        levels.append(functools.partial(lambda ref, r0, rows: ref[pl.ds(r0, rows), :], ref))
    return final(levels[-1], w // 2)


def _pooled(ext_ref, s1_ref, s2_ref, tm, row0):
    t = (row0 + lax.broadcasted_iota(jnp.int32, (tm, 1), 0)).astype(F32)
    out = []
    for gi, w in enumerate(POOL_WINDOWS):
        cols = slice(gi * POOL_GROUP, (gi + 1) * POOL_GROUP)
        acc = _window_sums(ext_ref, s1_ref, s2_ref, cols, w, tm, ahead=False)
        cnt = jnp.minimum(t + 1.0, float(w))
        out.append(acc / cnt - ext_ref[pl.ds(POOL_HALO, tm), cols])
    return out


def _pool_fwd(u, wpool, pool_scale, tm):
    s = u.shape[0]
    hb = tm // POOL_HALO

    def body(u_ref, halo_ref, wp_ref, ps_ref, a_ref, ext_ref, s1_ref, s2_ref):
        i = pl.program_id(0)
        ext_ref[pl.ds(0, POOL_HALO), :] = jnp.where(i > 0, halo_ref[...], 0.0)
        ext_ref[pl.ds(POOL_HALO, tm), :] = u_ref[...]
        pooled = _pooled(ext_ref, s1_ref, s2_ref, tm, i * tm)
        for gi in range(4):
            cols = slice(gi * POOL_GROUP, (gi + 1) * POOL_GROUP)
            a_ref[:, cols] = (_mm(pooled[gi], wp_ref[gi]) * ps_ref[:, cols]).astype(a_ref.dtype)

    return pl.pallas_call(
        body, name="pool_fwd", grid=(s // tm,), out_shape=jax.ShapeDtypeStruct((s, 1024), ACT),
        in_specs=[_rows(tm, 1024), pl.BlockSpec((POOL_HALO, 1024), lambda i: (jnp.maximum(i * hb - 1, 0), 0)),
                  _resident((4, POOL_GROUP, POOL_GROUP)), _resident((1, 1024))],
        out_specs=_rows(tm, 1024),
        scratch_shapes=[pltpu.VMEM((POOL_HALO + tm, 1024), F32), pltpu.VMEM((POOL_HALO + tm, POOL_GROUP), F32),
                        pltpu.VMEM((POOL_HALO + tm, POOL_GROUP), F32)],
        compiler_params=_params("parallel"),
    )(u, u, wpool, pool_scale)


PAIRS = GQA_GROUP // 2
PAIR_COLS = PAIRS * BLOCK


def _fold_masks(n):
    r = lax.broadcasted_iota(jnp.int32, (BLOCK, BLOCK), 0)
    i = lax.broadcasted_iota(jnp.int32, (BLOCK, BLOCK), 1)
    prev = r > i
    return prev, jnp.where(prev & (n == 0), NEG, 0.0)


def _fold(band, prev):
    return jnp.where(prev, band[0:BLOCK, :], band[BLOCK:2 * BLOCK, :])


def _unfold(folded, prev):
    top = jnp.where(prev, folded, 0.0)
    return jnp.concatenate([top, folded - top], axis=0).astype(MM)


def _probs_by_pair(sc, sink, masks):
    prev, bias = masks
    out = []
    for j in range(PAIRS):
        cols = slice(j * BLOCK, (j + 1) * BLOCK)
        out.append(_softmax_sink_t(_fold(sc[:, cols], prev) + bias, sink[:, cols]))
    return out


def _stack_pairs(x, hk):
    return jnp.concatenate([x[:, (PAIRS * hk + j) * LANES:(PAIRS * hk + j + 1) * LANES] for j in range(PAIRS)], axis=0)


def _parity_bands(t, hk):
    low = lax.broadcasted_iota(jnp.int32, t.shape, 1) < HEAD_DIM
    own = jnp.where(low if hk == 0 else ~low, t, 0.0)
    other = pltpu.roll(own, HEAD_DIM, 1)
    return (own, other) if hk == 0 else (other, own)


def _fold_parity(even, odd, hk):
    low = lax.broadcasted_iota(jnp.int32, even.shape, 1) < HEAD_DIM
    comb = jnp.where(low, even, odd)
    comb = comb + pltpu.roll(comb, HEAD_DIM, 1)
    return jnp.where(low if hk == 0 else ~low, comb, 0.0)


def _softmax_sink_t(s, sink):
    m = jnp.maximum(jnp.max(s, axis=0, keepdims=True), sink)
    p = jnp.exp(s - m)
    es = jnp.exp(sink - m)
    inv = 1.0 / (jnp.sum(p, axis=0, keepdims=True) + es)
    return p * inv, es * inv


def _attn_fwd(q, kv, tabs, q_norm_t, k_norm_t, sink_rows, bd):
    s = q.shape[0]
    nb = s // BLOCK
    scale = HEAD_DIM ** -0.5
    cur = lambda n: (n, 0)
    prv = lambda n: (jnp.maximum(2 * n - 1, 0), 0)

    def body(q_ref, kvc_ref, kvp_ref, c_ref, s1_ref, s2_ref, cp_ref, s1p_ref, s2p_ref, qn_ref, kn_ref, sink_ref,
             bd_ref, o_ref, pr_ref, ps_ref):
        n = pl.program_id(0)
        bdm = bd_ref[...]
        c, s1, s2 = c_ref[...], s1_ref[...], s2_ref[...]
        qh, _ = _head_norm_fwd(q_ref[...], qn_ref[...], bdm)
        qr = (_rope(qh, _tile_lanes(c, 8), _tile_lanes(s1, 8), _tile_lanes(s2, 8)) * scale).astype(MM)
        kc, _ = _head_norm_fwd(kvc_ref[:, 0:128], kn_ref[...], bdm)
        kp, _ = _head_norm_fwd(kvp_ref[:, 0:128], kn_ref[...], bdm)
        k3 = jnp.concatenate([_rope(kp, cp_ref[...], s1p_ref[...], s2p_ref[...]), _rope(kc, c, s1, s2)], axis=0)
        v3 = jnp.concatenate([kvp_ref[:, 128:256], kvc_ref[:, 128:256]], axis=0)
        for sub in range(2):
            rows = slice(sub * BLOCK, (sub + 1) * BLOCK)
            k2, v2 = k3[sub * BLOCK:(sub + 2) * BLOCK, :], v3[sub * BLOCK:(sub + 2) * BLOCK, :]
            mask = _fold_masks(2 * n + sub)
            for hk in range(N_KV_HEADS):
                qs = _stack_pairs(qr[rows, :], hk)
                ot = jnp.zeros((LANES, PAIR_COLS), F32)
                for par, (kb, vb) in enumerate(zip(_parity_bands(k2, hk), _parity_bands(v2, hk))):
                    probs = _probs_by_pair(_mm_nt(kb, qs), sink_ref[hk, par], mask)
                    pr_ref[sub, 2 * hk + par] = jnp.concatenate([pr for pr, _ in probs], axis=1).astype(MM)
                    ps_ref[sub, 2 * hk + par] = jnp.concatenate([ps for _, ps in probs], axis=1)
                    ot = ot + _mm(vb.T, jnp.concatenate([_unfold(pr, mask[0]) for pr, _ in probs], axis=1))
                for j in range(PAIRS):
                    col = (PAIRS * hk + j) * LANES
                    o_ref[rows, col:col + LANES] = ot[:, j * BLOCK:(j + 1) * BLOCK].T.astype(o_ref.dtype)

    two = 2 * BLOCK
    sd = jax.ShapeDtypeStruct
    return pl.pallas_call(
        body, name="attn_fwd", grid=(nb // 2,),
        out_shape=(sd((s, 1024), ACT), sd((nb, 4, BLOCK, PAIR_COLS), MM), sd((nb, 4, 1, PAIR_COLS), F32)),
        in_specs=[pl.BlockSpec((two, 1024), cur), pl.BlockSpec((two, 256), cur), pl.BlockSpec((BLOCK, 256), prv),
                  pl.BlockSpec((two, LANES), cur), pl.BlockSpec((two, LANES), cur), pl.BlockSpec((two, LANES), cur),
                  pl.BlockSpec((BLOCK, LANES), prv), pl.BlockSpec((BLOCK, LANES), prv), pl.BlockSpec((BLOCK, LANES), prv),
                  _resident((1, 1024)), _resident((1, 128)), _resident((N_KV_HEADS, 2, 1, PAIR_COLS)),
                  _resident((LANES, LANES))],
        out_specs=(pl.BlockSpec((two, 1024), cur), pl.BlockSpec((2, 4, BLOCK, PAIR_COLS), lambda n: (n, 0, 0, 0)),
                   pl.BlockSpec((2, 4, 1, PAIR_COLS), lambda n: (n, 0, 0, 0))),
        compiler_params=_params("parallel"),
    )(q, kv, kv, *tabs, *tabs, q_norm_t, k_norm_t, sink_rows, bd)


def _mix_out_fwd(x, g, a, b, wout, tm):
    s = x.shape[0]

    def body(x_ref, g_ref, a_ref, b_ref, w_ref, x1_ref, mix_ref):
        mix = (g_ref[:, 0:1024] * a_ref[...] + g_ref[:, 1024:2048] * b_ref[...]).astype(MM)
        mix_ref[...] = mix
        x1_ref[...] = x_ref[...] + _mm(mix, w_ref[...])

    return pl.pallas_call(
        body, name="mix_out_fwd", grid=(s // tm,),
        out_shape=(jax.ShapeDtypeStruct((s, D_MODEL), F32), jax.ShapeDtypeStruct((s, D_MODEL), MM)),
        in_specs=[_rows(tm, 1024), _rows(tm, 2048), _rows(tm, 1024), _rows(tm, 1024), _resident((1024, 1024))],
        out_specs=(_rows(tm, 1024), _rows(tm, 1024)), compiler_params=_params("parallel"),
    )(x, g, a, b, wout)


SHIFT_ROWS = 16


def _sublane_major_matrices(tm):
    r = np.arange(tm)
    pm = r[None, :] == ((tm // 8) * (r % 8) + r // 8)[:, None]
    return jnp.asarray(pm, MM), jnp.asarray(pm.T, MM)


def _to_sublane_major(pm, v):
    return jnp.dot(pm, v, preferred_element_type=F32).astype(MM)


def _to_time_order(pmt, v):
    hi = v.astype(MM)
    r1 = v - hi.astype(F32)
    mid = r1.astype(MM)
    lo = (r1 - mid.astype(F32)).astype(MM)
    dot = functools.partial(jnp.dot, preferred_element_type=F32)
    return dot(pmt, hi) + dot(pmt, mid) + dot(pmt, lo)


def _step_back(vreg_rows, before):
    sub = lax.broadcasted_iota(jnp.int32, vreg_rows.shape, 0)
    return jnp.where(sub == 0, before[7:8, :], pltpu.roll(vreg_rows, 1, 0))


def _step_ahead(vreg_rows, after):
    sub = lax.broadcasted_iota(jnp.int32, vreg_rows.shape, 0)
    return jnp.where(sub == 7, after[0:1, :], pltpu.roll(vreg_rows, 7, 0))


def _fill_back_rows(ext_ref, before, tm, cols):
    last = ext_ref[pl.ds(SHIFT_ROWS + tm - 8, 8), cols]
    pen = ext_ref[pl.ds(SHIFT_ROWS + tm - 16, 8), cols]
    ext_ref[pl.ds(8, 8), cols] = _step_back(last, before[8:16, :])
    ext_ref[pl.ds(0, 8), cols] = _step_back(pen, before[0:8, :])


def _causal_conv(uc, before, w, b, tm):
    back1 = _step_back(uc[tm - 8:tm, :], before[8:16, :])
    back2 = _step_back(uc[tm - 16:tm - 8, :], before[0:8, :])
    w0, w1, w2 = w[0:1, :], w[1:2, :], w[2:3, :]
    first = b + w0 * back2 + w1 * back1 + w2 * uc[0:8, :]
    second = b + w0 * back1 + w1 * uc[0:8, :] + w2 * uc[8:16, :]
    rest = b + w0 * uc[0:tm - 16, :] + w1 * uc[8:tm - 8, :] + w2 * uc[16:tm, :]
    return jnp.concatenate([first, second, rest], axis=0)


def _ffn_fwd(x1, ffn_norm, wup_t, conv_w, conv_b, wdown, target, tm):
    s = x1.shape[0]
    inv_d = 1.0 / D_MODEL
    pm, pmt = _sublane_major_matrices(tm)

    def body(x1_ref, gn_ref, wu_ref, cw_ref, cb_ref, wd_ref, tgt_ref, pm_ref, pmt_ref, h2_ref, u_ref, uc_ref, dy_ref,
             dyb_ref, loss_ref, carry_ref):
        i = pl.program_id(0)

        @pl.when(i == 0)
        def _():
            carry_ref[...] = jnp.zeros_like(carry_ref)
            loss_ref[...] = jnp.zeros_like(loss_ref)

        x1 = x1_ref[...]
        h2, _ = _rmsnorm_fwd(x1, gn_ref[...])
        h2 = _to_sublane_major(pm_ref[...], h2.astype(MM))
        h2_ref[...] = h2
        for c in range(4):
            cols = slice(c * FF_CHUNK, (c + 1) * FF_CHUNK)
            uc = _mm_nt(h2, wu_ref[cols, :])
            u_ref[:, cols] = uc
            uc_ref[:, cols] = _causal_conv(uc, carry_ref[:, cols], cw_ref[:, cols], cb_ref[:, cols], tm)
            carry_ref[:, cols] = uc[tm - SHIFT_ROWS:tm, :]
        down = jnp.zeros((tm, D_MODEL), F32)
        for c in range(2):
            gate = uc_ref[:, c * FF_CHUNK:(c + 1) * FF_CHUNK]
            val = uc_ref[:, D_FF + c * FF_CHUNK:D_FF + (c + 1) * FF_CHUNK]
            act = gate * jax.nn.sigmoid(gate) * val
            down = down + _mm(act, wd_ref[c * FF_CHUNK:(c + 1) * FF_CHUNK, :])
        err = x1 + _to_time_order(pmt_ref[...], down) - tgt_ref[...]
        loss_ref[...] += jnp.full(loss_ref.shape, 0.5 * inv_d * jnp.sum(err * err), F32)
        dy = err * inv_d
        dy_ref[...] = dy
        dyb_ref[...] = _to_sublane_major(pm_ref[...], dy.astype(MM))

    sd = jax.ShapeDtypeStruct
    return pl.pallas_call(
        body, name="ffn_fwd", grid=(s // tm,),
        out_shape=(sd((s, D_MODEL), MM), sd((s, 2 * D_FF), F32), sd((s, 2 * D_FF), F32), sd((s, D_MODEL), F32),
                   sd((s, D_MODEL), MM), sd((8, LANES), F32)),
        in_specs=[_rows(tm, 1024), _resident((1, 1024)), _resident((2 * D_FF, D_MODEL)), _resident((3, 2 * D_FF)),
                  _resident((1, 2 * D_FF)), _resident((D_FF, D_MODEL)), _rows(tm, 1024), _resident((tm, tm)),
                  _resident((tm, tm))],
        out_specs=(_rows(tm, 1024), _rows(tm, 2 * D_FF), _rows(tm, 2 * D_FF), _rows(tm, 1024), _rows(tm, 1024),
                   pl.BlockSpec((8, LANES), lambda i: (0, 0))),
        scratch_shapes=[pltpu.VMEM((SHIFT_ROWS, 2 * D_FF), F32)],
        compiler_params=_params("arbitrary"),
    )(x1, ffn_norm, wup_t, conv_w, conv_b, wdown, target, pm, pmt)


def _ffn_bwd_a(dyb, u, uc, conv_w, wdown, tm):
    s = dyb.shape[0]
    nt = s // tm
    hb = tm // SHIFT_ROWS
    rev = lambda i: (nt - 1 - i, 0)

    def body(dy_ref, u_ref, before_ref, uc_ref, cw_ref, wd_ref, du_ref, act_ref, dcw_ref, dcb_ref, ext_ref, extd_ref,
             ahead_ref):
        i = pl.program_id(0)
        first_tile = i == nt - 1

        @pl.when(i == 0)
        def _():
            ahead_ref[...] = jnp.zeros_like(ahead_ref)
            dcw_ref[...] = jnp.zeros_like(dcw_ref)
            dcb_ref[...] = jnp.zeros_like(dcb_ref)

        ext_ref[pl.ds(SHIFT_ROWS, tm), :] = u_ref[...]
        for c in range(4):
            cols = slice(c * FF_CHUNK, (c + 1) * FF_CHUNK)
            _fill_back_rows(ext_ref, jnp.where(first_tile, 0.0, before_ref[:, cols]), tm, cols)
        dy = dy_ref[...]
        for c in range(2):
            gate = uc_ref[:, c * FF_CHUNK:(c + 1) * FF_CHUNK]
            val = uc_ref[:, D_FF + c * FF_CHUNK:D_FF + (c + 1) * FF_CHUNK]
            sg = jax.nn.sigmoid(gate)
            sl = gate * sg
            act_ref[:, c * FF_CHUNK:(c + 1) * FF_CHUNK] = (sl * val).astype(MM)
            d_act = _mm_nt(dy, wd_ref[c * FF_CHUNK:(c + 1) * FF_CHUNK, :])
            extd_ref[pl.ds(0, tm), c * FF_CHUNK:(c + 1) * FF_CHUNK] = d_act * val * (sg * (1.0 + gate * (1.0 - sg)))
            extd_ref[pl.ds(0, tm), D_FF + c * FF_CHUNK:D_FF + (c + 1) * FF_CHUNK] = d_act * sl
        for c in range(4):
            cols = slice(c * FF_CHUNK, (c + 1) * FF_CHUNK)
            ahead = ahead_ref[:, cols]
            first2 = extd_ref[pl.ds(0, SHIFT_ROWS), cols]
            extd_ref[pl.ds(tm, 8), cols] = _step_ahead(first2[0:8, :], ahead[0:8, :])
            extd_ref[pl.ds(tm + 8, 8), cols] = _step_ahead(first2[8:16, :], ahead[8:16, :])
            ahead_ref[:, cols] = first2
            d0 = extd_ref[pl.ds(0, tm), cols]
            dcb_ref[:, cols] += jnp.sum(d0, axis=0, keepdims=True)
            for j in range(3):
                dcw_ref[j:j + 1, cols] += jnp.sum(d0 * ext_ref[pl.ds(8 * j, tm), cols], axis=0, keepdims=True)
            du = cw_ref[2:3, cols] * d0 + cw_ref[1:2, cols] * extd_ref[pl.ds(8, tm), cols]
            du = du + cw_ref[0:1, cols] * extd_ref[pl.ds(SHIFT_ROWS, tm), cols]
            du_ref[:, cols] = du.astype(MM)

    sd = jax.ShapeDtypeStruct
    return pl.pallas_call(
        body, name="ffn_bwd_a", grid=(nt,),
        out_shape=(sd((s, 2 * D_FF), MM), sd((s, D_FF), MM), sd((3, 2 * D_FF), F32), sd((1, 2 * D_FF), F32)),
        in_specs=[pl.BlockSpec((tm, D_MODEL), rev), pl.BlockSpec((tm, 2 * D_FF), rev),
                  pl.BlockSpec((SHIFT_ROWS, 2 * D_FF), lambda i: (jnp.maximum((nt - 1 - i) * hb - 1, 0), 0)),
                  pl.BlockSpec((tm, 2 * D_FF), rev), _resident((3, 2 * D_FF)), _resident((D_FF, D_MODEL))],
        out_specs=(pl.BlockSpec((tm, 2 * D_FF), rev), pl.BlockSpec((tm, D_FF), rev),
                   pl.BlockSpec((3, 2 * D_FF), lambda i: (0, 0)), pl.BlockSpec((1, 2 * D_FF), lambda i: (0, 0))),
        scratch_shapes=[pltpu.VMEM((SHIFT_ROWS + tm, 2 * D_FF), F32), pltpu.VMEM((tm + SHIFT_ROWS, 2 * D_FF), F32),
                        pltpu.VMEM((SHIFT_ROWS, 2 * D_FF), F32)],
        compiler_params=_params("arbitrary"),
    )(dyb, u, u, uc, conv_w, wdown)


def _after(after):
    tie = [] if after is None else list(after) if isinstance(after, (list, tuple)) else [after]
    return tie, [pl.BlockSpec(memory_space=pl.ANY)] * len(tie)


def _matmul_tn(a, b, tmo, tk, name, into=None, row0=0):
    s, m = a.shape
    n = b.shape[1]
    nk = s // tk
    rows = m if into is None else into.shape[0]
    assert row0 % LANES == 0 and tmo % LANES == 0
    grown, grown_spec = ([], []) if into is None else ([into], [ANY])

    def body(a_ref, b_ref, *rest):
        o_ref = rest[-1]
        k = pl.program_id(1)

        @pl.when(k == 0)
        def _():
            o_ref[...] = jnp.zeros_like(o_ref)

        o_ref[...] += _mm_tn(a_ref[...], b_ref[pl.ds(pl.multiple_of(k * tk, tk), tk), :])

    return pl.pallas_call(
        body, name=name, grid=(m // tmo, nk), out_shape=jax.ShapeDtypeStruct((rows, n), F32),
        in_specs=[pl.BlockSpec((tk, tmo), lambda i, k: (k, i)), _resident((s, n))] + grown_spec,
        out_specs=pl.BlockSpec((pl.Element(tmo), pl.Element(n)), lambda i, k: (pl.multiple_of(row0 + i * tmo, LANES), 0)),
        input_output_aliases={2: 0} if grown else {},
        compiler_params=_params("parallel", "arbitrary"),
    )(a, b, *grown)


def _ffn_bwd_b(du, wup_t, x1, ffn_norm, dy, tm):
    s = du.shape[0]

    def body(du_ref, wu_ref, x1_ref, gn_ref, dy_ref, pmt_ref, dx1_ref, dx1b_ref, dg_ref):
        @pl.when(pl.program_id(0) == 0)
        def _():
            dg_ref[...] = jnp.zeros_like(dg_ref)

        dh2 = _to_time_order(pmt_ref[...], _mm(du_ref[...], wu_ref[...]))
        x1 = x1_ref[...]
        _, r = _rmsnorm_fwd(x1, gn_ref[...])
        dx, dgr = _rmsnorm_bwd(x1, r, gn_ref[...], dh2)
        dg_ref[...] += jnp.sum(dgr, axis=0, keepdims=True)
        dx1 = dy_ref[...] + dx
        dx1_ref[...] = dx1
        dx1b_ref[...] = dx1.astype(MM)

    return pl.pallas_call(
        body, name="ffn_bwd_b", grid=(s // tm,),
        out_shape=(jax.ShapeDtypeStruct((s, D_MODEL), F32), jax.ShapeDtypeStruct((s, D_MODEL), MM),
                   jax.ShapeDtypeStruct((1, D_MODEL), F32)),
        in_specs=[_rows(tm, 2 * D_FF), _resident((2 * D_FF, D_MODEL)), _rows(tm, 1024), _resident((1, 1024)),
                  _rows(tm, 1024), _resident((tm, tm))],
        out_specs=(_rows(tm, 1024), _rows(tm, 1024), pl.BlockSpec((1, D_MODEL), lambda i: (0, 0))),
        compiler_params=_params("arbitrary"),
    )(du, wup_t, x1, ffn_norm, dy, _sublane_major_matrices(tm)[1])


def _win_rows(row0, rows):
    return pl.BlockSpec((pl.Element(rows), pl.Element(D_MODEL)), lambda i: (row0, 0), pipeline_mode=pl.Buffered(1))


def _mix_bwd(dx1b, wout, g, a, b, mix, h, tm, after=None):
    s = dx1b.shape[0]
    tie, tie_spec = _after(after)

    def body(dx_ref, w_ref, g_ref, a_ref, b_ref, mix_ref, h_ref, *rest):
        da_ref, db_ref, dzg_ref, dbg_ref, dwo_ref, dwin_ref = rest[-6:]

        @pl.when(pl.program_id(0) == 0)
        def _():
            dbg_ref[...] = jnp.zeros_like(dbg_ref)
            dwo_ref[...] = jnp.zeros_like(dwo_ref)
            dwin_ref[...] = jnp.zeros_like(dwin_ref)

        dx = dx_ref[...]
        dwo_ref[...] += _mm_tn(mix_ref[...], dx)
        dmix = _mm_nt(dx, w_ref[...])
        for half, src, dst in ((0, a_ref, da_ref), (1, b_ref, db_ref)):
            cols = slice(half * 1024, (half + 1) * 1024)
            gt = g_ref[:, cols]
            dst[...] = (dmix * gt).astype(dst.dtype)
            dz = dmix * src[...] * gt * (1.0 - gt)
            dzb = dz.astype(MM)
            dzg_ref[:, cols] = dzb
            dwin_ref[cols, :] += _mm_tn(dzb, h_ref[...])
            dbg_ref[:, cols] += jnp.sum(dz, axis=0, keepdims=True)

    sd = jax.ShapeDtypeStruct
    return pl.pallas_call(
        body, name="mix_bwd", grid=(s // tm,),
        out_shape=(sd((s, 1024), F32), sd((s, 1024), MM), sd((s, 2048), MM), sd((1, 2048), F32),
                   sd((D_MODEL, D_MODEL), F32), sd((IN_WIDTH, D_MODEL), F32)),
        in_specs=[_rows(tm, 1024), _resident((1024, 1024)), _rows(tm, 2048), _rows(tm, 1024), _rows(tm, 1024),
                  _rows(tm, 1024), _rows(tm, 1024)] + tie_spec,
        out_specs=(_rows(tm, 1024), _rows(tm, 1024), _rows(tm, 2048), pl.BlockSpec((1, 2048), lambda i: (0, 0)),
                   _resident((D_MODEL, D_MODEL)), _win_rows(O_G, IN_WIDTH - O_G)),
        compiler_params=_params("arbitrary"),
    )(dx1b, wout, g, a, b, mix, h, *tie)


def _pool_bwd(u, da, wpool, pool_scale, h, d_win_t, tm, after=None):
    s = u.shape[0]
    nt = s // tm
    hb = tm // POOL_HALO

    tie, tie_spec = _after(after)

    def body(u_ref, uh_ref, da_ref, dah_ref, wp_ref, ps_ref, h_ref, *rest):
        dzu_ref, dwp_ref, dps_ref, dwin_ref, ext_ref, exte_ref, s1_ref, s2_ref = rest[-8:]
        i = pl.program_id(0)

        @pl.when(i == 0)
        def _():
            dwp_ref[...] = jnp.zeros_like(dwp_ref)
            dps_ref[...] = jnp.zeros_like(dps_ref)
            dwin_ref[...] = jnp.zeros_like(dwin_ref)

        ext_ref[pl.ds(0, POOL_HALO), :] = jnp.where(i > 0, uh_ref[...], 0.0)
        ext_ref[pl.ds(POOL_HALO, tm), :] = u_ref[...]
        pooled = _pooled(ext_ref, s1_ref, s2_ref, tm, i * tm)
        da = da_ref[...]
        dah = jnp.where(i < nt - 1, dah_ref[...], 0.0)
        t = (i * tm + lax.broadcasted_iota(jnp.int32, (tm + POOL_HALO, 1), 0)).astype(F32)
        for gi, w in enumerate(POOL_WINDOWS):
            cols = slice(gi * POOL_GROUP, (gi + 1) * POOL_GROUP)
            pg = pooled[gi].astype(MM)
            wg = wp_ref[gi]
            mixed = _mm(pg, wg)
            dps_ref[:, cols] += jnp.sum(da[:, cols] * mixed, axis=0, keepdims=True)
            dmx = (da[:, cols] * ps_ref[:, cols]).astype(MM)
            dwp_ref[gi] += _mm_tn(pg, dmx)
            dpl = _mm_nt(dmx, wg)
            dplh = _mm_nt(dah[:, cols] * ps_ref[:, cols], wg)
            cnt = jnp.minimum(t + 1.0, float(w))
            exte_ref[pl.ds(0, tm), cols] = dpl / cnt[0:tm]
            exte_ref[pl.ds(tm, POOL_HALO), cols] = dplh / cnt[tm:tm + POOL_HALO]
            acc = _window_sums(exte_ref, s1_ref, s2_ref, cols, w, tm, ahead=True)
            dzu = (acc - dpl).astype(MM)
            dzu_ref[:, cols] = dzu
            dwin_ref[cols, :] += _mm_tn(dzu, h_ref[...])

    sd = jax.ShapeDtypeStruct
    last_halo = s // POOL_HALO - 1
    return pl.pallas_call(
        body, name="pool_bwd", grid=(nt,),
        out_shape=(sd((s, 1024), MM), sd((4, POOL_GROUP, POOL_GROUP), F32), sd((1, 1024), F32),
                   sd((IN_WIDTH, D_MODEL), F32)),
        in_specs=[_rows(tm, 1024), pl.BlockSpec((POOL_HALO, 1024), lambda i: (jnp.maximum(i * hb - 1, 0), 0)),
                  _rows(tm, 1024),
                  pl.BlockSpec((POOL_HALO, 1024), lambda i: (jnp.minimum((i + 1) * hb, last_halo), 0)),
                  _resident((4, POOL_GROUP, POOL_GROUP)), _resident((1, 1024)), _rows(tm, 1024)] + tie_spec + [ANY],
        out_specs=(_rows(tm, 1024), pl.BlockSpec((4, POOL_GROUP, POOL_GROUP), lambda i: (0, 0, 0)),
                   pl.BlockSpec((1, 1024), lambda i: (0, 0)), _win_rows(O_U, O_Q - O_U)),
        input_output_aliases={7 + len(tie): 3},
        scratch_shapes=[pltpu.VMEM((POOL_HALO + tm, 1024), F32), pltpu.VMEM((tm + POOL_HALO, 1024), F32),
                        pltpu.VMEM((POOL_HALO + tm, POOL_GROUP), F32), pltpu.VMEM((POOL_HALO + tm, POOL_GROUP), F32)],
        compiler_params=_params("arbitrary"),
    )(u, u, da, da, wpool, pool_scale, h, *tie, d_win_t)


def _attn_bwd(q, kv, db, tabs, q_norm_t, k_norm_t, probs, sink_probs, bd, after=None):
    s = q.shape[0]
    nb = s // BLOCK
    scale = HEAD_DIM ** -0.5
    cur = lambda n: (jnp.minimum(n, nb - 1), 0)
    prv = lambda n: (jnp.maximum(n - 1, 0), 0)
    tie, tie_spec = _after(after)

    def body(q_ref, kvc_ref, kvp_ref, db_ref, c_ref, s1_ref, s2_ref, cp_ref, s1p_ref, s2p_ref, qn_ref, kn_ref,
             pr_ref, ps_ref, bd_ref, *rest):
        (dzq_ref, dzkv_ref, dqn_ref, dkn_ref, dsk_ref,
         carry_ref, tot_ref, dqr_ref, qacc_ref, kacc_ref, sacc_ref) = rest[-11:]
        n = pl.program_id(0)
        bdm = bd_ref[...]
        kn = kn_ref[...]

        @pl.when(n == 0)
        def _():
            carry_ref[...] = jnp.zeros_like(carry_ref)
            qacc_ref[...] = jnp.zeros_like(qacc_ref)
            kacc_ref[...] = jnp.zeros_like(kacc_ref)
            sacc_ref[...] = jnp.zeros_like(sacc_ref)

        kp_raw = kvp_ref[:, 0:128]
        kph, rp = _head_norm_fwd(kp_raw, kn, bdm)
        cp, s1p, s2p = cp_ref[...], s1p_ref[...], s2p_ref[...]

        @pl.when(n < nb)
        def _():
            c, s1, s2 = c_ref[...], s1_ref[...], s2_ref[...]
            c8, s18, s28 = _tile_lanes(c, 8), _tile_lanes(s1, 8), _tile_lanes(s2, 8)
            q_raw = q_ref[...]
            qh, rq = _head_norm_fwd(q_raw, qn_ref[...], bdm)
            qr = (_rope(qh, c8, s18, s28) * scale).astype(MM)
            kc, _ = _head_norm_fwd(kvc_ref[:, 0:128], kn, bdm)
            k2 = jnp.concatenate([_rope(kph, cp, s1p, s2p), _rope(kc, c, s1, s2)], axis=0)
            v2 = jnp.concatenate([kvp_ref[:, 128:256], kvc_ref[:, 128:256]], axis=0)
            dob = db_ref[...].astype(MM)
            mask = _fold_masks(n)
            lane = lax.broadcasted_iota(jnp.int32, (1, LANES), 1)
            dsk = jnp.zeros((1, LANES), F32)
            dk2 = jnp.zeros((2 * BLOCK, LANES), F32)
            dv2 = jnp.zeros((2 * BLOCK, LANES), F32)
            for hk in range(N_KV_HEADS):
                qs = _stack_pairs(qr, hk)
                do = _stack_pairs(dob, hk)
                dqt = jnp.zeros((LANES, PAIR_COLS), F32)
                dkb, dvb = [], []
                for par, (kb, vb) in enumerate(zip(_parity_bands(k2, hk), _parity_bands(v2, hk))):
                    dp = _mm_nt(vb, do)
                    prs, dss = [], []
                    for j in range(PAIRS):
                        cols = slice(j * BLOCK, (j + 1) * BLOCK)
                        pr = pr_ref[0, 2 * hk + par, :, cols].astype(F32)
                        psink = ps_ref[0, 2 * hk + par, :, cols]
                        dpj = _fold(dp[:, cols], mask[0])
                        coldot = jnp.sum(pr * dpj, axis=0, keepdims=True)
                        dss.append(_unfold(pr * (dpj - coldot), mask[0]))
                        prs.append(_unfold(pr, mask[0]))
                        h = hk * GQA_GROUP + 2 * j + par
                        dsk = dsk + jnp.where(lane == h, jnp.sum(-psink * coldot), 0.0)
                    ds, pr = jnp.concatenate(dss, axis=1), jnp.concatenate(prs, axis=1)
                    dqt = dqt + _mm(kb.T, ds)
                    dkb.append(_mm(ds, qs))
                    dvb.append(_mm(pr, do))
                for j in range(PAIRS):
                    col = (PAIRS * hk + j) * LANES
                    dqr_ref[:, col:col + LANES] = dqt[:, j * BLOCK:(j + 1) * BLOCK].T
                dk2 = dk2 + _fold_parity(dkb[0], dkb[1], hk)
                dv2 = dv2 + _fold_parity(dvb[0], dvb[1], hk)
            tot_ref[:, 0:128] = carry_ref[:, 0:128] + dk2[0:BLOCK, :]
            tot_ref[:, 128:256] = carry_ref[:, 128:256] + dv2[0:BLOCK, :]
            carry_ref[:, 0:128] = dk2[BLOCK:2 * BLOCK, :]
            carry_ref[:, 128:256] = dv2[BLOCK:2 * BLOCK, :]
            sacc_ref[...] += dsk
            dqh = _rope_bwd(dqr_ref[...] * scale, c8, s18, s28)
            dq, dgq = _head_norm_bwd(q_raw, rq, qn_ref[...], dqh, bdm)
            dzq_ref[...] = dq.astype(MM)
            qacc_ref[...] += jnp.sum(dgq, axis=0, keepdims=True)

        @pl.when(n == nb)
        def _():
            tot_ref[...] = carry_ref[...]

        dkh = _rope_bwd(tot_ref[:, 0:128], cp, s1p, s2p)
        dkr, dgk = _head_norm_bwd(kp_raw, rp, kn, dkh, bdm)
        dzkv_ref[:, 0:128] = dkr.astype(MM)
        dzkv_ref[:, 128:256] = tot_ref[:, 128:256].astype(MM)
        kacc_ref[...] += jnp.where(n > 0, jnp.sum(dgk, axis=0, keepdims=True), 0.0)

        @pl.when(n == nb)
        def _():
            fold = qacc_ref[:, 0:HEAD_DIM]
            for h in range(1, N_Q_HEADS):
                fold = fold + qacc_ref[:, h * HEAD_DIM:(h + 1) * HEAD_DIM]
            dqn_ref[...] = fold
            dkn_ref[...] = kacc_ref[:, 0:HEAD_DIM] + kacc_ref[:, HEAD_DIM:2 * HEAD_DIM]
            dsk_ref[...] = sacc_ref[...]

    tab = lambda im: pl.BlockSpec((BLOCK, LANES), im)
    sd = jax.ShapeDtypeStruct
    const = lambda n: (0, 0)
    return pl.pallas_call(
        body, name="attn_bwd", grid=(nb + 1,),
        out_shape=(sd((s, 1024), MM), sd((s, 256), MM), sd((1, HEAD_DIM), F32), sd((1, HEAD_DIM), F32),
                   sd((1, LANES), F32)),
        in_specs=[pl.BlockSpec((BLOCK, 1024), cur), pl.BlockSpec((BLOCK, 256), cur), pl.BlockSpec((BLOCK, 256), prv),
                  pl.BlockSpec((BLOCK, 1024), cur), tab(cur), tab(cur), tab(cur), tab(prv), tab(prv), tab(prv),
                  _resident((1, 1024)), _resident((1, 128)),
                  pl.BlockSpec((1, 4, BLOCK, PAIR_COLS), lambda n: (jnp.minimum(n, nb - 1), 0, 0, 0)),
                  pl.BlockSpec((1, 4, 1, PAIR_COLS), lambda n: (jnp.minimum(n, nb - 1), 0, 0, 0)),
                  _resident((LANES, LANES))] + tie_spec,
        out_specs=(pl.BlockSpec((BLOCK, 1024), cur), pl.BlockSpec((BLOCK, 256), prv),
                   pl.BlockSpec((1, HEAD_DIM), const), pl.BlockSpec((1, HEAD_DIM), const),
                   pl.BlockSpec((1, LANES), const)),
        scratch_shapes=[pltpu.VMEM((BLOCK, 256), F32), pltpu.VMEM((BLOCK, 256), F32), pltpu.VMEM((BLOCK, 1024), F32),
                        pltpu.VMEM((1, 1024), F32), pltpu.VMEM((1, 128), F32), pltpu.VMEM((1, LANES), F32)],
        compiler_params=_params("arbitrary"),
    )(q, kv, kv, db, *tabs, *tabs, q_norm_t, k_norm_t, probs, sink_probs, bd, *tie)


def _inproj_bwd(dzu, dzq, dzkv, dzg, win_t, x, attn_norm, dx1, tm, after=None):
    s = x.shape[0]
    tie, tie_spec = _after(after)

    def body(du_ref, dq_ref, dkv_ref, dg_ref, w_ref, x_ref, gn_ref, dx1_ref, *rest):
        gx_ref, dgn_ref = rest[-2:]

        @pl.when(pl.program_id(0) == 0)
        def _():
            dgn_ref[...] = jnp.zeros_like(dgn_ref)

        dh = _mm(du_ref[...], w_ref[O_U:O_Q, :]) + _mm(dq_ref[...], w_ref[O_Q:O_KV, :])
        dh = dh + _mm(dkv_ref[...], w_ref[O_KV:O_G, :]) + _mm(dg_ref[...], w_ref[O_G:IN_WIDTH, :])
        x = x_ref[...]
        _, r = _rmsnorm_fwd(x, gn_ref[...])
        dx, dgr = _rmsnorm_bwd(x, r, gn_ref[...], dh)
        dgn_ref[...] += jnp.sum(dgr, axis=0, keepdims=True)
        gx_ref[...] = dx1_ref[...] + dx

    return pl.pallas_call(
        body, name="inproj_bwd", grid=(s // tm,),
        out_shape=(jax.ShapeDtypeStruct((s, D_MODEL), F32), jax.ShapeDtypeStruct((1, D_MODEL), F32)),
        in_specs=[_rows(tm, 1024), _rows(tm, 1024), _rows(tm, 256), _rows(tm, 2048),
                  _resident((IN_WIDTH, D_MODEL)), _rows(tm, 1024), _resident((1, 1024)), _rows(tm, 1024)] + tie_spec,
        out_specs=(_rows(tm, 1024), pl.BlockSpec((1, D_MODEL), lambda i: (0, 0))),
        compiler_params=_params("arbitrary"),
    )(dzu, dzq, dzkv, dzg, win_t, x, attn_norm, dx1, *tie)


def _attention_constants(q_norm, k_norm, sinks):
    inv_freq = np.float32(ROPE_THETA) ** (-np.arange(0, ROPE_DIM, 2, dtype=np.float32) / np.float32(ROPE_DIM))
    lane = np.arange(LANES) % HEAD_DIM
    invf = jnp.asarray(np.where(lane < ROPE_DIM, inv_freq[lane % (ROPE_DIM // 2)], 0.0).reshape(1, LANES), F32)
    bd = jnp.asarray(np.arange(LANES)[:, None] // HEAD_DIM == np.arange(LANES)[None, :] // HEAD_DIM, MM)
    q_norm_t = jnp.tile(q_norm, (1, N_Q_HEADS))
    k_norm_t = jnp.tile(k_norm, (1, N_KV_HEADS))
    sink_rows = jnp.repeat(sinks.reshape(N_KV_HEADS, PAIRS, 2).transpose(0, 2, 1), BLOCK, axis=2)
    sink_rows = sink_rows.reshape(N_KV_HEADS, 2, 1, PAIR_COLS)
    return invf, bd, q_norm_t, k_norm_t, sink_rows


ANY = pl.BlockSpec(memory_space=pl.ANY)


def _position():
    return lax.axis_index("x"), lax.axis_index("y"), lax.axis_index("c")


def _all_gather(shards):
    k = len(shards)

    def body(*refs):
        ins, outs = refs[:k], refs[k:2 * k]
        send_sems, recv_sems, local_sems = refs[2 * k:]
        x, y, c = _position()
        me, sibling = (x, y, c), (x, y, 1 - c)
        chips = [(1 - x, y), (x, 1 - y), (1 - x, 1 - y)]

        def copy(a, kk, block, to, src=None):
            dst = outs[a].at[4 * block[0] + 2 * block[1] + block[2]]
            return pltpu.make_async_remote_copy(
                src_ref=dst if src is None else src, dst_ref=dst, send_sem=send_sems.at[a * 7 + kk],
                recv_sem=recv_sems.at[a * 7 + kk], device_id=to, device_id_type=MESH)

        mine = [pltpu.make_async_copy(ins[a], outs[a].at[4 * x + 2 * y + c], local_sems.at[a]) for a in range(k)]
        for cp in mine:
            cp.start()
        first = []
        for a in range(k):
            first.append(copy(a, 0, me, sibling, src=ins[a]))
            first += [copy(a, 1 + j, me, (*chip, c), src=ins[a]) for j, chip in enumerate(chips)]
        for cp in first:
            cp.start()
        passed = []
        for j, chip in enumerate(chips):
            for a in range(k):
                copy(a, 1 + j, (*chip, c), me).wait_recv()
                cp = copy(a, 4 + j, (*chip, c), sibling)
                cp.start()
                passed.append(cp)
        for a in range(k):
            copy(a, 0, sibling, me).wait_recv()
            for j, chip in enumerate(chips):
                copy(a, 4 + j, (*chip, 1 - c), me).wait_recv()
        for cp in first + passed:
            cp.wait_send()
        for cp in mine:
            cp.wait()

    return pl.pallas_call(
        body, name="all_gather_weights",
        out_shape=tuple(jax.ShapeDtypeStruct((N_DEV,) + s.shape, s.dtype) for s in shards),
        in_specs=[ANY] * k, out_specs=(ANY,) * k,
        scratch_shapes=[pltpu.SemaphoreType.DMA((7 * k,)), pltpu.SemaphoreType.DMA((7 * k,)),
                        pltpu.SemaphoreType.DMA((k,))],
    )(*shards)


HBM = pl.BlockSpec(memory_space=pltpu.HBM)
SEM = pl.BlockSpec(memory_space=pltpu.SEMAPHORE)
EFFECT = pltpu.SideEffectType.DATAFLOW_SIDE_EFFECTING


def _exchange_start(name, bufs, n_sems, copies, after=None):
    k = len(bufs)
    tie, tie_spec = _after(after)
    n_in = k + len(tie)

    def body(*refs):
        for cp in copies(refs[:k], refs[n_in], refs[n_in + 1]):
            cp.start()
        refs[-1][...] = jnp.zeros_like(refs[-1])

    dma = pltpu.SemaphoreType.DMA((n_sems,))
    out = pl.pallas_call(
        body, name=name,
        out_shape=(dma, dma, *[pltpu.HBM(b.shape, b.dtype) for b in bufs], jax.ShapeDtypeStruct((8, LANES), F32)),
        in_specs=[HBM] * k + tie_spec, out_specs=(SEM, SEM, *[HBM] * k, pl.BlockSpec(memory_space=pltpu.VMEM)),
        input_output_aliases={i: 2 + i for i in range(k)},
        compiler_params=pltpu.CompilerParams(has_side_effects=EFFECT),
    )(*[pltpu.with_memory_space_constraint(b, pltpu.HBM) for b in bufs], *tie)
    return out[0], out[1], list(out[2:2 + k]), out[-1]


def _exchange_mid(name, bufs, sems_in, n_sems, waits, copies, after):
    k, ns = len(bufs), len(sems_in)

    def body(*refs):
        ins = refs[:k]
        waits(ins, *refs[k:k + ns])
        for cp in copies(ins, refs[k + ns + 1], refs[k + ns + 2]):
            cp.start()

    dma = pltpu.SemaphoreType.DMA((n_sems,))
    out = pl.pallas_call(
        body, name=name, out_shape=(dma, dma, *[pltpu.HBM(b.shape, b.dtype) for b in bufs]),
        in_specs=[HBM] * k + [SEM] * ns + [ANY], out_specs=(SEM, SEM, *[HBM] * k),
        input_output_aliases={i: 2 + i for i in range(k)},
        compiler_params=pltpu.CompilerParams(has_side_effects=EFFECT),
    )(*bufs, *sems_in, after)
    return out[0], out[1], list(out[2:])


def _exchange_wait(name, bufs, sems, waits, after=None):
    k, ns = len(bufs), len(sems)
    tie, tie_spec = _after(after)

    def body(*refs):
        waits(refs[:k], *refs[k:k + ns])

    out = pl.pallas_call(
        body, name=name, out_shape=tuple(pltpu.HBM(b.shape, b.dtype) for b in bufs),
        in_specs=[HBM] * k + [SEM] * ns + tie_spec, out_specs=(HBM,) * k,
        input_output_aliases={i: i for i in range(k)},
        compiler_params=pltpu.CompilerParams(has_side_effects=EFFECT),
    )(*bufs, *sems, *tie)
    return list(out)


def _gather_copies(k, direct):
    def copies(refs, send_sems, recv_sems):
        x, y, c = _position()
        chips = [(1 - x, y), (x, 1 - y), (1 - x, 1 - y)]
        out = []
        for a in range(k):
            land = refs[k + a]
            if direct:
                mine = land.at[4 * x + 2 * y + c]
                for kk, to in enumerate([(x, y, 1 - c)] + [(*chip, c) for chip in chips]):
                    out.append(pltpu.make_async_remote_copy(
                        src_ref=refs[a], dst_ref=mine, send_sem=send_sems.at[4 * a + kk],
                        recv_sem=recv_sems.at[4 * a + kk], device_id=to, device_id_type=MESH))
            else:
                for j, (px, py) in enumerate(chips):
                    slot = land.at[4 * px + 2 * py + c]
                    out.append(pltpu.make_async_remote_copy(
                        src_ref=slot, dst_ref=slot, send_sem=send_sems.at[3 * a + j], recv_sem=recv_sems.at[3 * a + j],
                        device_id=(x, y, 1 - c), device_id_type=MESH))
        return out
    return copies


def _all_gather_behind(shards, start_after, mid_after):
    k = len(shards)
    me = 4 * lax.axis_index("x") + 2 * lax.axis_index("y") + lax.axis_index("c")
    lands = [lax.dynamic_update_slice(lax.empty((N_DEV,) + s.shape, s.dtype), s[None], (me, 0, 0)) for s in shards]
    direct, passed = _gather_copies(k, True), _gather_copies(k, False)

    send_a, recv_a, bufs, token = _exchange_start("gather_start", list(shards) + lands, 4 * k, direct, start_after)

    def finish():
        def wait_ici(refs, send_sems, recv_sems):
            for i, cp in enumerate(direct(refs, send_sems, recv_sems)):
                if i % 4:
                    cp.wait_recv()

        send_b, recv_b, bufs2 = _exchange_mid("gather_pass", bufs, [send_a, recv_a], 3 * k, wait_ici, passed,
                                              mid_after())

        def wait_all(refs, sa, ra, sb, rb):
            for i, cp in enumerate(direct(refs, sa, ra)):
                cp.wait_send()
                if i % 4 == 0:
                    cp.wait_recv()
            for cp in passed(refs, sb, rb):
                cp.wait()

        return _exchange_wait("gather_wait", bufs2, [send_a, recv_a, send_b, recv_b], wait_all)[k:]

    return token, finish


def _pair_copies(k):
    def copies(refs, send_sems, recv_sems):
        x, y, c = _position()
        return [pltpu.make_async_remote_copy(
            src_ref=refs[a].at[2 * ch + 1 - c], dst_ref=refs[k + a].at[ch], send_sem=send_sems.at[4 * a + ch],
            recv_sem=recv_sems.at[4 * a + ch], device_id=(x, y, 1 - c), device_id_type=MESH)
            for a in range(k) for ch in range(4)]
    return copies


def _chip_copies(k):
    def copies(refs, send_sems, recv_sems):
        x, y, c = _position()
        return [pltpu.make_async_remote_copy(
            src_ref=refs[a].at[2 * px + py], dst_ref=refs[k + a].at[rel], send_sem=send_sems.at[3 * a + rel],
            recv_sem=recv_sems.at[3 * a + rel], device_id=(px, py, c), device_id_type=MESH)
            for a in range(k) for rel, (px, py) in enumerate([(1 - x, y), (x, 1 - y), (1 - x, 1 - y)])]
    return copies


def _symmetric_exchange(name, srcs, n_land, copies_of):
    k = len(srcs)
    lands = [lax.empty((n_land,) + s.shape[1:], s.dtype) for s in srcs]
    copies = copies_of(k)
    send_sems, recv_sems, bufs, token = _exchange_start(name + "_start", list(srcs) + lands, n_land * k, copies)

    def finish(after):
        def wait_all(refs, ss, rs):
            for cp in copies(refs, ss, rs):
                cp.wait()

        done = _exchange_wait(name + "_wait", bufs, [send_sems, recv_sems], wait_all, after)
        return done[:k], done[k:]

    return token, finish


def _pair_add(fulls, recvs, wires):
    k = len(fulls)
    core = lax.axis_index("c").astype(jnp.int32).reshape(1)
    shapes = [f.shape[1:] for f in fulls]

    def body(core_ref, *refs):
        f_refs, r_refs, pw_refs, own_refs = (refs[j * k:(j + 1) * k] for j in range(4))
        x, y, _ = _position()
        mine = pl.program_id(0) == 2 * x + y
        for f_ref, r_ref, pw_ref, own_ref in zip(f_refs, r_refs, pw_refs, own_refs):
            tot = f_ref[0, 0] + r_ref[0]
            pw_ref[0] = tot.astype(pw_ref.dtype)

            @pl.when(mine)
            def _():
                own_ref[...] = tot

    out = pl.pallas_call(
        body, name="grad_pair_add",
        grid_spec=pltpu.PrefetchScalarGridSpec(
            num_scalar_prefetch=1, grid=(4,),
            in_specs=[pl.BlockSpec((1, 1, r, c), lambda i, core_ref: (i, core_ref[0], 0, 0)) for r, c in shapes]
            + [pl.BlockSpec((1, r, c), lambda i, core_ref: (i, 0, 0)) for r, c in shapes],
            out_specs=tuple([pl.BlockSpec((1, r, c), lambda i, core_ref: (i, 0, 0)) for r, c in shapes]
                            + [pl.BlockSpec((r, c), lambda i, core_ref: (0, 0)) for r, c in shapes])),
        out_shape=tuple([jax.ShapeDtypeStruct((4, r, c), wd) for (r, c), wd in zip(shapes, wires)]
                        + [jax.ShapeDtypeStruct((r, c), F32) for r, c in shapes]),
        compiler_params=_params("arbitrary"),
    )(core, *[f.reshape(4, 2, *f.shape[1:]) for f in fulls], *recvs)
    return out[:k], out[k:]


def _adamw_math(w, g, m, v):
    m = ADAM_B1 * m + (1.0 - ADAM_B1) * g
    v = ADAM_B2 * v + (1.0 - ADAM_B2) * (g * g)
    m_hat = m / (1.0 - ADAM_B1 ** ADAM_STEP)
    v_hat = v / (1.0 - ADAM_B2 ** ADAM_STEP)
    delta = -ADAM_LR * (m_hat / (jnp.sqrt(v_hat) + ADAM_EPS) + ADAM_WD * w)
    return delta, m, v


ADAMW_STEPS = 2


def _adamw(owns, recvs, ws, ms, vs):
    k = len(ws)
    tiles = [w.shape[0] // ADAMW_STEPS if w.shape[0] % (16 * ADAMW_STEPS) == 0 else w.shape[0] for w in ws]
    whole = [t == w.shape[0] for t, w in zip(tiles, ws)]

    def body(*refs):
        g_refs, r_refs, w_refs, m_refs, v_refs = (refs[j * k:(j + 1) * k] for j in range(5))
        outs = refs[5 * k:]
        for j in range(k):
            g = g_refs[j][...]
            for i in range(3):
                g = g + r_refs[j][i].astype(F32)
            outs[4 * j][...] = g
            outs[4 * j + 1][...], outs[4 * j + 2][...], outs[4 * j + 3][...] = _adamw_math(
                w_refs[j][...], g, m_refs[j][...], v_refs[j][...])

    def blk(j, lead=()):
        shape = lead + (tiles[j], ws[j].shape[1])
        if whole[j]:
            return pl.BlockSpec(shape, lambda i: (0,) * len(shape))
        return pl.BlockSpec(shape, lambda i: (0,) * len(lead) + (i, 0))

    plain = [blk(j) for j in range(k)]
    out = pl.pallas_call(
        body, name="adamw", grid=(ADAMW_STEPS,),
        out_shape=tuple(jax.ShapeDtypeStruct(w.shape, F32) for w in ws for _ in range(4)),
        in_specs=plain + [blk(j, (3,)) for j in range(k)] + plain * 3,
        out_specs=tuple(plain[j] for j in range(k) for _ in range(4)),
        compiler_params=_params("arbitrary"),
    )(*owns, *recvs, *ws, *ms, *vs)
    return [out[4 * j:4 * j + 4] for j in range(k)]


SMALL = ("attn_norm", "b_gate", "pool_scale", "q_norm", "k_norm", "sinks", "ffn_norm", "conv_b")
SMALL_SIZES = (1024, 2048, 1024, 64, 64, 16, 1024, 5632)
SMALL_OFFSETS = tuple(sum(-(-s // LANES) * LANES for s in SMALL_SIZES[:i]) for i in range(len(SMALL_SIZES) + 1))
SMALL_WIDTH = SMALL_OFFSETS[-1] + LANES


def _pack_small(d, loss=None):
    parts = [jnp.pad(d[n].reshape(1, -1), ((0, 0), (0, -s % LANES))) for n, s in zip(SMALL, SMALL_SIZES)]
    last = jnp.zeros((1, LANES), F32) if loss is None else jnp.pad(loss.reshape(1, 1), ((0, 0), (0, LANES - 1)))
    return jnp.concatenate(parts + [last], axis=1)


def _small_allreduce(gp):
    def body(g_ref, sum_ref, slots_ref, send_sems, recv_sems):
        x, y, c = _position()
        me = 4 * x + 2 * y + c
        slots_ref[me] = g_ref[...]
        cps = []
        for rel in range(1, N_DEV):
            fx, fy, fc = (rel >> 2) & 1, (rel >> 1) & 1, rel & 1
            to = (1 - x if fx else x, 1 - y if fy else y, 1 - c if fc else c)
            cps.append(pltpu.make_async_remote_copy(
                src_ref=g_ref, dst_ref=slots_ref.at[me], send_sem=send_sems.at[rel - 1],
                recv_sem=recv_sems.at[rel - 1], device_id=to, device_id_type=MESH))
        for cp in cps:
            cp.start()
        for cp in cps:
            cp.wait()
        g = slots_ref[0]
        for i in range(1, N_DEV):
            g = g + slots_ref[i]
        sum_ref[...] = g

    vm = pl.BlockSpec(memory_space=pltpu.VMEM)
    return pl.pallas_call(
        body, name="small_allreduce", out_shape=jax.ShapeDtypeStruct((1, SMALL_WIDTH), F32),
        in_specs=[vm], out_specs=vm,
        scratch_shapes=[pltpu.VMEM((N_DEV, 1, SMALL_WIDTH), F32), pltpu.SemaphoreType.DMA((N_DEV - 1,)),
                        pltpu.SemaphoreType.DMA((N_DEV - 1,))],
    )(gp)


def _small_adamw(gsum, ws, ms, vs):
    n = len(SMALL)

    def body(g_ref, *rest):
        w_refs, m_refs, v_refs, outs = rest[:n], rest[n:2 * n], rest[2 * n:3 * n], rest[3 * n:]
        for j, size in enumerate(SMALL_SIZES):
            g = g_ref[:, SMALL_OFFSETS[j]:SMALL_OFFSETS[j] + size]
            results = (g,) + _adamw_math(w_refs[j][...], g, m_refs[j][...], v_refs[j][...])
            for kind, val in enumerate(results):
                outs[kind * n + j][...] = val
        outs[4 * n][...] = g_ref[:, SMALL_OFFSETS[-1]:SMALL_WIDTH]

    vm = pl.BlockSpec(memory_space=pltpu.VMEM)
    shapes = tuple(jax.ShapeDtypeStruct((1, s), F32) for s in SMALL_SIZES) * 4
    return pl.pallas_call(
        body, name="small_adamw", out_shape=shapes + (jax.ShapeDtypeStruct((1, LANES), F32),),
        in_specs=[vm] * (1 + 3 * n), out_specs=(vm,) * (4 * n + 1),
    )(gsum, *ws, *ms, *vs)


WEIGHTS = ("attn_norm", "w_in", "b_gate", "w_pool", "pool_scale", "q_norm", "k_norm", "sinks", "w_out", "ffn_norm",
           "w_up", "conv_w", "conv_b", "w_down")


def kernel(x, positions, attn_norm, w_in, b_gate, w_pool, pool_scale, q_norm, k_norm, sinks, w_out, ffn_norm, w_up, conv_w, conv_b, w_down, loss_target, m_attn_norm, m_w_in, m_b_gate, m_w_pool, m_pool_scale, m_q_norm, m_k_norm, m_sinks, m_w_out, m_ffn_norm, m_w_up, m_conv_w, m_conv_b, m_w_down, v_attn_norm, v_w_in, v_b_gate, v_w_pool, v_pool_scale, v_q_norm, v_k_norm, v_sinks, v_w_out, v_ffn_norm, v_w_up, v_conv_w, v_conv_b, v_w_down):
    w = dict(attn_norm=attn_norm, w_in=w_in, b_gate=b_gate, w_pool=w_pool, pool_scale=pool_scale, q_norm=q_norm,
             k_norm=k_norm, sinks=sinks, w_out=w_out, ffn_norm=ffn_norm, w_up=w_up, conv_w=conv_w, conv_b=conv_b,
             w_down=w_down)
    m = dict(attn_norm=m_attn_norm, w_in=m_w_in, b_gate=m_b_gate, w_pool=m_w_pool, pool_scale=m_pool_scale,
             q_norm=m_q_norm, k_norm=m_k_norm, sinks=m_sinks, w_out=m_w_out, ffn_norm=m_ffn_norm, w_up=m_w_up,
             conv_w=m_conv_w, conv_b=m_conv_b, w_down=m_w_down)
    v = dict(attn_norm=v_attn_norm, w_in=v_w_in, b_gate=v_b_gate, w_pool=v_w_pool, pool_scale=v_pool_scale,
             q_norm=v_q_norm, k_norm=v_k_norm, sinks=v_sinks, w_out=v_w_out, ffn_norm=v_ffn_norm, w_up=v_w_up,
             conv_w=v_conv_w, conv_b=v_conv_b, w_down=v_w_down)
    seq = x.shape[1]
    tm = 256
    tw = min(seq, 512)
    tk, tk_ff = min(seq, 1024), min(seq, 2048)
    xs, target, pos_col = x[0], loss_target[0], positions.reshape(seq, 1)
    invf, bd, q_norm_t, k_norm_t, sink_rows = _attention_constants(q_norm, k_norm, sinks)
    out, done = {}, {}
    nat = {"w_in": (D_MODEL, 544), "w_pool": (128, POOL_GROUP), "w_out": (128, D_MODEL), "w_up": (D_MODEL, 704),
           "conv_w": (3, 704), "w_down": (352, D_MODEL)}

    def update(names, owns, recvs):
        flip = [name in ("w_in", "w_up") for name in names]
        shards = [[t[name].reshape(nat[name]).T if f else t[name].reshape(nat[name]) for name, f in zip(names, flip)]
                  for t in (w, m, v)]
        for name, f, res in zip(names, flip, _adamw(owns, recvs, *shards)):
            done[name] = res[1]
            out[name] = [(t.T if f else t).reshape(w[name].shape) for t in res]

    (g_win,) = _all_gather([w_in[0].T.astype(MM)])
    win_t = g_win.reshape(IN_WIDTH, D_MODEL)
    fwd = {}
    token, gather_rest = _all_gather_behind(
        [w_pool[0].astype(MM).reshape(128, POOL_GROUP), w_out[0].astype(MM), w_up[0].T.astype(MM), conv_w[0],
         w_down[0].astype(MM)], g_win, lambda: fwd["b"])

    tabs = _rope_tables(pos_col, invf)
    h, u, q, kv, g = _inproj_fwd(xs, attn_norm + token[0:1, 0:1], win_t, b_gate, tw)
    b, probs, sink_probs = _attn_fwd(q, kv, tabs, q_norm_t, k_norm_t, sink_rows, bd)
    fwd["b"] = b
    g_wpool, g_wout, g_wup, g_convw, g_wdown = gather_rest()
    wpool = g_wpool.reshape(N_DEV, 4, 32, POOL_GROUP).transpose(1, 0, 2, 3).reshape(4, POOL_GROUP, POOL_GROUP)
    wout = g_wout.reshape(D_MODEL, D_MODEL)
    wup_t = g_wup.reshape(2 * D_FF, D_MODEL)
    convw = g_convw.transpose(1, 0, 2).reshape(3, 2 * D_FF)
    wdown = g_wdown.reshape(D_FF, D_MODEL)
    a = _pool_fwd(u, wpool, pool_scale, tw)
    x1, mix = _mix_out_fwd(xs, g, a, b, wout, tw)
    h2, uff, ucv, dy, dyb, lossp = _ffn_fwd(x1, ffn_norm, wup_t, convw, conv_b, wdown, target, tm)

    du, act, d_conv_w, d_conv_b = _ffn_bwd_a(dyb, uff, ucv, convw, wdown, tm)
    d_wdown = _matmul_tn(act, dyb, FF_CHUNK, tk_ff, "dw_down")
    dx1, dx1b, d_ffn_norm = _ffn_bwd_b(du, wup_t, x1, ffn_norm, dy, tm)
    d_wup_t = _matmul_tn(du, h2, FF_CHUNK, tk_ff, "dw_up")
    late = ("w_down", "w_up", "conv_w")
    late_wire = (WIRE, WIRE, F32)
    late_full = [d_wdown.reshape(N_DEV, 352, D_MODEL), d_wup_t.reshape(N_DEV, 704, D_MODEL),
                 d_conv_w.reshape(3, N_DEV, 704).transpose(1, 0, 2)]
    token, late_pair = _symmetric_exchange("late_pair", late_full, 4, _pair_copies)
    da, db, dzg, d_b_gate, d_wout, d_win_t = _mix_bwd(dx1b, wout, g, a, b, mix, h, tm, after=token)
    late_pw, late_own = _pair_add(*late_pair(dzg), late_wire)
    token, late_chip = _symmetric_exchange("late_chip", list(late_pw), 3, _chip_copies)
    dzu, d_wpool, d_pool_scale, d_win_t = _pool_bwd(u, da, wpool, pool_scale, h, d_win_t, tw, after=token)
    dzq, dzkv, d_q_norm, d_k_norm, d_sinks = _attn_bwd(q, kv, db, tabs, q_norm_t, k_norm_t, probs, sink_probs, bd,
                                                       after=token)
    d_win_t = _matmul_tn(dzq, h, 1024, tk, "dw_in_q", into=d_win_t, row0=O_Q)
    d_win_t = _matmul_tn(dzkv, h, 256, tk, "dw_in_kv", into=d_win_t, row0=O_KV)
    early = ("w_in", "w_pool", "w_out")
    early_full = [d_win_t.reshape(N_DEV, 544, D_MODEL),
                  d_wpool.reshape(4, N_DEV, 32, POOL_GROUP).transpose(1, 0, 2, 3).reshape(N_DEV, 128, POOL_GROUP),
                  d_wout.reshape(N_DEV, 128, D_MODEL)]
    token, early_pair = _symmetric_exchange("early_pair", early_full, 4, _pair_copies)
    update(late, late_own, late_chip(token)[1])
    early_pw, early_own = _pair_add(*early_pair([done[n] for n in late]), (WIRE,) * 3)
    token, early_chip = _symmetric_exchange("early_chip", list(early_pw), 3, _chip_copies)
    grad_x, d_attn_norm = _inproj_bwd(dzu, dzq, dzkv, dzg, win_t, xs, attn_norm, dx1, tw, after=token)
    gr = dict(attn_norm=d_attn_norm, b_gate=d_b_gate, pool_scale=d_pool_scale, q_norm=d_q_norm, k_norm=d_k_norm,
              sinks=d_sinks[:, 0:N_Q_HEADS], ffn_norm=d_ffn_norm, conv_b=d_conv_b)
    small = _small_adamw(_small_allreduce(_pack_small(gr, lossp[0, 0])), *[[t[n] for n in SMALL] for t in (w, m, v)])
    loss = small[-1][0, 0]
    for j, name in enumerate(SMALL):
        out[name] = [small[kind * len(SMALL) + j] for kind in range(4)]
    update(early, early_own, early_chip(small[0])[1])

    return (loss, grad_x[None], *[out[n][0] for n in WEIGHTS], *[out[n][1] for n in WEIGHTS],
            *[out[n][2] for n in WEIGHTS], *[out[n][3] for n in WEIGHTS])
```

```python
import functools

import numpy as np
import jax
import jax.numpy as jnp
from jax import lax
from jax.experimental import pallas as pl
from jax.experimental.pallas import tpu as pltpu

F32 = jnp.float32
MM = jnp.bfloat16
WIRE = jnp.bfloat16
ACT = jnp.bfloat16

D_MODEL = 1024
D_FF = 2816
HEAD_DIM = 64
N_Q_HEADS = 16
N_KV_HEADS = 2
GQA_GROUP = 8
BLOCK = 128
ROPE_DIM = 16
ROPE_THETA = 500000.0
POOL_WINDOWS = (2, 4, 8, 16)
POOL_GROUP = 256
POOL_HALO = 32
EPS = 1e-6
NEG = -1e30
O_U, O_Q, O_KV, O_G, IN_WIDTH = 0, 1024, 2048, 2304, 4352
FF_CHUNK = 1408

ADAM_LR, ADAM_B1, ADAM_B2, ADAM_EPS, ADAM_WD, ADAM_STEP = 0.001, 0.9, 0.999, 1e-08, 0.01, 10

N_DEV = 8
LANES = 128
VMEM_LIMIT_BYTES = 56 * 1024 * 1024
MESH = pl.DeviceIdType.MESH


def _params(*sem):
    return pltpu.CompilerParams(dimension_semantics=sem, vmem_limit_bytes=VMEM_LIMIT_BYTES)


def _resident(shape):
    nd = len(shape)
    return pl.BlockSpec(shape, lambda *_: (0,) * nd, pipeline_mode=pl.Buffered(1))


def _rows(tm, width):
    return pl.BlockSpec((tm, width), lambda i: (i, 0))


def _mm(a, b):
    return jnp.dot(a.astype(MM), b.astype(MM), preferred_element_type=F32)


def _mm_nt(a, b):
    return lax.dot_general(a.astype(MM), b.astype(MM), (((1,), (1,)), ((), ())), preferred_element_type=F32)


def _mm_tn(a, b):
    return lax.dot_general(a.astype(MM), b.astype(MM), (((0,), (0,)), ((), ())), preferred_element_type=F32)


def _rmsnorm_fwd(x, g):
    r = lax.rsqrt(jnp.mean(x * x, axis=-1, keepdims=True) + EPS)
    return x * r * g, r


def _rmsnorm_bwd(x, r, g, dy):
    xn = x * r
    dxn = dy * g
    dx = r * (dxn - xn * jnp.mean(dxn * xn, axis=-1, keepdims=True))
    return dx, dy * xn


def _group_sum64(v, bd):
    hi = v.astype(MM)
    lo = (v - hi.astype(F32)).astype(MM)
    outs = []
    for t in range(v.shape[1] // LANES):
        sl = slice(LANES * t, LANES * (t + 1))
        outs.append(jnp.dot(hi[:, sl], bd, preferred_element_type=F32)
                    + jnp.dot(lo[:, sl], bd, preferred_element_type=F32))
    return outs[0] if len(outs) == 1 else jnp.concatenate(outs, axis=1)


def _head_norm_fwd(x, g, bd):
    r = lax.rsqrt(_group_sum64(x * x, bd) * (1.0 / HEAD_DIM) + EPS)
    return x * r * g, r


def _head_norm_bwd(x, r, g, dy, bd):
    xn = x * r
    dxn = dy * g
    dx = r * (dxn - xn * (_group_sum64(dxn * xn, bd) * (1.0 / HEAD_DIM)))
    return dx, dy * xn


def _rope(x, c, s1, s2):
    w = x.shape[1]
    return x * c + pltpu.roll(x, w - ROPE_DIM // 2, 1) * s1 + pltpu.roll(x, ROPE_DIM // 2, 1) * s2


def _rope_bwd(dy, c, s1, s2):
    w = dy.shape[1]
    return dy * c + pltpu.roll(dy * s1, ROPE_DIM // 2, 1) + pltpu.roll(dy * s2, w - ROPE_DIM // 2, 1)


def _tile_lanes(t, reps):
    return t if reps == 1 else jnp.concatenate([t] * reps, axis=1)


def _rope_tables(pos_col, invf):
    s = pos_col.shape[0]
    tm = min(s, 1024)

    def body(pos_ref, invf_ref, c_ref, s1_ref, s2_ref):
        ang = pos_ref[...].astype(F32) * invf_ref[...]
        lane = lax.broadcasted_iota(jnp.int32, ang.shape, 1) % HEAD_DIM
        sn = jnp.sin(ang)
        c_ref[...] = jnp.cos(ang)
        s1_ref[...] = jnp.where(lane < ROPE_DIM // 2, -sn, 0.0)
        s2_ref[...] = jnp.where((lane >= ROPE_DIM // 2) & (lane < ROPE_DIM), sn, 0.0)

    out = jax.ShapeDtypeStruct((s, LANES), F32)
    return pl.pallas_call(
        body, name="rope_tables", grid=(s // tm,), out_shape=(out, out, out),
        in_specs=[_rows(tm, 1), _resident((1, LANES))],
        out_specs=(_rows(tm, LANES),) * 3, compiler_params=_params("parallel"),
    )(pos_col, invf)


def _inproj_fwd(x, attn_norm, win_t, b_gate, tm):
    s = x.shape[0]

    def body(x_ref, gn_ref, w_ref, bg_ref, h_ref, u_ref, q_ref, kv_ref, g_ref):
        h, _ = _rmsnorm_fwd(x_ref[...], gn_ref[...])
        h = h.astype(MM)
        h_ref[...] = h
        u_ref[...] = _mm_nt(h, w_ref[O_U:O_Q, :])
        q_ref[...] = _mm_nt(h, w_ref[O_Q:O_KV, :])
        kv_ref[...] = _mm_nt(h, w_ref[O_KV:O_G, :])
        g_ref[...] = jax.nn.sigmoid(_mm_nt(h, w_ref[O_G:IN_WIDTH, :]) + bg_ref[...])

    sd = jax.ShapeDtypeStruct
    return pl.pallas_call(
        body, name="inproj_fwd", grid=(s // tm,),
        out_shape=(sd((s, D_MODEL), MM), sd((s, 1024), F32), sd((s, 1024), F32), sd((s, 256), F32),
                   sd((s, 2048), F32)),
        in_specs=[_rows(tm, D_MODEL), _resident((1, D_MODEL)), _resident((IN_WIDTH, D_MODEL)), _resident((1, 2048))],
        out_specs=(_rows(tm, D_MODEL), _rows(tm, 1024), _rows(tm, 1024), _rows(tm, 256), _rows(tm, 2048)),
        compiler_params=_params("parallel"),
    )(x, attn_norm, win_t, b_gate)


def _window_sums(src_ref, s1_ref, s2_ref, cols, w, tm, ahead):
    n = POOL_HALO + tm

    def level(src, width, lo):
        if ahead:
            return src(0, n - lo) + src(width, n - lo)
        return src(lo, n - lo) + src(lo - width, n - lo)

    def final(src, width):
        if ahead:
            return src(0, tm) + src(width, tm)
        return src(POOL_HALO, tm) + src(POOL_HALO - width, tm)

    levels = [lambda r0, rows: src_ref[pl.ds(r0, rows), cols]]
    for k, ref in ((1, s1_ref), (2, s2_ref), (3, s1_ref)):
        if w == 2 ** k:
            break
        lo = 8 * k
        ref[pl.ds(0 if ahead else lo, n - lo), :] = level(levels[-1], 2 ** (k - 1), lo)
        levels.append(functools.partial(lambda ref, r0, rows: ref[pl.ds(r0, rows), :], ref))
    return final(levels[-1], w // 2)


def _pooled(ext_ref, s1_ref, s2_ref, tm, row0):
    t = (row0 + lax.broadcasted_iota(jnp.int32, (tm, 1), 0)).astype(F32)
    out = []
    for gi, w in enumerate(POOL_WINDOWS):
        cols = slice(gi * POOL_GROUP, (gi + 1) * POOL_GROUP)
        acc = _window_sums(ext_ref, s1_ref, s2_ref, cols, w, tm, ahead=False)
        cnt = jnp.minimum(t + 1.0, float(w))
        out.append(acc / cnt - ext_ref[pl.ds(POOL_HALO, tm), cols])
    return out


def _pool_fwd(u, wpool, pool_scale, tm):
    s = u.shape[0]
    hb = tm // POOL_HALO

    def body(u_ref, halo_ref, wp_ref, ps_ref, a_ref, ext_ref, s1_ref, s2_ref):
        i = pl.program_id(0)
        ext_ref[pl.ds(0, POOL_HALO), :] = jnp.where(i > 0, halo_ref[...], 0.0)
        ext_ref[pl.ds(POOL_HALO, tm), :] = u_ref[...]
        pooled = _pooled(ext_ref, s1_ref, s2_ref, tm, i * tm)
        for gi in range(4):
            cols = slice(gi * POOL_GROUP, (gi + 1) * POOL_GROUP)
            a_ref[:, cols] = (_mm(pooled[gi], wp_ref[gi]) * ps_ref[:, cols]).astype(a_ref.dtype)

    return pl.pallas_call(
        body, name="pool_fwd", grid=(s // tm,), out_shape=jax.ShapeDtypeStruct((s, 1024), ACT),
        in_specs=[_rows(tm, 1024), pl.BlockSpec((POOL_HALO, 1024), lambda i: (jnp.maximum(i * hb - 1, 0), 0)),
                  _resident((4, POOL_GROUP, POOL_GROUP)), _resident((1, 1024))],
        out_specs=_rows(tm, 1024),
        scratch_shapes=[pltpu.VMEM((POOL_HALO + tm, 1024), F32), pltpu.VMEM((POOL_HALO + tm, POOL_GROUP), F32),
                        pltpu.VMEM((POOL_HALO + tm, POOL_GROUP), F32)],
        compiler_params=_params("parallel"),
    )(u, u, wpool, pool_scale)


PAIRS = GQA_GROUP // 2
PAIR_COLS = PAIRS * BLOCK


def _fold_masks(n):
    r = lax.broadcasted_iota(jnp.int32, (BLOCK, BLOCK), 0)
    i = lax.broadcasted_iota(jnp.int32, (BLOCK, BLOCK), 1)
    prev = r > i
    return prev, jnp.where(prev & (n == 0), NEG, 0.0)


def _fold(band, prev):
    return jnp.where(prev, band[0:BLOCK, :], band[BLOCK:2 * BLOCK, :])


def _unfold(folded, prev):
    top = jnp.where(prev, folded, 0.0)
    return jnp.concatenate([top, folded - top], axis=0).astype(MM)


def _probs_by_pair(sc, sink, masks):
    prev, bias = masks
    out = []
    for j in range(PAIRS):
        cols = slice(j * BLOCK, (j + 1) * BLOCK)
        out.append(_softmax_sink_t(_fold(sc[:, cols], prev) + bias, sink[:, cols]))
    return out


def _stack_pairs(x, hk):
    return jnp.concatenate([x[:, (PAIRS * hk + j) * LANES:(PAIRS * hk + j + 1) * LANES] for j in range(PAIRS)], axis=0)


def _parity_bands(t, hk):
    low = lax.broadcasted_iota(jnp.int32, t.shape, 1) < HEAD_DIM
    own = jnp.where(low if hk == 0 else ~low, t, 0.0)
    other = pltpu.roll(own, HEAD_DIM, 1)
    return (own, other) if hk == 0 else (other, own)


def _fold_parity(even, odd, hk):
    low = lax.broadcasted_iota(jnp.int32, even.shape, 1) < HEAD_DIM
    comb = jnp.where(low, even, odd)
    comb = comb + pltpu.roll(comb, HEAD_DIM, 1)
    return jnp.where(low if hk == 0 else ~low, comb, 0.0)


def _softmax_sink_t(s, sink):
    m = jnp.maximum(jnp.max(s, axis=0, keepdims=True), sink)
    p = jnp.exp(s - m)
    es = jnp.exp(sink - m)
    inv = 1.0 / (jnp.sum(p, axis=0, keepdims=True) + es)
    return p * inv, es * inv


def _attn_fwd(q, kv, tabs, q_norm_t, k_norm_t, sink_rows, bd):
    s = q.shape[0]
    nb = s // BLOCK
    scale = HEAD_DIM ** -0.5
    cur = lambda n: (n, 0)
    prv = lambda n: (jnp.maximum(2 * n - 1, 0), 0)

    def body(q_ref, kvc_ref, kvp_ref, c_ref, s1_ref, s2_ref, cp_ref, s1p_ref, s2p_ref, qn_ref, kn_ref, sink_ref,
             bd_ref, o_ref, pr_ref, ps_ref):
        n = pl.program_id(0)
        bdm = bd_ref[...]
        c, s1, s2 = c_ref[...], s1_ref[...], s2_ref[...]
        qh, _ = _head_norm_fwd(q_ref[...], qn_ref[...], bdm)
        qr = (_rope(qh, _tile_lanes(c, 8), _tile_lanes(s1, 8), _tile_lanes(s2, 8)) * scale).astype(MM)
        kc, _ = _head_norm_fwd(kvc_ref[:, 0:128], kn_ref[...], bdm)
        kp, _ = _head_norm_fwd(kvp_ref[:, 0:128], kn_ref[...], bdm)
        k3 = jnp.concatenate([_rope(kp, cp_ref[...], s1p_ref[...], s2p_ref[...]), _rope(kc, c, s1, s2)], axis=0)
        v3 = jnp.concatenate([kvp_ref[:, 128:256], kvc_ref[:, 128:256]], axis=0)
        for sub in range(2):
            rows = slice(sub * BLOCK, (sub + 1) * BLOCK)
            k2, v2 = k3[sub * BLOCK:(sub + 2) * BLOCK, :], v3[sub * BLOCK:(sub + 2) * BLOCK, :]
            mask = _fold_masks(2 * n + sub)
            for hk in range(N_KV_HEADS):
                qs = _stack_pairs(qr[rows, :], hk)
                ot = jnp.zeros((LANES, PAIR_COLS), F32)
                for par, (kb, vb) in enumerate(zip(_parity_bands(k2, hk), _parity_bands(v2, hk))):
                    probs = _probs_by_pair(_mm_nt(kb, qs), sink_ref[hk, par], mask)
                    pr_ref[sub, 2 * hk + par] = jnp.concatenate([pr for pr, _ in probs], axis=1).astype(MM)
                    ps_ref[sub, 2 * hk + par] = jnp.concatenate([ps for _, ps in probs], axis=1)
                    ot = ot + _mm(vb.T, jnp.concatenate([_unfold(pr, mask[0]) for pr, _ in probs], axis=1))
                for j in range(PAIRS):
                    col = (PAIRS * hk + j) * LANES
                    o_ref[rows, col:col + LANES] = ot[:, j * BLOCK:(j + 1) * BLOCK].T.astype(o_ref.dtype)

    two = 2 * BLOCK
    sd = jax.ShapeDtypeStruct
    return pl.pallas_call(
        body, name="attn_fwd", grid=(nb // 2,),
        out_shape=(sd((s, 1024), ACT), sd((nb, 4, BLOCK, PAIR_COLS), MM), sd((nb, 4, 1, PAIR_COLS), F32)),
        in_specs=[pl.BlockSpec((two, 1024), cur), pl.BlockSpec((two, 256), cur), pl.BlockSpec((BLOCK, 256), prv),
                  pl.BlockSpec((two, LANES), cur), pl.BlockSpec((two, LANES), cur), pl.BlockSpec((two, LANES), cur),
                  pl.BlockSpec((BLOCK, LANES), prv), pl.BlockSpec((BLOCK, LANES), prv), pl.BlockSpec((BLOCK, LANES), prv),
                  _resident((1, 1024)), _resident((1, 128)), _resident((N_KV_HEADS, 2, 1, PAIR_COLS)),
                  _resident((LANES, LANES))],
        out_specs=(pl.BlockSpec((two, 1024), cur), pl.BlockSpec((2, 4, BLOCK, PAIR_COLS), lambda n: (n, 0, 0, 0)),
                   pl.BlockSpec((2, 4, 1, PAIR_COLS), lambda n: (n, 0, 0, 0))),
        compiler_params=_params("parallel"),
    )(q, kv, kv, *tabs, *tabs, q_norm_t, k_norm_t, sink_rows, bd)


def _mix_out_fwd(x, g, a, b, wout, tm):
    s = x.shape[0]

    def body(x_ref, g_ref, a_ref, b_ref, w_ref, x1_ref, mix_ref):
        mix = (g_ref[:, 0:1024] * a_ref[...] + g_ref[:, 1024:2048] * b_ref[...]).astype(MM)
        mix_ref[...] = mix
        x1_ref[...] = x_ref[...] + _mm(mix, w_ref[...])

    return pl.pallas_call(
        body, name="mix_out_fwd", grid=(s // tm,),
        out_shape=(jax.ShapeDtypeStruct((s, D_MODEL), F32), jax.ShapeDtypeStruct((s, D_MODEL), MM)),
        in_specs=[_rows(tm, 1024), _rows(tm, 2048), _rows(tm, 1024), _rows(tm, 1024), _resident((1024, 1024))],
        out_specs=(_rows(tm, 1024), _rows(tm, 1024)), compiler_params=_params("parallel"),
    )(x, g, a, b, wout)


SHIFT_ROWS = 16


def _sublane_major_matrices(tm):
    r = np.arange(tm)
    pm = r[None, :] == ((tm // 8) * (r % 8) + r // 8)[:, None]
    return jnp.asarray(pm, MM), jnp.asarray(pm.T, MM)


def _to_sublane_major(pm, v):
    return jnp.dot(pm, v, preferred_element_type=F32).astype(MM)


def _to_time_order(pmt, v):
    hi = v.astype(MM)
    r1 = v - hi.astype(F32)
    mid = r1.astype(MM)
    lo = (r1 - mid.astype(F32)).astype(MM)
    dot = functools.partial(jnp.dot, preferred_element_type=F32)
    return dot(pmt, hi) + dot(pmt, mid) + dot(pmt, lo)


def _step_back(vreg_rows, before):
    sub = lax.broadcasted_iota(jnp.int32, vreg_rows.shape, 0)
    return jnp.where(sub == 0, before[7:8, :], pltpu.roll(vreg_rows, 1, 0))


def _step_ahead(vreg_rows, after):
    sub = lax.broadcasted_iota(jnp.int32, vreg_rows.shape, 0)
    return jnp.where(sub == 7, after[0:1, :], pltpu.roll(vreg_rows, 7, 0))


def _fill_back_rows(ext_ref, before, tm, cols):
    last = ext_ref[pl.ds(SHIFT_ROWS + tm - 8, 8), cols]
    pen = ext_ref[pl.ds(SHIFT_ROWS + tm - 16, 8), cols]
    ext_ref[pl.ds(8, 8), cols] = _step_back(last, before[8:16, :])
    ext_ref[pl.ds(0, 8), cols] = _step_back(pen, before[0:8, :])


def _causal_conv(uc, before, w, b, tm):
    back1 = _step_back(uc[tm - 8:tm, :], before[8:16, :])
    back2 = _step_back(uc[tm - 16:tm - 8, :], before[0:8, :])
    w0, w1, w2 = w[0:1, :], w[1:2, :], w[2:3, :]
    first = b + w0 * back2 + w1 * back1 + w2 * uc[0:8, :]
    second = b + w0 * back1 + w1 * uc[0:8, :] + w2 * uc[8:16, :]
    rest = b + w0 * uc[0:tm - 16, :] + w1 * uc[8:tm - 8, :] + w2 * uc[16:tm, :]
    return jnp.concatenate([first, second, rest], axis=0)


def _ffn_fwd(x1, ffn_norm, wup_t, conv_w, conv_b, wdown, target, tm):
    s = x1.shape[0]
    inv_d = 1.0 / D_MODEL
    pm, pmt = _sublane_major_matrices(tm)

    def body(x1_ref, gn_ref, wu_ref, cw_ref, cb_ref, wd_ref, tgt_ref, pm_ref, pmt_ref, h2_ref, u_ref, uc_ref, dy_ref,
             dyb_ref, loss_ref, carry_ref):
        i = pl.program_id(0)

        @pl.when(i == 0)
        def _():
            carry_ref[...] = jnp.zeros_like(carry_ref)
            loss_ref[...] = jnp.zeros_like(loss_ref)

        x1 = x1_ref[...]
        h2, _ = _rmsnorm_fwd(x1, gn_ref[...])
        h2 = _to_sublane_major(pm_ref[...], h2.astype(MM))
        h2_ref[...] = h2
        for c in range(4):
            cols = slice(c * FF_CHUNK, (c + 1) * FF_CHUNK)
            uc = _mm_nt(h2, wu_ref[cols, :])
            u_ref[:, cols] = uc
            uc_ref[:, cols] = _causal_conv(uc, carry_ref[:, cols], cw_ref[:, cols], cb_ref[:, cols], tm)
            carry_ref[:, cols] = uc[tm - SHIFT_ROWS:tm, :]
        down = jnp.zeros((tm, D_MODEL), F32)
        for c in range(2):
            gate = uc_ref[:, c * FF_CHUNK:(c + 1) * FF_CHUNK]
            val = uc_ref[:, D_FF + c * FF_CHUNK:D_FF + (c + 1) * FF_CHUNK]
            act = gate * jax.nn.sigmoid(gate) * val
            down = down + _mm(act, wd_ref[c * FF_CHUNK:(c + 1) * FF_CHUNK, :])
        err = x1 + _to_time_order(pmt_ref[...], down) - tgt_ref[...]
        loss_ref[...] += jnp.full(loss_ref.shape, 0.5 * inv_d * jnp.sum(err * err), F32)
        dy = err * inv_d
        dy_ref[...] = dy
        dyb_ref[...] = _to_sublane_major(pm_ref[...], dy.astype(MM))

    sd = jax.ShapeDtypeStruct
    return pl.pallas_call(
        body, name="ffn_fwd", grid=(s // tm,),
        out_shape=(sd((s, D_MODEL), MM), sd((s, 2 * D_FF), F32), sd((s, 2 * D_FF), F32), sd((s, D_MODEL), F32),
                   sd((s, D_MODEL), MM), sd((8, LANES), F32)),
        in_specs=[_rows(tm, 1024), _resident((1, 1024)), _resident((2 * D_FF, D_MODEL)), _resident((3, 2 * D_FF)),
                  _resident((1, 2 * D_FF)), _resident((D_FF, D_MODEL)), _rows(tm, 1024), _resident((tm, tm)),
                  _resident((tm, tm))],
        out_specs=(_rows(tm, 1024), _rows(tm, 2 * D_FF), _rows(tm, 2 * D_FF), _rows(tm, 1024), _rows(tm, 1024),
                   pl.BlockSpec((8, LANES), lambda i: (0, 0))),
        scratch_shapes=[pltpu.VMEM((SHIFT_ROWS, 2 * D_FF), F32)],
        compiler_params=_params("arbitrary"),
    )(x1, ffn_norm, wup_t, conv_w, conv_b, wdown, target, pm, pmt)


def _ffn_bwd_a(dyb, u, uc, conv_w, wdown, tm):
    s = dyb.shape[0]
    nt = s // tm
    hb = tm // SHIFT_ROWS
    rev = lambda i: (nt - 1 - i, 0)

    def body(dy_ref, u_ref, before_ref, uc_ref, cw_ref, wd_ref, du_ref, act_ref, dcw_ref, dcb_ref, ext_ref, extd_ref,
             ahead_ref):
        i = pl.program_id(0)
        first_tile = i == nt - 1

        @pl.when(i == 0)
        def _():
            ahead_ref[...] = jnp.zeros_like(ahead_ref)
            dcw_ref[...] = jnp.zeros_like(dcw_ref)
            dcb_ref[...] = jnp.zeros_like(dcb_ref)

        ext_ref[pl.ds(SHIFT_ROWS, tm), :] = u_ref[...]
        for c in range(4):
            cols = slice(c * FF_CHUNK, (c + 1) * FF_CHUNK)
            _fill_back_rows(ext_ref, jnp.where(first_tile, 0.0, before_ref[:, cols]), tm, cols)
        dy = dy_ref[...]
        for c in range(2):
            gate = uc_ref[:, c * FF_CHUNK:(c + 1) * FF_CHUNK]
            val = uc_ref[:, D_FF + c * FF_CHUNK:D_FF + (c + 1) * FF_CHUNK]
            sg = jax.nn.sigmoid(gate)
            sl = gate * sg
            act_ref[:, c * FF_CHUNK:(c + 1) * FF_CHUNK] = (sl * val).astype(MM)
            d_act = _mm_nt(dy, wd_ref[c * FF_CHUNK:(c + 1) * FF_CHUNK, :])
            extd_ref[pl.ds(0, tm), c * FF_CHUNK:(c + 1) * FF_CHUNK] = d_act * val * (sg * (1.0 + gate * (1.0 - sg)))
            extd_ref[pl.ds(0, tm), D_FF + c * FF_CHUNK:D_FF + (c + 1) * FF_CHUNK] = d_act * sl
        for c in range(4):
            cols = slice(c * FF_CHUNK, (c + 1) * FF_CHUNK)
            ahead = ahead_ref[:, cols]
            first2 = extd_ref[pl.ds(0, SHIFT_ROWS), cols]
            extd_ref[pl.ds(tm, 8), cols] = _step_ahead(first2[0:8, :], ahead[0:8, :])
            extd_ref[pl.ds(tm + 8, 8), cols] = _step_ahead(first2[8:16, :], ahead[8:16, :])
            ahead_ref[:, cols] = first2
            d0 = extd_ref[pl.ds(0, tm), cols]
            dcb_ref[:, cols] += jnp.sum(d0, axis=0, keepdims=True)
            for j in range(3):
                dcw_ref[j:j + 1, cols] += jnp.sum(d0 * ext_ref[pl.ds(8 * j, tm), cols], axis=0, keepdims=True)
            du = cw_ref[2:3, cols] * d0 + cw_ref[1:2, cols] * extd_ref[pl.ds(8, tm), cols]
            du = du + cw_ref[0:1, cols] * extd_ref[pl.ds(SHIFT_ROWS, tm), cols]
            du_ref[:, cols] = du.astype(MM)

    sd = jax.ShapeDtypeStruct
    return pl.pallas_call(
        body, name="ffn_bwd_a", grid=(nt,),
        out_shape=(sd((s, 2 * D_FF), MM), sd((s, D_FF), MM), sd((3, 2 * D_FF), F32), sd((1, 2 * D_FF), F32)),
        in_specs=[pl.BlockSpec((tm, D_MODEL), rev), pl.BlockSpec((tm, 2 * D_FF), rev),
                  pl.BlockSpec((SHIFT_ROWS, 2 * D_FF), lambda i: (jnp.maximum((nt - 1 - i) * hb - 1, 0), 0)),
                  pl.BlockSpec((tm, 2 * D_FF), rev), _resident((3, 2 * D_FF)), _resident((D_FF, D_MODEL))],
        out_specs=(pl.BlockSpec((tm, 2 * D_FF), rev), pl.BlockSpec((tm, D_FF), rev),
                   pl.BlockSpec((3, 2 * D_FF), lambda i: (0, 0)), pl.BlockSpec((1, 2 * D_FF), lambda i: (0, 0))),
        scratch_shapes=[pltpu.VMEM((SHIFT_ROWS + tm, 2 * D_FF), F32), pltpu.VMEM((tm + SHIFT_ROWS, 2 * D_FF), F32),
                        pltpu.VMEM((SHIFT_ROWS, 2 * D_FF), F32)],
        compiler_params=_params("arbitrary"),
    )(dyb, u, u, uc, conv_w, wdown)


def _after(after):
    tie = [] if after is None else list(after) if isinstance(after, (list, tuple)) else [after]
    return tie, [pl.BlockSpec(memory_space=pl.ANY)] * len(tie)


def _matmul_tn(a, b, tmo, tk, name, into=None, row0=0):
    s, m = a.shape
    n = b.shape[1]
    nk = s // tk
    rows = m if into is None else into.shape[0]
    assert row0 % LANES == 0 and tmo % LANES == 0
    grown, grown_spec = ([], []) if into is None else ([into], [ANY])

    def body(a_ref, b_ref, *rest):
        o_ref = rest[-1]
        k = pl.program_id(1)

        @pl.when(k == 0)
        def _():
            o_ref[...] = jnp.zeros_like(o_ref)

        o_ref[...] += _mm_tn(a_ref[...], b_ref[pl.ds(pl.multiple_of(k * tk, tk), tk), :])

    return pl.pallas_call(
        body, name=name, grid=(m // tmo, nk), out_shape=jax.ShapeDtypeStruct((rows, n), F32),
        in_specs=[pl.BlockSpec((tk, tmo), lambda i, k: (k, i)), _resident((s, n))] + grown_spec,
        out_specs=pl.BlockSpec((pl.Element(tmo), pl.Element(n)), lambda i, k: (pl.multiple_of(row0 + i * tmo, LANES), 0)),
        input_output_aliases={2: 0} if grown else {},
        compiler_params=_params("parallel", "arbitrary"),
    )(a, b, *grown)


def _ffn_bwd_b(du, wup_t, x1, ffn_norm, dy, tm):
    s = du.shape[0]

    def body(du_ref, wu_ref, x1_ref, gn_ref, dy_ref, pmt_ref, dx1_ref, dx1b_ref, dg_ref):
        @pl.when(pl.program_id(0) == 0)
        def _():
            dg_ref[...] = jnp.zeros_like(dg_ref)

        dh2 = _to_time_order(pmt_ref[...], _mm(du_ref[...], wu_ref[...]))
        x1 = x1_ref[...]
        _, r = _rmsnorm_fwd(x1, gn_ref[...])
        dx, dgr = _rmsnorm_bwd(x1, r, gn_ref[...], dh2)
        dg_ref[...] += jnp.sum(dgr, axis=0, keepdims=True)
        dx1 = dy_ref[...] + dx
        dx1_ref[...] = dx1
        dx1b_ref[...] = dx1.astype(MM)

    return pl.pallas_call(
        body, name="ffn_bwd_b", grid=(s // tm,),
        out_shape=(jax.ShapeDtypeStruct((s, D_MODEL), F32), jax.ShapeDtypeStruct((s, D_MODEL), MM),
                   jax.ShapeDtypeStruct((1, D_MODEL), F32)),
        in_specs=[_rows(tm, 2 * D_FF), _resident((2 * D_FF, D_MODEL)), _rows(tm, 1024), _resident((1, 1024)),
                  _rows(tm, 1024), _resident((tm, tm))],
        out_specs=(_rows(tm, 1024), _rows(tm, 1024), pl.BlockSpec((1, D_MODEL), lambda i: (0, 0))),
        compiler_params=_params("arbitrary"),
    )(du, wup_t, x1, ffn_norm, dy, _sublane_major_matrices(tm)[1])


def _win_rows(row0, rows):
    return pl.BlockSpec((pl.Element(rows), pl.Element(D_MODEL)), lambda i: (row0, 0), pipeline_mode=pl.Buffered(1))


def _mix_bwd(dx1b, wout, g, a, b, mix, h, tm, after=None):
    s = dx1b.shape[0]
    tie, tie_spec = _after(after)

    def body(dx_ref, w_ref, g_ref, a_ref, b_ref, mix_ref, h_ref, *rest):
        da_ref, db_ref, dzg_ref, dbg_ref, dwo_ref, dwin_ref = rest[-6:]

        @pl.when(pl.program_id(0) == 0)
        def _():
            dbg_ref[...] = jnp.zeros_like(dbg_ref)
            dwo_ref[...] = jnp.zeros_like(dwo_ref)
            dwin_ref[...] = jnp.zeros_like(dwin_ref)

        dx = dx_ref[...]
        dwo_ref[...] += _mm_tn(mix_ref[...], dx)
        dmix = _mm_nt(dx, w_ref[...])
        for half, src, dst in ((0, a_ref, da_ref), (1, b_ref, db_ref)):
            cols = slice(half * 1024, (half + 1) * 1024)
            gt = g_ref[:, cols]
            dst[...] = (dmix * gt).astype(dst.dtype)
            dz = dmix * src[...] * gt * (1.0 - gt)
            dzb = dz.astype(MM)
            dzg_ref[:, cols] = dzb
            dwin_ref[cols, :] += _mm_tn(dzb, h_ref[...])
            dbg_ref[:, cols] += jnp.sum(dz, axis=0, keepdims=True)

    sd = jax.ShapeDtypeStruct
    return pl.pallas_call(
        body, name="mix_bwd", grid=(s // tm,),
        out_shape=(sd((s, 1024), F32), sd((s, 1024), MM), sd((s, 2048), MM), sd((1, 2048), F32),
                   sd((D_MODEL, D_MODEL), F32), sd((IN_WIDTH, D_MODEL), F32)),
        in_specs=[_rows(tm, 1024), _resident((1024, 1024)), _rows(tm, 2048), _rows(tm, 1024), _rows(tm, 1024),
                  _rows(tm, 1024), _rows(tm, 1024)] + tie_spec,
        out_specs=(_rows(tm, 1024), _rows(tm, 1024), _rows(tm, 2048), pl.BlockSpec((1, 2048), lambda i: (0, 0)),
                   _resident((D_MODEL, D_MODEL)), _win_rows(O_G, IN_WIDTH - O_G)),
        compiler_params=_params("arbitrary"),
    )(dx1b, wout, g, a, b, mix, h, *tie)


def _pool_bwd(u, da, wpool, pool_scale, h, d_win_t, tm, after=None):
    s = u.shape[0]
    nt = s // tm
    hb = tm // POOL_HALO

    tie, tie_spec = _after(after)

    def body(u_ref, uh_ref, da_ref, dah_ref, wp_ref, ps_ref, h_ref, *rest):
        dzu_ref, dwp_ref, dps_ref, dwin_ref, ext_ref, exte_ref, s1_ref, s2_ref = rest[-8:]
        i = pl.program_id(0)

        @pl.when(i == 0)
        def _():
            dwp_ref[...] = jnp.zeros_like(dwp_ref)
            dps_ref[...] = jnp.zeros_like(dps_ref)
            dwin_ref[...] = jnp.zeros_like(dwin_ref)

        ext_ref[pl.ds(0, POOL_HALO), :] = jnp.where(i > 0, uh_ref[...], 0.0)
        ext_ref[pl.ds(POOL_HALO, tm), :] = u_ref[...]
        pooled = _pooled(ext_ref, s1_ref, s2_ref, tm, i * tm)
        da = da_ref[...]
        dah = jnp.where(i < nt - 1, dah_ref[...], 0.0)
        t = (i * tm + lax.broadcasted_iota(jnp.int32, (tm + POOL_HALO, 1), 0)).astype(F32)
        for gi, w in enumerate(POOL_WINDOWS):
            cols = slice(gi * POOL_GROUP, (gi + 1) * POOL_GROUP)
            pg = pooled[gi].astype(MM)
            wg = wp_ref[gi]
            mixed = _mm(pg, wg)
            dps_ref[:, cols] += jnp.sum(da[:, cols] * mixed, axis=0, keepdims=True)
            dmx = (da[:, cols] * ps_ref[:, cols]).astype(MM)
            dwp_ref[gi] += _mm_tn(pg, dmx)
            dpl = _mm_nt(dmx, wg)
            dplh = _mm_nt(dah[:, cols] * ps_ref[:, cols], wg)
            cnt = jnp.minimum(t + 1.0, float(w))
            exte_ref[pl.ds(0, tm), cols] = dpl / cnt[0:tm]
            exte_ref[pl.ds(tm, POOL_HALO), cols] = dplh / cnt[tm:tm + POOL_HALO]
            acc = _window_sums(exte_ref, s1_ref, s2_ref, cols, w, tm, ahead=True)
            dzu = (acc - dpl).astype(MM)
            dzu_ref[:, cols] = dzu
            dwin_ref[cols, :] += _mm_tn(dzu, h_ref[...])

    sd = jax.ShapeDtypeStruct
    last_halo = s // POOL_HALO - 1
    return pl.pallas_call(
        body, name="pool_bwd", grid=(nt,),
        out_shape=(sd((s, 1024), MM), sd((4, POOL_GROUP, POOL_GROUP), F32), sd((1, 1024), F32),
                   sd((IN_WIDTH, D_MODEL), F32)),
        in_specs=[_rows(tm, 1024), pl.BlockSpec((POOL_HALO, 1024), lambda i: (jnp.maximum(i * hb - 1, 0), 0)),
                  _rows(tm, 1024),
                  pl.BlockSpec((POOL_HALO, 1024), lambda i: (jnp.minimum((i + 1) * hb, last_halo), 0)),
                  _resident((4, POOL_GROUP, POOL_GROUP)), _resident((1, 1024)), _rows(tm, 1024)] + tie_spec + [ANY],
        out_specs=(_rows(tm, 1024), pl.BlockSpec((4, POOL_GROUP, POOL_GROUP), lambda i: (0, 0, 0)),
                   pl.BlockSpec((1, 1024), lambda i: (0, 0)), _win_rows(O_U, O_Q - O_U)),
        input_output_aliases={7 + len(tie): 3},
        scratch_shapes=[pltpu.VMEM((POOL_HALO + tm, 1024), F32), pltpu.VMEM((tm + POOL_HALO, 1024), F32),
                        pltpu.VMEM((POOL_HALO + tm, POOL_GROUP), F32), pltpu.VMEM((POOL_HALO + tm, POOL_GROUP), F32)],
        compiler_params=_params("arbitrary"),
    )(u, u, da, da, wpool, pool_scale, h, *tie, d_win_t)


def _attn_bwd(q, kv, db, tabs, q_norm_t, k_norm_t, probs, sink_probs, bd, after=None):
    s = q.shape[0]
    nb = s // BLOCK
    scale = HEAD_DIM ** -0.5
    steps = nb // 2
    cur = lambda n: (jnp.minimum(n, steps - 1), 0)
    prv = lambda n: (jnp.maximum(n - 1, 0), 0)
    tie, tie_spec = _after(after)

    def body(q_ref, kvc_ref, kvp_ref, db_ref, c_ref, s1_ref, s2_ref, cp_ref, s1p_ref, s2p_ref, qn_ref, kn_ref,
             pr_ref, ps_ref, bd_ref, *rest):
        (dzq_ref, dzkv_ref, dqn_ref, dkn_ref, dsk_ref,
         carry_ref, tot_ref, dqr_ref, qacc_ref, kacc_ref, sacc_ref) = rest[-11:]
        n = pl.program_id(0)
        bdm = bd_ref[...]
        kn = kn_ref[...]

        @pl.when(n == 0)
        def _():
            carry_ref[...] = jnp.zeros_like(carry_ref)
            qacc_ref[...] = jnp.zeros_like(qacc_ref)
            kacc_ref[...] = jnp.zeros_like(kacc_ref)
            sacc_ref[...] = jnp.zeros_like(sacc_ref)

        kp_raw = kvp_ref[:, 0:128]
        kph, rp = _head_norm_fwd(kp_raw, kn, bdm)
        cp, s1p, s2p = cp_ref[...], s1p_ref[...], s2p_ref[...]

        @pl.when(n < steps)
        def _():
            c, s1, s2 = c_ref[...], s1_ref[...], s2_ref[...]
            c8, s18, s28 = _tile_lanes(c, 8), _tile_lanes(s1, 8), _tile_lanes(s2, 8)
            q_raw = q_ref[...]
            qh, rq = _head_norm_fwd(q_raw, qn_ref[...], bdm)
            qr = (_rope(qh, c8, s18, s28) * scale).astype(MM)
            kc, _ = _head_norm_fwd(kvc_ref[:, 0:128], kn, bdm)
            last = slice(BLOCK, 2 * BLOCK)
            k3 = jnp.concatenate([_rope(kph[last, :], cp[last, :], s1p[last, :], s2p[last, :]), _rope(kc, c, s1, s2)],
                                 axis=0)
            v3 = jnp.concatenate([kvp_ref[last, 128:256], kvc_ref[:, 128:256]], axis=0)
            dob = db_ref[...].astype(MM)
            lane = lax.broadcasted_iota(jnp.int32, (1, LANES), 1)
            dsk = jnp.zeros((1, LANES), F32)
            parts = []
            for sub in range(2):
                rows = slice(sub * BLOCK, (sub + 1) * BLOCK)
                k2, v2 = k3[sub * BLOCK:(sub + 2) * BLOCK, :], v3[sub * BLOCK:(sub + 2) * BLOCK, :]
                mask = _fold_masks(2 * n + sub)
                dk2 = jnp.zeros((2 * BLOCK, LANES), F32)
                dv2 = jnp.zeros((2 * BLOCK, LANES), F32)
                for hk in range(N_KV_HEADS):
                    qs = _stack_pairs(qr[rows, :], hk)
                    do = _stack_pairs(dob[rows, :], hk)
                    dqt = jnp.zeros((LANES, PAIR_COLS), F32)
                    dkb, dvb = [], []
                    for par, (kb, vb) in enumerate(zip(_parity_bands(k2, hk), _parity_bands(v2, hk))):
                        dp = _mm_nt(vb, do)
                        prs, dss = [], []
                        for j in range(PAIRS):
                            cols = slice(j * BLOCK, (j + 1) * BLOCK)
                            pr = pr_ref[sub, 2 * hk + par, :, cols].astype(F32)
                            psink = ps_ref[sub, 2 * hk + par, :, cols]
                            dpj = _fold(dp[:, cols], mask[0])
                            coldot = jnp.sum(pr * dpj, axis=0, keepdims=True)
                            dss.append(_unfold(pr * (dpj - coldot), mask[0]))
                            prs.append(_unfold(pr, mask[0]))
                            h = hk * GQA_GROUP + 2 * j + par
                            dsk = dsk + jnp.where(lane == h, jnp.sum(-psink * coldot), 0.0)
                        ds, pr = jnp.concatenate(dss, axis=1), jnp.concatenate(prs, axis=1)
                        dqt = dqt + _mm(kb.T, ds)
                        dkb.append(_mm(ds, qs))
                        dvb.append(_mm(pr, do))
                    for j in range(PAIRS):
                        col = (PAIRS * hk + j) * LANES
                        dqr_ref[rows, col:col + LANES] = dqt[:, j * BLOCK:(j + 1) * BLOCK].T
                    dk2 = dk2 + _fold_parity(dkb[0], dkb[1], hk)
                    dv2 = dv2 + _fold_parity(dvb[0], dvb[1], hk)
                parts.append((dk2, dv2))
            first, second = slice(0, BLOCK), slice(BLOCK, 2 * BLOCK)
            for cols, (even, odd) in ((slice(0, 128), (parts[0][0], parts[1][0])),
                                      (slice(128, 256), (parts[0][1], parts[1][1]))):
                tot_ref[first, cols] = carry_ref[first, cols]
                tot_ref[second, cols] = carry_ref[second, cols] + even[first, :]
                carry_ref[first, cols] = even[second, :] + odd[first, :]
                carry_ref[second, cols] = odd[second, :]
            sacc_ref[...] += dsk
            dqh = _rope_bwd(dqr_ref[...] * scale, c8, s18, s28)
            dq, dgq = _head_norm_bwd(q_raw, rq, qn_ref[...], dqh, bdm)
            dzq_ref[...] = dq.astype(MM)
            qacc_ref[...] += jnp.sum(dgq, axis=0, keepdims=True)

        @pl.when(n == steps)
        def _():
            tot_ref[...] = carry_ref[...]

        dkh = _rope_bwd(tot_ref[:, 0:128], cp, s1p, s2p)
        dkr, dgk = _head_norm_bwd(kp_raw, rp, kn, dkh, bdm)
        dzkv_ref[:, 0:128] = dkr.astype(MM)
        dzkv_ref[:, 128:256] = tot_ref[:, 128:256].astype(MM)
        kacc_ref[...] += jnp.where(n > 0, jnp.sum(dgk, axis=0, keepdims=True), 0.0)

        @pl.when(n == steps)
        def _():
            fold = qacc_ref[:, 0:HEAD_DIM]
            for h in range(1, N_Q_HEADS):
                fold = fold + qacc_ref[:, h * HEAD_DIM:(h + 1) * HEAD_DIM]
            dqn_ref[...] = fold
            dkn_ref[...] = kacc_ref[:, 0:HEAD_DIM] + kacc_ref[:, HEAD_DIM:2 * HEAD_DIM]
            dsk_ref[...] = sacc_ref[...]

    two = 2 * BLOCK
    tab = lambda im: pl.BlockSpec((two, LANES), im)
    sd = jax.ShapeDtypeStruct
    const = lambda n: (0, 0)
    return pl.pallas_call(
        body, name="attn_bwd", grid=(steps + 1,),
        out_shape=(sd((s, 1024), MM), sd((s, 256), MM), sd((1, HEAD_DIM), F32), sd((1, HEAD_DIM), F32),
                   sd((1, LANES), F32)),
        in_specs=[pl.BlockSpec((two, 1024), cur), pl.BlockSpec((two, 256), cur), pl.BlockSpec((two, 256), prv),
                  pl.BlockSpec((two, 1024), cur), tab(cur), tab(cur), tab(cur), tab(prv), tab(prv), tab(prv),
                  _resident((1, 1024)), _resident((1, 128)),
                  pl.BlockSpec((2, 4, BLOCK, PAIR_COLS), lambda n: (jnp.minimum(n, steps - 1), 0, 0, 0)),
                  pl.BlockSpec((2, 4, 1, PAIR_COLS), lambda n: (jnp.minimum(n, steps - 1), 0, 0, 0)),
                  _resident((LANES, LANES))] + tie_spec,
        out_specs=(pl.BlockSpec((two, 1024), cur), pl.BlockSpec((two, 256), prv),
                   pl.BlockSpec((1, HEAD_DIM), const), pl.BlockSpec((1, HEAD_DIM), const),
                   pl.BlockSpec((1, LANES), const)),
        scratch_shapes=[pltpu.VMEM((two, 256), F32), pltpu.VMEM((two, 256), F32), pltpu.VMEM((two, 1024), F32),
                        pltpu.VMEM((1, 1024), F32), pltpu.VMEM((1, 128), F32), pltpu.VMEM((1, LANES), F32)],
        compiler_params=_params("arbitrary"),
    )(q, kv, kv, db, *tabs, *tabs, q_norm_t, k_norm_t, probs, sink_probs, bd, *tie)


def _inproj_bwd(dzu, dzq, dzkv, dzg, win_t, x, attn_norm, dx1, tm, after=None):
    s = x.shape[0]
    tie, tie_spec = _after(after)

    def body(du_ref, dq_ref, dkv_ref, dg_ref, w_ref, x_ref, gn_ref, dx1_ref, *rest):
        gx_ref, dgn_ref = rest[-2:]

        @pl.when(pl.program_id(0) == 0)
        def _():
            dgn_ref[...] = jnp.zeros_like(dgn_ref)

        dh = _mm(du_ref[...], w_ref[O_U:O_Q, :]) + _mm(dq_ref[...], w_ref[O_Q:O_KV, :])
        dh = dh + _mm(dkv_ref[...], w_ref[O_KV:O_G, :]) + _mm(dg_ref[...], w_ref[O_G:IN_WIDTH, :])
        x = x_ref[...]
        _, r = _rmsnorm_fwd(x, gn_ref[...])
        dx, dgr = _rmsnorm_bwd(x, r, gn_ref[...], dh)
        dgn_ref[...] += jnp.sum(dgr, axis=0, keepdims=True)
        gx_ref[...] = dx1_ref[...] + dx

    return pl.pallas_call(
        body, name="inproj_bwd", grid=(s // tm,),
        out_shape=(jax.ShapeDtypeStruct((s, D_MODEL), F32), jax.ShapeDtypeStruct((1, D_MODEL), F32)),
        in_specs=[_rows(tm, 1024), _rows(tm, 1024), _rows(tm, 256), _rows(tm, 2048),
                  _resident((IN_WIDTH, D_MODEL)), _rows(tm, 1024), _resident((1, 1024)), _rows(tm, 1024)] + tie_spec,
        out_specs=(_rows(tm, 1024), pl.BlockSpec((1, D_MODEL), lambda i: (0, 0))),
        compiler_params=_params("arbitrary"),
    )(dzu, dzq, dzkv, dzg, win_t, x, attn_norm, dx1, *tie)


def _attention_constants(q_norm, k_norm, sinks):
    inv_freq = np.float32(ROPE_THETA) ** (-np.arange(0, ROPE_DIM, 2, dtype=np.float32) / np.float32(ROPE_DIM))
    lane = np.arange(LANES) % HEAD_DIM
    invf = jnp.asarray(np.where(lane < ROPE_DIM, inv_freq[lane % (ROPE_DIM // 2)], 0.0).reshape(1, LANES), F32)
    bd = jnp.asarray(np.arange(LANES)[:, None] // HEAD_DIM == np.arange(LANES)[None, :] // HEAD_DIM, MM)
    q_norm_t = jnp.tile(q_norm, (1, N_Q_HEADS))
    k_norm_t = jnp.tile(k_norm, (1, N_KV_HEADS))
    sink_rows = jnp.repeat(sinks.reshape(N_KV_HEADS, PAIRS, 2).transpose(0, 2, 1), BLOCK, axis=2)
    sink_rows = sink_rows.reshape(N_KV_HEADS, 2, 1, PAIR_COLS)
    return invf, bd, q_norm_t, k_norm_t, sink_rows


ANY = pl.BlockSpec(memory_space=pl.ANY)


def _position():
    return lax.axis_index("x"), lax.axis_index("y"), lax.axis_index("c")


def _all_gather(shards):
    k = len(shards)

    def body(*refs):
        ins, outs = refs[:k], refs[k:2 * k]
        send_sems, recv_sems, local_sems = refs[2 * k:]
        x, y, c = _position()
        me, sibling = (x, y, c), (x, y, 1 - c)
        chips = [(1 - x, y), (x, 1 - y), (1 - x, 1 - y)]

        def copy(a, kk, block, to, src=None):
            dst = outs[a].at[4 * block[0] + 2 * block[1] + block[2]]
            return pltpu.make_async_remote_copy(
                src_ref=dst if src is None else src, dst_ref=dst, send_sem=send_sems.at[a * 7 + kk],
                recv_sem=recv_sems.at[a * 7 + kk], device_id=to, device_id_type=MESH)

        mine = [pltpu.make_async_copy(ins[a], outs[a].at[4 * x + 2 * y + c], local_sems.at[a]) for a in range(k)]
        for cp in mine:
            cp.start()
        first = []
        for a in range(k):
            first.append(copy(a, 0, me, sibling, src=ins[a]))
            first += [copy(a, 1 + j, me, (*chip, c), src=ins[a]) for j, chip in enumerate(chips)]
        for cp in first:
            cp.start()
        passed = []
        for j, chip in enumerate(chips):
            for a in range(k):
                copy(a, 1 + j, (*chip, c), me).wait_recv()
                cp = copy(a, 4 + j, (*chip, c), sibling)
                cp.start()
                passed.append(cp)
        for a in range(k):
            copy(a, 0, sibling, me).wait_recv()
            for j, chip in enumerate(chips):
                copy(a, 4 + j, (*chip, 1 - c), me).wait_recv()
        for cp in first + passed:
            cp.wait_send()
        for cp in mine:
            cp.wait()

    return pl.pallas_call(
        body, name="all_gather_weights",
        out_shape=tuple(jax.ShapeDtypeStruct((N_DEV,) + s.shape, s.dtype) for s in shards),
        in_specs=[ANY] * k, out_specs=(ANY,) * k,
        scratch_shapes=[pltpu.SemaphoreType.DMA((7 * k,)), pltpu.SemaphoreType.DMA((7 * k,)),
                        pltpu.SemaphoreType.DMA((k,))],
    )(*shards)


HBM = pl.BlockSpec(memory_space=pltpu.HBM)
SEM = pl.BlockSpec(memory_space=pltpu.SEMAPHORE)
EFFECT = pltpu.SideEffectType.DATAFLOW_SIDE_EFFECTING


def _exchange_start(name, bufs, n_sems, copies, after=None):
    k = len(bufs)
    tie, tie_spec = _after(after)
    n_in = k + len(tie)

    def body(*refs):
        for cp in copies(refs[:k], refs[n_in], refs[n_in + 1]):
            cp.start()
        refs[-1][...] = jnp.zeros_like(refs[-1])

    dma = pltpu.SemaphoreType.DMA((n_sems,))
    out = pl.pallas_call(
        body, name=name,
        out_shape=(dma, dma, *[pltpu.HBM(b.shape, b.dtype) for b in bufs], jax.ShapeDtypeStruct((8, LANES), F32)),
        in_specs=[HBM] * k + tie_spec, out_specs=(SEM, SEM, *[HBM] * k, pl.BlockSpec(memory_space=pltpu.VMEM)),
        input_output_aliases={i: 2 + i for i in range(k)},
        compiler_params=pltpu.CompilerParams(has_side_effects=EFFECT),
    )(*[pltpu.with_memory_space_constraint(b, pltpu.HBM) for b in bufs], *tie)
    return out[0], out[1], list(out[2:2 + k]), out[-1]


def _exchange_mid(name, bufs, sems_in, n_sems, waits, copies, after):
    k, ns = len(bufs), len(sems_in)

    def body(*refs):
        ins = refs[:k]
        waits(ins, *refs[k:k + ns])
        for cp in copies(ins, refs[k + ns + 1], refs[k + ns + 2]):
            cp.start()

    dma = pltpu.SemaphoreType.DMA((n_sems,))
    out = pl.pallas_call(
        body, name=name, out_shape=(dma, dma, *[pltpu.HBM(b.shape, b.dtype) for b in bufs]),
        in_specs=[HBM] * k + [SEM] * ns + [ANY], out_specs=(SEM, SEM, *[HBM] * k),
        input_output_aliases={i: 2 + i for i in range(k)},
        compiler_params=pltpu.CompilerParams(has_side_effects=EFFECT),
    )(*bufs, *sems_in, after)
    return out[0], out[1], list(out[2:])


def _exchange_wait(name, bufs, sems, waits, after=None):
    k, ns = len(bufs), len(sems)
    tie, tie_spec = _after(after)

    def body(*refs):
        waits(refs[:k], *refs[k:k + ns])

    out = pl.pallas_call(
        body, name=name, out_shape=tuple(pltpu.HBM(b.shape, b.dtype) for b in bufs),
        in_specs=[HBM] * k + [SEM] * ns + tie_spec, out_specs=(HBM,) * k,
        input_output_aliases={i: i for i in range(k)},
        compiler_params=pltpu.CompilerParams(has_side_effects=EFFECT),
    )(*bufs, *sems, *tie)
    return list(out)


def _gather_copies(k, direct):
    def copies(refs, send_sems, recv_sems):
        x, y, c = _position()
        chips = [(1 - x, y), (x, 1 - y), (1 - x, 1 - y)]
        out = []
        for a in range(k):
            land = refs[k + a]
            if direct:
                mine = land.at[4 * x + 2 * y + c]
                for kk, to in enumerate([(x, y, 1 - c)] + [(*chip, c) for chip in chips]):
                    out.append(pltpu.make_async_remote_copy(
                        src_ref=refs[a], dst_ref=mine, send_sem=send_sems.at[4 * a + kk],
                        recv_sem=recv_sems.at[4 * a + kk], device_id=to, device_id_type=MESH))
            else:
                for j, (px, py) in enumerate(chips):
                    slot = land.at[4 * px + 2 * py + c]
                    out.append(pltpu.make_async_remote_copy(
                        src_ref=slot, dst_ref=slot, send_sem=send_sems.at[3 * a + j], recv_sem=recv_sems.at[3 * a + j],
                        device_id=(x, y, 1 - c), device_id_type=MESH))
        return out
    return copies


def _all_gather_behind(shards, start_after, mid_after):
    k = len(shards)
    me = 4 * lax.axis_index("x") + 2 * lax.axis_index("y") + lax.axis_index("c")
    lands = [lax.dynamic_update_slice(lax.empty((N_DEV,) + s.shape, s.dtype), s[None], (me, 0, 0)) for s in shards]
    direct, passed = _gather_copies(k, True), _gather_copies(k, False)

    send_a, recv_a, bufs, token = _exchange_start("gather_start", list(shards) + lands, 4 * k, direct, start_after)

    def finish():
        def wait_ici(refs, send_sems, recv_sems):
            for i, cp in enumerate(direct(refs, send_sems, recv_sems)):
                if i % 4:
                    cp.wait_recv()

        send_b, recv_b, bufs2 = _exchange_mid("gather_pass", bufs, [send_a, recv_a], 3 * k, wait_ici, passed,
                                              mid_after())

        def wait_all(refs, sa, ra, sb, rb):
            for i, cp in enumerate(direct(refs, sa, ra)):
                cp.wait_send()
                if i % 4 == 0:
                    cp.wait_recv()
            for cp in passed(refs, sb, rb):
                cp.wait()

        return _exchange_wait("gather_wait", bufs2, [send_a, recv_a, send_b, recv_b], wait_all)[k:]

    return token, finish


def _pair_copies(k):
    def copies(refs, send_sems, recv_sems):
        x, y, c = _position()
        return [pltpu.make_async_remote_copy(
            src_ref=refs[a].at[2 * ch + 1 - c], dst_ref=refs[k + a].at[ch], send_sem=send_sems.at[4 * a + ch],
            recv_sem=recv_sems.at[4 * a + ch], device_id=(x, y, 1 - c), device_id_type=MESH)
            for a in range(k) for ch in range(4)]
    return copies


def _chip_copies(k):
    def copies(refs, send_sems, recv_sems):
        x, y, c = _position()
        return [pltpu.make_async_remote_copy(
            src_ref=refs[a].at[2 * px + py], dst_ref=refs[k + a].at[rel], send_sem=send_sems.at[3 * a + rel],
            recv_sem=recv_sems.at[3 * a + rel], device_id=(px, py, c), device_id_type=MESH)
            for a in range(k) for rel, (px, py) in enumerate([(1 - x, y), (x, 1 - y), (1 - x, 1 - y)])]
    return copies


def _symmetric_exchange(name, srcs, n_land, copies_of):
    k = len(srcs)
    lands = [lax.empty((n_land,) + s.shape[1:], s.dtype) for s in srcs]
    copies = copies_of(k)
    send_sems, recv_sems, bufs, token = _exchange_start(name + "_start", list(srcs) + lands, n_land * k, copies)

    def finish(after):
        def wait_all(refs, ss, rs):
            for cp in copies(refs, ss, rs):
                cp.wait()

        done = _exchange_wait(name + "_wait", bufs, [send_sems, recv_sems], wait_all, after)
        return done[:k], done[k:]

    return token, finish


def _pair_add(fulls, recvs, wires):
    k = len(fulls)
    core = lax.axis_index("c").astype(jnp.int32).reshape(1)
    shapes = [f.shape[1:] for f in fulls]

    def body(core_ref, *refs):
        f_refs, r_refs, pw_refs, own_refs = (refs[j * k:(j + 1) * k] for j in range(4))
        x, y, _ = _position()
        mine = pl.program_id(0) == 2 * x + y
        for f_ref, r_ref, pw_ref, own_ref in zip(f_refs, r_refs, pw_refs, own_refs):
            tot = f_ref[0, 0] + r_ref[0]
            pw_ref[0] = tot.astype(pw_ref.dtype)

            @pl.when(mine)
            def _():
                own_ref[...] = tot

    out = pl.pallas_call(
        body, name="grad_pair_add",
        grid_spec=pltpu.PrefetchScalarGridSpec(
            num_scalar_prefetch=1, grid=(4,),
            in_specs=[pl.BlockSpec((1, 1, r, c), lambda i, core_ref: (i, core_ref[0], 0, 0)) for r, c in shapes]
            + [pl.BlockSpec((1, r, c), lambda i, core_ref: (i, 0, 0)) for r, c in shapes],
            out_specs=tuple([pl.BlockSpec((1, r, c), lambda i, core_ref: (i, 0, 0)) for r, c in shapes]
                            + [pl.BlockSpec((r, c), lambda i, core_ref: (0, 0)) for r, c in shapes])),
        out_shape=tuple([jax.ShapeDtypeStruct((4, r, c), wd) for (r, c), wd in zip(shapes, wires)]
                        + [jax.ShapeDtypeStruct((r, c), F32) for r, c in shapes]),
        compiler_params=_params("arbitrary"),
    )(core, *[f.reshape(4, 2, *f.shape[1:]) for f in fulls], *recvs)
    return out[:k], out[k:]


def _adamw_math(w, g, m, v):
    m = ADAM_B1 * m + (1.0 - ADAM_B1) * g
    v = ADAM_B2 * v + (1.0 - ADAM_B2) * (g * g)
    m_hat = m / (1.0 - ADAM_B1 ** ADAM_STEP)
    v_hat = v / (1.0 - ADAM_B2 ** ADAM_STEP)
    delta = -ADAM_LR * (m_hat / (jnp.sqrt(v_hat) + ADAM_EPS) + ADAM_WD * w)
    return delta, m, v


ADAMW_STEPS = 2


def _adamw(owns, recvs, ws, ms, vs):
    k = len(ws)
    tiles = [w.shape[0] // ADAMW_STEPS if w.shape[0] % (16 * ADAMW_STEPS) == 0 else w.shape[0] for w in ws]
    whole = [t == w.shape[0] for t, w in zip(tiles, ws)]

    def body(*refs):
        g_refs, r_refs, w_refs, m_refs, v_refs = (refs[j * k:(j + 1) * k] for j in range(5))
        outs = refs[5 * k:]
        for j in range(k):
            g = g_refs[j][...]
            for i in range(3):
                g = g + r_refs[j][i].astype(F32)
            outs[4 * j][...] = g
            outs[4 * j + 1][...], outs[4 * j + 2][...], outs[4 * j + 3][...] = _adamw_math(
                w_refs[j][...], g, m_refs[j][...], v_refs[j][...])

    def blk(j, lead=()):
        shape = lead + (tiles[j], ws[j].shape[1])
        if whole[j]:
            return pl.BlockSpec(shape, lambda i: (0,) * len(shape))
        return pl.BlockSpec(shape, lambda i: (0,) * len(lead) + (i, 0))

    plain = [blk(j) for j in range(k)]
    out = pl.pallas_call(
        body, name="adamw", grid=(ADAMW_STEPS,),
        out_shape=tuple(jax.ShapeDtypeStruct(w.shape, F32) for w in ws for _ in range(4)),
        in_specs=plain + [blk(j, (3,)) for j in range(k)] + plain * 3,
        out_specs=tuple(plain[j] for j in range(k) for _ in range(4)),
        compiler_params=_params("arbitrary"),
    )(*owns, *recvs, *ws, *ms, *vs)
    return [out[4 * j:4 * j + 4] for j in range(k)]


SMALL = ("attn_norm", "b_gate", "pool_scale", "q_norm", "k_norm", "sinks", "ffn_norm", "conv_b")
SMALL_SIZES = (1024, 2048, 1024, 64, 64, 16, 1024, 5632)
SMALL_OFFSETS = tuple(sum(-(-s // LANES) * LANES for s in SMALL_SIZES[:i]) for i in range(len(SMALL_SIZES) + 1))
SMALL_WIDTH = SMALL_OFFSETS[-1] + LANES


def _pack_small(d, loss=None):
    parts = [jnp.pad(d[n].reshape(1, -1), ((0, 0), (0, -s % LANES))) for n, s in zip(SMALL, SMALL_SIZES)]
    last = jnp.zeros((1, LANES), F32) if loss is None else jnp.pad(loss.reshape(1, 1), ((0, 0), (0, LANES - 1)))
    return jnp.concatenate(parts + [last], axis=1)


def _small_allreduce(gp):
    def body(g_ref, sum_ref, slots_ref, send_sems, recv_sems):
        x, y, c = _position()
        me = 4 * x + 2 * y + c
        slots_ref[me] = g_ref[...]
        cps = []
        for rel in range(1, N_DEV):
            fx, fy, fc = (rel >> 2) & 1, (rel >> 1) & 1, rel & 1
            to = (1 - x if fx else x, 1 - y if fy else y, 1 - c if fc else c)
            cps.append(pltpu.make_async_remote_copy(
                src_ref=g_ref, dst_ref=slots_ref.at[me], send_sem=send_sems.at[rel - 1],
                recv_sem=recv_sems.at[rel - 1], device_id=to, device_id_type=MESH))
        for cp in cps:
            cp.start()
        for cp in cps:
            cp.wait()
        g = slots_ref[0]
        for i in range(1, N_DEV):
            g = g + slots_ref[i]
        sum_ref[...] = g

    vm = pl.BlockSpec(memory_space=pltpu.VMEM)
    return pl.pallas_call(
        body, name="small_allreduce", out_shape=jax.ShapeDtypeStruct((1, SMALL_WIDTH), F32),
        in_specs=[vm], out_specs=vm,
        scratch_shapes=[pltpu.VMEM((N_DEV, 1, SMALL_WIDTH), F32), pltpu.SemaphoreType.DMA((N_DEV - 1,)),
                        pltpu.SemaphoreType.DMA((N_DEV - 1,))],
    )(gp)


def _small_adamw(gsum, ws, ms, vs):
    n = len(SMALL)

    def body(g_ref, *rest):
        w_refs, m_refs, v_refs, outs = rest[:n], rest[n:2 * n], rest[2 * n:3 * n], rest[3 * n:]
        for j, size in enumerate(SMALL_SIZES):
            g = g_ref[:, SMALL_OFFSETS[j]:SMALL_OFFSETS[j] + size]
            results = (g,) + _adamw_math(w_refs[j][...], g, m_refs[j][...], v_refs[j][...])
            for kind, val in enumerate(results):
                outs[kind * n + j][...] = val
        outs[4 * n][...] = g_ref[:, SMALL_OFFSETS[-1]:SMALL_WIDTH]

    vm = pl.BlockSpec(memory_space=pltpu.VMEM)
    shapes = tuple(jax.ShapeDtypeStruct((1, s), F32) for s in SMALL_SIZES) * 4
    return pl.pallas_call(
        body, name="small_adamw", out_shape=shapes + (jax.ShapeDtypeStruct((1, LANES), F32),),
        in_specs=[vm] * (1 + 3 * n), out_specs=(vm,) * (4 * n + 1),
    )(gsum, *ws, *ms, *vs)


WEIGHTS = ("attn_norm", "w_in", "b_gate", "w_pool", "pool_scale", "q_norm", "k_norm", "sinks", "w_out", "ffn_norm",
           "w_up", "conv_w", "conv_b", "w_down")


def kernel(x, positions, attn_norm, w_in, b_gate, w_pool, pool_scale, q_norm, k_norm, sinks, w_out, ffn_norm, w_up, conv_w, conv_b, w_down, loss_target, m_attn_norm, m_w_in, m_b_gate, m_w_pool, m_pool_scale, m_q_norm, m_k_norm, m_sinks, m_w_out, m_ffn_norm, m_w_up, m_conv_w, m_conv_b, m_w_down, v_attn_norm, v_w_in, v_b_gate, v_w_pool, v_pool_scale, v_q_norm, v_k_norm, v_sinks, v_w_out, v_ffn_norm, v_w_up, v_conv_w, v_conv_b, v_w_down):
    w = dict(attn_norm=attn_norm, w_in=w_in, b_gate=b_gate, w_pool=w_pool, pool_scale=pool_scale, q_norm=q_norm,
             k_norm=k_norm, sinks=sinks, w_out=w_out, ffn_norm=ffn_norm, w_up=w_up, conv_w=conv_w, conv_b=conv_b,
             w_down=w_down)
    m = dict(attn_norm=m_attn_norm, w_in=m_w_in, b_gate=m_b_gate, w_pool=m_w_pool, pool_scale=m_pool_scale,
             q_norm=m_q_norm, k_norm=m_k_norm, sinks=m_sinks, w_out=m_w_out, ffn_norm=m_ffn_norm, w_up=m_w_up,
             conv_w=m_conv_w, conv_b=m_conv_b, w_down=m_w_down)
    v = dict(attn_norm=v_attn_norm, w_in=v_w_in, b_gate=v_b_gate, w_pool=v_w_pool, pool_scale=v_pool_scale,
             q_norm=v_q_norm, k_norm=v_k_norm, sinks=v_sinks, w_out=v_w_out, ffn_norm=v_ffn_norm, w_up=v_w_up,
             conv_w=v_conv_w, conv_b=v_conv_b, w_down=v_w_down)
    seq = x.shape[1]
    tm = 256
    tw = min(seq, 512)
    tk, tk_ff = min(seq, 1024), min(seq, 2048)
    xs, target, pos_col = x[0], loss_target[0], positions.reshape(seq, 1)
    invf, bd, q_norm_t, k_norm_t, sink_rows = _attention_constants(q_norm, k_norm, sinks)
    out, done = {}, {}
    nat = {"w_in": (D_MODEL, 544), "w_pool": (128, POOL_GROUP), "w_out": (128, D_MODEL), "w_up": (D_MODEL, 704),
           "conv_w": (3, 704), "w_down": (352, D_MODEL)}

    def update(names, owns, recvs):
        flip = [name in ("w_in", "w_up") for name in names]
        shards = [[t[name].reshape(nat[name]).T if f else t[name].reshape(nat[name]) for name, f in zip(names, flip)]
                  for t in (w, m, v)]
        for name, f, res in zip(names, flip, _adamw(owns, recvs, *shards)):
            done[name] = res[1]
            out[name] = [(t.T if f else t).reshape(w[name].shape) for t in res]

    (g_win,) = _all_gather([w_in[0].T.astype(MM)])
    win_t = g_win.reshape(IN_WIDTH, D_MODEL)
    fwd = {}
    token, gather_rest = _all_gather_behind(
        [w_pool[0].astype(MM).reshape(128, POOL_GROUP), w_out[0].astype(MM), w_up[0].T.astype(MM), conv_w[0],
         w_down[0].astype(MM)], g_win, lambda: fwd["b"])

    tabs = _rope_tables(pos_col, invf)
    h, u, q, kv, g = _inproj_fwd(xs, attn_norm + token[0:1, 0:1], win_t, b_gate, tw)
    b, probs, sink_probs = _attn_fwd(q, kv, tabs, q_norm_t, k_norm_t, sink_rows, bd)
    fwd["b"] = b
    g_wpool, g_wout, g_wup, g_convw, g_wdown = gather_rest()
    wpool = g_wpool.reshape(N_DEV, 4, 32, POOL_GROUP).transpose(1, 0, 2, 3).reshape(4, POOL_GROUP, POOL_GROUP)
    wout = g_wout.reshape(D_MODEL, D_MODEL)
    wup_t = g_wup.reshape(2 * D_FF, D_MODEL)
    convw = g_convw.transpose(1, 0, 2).reshape(3, 2 * D_FF)
    wdown = g_wdown.reshape(D_FF, D_MODEL)
    a = _pool_fwd(u, wpool, pool_scale, tw)
    x1, mix = _mix_out_fwd(xs, g, a, b, wout, tw)
    h2, uff, ucv, dy, dyb, lossp = _ffn_fwd(x1, ffn_norm, wup_t, convw, conv_b, wdown, target, tm)

    du, act, d_conv_w, d_conv_b = _ffn_bwd_a(dyb, uff, ucv, convw, wdown, tm)
    d_wdown = _matmul_tn(act, dyb, FF_CHUNK, tk_ff, "dw_down")
    dx1, dx1b, d_ffn_norm = _ffn_bwd_b(du, wup_t, x1, ffn_norm, dy, tm)
    d_wup_t = _matmul_tn(du, h2, FF_CHUNK, tk_ff, "dw_up")
    late = ("w_down", "w_up", "conv_w")
    late_wire = (WIRE, WIRE, F32)
    late_full = [d_wdown.reshape(N_DEV, 352, D_MODEL), d_wup_t.reshape(N_DEV, 704, D_MODEL),
                 d_conv_w.reshape(3, N_DEV, 704).transpose(1, 0, 2)]
    token, late_pair = _symmetric_exchange("late_pair", late_full, 4, _pair_copies)
    da, db, dzg, d_b_gate, d_wout, d_win_t = _mix_bwd(dx1b, wout, g, a, b, mix, h, tm, after=token)
    late_pw, late_own = _pair_add(*late_pair(dzg), late_wire)
    token, late_chip = _symmetric_exchange("late_chip", list(late_pw), 3, _chip_copies)
    dzu, d_wpool, d_pool_scale, d_win_t = _pool_bwd(u, da, wpool, pool_scale, h, d_win_t, tw, after=token)
    dzq, dzkv, d_q_norm, d_k_norm, d_sinks = _attn_bwd(q, kv, db, tabs, q_norm_t, k_norm_t, probs, sink_probs, bd,
                                                       after=token)
    d_win_t = _matmul_tn(dzq, h, 1024, tk, "dw_in_q", into=d_win_t, row0=O_Q)
    d_win_t = _matmul_tn(dzkv, h, 256, tk, "dw_in_kv", into=d_win_t, row0=O_KV)
    early = ("w_in", "w_pool", "w_out")
    early_full = [d_win_t.reshape(N_DEV, 544, D_MODEL),
                  d_wpool.reshape(4, N_DEV, 32, POOL_GROUP).transpose(1, 0, 2, 3).reshape(N_DEV, 128, POOL_GROUP),
                  d_wout.reshape(N_DEV, 128, D_MODEL)]
    token, early_pair = _symmetric_exchange("early_pair", early_full, 4, _pair_copies)
    update(late, late_own, late_chip(token)[1])
    early_pw, early_own = _pair_add(*early_pair([done[n] for n in late]), (WIRE,) * 3)
    token, early_chip = _symmetric_exchange("early_chip", list(early_pw), 3, _chip_copies)
    grad_x, d_attn_norm = _inproj_bwd(dzu, dzq, dzkv, dzg, win_t, xs, attn_norm, dx1, tw, after=token)
    gr = dict(attn_norm=d_attn_norm, b_gate=d_b_gate, pool_scale=d_pool_scale, q_norm=d_q_norm, k_norm=d_k_norm,
              sinks=d_sinks[:, 0:N_Q_HEADS], ffn_norm=d_ffn_norm, conv_b=d_conv_b)
    small = _small_adamw(_small_allreduce(_pack_small(gr, lossp[0, 0])), *[[t[n] for n in SMALL] for t in (w, m, v)])
    loss = small[-1][0, 0]
    for j, name in enumerate(SMALL):
        out[name] = [small[kind * len(SMALL) + j] for kind in range(4)]
    update(early, early_own, early_chip(small[0])[1])

    return (loss, grad_x[None], *[out[n][0] for n in WEIGHTS], *[out[n][1] for n in WEIGHTS],
            *[out[n][2] for n in WEIGHTS], *[out[n][3] for n in WEIGHTS])
```

```python
import functools

import numpy as np
import jax
import jax.numpy as jnp
from jax import lax
from jax.experimental import pallas as pl
from jax.experimental.pallas import tpu as pltpu

F32 = jnp.float32
MM = jnp.bfloat16
WIRE = jnp.bfloat16
ACT = jnp.bfloat16

D_MODEL = 1024
D_FF = 2816
HEAD_DIM = 64
N_Q_HEADS = 16
N_KV_HEADS = 2
GQA_GROUP = 8
BLOCK = 128
ROPE_DIM = 16
ROPE_THETA = 500000.0
POOL_WINDOWS = (2, 4, 8, 16)
POOL_GROUP = 256
POOL_HALO = 32
EPS = 1e-6
NEG = -1e30
O_U, O_Q, O_KV, O_G, IN_WIDTH = 0, 1024, 2048, 2304, 4352
FF_CHUNK = 1408

ADAM_LR, ADAM_B1, ADAM_B2, ADAM_EPS, ADAM_WD, ADAM_STEP = 0.001, 0.9, 0.999, 1e-08, 0.01, 10

N_DEV = 8
LANES = 128
VMEM_LIMIT_BYTES = 56 * 1024 * 1024
MESH = pl.DeviceIdType.MESH


def _params(*sem):
    return pltpu.CompilerParams(dimension_semantics=sem, vmem_limit_bytes=VMEM_LIMIT_BYTES)


def _resident(shape):
    nd = len(shape)
    return pl.BlockSpec(shape, lambda *_: (0,) * nd, pipeline_mode=pl.Buffered(1))


def _rows(tm, width):
    return pl.BlockSpec((tm, width), lambda i: (i, 0))


def _mm(a, b):
    return jnp.dot(a.astype(MM), b.astype(MM), preferred_element_type=F32)


def _mm_nt(a, b):
    return lax.dot_general(a.astype(MM), b.astype(MM), (((1,), (1,)), ((), ())), preferred_element_type=F32)


def _mm_tn(a, b):
    return lax.dot_general(a.astype(MM), b.astype(MM), (((0,), (0,)), ((), ())), preferred_element_type=F32)


def _rmsnorm_fwd(x, g):
    r = lax.rsqrt(jnp.mean(x * x, axis=-1, keepdims=True) + EPS)
    return x * r * g, r


def _rmsnorm_bwd(x, r, g, dy):
    xn = x * r
    dxn = dy * g
    dx = r * (dxn - xn * jnp.mean(dxn * xn, axis=-1, keepdims=True))
    return dx, dy * xn


def _group_sum64(v, bd):
    hi = v.astype(MM)
    lo = (v - hi.astype(F32)).astype(MM)
    outs = []
    for t in range(v.shape[1] // LANES):
        sl = slice(LANES * t, LANES * (t + 1))
        outs.append(jnp.dot(hi[:, sl], bd, preferred_element_type=F32)
                    + jnp.dot(lo[:, sl], bd, preferred_element_type=F32))
    return outs[0] if len(outs) == 1 else jnp.concatenate(outs, axis=1)


def _head_norm_fwd(x, g, bd):
    r = lax.rsqrt(_group_sum64(x * x, bd) * (1.0 / HEAD_DIM) + EPS)
    return x * r * g, r


def _head_norm_bwd(x, r, g, dy, bd):
    xn = x * r
    dxn = dy * g
    dx = r * (dxn - xn * (_group_sum64(dxn * xn, bd) * (1.0 / HEAD_DIM)))
    return dx, dy * xn


def _rope(x, c, s1, s2):
    w = x.shape[1]
    return x * c + pltpu.roll(x, w - ROPE_DIM // 2, 1) * s1 + pltpu.roll(x, ROPE_DIM // 2, 1) * s2


def _rope_bwd(dy, c, s1, s2):
    w = dy.shape[1]
    return dy * c + pltpu.roll(dy * s1, ROPE_DIM // 2, 1) + pltpu.roll(dy * s2, w - ROPE_DIM // 2, 1)


def _tile_lanes(t, reps):
    return t if reps == 1 else jnp.concatenate([t] * reps, axis=1)


def _rope_tables(pos_col, invf):
    s = pos_col.shape[0]
    tm = min(s, 1024)

    def body(pos_ref, invf_ref, c_ref, s1_ref, s2_ref):
        ang = pos_ref[...].astype(F32) * invf_ref[...]
        lane = lax.broadcasted_iota(jnp.int32, ang.shape, 1) % HEAD_DIM
        sn = jnp.sin(ang)
        c_ref[...] = jnp.cos(ang)
        s1_ref[...] = jnp.where(lane < ROPE_DIM // 2, -sn, 0.0)
        s2_ref[...] = jnp.where((lane >= ROPE_DIM // 2) & (lane < ROPE_DIM), sn, 0.0)

    out = jax.ShapeDtypeStruct((s, LANES), F32)
    return pl.pallas_call(
        body, name="rope_tables", grid=(s // tm,), out_shape=(out, out, out),
        in_specs=[_rows(tm, 1), _resident((1, LANES))],
        out_specs=(_rows(tm, LANES),) * 3, compiler_params=_params("parallel"),
    )(pos_col, invf)


def _inproj_fwd(x, attn_norm, win_t, b_gate, tm):
    s = x.shape[0]

    def body(x_ref, gn_ref, w_ref, bg_ref, h_ref, u_ref, q_ref, kv_ref, g_ref):
        h, _ = _rmsnorm_fwd(x_ref[...], gn_ref[...])
        h = h.astype(MM)
        h_ref[...] = h
        u_ref[...] = _mm_nt(h, w_ref[O_U:O_Q, :])
        q_ref[...] = _mm_nt(h, w_ref[O_Q:O_KV, :])
        kv_ref[...] = _mm_nt(h, w_ref[O_KV:O_G, :])
        g_ref[...] = jax.nn.sigmoid(_mm_nt(h, w_ref[O_G:IN_WIDTH, :]) + bg_ref[...])

    sd = jax.ShapeDtypeStruct
    return pl.pallas_call(
        body, name="inproj_fwd", grid=(s // tm,),
        out_shape=(sd((s, D_MODEL), MM), sd((s, 1024), F32), sd((s, 1024), F32), sd((s, 256), F32),
                   sd((s, 2048), F32)),
        in_specs=[_rows(tm, D_MODEL), _resident((1, D_MODEL)), _resident((IN_WIDTH, D_MODEL)), _resident((1, 2048))],
        out_specs=(_rows(tm, D_MODEL), _rows(tm, 1024), _rows(tm, 1024), _rows(tm, 256), _rows(tm, 2048)),
        compiler_params=_params("parallel"),
    )(x, attn_norm, win_t, b_gate)


def _window_sums(src_ref, s1_ref, s2_ref, cols, w, tm, ahead):
    n = POOL_HALO + tm

    def level(src, width, lo):
        if ahead:
            return src(0, n - lo) + src(width, n - lo)
        return src(lo, n - lo) + src(lo - width, n - lo)

    def final(src, width):
        if ahead:
            return src(0, tm) + src(width, tm)
        return src(POOL_HALO, tm) + src(POOL_HALO - width, tm)

    levels = [lambda r0, rows: src_ref[pl.ds(r0, rows), cols]]
    for k, ref in ((1, s1_ref), (2, s2_ref), (3, s1_ref)):
        if w == 2 ** k:
            break
        lo = 8 * k
        ref[pl.ds(0 if ahead else lo, n - lo), :] = level(levels[-1], 2 ** (k - 1), lo)
        levels.append(functools.partial(lambda ref, r0, rows: ref[pl.ds(r0, rows), :], ref))
    return final(levels[-1], w // 2)


def _pooled(ext_ref, s1_ref, s2_ref, tm, row0):
    t = (row0 + lax.broadcasted_iota(jnp.int32, (tm, 1), 0)).astype(F32)
    out = []
    for gi, w in enumerate(POOL_WINDOWS):
        cols = slice(gi * POOL_GROUP, (gi + 1) * POOL_GROUP)
        acc = _window_sums(ext_ref, s1_ref, s2_ref, cols, w, tm, ahead=False)
        cnt = jnp.minimum(t + 1.0, float(w))
        out.append(acc / cnt - ext_ref[pl.ds(POOL_HALO, tm), cols])
    return out


def _pool_fwd(u, wpool, pool_scale, tm):
    s = u.shape[0]
    hb = tm // POOL_HALO

    def body(u_ref, halo_ref, wp_ref, ps_ref, a_ref, ext_ref, s1_ref, s2_ref):
        i = pl.program_id(0)
        ext_ref[pl.ds(0, POOL_HALO), :] = jnp.where(i > 0, halo_ref[...], 0.0)
        ext_ref[pl.ds(POOL_HALO, tm), :] = u_ref[...]
        pooled = _pooled(ext_ref, s1_ref, s2_ref, tm, i * tm)
        for gi in range(4):
            cols = slice(gi * POOL_GROUP, (gi + 1) * POOL_GROUP)
            a_ref[:, cols] = (_mm(pooled[gi], wp_ref[gi]) * ps_ref[:, cols]).astype(a_ref.dtype)

    return pl.pallas_call(
        body, name="pool_fwd", grid=(s // tm,), out_shape=jax.ShapeDtypeStruct((s, 1024), ACT),
        in_specs=[_rows(tm, 1024), pl.BlockSpec((POOL_HALO, 1024), lambda i: (jnp.maximum(i * hb - 1, 0), 0)),
                  _resident((4, POOL_GROUP, POOL_GROUP)), _resident((1, 1024))],
        out_specs=_rows(tm, 1024),
        scratch_shapes=[pltpu.VMEM((POOL_HALO + tm, 1024), F32), pltpu.VMEM((POOL_HALO + tm, POOL_GROUP), F32),
                        pltpu.VMEM((POOL_HALO + tm, POOL_GROUP), F32)],
        compiler_params=_params("parallel"),
    )(u, u, wpool, pool_scale)


PAIRS = GQA_GROUP // 2
PAIR_COLS = PAIRS * BLOCK


def _fold_masks(n):
    r = lax.broadcasted_iota(jnp.int32, (BLOCK, BLOCK), 0)
    i = lax.broadcasted_iota(jnp.int32, (BLOCK, BLOCK), 1)
    prev = r > i
    return prev, jnp.where(prev & (n == 0), NEG, 0.0)


def _fold(band, prev):
    return jnp.where(prev, band[0:BLOCK, :], band[BLOCK:2 * BLOCK, :])


def _unfold(folded, prev):
    top = jnp.where(prev, folded, 0.0)
    return jnp.concatenate([top, folded - top], axis=0).astype(MM)


def _probs_by_pair(sc, sink, masks):
    prev, bias = masks
    out = []
    for j in range(PAIRS):
        cols = slice(j * BLOCK, (j + 1) * BLOCK)
        out.append(_softmax_sink_t(_fold(sc[:, cols], prev) + bias, sink[:, cols]))
    return out


def _stack_pairs(x, hk):
    return jnp.concatenate([x[:, (PAIRS * hk + j) * LANES:(PAIRS * hk + j + 1) * LANES] for j in range(PAIRS)], axis=0)


def _parity_bands(t, hk):
    low = lax.broadcasted_iota(jnp.int32, t.shape, 1) < HEAD_DIM
    own = jnp.where(low if hk == 0 else ~low, t, 0.0)
    other = pltpu.roll(own, HEAD_DIM, 1)
    return (own, other) if hk == 0 else (other, own)


def _fold_parity(even, odd, hk):
    low = lax.broadcasted_iota(jnp.int32, even.shape, 1) < HEAD_DIM
    comb = jnp.where(low, even, odd)
    comb = comb + pltpu.roll(comb, HEAD_DIM, 1)
    return jnp.where(low if hk == 0 else ~low, comb, 0.0)


def _softmax_sink_t(s, sink):
    m = jnp.maximum(jnp.max(s, axis=0, keepdims=True), sink)
    p = jnp.exp(s - m)
    es = jnp.exp(sink - m)
    inv = 1.0 / (jnp.sum(p, axis=0, keepdims=True) + es)
    return p * inv, es * inv


def _attn_fwd(q, kv, tabs, q_norm_t, k_norm_t, sink_rows, bd):
    s = q.shape[0]
    nb = s // BLOCK
    scale = HEAD_DIM ** -0.5
    cur = lambda n: (n, 0)
    prv = lambda n: (jnp.maximum(2 * n - 1, 0), 0)

    def body(q_ref, kvc_ref, kvp_ref, c_ref, s1_ref, s2_ref, cp_ref, s1p_ref, s2p_ref, qn_ref, kn_ref, sink_ref,
             bd_ref, o_ref, pr_ref, ps_ref):
        n = pl.program_id(0)
        bdm = bd_ref[...]
        c, s1, s2 = c_ref[...], s1_ref[...], s2_ref[...]
        qh, _ = _head_norm_fwd(q_ref[...], qn_ref[...], bdm)
        qr = (_rope(qh, _tile_lanes(c, 8), _tile_lanes(s1, 8), _tile_lanes(s2, 8)) * scale).astype(MM)
        kc, _ = _head_norm_fwd(kvc_ref[:, 0:128], kn_ref[...], bdm)
        kp, _ = _head_norm_fwd(kvp_ref[:, 0:128], kn_ref[...], bdm)
        k3 = jnp.concatenate([_rope(kp, cp_ref[...], s1p_ref[...], s2p_ref[...]), _rope(kc, c, s1, s2)], axis=0)
        v3 = jnp.concatenate([kvp_ref[:, 128:256], kvc_ref[:, 128:256]], axis=0)
        for sub in range(2):
            rows = slice(sub * BLOCK, (sub + 1) * BLOCK)
            k2, v2 = k3[sub * BLOCK:(sub + 2) * BLOCK, :], v3[sub * BLOCK:(sub + 2) * BLOCK, :]
            mask = _fold_masks(2 * n + sub)
            for hk in range(N_KV_HEADS):
                qs = _stack_pairs(qr[rows, :], hk)
                ot = jnp.zeros((LANES, PAIR_COLS), F32)
                for par, (kb, vb) in enumerate(zip(_parity_bands(k2, hk), _parity_bands(v2, hk))):
                    probs = _probs_by_pair(_mm_nt(kb, qs), sink_ref[hk, par], mask)
                    pr_ref[sub, 2 * hk + par] = jnp.concatenate([pr for pr, _ in probs], axis=1).astype(MM)
                    ps_ref[sub, 2 * hk + par] = jnp.concatenate([ps for _, ps in probs], axis=1)
                    ot = ot + _mm(vb.T, jnp.concatenate([_unfold(pr, mask[0]) for pr, _ in probs], axis=1))
                for j in range(PAIRS):
                    col = (PAIRS * hk + j) * LANES
                    o_ref[rows, col:col + LANES] = ot[:, j * BLOCK:(j + 1) * BLOCK].T.astype(o_ref.dtype)

    two = 2 * BLOCK
    sd = jax.ShapeDtypeStruct
    return pl.pallas_call(
        body, name="attn_fwd", grid=(nb // 2,),
        out_shape=(sd((s, 1024), ACT), sd((nb, 4, BLOCK, PAIR_COLS), MM), sd((nb, 4, 1, PAIR_COLS), F32)),
        in_specs=[pl.BlockSpec((two, 1024), cur), pl.BlockSpec((two, 256), cur), pl.BlockSpec((BLOCK, 256), prv),
                  pl.BlockSpec((two, LANES), cur), pl.BlockSpec((two, LANES), cur), pl.BlockSpec((two, LANES), cur),
                  pl.BlockSpec((BLOCK, LANES), prv), pl.BlockSpec((BLOCK, LANES), prv), pl.BlockSpec((BLOCK, LANES), prv),
                  _resident((1, 1024)), _resident((1, 128)), _resident((N_KV_HEADS, 2, 1, PAIR_COLS)),
                  _resident((LANES, LANES))],
        out_specs=(pl.BlockSpec((two, 1024), cur), pl.BlockSpec((2, 4, BLOCK, PAIR_COLS), lambda n: (n, 0, 0, 0)),
                   pl.BlockSpec((2, 4, 1, PAIR_COLS), lambda n: (n, 0, 0, 0))),
        compiler_params=_params("parallel"),
    )(q, kv, kv, *tabs, *tabs, q_norm_t, k_norm_t, sink_rows, bd)


def _mix_out_fwd(x, g, a, b, wout, tm):
    s = x.shape[0]

    def body(x_ref, g_ref, a_ref, b_ref, w_ref, x1_ref, mix_ref):
        mix = (g_ref[:, 0:1024] * a_ref[...] + g_ref[:, 1024:2048] * b_ref[...]).astype(MM)
        mix_ref[...] = mix
        x1_ref[...] = x_ref[...] + _mm(mix, w_ref[...])

    return pl.pallas_call(
        body, name="mix_out_fwd", grid=(s // tm,),
        out_shape=(jax.ShapeDtypeStruct((s, D_MODEL), F32), jax.ShapeDtypeStruct((s, D_MODEL), MM)),
        in_specs=[_rows(tm, 1024), _rows(tm, 2048), _rows(tm, 1024), _rows(tm, 1024), _resident((1024, 1024))],
        out_specs=(_rows(tm, 1024), _rows(tm, 1024)), compiler_params=_params("parallel"),
    )(x, g, a, b, wout)


SHIFT_ROWS = 16


def _sublane_major_matrices(tm):
    r = np.arange(tm)
    pm = r[None, :] == ((tm // 8) * (r % 8) + r // 8)[:, None]
    return jnp.asarray(pm, MM), jnp.asarray(pm.T, MM)


def _to_sublane_major(pm, v):
    return jnp.dot(pm, v, preferred_element_type=F32).astype(MM)


def _to_time_order(pmt, v):
    hi = v.astype(MM)
    r1 = v - hi.astype(F32)
    mid = r1.astype(MM)
    lo = (r1 - mid.astype(F32)).astype(MM)
    dot = functools.partial(jnp.dot, preferred_element_type=F32)
    return dot(pmt, hi) + dot(pmt, mid) + dot(pmt, lo)


def _step_back(vreg_rows, before):
    sub = lax.broadcasted_iota(jnp.int32, vreg_rows.shape, 0)
    return jnp.where(sub == 0, before[7:8, :], pltpu.roll(vreg_rows, 1, 0))


def _step_ahead(vreg_rows, after):
    sub = lax.broadcasted_iota(jnp.int32, vreg_rows.shape, 0)
    return jnp.where(sub == 7, after[0:1, :], pltpu.roll(vreg_rows, 7, 0))


def _fill_back_rows(ext_ref, before, tm, cols):
    last = ext_ref[pl.ds(SHIFT_ROWS + tm - 8, 8), cols]
    pen = ext_ref[pl.ds(SHIFT_ROWS + tm - 16, 8), cols]
    ext_ref[pl.ds(8, 8), cols] = _step_back(last, before[8:16, :])
    ext_ref[pl.ds(0, 8), cols] = _step_back(pen, before[0:8, :])


def _causal_conv(uc, before, w, b, tm):
    back1 = _step_back(uc[tm - 8:tm, :], before[8:16, :])
    back2 = _step_back(uc[tm - 16:tm - 8, :], before[0:8, :])
    w0, w1, w2 = w[0:1, :], w[1:2, :], w[2:3, :]
    first = b + w0 * back2 + w1 * back1 + w2 * uc[0:8, :]
    second = b + w0 * back1 + w1 * uc[0:8, :] + w2 * uc[8:16, :]
    rest = b + w0 * uc[0:tm - 16, :] + w1 * uc[8:tm - 8, :] + w2 * uc[16:tm, :]
    return jnp.concatenate([first, second, rest], axis=0)


def _ffn_fwd(x1, ffn_norm, wup_t, conv_w, conv_b, wdown, target, tm):
    s = x1.shape[0]
    inv_d = 1.0 / D_MODEL
    pm, pmt = _sublane_major_matrices(tm)

    def body(x1_ref, gn_ref, wu_ref, cw_ref, cb_ref, wd_ref, tgt_ref, pm_ref, pmt_ref, h2_ref, u_ref, uc_ref, dy_ref,
             dyb_ref, loss_ref, carry_ref):
        i = pl.program_id(0)

        @pl.when(i == 0)
        def _():
            carry_ref[...] = jnp.zeros_like(carry_ref)
            loss_ref[...] = jnp.zeros_like(loss_ref)

        x1 = x1_ref[...]
        h2, _ = _rmsnorm_fwd(x1, gn_ref[...])
        h2 = _to_sublane_major(pm_ref[...], h2.astype(MM))
        h2_ref[...] = h2
        for c in range(4):
            cols = slice(c * FF_CHUNK, (c + 1) * FF_CHUNK)
            uc = _mm_nt(h2, wu_ref[cols, :])
            u_ref[:, cols] = uc
            uc_ref[:, cols] = _causal_conv(uc, carry_ref[:, cols], cw_ref[:, cols], cb_ref[:, cols], tm)
            carry_ref[:, cols] = uc[tm - SHIFT_ROWS:tm, :]
        down = jnp.zeros((tm, D_MODEL), F32)
        for c in range(2):
            gate = uc_ref[:, c * FF_CHUNK:(c + 1) * FF_CHUNK]
            val = uc_ref[:, D_FF + c * FF_CHUNK:D_FF + (c + 1) * FF_CHUNK]
            act = gate * jax.nn.sigmoid(gate) * val
            down = down + _mm(act, wd_ref[c * FF_CHUNK:(c + 1) * FF_CHUNK, :])
        err = x1 + _to_time_order(pmt_ref[...], down) - tgt_ref[...]
        loss_ref[...] += jnp.full(loss_ref.shape, 0.5 * inv_d * jnp.sum(err * err), F32)
        dy = err * inv_d
        dy_ref[...] = dy
        dyb_ref[...] = _to_sublane_major(pm_ref[...], dy.astype(MM))

    sd = jax.ShapeDtypeStruct
    return pl.pallas_call(
        body, name="ffn_fwd", grid=(s // tm,),
        out_shape=(sd((s, D_MODEL), MM), sd((s, 2 * D_FF), F32), sd((s, 2 * D_FF), F32), sd((s, D_MODEL), F32),
                   sd((s, D_MODEL), MM), sd((8, LANES), F32)),
        in_specs=[_rows(tm, 1024), _resident((1, 1024)), _resident((2 * D_FF, D_MODEL)), _resident((3, 2 * D_FF)),
                  _resident((1, 2 * D_FF)), _resident((D_FF, D_MODEL)), _rows(tm, 1024), _resident((tm, tm)),
                  _resident((tm, tm))],
        out_specs=(_rows(tm, 1024), _rows(tm, 2 * D_FF), _rows(tm, 2 * D_FF), _rows(tm, 1024), _rows(tm, 1024),
                   pl.BlockSpec((8, LANES), lambda i: (0, 0))),
        scratch_shapes=[pltpu.VMEM((SHIFT_ROWS, 2 * D_FF), F32)],
        compiler_params=_params("arbitrary"),
    )(x1, ffn_norm, wup_t, conv_w, conv_b, wdown, target, pm, pmt)


def _ffn_bwd_a(dyb, u, uc, conv_w, wdown, tm):
    s = dyb.shape[0]
    nt = s // tm
    hb = tm // SHIFT_ROWS
    rev = lambda i: (nt - 1 - i, 0)

    def body(dy_ref, u_ref, before_ref, uc_ref, cw_ref, wd_ref, du_ref, act_ref, dcw_ref, dcb_ref, ext_ref, extd_ref,
             ahead_ref):
        i = pl.program_id(0)
        first_tile = i == nt - 1

        @pl.when(i == 0)
        def _():
            ahead_ref[...] = jnp.zeros_like(ahead_ref)
            dcw_ref[...] = jnp.zeros_like(dcw_ref)
            dcb_ref[...] = jnp.zeros_like(dcb_ref)

        ext_ref[pl.ds(SHIFT_ROWS, tm), :] = u_ref[...]
        for c in range(4):
            cols = slice(c * FF_CHUNK, (c + 1) * FF_CHUNK)
            _fill_back_rows(ext_ref, jnp.where(first_tile, 0.0, before_ref[:, cols]), tm, cols)
        dy = dy_ref[...]
        for c in range(2):
            gate = uc_ref[:, c * FF_CHUNK:(c + 1) * FF_CHUNK]
            val = uc_ref[:, D_FF + c * FF_CHUNK:D_FF + (c + 1) * FF_CHUNK]
            sg = jax.nn.sigmoid(gate)
            sl = gate * sg
            act_ref[:, c * FF_CHUNK:(c + 1) * FF_CHUNK] = (sl * val).astype(MM)
            d_act = _mm_nt(dy, wd_ref[c * FF_CHUNK:(c + 1) * FF_CHUNK, :])
            extd_ref[pl.ds(0, tm), c * FF_CHUNK:(c + 1) * FF_CHUNK] = d_act * val * (sg + sl * (1.0 - sg))
            extd_ref[pl.ds(0, tm), D_FF + c * FF_CHUNK:D_FF + (c + 1) * FF_CHUNK] = d_act * sl
        for c in range(4):
            cols = slice(c * FF_CHUNK, (c + 1) * FF_CHUNK)
            ahead = ahead_ref[:, cols]
            first2 = extd_ref[pl.ds(0, SHIFT_ROWS), cols]
            extd_ref[pl.ds(tm, 8), cols] = _step_ahead(first2[0:8, :], ahead[0:8, :])
            extd_ref[pl.ds(tm + 8, 8), cols] = _step_ahead(first2[8:16, :], ahead[8:16, :])
            ahead_ref[:, cols] = first2
            d0 = extd_ref[pl.ds(0, tm), cols]
            dcb_ref[:, cols] += jnp.sum(d0, axis=0, keepdims=True)
            for j in range(3):
                dcw_ref[j:j + 1, cols] += jnp.sum(d0 * ext_ref[pl.ds(8 * j, tm), cols], axis=0, keepdims=True)
            du = cw_ref[2:3, cols] * d0 + cw_ref[1:2, cols] * extd_ref[pl.ds(8, tm), cols]
            du = du + cw_ref[0:1, cols] * extd_ref[pl.ds(SHIFT_ROWS, tm), cols]
            du_ref[:, cols] = du.astype(MM)

    sd = jax.ShapeDtypeStruct
    return pl.pallas_call(
        body, name="ffn_bwd_a", grid=(nt,),
        out_shape=(sd((s, 2 * D_FF), MM), sd((s, D_FF), MM), sd((3, 2 * D_FF), F32), sd((1, 2 * D_FF), F32)),
        in_specs=[pl.BlockSpec((tm, D_MODEL), rev), pl.BlockSpec((tm, 2 * D_FF), rev),
                  pl.BlockSpec((SHIFT_ROWS, 2 * D_FF), lambda i: (jnp.maximum((nt - 1 - i) * hb - 1, 0), 0)),
                  pl.BlockSpec((tm, 2 * D_FF), rev), _resident((3, 2 * D_FF)), _resident((D_FF, D_MODEL))],
        out_specs=(pl.BlockSpec((tm, 2 * D_FF), rev), pl.BlockSpec((tm, D_FF), rev),
                   pl.BlockSpec((3, 2 * D_FF), lambda i: (0, 0)), pl.BlockSpec((1, 2 * D_FF), lambda i: (0, 0))),
        scratch_shapes=[pltpu.VMEM((SHIFT_ROWS + tm, 2 * D_FF), F32), pltpu.VMEM((tm + SHIFT_ROWS, 2 * D_FF), F32),
                        pltpu.VMEM((SHIFT_ROWS, 2 * D_FF), F32)],
        compiler_params=_params("arbitrary"),
    )(dyb, u, u, uc, conv_w, wdown)


def _after(after):
    tie = [] if after is None else list(after) if isinstance(after, (list, tuple)) else [after]
    return tie, [pl.BlockSpec(memory_space=pl.ANY)] * len(tie)


def _matmul_tn(a, b, tmo, tk, name, into=None, row0=0):
    s, m = a.shape
    n = b.shape[1]
    nk = s // tk
    rows = m if into is None else into.shape[0]
    assert row0 % LANES == 0 and tmo % LANES == 0
    grown, grown_spec = ([], []) if into is None else ([into], [ANY])

    def body(a_ref, b_ref, *rest):
        o_ref = rest[-1]
        k = pl.program_id(1)

        @pl.when(k == 0)
        def _():
            o_ref[...] = jnp.zeros_like(o_ref)

        o_ref[...] += _mm_tn(a_ref[...], b_ref[pl.ds(pl.multiple_of(k * tk, tk), tk), :])

    return pl.pallas_call(
        body, name=name, grid=(m // tmo, nk), out_shape=jax.ShapeDtypeStruct((rows, n), F32),
        in_specs=[pl.BlockSpec((tk, tmo), lambda i, k: (k, i)), _resident((s, n))] + grown_spec,
        out_specs=pl.BlockSpec((pl.Element(tmo), pl.Element(n)), lambda i, k: (pl.multiple_of(row0 + i * tmo, LANES), 0)),
        input_output_aliases={2: 0} if grown else {},
        compiler_params=_params("parallel", "arbitrary"),
    )(a, b, *grown)


def _ffn_bwd_b(du, wup_t, x1, ffn_norm, dy, tm):
    s = du.shape[0]

    def body(du_ref, wu_ref, x1_ref, gn_ref, dy_ref, pmt_ref, dx1_ref, dx1b_ref, dg_ref):
        @pl.when(pl.program_id(0) == 0)
        def _():
            dg_ref[...] = jnp.zeros_like(dg_ref)

        dh2 = _to_time_order(pmt_ref[...], _mm(du_ref[...], wu_ref[...]))
        x1 = x1_ref[...]
        _, r = _rmsnorm_fwd(x1, gn_ref[...])
        dx, dgr = _rmsnorm_bwd(x1, r, gn_ref[...], dh2)
        dg_ref[...] += jnp.sum(dgr, axis=0, keepdims=True)
        dx1 = dy_ref[...] + dx
        dx1_ref[...] = dx1
        dx1b_ref[...] = dx1.astype(MM)

    return pl.pallas_call(
        body, name="ffn_bwd_b", grid=(s // tm,),
        out_shape=(jax.ShapeDtypeStruct((s, D_MODEL), F32), jax.ShapeDtypeStruct((s, D_MODEL), MM),
                   jax.ShapeDtypeStruct((1, D_MODEL), F32)),
        in_specs=[_rows(tm, 2 * D_FF), _resident((2 * D_FF, D_MODEL)), _rows(tm, 1024), _resident((1, 1024)),
                  _rows(tm, 1024), _resident((tm, tm))],
        out_specs=(_rows(tm, 1024), _rows(tm, 1024), pl.BlockSpec((1, D_MODEL), lambda i: (0, 0))),
        compiler_params=_params("arbitrary"),
    )(du, wup_t, x1, ffn_norm, dy, _sublane_major_matrices(tm)[1])


def _win_rows(row0, rows):
    return pl.BlockSpec((pl.Element(rows), pl.Element(D_MODEL)), lambda i: (row0, 0), pipeline_mode=pl.Buffered(1))


def _mix_bwd(dx1b, wout, g, a, b, mix, h, tm, after=None):
    s = dx1b.shape[0]
    tie, tie_spec = _after(after)

    def body(dx_ref, w_ref, g_ref, a_ref, b_ref, mix_ref, h_ref, *rest):
        da_ref, db_ref, dzg_ref, dbg_ref, dwo_ref, dwin_ref = rest[-6:]

        @pl.when(pl.program_id(0) == 0)
        def _():
            dbg_ref[...] = jnp.zeros_like(dbg_ref)
            dwo_ref[...] = jnp.zeros_like(dwo_ref)
            dwin_ref[...] = jnp.zeros_like(dwin_ref)

        dx = dx_ref[...]
        dwo_ref[...] += _mm_tn(mix_ref[...], dx)
        dmix = _mm_nt(dx, w_ref[...])
        for half, src, dst in ((0, a_ref, da_ref), (1, b_ref, db_ref)):
            cols = slice(half * 1024, (half + 1) * 1024)
            gt = g_ref[:, cols]
            dst[...] = (dmix * gt).astype(dst.dtype)
            dz = dmix * src[...] * gt * (1.0 - gt)
            dzb = dz.astype(MM)
            dzg_ref[:, cols] = dzb
            dwin_ref[cols, :] += _mm_tn(dzb, h_ref[...])
            dbg_ref[:, cols] += jnp.sum(dz, axis=0, keepdims=True)

    sd = jax.ShapeDtypeStruct
    return pl.pallas_call(
        body, name="mix_bwd", grid=(s // tm,),
        out_shape=(sd((s, 1024), F32), sd((s, 1024), MM), sd((s, 2048), MM), sd((1, 2048), F32),
                   sd((D_MODEL, D_MODEL), F32), sd((IN_WIDTH, D_MODEL), F32)),
        in_specs=[_rows(tm, 1024), _resident((1024, 1024)), _rows(tm, 2048), _rows(tm, 1024), _rows(tm, 1024),
                  _rows(tm, 1024), _rows(tm, 1024)] + tie_spec,
        out_specs=(_rows(tm, 1024), _rows(tm, 1024), _rows(tm, 2048), pl.BlockSpec((1, 2048), lambda i: (0, 0)),
                   _resident((D_MODEL, D_MODEL)), _win_rows(O_G, IN_WIDTH - O_G)),
        compiler_params=_params("arbitrary"),
    )(dx1b, wout, g, a, b, mix, h, *tie)


def _pool_bwd(u, da, wpool, pool_scale, h, d_win_t, tm, after=None):
    s = u.shape[0]
    nt = s // tm
    hb = tm // POOL_HALO

    tie, tie_spec = _after(after)

    def body(u_ref, uh_ref, da_ref, dah_ref, wp_ref, ps_ref, h_ref, *rest):
        dzu_ref, dwp_ref, dps_ref, dwin_ref, ext_ref, exte_ref, s1_ref, s2_ref = rest[-8:]
        i = pl.program_id(0)

        @pl.when(i == 0)
        def _():
            dwp_ref[...] = jnp.zeros_like(dwp_ref)
            dps_ref[...] = jnp.zeros_like(dps_ref)
            dwin_ref[...] = jnp.zeros_like(dwin_ref)

        ext_ref[pl.ds(0, POOL_HALO), :] = jnp.where(i > 0, uh_ref[...], 0.0)
        ext_ref[pl.ds(POOL_HALO, tm), :] = u_ref[...]
        pooled = _pooled(ext_ref, s1_ref, s2_ref, tm, i * tm)
        da = da_ref[...]
        dah = jnp.where(i < nt - 1, dah_ref[...], 0.0)
        t = (i * tm + lax.broadcasted_iota(jnp.int32, (tm + POOL_HALO, 1), 0)).astype(F32)
        for gi, w in enumerate(POOL_WINDOWS):
            cols = slice(gi * POOL_GROUP, (gi + 1) * POOL_GROUP)
            pg = pooled[gi].astype(MM)
            wg = wp_ref[gi]
            mixed = _mm(pg, wg)
            dps_ref[:, cols] += jnp.sum(da[:, cols] * mixed, axis=0, keepdims=True)
            dmx = (da[:, cols] * ps_ref[:, cols]).astype(MM)
            dwp_ref[gi] += _mm_tn(pg, dmx)
            dpl = _mm_nt(dmx, wg)
            dplh = _mm_nt(dah[:, cols] * ps_ref[:, cols], wg)
            cnt = jnp.minimum(t + 1.0, float(w))
            exte_ref[pl.ds(0, tm), cols] = dpl / cnt[0:tm]
            exte_ref[pl.ds(tm, POOL_HALO), cols] = dplh / cnt[tm:tm + POOL_HALO]
            acc = _window_sums(exte_ref, s1_ref, s2_ref, cols, w, tm, ahead=True)
            dzu = (acc - dpl).astype(MM)
            dzu_ref[:, cols] = dzu
            dwin_ref[cols, :] += _mm_tn(dzu, h_ref[...])

    sd = jax.ShapeDtypeStruct
    last_halo = s // POOL_HALO - 1
    return pl.pallas_call(
        body, name="pool_bwd", grid=(nt,),
        out_shape=(sd((s, 1024), MM), sd((4, POOL_GROUP, POOL_GROUP), F32), sd((1, 1024), F32),
                   sd((IN_WIDTH, D_MODEL), F32)),
        in_specs=[_rows(tm, 1024), pl.BlockSpec((POOL_HALO, 1024), lambda i: (jnp.maximum(i * hb - 1, 0), 0)),
                  _rows(tm, 1024),
                  pl.BlockSpec((POOL_HALO, 1024), lambda i: (jnp.minimum((i + 1) * hb, last_halo), 0)),
                  _resident((4, POOL_GROUP, POOL_GROUP)), _resident((1, 1024)), _rows(tm, 1024)] + tie_spec + [ANY],
        out_specs=(_rows(tm, 1024), pl.BlockSpec((4, POOL_GROUP, POOL_GROUP), lambda i: (0, 0, 0)),
                   pl.BlockSpec((1, 1024), lambda i: (0, 0)), _win_rows(O_U, O_Q - O_U)),
        input_output_aliases={7 + len(tie): 3},
        scratch_shapes=[pltpu.VMEM((POOL_HALO + tm, 1024), F32), pltpu.VMEM((tm + POOL_HALO, 1024), F32),
                        pltpu.VMEM((POOL_HALO + tm, POOL_GROUP), F32), pltpu.VMEM((POOL_HALO + tm, POOL_GROUP), F32)],
        compiler_params=_params("arbitrary"),
    )(u, u, da, da, wpool, pool_scale, h, *tie, d_win_t)


def _attn_bwd(q, kv, db, tabs, q_norm_t, k_norm_t, probs, sink_probs, bd, after=None):
    s = q.shape[0]
    nb = s // BLOCK
    scale = HEAD_DIM ** -0.5
    steps = nb // 2
    cur = lambda n: (jnp.minimum(n, steps - 1), 0)
    prv = lambda n: (jnp.maximum(n - 1, 0), 0)
    tie, tie_spec = _after(after)

    def body(q_ref, kvc_ref, kvp_ref, db_ref, c_ref, s1_ref, s2_ref, cp_ref, s1p_ref, s2p_ref, qn_ref, kn_ref,
             pr_ref, ps_ref, bd_ref, *rest):
        (dzq_ref, dzkv_ref, dqn_ref, dkn_ref, dsk_ref,
         carry_ref, tot_ref, dqr_ref, qacc_ref, kacc_ref, sacc_ref) = rest[-11:]
        n = pl.program_id(0)
        bdm = bd_ref[...]
        kn = kn_ref[...]

        @pl.when(n == 0)
        def _():
            carry_ref[...] = jnp.zeros_like(carry_ref)
            qacc_ref[...] = jnp.zeros_like(qacc_ref)
            kacc_ref[...] = jnp.zeros_like(kacc_ref)
            sacc_ref[...] = jnp.zeros_like(sacc_ref)

        kp_raw = kvp_ref[:, 0:128]
        kph, rp = _head_norm_fwd(kp_raw, kn, bdm)
        cp, s1p, s2p = cp_ref[...], s1p_ref[...], s2p_ref[...]

        @pl.when(n < steps)
        def _():
            c, s1, s2 = c_ref[...], s1_ref[...], s2_ref[...]
            c8, s18, s28 = _tile_lanes(c, 8), _tile_lanes(s1, 8), _tile_lanes(s2, 8)
            q_raw = q_ref[...]
            qh, rq = _head_norm_fwd(q_raw, qn_ref[...], bdm)
            qr = (_rope(qh, c8, s18, s28) * scale).astype(MM)
            kc, _ = _head_norm_fwd(kvc_ref[:, 0:128], kn, bdm)
            last = slice(BLOCK, 2 * BLOCK)
            k3 = jnp.concatenate([_rope(kph[last, :], cp[last, :], s1p[last, :], s2p[last, :]), _rope(kc, c, s1, s2)],
                                 axis=0)
            v3 = jnp.concatenate([kvp_ref[last, 128:256], kvc_ref[:, 128:256]], axis=0)
            dob = db_ref[...].astype(MM)
            lane = lax.broadcasted_iota(jnp.int32, (1, LANES), 1)
            dsk = jnp.zeros((1, LANES), F32)
            parts = []
            for sub in range(2):
                rows = slice(sub * BLOCK, (sub + 1) * BLOCK)
                k2, v2 = k3[sub * BLOCK:(sub + 2) * BLOCK, :], v3[sub * BLOCK:(sub + 2) * BLOCK, :]
                mask = _fold_masks(2 * n + sub)
                dk2 = jnp.zeros((2 * BLOCK, LANES), F32)
                dv2 = jnp.zeros((2 * BLOCK, LANES), F32)
                for hk in range(N_KV_HEADS):
                    qs = _stack_pairs(qr[rows, :], hk)
                    do = _stack_pairs(dob[rows, :], hk)
                    dqt = jnp.zeros((LANES, PAIR_COLS), F32)
                    dkb, dvb = [], []
                    for par, (kb, vb) in enumerate(zip(_parity_bands(k2, hk), _parity_bands(v2, hk))):
                        dp = _mm_nt(vb, do)
                        prs, dss = [], []
                        for j in range(PAIRS):
                            cols = slice(j * BLOCK, (j + 1) * BLOCK)
                            pr = pr_ref[sub, 2 * hk + par, :, cols].astype(F32)
                            psink = ps_ref[sub, 2 * hk + par, :, cols]
                            dpj = _fold(dp[:, cols], mask[0])
                            coldot = jnp.sum(pr * dpj, axis=0, keepdims=True)
                            dss.append(_unfold(pr * (dpj - coldot), mask[0]))
                            prs.append(_unfold(pr, mask[0]))
                            h = hk * GQA_GROUP + 2 * j + par
                            dsk = dsk + jnp.where(lane == h, jnp.sum(-psink * coldot), 0.0)
                        ds, pr = jnp.concatenate(dss, axis=1), jnp.concatenate(prs, axis=1)
                        dqt = dqt + _mm(kb.T, ds)
                        dkb.append(_mm(ds, qs))
                        dvb.append(_mm(pr, do))
                    for j in range(PAIRS):
                        col = (PAIRS * hk + j) * LANES
                        dqr_ref[rows, col:col + LANES] = dqt[:, j * BLOCK:(j + 1) * BLOCK].T
                    dk2 = dk2 + _fold_parity(dkb[0], dkb[1], hk)
                    dv2 = dv2 + _fold_parity(dvb[0], dvb[1], hk)
                parts.append((dk2, dv2))
            first, second = slice(0, BLOCK), slice(BLOCK, 2 * BLOCK)
            for cols, (even, odd) in ((slice(0, 128), (parts[0][0], parts[1][0])),
                                      (slice(128, 256), (parts[0][1], parts[1][1]))):
                tot_ref[first, cols] = carry_ref[first, cols]
                tot_ref[second, cols] = carry_ref[second, cols] + even[first, :]
                carry_ref[first, cols] = even[second, :] + odd[first, :]
                carry_ref[second, cols] = odd[second, :]
            sacc_ref[...] += dsk
            dqh = _rope_bwd(dqr_ref[...] * scale, c8, s18, s28)
            dq, dgq = _head_norm_bwd(q_raw, rq, qn_ref[...], dqh, bdm)
            dzq_ref[...] = dq.astype(MM)
            qacc_ref[...] += jnp.sum(dgq, axis=0, keepdims=True)

        @pl.when(n == steps)
        def _():
            tot_ref[...] = carry_ref[...]

        dkh = _rope_bwd(tot_ref[:, 0:128], cp, s1p, s2p)
        dkr, dgk = _head_norm_bwd(kp_raw, rp, kn, dkh, bdm)
        dzkv_ref[:, 0:128] = dkr.astype(MM)
        dzkv_ref[:, 128:256] = tot_ref[:, 128:256].astype(MM)
        kacc_ref[...] += jnp.where(n > 0, jnp.sum(dgk, axis=0, keepdims=True), 0.0)

        @pl.when(n == steps)
        def _():
            fold = qacc_ref[:, 0:HEAD_DIM]
            for h in range(1, N_Q_HEADS):
                fold = fold + qacc_ref[:, h * HEAD_DIM:(h + 1) * HEAD_DIM]
            dqn_ref[...] = fold
            dkn_ref[...] = kacc_ref[:, 0:HEAD_DIM] + kacc_ref[:, HEAD_DIM:2 * HEAD_DIM]
            dsk_ref[...] = sacc_ref[...]

    two = 2 * BLOCK
    tab = lambda im: pl.BlockSpec((two, LANES), im)
    sd = jax.ShapeDtypeStruct
    const = lambda n: (0, 0)
    return pl.pallas_call(
        body, name="attn_bwd", grid=(steps + 1,),
        out_shape=(sd((s, 1024), MM), sd((s, 256), MM), sd((1, HEAD_DIM), F32), sd((1, HEAD_DIM), F32),
                   sd((1, LANES), F32)),
        in_specs=[pl.BlockSpec((two, 1024), cur), pl.BlockSpec((two, 256), cur), pl.BlockSpec((two, 256), prv),
                  pl.BlockSpec((two, 1024), cur), tab(cur), tab(cur), tab(cur), tab(prv), tab(prv), tab(prv),
                  _resident((1, 1024)), _resident((1, 128)),
                  pl.BlockSpec((2, 4, BLOCK, PAIR_COLS), lambda n: (jnp.minimum(n, steps - 1), 0, 0, 0)),
                  pl.BlockSpec((2, 4, 1, PAIR_COLS), lambda n: (jnp.minimum(n, steps - 1), 0, 0, 0)),
                  _resident((LANES, LANES))] + tie_spec,
        out_specs=(pl.BlockSpec((two, 1024), cur), pl.BlockSpec((two, 256), prv),
                   pl.BlockSpec((1, HEAD_DIM), const), pl.BlockSpec((1, HEAD_DIM), const),
                   pl.BlockSpec((1, LANES), const)),
        scratch_shapes=[pltpu.VMEM((two, 256), F32), pltpu.VMEM((two, 256), F32), pltpu.VMEM((two, 1024), F32),
                        pltpu.VMEM((1, 1024), F32), pltpu.VMEM((1, 128), F32), pltpu.VMEM((1, LANES), F32)],
        compiler_params=_params("arbitrary"),
    )(q, kv, kv, db, *tabs, *tabs, q_norm_t, k_norm_t, probs, sink_probs, bd, *tie)


def _inproj_bwd(dzu, dzq, dzkv, dzg, win_t, x, attn_norm, dx1, tm, after=None):
    s = x.shape[0]
    tie, tie_spec = _after(after)

    def body(du_ref, dq_ref, dkv_ref, dg_ref, w_ref, x_ref, gn_ref, dx1_ref, *rest):
        gx_ref, dgn_ref = rest[-2:]

        @pl.when(pl.program_id(0) == 0)
        def _():
            dgn_ref[...] = jnp.zeros_like(dgn_ref)

        dh = _mm(du_ref[...], w_ref[O_U:O_Q, :]) + _mm(dq_ref[...], w_ref[O_Q:O_KV, :])
        dh = dh + _mm(dkv_ref[...], w_ref[O_KV:O_G, :]) + _mm(dg_ref[...], w_ref[O_G:IN_WIDTH, :])
        x = x_ref[...]
        _, r = _rmsnorm_fwd(x, gn_ref[...])
        dx, dgr = _rmsnorm_bwd(x, r, gn_ref[...], dh)
        dgn_ref[...] += jnp.sum(dgr, axis=0, keepdims=True)
        gx_ref[...] = dx1_ref[...] + dx

    return pl.pallas_call(
        body, name="inproj_bwd", grid=(s // tm,),
        out_shape=(jax.ShapeDtypeStruct((s, D_MODEL), F32), jax.ShapeDtypeStruct((1, D_MODEL), F32)),
        in_specs=[_rows(tm, 1024), _rows(tm, 1024), _rows(tm, 256), _rows(tm, 2048),
                  _resident((IN_WIDTH, D_MODEL)), _rows(tm, 1024), _resident((1, 1024)), _rows(tm, 1024)] + tie_spec,
        out_specs=(_rows(tm, 1024), pl.BlockSpec((1, D_MODEL), lambda i: (0, 0))),
        compiler_params=_params("arbitrary"),
    )(dzu, dzq, dzkv, dzg, win_t, x, attn_norm, dx1, *tie)


def _attention_constants(q_norm, k_norm, sinks):
    inv_freq = np.float32(ROPE_THETA) ** (-np.arange(0, ROPE_DIM, 2, dtype=np.float32) / np.float32(ROPE_DIM))
    lane = np.arange(LANES) % HEAD_DIM
    invf = jnp.asarray(np.where(lane < ROPE_DIM, inv_freq[lane % (ROPE_DIM // 2)], 0.0).reshape(1, LANES), F32)
    bd = jnp.asarray(np.arange(LANES)[:, None] // HEAD_DIM == np.arange(LANES)[None, :] // HEAD_DIM, MM)
    q_norm_t = jnp.tile(q_norm, (1, N_Q_HEADS))
    k_norm_t = jnp.tile(k_norm, (1, N_KV_HEADS))
    sink_rows = jnp.repeat(sinks.reshape(N_KV_HEADS, PAIRS, 2).transpose(0, 2, 1), BLOCK, axis=2)
    sink_rows = sink_rows.reshape(N_KV_HEADS, 2, 1, PAIR_COLS)
    return invf, bd, q_norm_t, k_norm_t, sink_rows


ANY = pl.BlockSpec(memory_space=pl.ANY)


def _position():
    return lax.axis_index("x"), lax.axis_index("y"), lax.axis_index("c")


def _all_gather(shards):
    k = len(shards)

    def body(*refs):
        ins, outs = refs[:k], refs[k:2 * k]
        send_sems, recv_sems, local_sems = refs[2 * k:]
        x, y, c = _position()
        me, sibling = (x, y, c), (x, y, 1 - c)
        chips = [(1 - x, y), (x, 1 - y), (1 - x, 1 - y)]

        def copy(a, kk, block, to, src=None):
            dst = outs[a].at[4 * block[0] + 2 * block[1] + block[2]]
            return pltpu.make_async_remote_copy(
                src_ref=dst if src is None else src, dst_ref=dst, send_sem=send_sems.at[a * 7 + kk],
                recv_sem=recv_sems.at[a * 7 + kk], device_id=to, device_id_type=MESH)

        mine = [pltpu.make_async_copy(ins[a], outs[a].at[4 * x + 2 * y + c], local_sems.at[a]) for a in range(k)]
        for cp in mine:
            cp.start()
        first = []
        for a in range(k):
            first.append(copy(a, 0, me, sibling, src=ins[a]))
            first += [copy(a, 1 + j, me, (*chip, c), src=ins[a]) for j, chip in enumerate(chips)]
        for cp in first:
            cp.start()
        passed = []
        for j, chip in enumerate(chips):
            for a in range(k):
                copy(a, 1 + j, (*chip, c), me).wait_recv()
                cp = copy(a, 4 + j, (*chip, c), sibling)
                cp.start()
                passed.append(cp)
        for a in range(k):
            copy(a, 0, sibling, me).wait_recv()
            for j, chip in enumerate(chips):
                copy(a, 4 + j, (*chip, 1 - c), me).wait_recv()
        for cp in first + passed:
            cp.wait_send()
        for cp in mine:
            cp.wait()

    return pl.pallas_call(
        body, name="all_gather_weights",
        out_shape=tuple(jax.ShapeDtypeStruct((N_DEV,) + s.shape, s.dtype) for s in shards),
        in_specs=[ANY] * k, out_specs=(ANY,) * k,
        scratch_shapes=[pltpu.SemaphoreType.DMA((7 * k,)), pltpu.SemaphoreType.DMA((7 * k,)),
                        pltpu.SemaphoreType.DMA((k,))],
    )(*shards)


HBM = pl.BlockSpec(memory_space=pltpu.HBM)
SEM = pl.BlockSpec(memory_space=pltpu.SEMAPHORE)
EFFECT = pltpu.SideEffectType.DATAFLOW_SIDE_EFFECTING


def _exchange_start(name, bufs, n_sems, copies, after=None):
    k = len(bufs)
    tie, tie_spec = _after(after)
    n_in = k + len(tie)

    def body(*refs):
        for cp in copies(refs[:k], refs[n_in], refs[n_in + 1]):
            cp.start()
        refs[-1][...] = jnp.zeros_like(refs[-1])

    dma = pltpu.SemaphoreType.DMA((n_sems,))
    out = pl.pallas_call(
        body, name=name,
        out_shape=(dma, dma, *[pltpu.HBM(b.shape, b.dtype) for b in bufs], jax.ShapeDtypeStruct((8, LANES), F32)),
        in_specs=[HBM] * k + tie_spec, out_specs=(SEM, SEM, *[HBM] * k, pl.BlockSpec(memory_space=pltpu.VMEM)),
        input_output_aliases={i: 2 + i for i in range(k)},
        compiler_params=pltpu.CompilerParams(has_side_effects=EFFECT),
    )(*[pltpu.with_memory_space_constraint(b, pltpu.HBM) for b in bufs], *tie)
    return out[0], out[1], list(out[2:2 + k]), out[-1]


def _exchange_mid(name, bufs, sems_in, n_sems, waits, copies, after):
    k, ns = len(bufs), len(sems_in)

    def body(*refs):
        ins = refs[:k]
        waits(ins, *refs[k:k + ns])
        for cp in copies(ins, refs[k + ns + 1], refs[k + ns + 2]):
            cp.start()

    dma = pltpu.SemaphoreType.DMA((n_sems,))
    out = pl.pallas_call(
        body, name=name, out_shape=(dma, dma, *[pltpu.HBM(b.shape, b.dtype) for b in bufs]),
        in_specs=[HBM] * k + [SEM] * ns + [ANY], out_specs=(SEM, SEM, *[HBM] * k),
        input_output_aliases={i: 2 + i for i in range(k)},
        compiler_params=pltpu.CompilerParams(has_side_effects=EFFECT),
    )(*bufs, *sems_in, after)
    return out[0], out[1], list(out[2:])


def _exchange_wait(name, bufs, sems, waits, after=None):
    k, ns = len(bufs), len(sems)
    tie, tie_spec = _after(after)

    def body(*refs):
        waits(refs[:k], *refs[k:k + ns])

    out = pl.pallas_call(
        body, name=name, out_shape=tuple(pltpu.HBM(b.shape, b.dtype) for b in bufs),
        in_specs=[HBM] * k + [SEM] * ns + tie_spec, out_specs=(HBM,) * k,
        input_output_aliases={i: i for i in range(k)},
        compiler_params=pltpu.CompilerParams(has_side_effects=EFFECT),
    )(*bufs, *sems, *tie)
    return list(out)


def _gather_copies(k, direct):
    def copies(refs, send_sems, recv_sems):
        x, y, c = _position()
        chips = [(1 - x, y), (x, 1 - y), (1 - x, 1 - y)]
        out = []
        for a in range(k):
            land = refs[k + a]
            if direct:
                mine = land.at[4 * x + 2 * y + c]
                for kk, to in enumerate([(x, y, 1 - c)] + [(*chip, c) for chip in chips]):
                    out.append(pltpu.make_async_remote_copy(
                        src_ref=refs[a], dst_ref=mine, send_sem=send_sems.at[4 * a + kk],
                        recv_sem=recv_sems.at[4 * a + kk], device_id=to, device_id_type=MESH))
            else:
                for j, (px, py) in enumerate(chips):
                    slot = land.at[4 * px + 2 * py + c]
                    out.append(pltpu.make_async_remote_copy(
                        src_ref=slot, dst_ref=slot, send_sem=send_sems.at[3 * a + j], recv_sem=recv_sems.at[3 * a + j],
                        device_id=(x, y, 1 - c), device_id_type=MESH))
        return out
    return copies


def _all_gather_behind(shards, start_after, mid_after):
    k = len(shards)
    me = 4 * lax.axis_index("x") + 2 * lax.axis_index("y") + lax.axis_index("c")
    lands = [lax.dynamic_update_slice(lax.empty((N_DEV,) + s.shape, s.dtype), s[None], (me, 0, 0)) for s in shards]
    direct, passed = _gather_copies(k, True), _gather_copies(k, False)

    send_a, recv_a, bufs, token = _exchange_start("gather_start", list(shards) + lands, 4 * k, direct, start_after)

    def finish():
        def wait_ici(refs, send_sems, recv_sems):
            for i, cp in enumerate(direct(refs, send_sems, recv_sems)):
                if i % 4:
                    cp.wait_recv()

        send_b, recv_b, bufs2 = _exchange_mid("gather_pass", bufs, [send_a, recv_a], 3 * k, wait_ici, passed,
                                              mid_after())

        def wait_all(refs, sa, ra, sb, rb):
            for i, cp in enumerate(direct(refs, sa, ra)):
                cp.wait_send()
                if i % 4 == 0:
                    cp.wait_recv()
            for cp in passed(refs, sb, rb):
                cp.wait()

        return _exchange_wait("gather_wait", bufs2, [send_a, recv_a, send_b, recv_b], wait_all)[k:]

    return token, finish


def _pair_copies(k):
    def copies(refs, send_sems, recv_sems):
        x, y, c = _position()
        return [pltpu.make_async_remote_copy(
            src_ref=refs[a].at[2 * ch + 1 - c], dst_ref=refs[k + a].at[ch], send_sem=send_sems.at[4 * a + ch],
            recv_sem=recv_sems.at[4 * a + ch], device_id=(x, y, 1 - c), device_id_type=MESH)
            for a in range(k) for ch in range(4)]
    return copies


def _chip_copies(k):
    def copies(refs, send_sems, recv_sems):
        x, y, c = _position()
        return [pltpu.make_async_remote_copy(
            src_ref=refs[a].at[2 * px + py], dst_ref=refs[k + a].at[rel], send_sem=send_sems.at[3 * a + rel],
            recv_sem=recv_sems.at[3 * a + rel], device_id=(px, py, c), device_id_type=MESH)
            for a in range(k) for rel, (px, py) in enumerate([(1 - x, y), (x, 1 - y), (1 - x, 1 - y)])]
    return copies


def _symmetric_exchange(name, srcs, n_land, copies_of):
    k = len(srcs)
    lands = [lax.empty((n_land,) + s.shape[1:], s.dtype) for s in srcs]
    copies = copies_of(k)
    send_sems, recv_sems, bufs, token = _exchange_start(name + "_start", list(srcs) + lands, n_land * k, copies)

    def finish(after):
        def wait_all(refs, ss, rs):
            for cp in copies(refs, ss, rs):
                cp.wait()

        done = _exchange_wait(name + "_wait", bufs, [send_sems, recv_sems], wait_all, after)
        return done[:k], done[k:]

    return token, finish


def _pair_add(fulls, recvs, wires):
    k = len(fulls)
    core = lax.axis_index("c").astype(jnp.int32).reshape(1)
    shapes = [f.shape[1:] for f in fulls]

    def body(core_ref, *refs):
        f_refs, r_refs, pw_refs, own_refs = (refs[j * k:(j + 1) * k] for j in range(4))
        x, y, _ = _position()
        mine = pl.program_id(0) == 2 * x + y
        for f_ref, r_ref, pw_ref, own_ref in zip(f_refs, r_refs, pw_refs, own_refs):
            tot = f_ref[0, 0] + r_ref[0]
            pw_ref[0] = tot.astype(pw_ref.dtype)

            @pl.when(mine)
            def _():
                own_ref[...] = tot

    out = pl.pallas_call(
        body, name="grad_pair_add",
        grid_spec=pltpu.PrefetchScalarGridSpec(
            num_scalar_prefetch=1, grid=(4,),
            in_specs=[pl.BlockSpec((1, 1, r, c), lambda i, core_ref: (i, core_ref[0], 0, 0)) for r, c in shapes]
            + [pl.BlockSpec((1, r, c), lambda i, core_ref: (i, 0, 0)) for r, c in shapes],
            out_specs=tuple([pl.BlockSpec((1, r, c), lambda i, core_ref: (i, 0, 0)) for r, c in shapes]
                            + [pl.BlockSpec((r, c), lambda i, core_ref: (0, 0)) for r, c in shapes])),
        out_shape=tuple([jax.ShapeDtypeStruct((4, r, c), wd) for (r, c), wd in zip(shapes, wires)]
                        + [jax.ShapeDtypeStruct((r, c), F32) for r, c in shapes]),
        compiler_params=_params("arbitrary"),
    )(core, *[f.reshape(4, 2, *f.shape[1:]) for f in fulls], *recvs)
    return out[:k], out[k:]


def _adamw_math(w, g, m, v):
    m = ADAM_B1 * m + (1.0 - ADAM_B1) * g
    v = ADAM_B2 * v + (1.0 - ADAM_B2) * (g * g)
    m_hat = m / (1.0 - ADAM_B1 ** ADAM_STEP)
    v_hat = v / (1.0 - ADAM_B2 ** ADAM_STEP)
    delta = -ADAM_LR * (m_hat / (jnp.sqrt(v_hat) + ADAM_EPS) + ADAM_WD * w)
    return delta, m, v


ADAMW_STEPS = 2


def _adamw(owns, recvs, ws, ms, vs):
    k = len(ws)
    tiles = [w.shape[0] // ADAMW_STEPS if w.shape[0] % (16 * ADAMW_STEPS) == 0 else w.shape[0] for w in ws]
    whole = [t == w.shape[0] for t, w in zip(tiles, ws)]

    def body(*refs):
        g_refs, r_refs, w_refs, m_refs, v_refs = (refs[j * k:(j + 1) * k] for j in range(5))
        outs = refs[5 * k:]
        for j in range(k):
            g = g_refs[j][...]
            for i in range(3):
                g = g + r_refs[j][i].astype(F32)
            outs[4 * j][...] = g
            outs[4 * j + 1][...], outs[4 * j + 2][...], outs[4 * j + 3][...] = _adamw_math(
                w_refs[j][...], g, m_refs[j][...], v_refs[j][...])

    def blk(j, lead=()):
        shape = lead + (tiles[j], ws[j].shape[1])
        if whole[j]:
            return pl.BlockSpec(shape, lambda i: (0,) * len(shape))
        return pl.BlockSpec(shape, lambda i: (0,) * len(lead) + (i, 0))

    plain = [blk(j) for j in range(k)]
    out = pl.pallas_call(
        body, name="adamw", grid=(ADAMW_STEPS,),
        out_shape=tuple(jax.ShapeDtypeStruct(w.shape, F32) for w in ws for _ in range(4)),
        in_specs=plain + [blk(j, (3,)) for j in range(k)] + plain * 3,
        out_specs=tuple(plain[j] for j in range(k) for _ in range(4)),
        compiler_params=_params("arbitrary"),
    )(*owns, *recvs, *ws, *ms, *vs)
    return [out[4 * j:4 * j + 4] for j in range(k)]


SMALL = ("attn_norm", "b_gate", "pool_scale", "q_norm", "k_norm", "sinks", "ffn_norm", "conv_b")
SMALL_SIZES = (1024, 2048, 1024, 64, 64, 16, 1024, 5632)
SMALL_OFFSETS = tuple(sum(-(-s // LANES) * LANES for s in SMALL_SIZES[:i]) for i in range(len(SMALL_SIZES) + 1))
SMALL_WIDTH = SMALL_OFFSETS[-1] + LANES


def _pack_small(d, loss=None):
    parts = [jnp.pad(d[n].reshape(1, -1), ((0, 0), (0, -s % LANES))) for n, s in zip(SMALL, SMALL_SIZES)]
    last = jnp.zeros((1, LANES), F32) if loss is None else jnp.pad(loss.reshape(1, 1), ((0, 0), (0, LANES - 1)))
    return jnp.concatenate(parts + [last], axis=1)


def _small_allreduce(gp):
    def body(g_ref, sum_ref, slots_ref, send_sems, recv_sems):
        x, y, c = _position()
        me = 4 * x + 2 * y + c
        slots_ref[me] = g_ref[...]
        cps = []
        for rel in range(1, N_DEV):
            fx, fy, fc = (rel >> 2) & 1, (rel >> 1) & 1, rel & 1
            to = (1 - x if fx else x, 1 - y if fy else y, 1 - c if fc else c)
            cps.append(pltpu.make_async_remote_copy(
                src_ref=g_ref, dst_ref=slots_ref.at[me], send_sem=send_sems.at[rel - 1],
                recv_sem=recv_sems.at[rel - 1], device_id=to, device_id_type=MESH))
        for cp in cps:
            cp.start()
        for cp in cps:
            cp.wait()
        g = slots_ref[0]
        for i in range(1, N_DEV):
            g = g + slots_ref[i]
        sum_ref[...] = g

    vm = pl.BlockSpec(memory_space=pltpu.VMEM)
    return pl.pallas_call(
        body, name="small_allreduce", out_shape=jax.ShapeDtypeStruct((1, SMALL_WIDTH), F32),
        in_specs=[vm], out_specs=vm,
        scratch_shapes=[pltpu.VMEM((N_DEV, 1, SMALL_WIDTH), F32), pltpu.SemaphoreType.DMA((N_DEV - 1,)),
                        pltpu.SemaphoreType.DMA((N_DEV - 1,))],
    )(gp)


def _small_adamw(gsum, ws, ms, vs):
    n = len(SMALL)

    def body(g_ref, *rest):
        w_refs, m_refs, v_refs, outs = rest[:n], rest[n:2 * n], rest[2 * n:3 * n], rest[3 * n:]
        for j, size in enumerate(SMALL_SIZES):
            g = g_ref[:, SMALL_OFFSETS[j]:SMALL_OFFSETS[j] + size]
            results = (g,) + _adamw_math(w_refs[j][...], g, m_refs[j][...], v_refs[j][...])
            for kind, val in enumerate(results):
                outs[kind * n + j][...] = val
        outs[4 * n][...] = g_ref[:, SMALL_OFFSETS[-1]:SMALL_WIDTH]

    vm = pl.BlockSpec(memory_space=pltpu.VMEM)
    shapes = tuple(jax.ShapeDtypeStruct((1, s), F32) for s in SMALL_SIZES) * 4
    return pl.pallas_call(
        body, name="small_adamw", out_shape=shapes + (jax.ShapeDtypeStruct((1, LANES), F32),),
        in_specs=[vm] * (1 + 3 * n), out_specs=(vm,) * (4 * n + 1),
    )(gsum, *ws, *ms, *vs)


WEIGHTS = ("attn_norm", "w_in", "b_gate", "w_pool", "pool_scale", "q_norm", "k_norm", "sinks", "w_out", "ffn_norm",
           "w_up", "conv_w", "conv_b", "w_down")


def kernel(x, positions, attn_norm, w_in, b_gate, w_pool, pool_scale, q_norm, k_norm, sinks, w_out, ffn_norm, w_up, conv_w, conv_b, w_down, loss_target, m_attn_norm, m_w_in, m_b_gate, m_w_pool, m_pool_scale, m_q_norm, m_k_norm, m_sinks, m_w_out, m_ffn_norm, m_w_up, m_conv_w, m_conv_b, m_w_down, v_attn_norm, v_w_in, v_b_gate, v_w_pool, v_pool_scale, v_q_norm, v_k_norm, v_sinks, v_w_out, v_ffn_norm, v_w_up, v_conv_w, v_conv_b, v_w_down):
    w = dict(attn_norm=attn_norm, w_in=w_in, b_gate=b_gate, w_pool=w_pool, pool_scale=pool_scale, q_norm=q_norm,
             k_norm=k_norm, sinks=sinks, w_out=w_out, ffn_norm=ffn_norm, w_up=w_up, conv_w=conv_w, conv_b=conv_b,
             w_down=w_down)
    m = dict(attn_norm=m_attn_norm, w_in=m_w_in, b_gate=m_b_gate, w_pool=m_w_pool, pool_scale=m_pool_scale,
             q_norm=m_q_norm, k_norm=m_k_norm, sinks=m_sinks, w_out=m_w_out, ffn_norm=m_ffn_norm, w_up=m_w_up,
             conv_w=m_conv_w, conv_b=m_conv_b, w_down=m_w_down)
    v = dict(attn_norm=v_attn_norm, w_in=v_w_in, b_gate=v_b_gate, w_pool=v_w_pool, pool_scale=v_pool_scale,
             q_norm=v_q_norm, k_norm=v_k_norm, sinks=v_sinks, w_out=v_w_out, ffn_norm=v_ffn_norm, w_up=v_w_up,
             conv_w=v_conv_w, conv_b=v_conv_b, w_down=v_w_down)
    seq = x.shape[1]
    tm = 256
    tw = min(seq, 512)
    tk, tk_ff = min(seq, 1024), min(seq, 2048)
    xs, target, pos_col = x[0], loss_target[0], positions.reshape(seq, 1)
    invf, bd, q_norm_t, k_norm_t, sink_rows = _attention_constants(q_norm, k_norm, sinks)
    out, done = {}, {}
    nat = {"w_in": (D_MODEL, 544), "w_pool": (128, POOL_GROUP), "w_out": (128, D_MODEL), "w_up": (D_MODEL, 704),
           "conv_w": (3, 704), "w_down": (352, D_MODEL)}

    def update(names, owns, recvs):
        flip = [name in ("w_in", "w_up") for name in names]
        shards = [[t[name].reshape(nat[name]).T if f else t[name].reshape(nat[name]) for name, f in zip(names, flip)]
                  for t in (w, m, v)]
        for name, f, res in zip(names, flip, _adamw(owns, recvs, *shards)):
            done[name] = res[1]
            out[name] = [(t.T if f else t).reshape(w[name].shape) for t in res]

    (g_win,) = _all_gather([w_in[0].T.astype(MM)])
    win_t = g_win.reshape(IN_WIDTH, D_MODEL)
    fwd = {}
    token, gather_rest = _all_gather_behind(
        [w_pool[0].astype(MM).reshape(128, POOL_GROUP), w_out[0].astype(MM), w_up[0].T.astype(MM), conv_w[0],
         w_down[0].astype(MM)], g_win, lambda: fwd["b"])

    tabs = _rope_tables(pos_col, invf)
    h, u, q, kv, g = _inproj_fwd(xs, attn_norm + token[0:1, 0:1], win_t, b_gate, tw)
    b, probs, sink_probs = _attn_fwd(q, kv, tabs, q_norm_t, k_norm_t, sink_rows, bd)
    fwd["b"] = b
    g_wpool, g_wout, g_wup, g_convw, g_wdown = gather_rest()
    wpool = g_wpool.reshape(N_DEV, 4, 32, POOL_GROUP).transpose(1, 0, 2, 3).reshape(4, POOL_GROUP, POOL_GROUP)
    wout = g_wout.reshape(D_MODEL, D_MODEL)
    wup_t = g_wup.reshape(2 * D_FF, D_MODEL)
    convw = g_convw.transpose(1, 0, 2).reshape(3, 2 * D_FF)
    wdown = g_wdown.reshape(D_FF, D_MODEL)
    a = _pool_fwd(u, wpool, pool_scale, tw)
    x1, mix = _mix_out_fwd(xs, g, a, b, wout, tw)
    h2, uff, ucv, dy, dyb, lossp = _ffn_fwd(x1, ffn_norm, wup_t, convw, conv_b, wdown, target, tm)

    du, act, d_conv_w, d_conv_b = _ffn_bwd_a(dyb, uff, ucv, convw, wdown, tm)
    d_wdown = _matmul_tn(act, dyb, FF_CHUNK, tk_ff, "dw_down")
    dx1, dx1b, d_ffn_norm = _ffn_bwd_b(du, wup_t, x1, ffn_norm, dy, tm)
    d_wup_t = _matmul_tn(du, h2, FF_CHUNK, tk_ff, "dw_up")
    late = ("w_down", "w_up", "conv_w")
    late_wire = (WIRE, WIRE, F32)
    late_full = [d_wdown.reshape(N_DEV, 352, D_MODEL), d_wup_t.reshape(N_DEV, 704, D_MODEL),
                 d_conv_w.reshape(3, N_DEV, 704).transpose(1, 0, 2)]
    token, late_pair = _symmetric_exchange("late_pair", late_full, 4, _pair_copies)
    da, db, dzg, d_b_gate, d_wout, d_win_t = _mix_bwd(dx1b, wout, g, a, b, mix, h, tm, after=token)
    late_pw, late_own = _pair_add(*late_pair(dzg), late_wire)
    token, late_chip = _symmetric_exchange("late_chip", list(late_pw), 3, _chip_copies)
    dzu, d_wpool, d_pool_scale, d_win_t = _pool_bwd(u, da, wpool, pool_scale, h, d_win_t, tw, after=token)
    dzq, dzkv, d_q_norm, d_k_norm, d_sinks = _attn_bwd(q, kv, db, tabs, q_norm_t, k_norm_t, probs, sink_probs, bd,
                                                       after=token)
    d_win_t = _matmul_tn(dzq, h, 1024, tk, "dw_in_q", into=d_win_t, row0=O_Q)
    d_win_t = _matmul_tn(dzkv, h, 256, tk, "dw_in_kv", into=d_win_t, row0=O_KV)
    early = ("w_in", "w_pool", "w_out")
    early_full = [d_win_t.reshape(N_DEV, 544, D_MODEL),
                  d_wpool.reshape(4, N_DEV, 32, POOL_GROUP).transpose(1, 0, 2, 3).reshape(N_DEV, 128, POOL_GROUP),
                  d_wout.reshape(N_DEV, 128, D_MODEL)]
    token, early_pair = _symmetric_exchange("early_pair", early_full, 4, _pair_copies)
    update(late, late_own, late_chip(token)[1])
    early_pw, early_own = _pair_add(*early_pair([done[n] for n in late]), (WIRE,) * 3)
    token, early_chip = _symmetric_exchange("early_chip", list(early_pw), 3, _chip_copies)
    grad_x, d_attn_norm = _inproj_bwd(dzu, dzq, dzkv, dzg, win_t, xs, attn_norm, dx1, tw, after=token)
    gr = dict(attn_norm=d_attn_norm, b_gate=d_b_gate, pool_scale=d_pool_scale, q_norm=d_q_norm, k_norm=d_k_norm,
              sinks=d_sinks[:, 0:N_Q_HEADS], ffn_norm=d_ffn_norm, conv_b=d_conv_b)
    small = _small_adamw(_small_allreduce(_pack_small(gr, lossp[0, 0])), *[[t[n] for n in SMALL] for t in (w, m, v)])
    loss = small[-1][0, 0]
    for j, name in enumerate(SMALL):
        out[name] = [small[kind * len(SMALL) + j] for kind in range(4)]
    update(early, early_own, early_chip(small[0])[1])

    return (loss, grad_x[None], *[out[n][0] for n in WEIGHTS], *[out[n][1] for n in WEIGHTS],
            *[out[n][2] for n in WEIGHTS], *[out[n][3] for n in WEIGHTS])
```
